```python
import math
import jax, jax.numpy as jnp
from jax import lax
import numpy as np

D_MODEL = 2048
BATCH = 8
SEQ = 4096
DEPTH = 4

N_MIXERS = 3
N_SSD_LAYERS = (DEPTH + 2) // 3
N_MLA_LAYERS = (DEPTH + 1) // 3
N_GDN_LAYERS = DEPTH // 3
DEEPNORM_ALPHA = (2.0 * DEPTH) ** 0.25
DEEPNORM_BETA = (8.0 * DEPTH) ** -0.25
LN_EPS = 1e-5
RMS_EPS = 1e-6

SSD_D_INNER = 2 * D_MODEL
SSD_HEAD_DIM = 64
SSD_N_HEADS = SSD_D_INNER // SSD_HEAD_DIM
SSD_N_GROUPS = 8
SSD_D_STATE = 128
SSD_CONV = 4
SSD_CHUNK = 128
SSD_CONV_DIM = SSD_D_INNER + 2 * SSD_N_GROUPS * SSD_D_STATE
SSD_PROJ = SSD_D_INNER + SSD_CONV_DIM + SSD_N_HEADS

MLA_N_HEADS = D_MODEL // 128
MLA_Q_RANK = 768
MLA_KV_RANK = 512
MLA_NOPE = 128
MLA_ROPE = 64
MLA_V = 128
MLA_GATE = MLA_N_HEADS * MLA_V
MLA_PROJ = MLA_Q_RANK + MLA_KV_RANK + MLA_ROPE + MLA_GATE
MLA_Q_BLOCK = 128
ROPE_THETA = 10000.0

GDN_N_QK_HEADS = 16
GDN_N_V_HEADS = 32
GDN_DK = 128
GDN_DV = 128
GDN_KEY_DIM = GDN_N_QK_HEADS * GDN_DK
GDN_VAL_DIM = GDN_N_V_HEADS * GDN_DV
GDN_CONV = 4
GDN_CHUNK = 64
GDN_CONV_DIM = 2 * GDN_KEY_DIM + GDN_VAL_DIM
GDN_PROJ = GDN_CONV_DIM + GDN_VAL_DIM + 2 * GDN_N_V_HEADS

kernel_name = "hybrid_ssd_mla_gdn_deepnorm"


def _layernorm(x, g, b):
    xf = x.astype(jnp.float32)
    mu = jnp.mean(xf, -1, keepdims=True)
    var = jnp.mean(jnp.square(xf - mu), -1, keepdims=True)
    return ((xf - mu) * lax.rsqrt(var + LN_EPS) * g + b).astype(x.dtype)


def _rmsnorm(x, w):
    xf = x.astype(jnp.float32)
    return (xf * lax.rsqrt(jnp.mean(xf * xf, -1, keepdims=True) + RMS_EPS) * w).astype(x.dtype)


def _l2norm(x):
    xf = x.astype(jnp.float32)
    return xf * lax.rsqrt(jnp.sum(xf * xf, -1, keepdims=True) + RMS_EPS)


def _causal_dwconv(x, w):
    K, C = w.shape
    return lax.conv_general_dilated(
        x, w[:, None, :].astype(x.dtype), window_strides=(1,), padding=[(K - 1, 0)],
        dimension_numbers=("NWC", "WIO", "NWC"), feature_group_count=C)


def _rope(x, cos, sin):
    x1, x2 = jnp.split(x, 2, axis=-1)
    return jnp.concatenate([x1 * cos - x2 * sin, x1 * sin + x2 * cos], axis=-1)


def _ssd_mixer(u, in_w, conv_w, conv_b, dt_bias, a_log, d_skip, norm_w, out_w):
    f32 = jnp.float32
    Bsz, S, _ = u.shape
    G, N, H, P, L = SSD_N_GROUPS, SSD_D_STATE, SSD_N_HEADS, SSD_HEAD_DIM, SSD_CHUNK
    R = H // G
    nc = S // L
    proj = u @ in_w
    z, xbc, dt = jnp.split(proj, [SSD_D_INNER, SSD_D_INNER + SSD_CONV_DIM], axis=-1)
    xbc = jax.nn.silu(_causal_dwconv(xbc, conv_w) + conv_b)
    xs, Bm, Cm = jnp.split(xbc, [SSD_D_INNER, SSD_D_INNER + G * N], axis=-1)
    dt = jax.nn.softplus(dt.astype(f32) + dt_bias.astype(f32))
    A = -jnp.exp(a_log.astype(f32)).reshape(G, R)
    x_c = xs.astype(f32).reshape(Bsz, nc, L, G, R, P)
    B_c = Bm.astype(f32).reshape(Bsz, nc, L, G, N)
    C_c = Cm.astype(f32).reshape(Bsz, nc, L, G, N)
    dt_c = dt.reshape(Bsz, nc, L, G, R)
    xdt = x_c * dt_c[..., None]
    a_cs = jnp.moveaxis(jnp.cumsum(dt_c * A, axis=2), 2, -1)
    causal = jnp.tril(jnp.ones((L, L), dtype=bool))
    decay = jnp.exp(jnp.where(causal, a_cs[..., :, None] - a_cs[..., None, :], -jnp.inf))
    cb = jnp.einsum("bclgn,bcsgn->bcgls", C_c, B_c)
    y_diag = jnp.einsum("bcgrls,bcsgrp->bclgrp", cb[:, :, :, None] * decay, xdt)
    decay_states = jnp.moveaxis(jnp.exp(a_cs[..., -1:] - a_cs), -1, 2)
    states = jnp.einsum("bcsgn,bcsgrp->bcgrpn", B_c, xdt * decay_states[..., None])
    chunk_decay = jnp.exp(a_cs[..., -1])

    def step(h, inp):
        st, dec = inp
        return h * dec[..., None, None] + st, h

    h0 = jnp.zeros((Bsz, G, R, P, N), f32)
    _, prev = lax.scan(step, h0, (jnp.moveaxis(states, 1, 0), jnp.moveaxis(chunk_decay, 1, 0)))
    prev = jnp.moveaxis(prev, 0, 1)
    decay_out = jnp.moveaxis(jnp.exp(a_cs), -1, 2)
    y_off = jnp.einsum("bclgn,bcgrpn->bclgrp", C_c, prev) * decay_out[..., None]
    y = y_diag + y_off + x_c * d_skip.astype(f32).reshape(G, R)[..., None]
    gs = SSD_D_INNER // G
    yg = y.reshape(Bsz, S, G, gs) * jax.nn.silu(z.astype(f32).reshape(Bsz, S, G, gs))
    yg = yg * lax.rsqrt(jnp.mean(yg * yg, -1, keepdims=True) + RMS_EPS) * norm_w.astype(f32).reshape(G, gs)
    return yg.reshape(Bsz, S, SSD_D_INNER).astype(u.dtype) @ out_w


def _mla_mixer(u, positions, in_w, q_norm_w, q_up_w, kv_norm_w, kv_up_w, out_w):
    f32 = jnp.float32
    Bsz, S, _ = u.shape
    H, QB = MLA_N_HEADS, MLA_Q_BLOCK
    nb = S // QB
    proj = u @ in_w
    q_c, kv_c, k_rope, z = jnp.split(
        proj, [MLA_Q_RANK, MLA_Q_RANK + MLA_KV_RANK, MLA_Q_RANK + MLA_KV_RANK + MLA_ROPE], axis=-1)
    q = (_rmsnorm(q_c, q_norm_w) @ q_up_w).astype(f32).reshape(Bsz, S, H, MLA_NOPE + MLA_ROPE)
    kv = (_rmsnorm(kv_c, kv_norm_w) @ kv_up_w).astype(f32).reshape(Bsz, S, H, MLA_NOPE + MLA_V)
    q_nope, q_rope = q[..., :MLA_NOPE], q[..., MLA_NOPE:]
    k_nope, v = kv[..., :MLA_NOPE], kv[..., MLA_NOPE:]
    inv_freq = ROPE_THETA ** (-jnp.arange(0, MLA_ROPE, 2, dtype=f32) / MLA_ROPE)
    ang = positions.astype(f32)[..., None] * inv_freq
    cos, sin = jnp.cos(ang), jnp.sin(ang)
    q_rope = _rope(q_rope, cos[:, :, None], sin[:, :, None])
    k_rope = _rope(k_rope.astype(f32), cos, sin)
    scale = (MLA_NOPE + MLA_ROPE) ** -0.5
    qn_b = q_nope.reshape(Bsz, nb, QB, H, MLA_NOPE).transpose(1, 0, 3, 2, 4)
    qr_b = q_rope.reshape(Bsz, nb, QB, H, MLA_ROPE).transpose(1, 0, 3, 2, 4)
    kpos = jnp.arange(S)

    def attend(args):
        qn, qr, blk = args
        s = (jnp.einsum("bhqd,bkhd->bhqk", qn, k_nope)
             + jnp.einsum("bhqr,bkr->bhqk", qr, k_rope)) * scale
        qpos = blk * QB + jnp.arange(QB)
        s = jnp.where(kpos[None, :] <= qpos[:, None], s, -jnp.inf)
        p = jax.nn.softmax(s, axis=-1)
        return jnp.einsum("bhqk,bkhd->bqhd", p, v)

    o = lax.map(attend, (qn_b, qr_b, jnp.arange(nb)))
    o = o.transpose(1, 0, 2, 3, 4).reshape(Bsz, S, H * MLA_V)
    o = o * jax.nn.silu(z.astype(f32))
    return o.astype(u.dtype) @ out_w


def _chunk_gated_delta(q, k, v, g, beta):
    Bsz, S, H, DK = q.shape
    DV = v.shape[-1]
    L = GDN_CHUNK
    nc = S // L

    def chunks(t):
        return jnp.swapaxes(t.reshape(Bsz, nc, L, H, *t.shape[3:]), 2, 3)

    q, k, v, g, beta = chunks(q), chunks(k), chunks(v), chunks(g), chunks(beta)
    g_cs = jnp.cumsum(g, axis=-1)
    incl = jnp.tril(jnp.ones((L, L), dtype=bool))
    strict = jnp.tril(jnp.ones((L, L), dtype=bool), -1)
    decay = jnp.exp(jnp.where(incl, g_cs[..., :, None] - g_cs[..., None, :], -jnp.inf))
    k_beta = k * beta[..., None]
    v_beta = v * beta[..., None]
    a_mat = jnp.where(strict, jnp.einsum("bchld,bchsd->bchls", k_beta, k) * decay, 0.0)
    eye = jnp.eye(L, dtype=q.dtype)
    rhs = jnp.concatenate([v_beta, k_beta * jnp.exp(g_cs)[..., None]], axis=-1)
    sol = lax.linalg.triangular_solve(eye + a_mat, rhs, left_side=True, lower=True, unit_diagonal=True)
    u_val, w_dec = sol[..., :DV], sol[..., DV:]
    qk = jnp.where(incl, jnp.einsum("bchld,bchsd->bchls", q, k) * decay, 0.0)
    q_dec = q * jnp.exp(g_cs)[..., None]
    g_last = g_cs[..., -1]
    k_dec = k * jnp.exp(g_last[..., None] - g_cs)[..., None]

    def step(state, inp):
        qk_c, q_dec_c, u_c, w_c, k_dec_c, gl_c = inp
        v_new = u_c - jnp.einsum("bhld,bhdv->bhlv", w_c, state)
        o_c = jnp.einsum("bhld,bhdv->bhlv", q_dec_c, state) + jnp.einsum("bhls,bhsv->bhlv", qk_c, v_new)
        state = state * jnp.exp(gl_c)[..., None, None] + jnp.einsum("bhld,bhlv->bhdv", k_dec_c, v_new)
        return state, o_c

    xs = tuple(jnp.moveaxis(t, 1, 0) for t in (qk, q_dec, u_val, w_dec, k_dec, g_last))
    _, o = lax.scan(step, jnp.zeros((Bsz, H, DK, DV), q.dtype), xs)
    o = jnp.swapaxes(jnp.moveaxis(o, 0, 1), 2, 3)
    return o.reshape(Bsz, S, H, DV)


def _gdn_mixer(u, in_w, conv_w, a_log, dt_bias, norm_w, out_w):
    f32 = jnp.float32
    Bsz, S, _ = u.shape
    HK, HV, DK, DV = GDN_N_QK_HEADS, GDN_N_V_HEADS, GDN_DK, GDN_DV
    proj = u @ in_w
    qkv, z, b, a = jnp.split(
        proj, [GDN_CONV_DIM, GDN_CONV_DIM + GDN_VAL_DIM, GDN_CONV_DIM + GDN_VAL_DIM + HV], axis=-1)
    qkv = jax.nn.silu(_causal_dwconv(qkv, conv_w))
    q, k, v = jnp.split(qkv, [GDN_KEY_DIM, 2 * GDN_KEY_DIM], axis=-1)
    rep = HV // HK
    q = jnp.repeat(_l2norm(q.reshape(Bsz, S, HK, DK)), rep, axis=2) * (DK ** -0.5)
    k = jnp.repeat(_l2norm(k.reshape(Bsz, S, HK, DK)), rep, axis=2)
    v = v.astype(f32).reshape(Bsz, S, HV, DV)
    beta = jax.nn.sigmoid(b.astype(f32))
    g = -jnp.exp(a_log.astype(f32)) * jax.nn.softplus(a.astype(f32) + dt_bias.astype(f32))
    o = _chunk_gated_delta(q, k, v, g, beta)
    o = _rmsnorm(o, norm_w) * jax.nn.silu(z.astype(f32).reshape(Bsz, S, HV, DV))
    return o.reshape(Bsz, S, GDN_VAL_DIM).astype(u.dtype) @ out_w


def _dt_bias(key, shape):
    dt = jnp.exp(jax.random.uniform(key, shape, jnp.float32, math.log(1e-3), math.log(1e-1)))
    return dt + jnp.log(-jnp.expm1(-dt))


def _fwd_setup_inputs(seed: int = 0) -> dict:
    key = jax.random.key(seed)
    ks = jax.random.split(key, 32)
    f32 = jnp.float32

    def nrm(k, shape, scale):
        return jax.random.normal(k, shape, f32) * scale

    def gain(k, shape):
        return 1.0 + 0.02 * jax.random.normal(k, shape, f32)

    nA, nB, nC = N_SSD_LAYERS, N_MLA_LAYERS, N_GDN_LAYERS
    x = jax.random.normal(ks[0], (BATCH, SEQ, D_MODEL), f32)
    offset = jax.random.randint(ks[1], (BATCH, 1), 0, 1024, dtype=jnp.int32)
    positions = offset + jnp.arange(SEQ, dtype=jnp.int32)[None, :]
    return {
        "x": x,
        "positions": positions,
        "ssd_in_w": nrm(ks[2], (nA, D_MODEL, SSD_PROJ), D_MODEL ** -0.5),
        "ssd_conv_w": nrm(ks[3], (nA, SSD_CONV, SSD_CONV_DIM), SSD_CONV ** -0.5),
        "ssd_conv_b": nrm(ks[4], (nA, SSD_CONV_DIM), 0.02),
        "ssd_dt_bias": _dt_bias(ks[5], (nA, SSD_N_HEADS)),
        "ssd_a_log": jnp.log(jax.random.uniform(ks[6], (nA, SSD_N_HEADS), f32, 1.0, 16.0)),
        "ssd_d": gain(ks[7], (nA, SSD_N_HEADS)),
        "ssd_norm_w": gain(ks[8], (nA, SSD_D_INNER)),
        "ssd_out_w": nrm(ks[9], (nA, SSD_D_INNER, D_MODEL), SSD_D_INNER ** -0.5 * DEEPNORM_BETA),
        "mla_in_w": nrm(ks[10], (nB, D_MODEL, MLA_PROJ), D_MODEL ** -0.5),
        "mla_q_norm_w": gain(ks[11], (nB, MLA_Q_RANK)),
        "mla_q_up_w": nrm(ks[12], (nB, MLA_Q_RANK, MLA_N_HEADS * (MLA_NOPE + MLA_ROPE)), MLA_Q_RANK ** -0.5),
        "mla_kv_norm_w": gain(ks[13], (nB, MLA_KV_RANK)),
        "mla_kv_up_w": nrm(ks[14], (nB, MLA_KV_RANK, MLA_N_HEADS * (MLA_NOPE + MLA_V)), MLA_KV_RANK ** -0.5),
        "mla_out_w": nrm(ks[15], (nB, MLA_N_HEADS * MLA_V, D_MODEL), (MLA_N_HEADS * MLA_V) ** -0.5 * DEEPNORM_BETA),
        "gdn_in_w": nrm(ks[16], (nC, D_MODEL, GDN_PROJ), D_MODEL ** -0.5),
        "gdn_conv_w": nrm(ks[17], (nC, GDN_CONV, GDN_CONV_DIM), GDN_CONV ** -0.5),
        "gdn_a_log": jnp.log(jax.random.uniform(ks[18], (nC, GDN_N_V_HEADS), f32, 1.0, 16.0)),
        "gdn_dt_bias": _dt_bias(ks[19], (nC, GDN_N_V_HEADS)),
        "gdn_norm_w": gain(ks[20], (nC, GDN_DV)),
        "gdn_out_w": nrm(ks[21], (nC, GDN_VAL_DIM, D_MODEL), GDN_VAL_DIM ** -0.5 * DEEPNORM_BETA),
        "ln_g": gain(ks[22], (DEPTH, D_MODEL)),
        "ln_b": nrm(ks[23], (DEPTH, D_MODEL), 0.02),
    }


def _fwd_reference(x, positions, ssd_in_w, ssd_conv_w, ssd_conv_b, ssd_dt_bias, ssd_a_log, ssd_d,
              ssd_norm_w, ssd_out_w, mla_in_w, mla_q_norm_w, mla_q_up_w, mla_kv_norm_w, mla_kv_up_w,
              mla_out_w, gdn_in_w, gdn_conv_w, gdn_a_log, gdn_dt_bias, gdn_norm_w, gdn_out_w, ln_g, ln_b):
    h = x
    for i in range(DEPTH):
        kind, j = i % N_MIXERS, i // N_MIXERS
        if kind == 0:
            y = _ssd_mixer(h, ssd_in_w[j], ssd_conv_w[j], ssd_conv_b[j], ssd_dt_bias[j], ssd_a_log[j],
                           ssd_d[j], ssd_norm_w[j], ssd_out_w[j])
        elif kind == 1:
            y = _mla_mixer(h, positions, mla_in_w[j], mla_q_norm_w[j], mla_q_up_w[j], mla_kv_norm_w[j],
                           mla_kv_up_w[j], mla_out_w[j])
        else:
            y = _gdn_mixer(h, gdn_in_w[j], gdn_conv_w[j], gdn_a_log[j], gdn_dt_bias[j], gdn_norm_w[j],
                           gdn_out_w[j])
        h = _layernorm(DEEPNORM_ALPHA * h + y.astype(h.dtype), ln_g[i], ln_b[i])
    return h


import jax as _jax
import jax.numpy as _jnp

TWIN_FORMAT = 'train_step'
FWD_PARAMS = ['x', 'positions', 'ssd_in_w', 'ssd_conv_w', 'ssd_conv_b', 'ssd_dt_bias', 'ssd_a_log', 'ssd_d', 'ssd_norm_w', 'ssd_out_w', 'mla_in_w', 'mla_q_norm_w', 'mla_q_up_w', 'mla_kv_norm_w', 'mla_kv_up_w', 'mla_out_w', 'gdn_in_w', 'gdn_conv_w', 'gdn_a_log', 'gdn_dt_bias', 'gdn_norm_w', 'gdn_out_w', 'ln_g', 'ln_b']
TWIN_WEIGHTS = ['ssd_in_w', 'ssd_conv_w', 'ssd_conv_b', 'ssd_dt_bias', 'ssd_a_log', 'ssd_d', 'ssd_norm_w', 'ssd_out_w', 'mla_in_w', 'mla_q_norm_w', 'mla_q_up_w', 'mla_kv_norm_w', 'mla_kv_up_w', 'mla_out_w', 'gdn_in_w', 'gdn_conv_w', 'gdn_a_log', 'gdn_dt_bias', 'gdn_norm_w', 'gdn_out_w', 'ln_g', 'ln_b']
TWIN_DIFF_INPUT = 'x'
TWIN_INPUTS = ['x', 'positions', 'ssd_in_w', 'ssd_conv_w', 'ssd_conv_b', 'ssd_dt_bias', 'ssd_a_log', 'ssd_d', 'ssd_norm_w', 'ssd_out_w', 'mla_in_w', 'mla_q_norm_w', 'mla_q_up_w', 'mla_kv_norm_w', 'mla_kv_up_w', 'mla_out_w', 'gdn_in_w', 'gdn_conv_w', 'gdn_a_log', 'gdn_dt_bias', 'gdn_norm_w', 'gdn_out_w', 'ln_g', 'ln_b', 'loss_target', 'm_ssd_in_w', 'm_ssd_conv_w', 'm_ssd_conv_b', 'm_ssd_dt_bias', 'm_ssd_a_log', 'm_ssd_d', 'm_ssd_norm_w', 'm_ssd_out_w', 'm_mla_in_w', 'm_mla_q_norm_w', 'm_mla_q_up_w', 'm_mla_kv_norm_w', 'm_mla_kv_up_w', 'm_mla_out_w', 'm_gdn_in_w', 'm_gdn_conv_w', 'm_gdn_a_log', 'm_gdn_dt_bias', 'm_gdn_norm_w', 'm_gdn_out_w', 'm_ln_g', 'm_ln_b', 'v_ssd_in_w', 'v_ssd_conv_w', 'v_ssd_conv_b', 'v_ssd_dt_bias', 'v_ssd_a_log', 'v_ssd_d', 'v_ssd_norm_w', 'v_ssd_out_w', 'v_mla_in_w', 'v_mla_q_norm_w', 'v_mla_q_up_w', 'v_mla_kv_norm_w', 'v_mla_kv_up_w', 'v_mla_out_w', 'v_gdn_in_w', 'v_gdn_conv_w', 'v_gdn_a_log', 'v_gdn_dt_bias', 'v_gdn_norm_w', 'v_gdn_out_w', 'v_ln_g', 'v_ln_b']
TWIN_OUTPUTS = ['loss', 'grad_x', 'grad_ssd_in_w', 'grad_ssd_conv_w', 'grad_ssd_conv_b', 'grad_ssd_dt_bias', 'grad_ssd_a_log', 'grad_ssd_d', 'grad_ssd_norm_w', 'grad_ssd_out_w', 'grad_mla_in_w', 'grad_mla_q_norm_w', 'grad_mla_q_up_w', 'grad_mla_kv_norm_w', 'grad_mla_kv_up_w', 'grad_mla_out_w', 'grad_gdn_in_w', 'grad_gdn_conv_w', 'grad_gdn_a_log', 'grad_gdn_dt_bias', 'grad_gdn_norm_w', 'grad_gdn_out_w', 'grad_ln_g', 'grad_ln_b', 'delta_ssd_in_w', 'delta_ssd_conv_w', 'delta_ssd_conv_b', 'delta_ssd_dt_bias', 'delta_ssd_a_log', 'delta_ssd_d', 'delta_ssd_norm_w', 'delta_ssd_out_w', 'delta_mla_in_w', 'delta_mla_q_norm_w', 'delta_mla_q_up_w', 'delta_mla_kv_norm_w', 'delta_mla_kv_up_w', 'delta_mla_out_w', 'delta_gdn_in_w', 'delta_gdn_conv_w', 'delta_gdn_a_log', 'delta_gdn_dt_bias', 'delta_gdn_norm_w', 'delta_gdn_out_w', 'delta_ln_g', 'delta_ln_b', 'new_m_ssd_in_w', 'new_m_ssd_conv_w', 'new_m_ssd_conv_b', 'new_m_ssd_dt_bias', 'new_m_ssd_a_log', 'new_m_ssd_d', 'new_m_ssd_norm_w', 'new_m_ssd_out_w', 'new_m_mla_in_w', 'new_m_mla_q_norm_w', 'new_m_mla_q_up_w', 'new_m_mla_kv_norm_w', 'new_m_mla_kv_up_w', 'new_m_mla_out_w', 'new_m_gdn_in_w', 'new_m_gdn_conv_w', 'new_m_gdn_a_log', 'new_m_gdn_dt_bias', 'new_m_gdn_norm_w', 'new_m_gdn_out_w', 'new_m_ln_g', 'new_m_ln_b', 'new_v_ssd_in_w', 'new_v_ssd_conv_w', 'new_v_ssd_conv_b', 'new_v_ssd_dt_bias', 'new_v_ssd_a_log', 'new_v_ssd_d', 'new_v_ssd_norm_w', 'new_v_ssd_out_w', 'new_v_mla_in_w', 'new_v_mla_q_norm_w', 'new_v_mla_q_up_w', 'new_v_mla_kv_norm_w', 'new_v_mla_kv_up_w', 'new_v_mla_out_w', 'new_v_gdn_in_w', 'new_v_gdn_conv_w', 'new_v_gdn_a_log', 'new_v_gdn_dt_bias', 'new_v_gdn_norm_w', 'new_v_gdn_out_w', 'new_v_ln_g', 'new_v_ln_b']
TWIN_LEAF_KINDS = {'loss': 'loss', 'grad_x': 'grad_x', 'grad_ssd_in_w': 'grad_w', 'grad_ssd_conv_w': 'grad_w', 'grad_ssd_conv_b': 'grad_w', 'grad_ssd_dt_bias': 'grad_w', 'grad_ssd_a_log': 'grad_w', 'grad_ssd_d': 'grad_w', 'grad_ssd_norm_w': 'grad_w', 'grad_ssd_out_w': 'grad_w', 'grad_mla_in_w': 'grad_w', 'grad_mla_q_norm_w': 'grad_w', 'grad_mla_q_up_w': 'grad_w', 'grad_mla_kv_norm_w': 'grad_w', 'grad_mla_kv_up_w': 'grad_w', 'grad_mla_out_w': 'grad_w', 'grad_gdn_in_w': 'grad_w', 'grad_gdn_conv_w': 'grad_w', 'grad_gdn_a_log': 'grad_w', 'grad_gdn_dt_bias': 'grad_w', 'grad_gdn_norm_w': 'grad_w', 'grad_gdn_out_w': 'grad_w', 'grad_ln_g': 'grad_w', 'grad_ln_b': 'grad_w', 'delta_ssd_in_w': 'delta_w', 'delta_ssd_conv_w': 'delta_w', 'delta_ssd_conv_b': 'delta_w', 'delta_ssd_dt_bias': 'delta_w', 'delta_ssd_a_log': 'delta_w', 'delta_ssd_d': 'delta_w', 'delta_ssd_norm_w': 'delta_w', 'delta_ssd_out_w': 'delta_w', 'delta_mla_in_w': 'delta_w', 'delta_mla_q_norm_w': 'delta_w', 'delta_mla_q_up_w': 'delta_w', 'delta_mla_kv_norm_w': 'delta_w', 'delta_mla_kv_up_w': 'delta_w', 'delta_mla_out_w': 'delta_w', 'delta_gdn_in_w': 'delta_w', 'delta_gdn_conv_w': 'delta_w', 'delta_gdn_a_log': 'delta_w', 'delta_gdn_dt_bias': 'delta_w', 'delta_gdn_norm_w': 'delta_w', 'delta_gdn_out_w': 'delta_w', 'delta_ln_g': 'delta_w', 'delta_ln_b': 'delta_w', 'new_m_ssd_in_w': 'new_m', 'new_m_ssd_conv_w': 'new_m', 'new_m_ssd_conv_b': 'new_m', 'new_m_ssd_dt_bias': 'new_m', 'new_m_ssd_a_log': 'new_m', 'new_m_ssd_d': 'new_m', 'new_m_ssd_norm_w': 'new_m', 'new_m_ssd_out_w': 'new_m', 'new_m_mla_in_w': 'new_m', 'new_m_mla_q_norm_w': 'new_m', 'new_m_mla_q_up_w': 'new_m', 'new_m_mla_kv_norm_w': 'new_m', 'new_m_mla_kv_up_w': 'new_m', 'new_m_mla_out_w': 'new_m', 'new_m_gdn_in_w': 'new_m', 'new_m_gdn_conv_w': 'new_m', 'new_m_gdn_a_log': 'new_m', 'new_m_gdn_dt_bias': 'new_m', 'new_m_gdn_norm_w': 'new_m', 'new_m_gdn_out_w': 'new_m', 'new_m_ln_g': 'new_m', 'new_m_ln_b': 'new_m', 'new_v_ssd_in_w': 'new_v', 'new_v_ssd_conv_w': 'new_v', 'new_v_ssd_conv_b': 'new_v', 'new_v_ssd_dt_bias': 'new_v', 'new_v_ssd_a_log': 'new_v', 'new_v_ssd_d': 'new_v', 'new_v_ssd_norm_w': 'new_v', 'new_v_ssd_out_w': 'new_v', 'new_v_mla_in_w': 'new_v', 'new_v_mla_q_norm_w': 'new_v', 'new_v_mla_q_up_w': 'new_v', 'new_v_mla_kv_norm_w': 'new_v', 'new_v_mla_kv_up_w': 'new_v', 'new_v_mla_out_w': 'new_v', 'new_v_gdn_in_w': 'new_v', 'new_v_gdn_conv_w': 'new_v', 'new_v_gdn_a_log': 'new_v', 'new_v_gdn_dt_bias': 'new_v', 'new_v_gdn_norm_w': 'new_v', 'new_v_gdn_out_w': 'new_v', 'new_v_ln_g': 'new_v', 'new_v_ln_b': 'new_v'}


def _forward(args):
    return _fwd_reference(*[args[k] for k in FWD_PARAMS])


def _output_shape():
    def fwd():
        inp = _fwd_setup_inputs(0)
        return _fwd_reference(*[inp[k] for k in FWD_PARAMS])
    out = _jax.eval_shape(fwd)
    return out.shape, out.dtype

N_MICROBATCH = 1
ADAM_LR = 0.001
ADAM_B1 = 0.9
ADAM_B2 = 0.999
ADAM_EPS = 1e-08
ADAM_WD = 0.01
ADAM_STEP = 10
PER_EXAMPLE_BATCH_AXIS = {'x': 0, 'positions': 0, 'loss_target': 0}
SHARED_INPUTS = []
_WEIGHT_DTYPES = {'ssd_in_w': _jnp.float32, 'ssd_conv_w': _jnp.float32, 'ssd_conv_b': _jnp.float32, 'ssd_dt_bias': _jnp.float32, 'ssd_a_log': _jnp.float32, 'ssd_d': _jnp.float32, 'ssd_norm_w': _jnp.float32, 'ssd_out_w': _jnp.float32, 'mla_in_w': _jnp.float32, 'mla_q_norm_w': _jnp.float32, 'mla_q_up_w': _jnp.float32, 'mla_kv_norm_w': _jnp.float32, 'mla_kv_up_w': _jnp.float32, 'mla_out_w': _jnp.float32, 'gdn_in_w': _jnp.float32, 'gdn_conv_w': _jnp.float32, 'gdn_a_log': _jnp.float32, 'gdn_dt_bias': _jnp.float32, 'gdn_norm_w': _jnp.float32, 'gdn_out_w': _jnp.float32, 'ln_g': _jnp.float32, 'ln_b': _jnp.float32}
MOMENT_SCALE = {'ssd_in_w': 1.483566e-02, 'ssd_conv_w': 1.410618e-02, 'ssd_conv_b': 2.348229e-02, 'ssd_dt_bias': 3.561378e-02, 'ssd_a_log': 7.017405e-02, 'ssd_d': 8.546830e-02, 'ssd_norm_w': 1.761427e-02, 'ssd_out_w': 5.827296e-02, 'mla_in_w': 4.396101e-03, 'mla_q_norm_w': 3.914608e-03, 'mla_q_up_w': 2.020383e-03, 'mla_kv_norm_w': 7.387160e-03, 'mla_kv_up_w': 2.650176e-03, 'mla_out_w': 7.421603e-03, 'gdn_in_w': 9.078454e-03, 'gdn_conv_w': 9.371503e-03, 'gdn_a_log': 5.132418e-02, 'gdn_dt_bias': 4.986359e-02, 'gdn_norm_w': 5.857023e-02, 'gdn_out_w': 3.546295e-02, 'ln_g': 8.034699e+00, 'ln_b': 6.817138e-01}


def _to_microbatches(a, axis):
    t = _jnp.moveaxis(a, axis, 0)
    t = t.reshape((N_MICROBATCH, t.shape[0] // N_MICROBATCH) + t.shape[1:])
    return _jnp.moveaxis(t, 1, axis + 1)


def setup_inputs(seed: int = 0) -> dict:
    inp = _fwd_setup_inputs(seed)
    key = _jax.random.fold_in(_jax.random.key(seed), 7919)
    shape, _ = _output_shape()
    out = dict(inp)
    out["loss_target"] = _jax.random.normal(_jax.random.fold_in(key, 0), shape, _jnp.float32)
    for i, name in enumerate(TWIN_WEIGHTS):
        w = inp[name].astype(_jnp.float32)
        if MOMENT_SCALE is None:
            s = _jnp.sqrt(_jnp.mean(_jnp.square(w)) + 1e-30)
        else:
            s = MOMENT_SCALE[name]
        km, kv = _jax.random.split(_jax.random.fold_in(key, i + 1))
        out[name] = w
        out["m_" + name] = s * _jax.random.normal(km, w.shape, _jnp.float32)
        out["v_" + name] = (s * s) * _jax.random.uniform(kv, w.shape, _jnp.float32, 0.5, 1.5)
    if N_MICROBATCH > 1:
        for name, axis in PER_EXAMPLE_BATCH_AXIS.items():
            out[name] = _to_microbatches(out[name], axis)
    return {'x': out['x'], 'positions': out['positions'], 'ssd_in_w': out['ssd_in_w'], 'ssd_conv_w': out['ssd_conv_w'], 'ssd_conv_b': out['ssd_conv_b'], 'ssd_dt_bias': out['ssd_dt_bias'], 'ssd_a_log': out['ssd_a_log'], 'ssd_d': out['ssd_d'], 'ssd_norm_w': out['ssd_norm_w'], 'ssd_out_w': out['ssd_out_w'], 'mla_in_w': out['mla_in_w'], 'mla_q_norm_w': out['mla_q_norm_w'], 'mla_q_up_w': out['mla_q_up_w'], 'mla_kv_norm_w': out['mla_kv_norm_w'], 'mla_kv_up_w': out['mla_kv_up_w'], 'mla_out_w': out['mla_out_w'], 'gdn_in_w': out['gdn_in_w'], 'gdn_conv_w': out['gdn_conv_w'], 'gdn_a_log': out['gdn_a_log'], 'gdn_dt_bias': out['gdn_dt_bias'], 'gdn_norm_w': out['gdn_norm_w'], 'gdn_out_w': out['gdn_out_w'], 'ln_g': out['ln_g'], 'ln_b': out['ln_b'], 'loss_target': out['loss_target'], 'm_ssd_in_w': out['m_ssd_in_w'], 'm_ssd_conv_w': out['m_ssd_conv_w'], 'm_ssd_conv_b': out['m_ssd_conv_b'], 'm_ssd_dt_bias': out['m_ssd_dt_bias'], 'm_ssd_a_log': out['m_ssd_a_log'], 'm_ssd_d': out['m_ssd_d'], 'm_ssd_norm_w': out['m_ssd_norm_w'], 'm_ssd_out_w': out['m_ssd_out_w'], 'm_mla_in_w': out['m_mla_in_w'], 'm_mla_q_norm_w': out['m_mla_q_norm_w'], 'm_mla_q_up_w': out['m_mla_q_up_w'], 'm_mla_kv_norm_w': out['m_mla_kv_norm_w'], 'm_mla_kv_up_w': out['m_mla_kv_up_w'], 'm_mla_out_w': out['m_mla_out_w'], 'm_gdn_in_w': out['m_gdn_in_w'], 'm_gdn_conv_w': out['m_gdn_conv_w'], 'm_gdn_a_log': out['m_gdn_a_log'], 'm_gdn_dt_bias': out['m_gdn_dt_bias'], 'm_gdn_norm_w': out['m_gdn_norm_w'], 'm_gdn_out_w': out['m_gdn_out_w'], 'm_ln_g': out['m_ln_g'], 'm_ln_b': out['m_ln_b'], 'v_ssd_in_w': out['v_ssd_in_w'], 'v_ssd_conv_w': out['v_ssd_conv_w'], 'v_ssd_conv_b': out['v_ssd_conv_b'], 'v_ssd_dt_bias': out['v_ssd_dt_bias'], 'v_ssd_a_log': out['v_ssd_a_log'], 'v_ssd_d': out['v_ssd_d'], 'v_ssd_norm_w': out['v_ssd_norm_w'], 'v_ssd_out_w': out['v_ssd_out_w'], 'v_mla_in_w': out['v_mla_in_w'], 'v_mla_q_norm_w': out['v_mla_q_norm_w'], 'v_mla_q_up_w': out['v_mla_q_up_w'], 'v_mla_kv_norm_w': out['v_mla_kv_norm_w'], 'v_mla_kv_up_w': out['v_mla_kv_up_w'], 'v_mla_out_w': out['v_mla_out_w'], 'v_gdn_in_w': out['v_gdn_in_w'], 'v_gdn_conv_w': out['v_gdn_conv_w'], 'v_gdn_a_log': out['v_gdn_a_log'], 'v_gdn_dt_bias': out['v_gdn_dt_bias'], 'v_gdn_norm_w': out['v_gdn_norm_w'], 'v_gdn_out_w': out['v_gdn_out_w'], 'v_ln_g': out['v_ln_g'], 'v_ln_b': out['v_ln_b']}


def _loss(weights, diff, rest, loss_target):
    with _jax.named_scope("forward"):
        args = {**rest, TWIN_DIFF_INPUT: diff, **{k: w.astype(_WEIGHT_DTYPES[k]) for k, w in weights.items()}}
        y = _forward(args)
    with _jax.named_scope("loss_head"):
        err = _jnp.square(y.astype(_jnp.float32) - loss_target)
        return 0.5 * _jnp.sum(_jnp.mean(err, axis=-1)) if err.ndim else 0.5 * err


def _adamw(w, g, m, v):
    m = ADAM_B1 * m + (1.0 - ADAM_B1) * g
    v = ADAM_B2 * v + (1.0 - ADAM_B2) * _jnp.square(g)
    m_hat = m / (1.0 - ADAM_B1 ** ADAM_STEP)
    v_hat = v / (1.0 - ADAM_B2 ** ADAM_STEP)
    delta = -ADAM_LR * (m_hat / (_jnp.sqrt(v_hat) + ADAM_EPS) + ADAM_WD * w)
    return delta, m, v


def reference(x, positions, ssd_in_w, ssd_conv_w, ssd_conv_b, ssd_dt_bias, ssd_a_log, ssd_d, ssd_norm_w, ssd_out_w, mla_in_w, mla_q_norm_w, mla_q_up_w, mla_kv_norm_w, mla_kv_up_w, mla_out_w, gdn_in_w, gdn_conv_w, gdn_a_log, gdn_dt_bias, gdn_norm_w, gdn_out_w, ln_g, ln_b, loss_target, m_ssd_in_w, m_ssd_conv_w, m_ssd_conv_b, m_ssd_dt_bias, m_ssd_a_log, m_ssd_d, m_ssd_norm_w, m_ssd_out_w, m_mla_in_w, m_mla_q_norm_w, m_mla_q_up_w, m_mla_kv_norm_w, m_mla_kv_up_w, m_mla_out_w, m_gdn_in_w, m_gdn_conv_w, m_gdn_a_log, m_gdn_dt_bias, m_gdn_norm_w, m_gdn_out_w, m_ln_g, m_ln_b, v_ssd_in_w, v_ssd_conv_w, v_ssd_conv_b, v_ssd_dt_bias, v_ssd_a_log, v_ssd_d, v_ssd_norm_w, v_ssd_out_w, v_mla_in_w, v_mla_q_norm_w, v_mla_q_up_w, v_mla_kv_norm_w, v_mla_kv_up_w, v_mla_out_w, v_gdn_in_w, v_gdn_conv_w, v_gdn_a_log, v_gdn_dt_bias, v_gdn_norm_w, v_gdn_out_w, v_ln_g, v_ln_b):
    given = dict(x=x, positions=positions, ssd_in_w=ssd_in_w, ssd_conv_w=ssd_conv_w, ssd_conv_b=ssd_conv_b, ssd_dt_bias=ssd_dt_bias, ssd_a_log=ssd_a_log, ssd_d=ssd_d, ssd_norm_w=ssd_norm_w, ssd_out_w=ssd_out_w, mla_in_w=mla_in_w, mla_q_norm_w=mla_q_norm_w, mla_q_up_w=mla_q_up_w, mla_kv_norm_w=mla_kv_norm_w, mla_kv_up_w=mla_kv_up_w, mla_out_w=mla_out_w, gdn_in_w=gdn_in_w, gdn_conv_w=gdn_conv_w, gdn_a_log=gdn_a_log, gdn_dt_bias=gdn_dt_bias, gdn_norm_w=gdn_norm_w, gdn_out_w=gdn_out_w, ln_g=ln_g, ln_b=ln_b, loss_target=loss_target, m_ssd_in_w=m_ssd_in_w, m_ssd_conv_w=m_ssd_conv_w, m_ssd_conv_b=m_ssd_conv_b, m_ssd_dt_bias=m_ssd_dt_bias, m_ssd_a_log=m_ssd_a_log, m_ssd_d=m_ssd_d, m_ssd_norm_w=m_ssd_norm_w, m_ssd_out_w=m_ssd_out_w, m_mla_in_w=m_mla_in_w, m_mla_q_norm_w=m_mla_q_norm_w, m_mla_q_up_w=m_mla_q_up_w, m_mla_kv_norm_w=m_mla_kv_norm_w, m_mla_kv_up_w=m_mla_kv_up_w, m_mla_out_w=m_mla_out_w, m_gdn_in_w=m_gdn_in_w, m_gdn_conv_w=m_gdn_conv_w, m_gdn_a_log=m_gdn_a_log, m_gdn_dt_bias=m_gdn_dt_bias, m_gdn_norm_w=m_gdn_norm_w, m_gdn_out_w=m_gdn_out_w, m_ln_g=m_ln_g, m_ln_b=m_ln_b, v_ssd_in_w=v_ssd_in_w, v_ssd_conv_w=v_ssd_conv_w, v_ssd_conv_b=v_ssd_conv_b, v_ssd_dt_bias=v_ssd_dt_bias, v_ssd_a_log=v_ssd_a_log, v_ssd_d=v_ssd_d, v_ssd_norm_w=v_ssd_norm_w, v_ssd_out_w=v_ssd_out_w, v_mla_in_w=v_mla_in_w, v_mla_q_norm_w=v_mla_q_norm_w, v_mla_q_up_w=v_mla_q_up_w, v_mla_kv_norm_w=v_mla_kv_norm_w, v_mla_kv_up_w=v_mla_kv_up_w, v_mla_out_w=v_mla_out_w, v_gdn_in_w=v_gdn_in_w, v_gdn_conv_w=v_gdn_conv_w, v_gdn_a_log=v_gdn_a_log, v_gdn_dt_bias=v_gdn_dt_bias, v_gdn_norm_w=v_gdn_norm_w, v_gdn_out_w=v_gdn_out_w, v_ln_g=v_ln_g, v_ln_b=v_ln_b)
    weights = {n: given[n] for n in TWIN_WEIGHTS}
    shared = {n: given[n] for n in SHARED_INPUTS}
    per_example = {n: given[n] for n in ['x', 'positions']}
    grad_fn = _jax.value_and_grad(_loss, argnums=(0, 1))

    def one_microbatch(ex, loss_target):
        ex = dict(ex)
        diff = ex.pop(TWIN_DIFF_INPUT)
        return grad_fn(weights, diff, {**shared, **ex}, loss_target)

    if N_MICROBATCH == 1:
        loss, (grad_w, grad_x) = one_microbatch(per_example, given["loss_target"])
    else:
        def body(carry, xs):
            loss_sum, grad_sum = carry
            l_k, (gw_k, gx_k) = one_microbatch(xs[0], xs[1])
            with _jax.named_scope("update"):
                return (loss_sum + l_k, _jax.tree.map(_jnp.add, grad_sum, gw_k)), gx_k

        init = (_jnp.zeros((), _jnp.float32), _jax.tree.map(_jnp.zeros_like, weights))
        (loss, grad_w), grad_x = _jax.lax.scan(body, init, (per_example, given["loss_target"]))
    with _jax.named_scope("update"):
        delta_w, new_m, new_v = {}, {}, {}
        for n in TWIN_WEIGHTS:
            delta_w[n], new_m[n], new_v[n] = _adamw(weights[n], grad_w[n], given["m_" + n], given["v_" + n])
    return (loss, grad_x, *[grad_w[n] for n in TWIN_WEIGHTS], *[delta_w[n] for n in TWIN_WEIGHTS],
            *[new_m[n] for n in TWIN_WEIGHTS], *[new_v[n] for n in TWIN_WEIGHTS])
```

```python
import functools
import math

import jax
import jax.numpy as jnp
from jax import lax
from jax.experimental import pallas as pl
from jax.experimental.pallas import tpu as pltpu

f32 = jnp.float32
HI = lax.Precision.HIGHEST
MXU_DT = jnp.bfloat16
MESH = pl.DeviceIdType.MESH

DEPTH = 4
LN_EPS = 1e-5
RMS_EPS = 1e-6
SSD_HEAD_DIM = 64
SSD_N_GROUPS = 8
SSD_D_STATE = 128
SSD_CONV = 4
SSD_CHUNK = 128
MLA_Q_RANK = 768
MLA_KV_RANK = 512
MLA_NOPE = 128
MLA_ROPE = 64
MLA_V = 128
ROPE_THETA = 10000.0
GDN_N_QK_HEADS = 16
GDN_N_V_HEADS = 32
GDN_DK = 128
GDN_DV = 128
GDN_CONV = 4
GDN_CHUNK = 64
ADAM_LR = 0.001
ADAM_B1 = 0.9
ADAM_B2 = 0.999
ADAM_EPS = 1e-08
ADAM_WD = 0.01
ADAM_STEP = 10

LANES = 128
VMEM_LIMIT = 48 * 1024 * 1024
ATT_TILE = 512
ROW_TILE = 256
MM_TILE_M = 1024
MM_TILE_N = 1024
MM_TILE_K = 512


def _alpha():
    return (2.0 * DEPTH) ** 0.25


def _tile(n, pref, align=LANES):
    t = min(pref, n) // align * align
    while t >= align:
        if n % t == 0:
            return t
        t -= align
    return n


def _cp(sem=None):
    return pltpu.CompilerParams(dimension_semantics=sem, vmem_limit_bytes=VMEM_LIMIT)


def _iota(shape, dim):
    return lax.broadcasted_iota(jnp.int32, shape, dim)


def _div_pow2(x, p):
    assert p & (p - 1) == 0
    return lax.shift_right_logical(x, jnp.int32(p.bit_length() - 1))


def _dot(a, b, dims=((1,), (0,)), hi=False):
    if hi:
        return lax.dot_general(a.astype(f32), b.astype(f32), (dims, ((), ())), precision=HI, preferred_element_type=f32)
    return lax.dot_general(a.astype(MXU_DT), b.astype(MXU_DT), (dims, ((), ())), preferred_element_type=f32)


_NT = ((1,), (1,))
_TN = ((0,), (0,))


def _softplus(x):
    return jnp.maximum(x, 0.0) + jnp.log1p(jnp.exp(-jnp.abs(x)))


def _silu(x):
    return x * jax.nn.sigmoid(x)


def _mm(a, b, *, name, ta=False, tb=False, add=None, add_scale=1.0, out_dtype=f32):
    M, K = (a.shape[1], a.shape[0]) if ta else a.shape
    N = b.shape[0] if tb else b.shape[1]
    assert (b.shape[1] if tb else b.shape[0]) == K, (a.shape, b.shape, ta, tb)
    tm, tn, tk = _tile(M, MM_TILE_M), _tile(N, MM_TILE_N), _tile(K, MM_TILE_K)
    nk = K // tk
    a_spec = pl.BlockSpec((tk, tm), lambda i, j, k: (k, i)) if ta else pl.BlockSpec((tm, tk), lambda i, j, k: (i, k))
    b_spec = pl.BlockSpec((tn, tk), lambda i, j, k: (j, k)) if tb else pl.BlockSpec((tk, tn), lambda i, j, k: (k, j))
    o_spec = pl.BlockSpec((tm, tn), lambda i, j, k: (i, j))
    dims = ((0 if ta else 1,), (1 if tb else 0,))
    has_add = add is not None

    def body(*refs):
        if has_add:
            a_ref, b_ref, add_ref, o_ref, acc = refs
        else:
            a_ref, b_ref, o_ref, acc = refs
        k = pl.program_id(2)

        @pl.when(k == 0)
        def _():
            acc[...] = jnp.zeros_like(acc)

        acc[...] += _dot(a_ref[...], b_ref[...], dims)

        @pl.when(k == nk - 1)
        def _():
            r = acc[...]
            if has_add:
                r = r + add_scale * add_ref[...].astype(f32)
            o_ref[...] = r.astype(out_dtype)

    ins = [a, b] + ([add] if has_add else [])
    specs = [a_spec, b_spec] + ([o_spec] if has_add else [])
    return pl.pallas_call(
        body, name=name, grid=(M // tm, N // tn, nk), in_specs=specs, out_specs=o_spec,
        out_shape=jax.ShapeDtypeStruct((M, N), out_dtype), scratch_shapes=[pltpu.VMEM((tm, tn), f32)],
        compiler_params=_cp(("parallel", "parallel", "arbitrary")))(*ins)


def _rw_specs(params, ins, ncol, tm):
    specs = []
    for arr, mode, bw, coff in params:
        if mode == "c":
            specs.append(pl.BlockSpec((1, bw), lambda c, r, coff=coff: (0, c + coff)))
        else:
            specs.append(pl.BlockSpec((1, bw), lambda c, r, coff=coff: (0, coff)))
    for arr, mode, bw, coff in ins:
        if mode == "c":
            specs.append(pl.BlockSpec((tm, bw), lambda c, r, coff=coff: (r, c + coff)))
        else:
            specs.append(pl.BlockSpec((tm, bw), lambda c, r, coff=coff: (r, coff)))
    return specs


def _norm_spec(lst):
    out = []
    for t in lst:
        arr, mode, bw = t[0], t[1], t[2]
        coff = t[3] if len(t) > 3 else 0
        out.append((arr, mode, bw, coff))
    return out


def _rowwise(fn, params, ins, outs, *, name, ncol=1, tm=None):
    params, ins = _norm_spec(params), _norm_spec(ins)
    S = ins[0][0].shape[0]
    tm = _tile(S, tm or ROW_TILE, 8)
    npar, nin = len(params), len(ins)

    def body(*refs):
        pv = [r[...].astype(f32) for r in refs[:npar]]
        iv = [r[...].astype(f32) for r in refs[npar:npar + nin]]
        res = fn(*pv, *iv)
        for o_ref, val in zip(refs[npar + nin:], res):
            o_ref[...] = val.astype(o_ref.dtype)

    out_specs, out_shapes = [], []
    for W, dt, mode, bw in outs:
        out_shapes.append(jax.ShapeDtypeStruct((S, W), dt))
        if mode == "c":
            out_specs.append(pl.BlockSpec((tm, bw), lambda c, r: (r, c)))
        else:
            out_specs.append(pl.BlockSpec((tm, bw), lambda c, r: (r, 0)))
    return pl.pallas_call(
        body, name=name, grid=(ncol, S // tm), in_specs=_rw_specs(params, ins, ncol, tm), out_specs=out_specs,
        out_shape=out_shapes, compiler_params=_cp(("parallel", "parallel")))(*[p[0] for p in params], *[i[0] for i in ins])


def _rowwise_bwd(fn, params, ins, couts, *, name, ncol=1, tm=None, diff_p=None, diff_i=None):
    params, ins, couts = _norm_spec(params), _norm_spec(ins), _norm_spec(couts)
    S = ins[0][0].shape[0]
    tm = _tile(S, tm or ROW_TILE, 8)
    npar, nin, nco = len(params), len(ins), len(couts)
    diff_p = list(range(npar)) if diff_p is None else diff_p
    diff_i = list(range(nin)) if diff_i is None else diff_i

    def body(*refs):
        c, r = pl.program_id(0), pl.program_id(1)
        pv = [x[...].astype(f32) for x in refs[:npar]]
        iv = [x[...].astype(f32) for x in refs[npar:npar + nin]]
        cv = [x[...].astype(f32) for x in refs[npar + nin:npar + nin + nco]]
        orefs = refs[npar + nin + nco:]

        def g(*dargs):
            p2, i2 = list(pv), list(iv)
            for n, k in enumerate(diff_p):
                p2[k] = dargs[n]
            for n, k in enumerate(diff_i):
                i2[k] = dargs[len(diff_p) + n]
            return tuple(fn(*p2, *i2))

        _, vjp = jax.vjp(g, *[pv[k] for k in diff_p], *[iv[k] for k in diff_i])
        grads = vjp(tuple(cv))
        for n, k in enumerate(diff_p):
            o_ref = orefs[n]
            first = (r == 0) if params[k][1] == "c" else jnp.logical_and(r == 0, c == 0)

            @pl.when(first)
            def _(o_ref=o_ref):
                o_ref[...] = jnp.zeros_like(o_ref)

            o_ref[...] += grads[n]
        for n, k in enumerate(diff_i):
            orefs[len(diff_p) + n][...] = grads[len(diff_p) + n]

    out_specs, out_shapes = [], []
    for k in diff_p:
        arr, mode, bw, coff = params[k]
        W = bw * ncol if mode == "c" else bw
        out_shapes.append(jax.ShapeDtypeStruct((1, W), f32))
        out_specs.append(pl.BlockSpec((1, bw), (lambda c, r: (0, c)) if mode == "c" else (lambda c, r: (0, 0))))
    for k in diff_i:
        arr, mode, bw, coff = ins[k]
        W = bw * ncol if mode == "c" else bw
        out_shapes.append(jax.ShapeDtypeStruct((S, W), f32))
        out_specs.append(pl.BlockSpec((tm, bw), (lambda c, r: (r, c)) if mode == "c" else (lambda c, r: (r, 0))))
    res = pl.pallas_call(
        body, name=name, grid=(ncol, S // tm), in_specs=_rw_specs(params, ins + couts, ncol, tm), out_specs=out_specs,
        out_shape=out_shapes, compiler_params=_cp(("arbitrary", "arbitrary")))(
            *[p[0] for p in params], *[i[0] for i in ins], *[c[0] for c in couts])
    return list(res[:len(diff_p)]), list(res[len(diff_p):])


def _ln_fn(g, b, r):
    mu = jnp.mean(r, -1, keepdims=True)
    xc = r - mu
    var = jnp.mean(xc * xc, -1, keepdims=True)
    return (xc * lax.rsqrt(var + LN_EPS) * g + b,)


def _res_ln_fn(g, b, h, y):
    r = _alpha() * h + y
    return (r,) + _ln_fn(g, b, r)


def _rms_fn(w, x):
    return (x * lax.rsqrt(jnp.mean(x * x, -1, keepdims=True) + RMS_EPS) * w,)


def _ssd_gate_fn(w, y, z):
    yg = y * _silu(z)
    return (yg * lax.rsqrt(jnp.mean(yg * yg, -1, keepdims=True) + RMS_EPS) * w,)


def _mul_silu_fn(o, z):
    return (o * _silu(z),)


def _gdn_gate_fn(w, o, z):
    return (o * lax.rsqrt(jnp.mean(o * o, -1, keepdims=True) + RMS_EPS) * w * _silu(z),)


def _l2_fn(scale, x):
    return (x * lax.rsqrt(jnp.sum(x * x, -1, keepdims=True) + RMS_EPS) * scale,)


def _rope_fn(cos, sin, x):
    half = MLA_ROPE // 2
    i = _iota((LANES, LANES), 0)
    j = _iota((LANES, LANES), 1)
    pm = jnp.where((i == j + half) & (j < half), -1.0, 0.0) + jnp.where((i + half == j) & (j < 2 * half), 1.0, 0.0)
    return (x * cos + _dot(x, pm.astype(f32), hi=True) * sin,)


def _conv_taps(x, K):
    S = x.shape[0]
    rows = _iota(x.shape, 0)
    return [x] + [jnp.where(rows < j, 0.0, pltpu.roll(x, j, 0)) for j in range(1, K)]


def _conv_fwd(x, w, b, *, name):
    S, C = x.shape
    K = w.shape[0]
    cw = _tile(C, LANES)

    def body(x_ref, w_ref, b_ref, o_ref):
        taps = _conv_taps(x_ref[...], K)
        wv = w_ref[...]
        pre = b_ref[...] + taps[0] * wv[K - 1:K, :]
        for j in range(1, K):
            pre = pre + taps[j] * wv[K - 1 - j:K - j, :]
        o_ref[...] = _silu(pre)

    return pl.pallas_call(
        body, name=name, grid=(C // cw,),
        in_specs=[pl.BlockSpec((S, cw), lambda c: (0, c)), pl.BlockSpec((K, cw), lambda c: (0, c)), pl.BlockSpec((1, cw), lambda c: (0, c))],
        out_specs=pl.BlockSpec((S, cw), lambda c: (0, c)), out_shape=jax.ShapeDtypeStruct((S, C), f32),
        compiler_params=_cp(("parallel",)))(x, w, b)


def _conv_bwd(x, w, b, dy, *, name):
    S, C = x.shape
    K = w.shape[0]
    cw = _tile(C, LANES)

    def body(x_ref, w_ref, b_ref, dy_ref, dx_ref, dw_ref, db_ref):
        taps = _conv_taps(x_ref[...], K)
        wv = w_ref[...]
        pre = b_ref[...] + taps[0] * wv[K - 1:K, :]
        for j in range(1, K):
            pre = pre + taps[j] * wv[K - 1 - j:K - j, :]
        sg = jax.nn.sigmoid(pre)
        dpre = dy_ref[...] * sg * (1.0 + pre * (1.0 - sg))
        db_ref[...] = jnp.sum(dpre, axis=0, keepdims=True)
        rows = _iota(dpre.shape, 0)
        dx = dpre * wv[K - 1:K, :]
        dw_ref[K - 1:K, :] = jnp.sum(dpre * taps[0], axis=0, keepdims=True)
        for j in range(1, K):
            dw_ref[K - 1 - j:K - j, :] = jnp.sum(dpre * taps[j], axis=0, keepdims=True)
            up = jnp.where(rows >= S - j, 0.0, pltpu.roll(dpre, S - j, 0))
            dx = dx + up * wv[K - 1 - j:K - j, :]
        dx_ref[...] = dx

    col = lambda c: (0, c)
    return pl.pallas_call(
        body, name=name, grid=(C // cw,),
        in_specs=[pl.BlockSpec((S, cw), col), pl.BlockSpec((K, cw), col), pl.BlockSpec((1, cw), col), pl.BlockSpec((S, cw), col)],
        out_specs=[pl.BlockSpec((S, cw), col), pl.BlockSpec((K, cw), col), pl.BlockSpec((1, cw), col)],
        out_shape=[jax.ShapeDtypeStruct((S, C), f32), jax.ShapeDtypeStruct((K, C), f32), jax.ShapeDtypeStruct((1, C), f32)],
        compiler_params=_cp(("parallel",)))(x, w, b, dy)


def _ssd_chunk(prev, xs, Bm, Cm, dtr, dtb, alog, dsk, g, *, R, P):
    L, GW = xs.shape
    H = dtr.shape[1]
    tril = _iota((L, L), 0) >= _iota((L, L), 1)
    dt = _softplus(dtr + dtb)
    acs = _dot(tril.astype(f32), dt * (-jnp.exp(alog)), hi=True)
    expand = (_iota((H, GW), 0) == g * R + _div_pow2(_iota((H, GW), 1), P)).astype(f32)
    dt_e = _dot(dt, expand, hi=True)
    acs_e = _dot(acs, expand, hi=True)
    d_e = _dot(jnp.broadcast_to(dsk, (L, H)), expand, hi=True)
    last = jnp.sum(jnp.where(_iota((L, GW), 0) == L - 1, acs_e, 0.0), axis=0, keepdims=True)
    xdt = xs * dt_e
    cb = _dot(Cm, Bm, _NT)
    nsel = max(R, 8)
    sel = (_iota((nsel, H), 1) == g * R + _iota((nsel, H), 0)).astype(f32)
    acs_t = _dot(sel, acs, _NT, hi=True)
    hp = LANES // P
    pieces = []
    for p in range(GW // LANES):
        xp = xdt[:, p * LANES:(p + 1) * LANES]
        acc = None
        for q in range(hp):
            r = p * hp + q
            col = jnp.sum(jnp.where(_iota((L, H), 1) == g * R + r, acs, 0.0), axis=1, keepdims=True)
            row = jnp.sum(jnp.where(_iota((nsel, L), 0) == r, acs_t, 0.0), axis=0, keepdims=True)
            dec = jnp.where(tril, jnp.exp(jnp.where(tril, col - row, 0.0)), 0.0)
            xm = jnp.where(_div_pow2(_iota((L, LANES), 1), P) == q, xp, 0.0)
            t = _dot(cb * dec, xm)
            acc = t if acc is None else acc + t
        pieces.append(acc)
    y_diag = pieces[0] if len(pieces) == 1 else jnp.concatenate(pieces, axis=1)
    st = _dot(Bm, xdt * jnp.exp(last - acs_e), _TN)
    y_off = _dot(Cm, prev) * jnp.exp(acs_e)
    new = prev * jnp.exp(last) + st
    return y_diag + y_off + xs * d_e, new


def _ssd_dims(xbc, dtr):
    S, CD = xbc.shape
    H = dtr.shape[1]
    G, N, P = SSD_N_GROUPS, SSD_D_STATE, SSD_HEAD_DIM
    DI = H * P
    R = H // G
    assert CD == DI + 2 * G * N and DI % N == 0
    return S, H, G, N, P, DI, R, R * P, SSD_CHUNK


def _ssd_scan_fwd(xbc, dtr, dtb, alog, dsk, *, name):
    S, H, G, N, P, DI, R, GW, L = _ssd_dims(xbc, dtr)
    nc = S // L
    boff, coff = DI // N, DI // N + G

    def body(xs_ref, b_ref, c_ref, dtr_ref, dtb_ref, alog_ref, dsk_ref, y_ref, st_ref, state):
        c, g = pl.program_id(0), pl.program_id(1)

        @pl.when(c == 0)
        def _():
            state[g] = jnp.zeros((N, GW), f32)

        prev = state[g]
        st_ref[0, 0] = prev
        y, new = _ssd_chunk(prev, xs_ref[...], b_ref[...], c_ref[...], dtr_ref[...], dtb_ref[...], alog_ref[...],
                            dsk_ref[...], g, R=R, P=P)
        y_ref[...] = y
        state[g] = new

    par = pl.BlockSpec((1, H), lambda c, g: (0, 0))
    return pl.pallas_call(
        body, name=name, grid=(nc, G),
        in_specs=[pl.BlockSpec((L, GW), lambda c, g: (c, g)), pl.BlockSpec((L, N), lambda c, g: (c, boff + g)),
                  pl.BlockSpec((L, N), lambda c, g: (c, coff + g)), pl.BlockSpec((L, H), lambda c, g: (c, 0)), par, par, par],
        out_specs=[pl.BlockSpec((L, GW), lambda c, g: (c, g)), pl.BlockSpec((1, 1, N, GW), lambda c, g: (c, g, 0, 0))],
        out_shape=[jax.ShapeDtypeStruct((S, DI), f32), jax.ShapeDtypeStruct((nc, G, N, GW), f32)],
        scratch_shapes=[pltpu.VMEM((G, N, GW), f32)],
        compiler_params=_cp(("arbitrary", "arbitrary")))(xbc, xbc, xbc, dtr, dtb, alog, dsk)


def _ssd_scan_bwd(xbc, dtr, dtb, alog, dsk, states, dy, *, name):
    S, H, G, N, P, DI, R, GW, L = _ssd_dims(xbc, dtr)
    nc = S // L
    boff, coff = DI // N, DI // N + G

    def body(xs_ref, b_ref, c_ref, dtr_ref, dtb_ref, alog_ref, dsk_ref, st_ref, dy_ref,
             dxs_ref, db_ref, dc_ref, ddtr_ref, ddtb_ref, dalog_ref, ddsk_ref, dstate):
        c, g = pl.program_id(0), pl.program_id(1)

        @pl.when(c == 0)
        def _():
            dstate[g] = jnp.zeros((N, GW), f32)

        @pl.when(jnp.logical_and(c == 0, g == 0))
        def _():
            ddtb_ref[...] = jnp.zeros_like(ddtb_ref)
            dalog_ref[...] = jnp.zeros_like(dalog_ref)
            ddsk_ref[...] = jnp.zeros_like(ddsk_ref)

        @pl.when(g == 0)
        def _():
            ddtr_ref[...] = jnp.zeros_like(ddtr_ref)

        fn = functools.partial(_ssd_chunk, g=g, R=R, P=P)
        _, vjp = jax.vjp(fn, st_ref[0, 0], xs_ref[...], b_ref[...], c_ref[...], dtr_ref[...], dtb_ref[...],
                         alog_ref[...], dsk_ref[...])
        dprev, dxs, dB, dC, ddtr, ddtb, dalog, ddsk = vjp((dy_ref[...], dstate[g]))
        dstate[g] = dprev
        dxs_ref[...] = dxs
        db_ref[...] = dB
        dc_ref[...] = dC
        ddtr_ref[...] += ddtr
        ddtb_ref[...] += ddtb
        dalog_ref[...] += dalog
        ddsk_ref[...] += ddsk

    rc = lambda c: nc - 1 - c
    par = pl.BlockSpec((1, H), lambda c, g: (0, 0))
    return pl.pallas_call(
        body, name=name, grid=(nc, G),
        in_specs=[pl.BlockSpec((L, GW), lambda c, g: (rc(c), g)), pl.BlockSpec((L, N), lambda c, g: (rc(c), boff + g)),
                  pl.BlockSpec((L, N), lambda c, g: (rc(c), coff + g)), pl.BlockSpec((L, H), lambda c, g: (rc(c), 0)),
                  par, par, par, pl.BlockSpec((1, 1, N, GW), lambda c, g: (rc(c), g, 0, 0)),
                  pl.BlockSpec((L, GW), lambda c, g: (rc(c), g))],
        out_specs=[pl.BlockSpec((L, GW), lambda c, g: (rc(c), g)), pl.BlockSpec((L, N), lambda c, g: (rc(c), g)),
                   pl.BlockSpec((L, N), lambda c, g: (rc(c), g)), pl.BlockSpec((L, H), lambda c, g: (rc(c), 0)), par, par, par],
        out_shape=[jax.ShapeDtypeStruct((S, DI), f32), jax.ShapeDtypeStruct((S, G * N), f32), jax.ShapeDtypeStruct((S, G * N), f32),
                   jax.ShapeDtypeStruct((S, H), f32)] + [jax.ShapeDtypeStruct((1, H), f32)] * 3,
        scratch_shapes=[pltpu.VMEM((G, N, GW), f32)],
        compiler_params=_cp(("arbitrary", "arbitrary")))(xbc, xbc, xbc, dtr, dtb, alog, dsk, states, dy)


def _neumann_inverse(A):
    L = A.shape[0]
    eye = (_iota((L, L), 0) == _iota((L, L), 1)).astype(f32)
    X = -A
    P = eye + X
    n = 1
    while 2 * n < L:
        X = _dot(X, X, hi=True)
        P = P + _dot(P, X, hi=True)
        n *= 2
    return P


@jax.custom_vjp
def _unit_lower_solve(A, R):
    return _dot(_neumann_inverse(A), R, hi=True)


def _uls_fwd(A, R):
    T = _neumann_inverse(A)
    X = _dot(T, R, hi=True)
    return X, (T, X)


def _uls_bwd(res, dX):
    T, X = res
    dR = _dot(T, dX, _TN, hi=True)
    return -_dot(dR, X, _NT, hi=True), dR


_unit_lower_solve.defvjp(_uls_fwd, _uls_bwd)


def _gdn_chunk(state, q, k, v, br, ar, alog, dtb, h):
    L, DK = q.shape
    DV = v.shape[1]
    HV = br.shape[1]
    incl = _iota((L, L), 0) >= _iota((L, L), 1)
    strict = _iota((L, L), 0) > _iota((L, L), 1)
    lane = _iota((L, HV), 1)
    g_all = -jnp.exp(alog) * _softplus(ar + dtb)
    gcs = _dot(incl.astype(f32), g_all, hi=True)
    gc = jnp.sum(jnp.where(lane == h, gcs, 0.0), axis=1, keepdims=True)
    beta = jnp.sum(jnp.where(lane == h, jax.nn.sigmoid(br), 0.0), axis=1, keepdims=True)
    sel = ((_iota((8, HV), 1) == h) & (_iota((8, HV), 0) == 0)).astype(f32)
    gc_row = jnp.sum(_dot(sel, gcs, _NT, hi=True), axis=0, keepdims=True)
    decay = jnp.where(incl, jnp.exp(jnp.where(incl, gc - gc_row, 0.0)), 0.0)
    kb = k * beta
    a_mat = jnp.where(strict, _dot(kb, k, _NT) * decay, 0.0)
    eg = jnp.exp(gc)
    sol = _unit_lower_solve(a_mat, jnp.concatenate([v * beta, kb * eg], axis=1))
    u_val, w_dec = sol[:, :DV], sol[:, DV:]
    qk = jnp.where(incl, _dot(q, k, _NT) * decay, 0.0)
    g_last = jnp.sum(jnp.where(_iota((L, 1), 0) == L - 1, gc, 0.0), axis=0, keepdims=True)
    k_dec = k * jnp.exp(g_last - gc)
    v_new = u_val - _dot(w_dec, state)
    o = _dot(q * eg, state) + _dot(qk, v_new)
    new = state * jnp.exp(g_last) + _dot(k_dec, v_new, _TN)
    return o, new


def _gdn_scan_fwd(qkn, qkv, br, ar, alog, dtb, *, name):
    S = qkn.shape[0]
    HK, HV, DK, DV, L = GDN_N_QK_HEADS, GDN_N_V_HEADS, GDN_DK, GDN_DV, GDN_CHUNK
    rep = HV // HK
    nc = S // L
    voff = 2 * HK * DK // DV

    def body(q_ref, k_ref, v_ref, br_ref, ar_ref, alog_ref, dtb_ref, o_ref, st_ref, state):
        c, h = pl.program_id(0), pl.program_id(1)

        @pl.when(c == 0)
        def _():
            state[h] = jnp.zeros((DK, DV), f32)

        prev = state[h]
        st_ref[0, 0] = prev
        o, new = _gdn_chunk(prev, q_ref[...], k_ref[...], v_ref[...], br_ref[...], ar_ref[...], alog_ref[...], dtb_ref[...], h)
        o_ref[...] = o
        state[h] = new

    par = pl.BlockSpec((1, HV), lambda c, h: (0, 0))
    return pl.pallas_call(
        body, name=name, grid=(nc, HV),
        in_specs=[pl.BlockSpec((L, DK), lambda c, h: (c, lax.div(h, rep))), pl.BlockSpec((L, DK), lambda c, h: (c, HK + lax.div(h, rep))),
                  pl.BlockSpec((L, DV), lambda c, h: (c, voff + h)), pl.BlockSpec((L, HV), lambda c, h: (c, 0)),
                  pl.BlockSpec((L, HV), lambda c, h: (c, 0)), par, par],
        out_specs=[pl.BlockSpec((L, DV), lambda c, h: (c, h)), pl.BlockSpec((1, 1, DK, DV), lambda c, h: (c, h, 0, 0))],
        out_shape=[jax.ShapeDtypeStruct((S, HV * DV), f32), jax.ShapeDtypeStruct((nc, HV, DK, DV), f32)],
        scratch_shapes=[pltpu.VMEM((HV, DK, DV), f32)],
        compiler_params=_cp(("arbitrary", "arbitrary")))(qkn, qkn, qkv, br, ar, alog, dtb)


def _gdn_scan_bwd(qkn, qkv, br, ar, alog, dtb, states, do, *, name):
    S = qkn.shape[0]
    HK, HV, DK, DV, L = GDN_N_QK_HEADS, GDN_N_V_HEADS, GDN_DK, GDN_DV, GDN_CHUNK
    rep = HV // HK
    nc = S // L
    voff = 2 * HK * DK // DV

    def body(q_ref, k_ref, v_ref, br_ref, ar_ref, alog_ref, dtb_ref, st_ref, do_ref,
             dq_ref, dk_ref, dv_ref, dbr_ref, dar_ref, dalog_ref, ddtb_ref, dstate):
        c, h = pl.program_id(0), pl.program_id(1)

        @pl.when(c == 0)
        def _():
            dstate[h] = jnp.zeros((DK, DV), f32)

        @pl.when(jnp.logical_and(c == 0, h == 0))
        def _():
            dalog_ref[...] = jnp.zeros_like(dalog_ref)
            ddtb_ref[...] = jnp.zeros_like(ddtb_ref)

        @pl.when(h == 0)
        def _():
            dbr_ref[...] = jnp.zeros_like(dbr_ref)
            dar_ref[...] = jnp.zeros_like(dar_ref)

        fn = functools.partial(_gdn_chunk, h=h)
        _, vjp = jax.vjp(fn, st_ref[0, 0], q_ref[...], k_ref[...], v_ref[...], br_ref[...], ar_ref[...], alog_ref[...], dtb_ref[...])
        dprev, dq, dk, dv, dbr, dar, dalog, ddtb = vjp((do_ref[...], dstate[h]))
        dstate[h] = dprev
        dq_ref[...] = dq
        dk_ref[...] = dk
        dv_ref[...] = dv
        dbr_ref[...] += dbr
        dar_ref[...] += dar
        dalog_ref[...] += dalog
        ddtb_ref[...] += ddtb

    rc = lambda c: nc - 1 - c
    par = pl.BlockSpec((1, HV), lambda c, h: (0, 0))
    blk = lambda W: pl.BlockSpec((L, W), lambda c, h: (rc(c), h))
    return pl.pallas_call(
        body, name=name, grid=(nc, HV),
        in_specs=[pl.BlockSpec((L, DK), lambda c, h: (rc(c), lax.div(h, rep))), pl.BlockSpec((L, DK), lambda c, h: (rc(c), HK + lax.div(h, rep))),
                  pl.BlockSpec((L, DV), lambda c, h: (rc(c), voff + h)), pl.BlockSpec((L, HV), lambda c, h: (rc(c), 0)),
                  pl.BlockSpec((L, HV), lambda c, h: (rc(c), 0)), par, par,
                  pl.BlockSpec((1, 1, DK, DV), lambda c, h: (rc(c), h, 0, 0)), blk(DV)],
        out_specs=[blk(DK), blk(DK), blk(DV), pl.BlockSpec((L, HV), lambda c, h: (rc(c), 0)),
                   pl.BlockSpec((L, HV), lambda c, h: (rc(c), 0)), par, par],
        out_shape=[jax.ShapeDtypeStruct((S, HV * DK), f32), jax.ShapeDtypeStruct((S, HV * DK), f32), jax.ShapeDtypeStruct((S, HV * DV), f32),
                   jax.ShapeDtypeStruct((S, HV), f32), jax.ShapeDtypeStruct((S, HV), f32),
                   jax.ShapeDtypeStruct((1, HV), f32), jax.ShapeDtypeStruct((1, HV), f32)],
        scratch_shapes=[pltpu.VMEM((HV, DK, DV), f32)],
        compiler_params=_cp(("arbitrary", "arbitrary")))(qkn, qkn, qkv, br, ar, alog, dtb, states, do)


def _att_scale():
    return (MLA_NOPE + MLA_ROPE) ** -0.5


def _causal(s, i, j, t):
    qpos = i * t + _iota(s.shape, 0)
    kpos = j * t + _iota(s.shape, 1)
    return kpos <= qpos


def _attn_fwd(qn, qr, kn, kr, v, *, name):
    S, W = qn.shape
    H = W // LANES
    t = _tile(S, ATT_TILE)
    scale = _att_scale()

    def body(qn_ref, qr_ref, kn_ref, kr_ref, v_ref, o_ref, lse_ref):
        i = pl.program_id(1)
        qnv, qrv = qn_ref[...], qr_ref[...]

        def step(j, carry):
            m, l, acc = carry
            rows = pl.ds(pl.multiple_of(j * t, t), t)
            s = (_dot(qnv, kn_ref[rows, :], _NT) + _dot(qrv, kr_ref[rows, :], _NT)) * scale
            s = jnp.where(_causal(s, i, j, t), s, -1e30)
            m_new = jnp.maximum(m, jnp.max(s, axis=1, keepdims=True))
            p = jnp.exp(s - m_new)
            a = jnp.exp(m - m_new)
            return m_new, a * l + jnp.sum(p, axis=1, keepdims=True), a * acc + _dot(p, v_ref[rows, :])

        m, l, acc = lax.fori_loop(0, i + 1, step, (jnp.full((t, 1), -1e30, f32), jnp.zeros((t, 1), f32), jnp.zeros((t, LANES), f32)))
        o_ref[...] = acc / l
        lse_ref[...] = jnp.broadcast_to(m + jnp.log(l), (t, LANES))

    qb = pl.BlockSpec((t, LANES), lambda h, i: (i, h))
    kb = pl.BlockSpec((S, LANES), lambda h, i: (0, h))
    return pl.pallas_call(
        body, name=name, grid=(H, S // t),
        in_specs=[qb, qb, kb, pl.BlockSpec((S, LANES), lambda h, i: (0, 0)), kb],
        out_specs=[qb, qb], out_shape=[jax.ShapeDtypeStruct((S, W), f32), jax.ShapeDtypeStruct((S, W), f32)],
        compiler_params=_cp(("parallel", "arbitrary")))(qn, qr, kn, kr, v)


def _attn_bwd_dq(qn, qr, kn, kr, v, o, lse, do, *, name):
    S, W = qn.shape
    H = W // LANES
    t = _tile(S, ATT_TILE)
    scale = _att_scale()

    def body(qn_ref, qr_ref, kn_ref, kr_ref, v_ref, o_ref, lse_ref, do_ref, dqn_ref, dqr_ref):
        i = pl.program_id(1)
        qnv, qrv, dov = qn_ref[...], qr_ref[...], do_ref[...]
        delta = jnp.sum(dov * o_ref[...], axis=1, keepdims=True)
        lsev = lse_ref[...][:, :1]

        def step(j, carry):
            dqn, dqr = carry
            rows = pl.ds(pl.multiple_of(j * t, t), t)
            knv, krv = kn_ref[rows, :], kr_ref[rows, :]
            s = (_dot(qnv, knv, _NT) + _dot(qrv, krv, _NT)) * scale
            p = jnp.where(_causal(s, i, j, t), jnp.exp(s - lsev), 0.0)
            ds = p * (_dot(dov, v_ref[rows, :], _NT) - delta) * scale
            return dqn + _dot(ds, knv), dqr + _dot(ds, krv)

        dqn, dqr = lax.fori_loop(0, i + 1, step, (jnp.zeros((t, LANES), f32), jnp.zeros((t, LANES), f32)))
        dqn_ref[...] = dqn
        dqr_ref[...] = dqr

    qb = pl.BlockSpec((t, LANES), lambda h, i: (i, h))
    kb = pl.BlockSpec((S, LANES), lambda h, i: (0, h))
    return pl.pallas_call(
        body, name=name, grid=(H, S // t),
        in_specs=[qb, qb, kb, pl.BlockSpec((S, LANES), lambda h, i: (0, 0)), kb, qb, qb, qb],
        out_specs=[qb, qb], out_shape=[jax.ShapeDtypeStruct((S, W), f32), jax.ShapeDtypeStruct((S, W), f32)],
        compiler_params=_cp(("parallel", "arbitrary")))(qn, qr, kn, kr, v, o, lse, do)


def _attn_bwd_dkv(qn, qr, kn, kr, v, o, lse, do, *, name):
    S, W = qn.shape
    H = W // LANES
    t = _tile(S, ATT_TILE)
    nb = S // t
    scale = _att_scale()

    def body(qn_ref, qr_ref, kn_ref, kr_ref, v_ref, o_ref, lse_ref, do_ref, dkn_ref, dkr_ref, dv_ref):
        j, h = pl.program_id(0), pl.program_id(1)
        knv, krv, vv = kn_ref[...], kr_ref[...], v_ref[...]

        def step(i, carry):
            dkn, dkr, dv = carry
            rows = pl.ds(pl.multiple_of(i * t, t), t)
            qnv, qrv, dov = qn_ref[rows, :], qr_ref[rows, :], do_ref[rows, :]
            delta = jnp.sum(dov * o_ref[rows, :], axis=1, keepdims=True)
            s = (_dot(qnv, knv, _NT) + _dot(qrv, krv, _NT)) * scale
            p = jnp.where(_causal(s, i, j, t), jnp.exp(s - lse_ref[rows, :][:, :1]), 0.0)
            ds = p * (_dot(dov, vv, _NT) - delta) * scale
            return dkn + _dot(ds, qnv, _TN), dkr + _dot(ds, qrv, _TN), dv + _dot(p, dov, _TN)

        z = jnp.zeros((t, LANES), f32)
        dkn, dkr, dv = lax.fori_loop(j, nb, step, (z, z, z))
        dkn_ref[...] = dkn
        dv_ref[...] = dv

        @pl.when(h == 0)
        def _():
            dkr_ref[...] = jnp.zeros_like(dkr_ref)

        dkr_ref[...] += dkr

    full = pl.BlockSpec((S, LANES), lambda j, h: (0, h))
    kb = pl.BlockSpec((t, LANES), lambda j, h: (j, h))
    k0 = pl.BlockSpec((t, LANES), lambda j, h: (j, 0))
    return pl.pallas_call(
        body, name=name, grid=(nb, H),
        in_specs=[full, full, kb, k0, kb, full, full, full],
        out_specs=[kb, k0, kb],
        out_shape=[jax.ShapeDtypeStruct((S, W), f32), jax.ShapeDtypeStruct((S, LANES), f32), jax.ShapeDtypeStruct((S, W), f32)],
        compiler_params=_cp(("arbitrary", "arbitrary")))(qn, qr, kn, kr, v, o, lse, do)


def _loss_head(y, target, *, name):
    S, D = y.shape
    tm = _tile(S, ROW_TILE, 8)

    def body(y_ref, t_ref, loss_ref, dy_ref):
        @pl.when(pl.program_id(0) == 0)
        def _():
            loss_ref[...] = jnp.zeros_like(loss_ref)

        e = y_ref[...] - t_ref[...]
        dy_ref[...] = e / D
        part = 0.5 * jnp.sum(jnp.mean(e * e, axis=1, keepdims=True), axis=0, keepdims=True)
        loss_ref[...] += jnp.broadcast_to(part, loss_ref.shape)

    rb = pl.BlockSpec((tm, D), lambda r: (r, 0))
    return pl.pallas_call(
        body, name=name, grid=(S // tm,), in_specs=[rb, rb],
        out_specs=[pl.BlockSpec((1, LANES), lambda r: (0, 0)), rb],
        out_shape=[jax.ShapeDtypeStruct((1, LANES), f32), jax.ShapeDtypeStruct((S, D), f32)],
        compiler_params=_cp(("arbitrary",)))(y, target)


def _adamw(w, g, m, v, *, name):
    R, C = w.shape
    tm = _tile(R, max(8, (1 << 19) // max(C, 1) // 8 * 8), 8)

    def body(w_ref, g_ref, m_ref, v_ref, d_ref, nm_ref, nv_ref):
        gv = g_ref[...]
        nm = ADAM_B1 * m_ref[...] + (1.0 - ADAM_B1) * gv
        nv = ADAM_B2 * v_ref[...] + (1.0 - ADAM_B2) * (gv * gv)
        m_hat = nm / (1.0 - ADAM_B1 ** ADAM_STEP)
        v_hat = nv / (1.0 - ADAM_B2 ** ADAM_STEP)
        d_ref[...] = -ADAM_LR * (m_hat / (jnp.sqrt(v_hat) + ADAM_EPS) + ADAM_WD * w_ref[...])
        nm_ref[...] = nm
        nv_ref[...] = nv

    rb = pl.BlockSpec((tm, C), lambda r: (r, 0))
    sh = jax.ShapeDtypeStruct((R, C), f32)
    return pl.pallas_call(body, name=name, grid=(R // tm,), in_specs=[rb] * 4, out_specs=[rb] * 3, out_shape=[sh] * 3,
                          compiler_params=_cp(("parallel",)))(w, g, m, v)


def _me():
    return lax.axis_index("x"), lax.axis_index("y"), lax.axis_index("c")


def _other_chips(mx, my):
    return [(1 - mx, my), (mx, 1 - my), (1 - mx, 1 - my)]


_ANY = pl.BlockSpec(memory_space=pl.ANY)


def _allgather_chips(xs, *, name):
    n = len(xs)
    halves = [x.shape[0] // 2 for x in xs]
    for x in xs:
        assert x.shape[0] % 2 == 0

    def body(*refs):
        x_refs, o_refs = refs[:n], refs[n:2 * n]
        send, recv, lsem = refs[2 * n:]
        mx, my, mc = _me()
        me = 2 * mx + my
        chips = _other_chips(mx, my)
        pending = []
        for t in range(n):
            hf = halves[t]
            mine = pl.ds(mc * hf, hf)
            local = pltpu.make_async_copy(x_refs[t], o_refs[t].at[me], lsem.at[t])
            local.start()
            pending.append(local)
            for j, (cx, cy) in enumerate(chips):
                cp = pltpu.make_async_remote_copy(x_refs[t].at[mine], o_refs[t].at[me, mine], send.at[t, j], recv.at[t, j],
                                                  device_id=(cx, cy, mc), device_id_type=MESH)
                cp.start()
                pending.append(cp)
        fwd = []
        for t in range(n):
            hf = halves[t]
            mine = pl.ds(mc * hf, hf)
            for j, (cx, cy) in enumerate(chips):
                k = 2 * cx + cy
                pltpu.make_async_remote_copy(x_refs[t].at[mine], o_refs[t].at[k, mine], send.at[t, j], recv.at[t, j],
                                             device_id=(cx, cy, mc), device_id_type=MESH).wait_recv()
                cp = pltpu.make_async_remote_copy(o_refs[t].at[k, mine], o_refs[t].at[k, mine], send.at[t, 3 + j], recv.at[t, 3 + j],
                                                  device_id=(mx, my, 1 - mc), device_id_type=MESH)
                cp.start()
                fwd.append(cp)
        for t in range(n):
            hf = halves[t]
            theirs = pl.ds((1 - mc) * hf, hf)
            for j, (cx, cy) in enumerate(chips):
                k = 2 * cx + cy
                pltpu.make_async_remote_copy(o_refs[t].at[k, theirs], o_refs[t].at[k, theirs], send.at[t, 3 + j], recv.at[t, 3 + j],
                                             device_id=(mx, my, 1 - mc), device_id_type=MESH).wait_recv()
        for t in range(n):
            pending[t * 4].wait()
            for j in range(3):
                pending[t * 4 + 1 + j].wait_send()
        for cp in fwd:
            cp.wait_send()

    return pl.pallas_call(
        body, name=name, in_specs=[_ANY] * n, out_specs=[_ANY] * n,
        out_shape=[jax.ShapeDtypeStruct((4,) + x.shape, x.dtype) for x in xs],
        scratch_shapes=[pltpu.SemaphoreType.DMA((n, 6)), pltpu.SemaphoreType.DMA((n, 6)), pltpu.SemaphoreType.DMA((n,))],
        compiler_params=pltpu.CompilerParams(has_side_effects=True))(*xs)


def _pair_send_halves(gs, *, name):
    n = len(gs)
    halves = [g.shape[1] // 2 for g in gs]

    def body(*refs):
        g_refs, o_refs = refs[:n], refs[n:2 * n]
        send, recv = refs[2 * n:]
        mx, my, mc = _me()
        cps = []
        for t in range(n):
            hf = halves[t]
            theirs = pl.ds((1 - mc) * hf, hf)
            for k in range(4):
                cp = pltpu.make_async_remote_copy(g_refs[t].at[k, theirs], o_refs[t].at[k], send.at[t, k], recv.at[t, k],
                                                  device_id=(mx, my, 1 - mc), device_id_type=MESH)
                cp.start()
                cps.append(cp)
        for cp in cps:
            cp.wait()

    return pl.pallas_call(
        body, name=name, in_specs=[_ANY] * n, out_specs=[_ANY] * n,
        out_shape=[jax.ShapeDtypeStruct((4, g.shape[1] // 2, g.shape[2]), g.dtype) for g in gs],
        scratch_shapes=[pltpu.SemaphoreType.DMA((n, 4)), pltpu.SemaphoreType.DMA((n, 4))],
        compiler_params=pltpu.CompilerParams(has_side_effects=True))(*gs)


def _chip_scatter(ps, *, name):
    n = len(ps)

    def body(*refs):
        p_refs, o_refs = refs[:n], refs[n:2 * n]
        send, recv = refs[2 * n:]
        mx, my, mc = _me()
        chips = _other_chips(mx, my)
        cps = []
        for t in range(n):
            for j, (cx, cy) in enumerate(chips):
                cp = pltpu.make_async_remote_copy(p_refs[t].at[2 * cx + cy], o_refs[t].at[j], send.at[t, j], recv.at[t, j],
                                                  device_id=(cx, cy, mc), device_id_type=MESH)
                cp.start()
                cps.append(cp)
        for cp in cps:
            cp.wait()

    return pl.pallas_call(
        body, name=name, in_specs=[_ANY] * n, out_specs=[_ANY] * n,
        out_shape=[jax.ShapeDtypeStruct((3,) + p.shape[1:], p.dtype) for p in ps],
        scratch_shapes=[pltpu.SemaphoreType.DMA((n, 3)), pltpu.SemaphoreType.DMA((n, 3))],
        compiler_params=pltpu.CompilerParams(has_side_effects=True))(*ps)


def _pair_exchange_halves(fs, *, name):
    n = len(fs)

    def body(*refs):
        f_refs, o_refs = refs[:n], refs[n:2 * n]
        send, recv, lsem = refs[2 * n:]
        mx, my, mc = _me()
        cps = []
        for t in range(n):
            hf = f_refs[t].shape[0]
            mine = pl.ds(mc * hf, hf)
            local = pltpu.make_async_copy(f_refs[t], o_refs[t].at[mine], lsem.at[t])
            local.start()
            cp = pltpu.make_async_remote_copy(f_refs[t], o_refs[t].at[mine], send.at[t], recv.at[t],
                                              device_id=(mx, my, 1 - mc), device_id_type=MESH)
            cp.start()
            cps.append((local, cp))
        for t in range(n):
            hf = f_refs[t].shape[0]
            theirs = pl.ds((1 - mc) * hf, hf)
            local, cp = cps[t]
            local.wait()
            cp.wait_send()
            pltpu.make_async_remote_copy(f_refs[t], o_refs[t].at[theirs], send.at[t], recv.at[t],
                                         device_id=(mx, my, 1 - mc), device_id_type=MESH).wait_recv()

    return pl.pallas_call(
        body, name=name, in_specs=[_ANY] * n, out_specs=[_ANY] * n,
        out_shape=[jax.ShapeDtypeStruct((2 * f.shape[0], f.shape[1]), f.dtype) for f in fs],
        scratch_shapes=[pltpu.SemaphoreType.DMA((n,)), pltpu.SemaphoreType.DMA((n,)), pltpu.SemaphoreType.DMA((n,))],
        compiler_params=pltpu.CompilerParams(has_side_effects=True))(*fs)


def _allgather_all(x, *, name):
    def body(x_ref, o_ref, send, recv, lsem):
        mx, my, mc = _me()
        me = 4 * mx + 2 * my + mc
        local = pltpu.make_async_copy(x_ref, o_ref.at[me], lsem)
        local.start()
        cps = []
        for j in range(1, 8):
            px, py, pc = mx ^ (j >> 2), my ^ ((j >> 1) & 1), mc ^ (j & 1)
            cp = pltpu.make_async_remote_copy(x_ref, o_ref.at[me], send.at[j - 1], recv.at[j - 1],
                                              device_id=(px, py, pc), device_id_type=MESH)
            cp.start()
            cps.append(cp)
        for j in range(1, 8):
            px, py, pc = mx ^ (j >> 2), my ^ ((j >> 1) & 1), mc ^ (j & 1)
            pltpu.make_async_remote_copy(x_ref, o_ref.at[4 * px + 2 * py + pc], send.at[j - 1], recv.at[j - 1],
                                         device_id=(px, py, pc), device_id_type=MESH).wait_recv()
        for cp in cps:
            cp.wait_send()
        local.wait()

    return pl.pallas_call(
        body, name=name, in_specs=[_ANY], out_specs=_ANY, out_shape=jax.ShapeDtypeStruct((8,) + x.shape, x.dtype),
        scratch_shapes=[pltpu.SemaphoreType.DMA((7,)), pltpu.SemaphoreType.DMA((7,)), pltpu.SemaphoreType.DMA],
        compiler_params=pltpu.CompilerParams(has_side_effects=True))(x)


def _add_half(g4, recv, mc, *, name):
    _, R, C = g4.shape
    hf = R // 2
    tm = _tile(hf, max(8, (1 << 19) // C // 8 * 8), 8)
    nb = hf // tm

    def body(mc_ref, g_ref, r_ref, o_ref):
        o_ref[...] = g_ref[...] + r_ref[...]

    return pl.pallas_call(
        body, name=name,
        grid_spec=pltpu.PrefetchScalarGridSpec(
            num_scalar_prefetch=1, grid=(4, nb),
            in_specs=[pl.BlockSpec((1, tm, C), lambda k, i, mc_ref: (k, mc_ref[0] * nb + i, 0)),
                      pl.BlockSpec((1, tm, C), lambda k, i, mc_ref: (k, i, 0))],
            out_specs=pl.BlockSpec((1, tm, C), lambda k, i, mc_ref: (k, i, 0))),
        out_shape=jax.ShapeDtypeStruct((4, hf, C), f32),
        compiler_params=_cp(("parallel", "parallel")))(mc, g4, recv)


def _sum_chips(p4, recv3, me, *, name):
    _, Rh, C = p4.shape
    tm = _tile(Rh, max(8, (1 << 19) // C // 8 * 8), 8)

    def body(me_ref, p_ref, r_ref, o_ref):
        o_ref[...] = ((p_ref[0] + r_ref[0]) + r_ref[1]) + r_ref[2]

    return pl.pallas_call(
        body, name=name,
        grid_spec=pltpu.PrefetchScalarGridSpec(
            num_scalar_prefetch=1, grid=(Rh // tm,),
            in_specs=[pl.BlockSpec((1, tm, C), lambda i, me_ref: (me_ref[0], i, 0)),
                      pl.BlockSpec((3, tm, C), lambda i, me_ref: (0, i, 0))],
            out_specs=pl.BlockSpec((tm, C), lambda i, me_ref: (i, 0))),
        out_shape=jax.ShapeDtypeStruct((Rh, C), f32),
        compiler_params=_cp(("parallel",)))(me, p4, recv3)


def _sum8(x8, *, name):
    _, R, C = x8.shape
    tm = _tile(R, 64, 8)

    def body(x_ref, o_ref):
        acc = x_ref[0]
        for k in range(1, 8):
            acc = acc + x_ref[k]
        o_ref[...] = acc

    return pl.pallas_call(body, name=name, grid=(R // tm,), in_specs=[pl.BlockSpec((8, tm, C), lambda i: (0, i, 0))],
                          out_specs=pl.BlockSpec((tm, C), lambda i: (i, 0)), out_shape=jax.ShapeDtypeStruct((R, C), f32),
                          compiler_params=_cp(("parallel",)))(x8)


def _ssd_layer_fwd(h, W, tag):
    z = _mm(h, W["wz"], name=tag + "_z")
    xp = _mm(h, W["wxbc"], name=tag + "_xbc")
    dtr = _mm(h, W["wdt"], name=tag + "_dt")
    xbc = _conv_fwd(xp, W["conv_w"], W["conv_b"], name=tag + "_conv")
    y, states = _ssd_scan_fwd(xbc, dtr, W["dt_bias"], W["a_log"], W["d"], name=tag + "_scan")
    DI = y.shape[1]
    G = SSD_N_GROUPS
    gs = DI // G
    (yn,) = _rowwise(_ssd_gate_fn, [(W["norm_w"], "c", gs)], [(y, "c", gs), (z, "c", gs)], [(DI, f32, "c", gs)],
                     name=tag + "_gate", ncol=G, tm=512)
    out = _mm(yn, W["wout"], name=tag + "_out")
    return out, dict(h=h, z=z, xp=xp, dtr=dtr, xbc=xbc, states=states, y=y, yn=yn)


def _ssd_layer_bwd(sv, W, dr, tag):
    h = sv["h"]
    DI = sv["y"].shape[1]
    G = SSD_N_GROUPS
    gs = DI // G
    gr = {}
    dyn = _mm(dr, W["wout"], tb=True, name=tag + "_dyn")
    gr["wout"] = _mm(sv["yn"], dr, ta=True, name=tag + "_dwout")
    (dnw,), (dy, dz) = _rowwise_bwd(_ssd_gate_fn, [(W["norm_w"], "c", gs)], [(sv["y"], "c", gs), (sv["z"], "c", gs)],
                                     [(dyn, "c", gs)], name=tag + "_dgate", ncol=G, tm=512)
    gr["norm_w"] = dnw
    dxs, dB, dC, ddtr, gr["dt_bias"], gr["a_log"], gr["d"] = _ssd_scan_bwd(
        sv["xbc"], sv["dtr"], W["dt_bias"], W["a_log"], W["d"], sv["states"], dy, name=tag + "_dscan")
    dxbc = jnp.concatenate([dxs, dB, dC], axis=1)
    dxp, gr["conv_w"], gr["conv_b"] = _conv_bwd(sv["xp"], W["conv_w"], W["conv_b"], dxbc, name=tag + "_dconv")
    dh = _mm(dz, W["wz"], tb=True, add=dr, add_scale=_alpha(), name=tag + "_dh1")
    dh = _mm(dxp, W["wxbc"], tb=True, add=dh, name=tag + "_dh2")
    dh = _mm(ddtr, W["wdt"], tb=True, add=dh, name=tag + "_dh3")
    gr["wz"] = _mm(h, dz, ta=True, name=tag + "_dwz")
    gr["wxbc"] = _mm(h, dxp, ta=True, name=tag + "_dwxbc")
    gr["wdt"] = _mm(h, ddtr, ta=True, name=tag + "_dwdt")
    return dh, gr


def _mla_layer_fwd(h, W, cos, sin, tag):
    QR, KR = W["wqc"].shape[1], W["wkvc"].shape[1]
    HW = W["wqn"].shape[1]
    H = HW // LANES
    qc = _mm(h, W["wqc"], name=tag + "_qc")
    kvc = _mm(h, W["wkvc"], name=tag + "_kvc")
    krp = _mm(h, W["wkr"], name=tag + "_krp")
    z = _mm(h, W["wz"], name=tag + "_z")
    (qcn,) = _rowwise(_rms_fn, [(W["q_norm"], "a", QR)], [(qc, "a", QR)], [(QR, f32, "a", QR)], name=tag + "_qnorm")
    (kvn,) = _rowwise(_rms_fn, [(W["kv_norm"], "a", KR)], [(kvc, "a", KR)], [(KR, f32, "a", KR)], name=tag + "_kvnorm")
    qn = _mm(qcn, W["wqn"], name=tag + "_qn")
    qrp = _mm(qcn, W["wqr"], name=tag + "_qrp")
    kn = _mm(kvn, W["wkn"], name=tag + "_kn")
    v = _mm(kvn, W["wv"], name=tag + "_v")
    (qr,) = _rowwise(_rope_fn, [], [(cos, "a", LANES), (sin, "a", LANES), (qrp, "c", LANES)], [(HW, f32, "c", LANES)],
                     name=tag + "_qrope", ncol=H)
    (kr,) = _rowwise(_rope_fn, [], [(cos, "a", LANES), (sin, "a", LANES), (krp, "a", LANES)], [(LANES, f32, "a", LANES)],
                     name=tag + "_krope")
    o, lse = _attn_fwd(qn, qr, kn, kr, v, name=tag + "_attn")
    (og,) = _rowwise(_mul_silu_fn, [], [(o, "a", HW), (z, "a", HW)], [(HW, f32, "a", HW)], name=tag + "_ogate")
    out = _mm(og, W["wout"], name=tag + "_out")
    return out, dict(h=h, qc=qc, kvc=kvc, z=z, qcn=qcn, kvn=kvn, qn=qn, qr=qr, kn=kn, kr=kr, v=v, o=o, lse=lse, og=og)


def _mla_layer_bwd(sv, W, cos, sin, dr, tag):
    h = sv["h"]
    QR, KR = W["wqc"].shape[1], W["wkvc"].shape[1]
    HW = W["wqn"].shape[1]
    H = HW // LANES
    gr = {}
    dog = _mm(dr, W["wout"], tb=True, name=tag + "_dog")
    gr["wout"] = _mm(sv["og"], dr, ta=True, name=tag + "_dwout")
    _, (do, dz) = _rowwise_bwd(_mul_silu_fn, [], [(sv["o"], "a", HW), (sv["z"], "a", HW)], [(dog, "a", HW)], name=tag + "_dogate")
    att = (sv["qn"], sv["qr"], sv["kn"], sv["kr"], sv["v"], sv["o"], sv["lse"], do)
    dqn, dqr = _attn_bwd_dq(*att, name=tag + "_dq")
    dkn, dkr, dv = _attn_bwd_dkv(*att, name=tag + "_dkv")
    _, (dqrp,) = _rowwise_bwd(_rope_fn, [], [(cos, "a", LANES), (sin, "a", LANES), (dqr, "c", LANES)], [(dqr, "c", LANES)],
                              name=tag + "_dqrope", ncol=H, diff_i=[2])
    _, (dkrp,) = _rowwise_bwd(_rope_fn, [], [(cos, "a", LANES), (sin, "a", LANES), (dkr, "a", LANES)], [(dkr, "a", LANES)],
                              name=tag + "_dkrope", diff_i=[2])
    dqcn = _mm(dqn, W["wqn"], tb=True, name=tag + "_dqcn1")
    dqcn = _mm(dqrp, W["wqr"], tb=True, add=dqcn, name=tag + "_dqcn2")
    dkvn = _mm(dkn, W["wkn"], tb=True, name=tag + "_dkvn1")
    dkvn = _mm(dv, W["wv"], tb=True, add=dkvn, name=tag + "_dkvn2")
    gr["wqn"] = _mm(sv["qcn"], dqn, ta=True, name=tag + "_dwqn")
    gr["wqr"] = _mm(sv["qcn"], dqrp, ta=True, name=tag + "_dwqr")
    gr["wkn"] = _mm(sv["kvn"], dkn, ta=True, name=tag + "_dwkn")
    gr["wv"] = _mm(sv["kvn"], dv, ta=True, name=tag + "_dwv")
    (gr["q_norm"],), (dqc,) = _rowwise_bwd(_rms_fn, [(W["q_norm"], "a", QR)], [(sv["qc"], "a", QR)], [(dqcn, "a", QR)], name=tag + "_dqnorm")
    (gr["kv_norm"],), (dkvc,) = _rowwise_bwd(_rms_fn, [(W["kv_norm"], "a", KR)], [(sv["kvc"], "a", KR)], [(dkvn, "a", KR)], name=tag + "_dkvnorm")
    dh = _mm(dz, W["wz"], tb=True, add=dr, add_scale=_alpha(), name=tag + "_dh1")
    dh = _mm(dqc, W["wqc"], tb=True, add=dh, name=tag + "_dh2")
    dh = _mm(dkvc, W["wkvc"], tb=True, add=dh, name=tag + "_dh3")
    dh = _mm(dkrp, W["wkr"], tb=True, add=dh, name=tag + "_dh4")
    gr["wz"] = _mm(h, dz, ta=True, name=tag + "_dwz")
    gr["wqc"] = _mm(h, dqc, ta=True, name=tag + "_dwqc")
    gr["wkvc"] = _mm(h, dkvc, ta=True, name=tag + "_dwkvc")
    gr["wkr"] = _mm(h, dkrp, ta=True, name=tag + "_dwkr")
    return dh, gr


def _gdn_layer_fwd(h, W, tag):
    HK, HV, DK, DV = GDN_N_QK_HEADS, GDN_N_V_HEADS, GDN_DK, GDN_DV
    KD, VD = HK * DK, HV * DV
    qkvp = _mm(h, W["wqkv"], name=tag + "_qkv")
    z = _mm(h, W["wz"], name=tag + "_z")
    br = _mm(h, W["wb"], name=tag + "_b")
    ar = _mm(h, W["wa"], name=tag + "_a")
    qkv = _conv_fwd(qkvp, W["conv_w"], jnp.zeros((1, qkvp.shape[1]), f32), name=tag + "_conv")
    scale = jnp.concatenate([jnp.full((1, KD), DK ** -0.5, f32), jnp.ones((1, KD), f32)], axis=1)
    (qkn,) = _rowwise(_l2_fn, [(scale, "c", DK)], [(qkv, "c", DK)], [(2 * KD, f32, "c", DK)], name=tag + "_l2", ncol=2 * HK)
    o, states = _gdn_scan_fwd(qkn, qkv, br, ar, W["a_log"], W["dt_bias"], name=tag + "_scan")
    (on,) = _rowwise(_gdn_gate_fn, [(W["norm_w"], "a", DV)], [(o, "c", DV), (z, "c", DV)], [(VD, f32, "c", DV)],
                     name=tag + "_gate", ncol=HV, tm=1024)
    out = _mm(on, W["wout"], name=tag + "_out")
    return out, dict(h=h, qkvp=qkvp, z=z, br=br, ar=ar, qkv=qkv, qkn=qkn, o=o, states=states, on=on, scale=scale)


def _gdn_layer_bwd(sv, W, dr, tag):
    h = sv["h"]
    HK, HV, DK, DV = GDN_N_QK_HEADS, GDN_N_V_HEADS, GDN_DK, GDN_DV
    KD, VD = HK * DK, HV * DV
    rep = HV // HK
    S = h.shape[0]
    gr = {}
    don = _mm(dr, W["wout"], tb=True, name=tag + "_don")
    gr["wout"] = _mm(sv["on"], dr, ta=True, name=tag + "_dwout")
    (gr["norm_w"],), (do, dz) = _rowwise_bwd(_gdn_gate_fn, [(W["norm_w"], "a", DV)], [(sv["o"], "c", DV), (sv["z"], "c", DV)],
                                              [(don, "c", DV)], name=tag + "_dgate", ncol=HV, tm=1024)
    dqh, dkh, dv, dbr, dar, gr["a_log"], gr["dt_bias"] = _gdn_scan_bwd(
        sv["qkn"], sv["qkv"], sv["br"], sv["ar"], W["a_log"], W["dt_bias"], sv["states"], do, name=tag + "_dscan")
    def headsum(xv):
        acc = xv[:, :DK]
        for r in range(1, rep):
            acc = acc + xv[:, r * DK:(r + 1) * DK]
        return (acc,)

    dqkn_parts = []
    for arr, nm in ((dqh, "_dqsum"), (dkh, "_dksum")):
        (sm,) = _rowwise(headsum, [], [(arr, "c", rep * DK)], [(KD, f32, "c", DK)], name=tag + nm, ncol=HK)
        dqkn_parts.append(sm)
    dqkn = jnp.concatenate(dqkn_parts, axis=1)
    _, (dqk,) = _rowwise_bwd(_l2_fn, [(sv["scale"], "c", DK)], [(sv["qkv"], "c", DK)], [(dqkn, "c", DK)],
                             name=tag + "_dl2", ncol=2 * HK, diff_p=[])
    dqkv = jnp.concatenate([dqk, dv], axis=1)
    dqkvp, gr["conv_w"], _ = _conv_bwd(sv["qkvp"], W["conv_w"], jnp.zeros((1, dqkv.shape[1]), f32), dqkv, name=tag + "_dconv")
    dh = _mm(dz, W["wz"], tb=True, add=dr, add_scale=_alpha(), name=tag + "_dh1")
    dh = _mm(dqkvp, W["wqkv"], tb=True, add=dh, name=tag + "_dh2")
    dh = _mm(dbr, W["wb"], tb=True, add=dh, name=tag + "_dh3")
    dh = _mm(dar, W["wa"], tb=True, add=dh, name=tag + "_dh4")
    gr["wz"] = _mm(h, dz, ta=True, name=tag + "_dwz")
    gr["wqkv"] = _mm(h, dqkvp, ta=True, name=tag + "_dwqkv")
    gr["wb"] = _mm(h, dbr, ta=True, name=tag + "_dwb")
    gr["wa"] = _mm(h, dar, ta=True, name=tag + "_dwa")
    return dh, gr


def _rope_tables(positions):
    half = MLA_ROPE // 2
    inv_freq = ROPE_THETA ** (-jnp.arange(0, MLA_ROPE, 2, dtype=f32) / MLA_ROPE)
    ang = positions.astype(f32)[:, None] * inv_freq
    cos, sin = jnp.cos(ang), jnp.sin(ang)
    S = positions.shape[0]
    pad = jnp.zeros((S, LANES - 2 * half), f32)
    return jnp.concatenate([cos, cos, pad + 1.0], axis=1), jnp.concatenate([sin, sin, pad], axis=1)


def _local_step(x, positions, target, LW, ln_g, ln_b):
    cos, sin = _rope_tables(positions)
    h = x
    saved = []
    for i in range(DEPTH):
        kind, tag = i % 3, "l%d" % i
        if kind == 0:
            y, sv = _ssd_layer_fwd(h, LW[i], tag)
        elif kind == 1:
            y, sv = _mla_layer_fwd(h, LW[i], cos, sin, tag)
        else:
            y, sv = _gdn_layer_fwd(h, LW[i], tag)
        D = h.shape[1]
        r, hn = _rowwise(_res_ln_fn, [(ln_g[i], "a", D), (ln_b[i], "a", D)], [(h, "a", D), (y, "a", D)],
                         [(D, f32, "a", D), (D, f32, "a", D)], name=tag + "_ln")
        sv["r"] = r
        saved.append(sv)
        h = hn
    loss, dh = _loss_head(h, target, name="loss_head")
    grads, dg, db = [None] * DEPTH, [None] * DEPTH, [None] * DEPTH
    for i in reversed(range(DEPTH)):
        kind, tag = i % 3, "l%d" % i
        sv = saved[i]
        D = dh.shape[1]
        (dg[i], db[i]), (dr,) = _rowwise_bwd(_ln_fn, [(ln_g[i], "a", D), (ln_b[i], "a", D)], [(sv["r"], "a", D)], [(dh, "a", D)],
                                             name=tag + "_dln")
        if kind == 0:
            dh, grads[i] = _ssd_layer_bwd(sv, LW[i], dr, tag)
        elif kind == 1:
            dh, grads[i] = _mla_layer_bwd(sv, LW[i], cos, sin, dr, tag)
        else:
            dh, grads[i] = _gdn_layer_bwd(sv, LW[i], dr, tag)
    return loss, dh, grads, dg, db


_WEIGHTS = ["ssd_in_w", "ssd_conv_w", "ssd_conv_b", "ssd_dt_bias", "ssd_a_log", "ssd_d", "ssd_norm_w", "ssd_out_w",
            "mla_in_w", "mla_q_norm_w", "mla_q_up_w", "mla_kv_norm_w", "mla_kv_up_w", "mla_out_w",
            "gdn_in_w", "gdn_conv_w", "gdn_a_log", "gdn_dt_bias", "gdn_norm_w", "gdn_out_w", "ln_g", "ln_b"]
_BIG = {"ssd_in_w": "col", "ssd_out_w": "row", "mla_in_w": "col", "mla_q_up_w": "col", "mla_kv_up_w": "col",
        "mla_out_w": "row", "gdn_in_w": "col", "gdn_out_w": "row"}
_SMALL_SHARDED = ["ssd_conv_w", "ssd_conv_b", "ssd_norm_w", "gdn_conv_w"]
_PACK_ROWS = 16


def _gathered_to_full(g, kind, nl):
    if kind == "col":
        _, RK, Ns = g.shape
        return g.reshape(4, nl, RK // nl, Ns).transpose(1, 2, 0, 3).reshape(nl, RK // nl, 4 * Ns)
    _, RK, N = g.shape
    return g.reshape(4, nl, RK // nl, N).transpose(1, 0, 2, 3).reshape(nl, 4 * (RK // nl), N)


def _full_to_slots(f, kind):
    nl, K, N = f.shape
    if kind == "col":
        return f.reshape(nl, K, 4, N // 4).transpose(2, 0, 1, 3).reshape(4, nl * K, N // 4)
    return f.reshape(nl, 4, K // 4, N).transpose(1, 0, 2, 3).reshape(4, nl * (K // 4), N)


def _pack(arrs):
    flat = jnp.concatenate([a.reshape(-1).astype(f32) for a in arrs])
    unit = _PACK_ROWS * LANES
    n = -(-flat.shape[0] // unit) * unit
    return jnp.pad(flat, (0, n - flat.shape[0])).reshape(_PACK_ROWS, n // _PACK_ROWS)


def _unpack(packed, shapes):
    flat = packed.reshape(-1)
    out, off = [], 0
    for sh in shapes:
        n = math.prod(sh)
        out.append(flat[off:off + n].reshape(sh))
        off += n
    return out


def _pad_lanes(a):
    return jnp.pad(a, [(0, 0)] * (a.ndim - 1) + [(0, LANES - a.shape[-1])])


def _layer_weights(full, D):
    G, N, P = SSD_N_GROUPS, SSD_D_STATE, SSD_HEAD_DIM
    LW = []
    for i in range(DEPTH):
        kind, j = i % 3, i // 3
        if kind == 0:
            H = full["ssd_dt_bias"].shape[1]
            DI = H * P
            CD = DI + 2 * G * N
            win = full["ssd_in_w"][j]
            LW.append(dict(wz=win[:, :DI], wxbc=win[:, DI:DI + CD], wdt=win[:, DI + CD:], conv_w=full["ssd_conv_w"][j],
                           conv_b=full["ssd_conv_b"][j][None], dt_bias=full["ssd_dt_bias"][j][None], a_log=full["ssd_a_log"][j][None],
                           d=full["ssd_d"][j][None], norm_w=full["ssd_norm_w"][j][None], wout=full["ssd_out_w"][j]))
        elif kind == 1:
            QR, KR = MLA_Q_RANK, MLA_KV_RANK
            win = full["mla_in_w"][j]
            Hh = full["mla_q_up_w"].shape[2] // (MLA_NOPE + MLA_ROPE)
            qup = full["mla_q_up_w"][j].reshape(QR, Hh, MLA_NOPE + MLA_ROPE)
            kvup = full["mla_kv_up_w"][j].reshape(KR, Hh, MLA_NOPE + MLA_V)
            LW.append(dict(wqc=win[:, :QR], wkvc=win[:, QR:QR + KR], wkr=_pad_lanes(win[:, QR + KR:QR + KR + MLA_ROPE]),
                           wz=win[:, QR + KR + MLA_ROPE:], q_norm=full["mla_q_norm_w"][j][None], kv_norm=full["mla_kv_norm_w"][j][None],
                           wqn=qup[:, :, :MLA_NOPE].reshape(QR, Hh * MLA_NOPE), wqr=_pad_lanes(qup[:, :, MLA_NOPE:]).reshape(QR, Hh * LANES),
                           wkn=kvup[:, :, :MLA_NOPE].reshape(KR, Hh * MLA_NOPE), wv=kvup[:, :, MLA_NOPE:].reshape(KR, Hh * MLA_V),
                           wout=full["mla_out_w"][j]))
        else:
            KD, VD, HV = GDN_N_QK_HEADS * GDN_DK, GDN_N_V_HEADS * GDN_DV, GDN_N_V_HEADS
            win = full["gdn_in_w"][j]
            c0, c1 = 2 * KD + VD, 2 * KD + 2 * VD
            LW.append(dict(wqkv=win[:, :c0], wz=win[:, c0:c1], wb=win[:, c1:c1 + HV], wa=win[:, c1 + HV:], conv_w=full["gdn_conv_w"][j],
                           a_log=full["gdn_a_log"][j][None], dt_bias=full["gdn_dt_bias"][j][None], norm_w=full["gdn_norm_w"][j][None],
                           wout=full["gdn_out_w"][j]))
    return LW


def _full_grads(grads, dg, db):
    per = {n: [] for n in _WEIGHTS}
    for i in range(DEPTH):
        kind, g = i % 3, grads[i]
        if kind == 0:
            per["ssd_in_w"].append(jnp.concatenate([g["wz"], g["wxbc"], g["wdt"]], axis=1))
            per["ssd_conv_w"].append(g["conv_w"])
            for n in ("conv_b", "dt_bias", "a_log", "d", "norm_w"):
                per["ssd_" + n].append(g[n][0])
            per["ssd_out_w"].append(g["wout"])
        elif kind == 1:
            QR, KR = g["wqn"].shape[0], g["wkn"].shape[0]
            Hh = g["wqn"].shape[1] // MLA_NOPE
            per["mla_in_w"].append(jnp.concatenate([g["wqc"], g["wkvc"], g["wkr"][:, :MLA_ROPE], g["wz"]], axis=1))
            per["mla_q_up_w"].append(jnp.concatenate(
                [g["wqn"].reshape(QR, Hh, MLA_NOPE), g["wqr"].reshape(QR, Hh, LANES)[:, :, :MLA_ROPE]], axis=2).reshape(QR, -1))
            per["mla_kv_up_w"].append(jnp.concatenate(
                [g["wkn"].reshape(KR, Hh, MLA_NOPE), g["wv"].reshape(KR, Hh, MLA_V)], axis=2).reshape(KR, -1))
            per["mla_q_norm_w"].append(g["q_norm"][0])
            per["mla_kv_norm_w"].append(g["kv_norm"][0])
            per["mla_out_w"].append(g["wout"])
        else:
            per["gdn_in_w"].append(jnp.concatenate([g["wqkv"], g["wz"], g["wb"], g["wa"]], axis=1))
            per["gdn_conv_w"].append(g["conv_w"])
            for n in ("a_log", "dt_bias", "norm_w"):
                per["gdn_" + n].append(g[n][0])
            per["gdn_out_w"].append(g["wout"])
        per["ln_g"].append(dg[i][0])
        per["ln_b"].append(db[i][0])
    return {n: jnp.stack(v) for n, v in per.items()}


def kernel(x, positions, ssd_in_w, ssd_conv_w, ssd_conv_b, ssd_dt_bias, ssd_a_log, ssd_d, ssd_norm_w, ssd_out_w, mla_in_w, mla_q_norm_w, mla_q_up_w, mla_kv_norm_w, mla_kv_up_w, mla_out_w, gdn_in_w, gdn_conv_w, gdn_a_log, gdn_dt_bias, gdn_norm_w, gdn_out_w, ln_g, ln_b, loss_target, m_ssd_in_w, m_ssd_conv_w, m_ssd_conv_b, m_ssd_dt_bias, m_ssd_a_log, m_ssd_d, m_ssd_norm_w, m_ssd_out_w, m_mla_in_w, m_mla_q_norm_w, m_mla_q_up_w, m_mla_kv_norm_w, m_mla_kv_up_w, m_mla_out_w, m_gdn_in_w, m_gdn_conv_w, m_gdn_a_log, m_gdn_dt_bias, m_gdn_norm_w, m_gdn_out_w, m_ln_g, m_ln_b, v_ssd_in_w, v_ssd_conv_w, v_ssd_conv_b, v_ssd_dt_bias, v_ssd_a_log, v_ssd_d, v_ssd_norm_w, v_ssd_out_w, v_mla_in_w, v_mla_q_norm_w, v_mla_q_up_w, v_mla_kv_norm_w, v_mla_kv_up_w, v_mla_out_w, v_gdn_in_w, v_gdn_conv_w, v_gdn_a_log, v_gdn_dt_bias, v_gdn_norm_w, v_gdn_out_w, v_ln_g, v_ln_b):
    args = dict(locals())
    w = {n: args[n] for n in _WEIGHTS}
    mom = {n: args["m_" + n] for n in _WEIGHTS}
    vel = {n: args["v_" + n] for n in _WEIGHTS}
    D = x.shape[-1]
    mx, my, mc = _me()
    chip = 2 * mx + my
    chip_arr = jnp.reshape(chip, (1,)).astype(jnp.int32)
    core_arr = jnp.reshape(mc, (1,)).astype(jnp.int32)
    small = [n for n in _WEIGHTS if n not in _BIG]

    big = list(_BIG)
    send = [w[n].astype(MXU_DT).reshape(-1, w[n].shape[-1]) for n in big] + [_pack([w[n] for n in _SMALL_SHARDED])]
    got = _allgather_chips(send, name="gather_weights")
    full = {n: _gathered_to_full(g, _BIG[n], w[n].shape[0]) for n, g in zip(big, got[:-1])}
    parts = [_unpack(got[-1][k], [w[n].shape for n in _SMALL_SHARDED]) for k in range(4)]
    for t, n in enumerate(_SMALL_SHARDED):
        full[n] = jnp.concatenate([parts[k][t] for k in range(4)], axis=-1)
    for n in small:
        if n not in full:
            full[n] = w[n]

    loss, gx, grads, dg, db = _local_step(x[0], positions[0], loss_target[0], _layer_weights(full, D), [full["ln_g"][i][None] for i in range(DEPTH)],
                                          [full["ln_b"][i][None] for i in range(DEPTH)])
    fg = _full_grads(grads, dg, db)

    g4 = [_full_to_slots(fg[n], _BIG[n]) for n in big]
    r1 = _pair_send_halves(g4, name="grad_pair_send")
    p4 = [_add_half(a, b, core_arr, name="grad_pair_add_" + n) for a, b, n in zip(g4, r1, big)]
    r2 = _chip_scatter(p4, name="grad_chip_scatter")
    fin = [_sum_chips(a, b, chip_arr, name="grad_chip_sum_" + n) for a, b, n in zip(p4, r2, big)]
    gsh = _pair_exchange_halves(fin, name="grad_pair_share")

    out_g, out_d, out_m, out_v = {}, {}, {}, {}
    for n, g in zip(big, gsh):
        sh = w[n].shape
        to2 = lambda a: a.reshape(-1, sh[-1])
        d_, m_, v_ = _adamw(to2(w[n]), g, to2(mom[n]), to2(vel[n]), name="adamw_" + n)
        out_g[n], out_d[n], out_m[n], out_v[n] = g.reshape(sh), d_.reshape(sh), m_.reshape(sh), v_.reshape(sh)

    summed = _sum8(_allgather_all(_pack([fg[n] for n in small] + [loss[0, :1]]), name="gather_small"), name="sum_small")
    sg = _unpack(summed, [fg[n].shape for n in small] + [(1,)])
    loss_total = sg[-1][0]
    gs = {}
    for n, g in zip(small, sg[:-1]):
        if n in _SMALL_SHARDED:
            ws = w[n].shape[-1]
            g = lax.dynamic_slice_in_dim(g, chip * ws, ws, axis=g.ndim - 1)
        gs[n] = g
    shapes = [w[n].shape for n in small]
    d_, m_, v_ = _adamw(_pack([w[n] for n in small]), _pack([gs[n] for n in small]), _pack([mom[n] for n in small]),
                        _pack([vel[n] for n in small]), name="adamw_small")
    for n, a, b, c in zip(small, _unpack(d_, shapes), _unpack(m_, shapes), _unpack(v_, shapes)):
        out_g[n], out_d[n], out_m[n], out_v[n] = gs[n], a, b, c

    return (loss_total, gx[None], *[out_g[n] for n in _WEIGHTS], *[out_d[n] for n in _WEIGHTS],
            *[out_m[n] for n in _WEIGHTS], *[out_v[n] for n in _WEIGHTS])
```

```python
import functools
import math

import jax
import jax.numpy as jnp
from jax import lax
from jax.experimental import pallas as pl
from jax.experimental.pallas import tpu as pltpu

f32 = jnp.float32
HI = lax.Precision.HIGHEST
MXU_DT = jnp.bfloat16
COMM_DT = jnp.bfloat16
MESH = pl.DeviceIdType.MESH

DEPTH = 4
LN_EPS = 1e-5
RMS_EPS = 1e-6
SSD_HEAD_DIM = 64
SSD_N_GROUPS = 8
SSD_D_STATE = 128
SSD_CONV = 4
SSD_CHUNK = 128
MLA_Q_RANK = 768
MLA_KV_RANK = 512
MLA_NOPE = 128
MLA_ROPE = 64
MLA_V = 128
ROPE_THETA = 10000.0
GDN_N_QK_HEADS = 16
GDN_N_V_HEADS = 32
GDN_DK = 128
GDN_DV = 128
GDN_CONV = 4
GDN_CHUNK = 64
ADAM_LR = 0.001
ADAM_B1 = 0.9
ADAM_B2 = 0.999
ADAM_EPS = 1e-08
ADAM_WD = 0.01
ADAM_STEP = 10

LANES = 128
VMEM_LIMIT = 48 * 1024 * 1024
ATT_TILE = 512
ROW_TILE = 256
MM_TILE_M = 1024
MM_TILE_N = 1024
MM_TILE_K = 2048
MM_VMEM_BUDGET = 40 * 1024 * 1024
GDN_HEADS_PER_STEP = 16


def _alpha():
    return (2.0 * DEPTH) ** 0.25


def _tile(n, pref, align=LANES):
    t = min(pref, n) // align * align
    while t >= align:
        if n % t == 0:
            return t
        t -= align
    return n


def _cp(sem=None):
    return pltpu.CompilerParams(dimension_semantics=sem, vmem_limit_bytes=VMEM_LIMIT)


def _iota(shape, dim):
    return lax.broadcasted_iota(jnp.int32, shape, dim)


def _div_pow2(x, p):
    assert p & (p - 1) == 0
    return lax.shift_right_logical(x, jnp.int32(p.bit_length() - 1))


def _dot(a, b, dims=((1,), (0,)), hi=False):
    if hi:
        return lax.dot_general(a.astype(f32), b.astype(f32), (dims, ((), ())), precision=HI, preferred_element_type=f32)
    return lax.dot_general(a.astype(MXU_DT), b.astype(MXU_DT), (dims, ((), ())), preferred_element_type=f32)


_NT = ((1,), (1,))
_TN = ((0,), (0,))


def _softplus(x):
    return jnp.maximum(x, 0.0) + jnp.log1p(jnp.exp(-jnp.abs(x)))


def _silu(x):
    return x * jax.nn.sigmoid(x)


def _mm(a, b, *, name, ta=False, tb=False, add=None, add_scale=1.0, out_dtype=f32):
    M, K = (a.shape[1], a.shape[0]) if ta else a.shape
    N = b.shape[0] if tb else b.shape[1]
    assert (b.shape[1] if tb else b.shape[0]) == K, (a.shape, b.shape, ta, tb)
    tm, tn, tk = _tile(M, MM_TILE_M), _tile(N, MM_TILE_N), _tile(K, MM_TILE_K)
    ab, bb = jnp.dtype(a.dtype).itemsize, jnp.dtype(b.dtype).itemsize
    while 2 * tk * (tm * ab + tn * bb) + 12 * tm * tn > MM_VMEM_BUDGET and tk % (2 * LANES) == 0:
        tk //= 2
    nk = K // tk
    a_spec = pl.BlockSpec((tk, tm), lambda i, j, k: (k, i)) if ta else pl.BlockSpec((tm, tk), lambda i, j, k: (i, k))
    b_spec = pl.BlockSpec((tn, tk), lambda i, j, k: (j, k)) if tb else pl.BlockSpec((tk, tn), lambda i, j, k: (k, j))
    o_spec = pl.BlockSpec((tm, tn), lambda i, j, k: (i, j))
    dims = ((0 if ta else 1,), (1 if tb else 0,))
    has_add = add is not None

    def body(*refs):
        a_ref, b_ref = refs[:2]
        add_ref = refs[2] if has_add else None
        o_ref = refs[3 if has_add else 2]

        def finish(r):
            if has_add:
                r = r + add_scale * add_ref[...].astype(f32)
            o_ref[...] = r.astype(out_dtype)

        if nk == 1:
            finish(_dot(a_ref[...], b_ref[...], dims))
            return
        acc = refs[-1]
        k = pl.program_id(2)

        @pl.when(k == 0)
        def _():
            acc[...] = jnp.zeros_like(acc)

        acc[...] += _dot(a_ref[...], b_ref[...], dims)

        @pl.when(k == nk - 1)
        def _():
            finish(acc[...])

    ins = [a, b] + ([add] if has_add else [])
    specs = [a_spec, b_spec] + ([o_spec] if has_add else [])
    return pl.pallas_call(
        body, name=name, grid=(M // tm, N // tn, nk), in_specs=specs, out_specs=o_spec,
        out_shape=jax.ShapeDtypeStruct((M, N), out_dtype), scratch_shapes=[pltpu.VMEM((tm, tn), f32)] if nk > 1 else [],
        compiler_params=_cp(("parallel", "parallel", "arbitrary")))(*ins)


def _rw_specs(params, ins, ncol, tm):
    specs = []
    for arr, mode, bw, coff in params:
        if mode == "c":
            specs.append(pl.BlockSpec((1, bw), lambda c, r, coff=coff: (0, c + coff)))
        else:
            specs.append(pl.BlockSpec((1, bw), lambda c, r, coff=coff: (0, coff)))
    for arr, mode, bw, coff in ins:
        if mode == "c":
            specs.append(pl.BlockSpec((tm, bw), lambda c, r, coff=coff: (r, c + coff)))
        else:
            specs.append(pl.BlockSpec((tm, bw), lambda c, r, coff=coff: (r, coff)))
    return specs


def _norm_spec(lst):
    out = []
    for t in lst:
        arr, mode, bw = t[0], t[1], t[2]
        coff = t[3] if len(t) > 3 else 0
        out.append((arr, mode, bw, coff))
    return out


def _rowwise(fn, params, ins, outs, *, name, ncol=1, tm=None):
    params, ins = _norm_spec(params), _norm_spec(ins)
    S = ins[0][0].shape[0]
    tm = _tile(S, tm or ROW_TILE, 8)
    npar, nin = len(params), len(ins)

    def body(*refs):
        pv = [r[...].astype(f32) for r in refs[:npar]]
        iv = [r[...].astype(f32) for r in refs[npar:npar + nin]]
        res = fn(*pv, *iv)
        for o_ref, val in zip(refs[npar + nin:], res):
            o_ref[...] = val.astype(o_ref.dtype)

    out_specs, out_shapes = [], []
    for W, dt, mode, bw in outs:
        out_shapes.append(jax.ShapeDtypeStruct((S, W), dt))
        if mode == "c":
            out_specs.append(pl.BlockSpec((tm, bw), lambda c, r: (r, c)))
        else:
            out_specs.append(pl.BlockSpec((tm, bw), lambda c, r: (r, 0)))
    return pl.pallas_call(
        body, name=name, grid=(ncol, S // tm), in_specs=_rw_specs(params, ins, ncol, tm), out_specs=out_specs,
        out_shape=out_shapes, compiler_params=_cp(("parallel", "parallel")))(*[p[0] for p in params], *[i[0] for i in ins])


def _rowwise_bwd(fn, params, ins, couts, *, name, ncol=1, tm=None, diff_p=None, diff_i=None, din_dtypes=None):
    params, ins, couts = _norm_spec(params), _norm_spec(ins), _norm_spec(couts)
    S = ins[0][0].shape[0]
    tm = _tile(S, tm or ROW_TILE, 8)
    npar, nin, nco = len(params), len(ins), len(couts)
    diff_p = list(range(npar)) if diff_p is None else diff_p
    diff_i = list(range(nin)) if diff_i is None else diff_i
    din_dtypes = [(f32,)] * len(diff_i) if din_dtypes is None else din_dtypes

    def body(*refs):
        c, r = pl.program_id(0), pl.program_id(1)
        pv = [x[...].astype(f32) for x in refs[:npar]]
        iv = [x[...].astype(f32) for x in refs[npar:npar + nin]]
        cv = [x[...].astype(f32) for x in refs[npar + nin:npar + nin + nco]]
        orefs = refs[npar + nin + nco:]

        def g(*dargs):
            p2, i2 = list(pv), list(iv)
            for n, k in enumerate(diff_p):
                p2[k] = dargs[n]
            for n, k in enumerate(diff_i):
                i2[k] = dargs[len(diff_p) + n]
            return tuple(fn(*p2, *i2))

        _, vjp = jax.vjp(g, *[pv[k] for k in diff_p], *[iv[k] for k in diff_i])
        grads = vjp(tuple(cv))
        for n, k in enumerate(diff_p):
            o_ref = orefs[n]
            first = (r == 0) if params[k][1] == "c" else jnp.logical_and(r == 0, c == 0)

            @pl.when(first)
            def _(o_ref=o_ref):
                o_ref[...] = jnp.zeros_like(o_ref)

            o_ref[...] += grads[n]
        pos = len(diff_p)
        for n, k in enumerate(diff_i):
            for _ in din_dtypes[n]:
                orefs[pos][...] = grads[len(diff_p) + n].astype(orefs[pos].dtype)
                pos += 1

    out_specs, out_shapes = [], []
    for k in diff_p:
        arr, mode, bw, coff = params[k]
        W = bw * ncol if mode == "c" else bw
        out_shapes.append(jax.ShapeDtypeStruct((1, W), f32))
        out_specs.append(pl.BlockSpec((1, bw), (lambda c, r: (0, c)) if mode == "c" else (lambda c, r: (0, 0))))
    for n, k in enumerate(diff_i):
        arr, mode, bw, coff = ins[k]
        W = bw * ncol if mode == "c" else bw
        for dt in din_dtypes[n]:
            out_shapes.append(jax.ShapeDtypeStruct((S, W), dt))
            out_specs.append(pl.BlockSpec((tm, bw), (lambda c, r: (r, c)) if mode == "c" else (lambda c, r: (r, 0))))
    res = pl.pallas_call(
        body, name=name, grid=(ncol, S // tm), in_specs=_rw_specs(params, ins + couts, ncol, tm), out_specs=out_specs,
        out_shape=out_shapes, compiler_params=_cp(("arbitrary", "arbitrary")))(
            *[p[0] for p in params], *[i[0] for i in ins], *[c[0] for c in couts])
    return list(res[:len(diff_p)]), list(res[len(diff_p):])


def _ln_fn(g, b, r):
    mu = jnp.mean(r, -1, keepdims=True)
    xc = r - mu
    var = jnp.mean(xc * xc, -1, keepdims=True)
    return (xc * lax.rsqrt(var + LN_EPS) * g + b,)


def _res_ln_fn(g, b, h, y):
    r = _alpha() * h + y
    hn = _ln_fn(g, b, r)
    return (r,) + hn + hn


def _rms_fn(w, x):
    return (x * lax.rsqrt(jnp.mean(x * x, -1, keepdims=True) + RMS_EPS) * w,)


def _ssd_gate_fn(w, y, z):
    yg = y * _silu(z)
    return (yg * lax.rsqrt(jnp.mean(yg * yg, -1, keepdims=True) + RMS_EPS) * w,)


def _mul_silu_fn(o, z):
    return (o * _silu(z),)


def _gdn_gate_fn(w, o, z):
    return (o * lax.rsqrt(jnp.mean(o * o, -1, keepdims=True) + RMS_EPS) * w * _silu(z),)


def _l2_fn(scale, x):
    return (x * lax.rsqrt(jnp.sum(x * x, -1, keepdims=True) + RMS_EPS) * scale,)


def _rope_fn(cos, sin, x):
    half = MLA_ROPE // 2
    i = _iota((LANES, LANES), 0)
    j = _iota((LANES, LANES), 1)
    pm = jnp.where((i == j + half) & (j < half), -1.0, 0.0) + jnp.where((i + half == j) & (j < 2 * half), 1.0, 0.0)
    return (x * cos + _dot(x, pm.astype(f32), hi=True) * sin,)


def _conv_taps(x, K):
    S = x.shape[0]
    rows = _iota(x.shape, 0)
    return [x] + [jnp.where(rows < j, 0.0, pltpu.roll(x, j, 0)) for j in range(1, K)]


def _conv_fwd(x, w, b, *, name):
    S, C = x.shape
    K = w.shape[0]
    cw = _tile(C, LANES)

    def body(x_ref, w_ref, b_ref, o_ref):
        taps = _conv_taps(x_ref[...], K)
        wv = w_ref[...]
        pre = b_ref[...] + taps[0] * wv[K - 1:K, :]
        for j in range(1, K):
            pre = pre + taps[j] * wv[K - 1 - j:K - j, :]
        o_ref[...] = _silu(pre)

    return pl.pallas_call(
        body, name=name, grid=(C // cw,),
        in_specs=[pl.BlockSpec((S, cw), lambda c: (0, c)), pl.BlockSpec((K, cw), lambda c: (0, c)), pl.BlockSpec((1, cw), lambda c: (0, c))],
        out_specs=pl.BlockSpec((S, cw), lambda c: (0, c)), out_shape=jax.ShapeDtypeStruct((S, C), f32),
        compiler_params=_cp(("parallel",)))(x, w, b)


def _conv_bwd(x, w, b, dy, *, name, dx_dtype=f32):
    S, C = x.shape
    K = w.shape[0]
    cw = _tile(C, LANES)

    def body(x_ref, w_ref, b_ref, dy_ref, dx_ref, dw_ref, db_ref):
        taps = _conv_taps(x_ref[...], K)
        wv = w_ref[...]
        pre = b_ref[...] + taps[0] * wv[K - 1:K, :]
        for j in range(1, K):
            pre = pre + taps[j] * wv[K - 1 - j:K - j, :]
        sg = jax.nn.sigmoid(pre)
        dpre = dy_ref[...] * sg * (1.0 + pre * (1.0 - sg))
        db_ref[...] = jnp.sum(dpre, axis=0, keepdims=True)
        rows = _iota(dpre.shape, 0)
        dx = dpre * wv[K - 1:K, :]
        dw_ref[K - 1:K, :] = jnp.sum(dpre * taps[0], axis=0, keepdims=True)
        for j in range(1, K):
            dw_ref[K - 1 - j:K - j, :] = jnp.sum(dpre * taps[j], axis=0, keepdims=True)
            up = jnp.where(rows >= S - j, 0.0, pltpu.roll(dpre, S - j, 0))
            dx = dx + up * wv[K - 1 - j:K - j, :]
        dx_ref[...] = dx.astype(dx_dtype)

    col = lambda c: (0, c)
    return pl.pallas_call(
        body, name=name, grid=(C // cw,),
        in_specs=[pl.BlockSpec((S, cw), col), pl.BlockSpec((K, cw), col), pl.BlockSpec((1, cw), col), pl.BlockSpec((S, cw), col)],
        out_specs=[pl.BlockSpec((S, cw), col), pl.BlockSpec((K, cw), col), pl.BlockSpec((1, cw), col)],
        out_shape=[jax.ShapeDtypeStruct((S, C), dx_dtype), jax.ShapeDtypeStruct((K, C), f32), jax.ShapeDtypeStruct((1, C), f32)],
        compiler_params=_cp(("parallel",)))(x, w, b, dy)


def _ssd_chunk(prev, xs, Bm, Cm, dtr, dtb, alog, dsk, g, *, R, P):
    L, GW = xs.shape
    H = dtr.shape[1]
    tril = _iota((L, L), 0) >= _iota((L, L), 1)
    dt = _softplus(dtr + dtb)
    acs = _dot(tril.astype(f32), dt * (-jnp.exp(alog)), hi=True)
    expand = (_iota((H, GW), 0) == g * R + _div_pow2(_iota((H, GW), 1), P)).astype(f32)
    dt_e = _dot(dt, expand, hi=True)
    acs_e = _dot(acs, expand, hi=True)
    d_e = _dot(jnp.broadcast_to(dsk, (L, H)), expand, hi=True)
    last = jnp.sum(jnp.where(_iota((L, GW), 0) == L - 1, acs_e, 0.0), axis=0, keepdims=True)
    xdt = xs * dt_e
    cb = _dot(Cm, Bm, _NT)
    nsel = max(R, 8)
    sel = (_iota((nsel, H), 1) == g * R + _iota((nsel, H), 0)).astype(f32)
    acs_t = _dot(sel, acs, _NT, hi=True)
    hp = LANES // P
    pieces = []
    for p in range(GW // LANES):
        xp = xdt[:, p * LANES:(p + 1) * LANES]
        acc = None
        for q in range(hp):
            r = p * hp + q
            col = jnp.sum(jnp.where(_iota((L, H), 1) == g * R + r, acs, 0.0), axis=1, keepdims=True)
            row = jnp.sum(jnp.where(_iota((nsel, L), 0) == r, acs_t, 0.0), axis=0, keepdims=True)
            dec = jnp.where(tril, jnp.exp(jnp.where(tril, col - row, 0.0)), 0.0)
            xm = jnp.where(_div_pow2(_iota((L, LANES), 1), P) == q, xp, 0.0)
            t = _dot(cb * dec, xm)
            acc = t if acc is None else acc + t
        pieces.append(acc)
    y_diag = pieces[0] if len(pieces) == 1 else jnp.concatenate(pieces, axis=1)
    st = _dot(Bm, xdt * jnp.exp(last - acs_e), _TN)
    y_off = _dot(Cm, prev) * jnp.exp(acs_e)
    new = prev * jnp.exp(last) + st
    return y_diag + y_off + xs * d_e, new


def _ssd_dims(xbc, dtr):
    S, CD = xbc.shape
    H = dtr.shape[1]
    G, N, P = SSD_N_GROUPS, SSD_D_STATE, SSD_HEAD_DIM
    DI = H * P
    R = H // G
    assert CD == DI + 2 * G * N and DI % N == 0
    return S, H, G, N, P, DI, R, R * P, SSD_CHUNK


def _ssd_scan_fwd(xbc, dtr, dtb, alog, dsk, *, name):
    S, H, G, N, P, DI, R, GW, L = _ssd_dims(xbc, dtr)
    nc = S // L
    boff, coff = DI // N, DI // N + G

    def body(xs_ref, b_ref, c_ref, dtr_ref, dtb_ref, alog_ref, dsk_ref, y_ref, st_ref, state):
        c, g = pl.program_id(0), pl.program_id(1)

        @pl.when(c == 0)
        def _():
            state[g] = jnp.zeros((N, GW), f32)

        prev = state[g]
        st_ref[0, 0] = prev
        y, new = _ssd_chunk(prev, xs_ref[...], b_ref[...], c_ref[...], dtr_ref[...], dtb_ref[...], alog_ref[...],
                            dsk_ref[...], g, R=R, P=P)
        y_ref[...] = y
        state[g] = new

    par = pl.BlockSpec((1, H), lambda c, g: (0, 0))
    return pl.pallas_call(
        body, name=name, grid=(nc, G),
        in_specs=[pl.BlockSpec((L, GW), lambda c, g: (c, g)), pl.BlockSpec((L, N), lambda c, g: (c, boff + g)),
                  pl.BlockSpec((L, N), lambda c, g: (c, coff + g)), pl.BlockSpec((L, H), lambda c, g: (c, 0)), par, par, par],
        out_specs=[pl.BlockSpec((L, GW), lambda c, g: (c, g)), pl.BlockSpec((1, 1, N, GW), lambda c, g: (c, g, 0, 0))],
        out_shape=[jax.ShapeDtypeStruct((S, DI), f32), jax.ShapeDtypeStruct((nc, G, N, GW), f32)],
        scratch_shapes=[pltpu.VMEM((G, N, GW), f32)],
        compiler_params=_cp(("arbitrary", "arbitrary")))(xbc, xbc, xbc, dtr, dtb, alog, dsk)


def _ssd_scan_bwd(xbc, dtr, dtb, alog, dsk, states, dy, *, name):
    S, H, G, N, P, DI, R, GW, L = _ssd_dims(xbc, dtr)
    nc = S // L
    boff, coff = DI // N, DI // N + G

    def body(xs_ref, b_ref, c_ref, dtr_ref, dtb_ref, alog_ref, dsk_ref, st_ref, dy_ref,
             dxs_ref, db_ref, dc_ref, ddtr_ref, ddtb_ref, dalog_ref, ddsk_ref, dstate):
        c, g = pl.program_id(0), pl.program_id(1)

        @pl.when(c == 0)
        def _():
            dstate[g] = jnp.zeros((N, GW), f32)

        @pl.when(jnp.logical_and(c == 0, g == 0))
        def _():
            ddtb_ref[...] = jnp.zeros_like(ddtb_ref)
            dalog_ref[...] = jnp.zeros_like(dalog_ref)
            ddsk_ref[...] = jnp.zeros_like(ddsk_ref)

        @pl.when(g == 0)
        def _():
            ddtr_ref[...] = jnp.zeros_like(ddtr_ref)

        fn = functools.partial(_ssd_chunk, g=g, R=R, P=P)
        _, vjp = jax.vjp(fn, st_ref[0, 0], xs_ref[...], b_ref[...], c_ref[...], dtr_ref[...], dtb_ref[...],
                         alog_ref[...], dsk_ref[...])
        dprev, dxs, dB, dC, ddtr, ddtb, dalog, ddsk = vjp((dy_ref[...], dstate[g]))
        dstate[g] = dprev
        dxs_ref[...] = dxs
        db_ref[...] = dB
        dc_ref[...] = dC
        ddtr_ref[...] += ddtr
        ddtb_ref[...] += ddtb
        dalog_ref[...] += dalog
        ddsk_ref[...] += ddsk

    rc = lambda c: nc - 1 - c
    par = pl.BlockSpec((1, H), lambda c, g: (0, 0))
    return pl.pallas_call(
        body, name=name, grid=(nc, G),
        in_specs=[pl.BlockSpec((L, GW), lambda c, g: (rc(c), g)), pl.BlockSpec((L, N), lambda c, g: (rc(c), boff + g)),
                  pl.BlockSpec((L, N), lambda c, g: (rc(c), coff + g)), pl.BlockSpec((L, H), lambda c, g: (rc(c), 0)),
                  par, par, par, pl.BlockSpec((1, 1, N, GW), lambda c, g: (rc(c), g, 0, 0)),
                  pl.BlockSpec((L, GW), lambda c, g: (rc(c), g))],
        out_specs=[pl.BlockSpec((L, GW), lambda c, g: (rc(c), g)), pl.BlockSpec((L, N), lambda c, g: (rc(c), g)),
                   pl.BlockSpec((L, N), lambda c, g: (rc(c), g)), pl.BlockSpec((L, H), lambda c, g: (rc(c), 0)), par, par, par],
        out_shape=[jax.ShapeDtypeStruct((S, DI), f32), jax.ShapeDtypeStruct((S, G * N), f32), jax.ShapeDtypeStruct((S, G * N), f32),
                   jax.ShapeDtypeStruct((S, H), f32)] + [jax.ShapeDtypeStruct((1, H), f32)] * 3,
        scratch_shapes=[pltpu.VMEM((G, N, GW), f32)],
        compiler_params=_cp(("arbitrary", "arbitrary")))(xbc, xbc, xbc, dtr, dtb, alog, dsk, states, dy)


def _dot3(a, b, dims=((1,), (0,))):
    def split(x):
        hi = x.astype(jnp.bfloat16)
        return hi, (x - hi.astype(f32)).astype(jnp.bfloat16)

    def d(x, y):
        return lax.dot_general(x, y, (dims, ((), ())), preferred_element_type=f32)

    ah, al = split(a)
    bh, bl = split(b)
    return d(ah, bh) + (d(ah, bl) + d(al, bh))


def _neumann_inverses(As):
    L = As[0].shape[0]
    eye = (_iota((L, L), 0) == _iota((L, L), 1)).astype(f32)
    X = [-A for A in As]
    P = [eye + x for x in X]
    n = 1
    while 2 * n < L:
        X = [_dot3(x, x) for x in X]
        P = [p + _dot3(p, x) for p, x in zip(P, X)]
        n *= 2
    return P


@jax.custom_vjp
def _unit_lower_solves(As, Rs):
    return tuple(_dot3(T, R) for T, R in zip(_neumann_inverses(As), Rs))


def _uls_fwd(As, Rs):
    Ts = _neumann_inverses(As)
    Xs = tuple(_dot3(T, R) for T, R in zip(Ts, Rs))
    return Xs, (tuple(Ts), Xs)


def _uls_bwd(res, dXs):
    Ts, Xs = res
    dRs = tuple(_dot3(T, dX, _TN) for T, dX in zip(Ts, dXs))
    dAs = tuple(-_dot3(dR, X, _NT) for dR, X in zip(dRs, Xs))
    return dAs, dRs


_unit_lower_solves.defvjp(_uls_fwd, _uls_bwd)


def _gdn_step(states, qb, kb_, vb, br, ar, alog, dtb, h0, *, rep):
    HB = len(states)
    L = qb.shape[0]
    DK, DV = states[0].shape
    HV = br.shape[1]
    incl = _iota((L, L), 0) >= _iota((L, L), 1)
    strict = _iota((L, L), 0) > _iota((L, L), 1)
    lane = _iota((L, HV), 1)
    g_all = -jnp.exp(alog) * _softplus(ar + dtb)
    gcs = _dot(incl.astype(f32), g_all, hi=True)
    beta_all = jax.nn.sigmoid(br)
    nsel = max(HV, 8)
    gcs_t = _dot((_iota((nsel, HV), 0) == _iota((nsel, HV), 1)).astype(f32), gcs, _NT, hi=True)
    hs = range(HB)
    q = [qb[:, (hh // rep) * DK:(hh // rep + 1) * DK] for hh in hs]
    k = [kb_[:, (hh // rep) * DK:(hh // rep + 1) * DK] for hh in hs]
    v = [vb[:, hh * DV:(hh + 1) * DV] for hh in hs]
    gc = [jnp.sum(jnp.where(lane == h0 + hh, gcs, 0.0), axis=1, keepdims=True) for hh in hs]
    beta = [jnp.sum(jnp.where(lane == h0 + hh, beta_all, 0.0), axis=1, keepdims=True) for hh in hs]
    gc_row = [jnp.sum(jnp.where(_iota((nsel, L), 0) == h0 + hh, gcs_t, 0.0), axis=0, keepdims=True) for hh in hs]
    decay = [jnp.where(incl, jnp.exp(jnp.where(incl, gc[hh] - gc_row[hh], 0.0)), 0.0) for hh in hs]
    kbeta = [k[hh] * beta[hh] for hh in hs]
    a_mat = [jnp.where(strict, _dot(kbeta[hh], k[hh], _NT) * decay[hh], 0.0) for hh in hs]
    eg = [jnp.exp(gc[hh]) for hh in hs]
    sol = _unit_lower_solves(tuple(a_mat), tuple(jnp.concatenate([v[hh] * beta[hh], kbeta[hh] * eg[hh]], axis=1) for hh in hs))
    qk = [jnp.where(incl, _dot(q[hh], k[hh], _NT) * decay[hh], 0.0) for hh in hs]
    g_last = [jnp.sum(jnp.where(_iota((L, 1), 0) == L - 1, gc[hh], 0.0), axis=0, keepdims=True) for hh in hs]
    v_new = [sol[hh][:, :DV] - _dot(sol[hh][:, DV:], states[hh]) for hh in hs]
    outs = [_dot(q[hh] * eg[hh], states[hh]) + _dot(qk[hh], v_new[hh]) for hh in hs]
    news = [states[hh] * jnp.exp(g_last[hh]) + _dot(k[hh] * jnp.exp(g_last[hh] - gc[hh]), v_new[hh], _TN) for hh in hs]
    return (outs[0] if HB == 1 else jnp.concatenate(outs, axis=1)), tuple(news)


def _gdn_dims():
    HK, HV = GDN_N_QK_HEADS, GDN_N_V_HEADS
    rep = HV // HK
    HB = min(GDN_HEADS_PER_STEP, HV)
    assert HV % HB == 0 and HB % rep == 0
    return HK, HV, GDN_DK, GDN_DV, GDN_CHUNK, rep, HB


def _gdn_scan_fwd(qkn, qkv, br, ar, alog, dtb, *, name):
    S = qkn.shape[0]
    HK, HV, DK, DV, L, rep, HB = _gdn_dims()
    nc = S // L
    QW = HB // rep * DK
    koff = HK * DK // QW
    voff = 2 * HK * DK // (HB * DV)

    def body(q_ref, k_ref, v_ref, br_ref, ar_ref, alog_ref, dtb_ref, o_ref, st_ref, state):
        c, hb = pl.program_id(0), pl.program_id(1)
        h0 = hb * HB

        @pl.when(c == 0)
        def _():
            for hh in range(HB):
                state[h0 + hh] = jnp.zeros((DK, DV), f32)

        prev = tuple(state[h0 + hh] for hh in range(HB))
        for hh in range(HB):
            st_ref[0, hh] = prev[hh]
        o, new = _gdn_step(prev, q_ref[...], k_ref[...], v_ref[...], br_ref[...], ar_ref[...], alog_ref[...], dtb_ref[...],
                           h0, rep=rep)
        o_ref[...] = o
        for hh in range(HB):
            state[h0 + hh] = new[hh]

    par = pl.BlockSpec((1, HV), lambda c, h: (0, 0))
    return pl.pallas_call(
        body, name=name, grid=(nc, HV // HB),
        in_specs=[pl.BlockSpec((L, QW), lambda c, h: (c, h)), pl.BlockSpec((L, QW), lambda c, h: (c, koff + h)),
                  pl.BlockSpec((L, HB * DV), lambda c, h: (c, voff + h)), pl.BlockSpec((L, HV), lambda c, h: (c, 0)),
                  pl.BlockSpec((L, HV), lambda c, h: (c, 0)), par, par],
        out_specs=[pl.BlockSpec((L, HB * DV), lambda c, h: (c, h)), pl.BlockSpec((1, HB, DK, DV), lambda c, h: (c, h, 0, 0))],
        out_shape=[jax.ShapeDtypeStruct((S, HV * DV), f32), jax.ShapeDtypeStruct((nc, HV, DK, DV), f32)],
        scratch_shapes=[pltpu.VMEM((HV, DK, DV), f32)],
        compiler_params=_cp(("arbitrary", "arbitrary")))(qkn, qkn, qkv, br, ar, alog, dtb)


def _gdn_scan_bwd(qkn, qkv, br, ar, alog, dtb, states, do, *, name):
    S = qkn.shape[0]
    HK, HV, DK, DV, L, rep, HB = _gdn_dims()
    nc = S // L
    QW = HB // rep * DK
    koff = HK * DK // QW
    voff = 2 * HK * DK // (HB * DV)

    def body(q_ref, k_ref, v_ref, br_ref, ar_ref, alog_ref, dtb_ref, st_ref, do_ref,
             dq_ref, dk_ref, dv_ref, dbr_ref, dar_ref, dalog_ref, ddtb_ref, dstate):
        c, hb = pl.program_id(0), pl.program_id(1)
        h0 = hb * HB

        @pl.when(c == 0)
        def _():
            for hh in range(HB):
                dstate[h0 + hh] = jnp.zeros((DK, DV), f32)

        @pl.when(jnp.logical_and(c == 0, hb == 0))
        def _():
            dalog_ref[...] = jnp.zeros_like(dalog_ref)
            ddtb_ref[...] = jnp.zeros_like(ddtb_ref)

        @pl.when(hb == 0)
        def _():
            dbr_ref[...] = jnp.zeros_like(dbr_ref)
            dar_ref[...] = jnp.zeros_like(dar_ref)

        fn = functools.partial(_gdn_step, h0=h0, rep=rep)
        prev = tuple(st_ref[0, hh] for hh in range(HB))
        _, vjp = jax.vjp(fn, prev, q_ref[...], k_ref[...], v_ref[...], br_ref[...], ar_ref[...], alog_ref[...], dtb_ref[...])
        dprev, dq, dk, dv, dbr, dar, dalog, ddtb = vjp((do_ref[...], tuple(dstate[h0 + hh] for hh in range(HB))))
        for hh in range(HB):
            dstate[h0 + hh] = dprev[hh]
        dq_ref[...] = dq
        dk_ref[...] = dk
        dv_ref[...] = dv
        dbr_ref[...] += dbr
        dar_ref[...] += dar
        dalog_ref[...] += dalog
        ddtb_ref[...] += ddtb

    rc = lambda c: nc - 1 - c
    par = pl.BlockSpec((1, HV), lambda c, h: (0, 0))
    blk = lambda W: pl.BlockSpec((L, W), lambda c, h: (rc(c), h))
    return pl.pallas_call(
        body, name=name, grid=(nc, HV // HB),
        in_specs=[pl.BlockSpec((L, QW), lambda c, h: (rc(c), h)), pl.BlockSpec((L, QW), lambda c, h: (rc(c), koff + h)),
                  pl.BlockSpec((L, HB * DV), lambda c, h: (rc(c), voff + h)), pl.BlockSpec((L, HV), lambda c, h: (rc(c), 0)),
                  pl.BlockSpec((L, HV), lambda c, h: (rc(c), 0)), par, par,
                  pl.BlockSpec((1, HB, DK, DV), lambda c, h: (rc(c), h, 0, 0)), blk(HB * DV)],
        out_specs=[blk(QW), blk(QW), blk(HB * DV), pl.BlockSpec((L, HV), lambda c, h: (rc(c), 0)),
                   pl.BlockSpec((L, HV), lambda c, h: (rc(c), 0)), par, par],
        out_shape=[jax.ShapeDtypeStruct((S, HK * DK), f32), jax.ShapeDtypeStruct((S, HK * DK), f32), jax.ShapeDtypeStruct((S, HV * DV), f32),
                   jax.ShapeDtypeStruct((S, HV), f32), jax.ShapeDtypeStruct((S, HV), f32),
                   jax.ShapeDtypeStruct((1, HV), f32), jax.ShapeDtypeStruct((1, HV), f32)],
        scratch_shapes=[pltpu.VMEM((HV, DK, DV), f32)],
        compiler_params=_cp(("arbitrary", "arbitrary")))(qkn, qkn, qkv, br, ar, alog, dtb, states, do)


def _att_scale():
    return (MLA_NOPE + MLA_ROPE) ** -0.5


def _causal(s, i, j, t):
    qpos = i * t + _iota(s.shape, 0)
    kpos = j * t + _iota(s.shape, 1)
    return kpos <= qpos


def _attn_fwd(qn, qr, kn, kr, v, *, name):
    S, W = qn.shape
    H = W // LANES
    t = _tile(S, ATT_TILE)
    scale = _att_scale()

    def body(qn_ref, qr_ref, kn_ref, kr_ref, v_ref, o_ref, lse_ref):
        i = pl.program_id(1)
        qnv, qrv = qn_ref[...], qr_ref[...]

        def step(j, carry):
            m, l, acc = carry
            rows = pl.ds(pl.multiple_of(j * t, t), t)
            s = (_dot(qnv, kn_ref[rows, :], _NT) + _dot(qrv, kr_ref[rows, :], _NT)) * scale
            s = jnp.where(_causal(s, i, j, t), s, -1e30)
            m_new = jnp.maximum(m, jnp.max(s, axis=1, keepdims=True))
            p = jnp.exp(s - m_new)
            a = jnp.exp(m - m_new)
            return m_new, a * l + jnp.sum(p, axis=1, keepdims=True), a * acc + _dot(p, v_ref[rows, :])

        m, l, acc = lax.fori_loop(0, i + 1, step, (jnp.full((t, 1), -1e30, f32), jnp.zeros((t, 1), f32), jnp.zeros((t, LANES), f32)))
        o_ref[...] = acc / l
        lse_ref[...] = jnp.broadcast_to(m + jnp.log(l), (t, LANES))

    qb = pl.BlockSpec((t, LANES), lambda h, i: (i, h))
    kb = pl.BlockSpec((S, LANES), lambda h, i: (0, h))
    return pl.pallas_call(
        body, name=name, grid=(H, S // t),
        in_specs=[qb, qb, kb, pl.BlockSpec((S, LANES), lambda h, i: (0, 0)), kb],
        out_specs=[qb, qb], out_shape=[jax.ShapeDtypeStruct((S, W), f32), jax.ShapeDtypeStruct((S, W), f32)],
        compiler_params=_cp(("parallel", "arbitrary")))(qn, qr, kn, kr, v)


def _attn_bwd_dq(qn, qr, kn, kr, v, o, lse, do, *, name):
    S, W = qn.shape
    H = W // LANES
    t = _tile(S, ATT_TILE)
    scale = _att_scale()

    def body(qn_ref, qr_ref, kn_ref, kr_ref, v_ref, o_ref, lse_ref, do_ref, dqn_ref, dqr_ref):
        i = pl.program_id(1)
        qnv, qrv, dov = qn_ref[...], qr_ref[...], do_ref[...]
        delta = jnp.sum(dov * o_ref[...], axis=1, keepdims=True)
        lsev = lse_ref[...][:, :1]

        def step(j, carry):
            dqn, dqr = carry
            rows = pl.ds(pl.multiple_of(j * t, t), t)
            knv, krv = kn_ref[rows, :], kr_ref[rows, :]
            s = (_dot(qnv, knv, _NT) + _dot(qrv, krv, _NT)) * scale
            p = jnp.where(_causal(s, i, j, t), jnp.exp(s - lsev), 0.0)
            ds = p * (_dot(dov, v_ref[rows, :], _NT) - delta) * scale
            return dqn + _dot(ds, knv), dqr + _dot(ds, krv)

        dqn, dqr = lax.fori_loop(0, i + 1, step, (jnp.zeros((t, LANES), f32), jnp.zeros((t, LANES), f32)))
        dqn_ref[...] = dqn.astype(MXU_DT)
        dqr_ref[...] = dqr

    qb = pl.BlockSpec((t, LANES), lambda h, i: (i, h))
    kb = pl.BlockSpec((S, LANES), lambda h, i: (0, h))
    return pl.pallas_call(
        body, name=name, grid=(H, S // t),
        in_specs=[qb, qb, kb, pl.BlockSpec((S, LANES), lambda h, i: (0, 0)), kb, qb, qb, qb],
        out_specs=[qb, qb], out_shape=[jax.ShapeDtypeStruct((S, W), MXU_DT), jax.ShapeDtypeStruct((S, W), f32)],
        compiler_params=_cp(("parallel", "arbitrary")))(qn, qr, kn, kr, v, o, lse, do)


def _attn_bwd_dkv(qn, qr, kn, kr, v, o, lse, do, *, name):
    S, W = qn.shape
    H = W // LANES
    t = _tile(S, ATT_TILE)
    nb = S // t
    scale = _att_scale()

    def body(qn_ref, qr_ref, kn_ref, kr_ref, v_ref, o_ref, lse_ref, do_ref, dkn_ref, dkr_ref, dv_ref):
        j, h = pl.program_id(0), pl.program_id(1)
        knv, krv, vv = kn_ref[...], kr_ref[...], v_ref[...]

        def step(i, carry):
            dkn, dkr, dv = carry
            rows = pl.ds(pl.multiple_of(i * t, t), t)
            qnv, qrv, dov = qn_ref[rows, :], qr_ref[rows, :], do_ref[rows, :]
            delta = jnp.sum(dov * o_ref[rows, :], axis=1, keepdims=True)
            s = (_dot(qnv, knv, _NT) + _dot(qrv, krv, _NT)) * scale
            p = jnp.where(_causal(s, i, j, t), jnp.exp(s - lse_ref[rows, :][:, :1]), 0.0)
            ds = p * (_dot(dov, vv, _NT) - delta) * scale
            return dkn + _dot(ds, qnv, _TN), dkr + _dot(ds, qrv, _TN), dv + _dot(p, dov, _TN)

        z = jnp.zeros((t, LANES), f32)
        dkn, dkr, dv = lax.fori_loop(j, nb, step, (z, z, z))
        dkn_ref[...] = dkn.astype(MXU_DT)
        dv_ref[...] = dv.astype(MXU_DT)

        @pl.when(h == 0)
        def _():
            dkr_ref[...] = jnp.zeros_like(dkr_ref)

        dkr_ref[...] += dkr

    full = pl.BlockSpec((S, LANES), lambda j, h: (0, h))
    kb = pl.BlockSpec((t, LANES), lambda j, h: (j, h))
    k0 = pl.BlockSpec((t, LANES), lambda j, h: (j, 0))
    return pl.pallas_call(
        body, name=name, grid=(nb, H),
        in_specs=[full, full, kb, k0, kb, full, full, full],
        out_specs=[kb, k0, kb],
        out_shape=[jax.ShapeDtypeStruct((S, W), MXU_DT), jax.ShapeDtypeStruct((S, LANES), f32), jax.ShapeDtypeStruct((S, W), MXU_DT)],
        compiler_params=_cp(("arbitrary", "arbitrary")))(qn, qr, kn, kr, v, o, lse, do)


def _loss_head(y, target, *, name):
    S, D = y.shape
    tm = _tile(S, ROW_TILE, 8)

    def body(y_ref, t_ref, loss_ref, dy_ref):
        @pl.when(pl.program_id(0) == 0)
        def _():
            loss_ref[...] = jnp.zeros_like(loss_ref)

        e = y_ref[...] - t_ref[...]
        dy_ref[...] = e / D
        part = 0.5 * jnp.sum(jnp.mean(e * e, axis=1, keepdims=True), axis=0, keepdims=True)
        loss_ref[...] += jnp.broadcast_to(part, loss_ref.shape)

    rb = pl.BlockSpec((tm, D), lambda r: (r, 0))
    return pl.pallas_call(
        body, name=name, grid=(S // tm,), in_specs=[rb, rb],
        out_specs=[pl.BlockSpec((1, LANES), lambda r: (0, 0)), rb],
        out_shape=[jax.ShapeDtypeStruct((1, LANES), f32), jax.ShapeDtypeStruct((S, D), f32)],
        compiler_params=_cp(("arbitrary",)))(y, target)


def _adamw(w, g, m, v, *, name):
    R, C = w.shape
    tm = _tile(R, max(8, (1 << 19) // max(C, 1) // 8 * 8), 8)

    def body(w_ref, g_ref, m_ref, v_ref, d_ref, nm_ref, nv_ref):
        gv = g_ref[...]
        nm = ADAM_B1 * m_ref[...] + (1.0 - ADAM_B1) * gv
        nv = ADAM_B2 * v_ref[...] + (1.0 - ADAM_B2) * (gv * gv)
        m_hat = nm / (1.0 - ADAM_B1 ** ADAM_STEP)
        v_hat = nv / (1.0 - ADAM_B2 ** ADAM_STEP)
        d_ref[...] = -ADAM_LR * (m_hat / (jnp.sqrt(v_hat) + ADAM_EPS) + ADAM_WD * w_ref[...])
        nm_ref[...] = nm
        nv_ref[...] = nv

    rb = pl.BlockSpec((tm, C), lambda r: (r, 0))
    sh = jax.ShapeDtypeStruct((R, C), f32)
    return pl.pallas_call(body, name=name, grid=(R // tm,), in_specs=[rb] * 4, out_specs=[rb] * 3, out_shape=[sh] * 3,
                          compiler_params=_cp(("parallel",)))(w, g, m, v)


def _me():
    return lax.axis_index("x"), lax.axis_index("y"), lax.axis_index("c")


def _other_chips(mx, my):
    return [(1 - mx, my), (mx, 1 - my), (1 - mx, 1 - my)]


_ANY = pl.BlockSpec(memory_space=pl.ANY)


def _allgather_chips(xs, *, name):
    n = len(xs)
    halves = [x.shape[0] // 2 for x in xs]
    for x in xs:
        assert x.shape[0] % 2 == 0

    def body(*refs):
        x_refs, o_refs = refs[:n], refs[n:2 * n]
        send, recv, lsem = refs[2 * n:]
        mx, my, mc = _me()
        me = 2 * mx + my
        chips = _other_chips(mx, my)
        pending = []
        for t in range(n):
            hf = halves[t]
            mine = pl.ds(mc * hf, hf)
            local = pltpu.make_async_copy(x_refs[t], o_refs[t].at[me], lsem.at[t])
            local.start()
            pending.append(local)
            for j, (cx, cy) in enumerate(chips):
                cp = pltpu.make_async_remote_copy(x_refs[t].at[mine], o_refs[t].at[me, mine], send.at[t, j], recv.at[t, j],
                                                  device_id=(cx, cy, mc), device_id_type=MESH)
                cp.start()
                pending.append(cp)
        fwd = []
        for t in range(n):
            hf = halves[t]
            mine = pl.ds(mc * hf, hf)
            for j, (cx, cy) in enumerate(chips):
                k = 2 * cx + cy
                pltpu.make_async_remote_copy(x_refs[t].at[mine], o_refs[t].at[k, mine], send.at[t, j], recv.at[t, j],
                                             device_id=(cx, cy, mc), device_id_type=MESH).wait_recv()
                cp = pltpu.make_async_remote_copy(o_refs[t].at[k, mine], o_refs[t].at[k, mine], send.at[t, 3 + j], recv.at[t, 3 + j],
                                                  device_id=(mx, my, 1 - mc), device_id_type=MESH)
                cp.start()
                fwd.append(cp)
        for t in range(n):
            hf = halves[t]
            theirs = pl.ds((1 - mc) * hf, hf)
            for j, (cx, cy) in enumerate(chips):
                k = 2 * cx + cy
                pltpu.make_async_remote_copy(o_refs[t].at[k, theirs], o_refs[t].at[k, theirs], send.at[t, 3 + j], recv.at[t, 3 + j],
                                             device_id=(mx, my, 1 - mc), device_id_type=MESH).wait_recv()
        for t in range(n):
            pending[t * 4].wait()
            for j in range(3):
                pending[t * 4 + 1 + j].wait_send()
        for cp in fwd:
            cp.wait_send()

    return pl.pallas_call(
        body, name=name, in_specs=[_ANY] * n, out_specs=[_ANY] * n,
        out_shape=[jax.ShapeDtypeStruct((4,) + x.shape, x.dtype) for x in xs],
        scratch_shapes=[pltpu.SemaphoreType.DMA((n, 6)), pltpu.SemaphoreType.DMA((n, 6)), pltpu.SemaphoreType.DMA((n,))],
        compiler_params=pltpu.CompilerParams(has_side_effects=True))(*xs)


def _pair_send_halves(gs, *, name):
    n = len(gs)
    halves = [g.shape[1] // 2 for g in gs]

    def body(*refs):
        g_refs, o_refs = refs[:n], refs[n:2 * n]
        send, recv = refs[2 * n:]
        mx, my, mc = _me()
        cps = []
        for t in range(n):
            hf = halves[t]
            theirs = pl.ds((1 - mc) * hf, hf)
            for k in range(4):
                cp = pltpu.make_async_remote_copy(g_refs[t].at[k, theirs], o_refs[t].at[k], send.at[t, k], recv.at[t, k],
                                                  device_id=(mx, my, 1 - mc), device_id_type=MESH)
                cp.start()
                cps.append(cp)
        for cp in cps:
            cp.wait()

    return pl.pallas_call(
        body, name=name, in_specs=[_ANY] * n, out_specs=[_ANY] * n,
        out_shape=[jax.ShapeDtypeStruct((4, g.shape[1] // 2, g.shape[2]), g.dtype) for g in gs],
        scratch_shapes=[pltpu.SemaphoreType.DMA((n, 4)), pltpu.SemaphoreType.DMA((n, 4))],
        compiler_params=pltpu.CompilerParams(has_side_effects=True))(*gs)


def _chip_scatter(ps, *, name):
    n = len(ps)

    def body(*refs):
        p_refs, o_refs = refs[:n], refs[n:2 * n]
        send, recv = refs[2 * n:]
        mx, my, mc = _me()
        chips = _other_chips(mx, my)
        cps = []
        for t in range(n):
            for j, (cx, cy) in enumerate(chips):
                cp = pltpu.make_async_remote_copy(p_refs[t].at[2 * cx + cy], o_refs[t].at[j], send.at[t, j], recv.at[t, j],
                                                  device_id=(cx, cy, mc), device_id_type=MESH)
                cp.start()
                cps.append(cp)
        for cp in cps:
            cp.wait()

    return pl.pallas_call(
        body, name=name, in_specs=[_ANY] * n, out_specs=[_ANY] * n,
        out_shape=[jax.ShapeDtypeStruct((3,) + p.shape[1:], p.dtype) for p in ps],
        scratch_shapes=[pltpu.SemaphoreType.DMA((n, 3)), pltpu.SemaphoreType.DMA((n, 3))],
        compiler_params=pltpu.CompilerParams(has_side_effects=True))(*ps)


def _pair_exchange_halves(fs, *, name):
    n = len(fs)

    def body(*refs):
        f_refs, o_refs = refs[:n], refs[n:2 * n]
        send, recv, lsem = refs[2 * n:]
        mx, my, mc = _me()
        cps = []
        for t in range(n):
            hf = f_refs[t].shape[0]
            mine = pl.ds(mc * hf, hf)
            local = pltpu.make_async_copy(f_refs[t], o_refs[t].at[mine], lsem.at[t])
            local.start()
            cp = pltpu.make_async_remote_copy(f_refs[t], o_refs[t].at[mine], send.at[t], recv.at[t],
                                              device_id=(mx, my, 1 - mc), device_id_type=MESH)
            cp.start()
            cps.append((local, cp))
        for t in range(n):
            hf = f_refs[t].shape[0]
            theirs = pl.ds((1 - mc) * hf, hf)
            local, cp = cps[t]
            local.wait()
            cp.wait_send()
            pltpu.make_async_remote_copy(f_refs[t], o_refs[t].at[theirs], send.at[t], recv.at[t],
                                         device_id=(mx, my, 1 - mc), device_id_type=MESH).wait_recv()

    return pl.pallas_call(
        body, name=name, in_specs=[_ANY] * n, out_specs=[_ANY] * n,
        out_shape=[jax.ShapeDtypeStruct((2 * f.shape[0], f.shape[1]), f.dtype) for f in fs],
        scratch_shapes=[pltpu.SemaphoreType.DMA((n,)), pltpu.SemaphoreType.DMA((n,)), pltpu.SemaphoreType.DMA((n,))],
        compiler_params=pltpu.CompilerParams(has_side_effects=True))(*fs)


def _allgather_all(x, *, name):
    def body(x_ref, o_ref, send, recv, lsem):
        mx, my, mc = _me()
        me = 4 * mx + 2 * my + mc
        local = pltpu.make_async_copy(x_ref, o_ref.at[me], lsem)
        local.start()
        cps = []
        for j in range(1, 8):
            px, py, pc = mx ^ (j >> 2), my ^ ((j >> 1) & 1), mc ^ (j & 1)
            cp = pltpu.make_async_remote_copy(x_ref, o_ref.at[me], send.at[j - 1], recv.at[j - 1],
                                              device_id=(px, py, pc), device_id_type=MESH)
            cp.start()
            cps.append(cp)
        for j in range(1, 8):
            px, py, pc = mx ^ (j >> 2), my ^ ((j >> 1) & 1), mc ^ (j & 1)
            pltpu.make_async_remote_copy(x_ref, o_ref.at[4 * px + 2 * py + pc], send.at[j - 1], recv.at[j - 1],
                                         device_id=(px, py, pc), device_id_type=MESH).wait_recv()
        for cp in cps:
            cp.wait_send()
        local.wait()

    return pl.pallas_call(
        body, name=name, in_specs=[_ANY], out_specs=_ANY, out_shape=jax.ShapeDtypeStruct((8,) + x.shape, x.dtype),
        scratch_shapes=[pltpu.SemaphoreType.DMA((7,)), pltpu.SemaphoreType.DMA((7,)), pltpu.SemaphoreType.DMA],
        compiler_params=pltpu.CompilerParams(has_side_effects=True))(x)


def _add_half(g4, recv, mc, *, name):
    _, R, C = g4.shape
    hf = R // 2
    tm = _tile(hf, max(16, (1 << 19) // C // 16 * 16), 16)
    nb = hf // tm

    def body(mc_ref, g_ref, r_ref, o_ref, ob_ref):
        s = g_ref[...] + r_ref[...]
        o_ref[...] = s
        ob_ref[...] = s.astype(COMM_DT)

    ospec = pl.BlockSpec((1, tm, C), lambda k, i, mc_ref: (k, i, 0))
    return pl.pallas_call(
        body, name=name,
        grid_spec=pltpu.PrefetchScalarGridSpec(
            num_scalar_prefetch=1, grid=(4, nb),
            in_specs=[pl.BlockSpec((1, tm, C), lambda k, i, mc_ref: (k, mc_ref[0] * nb + i, 0)),
                      pl.BlockSpec((1, tm, C), lambda k, i, mc_ref: (k, i, 0))],
            out_specs=[ospec, ospec]),
        out_shape=[jax.ShapeDtypeStruct((4, hf, C), f32), jax.ShapeDtypeStruct((4, hf, C), COMM_DT)],
        compiler_params=_cp(("parallel", "parallel")))(mc, g4, recv)


def _sum_chips(p4, recv3, me, *, name):
    _, Rh, C = p4.shape
    tm = _tile(Rh, max(16, (1 << 19) // C // 16 * 16), 16)

    def body(me_ref, p_ref, r_ref, o_ref):
        o_ref[...] = ((p_ref[0] + r_ref[0].astype(f32)) + r_ref[1].astype(f32)) + r_ref[2].astype(f32)

    return pl.pallas_call(
        body, name=name,
        grid_spec=pltpu.PrefetchScalarGridSpec(
            num_scalar_prefetch=1, grid=(Rh // tm,),
            in_specs=[pl.BlockSpec((1, tm, C), lambda i, me_ref: (me_ref[0], i, 0)),
                      pl.BlockSpec((3, tm, C), lambda i, me_ref: (0, i, 0))],
            out_specs=pl.BlockSpec((tm, C), lambda i, me_ref: (i, 0))),
        out_shape=jax.ShapeDtypeStruct((Rh, C), f32),
        compiler_params=_cp(("parallel",)))(me, p4, recv3)


def _sum8(x8, *, name):
    _, R, C = x8.shape
    tm = _tile(R, 64, 8)

    def body(x_ref, o_ref):
        acc = x_ref[0]
        for k in range(1, 8):
            acc = acc + x_ref[k]
        o_ref[...] = acc

    return pl.pallas_call(body, name=name, grid=(R // tm,), in_specs=[pl.BlockSpec((8, tm, C), lambda i: (0, i, 0))],
                          out_specs=pl.BlockSpec((tm, C), lambda i: (i, 0)), out_shape=jax.ShapeDtypeStruct((R, C), f32),
                          compiler_params=_cp(("parallel",)))(x8)


def _ssd_layer_fwd(h, W, tag):
    z = _mm(h, W["wz"], name=tag + "_z")
    xp = _mm(h, W["wxbc"], name=tag + "_xbc")
    dtr = _mm(h, W["wdt"], name=tag + "_dt")
    xbc = _conv_fwd(xp, W["conv_w"], W["conv_b"], name=tag + "_conv")
    y, states = _ssd_scan_fwd(xbc, dtr, W["dt_bias"], W["a_log"], W["d"], name=tag + "_scan")
    DI = y.shape[1]
    G = SSD_N_GROUPS
    gs = DI // G
    (yn,) = _rowwise(_ssd_gate_fn, [(W["norm_w"], "c", gs)], [(y, "c", gs), (z, "c", gs)], [(DI, MXU_DT, "c", gs)],
                     name=tag + "_gate", ncol=G, tm=512)
    out = _mm(yn, W["wout"], name=tag + "_out")
    return out, dict(h=h, z=z, xp=xp, dtr=dtr, xbc=xbc, states=states, y=y, yn=yn)


def _ssd_layer_bwd(sv, W, dr, drb, tag):
    h = sv["h"]
    DI = sv["y"].shape[1]
    G = SSD_N_GROUPS
    gs = DI // G
    gr = {}
    dyn = _mm(drb, W["wout"], tb=True, name=tag + "_dyn")
    gr["wout"] = _mm(sv["yn"], drb, ta=True, name=tag + "_dwout")
    (dnw,), (dy, dz) = _rowwise_bwd(_ssd_gate_fn, [(W["norm_w"], "c", gs)], [(sv["y"], "c", gs), (sv["z"], "c", gs)],
                                     [(dyn, "c", gs)], name=tag + "_dgate", ncol=G, tm=512, din_dtypes=[(f32,), (MXU_DT,)])
    gr["norm_w"] = dnw
    dxs, dB, dC, ddtr, gr["dt_bias"], gr["a_log"], gr["d"] = _ssd_scan_bwd(
        sv["xbc"], sv["dtr"], W["dt_bias"], W["a_log"], W["d"], sv["states"], dy, name=tag + "_dscan")
    dxbc = jnp.concatenate([dxs, dB, dC], axis=1)
    dxp, gr["conv_w"], gr["conv_b"] = _conv_bwd(sv["xp"], W["conv_w"], W["conv_b"], dxbc, name=tag + "_dconv", dx_dtype=MXU_DT)
    dh = _mm(dz, W["wz"], tb=True, add=dr, add_scale=_alpha(), name=tag + "_dh1")
    dh = _mm(dxp, W["wxbc"], tb=True, add=dh, name=tag + "_dh2")
    dh = _mm(ddtr, W["wdt"], tb=True, add=dh, name=tag + "_dh3")
    gr["wz"] = _mm(h, dz, ta=True, name=tag + "_dwz")
    gr["wxbc"] = _mm(h, dxp, ta=True, name=tag + "_dwxbc")
    gr["wdt"] = _mm(h, ddtr, ta=True, name=tag + "_dwdt")
    return dh, gr


def _mla_layer_fwd(h, W, cos, sin, tag):
    QR, KR = W["wqc"].shape[1], W["wkvc"].shape[1]
    HW = W["wqn"].shape[1]
    H = HW // LANES
    qc = _mm(h, W["wqc"], name=tag + "_qc")
    kvc = _mm(h, W["wkvc"], name=tag + "_kvc")
    krp = _mm(h, W["wkr"], name=tag + "_krp")
    z = _mm(h, W["wz"], name=tag + "_z")
    (qcn,) = _rowwise(_rms_fn, [(W["q_norm"], "a", QR)], [(qc, "a", QR)], [(QR, MXU_DT, "a", QR)], name=tag + "_qnorm")
    (kvn,) = _rowwise(_rms_fn, [(W["kv_norm"], "a", KR)], [(kvc, "a", KR)], [(KR, MXU_DT, "a", KR)], name=tag + "_kvnorm")
    qn = _mm(qcn, W["wqn"], name=tag + "_qn", out_dtype=MXU_DT)
    qrp = _mm(qcn, W["wqr"], name=tag + "_qrp")
    kn = _mm(kvn, W["wkn"], name=tag + "_kn", out_dtype=MXU_DT)
    v = _mm(kvn, W["wv"], name=tag + "_v", out_dtype=MXU_DT)
    (qr,) = _rowwise(_rope_fn, [], [(cos, "a", LANES), (sin, "a", LANES), (qrp, "c", LANES)], [(HW, MXU_DT, "c", LANES)],
                     name=tag + "_qrope", ncol=H)
    (kr,) = _rowwise(_rope_fn, [], [(cos, "a", LANES), (sin, "a", LANES), (krp, "a", LANES)], [(LANES, MXU_DT, "a", LANES)],
                     name=tag + "_krope")
    o, lse = _attn_fwd(qn, qr, kn, kr, v, name=tag + "_attn")
    (og,) = _rowwise(_mul_silu_fn, [], [(o, "a", HW), (z, "a", HW)], [(HW, MXU_DT, "a", HW)], name=tag + "_ogate")
    out = _mm(og, W["wout"], name=tag + "_out")
    return out, dict(h=h, qc=qc, kvc=kvc, z=z, qcn=qcn, kvn=kvn, qn=qn, qr=qr, kn=kn, kr=kr, v=v, o=o, lse=lse, og=og)


def _mla_layer_bwd(sv, W, cos, sin, dr, drb, tag):
    h = sv["h"]
    QR, KR = W["wqc"].shape[1], W["wkvc"].shape[1]
    HW = W["wqn"].shape[1]
    H = HW // LANES
    gr = {}
    dog = _mm(drb, W["wout"], tb=True, name=tag + "_dog")
    gr["wout"] = _mm(sv["og"], drb, ta=True, name=tag + "_dwout")
    _, (do, dz) = _rowwise_bwd(_mul_silu_fn, [], [(sv["o"], "a", HW), (sv["z"], "a", HW)], [(dog, "a", HW)], name=tag + "_dogate",
                               din_dtypes=[(f32,), (MXU_DT,)])
    att = (sv["qn"], sv["qr"], sv["kn"], sv["kr"], sv["v"], sv["o"], sv["lse"], do)
    dqn, dqr = _attn_bwd_dq(*att, name=tag + "_dq")
    dkn, dkr, dv = _attn_bwd_dkv(*att, name=tag + "_dkv")
    _, (dqrp,) = _rowwise_bwd(_rope_fn, [], [(cos, "a", LANES), (sin, "a", LANES), (dqr, "c", LANES)], [(dqr, "c", LANES)],
                              name=tag + "_dqrope", ncol=H, diff_i=[2], din_dtypes=[(MXU_DT,)])
    _, (dkrp,) = _rowwise_bwd(_rope_fn, [], [(cos, "a", LANES), (sin, "a", LANES), (dkr, "a", LANES)], [(dkr, "a", LANES)],
                              name=tag + "_dkrope", diff_i=[2], din_dtypes=[(MXU_DT,)])
    dqcn = _mm(dqn, W["wqn"], tb=True, name=tag + "_dqcn1")
    dqcn = _mm(dqrp, W["wqr"], tb=True, add=dqcn, name=tag + "_dqcn2")
    dkvn = _mm(dkn, W["wkn"], tb=True, name=tag + "_dkvn1")
    dkvn = _mm(dv, W["wv"], tb=True, add=dkvn, name=tag + "_dkvn2")
    gr["wqn"] = _mm(sv["qcn"], dqn, ta=True, name=tag + "_dwqn")
    gr["wqr"] = _mm(sv["qcn"], dqrp, ta=True, name=tag + "_dwqr")
    gr["wkn"] = _mm(sv["kvn"], dkn, ta=True, name=tag + "_dwkn")
    gr["wv"] = _mm(sv["kvn"], dv, ta=True, name=tag + "_dwv")
    (gr["q_norm"],), (dqc,) = _rowwise_bwd(_rms_fn, [(W["q_norm"], "a", QR)], [(sv["qc"], "a", QR)], [(dqcn, "a", QR)], name=tag + "_dqnorm",
                                           din_dtypes=[(MXU_DT,)])
    (gr["kv_norm"],), (dkvc,) = _rowwise_bwd(_rms_fn, [(W["kv_norm"], "a", KR)], [(sv["kvc"], "a", KR)], [(dkvn, "a", KR)], name=tag + "_dkvnorm",
                                             din_dtypes=[(MXU_DT,)])
    dh = _mm(dz, W["wz"], tb=True, add=dr, add_scale=_alpha(), name=tag + "_dh1")
    dh = _mm(dqc, W["wqc"], tb=True, add=dh, name=tag + "_dh2")
    dh = _mm(dkvc, W["wkvc"], tb=True, add=dh, name=tag + "_dh3")
    dh = _mm(dkrp, W["wkr"], tb=True, add=dh, name=tag + "_dh4")
    gr["wz"] = _mm(h, dz, ta=True, name=tag + "_dwz")
    gr["wqc"] = _mm(h, dqc, ta=True, name=tag + "_dwqc")
    gr["wkvc"] = _mm(h, dkvc, ta=True, name=tag + "_dwkvc")
    gr["wkr"] = _mm(h, dkrp, ta=True, name=tag + "_dwkr")
    return dh, gr


def _gdn_layer_fwd(h, W, tag):
    HK, HV, DK, DV = GDN_N_QK_HEADS, GDN_N_V_HEADS, GDN_DK, GDN_DV
    KD, VD = HK * DK, HV * DV
    qkvp = _mm(h, W["wqkv"], name=tag + "_qkv")
    z = _mm(h, W["wz"], name=tag + "_z")
    br = _mm(h, W["wb"], name=tag + "_b")
    ar = _mm(h, W["wa"], name=tag + "_a")
    qkv = _conv_fwd(qkvp, W["conv_w"], jnp.zeros((1, qkvp.shape[1]), f32), name=tag + "_conv")
    scale = jnp.concatenate([jnp.full((1, KD), DK ** -0.5, f32), jnp.ones((1, KD), f32)], axis=1)
    (qkn,) = _rowwise(_l2_fn, [(scale, "c", DK)], [(qkv, "c", DK)], [(2 * KD, f32, "c", DK)], name=tag + "_l2", ncol=2 * HK)
    o, states = _gdn_scan_fwd(qkn, qkv, br, ar, W["a_log"], W["dt_bias"], name=tag + "_scan")
    (on,) = _rowwise(_gdn_gate_fn, [(W["norm_w"], "a", DV)], [(o, "c", DV), (z, "c", DV)], [(VD, MXU_DT, "c", DV)],
                     name=tag + "_gate", ncol=HV, tm=1024)
    out = _mm(on, W["wout"], name=tag + "_out")
    return out, dict(h=h, qkvp=qkvp, z=z, br=br, ar=ar, qkv=qkv, qkn=qkn, o=o, states=states, on=on, scale=scale)


def _gdn_layer_bwd(sv, W, dr, drb, tag):
    h = sv["h"]
    HK, HV, DK, DV = GDN_N_QK_HEADS, GDN_N_V_HEADS, GDN_DK, GDN_DV
    KD, VD = HK * DK, HV * DV
    gr = {}
    don = _mm(drb, W["wout"], tb=True, name=tag + "_don")
    gr["wout"] = _mm(sv["on"], drb, ta=True, name=tag + "_dwout")
    (gr["norm_w"],), (do, dz) = _rowwise_bwd(_gdn_gate_fn, [(W["norm_w"], "a", DV)], [(sv["o"], "c", DV), (sv["z"], "c", DV)],
                                              [(don, "c", DV)], name=tag + "_dgate", ncol=HV, tm=1024, din_dtypes=[(f32,), (MXU_DT,)])
    dq, dk, dv, dbr, dar, gr["a_log"], gr["dt_bias"] = _gdn_scan_bwd(
        sv["qkn"], sv["qkv"], sv["br"], sv["ar"], W["a_log"], W["dt_bias"], sv["states"], do, name=tag + "_dscan")
    dqkn = jnp.concatenate([dq, dk], axis=1)
    _, (dqk,) = _rowwise_bwd(_l2_fn, [(sv["scale"], "c", DK)], [(sv["qkv"], "c", DK)], [(dqkn, "c", DK)],
                             name=tag + "_dl2", ncol=2 * HK, diff_p=[])
    dqkv = jnp.concatenate([dqk, dv], axis=1)
    dqkvp, gr["conv_w"], _ = _conv_bwd(sv["qkvp"], W["conv_w"], jnp.zeros((1, dqkv.shape[1]), f32), dqkv, name=tag + "_dconv",
                                       dx_dtype=MXU_DT)
    dh = _mm(dz, W["wz"], tb=True, add=dr, add_scale=_alpha(), name=tag + "_dh1")
    dh = _mm(dqkvp, W["wqkv"], tb=True, add=dh, name=tag + "_dh2")
    dh = _mm(dbr, W["wb"], tb=True, add=dh, name=tag + "_dh3")
    dh = _mm(dar, W["wa"], tb=True, add=dh, name=tag + "_dh4")
    gr["wz"] = _mm(h, dz, ta=True, name=tag + "_dwz")
    gr["wqkv"] = _mm(h, dqkvp, ta=True, name=tag + "_dwqkv")
    gr["wb"] = _mm(h, dbr, ta=True, name=tag + "_dwb")
    gr["wa"] = _mm(h, dar, ta=True, name=tag + "_dwa")
    return dh, gr


def _rope_tables(positions):
    half = MLA_ROPE // 2
    inv_freq = ROPE_THETA ** (-jnp.arange(0, MLA_ROPE, 2, dtype=f32) / MLA_ROPE)
    ang = positions.astype(f32)[:, None] * inv_freq
    cos, sin = jnp.cos(ang), jnp.sin(ang)
    S = positions.shape[0]
    pad = jnp.zeros((S, LANES - 2 * half), f32)
    return jnp.concatenate([cos, cos, pad + 1.0], axis=1), jnp.concatenate([sin, sin, pad], axis=1)


def _local_step(x, positions, target, LW, ln_g, ln_b):
    cos, sin = _rope_tables(positions)
    h, hb = x, x.astype(MXU_DT)
    saved = []
    for i in range(DEPTH):
        kind, tag = i % 3, "l%d" % i
        if kind == 0:
            y, sv = _ssd_layer_fwd(hb, LW[i], tag)
        elif kind == 1:
            y, sv = _mla_layer_fwd(hb, LW[i], cos, sin, tag)
        else:
            y, sv = _gdn_layer_fwd(hb, LW[i], tag)
        D = h.shape[1]
        r, h, hb = _rowwise(_res_ln_fn, [(ln_g[i], "a", D), (ln_b[i], "a", D)], [(h, "a", D), (y, "a", D)],
                            [(D, f32, "a", D), (D, f32, "a", D), (D, MXU_DT, "a", D)], name=tag + "_ln")
        sv["r"] = r
        saved.append(sv)
    loss, dh = _loss_head(h, target, name="loss_head")
    grads, dg, db = [None] * DEPTH, [None] * DEPTH, [None] * DEPTH
    for i in reversed(range(DEPTH)):
        kind, tag = i % 3, "l%d" % i
        sv = saved[i]
        D = dh.shape[1]
        (dg[i], db[i]), (dr, drb) = _rowwise_bwd(_ln_fn, [(ln_g[i], "a", D), (ln_b[i], "a", D)], [(sv["r"], "a", D)], [(dh, "a", D)],
                                                 name=tag + "_dln", din_dtypes=[(f32, MXU_DT)])
        if kind == 0:
            dh, grads[i] = _ssd_layer_bwd(sv, LW[i], dr, drb, tag)
        elif kind == 1:
            dh, grads[i] = _mla_layer_bwd(sv, LW[i], cos, sin, dr, drb, tag)
        else:
            dh, grads[i] = _gdn_layer_bwd(sv, LW[i], dr, drb, tag)
    return loss, dh, grads, dg, db


_WEIGHTS = ["ssd_in_w", "ssd_conv_w", "ssd_conv_b", "ssd_dt_bias", "ssd_a_log", "ssd_d", "ssd_norm_w", "ssd_out_w",
            "mla_in_w", "mla_q_norm_w", "mla_q_up_w", "mla_kv_norm_w", "mla_kv_up_w", "mla_out_w",
            "gdn_in_w", "gdn_conv_w", "gdn_a_log", "gdn_dt_bias", "gdn_norm_w", "gdn_out_w", "ln_g", "ln_b"]
_BIG = {"ssd_in_w": "col", "ssd_out_w": "row", "mla_in_w": "col", "mla_q_up_w": "col", "mla_kv_up_w": "col",
        "mla_out_w": "row", "gdn_in_w": "col", "gdn_out_w": "row"}
_SMALL_SHARDED = ["ssd_conv_w", "ssd_conv_b", "ssd_norm_w", "gdn_conv_w"]
_PACK_ROWS = 16


def _gathered_to_full(g, kind, nl):
    if kind == "col":
        _, RK, Ns = g.shape
        return g.reshape(4, nl, RK // nl, Ns).transpose(1, 2, 0, 3).reshape(nl, RK // nl, 4 * Ns)
    _, RK, N = g.shape
    return g.reshape(4, nl, RK // nl, N).transpose(1, 0, 2, 3).reshape(nl, 4 * (RK // nl), N)


def _full_to_slots(f, kind):
    nl, K, N = f.shape
    if kind == "col":
        return f.reshape(nl, K, 4, N // 4).transpose(2, 0, 1, 3).reshape(4, nl * K, N // 4)
    return f.reshape(nl, 4, K // 4, N).transpose(1, 0, 2, 3).reshape(4, nl * (K // 4), N)


def _pack(arrs):
    flat = jnp.concatenate([a.reshape(-1).astype(f32) for a in arrs])
    unit = _PACK_ROWS * LANES
    n = -(-flat.shape[0] // unit) * unit
    return jnp.pad(flat, (0, n - flat.shape[0])).reshape(_PACK_ROWS, n // _PACK_ROWS)


def _unpack(packed, shapes):
    flat = packed.reshape(-1)
    out, off = [], 0
    for sh in shapes:
        n = math.prod(sh)
        out.append(flat[off:off + n].reshape(sh))
        off += n
    return out


def _pad_lanes(a):
    return jnp.pad(a, [(0, 0)] * (a.ndim - 1) + [(0, LANES - a.shape[-1])])


def _layer_weights(full, D):
    G, N, P = SSD_N_GROUPS, SSD_D_STATE, SSD_HEAD_DIM
    LW = []
    for i in range(DEPTH):
        kind, j = i % 3, i // 3
        if kind == 0:
            H = full["ssd_dt_bias"].shape[1]
            DI = H * P
            CD = DI + 2 * G * N
            win = full["ssd_in_w"][j]
            LW.append(dict(wz=win[:, :DI], wxbc=win[:, DI:DI + CD], wdt=win[:, DI + CD:], conv_w=full["ssd_conv_w"][j],
                           conv_b=full["ssd_conv_b"][j][None], dt_bias=full["ssd_dt_bias"][j][None], a_log=full["ssd_a_log"][j][None],
                           d=full["ssd_d"][j][None], norm_w=full["ssd_norm_w"][j][None], wout=full["ssd_out_w"][j]))
        elif kind == 1:
            QR, KR = MLA_Q_RANK, MLA_KV_RANK
            win = full["mla_in_w"][j]
            Hh = full["mla_q_up_w"].shape[2] // (MLA_NOPE + MLA_ROPE)
            qup = full["mla_q_up_w"][j].reshape(QR, Hh, MLA_NOPE + MLA_ROPE)
            kvup = full["mla_kv_up_w"][j].reshape(KR, Hh, MLA_NOPE + MLA_V)
            LW.append(dict(wqc=win[:, :QR], wkvc=win[:, QR:QR + KR], wkr=_pad_lanes(win[:, QR + KR:QR + KR + MLA_ROPE]),
                           wz=win[:, QR + KR + MLA_ROPE:], q_norm=full["mla_q_norm_w"][j][None], kv_norm=full["mla_kv_norm_w"][j][None],
                           wqn=qup[:, :, :MLA_NOPE].reshape(QR, Hh * MLA_NOPE), wqr=_pad_lanes(qup[:, :, MLA_NOPE:]).reshape(QR, Hh * LANES),
                           wkn=kvup[:, :, :MLA_NOPE].reshape(KR, Hh * MLA_NOPE), wv=kvup[:, :, MLA_NOPE:].reshape(KR, Hh * MLA_V),
                           wout=full["mla_out_w"][j]))
        else:
            KD, VD, HV = GDN_N_QK_HEADS * GDN_DK, GDN_N_V_HEADS * GDN_DV, GDN_N_V_HEADS
            win = full["gdn_in_w"][j]
            c0, c1 = 2 * KD + VD, 2 * KD + 2 * VD
            LW.append(dict(wqkv=win[:, :c0], wz=win[:, c0:c1], wb=win[:, c1:c1 + HV], wa=win[:, c1 + HV:], conv_w=full["gdn_conv_w"][j],
                           a_log=full["gdn_a_log"][j][None], dt_bias=full["gdn_dt_bias"][j][None], norm_w=full["gdn_norm_w"][j][None],
                           wout=full["gdn_out_w"][j]))
    return LW


def _full_grads(grads, dg, db):
    per = {n: [] for n in _WEIGHTS}
    for i in range(DEPTH):
        kind, g = i % 3, grads[i]
        if kind == 0:
            per["ssd_in_w"].append(jnp.concatenate([g["wz"], g["wxbc"], g["wdt"]], axis=1))
            per["ssd_conv_w"].append(g["conv_w"])
            for n in ("conv_b", "dt_bias", "a_log", "d", "norm_w"):
                per["ssd_" + n].append(g[n][0])
            per["ssd_out_w"].append(g["wout"])
        elif kind == 1:
            QR, KR = g["wqn"].shape[0], g["wkn"].shape[0]
            Hh = g["wqn"].shape[1] // MLA_NOPE
            per["mla_in_w"].append(jnp.concatenate([g["wqc"], g["wkvc"], g["wkr"][:, :MLA_ROPE], g["wz"]], axis=1))
            per["mla_q_up_w"].append(jnp.concatenate(
                [g["wqn"].reshape(QR, Hh, MLA_NOPE), g["wqr"].reshape(QR, Hh, LANES)[:, :, :MLA_ROPE]], axis=2).reshape(QR, -1))
            per["mla_kv_up_w"].append(jnp.concatenate(
                [g["wkn"].reshape(KR, Hh, MLA_NOPE), g["wv"].reshape(KR, Hh, MLA_V)], axis=2).reshape(KR, -1))
            per["mla_q_norm_w"].append(g["q_norm"][0])
            per["mla_kv_norm_w"].append(g["kv_norm"][0])
            per["mla_out_w"].append(g["wout"])
        else:
            per["gdn_in_w"].append(jnp.concatenate([g["wqkv"], g["wz"], g["wb"], g["wa"]], axis=1))
            per["gdn_conv_w"].append(g["conv_w"])
            for n in ("a_log", "dt_bias", "norm_w"):
                per["gdn_" + n].append(g[n][0])
            per["gdn_out_w"].append(g["wout"])
        per["ln_g"].append(dg[i][0])
        per["ln_b"].append(db[i][0])
    return {n: jnp.stack(v) for n, v in per.items()}


def kernel(x, positions, ssd_in_w, ssd_conv_w, ssd_conv_b, ssd_dt_bias, ssd_a_log, ssd_d, ssd_norm_w, ssd_out_w, mla_in_w, mla_q_norm_w, mla_q_up_w, mla_kv_norm_w, mla_kv_up_w, mla_out_w, gdn_in_w, gdn_conv_w, gdn_a_log, gdn_dt_bias, gdn_norm_w, gdn_out_w, ln_g, ln_b, loss_target, m_ssd_in_w, m_ssd_conv_w, m_ssd_conv_b, m_ssd_dt_bias, m_ssd_a_log, m_ssd_d, m_ssd_norm_w, m_ssd_out_w, m_mla_in_w, m_mla_q_norm_w, m_mla_q_up_w, m_mla_kv_norm_w, m_mla_kv_up_w, m_mla_out_w, m_gdn_in_w, m_gdn_conv_w, m_gdn_a_log, m_gdn_dt_bias, m_gdn_norm_w, m_gdn_out_w, m_ln_g, m_ln_b, v_ssd_in_w, v_ssd_conv_w, v_ssd_conv_b, v_ssd_dt_bias, v_ssd_a_log, v_ssd_d, v_ssd_norm_w, v_ssd_out_w, v_mla_in_w, v_mla_q_norm_w, v_mla_q_up_w, v_mla_kv_norm_w, v_mla_kv_up_w, v_mla_out_w, v_gdn_in_w, v_gdn_conv_w, v_gdn_a_log, v_gdn_dt_bias, v_gdn_norm_w, v_gdn_out_w, v_ln_g, v_ln_b):
    args = dict(locals())
    w = {n: args[n] for n in _WEIGHTS}
    mom = {n: args["m_" + n] for n in _WEIGHTS}
    vel = {n: args["v_" + n] for n in _WEIGHTS}
    D = x.shape[-1]
    mx, my, mc = _me()
    chip = 2 * mx + my
    chip_arr = jnp.reshape(chip, (1,)).astype(jnp.int32)
    core_arr = jnp.reshape(mc, (1,)).astype(jnp.int32)
    small = [n for n in _WEIGHTS if n not in _BIG]

    big = list(_BIG)
    send = [w[n].astype(MXU_DT).reshape(-1, w[n].shape[-1]) for n in big] + [_pack([w[n] for n in _SMALL_SHARDED])]
    got = _allgather_chips(send, name="gather_weights")
    full = {n: _gathered_to_full(g, _BIG[n], w[n].shape[0]) for n, g in zip(big, got[:-1])}
    parts = [_unpack(got[-1][k], [w[n].shape for n in _SMALL_SHARDED]) for k in range(4)]
    for t, n in enumerate(_SMALL_SHARDED):
        full[n] = jnp.concatenate([parts[k][t] for k in range(4)], axis=-1)
    for n in small:
        if n not in full:
            full[n] = w[n]

    loss, gx, grads, dg, db = _local_step(x[0], positions[0], loss_target[0], _layer_weights(full, D), [full["ln_g"][i][None] for i in range(DEPTH)],
                                          [full["ln_b"][i][None] for i in range(DEPTH)])
    fg = _full_grads(grads, dg, db)

    g4 = [_full_to_slots(fg[n], _BIG[n]) for n in big]
    r1 = _pair_send_halves(g4, name="grad_pair_send")
    p4 = [_add_half(a, b, core_arr, name="grad_pair_add_" + n) for a, b, n in zip(g4, r1, big)]
    r2 = _chip_scatter([p[1] for p in p4], name="grad_chip_scatter")
    fin = [_sum_chips(a[0], b, chip_arr, name="grad_chip_sum_" + n) for a, b, n in zip(p4, r2, big)]
    gsh = _pair_exchange_halves(fin, name="grad_pair_share")

    out_g, out_d, out_m, out_v = {}, {}, {}, {}
    for n, g in zip(big, gsh):
        sh = w[n].shape
        to2 = lambda a: a.reshape(-1, sh[-1])
        d_, m_, v_ = _adamw(to2(w[n]), g, to2(mom[n]), to2(vel[n]), name="adamw_" + n)
        out_g[n], out_d[n], out_m[n], out_v[n] = g.reshape(sh), d_.reshape(sh), m_.reshape(sh), v_.reshape(sh)

    summed = _sum8(_allgather_all(_pack([fg[n] for n in small] + [loss[0, :1]]), name="gather_small"), name="sum_small")
    sg = _unpack(summed, [fg[n].shape for n in small] + [(1,)])
    loss_total = sg[-1][0]
    gs = {}
    for n, g in zip(small, sg[:-1]):
        if n in _SMALL_SHARDED:
            ws = w[n].shape[-1]
            g = lax.dynamic_slice_in_dim(g, chip * ws, ws, axis=g.ndim - 1)
        gs[n] = g
    shapes = [w[n].shape for n in small]
    d_, m_, v_ = _adamw(_pack([w[n] for n in small]), _pack([gs[n] for n in small]), _pack([mom[n] for n in small]),
                        _pack([vel[n] for n in small]), name="adamw_small")
    for n, a, b, c in zip(small, _unpack(d_, shapes), _unpack(m_, shapes), _unpack(v_, shapes)):
        out_g[n], out_d[n], out_m[n], out_v[n] = gs[n], a, b, c

    return (loss_total, gx[None], *[out_g[n] for n in _WEIGHTS], *[out_d[n] for n in _WEIGHTS],
            *[out_m[n] for n in _WEIGHTS], *[out_v[n] for n in _WEIGHTS])
```

```python
import functools
import math

import jax
import jax.numpy as jnp
from jax import lax
from jax.experimental import pallas as pl
from jax.experimental.pallas import tpu as pltpu

f32 = jnp.float32
HI = lax.Precision.HIGHEST
MXU_DT = jnp.bfloat16
COMM_DT = jnp.bfloat16
MESH = pl.DeviceIdType.MESH

DEPTH = 4
LN_EPS = 1e-5
RMS_EPS = 1e-6
SSD_HEAD_DIM = 64
SSD_N_GROUPS = 8
SSD_D_STATE = 128
SSD_CONV = 4
SSD_CHUNK = 128
MLA_Q_RANK = 768
MLA_KV_RANK = 512
MLA_NOPE = 128
MLA_ROPE = 64
MLA_V = 128
ROPE_THETA = 10000.0
GDN_N_QK_HEADS = 16
GDN_N_V_HEADS = 32
GDN_DK = 128
GDN_DV = 128
GDN_CONV = 4
GDN_CHUNK = 64
ADAM_LR = 0.001
ADAM_B1 = 0.9
ADAM_B2 = 0.999
ADAM_EPS = 1e-08
ADAM_WD = 0.01
ADAM_STEP = 10

LANES = 128
VMEM_LIMIT = 48 * 1024 * 1024
ATT_TILE = 512
ROW_TILE = 256
MM_TILE_M = 1024
MM_TILE_N = 1024
MM_TILE_K = 2048
MM_VMEM_BUDGET = 40 * 1024 * 1024
GDN_HEADS_PER_STEP = 16


def _alpha():
    return (2.0 * DEPTH) ** 0.25


def _tile(n, pref, align=LANES):
    t = min(pref, n) // align * align
    while t >= align:
        if n % t == 0:
            return t
        t -= align
    return n


def _cp(sem=None):
    return pltpu.CompilerParams(dimension_semantics=sem, vmem_limit_bytes=VMEM_LIMIT)


def _iota(shape, dim):
    return lax.broadcasted_iota(jnp.int32, shape, dim)


def _div_pow2(x, p):
    assert p & (p - 1) == 0
    return lax.shift_right_logical(x, jnp.int32(p.bit_length() - 1))


def _dot(a, b, dims=((1,), (0,)), hi=False):
    if hi:
        return lax.dot_general(a.astype(f32), b.astype(f32), (dims, ((), ())), precision=HI, preferred_element_type=f32)
    return lax.dot_general(a.astype(MXU_DT), b.astype(MXU_DT), (dims, ((), ())), preferred_element_type=f32)


_NT = ((1,), (1,))
_TN = ((0,), (0,))


def _split3(x):
    hi = x.astype(jnp.bfloat16)
    r = x - hi.astype(f32)
    mid = r.astype(jnp.bfloat16)
    return hi, mid, (r - mid.astype(f32)).astype(jnp.bfloat16)


def _seldot_impl(a, b, dims, exact):
    def d(x, y):
        return lax.dot_general(x, y, (dims, ((), ())), preferred_element_type=f32)

    if exact == 0:
        a01 = a.astype(jnp.bfloat16)
        t = _split3(b.astype(f32))
        return (d(a01, t[0]) + d(a01, t[1])) + d(a01, t[2])
    b01 = b.astype(jnp.bfloat16)
    t = _split3(a.astype(f32))
    return (d(t[0], b01) + d(t[1], b01)) + d(t[2], b01)


@functools.partial(jax.custom_vjp, nondiff_argnums=(2, 3))
def _seldot(a, b, dims, exact):
    return _seldot_impl(a, b, dims, exact)


def _seldot_fwd(a, b, dims, exact):
    return _seldot_impl(a, b, dims, exact), (a, b)


def _seldot_bwd(dims, exact, res, dy):
    a, b = res
    (ca,), (cb,) = dims
    if exact == 0:
        assert ca == 1
        db = _seldot_impl(a, dy, _TN, 0) if cb == 0 else _seldot_impl(dy, a, _TN, 1)
        return jnp.zeros_like(a), db
    assert ca == 1 and cb == 0
    return _seldot_impl(dy, b, _NT, 1), jnp.zeros_like(b)


_seldot.defvjp(_seldot_fwd, _seldot_bwd)


def _softplus(x):
    return jnp.maximum(x, 0.0) + jnp.log1p(jnp.exp(-jnp.abs(x)))


def _silu(x):
    return x * jax.nn.sigmoid(x)


def _mm(a, b, *, name, ta=False, tb=False, add=None, add_scale=1.0, out_dtype=f32):
    M, K = (a.shape[1], a.shape[0]) if ta else a.shape
    N = b.shape[0] if tb else b.shape[1]
    assert (b.shape[1] if tb else b.shape[0]) == K, (a.shape, b.shape, ta, tb)
    tm, tn, tk = _tile(M, MM_TILE_M), _tile(N, MM_TILE_N), _tile(K, MM_TILE_K)
    ab, bb = jnp.dtype(a.dtype).itemsize, jnp.dtype(b.dtype).itemsize
    while 2 * tk * (tm * ab + tn * bb) + 12 * tm * tn > MM_VMEM_BUDGET and tk % (2 * LANES) == 0:
        tk //= 2
    nk = K // tk
    a_spec = pl.BlockSpec((tk, tm), lambda i, j, k: (k, i)) if ta else pl.BlockSpec((tm, tk), lambda i, j, k: (i, k))
    b_spec = pl.BlockSpec((tn, tk), lambda i, j, k: (j, k)) if tb else pl.BlockSpec((tk, tn), lambda i, j, k: (k, j))
    o_spec = pl.BlockSpec((tm, tn), lambda i, j, k: (i, j))
    dims = ((0 if ta else 1,), (1 if tb else 0,))
    has_add = add is not None

    def body(*refs):
        a_ref, b_ref = refs[:2]
        add_ref = refs[2] if has_add else None
        o_ref = refs[3 if has_add else 2]

        def finish(r):
            if has_add:
                r = r + add_scale * add_ref[...].astype(f32)
            o_ref[...] = r.astype(out_dtype)

        if nk == 1:
            finish(_dot(a_ref[...], b_ref[...], dims))
            return
        acc = refs[-1]
        k = pl.program_id(2)

        @pl.when(k == 0)
        def _():
            acc[...] = jnp.zeros_like(acc)

        acc[...] += _dot(a_ref[...], b_ref[...], dims)

        @pl.when(k == nk - 1)
        def _():
            finish(acc[...])

    ins = [a, b] + ([add] if has_add else [])
    specs = [a_spec, b_spec] + ([o_spec] if has_add else [])
    return pl.pallas_call(
        body, name=name, grid=(M // tm, N // tn, nk), in_specs=specs, out_specs=o_spec,
        out_shape=jax.ShapeDtypeStruct((M, N), out_dtype), scratch_shapes=[pltpu.VMEM((tm, tn), f32)] if nk > 1 else [],
        compiler_params=_cp(("parallel", "parallel", "arbitrary")))(*ins)


def _rw_specs(params, ins, ncol, tm):
    specs = []
    for arr, mode, bw, coff in params:
        if mode == "c":
            specs.append(pl.BlockSpec((1, bw), lambda c, r, coff=coff: (0, c + coff)))
        else:
            specs.append(pl.BlockSpec((1, bw), lambda c, r, coff=coff: (0, coff)))
    for arr, mode, bw, coff in ins:
        if mode == "c":
            specs.append(pl.BlockSpec((tm, bw), lambda c, r, coff=coff: (r, c + coff)))
        else:
            specs.append(pl.BlockSpec((tm, bw), lambda c, r, coff=coff: (r, coff)))
    return specs


def _norm_spec(lst):
    out = []
    for t in lst:
        arr, mode, bw = t[0], t[1], t[2]
        coff = t[3] if len(t) > 3 else 0
        out.append((arr, mode, bw, coff))
    return out


def _rowwise(fn, params, ins, outs, *, name, ncol=1, tm=None):
    params, ins = _norm_spec(params), _norm_spec(ins)
    S = ins[0][0].shape[0]
    tm = _tile(S, tm or ROW_TILE, 8)
    npar, nin = len(params), len(ins)

    def body(*refs):
        pv = [r[...].astype(f32) for r in refs[:npar]]
        iv = [r[...].astype(f32) for r in refs[npar:npar + nin]]
        res = fn(*pv, *iv)
        for o_ref, val in zip(refs[npar + nin:], res):
            o_ref[...] = val.astype(o_ref.dtype)

    out_specs, out_shapes = [], []
    for W, dt, mode, bw in outs:
        out_shapes.append(jax.ShapeDtypeStruct((S, W), dt))
        if mode == "c":
            out_specs.append(pl.BlockSpec((tm, bw), lambda c, r: (r, c)))
        else:
            out_specs.append(pl.BlockSpec((tm, bw), lambda c, r: (r, 0)))
    return pl.pallas_call(
        body, name=name, grid=(ncol, S // tm), in_specs=_rw_specs(params, ins, ncol, tm), out_specs=out_specs,
        out_shape=out_shapes, compiler_params=_cp(("parallel", "parallel")))(*[p[0] for p in params], *[i[0] for i in ins])


def _rowwise_bwd(fn, params, ins, couts, *, name, ncol=1, tm=None, diff_p=None, diff_i=None, din_dtypes=None):
    params, ins, couts = _norm_spec(params), _norm_spec(ins), _norm_spec(couts)
    S = ins[0][0].shape[0]
    tm = _tile(S, tm or ROW_TILE, 8)
    npar, nin, nco = len(params), len(ins), len(couts)
    diff_p = list(range(npar)) if diff_p is None else diff_p
    diff_i = list(range(nin)) if diff_i is None else diff_i
    din_dtypes = [(f32,)] * len(diff_i) if din_dtypes is None else din_dtypes

    def body(*refs):
        c, r = pl.program_id(0), pl.program_id(1)
        pv = [x[...].astype(f32) for x in refs[:npar]]
        iv = [x[...].astype(f32) for x in refs[npar:npar + nin]]
        cv = [x[...].astype(f32) for x in refs[npar + nin:npar + nin + nco]]
        orefs = refs[npar + nin + nco:]

        def g(*dargs):
            p2, i2 = list(pv), list(iv)
            for n, k in enumerate(diff_p):
                p2[k] = dargs[n]
            for n, k in enumerate(diff_i):
                i2[k] = dargs[len(diff_p) + n]
            return tuple(fn(*p2, *i2))

        _, vjp = jax.vjp(g, *[pv[k] for k in diff_p], *[iv[k] for k in diff_i])
        grads = vjp(tuple(cv))
        for n, k in enumerate(diff_p):
            o_ref = orefs[n]
            first = (r == 0) if params[k][1] == "c" else jnp.logical_and(r == 0, c == 0)

            @pl.when(first)
            def _(o_ref=o_ref):
                o_ref[...] = jnp.zeros_like(o_ref)

            o_ref[...] += grads[n]
        pos = len(diff_p)
        for n, k in enumerate(diff_i):
            for _ in din_dtypes[n]:
                orefs[pos][...] = grads[len(diff_p) + n].astype(orefs[pos].dtype)
                pos += 1

    out_specs, out_shapes = [], []
    for k in diff_p:
        arr, mode, bw, coff = params[k]
        W = bw * ncol if mode == "c" else bw
        out_shapes.append(jax.ShapeDtypeStruct((1, W), f32))
        out_specs.append(pl.BlockSpec((1, bw), (lambda c, r: (0, c)) if mode == "c" else (lambda c, r: (0, 0))))
    for n, k in enumerate(diff_i):
        arr, mode, bw, coff = ins[k]
        W = bw * ncol if mode == "c" else bw
        for dt in din_dtypes[n]:
            out_shapes.append(jax.ShapeDtypeStruct((S, W), dt))
            out_specs.append(pl.BlockSpec((tm, bw), (lambda c, r: (r, c)) if mode == "c" else (lambda c, r: (r, 0))))
    res = pl.pallas_call(
        body, name=name, grid=(ncol, S // tm), in_specs=_rw_specs(params, ins + couts, ncol, tm), out_specs=out_specs,
        out_shape=out_shapes, compiler_params=_cp(("arbitrary", "arbitrary")))(
            *[p[0] for p in params], *[i[0] for i in ins], *[c[0] for c in couts])
    return list(res[:len(diff_p)]), list(res[len(diff_p):])


def _ln_fn(g, b, r):
    mu = jnp.mean(r, -1, keepdims=True)
    xc = r - mu
    var = jnp.mean(xc * xc, -1, keepdims=True)
    return (xc * lax.rsqrt(var + LN_EPS) * g + b,)


def _res_ln_fn(g, b, h, y):
    r = _alpha() * h + y
    hn = _ln_fn(g, b, r)
    return (r,) + hn + hn


def _rms_fn(w, x):
    return (x * lax.rsqrt(jnp.mean(x * x, -1, keepdims=True) + RMS_EPS) * w,)


def _ssd_gate_fn(w, y, z):
    yg = y * _silu(z)
    return (yg * lax.rsqrt(jnp.mean(yg * yg, -1, keepdims=True) + RMS_EPS) * w,)


def _mul_silu_fn(o, z):
    return (o * _silu(z),)


def _gdn_gate_fn(w, o, z):
    return (o * lax.rsqrt(jnp.mean(o * o, -1, keepdims=True) + RMS_EPS) * w * _silu(z),)


def _l2_fn(scale, x):
    return (x * lax.rsqrt(jnp.sum(x * x, -1, keepdims=True) + RMS_EPS) * scale,)


def _rope_fn(cos, sin, x):
    half = MLA_ROPE // 2
    i = _iota((LANES, LANES), 0)
    j = _iota((LANES, LANES), 1)
    pm = jnp.where((i == j + half) & (j < half), -1.0, 0.0) + jnp.where((i + half == j) & (j < 2 * half), 1.0, 0.0)
    return (x * cos + _seldot(x, pm.astype(f32), ((1,), (0,)), 1) * sin,)


def _conv_taps(x, K):
    S = x.shape[0]
    rows = _iota(x.shape, 0)
    return [x] + [jnp.where(rows < j, 0.0, pltpu.roll(x, j, 0)) for j in range(1, K)]


def _conv_fwd(x, w, b, *, name):
    S, C = x.shape
    K = w.shape[0]
    cw = _tile(C, LANES)

    def body(x_ref, w_ref, b_ref, o_ref):
        taps = _conv_taps(x_ref[...], K)
        wv = w_ref[...]
        pre = b_ref[...] + taps[0] * wv[K - 1:K, :]
        for j in range(1, K):
            pre = pre + taps[j] * wv[K - 1 - j:K - j, :]
        o_ref[...] = _silu(pre)

    return pl.pallas_call(
        body, name=name, grid=(C // cw,),
        in_specs=[pl.BlockSpec((S, cw), lambda c: (0, c)), pl.BlockSpec((K, cw), lambda c: (0, c)), pl.BlockSpec((1, cw), lambda c: (0, c))],
        out_specs=pl.BlockSpec((S, cw), lambda c: (0, c)), out_shape=jax.ShapeDtypeStruct((S, C), f32),
        compiler_params=_cp(("parallel",)))(x, w, b)


def _conv_bwd(x, w, b, dy, *, name, dx_dtype=f32):
    S, C = x.shape
    K = w.shape[0]
    cw = _tile(C, LANES)

    def body(x_ref, w_ref, b_ref, dy_ref, dx_ref, dw_ref, db_ref):
        taps = _conv_taps(x_ref[...], K)
        wv = w_ref[...]
        pre = b_ref[...] + taps[0] * wv[K - 1:K, :]
        for j in range(1, K):
            pre = pre + taps[j] * wv[K - 1 - j:K - j, :]
        sg = jax.nn.sigmoid(pre)
        dpre = dy_ref[...] * sg * (1.0 + pre * (1.0 - sg))
        db_ref[...] = jnp.sum(dpre, axis=0, keepdims=True)
        rows = _iota(dpre.shape, 0)
        dx = dpre * wv[K - 1:K, :]
        dw_ref[K - 1:K, :] = jnp.sum(dpre * taps[0], axis=0, keepdims=True)
        for j in range(1, K):
            dw_ref[K - 1 - j:K - j, :] = jnp.sum(dpre * taps[j], axis=0, keepdims=True)
            up = jnp.where(rows >= S - j, 0.0, pltpu.roll(dpre, S - j, 0))
            dx = dx + up * wv[K - 1 - j:K - j, :]
        dx_ref[...] = dx.astype(dx_dtype)

    col = lambda c: (0, c)
    return pl.pallas_call(
        body, name=name, grid=(C // cw,),
        in_specs=[pl.BlockSpec((S, cw), col), pl.BlockSpec((K, cw), col), pl.BlockSpec((1, cw), col), pl.BlockSpec((S, cw), col)],
        out_specs=[pl.BlockSpec((S, cw), col), pl.BlockSpec((K, cw), col), pl.BlockSpec((1, cw), col)],
        out_shape=[jax.ShapeDtypeStruct((S, C), dx_dtype), jax.ShapeDtypeStruct((K, C), f32), jax.ShapeDtypeStruct((1, C), f32)],
        compiler_params=_cp(("parallel",)))(x, w, b, dy)


def _ssd_chunk(prev, xs, Bm, Cm, dtr, dtb, alog, dsk, g, *, R, P):
    L, GW = xs.shape
    H = dtr.shape[1]
    tril = _iota((L, L), 0) >= _iota((L, L), 1)
    dt = _softplus(dtr + dtb)
    acs = _seldot(tril.astype(f32), dt * (-jnp.exp(alog)), ((1,), (0,)), 0)
    expand = (_iota((H, GW), 0) == g * R + _div_pow2(_iota((H, GW), 1), P)).astype(f32)
    dt_e = _seldot(dt, expand, ((1,), (0,)), 1)
    acs_e = _seldot(acs, expand, ((1,), (0,)), 1)
    d_e = jnp.sum(_seldot(jnp.broadcast_to(dsk, (8, H)), expand, ((1,), (0,)), 1), axis=0, keepdims=True) * 0.125
    last = jnp.sum(jnp.where(_iota((L, GW), 0) == L - 1, acs_e, 0.0), axis=0, keepdims=True)
    xdt = xs * dt_e
    cb = _dot(Cm, Bm, _NT)
    nsel = max(R, 8)
    sel = (_iota((nsel, H), 1) == g * R + _iota((nsel, H), 0)).astype(f32)
    acs_t = _seldot(sel, acs, _NT, 0)
    hp = LANES // P
    pieces = []
    for p in range(GW // LANES):
        xp = xdt[:, p * LANES:(p + 1) * LANES]
        acc = None
        for q in range(hp):
            r = p * hp + q
            col = jnp.sum(jnp.where(_iota((L, H), 1) == g * R + r, acs, 0.0), axis=1, keepdims=True)
            row = jnp.sum(jnp.where(_iota((nsel, L), 0) == r, acs_t, 0.0), axis=0, keepdims=True)
            dec = jnp.where(tril, jnp.exp(jnp.where(tril, col - row, 0.0)), 0.0)
            xm = jnp.where(_div_pow2(_iota((L, LANES), 1), P) == q, xp, 0.0)
            t = _dot(cb * dec, xm)
            acc = t if acc is None else acc + t
        pieces.append(acc)
    y_diag = pieces[0] if len(pieces) == 1 else jnp.concatenate(pieces, axis=1)
    st = _dot(Bm, xdt * jnp.exp(last - acs_e), _TN)
    y_off = _dot(Cm, prev) * jnp.exp(acs_e)
    new = prev * jnp.exp(last) + st
    return y_diag + y_off + xs * d_e, new


def _ssd_dims(xbc, dtr):
    S, CD = xbc.shape
    H = dtr.shape[1]
    G, N, P = SSD_N_GROUPS, SSD_D_STATE, SSD_HEAD_DIM
    DI = H * P
    R = H // G
    assert CD == DI + 2 * G * N and DI % N == 0
    return S, H, G, N, P, DI, R, R * P, SSD_CHUNK


def _ssd_scan_fwd(xbc, dtr, dtb, alog, dsk, *, name):
    S, H, G, N, P, DI, R, GW, L = _ssd_dims(xbc, dtr)
    nc = S // L
    boff, coff = DI // N, DI // N + G

    def body(xs_ref, b_ref, c_ref, dtr_ref, dtb_ref, alog_ref, dsk_ref, y_ref, st_ref, state):
        c, g = pl.program_id(0), pl.program_id(1)

        @pl.when(c == 0)
        def _():
            state[g] = jnp.zeros((N, GW), f32)

        prev = state[g]
        st_ref[0, 0] = prev
        y, new = _ssd_chunk(prev, xs_ref[...], b_ref[...], c_ref[...], dtr_ref[...], dtb_ref[...], alog_ref[...],
                            dsk_ref[...], g, R=R, P=P)
        y_ref[...] = y
        state[g] = new

    par = pl.BlockSpec((1, H), lambda c, g: (0, 0))
    return pl.pallas_call(
        body, name=name, grid=(nc, G),
        in_specs=[pl.BlockSpec((L, GW), lambda c, g: (c, g)), pl.BlockSpec((L, N), lambda c, g: (c, boff + g)),
                  pl.BlockSpec((L, N), lambda c, g: (c, coff + g)), pl.BlockSpec((L, H), lambda c, g: (c, 0)), par, par, par],
        out_specs=[pl.BlockSpec((L, GW), lambda c, g: (c, g)), pl.BlockSpec((1, 1, N, GW), lambda c, g: (c, g, 0, 0))],
        out_shape=[jax.ShapeDtypeStruct((S, DI), f32), jax.ShapeDtypeStruct((nc, G, N, GW), f32)],
        scratch_shapes=[pltpu.VMEM((G, N, GW), f32)],
        compiler_params=_cp(("arbitrary", "arbitrary")))(xbc, xbc, xbc, dtr, dtb, alog, dsk)


def _ssd_scan_bwd(xbc, dtr, dtb, alog, dsk, states, dy, *, name):
    S, H, G, N, P, DI, R, GW, L = _ssd_dims(xbc, dtr)
    nc = S // L
    boff, coff = DI // N, DI // N + G

    def body(xs_ref, b_ref, c_ref, dtr_ref, dtb_ref, alog_ref, dsk_ref, st_ref, dy_ref,
             dxs_ref, db_ref, dc_ref, ddtr_ref, ddtb_ref, dalog_ref, ddsk_ref, dstate):
        c, g = pl.program_id(0), pl.program_id(1)

        @pl.when(c == 0)
        def _():
            dstate[g] = jnp.zeros((N, GW), f32)

        @pl.when(jnp.logical_and(c == 0, g == 0))
        def _():
            ddtb_ref[...] = jnp.zeros_like(ddtb_ref)
            dalog_ref[...] = jnp.zeros_like(dalog_ref)
            ddsk_ref[...] = jnp.zeros_like(ddsk_ref)

        @pl.when(g == 0)
        def _():
            ddtr_ref[...] = jnp.zeros_like(ddtr_ref)

        fn = functools.partial(_ssd_chunk, g=g, R=R, P=P)
        _, vjp = jax.vjp(fn, st_ref[0, 0], xs_ref[...], b_ref[...], c_ref[...], dtr_ref[...], dtb_ref[...],
                         alog_ref[...], dsk_ref[...])
        dprev, dxs, dB, dC, ddtr, ddtb, dalog, ddsk = vjp((dy_ref[...], dstate[g]))
        dstate[g] = dprev
        dxs_ref[...] = dxs
        db_ref[...] = dB
        dc_ref[...] = dC
        ddtr_ref[...] += ddtr
        ddtb_ref[...] += ddtb
        dalog_ref[...] += dalog
        ddsk_ref[...] += ddsk

    rc = lambda c: nc - 1 - c
    par = pl.BlockSpec((1, H), lambda c, g: (0, 0))
    return pl.pallas_call(
        body, name=name, grid=(nc, G),
        in_specs=[pl.BlockSpec((L, GW), lambda c, g: (rc(c), g)), pl.BlockSpec((L, N), lambda c, g: (rc(c), boff + g)),
                  pl.BlockSpec((L, N), lambda c, g: (rc(c), coff + g)), pl.BlockSpec((L, H), lambda c, g: (rc(c), 0)),
                  par, par, par, pl.BlockSpec((1, 1, N, GW), lambda c, g: (rc(c), g, 0, 0)),
                  pl.BlockSpec((L, GW), lambda c, g: (rc(c), g))],
        out_specs=[pl.BlockSpec((L, GW), lambda c, g: (rc(c), g)), pl.BlockSpec((L, N), lambda c, g: (rc(c), g)),
                   pl.BlockSpec((L, N), lambda c, g: (rc(c), g)), pl.BlockSpec((L, H), lambda c, g: (rc(c), 0)), par, par, par],
        out_shape=[jax.ShapeDtypeStruct((S, DI), f32), jax.ShapeDtypeStruct((S, G * N), f32), jax.ShapeDtypeStruct((S, G * N), f32),
                   jax.ShapeDtypeStruct((S, H), f32)] + [jax.ShapeDtypeStruct((1, H), f32)] * 3,
        scratch_shapes=[pltpu.VMEM((G, N, GW), f32)],
        compiler_params=_cp(("arbitrary", "arbitrary")))(xbc, xbc, xbc, dtr, dtb, alog, dsk, states, dy)


def _dot3(a, b, dims=((1,), (0,))):
    def split(x):
        hi = x.astype(jnp.bfloat16)
        return hi, (x - hi.astype(f32)).astype(jnp.bfloat16)

    def d(x, y):
        return lax.dot_general(x, y, (dims, ((), ())), preferred_element_type=f32)

    ah, al = split(a)
    bh, bl = split(b)
    return d(ah, bh) + (d(ah, bl) + d(al, bh))


def _neumann_inverses(As):
    L = As[0].shape[0]
    eye = (_iota((L, L), 0) == _iota((L, L), 1)).astype(f32)
    X = [-A for A in As]
    P = [eye + x for x in X]
    n = 1
    while 2 * n < L:
        X = [_dot3(x, x) for x in X]
        P = [p + _dot3(p, x) for p, x in zip(P, X)]
        n *= 2
    return P


@jax.custom_vjp
def _unit_lower_solves(As, Rs):
    return tuple(_dot3(T, R) for T, R in zip(_neumann_inverses(As), Rs))


def _uls_fwd(As, Rs):
    Ts = _neumann_inverses(As)
    Xs = tuple(_dot3(T, R) for T, R in zip(Ts, Rs))
    return Xs, (tuple(Ts), Xs)


def _uls_bwd(res, dXs):
    Ts, Xs = res
    dRs = tuple(_dot3(T, dX, _TN) for T, dX in zip(Ts, dXs))
    dAs = tuple(-_dot3(dR, X, _NT) for dR, X in zip(dRs, Xs))
    return dAs, dRs


_unit_lower_solves.defvjp(_uls_fwd, _uls_bwd)


def _gdn_step(states, qb, kb_, vb, br, ar, alog, dtb, h0, *, rep):
    HB = len(states)
    L = qb.shape[0]
    DK, DV = states[0].shape
    HV = br.shape[1]
    incl = _iota((L, L), 0) >= _iota((L, L), 1)
    strict = _iota((L, L), 0) > _iota((L, L), 1)
    lane = _iota((L, HV), 1)
    g_all = -jnp.exp(alog) * _softplus(ar + dtb)
    gcs = _seldot(incl.astype(f32), g_all, ((1,), (0,)), 0)
    beta_all = jax.nn.sigmoid(br)
    nsel = max(HV, 8)
    gcs_t = _seldot((_iota((nsel, HV), 0) == _iota((nsel, HV), 1)).astype(f32), gcs, _NT, 0)
    hs = range(HB)
    q = [qb[:, (hh // rep) * DK:(hh // rep + 1) * DK] for hh in hs]
    k = [kb_[:, (hh // rep) * DK:(hh // rep + 1) * DK] for hh in hs]
    v = [vb[:, hh * DV:(hh + 1) * DV] for hh in hs]
    gc = [jnp.sum(jnp.where(lane == h0 + hh, gcs, 0.0), axis=1, keepdims=True) for hh in hs]
    beta = [jnp.sum(jnp.where(lane == h0 + hh, beta_all, 0.0), axis=1, keepdims=True) for hh in hs]
    gc_row = [jnp.sum(jnp.where(_iota((nsel, L), 0) == h0 + hh, gcs_t, 0.0), axis=0, keepdims=True) for hh in hs]
    decay = [jnp.where(incl, jnp.exp(jnp.where(incl, gc[hh] - gc_row[hh], 0.0)), 0.0) for hh in hs]
    kbeta = [k[hh] * beta[hh] for hh in hs]
    a_mat = [jnp.where(strict, _dot(kbeta[hh], k[hh], _NT) * decay[hh], 0.0) for hh in hs]
    eg = [jnp.exp(gc[hh]) for hh in hs]
    sol = _unit_lower_solves(tuple(a_mat), tuple(jnp.concatenate([v[hh] * beta[hh], kbeta[hh] * eg[hh]], axis=1) for hh in hs))
    qk = [jnp.where(incl, _dot(q[hh], k[hh], _NT) * decay[hh], 0.0) for hh in hs]
    g_last = [jnp.sum(jnp.where(_iota((L, 1), 0) == L - 1, gc[hh], 0.0), axis=0, keepdims=True) for hh in hs]
    v_new = [sol[hh][:, :DV] - _dot(sol[hh][:, DV:], states[hh]) for hh in hs]
    outs = [_dot(q[hh] * eg[hh], states[hh]) + _dot(qk[hh], v_new[hh]) for hh in hs]
    news = [states[hh] * jnp.exp(g_last[hh]) + _dot(k[hh] * jnp.exp(g_last[hh] - gc[hh]), v_new[hh], _TN) for hh in hs]
    return (outs[0] if HB == 1 else jnp.concatenate(outs, axis=1)), tuple(news)


def _gdn_dims():
    HK, HV = GDN_N_QK_HEADS, GDN_N_V_HEADS
    rep = HV // HK
    HB = min(GDN_HEADS_PER_STEP, HV)
    assert HV % HB == 0 and HB % rep == 0
    return HK, HV, GDN_DK, GDN_DV, GDN_CHUNK, rep, HB


def _gdn_scan_fwd(qkn, qkv, br, ar, alog, dtb, *, name):
    S = qkn.shape[0]
    HK, HV, DK, DV, L, rep, HB = _gdn_dims()
    nc = S // L
    QW = HB // rep * DK
    koff = HK * DK // QW
    voff = 2 * HK * DK // (HB * DV)

    def body(q_ref, k_ref, v_ref, br_ref, ar_ref, alog_ref, dtb_ref, o_ref, st_ref, state):
        c, hb = pl.program_id(0), pl.program_id(1)
        h0 = hb * HB

        @pl.when(c == 0)
        def _():
            for hh in range(HB):
                state[h0 + hh] = jnp.zeros((DK, DV), f32)

        prev = tuple(state[h0 + hh] for hh in range(HB))
        for hh in range(HB):
            st_ref[0, hh] = prev[hh]
        o, new = _gdn_step(prev, q_ref[...], k_ref[...], v_ref[...], br_ref[...], ar_ref[...], alog_ref[...], dtb_ref[...],
                           h0, rep=rep)
        o_ref[...] = o
        for hh in range(HB):
            state[h0 + hh] = new[hh]

    par = pl.BlockSpec((1, HV), lambda c, h: (0, 0))
    return pl.pallas_call(
        body, name=name, grid=(nc, HV // HB),
        in_specs=[pl.BlockSpec((L, QW), lambda c, h: (c, h)), pl.BlockSpec((L, QW), lambda c, h: (c, koff + h)),
                  pl.BlockSpec((L, HB * DV), lambda c, h: (c, voff + h)), pl.BlockSpec((L, HV), lambda c, h: (c, 0)),
                  pl.BlockSpec((L, HV), lambda c, h: (c, 0)), par, par],
        out_specs=[pl.BlockSpec((L, HB * DV), lambda c, h: (c, h)), pl.BlockSpec((1, HB, DK, DV), lambda c, h: (c, h, 0, 0))],
        out_shape=[jax.ShapeDtypeStruct((S, HV * DV), f32), jax.ShapeDtypeStruct((nc, HV, DK, DV), f32)],
        scratch_shapes=[pltpu.VMEM((HV, DK, DV), f32)],
        compiler_params=_cp(("arbitrary", "arbitrary")))(qkn, qkn, qkv, br, ar, alog, dtb)


def _gdn_scan_bwd(qkn, qkv, br, ar, alog, dtb, states, do, *, name):
    S = qkn.shape[0]
    HK, HV, DK, DV, L, rep, HB = _gdn_dims()
    nc = S // L
    QW = HB // rep * DK
    koff = HK * DK // QW
    voff = 2 * HK * DK // (HB * DV)

    def body(q_ref, k_ref, v_ref, br_ref, ar_ref, alog_ref, dtb_ref, st_ref, do_ref,
             dq_ref, dk_ref, dv_ref, dbr_ref, dar_ref, dalog_ref, ddtb_ref, dstate):
        c, hb = pl.program_id(0), pl.program_id(1)
        h0 = hb * HB

        @pl.when(c == 0)
        def _():
            for hh in range(HB):
                dstate[h0 + hh] = jnp.zeros((DK, DV), f32)

        @pl.when(jnp.logical_and(c == 0, hb == 0))
        def _():
            dalog_ref[...] = jnp.zeros_like(dalog_ref)
            ddtb_ref[...] = jnp.zeros_like(ddtb_ref)

        @pl.when(hb == 0)
        def _():
            dbr_ref[...] = jnp.zeros_like(dbr_ref)
            dar_ref[...] = jnp.zeros_like(dar_ref)

        fn = functools.partial(_gdn_step, h0=h0, rep=rep)
        prev = tuple(st_ref[0, hh] for hh in range(HB))
        _, vjp = jax.vjp(fn, prev, q_ref[...], k_ref[...], v_ref[...], br_ref[...], ar_ref[...], alog_ref[...], dtb_ref[...])
        dprev, dq, dk, dv, dbr, dar, dalog, ddtb = vjp((do_ref[...], tuple(dstate[h0 + hh] for hh in range(HB))))
        for hh in range(HB):
            dstate[h0 + hh] = dprev[hh]
        dq_ref[...] = dq
        dk_ref[...] = dk
        dv_ref[...] = dv
        dbr_ref[...] += dbr
        dar_ref[...] += dar
        dalog_ref[...] += dalog
        ddtb_ref[...] += ddtb

    rc = lambda c: nc - 1 - c
    par = pl.BlockSpec((1, HV), lambda c, h: (0, 0))
    blk = lambda W: pl.BlockSpec((L, W), lambda c, h: (rc(c), h))
    return pl.pallas_call(
        body, name=name, grid=(nc, HV // HB),
        in_specs=[pl.BlockSpec((L, QW), lambda c, h: (rc(c), h)), pl.BlockSpec((L, QW), lambda c, h: (rc(c), koff + h)),
                  pl.BlockSpec((L, HB * DV), lambda c, h: (rc(c), voff + h)), pl.BlockSpec((L, HV), lambda c, h: (rc(c), 0)),
                  pl.BlockSpec((L, HV), lambda c, h: (rc(c), 0)), par, par,
                  pl.BlockSpec((1, HB, DK, DV), lambda c, h: (rc(c), h, 0, 0)), blk(HB * DV)],
        out_specs=[blk(QW), blk(QW), blk(HB * DV), pl.BlockSpec((L, HV), lambda c, h: (rc(c), 0)),
                   pl.BlockSpec((L, HV), lambda c, h: (rc(c), 0)), par, par],
        out_shape=[jax.ShapeDtypeStruct((S, HK * DK), f32), jax.ShapeDtypeStruct((S, HK * DK), f32), jax.ShapeDtypeStruct((S, HV * DV), f32),
                   jax.ShapeDtypeStruct((S, HV), f32), jax.ShapeDtypeStruct((S, HV), f32),
                   jax.ShapeDtypeStruct((1, HV), f32), jax.ShapeDtypeStruct((1, HV), f32)],
        scratch_shapes=[pltpu.VMEM((HV, DK, DV), f32)],
        compiler_params=_cp(("arbitrary", "arbitrary")))(qkn, qkn, qkv, br, ar, alog, dtb, states, do)


def _att_scale():
    return (MLA_NOPE + MLA_ROPE) ** -0.5


def _causal(s, i, j, t):
    qpos = i * t + _iota(s.shape, 0)
    kpos = j * t + _iota(s.shape, 1)
    return kpos <= qpos


def _attn_fwd(qn, qr, kn, kr, v, *, name):
    S, W = qn.shape
    H = W // LANES
    t = _tile(S, ATT_TILE)
    scale = _att_scale()

    def body(qn_ref, qr_ref, kn_ref, kr_ref, v_ref, o_ref, lse_ref):
        i = pl.program_id(1)
        qc = jnp.concatenate([qn_ref[...], qr_ref[...]], axis=1)

        def step(j, carry, masked):
            m, l, acc = carry
            rows = pl.ds(pl.multiple_of(j * t, t), t)
            s = _dot(qc, jnp.concatenate([kn_ref[rows, :], kr_ref[rows, :]], axis=1), _NT) * scale
            if masked:
                s = jnp.where(_causal(s, i, j, t), s, -1e30)
            m_new = jnp.maximum(m, jnp.max(s, axis=1, keepdims=True))
            p = jnp.exp(s - m_new)
            a = jnp.exp(m - m_new)
            return m_new, a * l + jnp.sum(p, axis=1, keepdims=True), a * acc + _dot(p, v_ref[rows, :])

        carry = lax.fori_loop(0, i, functools.partial(step, masked=False),
                              (jnp.full((t, 1), -1e30, f32), jnp.zeros((t, 1), f32), jnp.zeros((t, LANES), f32)))
        m, l, acc = step(i, carry, True)
        o_ref[...] = acc / l
        lse_ref[...] = jnp.broadcast_to(m + jnp.log(l), (t, LANES))

    qb = pl.BlockSpec((t, LANES), lambda h, i: (i, h))
    kb = pl.BlockSpec((S, LANES), lambda h, i: (0, h))
    return pl.pallas_call(
        body, name=name, grid=(H, S // t),
        in_specs=[qb, qb, kb, pl.BlockSpec((S, LANES), lambda h, i: (0, 0)), kb],
        out_specs=[qb, qb], out_shape=[jax.ShapeDtypeStruct((S, W), f32), jax.ShapeDtypeStruct((S, W), f32)],
        compiler_params=_cp(("parallel", "arbitrary")))(qn, qr, kn, kr, v)


def _attn_bwd_dq(qn, qr, kn, kr, v, o, lse, do, *, name):
    S, W = qn.shape
    H = W // LANES
    t = _tile(S, ATT_TILE)
    scale = _att_scale()

    def body(qn_ref, qr_ref, kn_ref, kr_ref, v_ref, o_ref, lse_ref, do_ref, dqn_ref, dqr_ref):
        i = pl.program_id(1)
        qc = jnp.concatenate([qn_ref[...], qr_ref[...]], axis=1)
        dov = do_ref[...]
        delta = jnp.sum(dov * o_ref[...], axis=1, keepdims=True)
        lsev = lse_ref[...][:, :1]

        def step(j, dq, masked):
            rows = pl.ds(pl.multiple_of(j * t, t), t)
            kc = jnp.concatenate([kn_ref[rows, :], kr_ref[rows, :]], axis=1)
            s = _dot(qc, kc, _NT) * scale
            p = jnp.exp(s - lsev)
            if masked:
                p = jnp.where(_causal(s, i, j, t), p, 0.0)
            ds = p * (_dot(dov, v_ref[rows, :], _NT) - delta) * scale
            return dq + _dot(ds, kc)

        dq = lax.fori_loop(0, i, functools.partial(step, masked=False), jnp.zeros((t, 2 * LANES), f32))
        dq = step(i, dq, True)
        dqn_ref[...] = dq[:, :LANES].astype(MXU_DT)
        dqr_ref[...] = dq[:, LANES:]

    qb = pl.BlockSpec((t, LANES), lambda h, i: (i, h))
    kb = pl.BlockSpec((S, LANES), lambda h, i: (0, h))
    return pl.pallas_call(
        body, name=name, grid=(H, S // t),
        in_specs=[qb, qb, kb, pl.BlockSpec((S, LANES), lambda h, i: (0, 0)), kb, qb, qb, qb],
        out_specs=[qb, qb], out_shape=[jax.ShapeDtypeStruct((S, W), MXU_DT), jax.ShapeDtypeStruct((S, W), f32)],
        compiler_params=_cp(("parallel", "arbitrary")))(qn, qr, kn, kr, v, o, lse, do)


def _attn_bwd_dkv(qn, qr, kn, kr, v, o, lse, do, *, name):
    S, W = qn.shape
    H = W // LANES
    t = _tile(S, ATT_TILE)
    nb = S // t
    scale = _att_scale()

    def body(qn_ref, qr_ref, kn_ref, kr_ref, v_ref, o_ref, lse_ref, do_ref, dkn_ref, dkr_ref, dv_ref):
        j, h = pl.program_id(0), pl.program_id(1)
        kc = jnp.concatenate([kn_ref[...], kr_ref[...]], axis=1)
        vv = v_ref[...]

        def step(i, carry, masked):
            dk, dv = carry
            rows = pl.ds(pl.multiple_of(i * t, t), t)
            qc = jnp.concatenate([qn_ref[rows, :], qr_ref[rows, :]], axis=1)
            dov = do_ref[rows, :]
            delta = jnp.sum(dov * o_ref[rows, :], axis=1, keepdims=True)
            s = _dot(qc, kc, _NT) * scale
            p = jnp.exp(s - lse_ref[rows, :][:, :1])
            if masked:
                p = jnp.where(_causal(s, i, j, t), p, 0.0)
            ds = p * (_dot(dov, vv, _NT) - delta) * scale
            return dk + _dot(ds, qc, _TN), dv + _dot(p, dov, _TN)

        carry = step(j, (jnp.zeros((t, 2 * LANES), f32), jnp.zeros((t, LANES), f32)), True)
        dk, dv = lax.fori_loop(j + 1, nb, functools.partial(step, masked=False), carry)
        dkn_ref[...] = dk[:, :LANES].astype(MXU_DT)
        dv_ref[...] = dv.astype(MXU_DT)

        @pl.when(h == 0)
        def _():
            dkr_ref[...] = jnp.zeros_like(dkr_ref)

        dkr_ref[...] += dk[:, LANES:]

    full = pl.BlockSpec((S, LANES), lambda j, h: (0, h))
    kb = pl.BlockSpec((t, LANES), lambda j, h: (j, h))
    k0 = pl.BlockSpec((t, LANES), lambda j, h: (j, 0))
    return pl.pallas_call(
        body, name=name, grid=(nb, H),
        in_specs=[full, full, kb, k0, kb, full, full, full],
        out_specs=[kb, k0, kb],
        out_shape=[jax.ShapeDtypeStruct((S, W), MXU_DT), jax.ShapeDtypeStruct((S, LANES), f32), jax.ShapeDtypeStruct((S, W), MXU_DT)],
        compiler_params=_cp(("arbitrary", "arbitrary")))(qn, qr, kn, kr, v, o, lse, do)


def _loss_head(y, target, *, name):
    S, D = y.shape
    tm = _tile(S, ROW_TILE, 8)

    def body(y_ref, t_ref, loss_ref, dy_ref):
        @pl.when(pl.program_id(0) == 0)
        def _():
            loss_ref[...] = jnp.zeros_like(loss_ref)

        e = y_ref[...] - t_ref[...]
        dy_ref[...] = e / D
        part = 0.5 * jnp.sum(jnp.mean(e * e, axis=1, keepdims=True), axis=0, keepdims=True)
        loss_ref[...] += jnp.broadcast_to(part, loss_ref.shape)

    rb = pl.BlockSpec((tm, D), lambda r: (r, 0))
    return pl.pallas_call(
        body, name=name, grid=(S // tm,), in_specs=[rb, rb],
        out_specs=[pl.BlockSpec((1, LANES), lambda r: (0, 0)), rb],
        out_shape=[jax.ShapeDtypeStruct((1, LANES), f32), jax.ShapeDtypeStruct((S, D), f32)],
        compiler_params=_cp(("arbitrary",)))(y, target)


def _adamw(w, g, m, v, *, name):
    R, C = w.shape
    tm = _tile(R, max(8, (1 << 19) // max(C, 1) // 8 * 8), 8)

    def body(w_ref, g_ref, m_ref, v_ref, d_ref, nm_ref, nv_ref):
        gv = g_ref[...]
        nm = ADAM_B1 * m_ref[...] + (1.0 - ADAM_B1) * gv
        nv = ADAM_B2 * v_ref[...] + (1.0 - ADAM_B2) * (gv * gv)
        m_hat = nm / (1.0 - ADAM_B1 ** ADAM_STEP)
        v_hat = nv / (1.0 - ADAM_B2 ** ADAM_STEP)
        d_ref[...] = -ADAM_LR * (m_hat / (jnp.sqrt(v_hat) + ADAM_EPS) + ADAM_WD * w_ref[...])
        nm_ref[...] = nm
        nv_ref[...] = nv

    rb = pl.BlockSpec((tm, C), lambda r: (r, 0))
    sh = jax.ShapeDtypeStruct((R, C), f32)
    return pl.pallas_call(body, name=name, grid=(R // tm,), in_specs=[rb] * 4, out_specs=[rb] * 3, out_shape=[sh] * 3,
                          compiler_params=_cp(("parallel",)))(w, g, m, v)


def _me():
    return lax.axis_index("x"), lax.axis_index("y"), lax.axis_index("c")


def _other_chips(mx, my):
    return [(1 - mx, my), (mx, 1 - my), (1 - mx, 1 - my)]


_ANY = pl.BlockSpec(memory_space=pl.ANY)


def _allgather_chips(xs, *, name):
    n = len(xs)
    halves = [x.shape[0] // 2 for x in xs]
    for x in xs:
        assert x.shape[0] % 2 == 0

    def body(*refs):
        x_refs, o_refs = refs[:n], refs[n:2 * n]
        send, recv = refs[2 * n:]
        mx, my, mc = _me()
        me = 2 * mx + my
        chips = _other_chips(mx, my)
        pending = []
        for t in range(n):
            hf = halves[t]
            mine = pl.ds(mc * hf, hf)
            for j, (cx, cy) in enumerate(chips):
                cp = pltpu.make_async_remote_copy(x_refs[t].at[mine], o_refs[t].at[me, mine], send.at[t, j], recv.at[t, j],
                                                  device_id=(cx, cy, mc), device_id_type=MESH)
                cp.start()
                pending.append(cp)
        fwd = []
        for t in range(n):
            hf = halves[t]
            mine = pl.ds(mc * hf, hf)
            for j, (cx, cy) in enumerate(chips):
                k = 2 * cx + cy
                pltpu.make_async_remote_copy(x_refs[t].at[mine], o_refs[t].at[k, mine], send.at[t, j], recv.at[t, j],
                                             device_id=(cx, cy, mc), device_id_type=MESH).wait_recv()
                cp = pltpu.make_async_remote_copy(o_refs[t].at[k, mine], o_refs[t].at[k, mine], send.at[t, 3 + j], recv.at[t, 3 + j],
                                                  device_id=(mx, my, 1 - mc), device_id_type=MESH)
                cp.start()
                fwd.append(cp)
        for t in range(n):
            hf = halves[t]
            theirs = pl.ds((1 - mc) * hf, hf)
            for j, (cx, cy) in enumerate(chips):
                k = 2 * cx + cy
                pltpu.make_async_remote_copy(o_refs[t].at[k, theirs], o_refs[t].at[k, theirs], send.at[t, 3 + j], recv.at[t, 3 + j],
                                             device_id=(mx, my, 1 - mc), device_id_type=MESH).wait_recv()
        for cp in pending + fwd:
            cp.wait_send()

    return pl.pallas_call(
        body, name=name, in_specs=[_ANY] * n, out_specs=[_ANY] * n,
        out_shape=[jax.ShapeDtypeStruct((4,) + x.shape, x.dtype) for x in xs],
        scratch_shapes=[pltpu.SemaphoreType.DMA((n, 6)), pltpu.SemaphoreType.DMA((n, 6))],
        compiler_params=pltpu.CompilerParams(has_side_effects=True))(*xs)


def _pair_send_halves(gs, *, name):
    n = len(gs)
    halves = [g.shape[1] // 2 for g in gs]

    def body(*refs):
        g_refs, o_refs = refs[:n], refs[n:2 * n]
        send, recv = refs[2 * n:]
        mx, my, mc = _me()
        cps = []
        for t in range(n):
            hf = halves[t]
            theirs = pl.ds((1 - mc) * hf, hf)
            for k in range(4):
                cp = pltpu.make_async_remote_copy(g_refs[t].at[k, theirs], o_refs[t].at[k], send.at[t, k], recv.at[t, k],
                                                  device_id=(mx, my, 1 - mc), device_id_type=MESH)
                cp.start()
                cps.append(cp)
        for cp in cps:
            cp.wait()

    return pl.pallas_call(
        body, name=name, in_specs=[_ANY] * n, out_specs=[_ANY] * n,
        out_shape=[jax.ShapeDtypeStruct((4, g.shape[1] // 2, g.shape[2]), g.dtype) for g in gs],
        scratch_shapes=[pltpu.SemaphoreType.DMA((n, 4)), pltpu.SemaphoreType.DMA((n, 4))],
        compiler_params=pltpu.CompilerParams(has_side_effects=True))(*gs)


def _chip_scatter(ps, *, name):
    n = len(ps)

    def body(*refs):
        p_refs, o_refs = refs[:n], refs[n:2 * n]
        send, recv = refs[2 * n:]
        mx, my, mc = _me()
        chips = _other_chips(mx, my)
        cps = []
        for t in range(n):
            for j, (cx, cy) in enumerate(chips):
                cp = pltpu.make_async_remote_copy(p_refs[t].at[2 * cx + cy], o_refs[t].at[j], send.at[t, j], recv.at[t, j],
                                                  device_id=(cx, cy, mc), device_id_type=MESH)
                cp.start()
                cps.append(cp)
        for cp in cps:
            cp.wait()

    return pl.pallas_call(
        body, name=name, in_specs=[_ANY] * n, out_specs=[_ANY] * n,
        out_shape=[jax.ShapeDtypeStruct((3,) + p.shape[1:], p.dtype) for p in ps],
        scratch_shapes=[pltpu.SemaphoreType.DMA((n, 3)), pltpu.SemaphoreType.DMA((n, 3))],
        compiler_params=pltpu.CompilerParams(has_side_effects=True))(*ps)


def _pair_exchange_halves(fs, *, name):
    n = len(fs)

    def body(*refs):
        f_refs, o_refs = refs[:n], refs[n:2 * n]
        send, recv = refs[2 * n:]
        mx, my, mc = _me()
        cps = []
        for t in range(n):
            hf = f_refs[t].shape[0]
            mine = pl.ds(mc * hf, hf)
            cp = pltpu.make_async_remote_copy(f_refs[t], o_refs[t].at[mine], send.at[t], recv.at[t],
                                              device_id=(mx, my, 1 - mc), device_id_type=MESH)
            cp.start()
            cps.append(cp)
        for t in range(n):
            hf = f_refs[t].shape[0]
            theirs = pl.ds((1 - mc) * hf, hf)
            cps[t].wait_send()
            pltpu.make_async_remote_copy(f_refs[t], o_refs[t].at[theirs], send.at[t], recv.at[t],
                                         device_id=(mx, my, 1 - mc), device_id_type=MESH).wait_recv()

    return pl.pallas_call(
        body, name=name, in_specs=[_ANY] * n, out_specs=[_ANY] * n,
        out_shape=[jax.ShapeDtypeStruct((2 * f.shape[0], f.shape[1]), f.dtype) for f in fs],
        scratch_shapes=[pltpu.SemaphoreType.DMA((n,)), pltpu.SemaphoreType.DMA((n,))],
        compiler_params=pltpu.CompilerParams(has_side_effects=True))(*fs)


def _allgather_all(x, *, name):
    def body(x_ref, o_ref, send, recv, lsem):
        mx, my, mc = _me()
        me = 4 * mx + 2 * my + mc
        local = pltpu.make_async_copy(x_ref, o_ref.at[me], lsem)
        local.start()
        cps = []
        for j in range(1, 8):
            px, py, pc = mx ^ (j >> 2), my ^ ((j >> 1) & 1), mc ^ (j & 1)
            cp = pltpu.make_async_remote_copy(x_ref, o_ref.at[me], send.at[j - 1], recv.at[j - 1],
                                              device_id=(px, py, pc), device_id_type=MESH)
            cp.start()
            cps.append(cp)
        for j in range(1, 8):
            px, py, pc = mx ^ (j >> 2), my ^ ((j >> 1) & 1), mc ^ (j & 1)
            pltpu.make_async_remote_copy(x_ref, o_ref.at[4 * px + 2 * py + pc], send.at[j - 1], recv.at[j - 1],
                                         device_id=(px, py, pc), device_id_type=MESH).wait_recv()
        for cp in cps:
            cp.wait_send()
        local.wait()

    return pl.pallas_call(
        body, name=name, in_specs=[_ANY], out_specs=_ANY, out_shape=jax.ShapeDtypeStruct((8,) + x.shape, x.dtype),
        scratch_shapes=[pltpu.SemaphoreType.DMA((7,)), pltpu.SemaphoreType.DMA((7,)), pltpu.SemaphoreType.DMA],
        compiler_params=pltpu.CompilerParams(has_side_effects=True))(x)


def _add_half(g4, recv, mc, *, name):
    _, R, C = g4.shape
    hf = R // 2
    tm = _tile(hf, max(16, (1 << 19) // C // 16 * 16), 16)
    nb = hf // tm

    def body(mc_ref, g_ref, r_ref, o_ref, ob_ref):
        s = g_ref[...] + r_ref[...]
        o_ref[...] = s
        ob_ref[...] = s.astype(COMM_DT)

    ospec = pl.BlockSpec((1, tm, C), lambda k, i, mc_ref: (k, i, 0))
    return pl.pallas_call(
        body, name=name,
        grid_spec=pltpu.PrefetchScalarGridSpec(
            num_scalar_prefetch=1, grid=(4, nb),
            in_specs=[pl.BlockSpec((1, tm, C), lambda k, i, mc_ref: (k, mc_ref[0] * nb + i, 0)),
                      pl.BlockSpec((1, tm, C), lambda k, i, mc_ref: (k, i, 0))],
            out_specs=[ospec, ospec]),
        out_shape=[jax.ShapeDtypeStruct((4, hf, C), f32), jax.ShapeDtypeStruct((4, hf, C), COMM_DT)],
        compiler_params=_cp(("parallel", "parallel")))(mc, g4, recv)


def _sum_chips(p4, recv3, me, *, name):
    _, Rh, C = p4.shape
    tm = _tile(Rh, max(16, (1 << 19) // C // 16 * 16), 16)

    def body(me_ref, p_ref, r_ref, o_ref):
        o_ref[...] = ((p_ref[0] + r_ref[0].astype(f32)) + r_ref[1].astype(f32)) + r_ref[2].astype(f32)

    return pl.pallas_call(
        body, name=name,
        grid_spec=pltpu.PrefetchScalarGridSpec(
            num_scalar_prefetch=1, grid=(Rh // tm,),
            in_specs=[pl.BlockSpec((1, tm, C), lambda i, me_ref: (me_ref[0], i, 0)),
                      pl.BlockSpec((3, tm, C), lambda i, me_ref: (0, i, 0))],
            out_specs=pl.BlockSpec((tm, C), lambda i, me_ref: (i, 0))),
        out_shape=jax.ShapeDtypeStruct((Rh, C), f32),
        compiler_params=_cp(("parallel",)))(me, p4, recv3)


def _sum8(x8, *, name):
    _, R, C = x8.shape
    tm = _tile(R, 64, 8)

    def body(x_ref, o_ref):
        acc = x_ref[0]
        for k in range(1, 8):
            acc = acc + x_ref[k]
        o_ref[...] = acc

    return pl.pallas_call(body, name=name, grid=(R // tm,), in_specs=[pl.BlockSpec((8, tm, C), lambda i: (0, i, 0))],
                          out_specs=pl.BlockSpec((tm, C), lambda i: (i, 0)), out_shape=jax.ShapeDtypeStruct((R, C), f32),
                          compiler_params=_cp(("parallel",)))(x8)


def _ssd_layer_fwd(h, W, tag):
    z = _mm(h, W["wz"], name=tag + "_z")
    xp = _mm(h, W["wxbc"], name=tag + "_xbc")
    dtr = _mm(h, W["wdt"], name=tag + "_dt")
    xbc = _conv_fwd(xp, W["conv_w"], W["conv_b"], name=tag + "_conv")
    y, states = _ssd_scan_fwd(xbc, dtr, W["dt_bias"], W["a_log"], W["d"], name=tag + "_scan")
    DI = y.shape[1]
    G = SSD_N_GROUPS
    gs = DI // G
    (yn,) = _rowwise(_ssd_gate_fn, [(W["norm_w"], "c", gs)], [(y, "c", gs), (z, "c", gs)], [(DI, MXU_DT, "c", gs)],
                     name=tag + "_gate", ncol=G, tm=512)
    out = _mm(yn, W["wout"], name=tag + "_out")
    return out, dict(h=h, z=z, xp=xp, dtr=dtr, xbc=xbc, states=states, y=y, yn=yn)


def _ssd_layer_bwd(sv, W, dr, drb, tag):
    h = sv["h"]
    DI = sv["y"].shape[1]
    G = SSD_N_GROUPS
    gs = DI // G
    gr = {}
    dyn = _mm(drb, W["wout"], tb=True, name=tag + "_dyn")
    gr["wout"] = _mm(sv["yn"], drb, ta=True, name=tag + "_dwout")
    (dnw,), (dy, dz) = _rowwise_bwd(_ssd_gate_fn, [(W["norm_w"], "c", gs)], [(sv["y"], "c", gs), (sv["z"], "c", gs)],
                                     [(dyn, "c", gs)], name=tag + "_dgate", ncol=G, tm=512, din_dtypes=[(f32,), (MXU_DT,)])
    gr["norm_w"] = dnw
    dxs, dB, dC, ddtr, gr["dt_bias"], gr["a_log"], gr["d"] = _ssd_scan_bwd(
        sv["xbc"], sv["dtr"], W["dt_bias"], W["a_log"], W["d"], sv["states"], dy, name=tag + "_dscan")
    dxbc = jnp.concatenate([dxs, dB, dC], axis=1)
    dxp, gr["conv_w"], gr["conv_b"] = _conv_bwd(sv["xp"], W["conv_w"], W["conv_b"], dxbc, name=tag + "_dconv", dx_dtype=MXU_DT)
    dh = _mm(dz, W["wz"], tb=True, add=dr, add_scale=_alpha(), name=tag + "_dh1")
    dh = _mm(dxp, W["wxbc"], tb=True, add=dh, name=tag + "_dh2")
    dh = _mm(ddtr, W["wdt"], tb=True, add=dh, name=tag + "_dh3")
    gr["wz"] = _mm(h, dz, ta=True, name=tag + "_dwz")
    gr["wxbc"] = _mm(h, dxp, ta=True, name=tag + "_dwxbc")
    gr["wdt"] = _mm(h, ddtr, ta=True, name=tag + "_dwdt")
    return dh, gr


def _mla_layer_fwd(h, W, cos, sin, tag):
    QR, KR = W["wqc"].shape[1], W["wkvc"].shape[1]
    HW = W["wqn"].shape[1]
    H = HW // LANES
    qc = _mm(h, W["wqc"], name=tag + "_qc")
    kvc = _mm(h, W["wkvc"], name=tag + "_kvc")
    krp = _mm(h, W["wkr"], name=tag + "_krp")
    z = _mm(h, W["wz"], name=tag + "_z")
    (qcn,) = _rowwise(_rms_fn, [(W["q_norm"], "a", QR)], [(qc, "a", QR)], [(QR, MXU_DT, "a", QR)], name=tag + "_qnorm")
    (kvn,) = _rowwise(_rms_fn, [(W["kv_norm"], "a", KR)], [(kvc, "a", KR)], [(KR, MXU_DT, "a", KR)], name=tag + "_kvnorm")
    qn = _mm(qcn, W["wqn"], name=tag + "_qn", out_dtype=MXU_DT)
    qrp = _mm(qcn, W["wqr"], name=tag + "_qrp")
    kn = _mm(kvn, W["wkn"], name=tag + "_kn", out_dtype=MXU_DT)
    v = _mm(kvn, W["wv"], name=tag + "_v", out_dtype=MXU_DT)
    (qr,) = _rowwise(_rope_fn, [], [(cos, "a", LANES), (sin, "a", LANES), (qrp, "c", LANES)], [(HW, MXU_DT, "c", LANES)],
                     name=tag + "_qrope", ncol=H, tm=1024)
    (kr,) = _rowwise(_rope_fn, [], [(cos, "a", LANES), (sin, "a", LANES), (krp, "a", LANES)], [(LANES, MXU_DT, "a", LANES)],
                     name=tag + "_krope")
    o, lse = _attn_fwd(qn, qr, kn, kr, v, name=tag + "_attn")
    (og,) = _rowwise(_mul_silu_fn, [], [(o, "a", HW), (z, "a", HW)], [(HW, MXU_DT, "a", HW)], name=tag + "_ogate")
    out = _mm(og, W["wout"], name=tag + "_out")
    return out, dict(h=h, qc=qc, kvc=kvc, z=z, qcn=qcn, kvn=kvn, qn=qn, qr=qr, kn=kn, kr=kr, v=v, o=o, lse=lse, og=og)


def _mla_layer_bwd(sv, W, cos, sin, dr, drb, tag):
    h = sv["h"]
    QR, KR = W["wqc"].shape[1], W["wkvc"].shape[1]
    HW = W["wqn"].shape[1]
    H = HW // LANES
    gr = {}
    dog = _mm(drb, W["wout"], tb=True, name=tag + "_dog")
    gr["wout"] = _mm(sv["og"], drb, ta=True, name=tag + "_dwout")
    _, (do, dz) = _rowwise_bwd(_mul_silu_fn, [], [(sv["o"], "a", HW), (sv["z"], "a", HW)], [(dog, "a", HW)], name=tag + "_dogate",
                               din_dtypes=[(f32,), (MXU_DT,)])
    att = (sv["qn"], sv["qr"], sv["kn"], sv["kr"], sv["v"], sv["o"], sv["lse"], do)
    dqn, dqr = _attn_bwd_dq(*att, name=tag + "_dq")
    dkn, dkr, dv = _attn_bwd_dkv(*att, name=tag + "_dkv")
    _, (dqrp,) = _rowwise_bwd(_rope_fn, [], [(cos, "a", LANES), (sin, "a", LANES), (dqr, "c", LANES)], [(dqr, "c", LANES)],
                              name=tag + "_dqrope", ncol=H, tm=1024, diff_i=[2], din_dtypes=[(MXU_DT,)])
    _, (dkrp,) = _rowwise_bwd(_rope_fn, [], [(cos, "a", LANES), (sin, "a", LANES), (dkr, "a", LANES)], [(dkr, "a", LANES)],
                              name=tag + "_dkrope", diff_i=[2], din_dtypes=[(MXU_DT,)])
    dqcn = _mm(dqn, W["wqn"], tb=True, name=tag + "_dqcn1")
    dqcn = _mm(dqrp, W["wqr"], tb=True, add=dqcn, name=tag + "_dqcn2")
    dkvn = _mm(dkn, W["wkn"], tb=True, name=tag + "_dkvn1")
    dkvn = _mm(dv, W["wv"], tb=True, add=dkvn, name=tag + "_dkvn2")
    gr["wqn"] = _mm(sv["qcn"], dqn, ta=True, name=tag + "_dwqn")
    gr["wqr"] = _mm(sv["qcn"], dqrp, ta=True, name=tag + "_dwqr")
    gr["wkn"] = _mm(sv["kvn"], dkn, ta=True, name=tag + "_dwkn")
    gr["wv"] = _mm(sv["kvn"], dv, ta=True, name=tag + "_dwv")
    (gr["q_norm"],), (dqc,) = _rowwise_bwd(_rms_fn, [(W["q_norm"], "a", QR)], [(sv["qc"], "a", QR)], [(dqcn, "a", QR)], name=tag + "_dqnorm",
                                           din_dtypes=[(MXU_DT,)])
    (gr["kv_norm"],), (dkvc,) = _rowwise_bwd(_rms_fn, [(W["kv_norm"], "a", KR)], [(sv["kvc"], "a", KR)], [(dkvn, "a", KR)], name=tag + "_dkvnorm",
                                             din_dtypes=[(MXU_DT,)])
    dh = _mm(dz, W["wz"], tb=True, add=dr, add_scale=_alpha(), name=tag + "_dh1")
    dh = _mm(dqc, W["wqc"], tb=True, add=dh, name=tag + "_dh2")
    dh = _mm(dkvc, W["wkvc"], tb=True, add=dh, name=tag + "_dh3")
    dh = _mm(dkrp, W["wkr"], tb=True, add=dh, name=tag + "_dh4")
    gr["wz"] = _mm(h, dz, ta=True, name=tag + "_dwz")
    gr["wqc"] = _mm(h, dqc, ta=True, name=tag + "_dwqc")
    gr["wkvc"] = _mm(h, dkvc, ta=True, name=tag + "_dwkvc")
    gr["wkr"] = _mm(h, dkrp, ta=True, name=tag + "_dwkr")
    return dh, gr


def _gdn_layer_fwd(h, W, tag):
    HK, HV, DK, DV = GDN_N_QK_HEADS, GDN_N_V_HEADS, GDN_DK, GDN_DV
    KD, VD = HK * DK, HV * DV
    qkvp = _mm(h, W["wqkv"], name=tag + "_qkv")
    z = _mm(h, W["wz"], name=tag + "_z")
    br = _mm(h, W["wb"], name=tag + "_b")
    ar = _mm(h, W["wa"], name=tag + "_a")
    qkv = _conv_fwd(qkvp, W["conv_w"], jnp.zeros((1, qkvp.shape[1]), f32), name=tag + "_conv")
    scale = jnp.concatenate([jnp.full((1, KD), DK ** -0.5, f32), jnp.ones((1, KD), f32)], axis=1)
    (qkn,) = _rowwise(_l2_fn, [(scale, "c", DK)], [(qkv, "c", DK)], [(2 * KD, f32, "c", DK)], name=tag + "_l2", ncol=2 * HK, tm=2048)
    o, states = _gdn_scan_fwd(qkn, qkv, br, ar, W["a_log"], W["dt_bias"], name=tag + "_scan")
    (on,) = _rowwise(_gdn_gate_fn, [(W["norm_w"], "a", DV)], [(o, "c", DV), (z, "c", DV)], [(VD, MXU_DT, "c", DV)],
                     name=tag + "_gate", ncol=HV, tm=1024)
    out = _mm(on, W["wout"], name=tag + "_out")
    return out, dict(h=h, qkvp=qkvp, z=z, br=br, ar=ar, qkv=qkv, qkn=qkn, o=o, states=states, on=on, scale=scale)


def _gdn_layer_bwd(sv, W, dr, drb, tag):
    h = sv["h"]
    HK, HV, DK, DV = GDN_N_QK_HEADS, GDN_N_V_HEADS, GDN_DK, GDN_DV
    KD, VD = HK * DK, HV * DV
    gr = {}
    don = _mm(drb, W["wout"], tb=True, name=tag + "_don")
    gr["wout"] = _mm(sv["on"], drb, ta=True, name=tag + "_dwout")
    (gr["norm_w"],), (do, dz) = _rowwise_bwd(_gdn_gate_fn, [(W["norm_w"], "a", DV)], [(sv["o"], "c", DV), (sv["z"], "c", DV)],
                                              [(don, "c", DV)], name=tag + "_dgate", ncol=HV, tm=1024, din_dtypes=[(f32,), (MXU_DT,)])
    dq, dk, dv, dbr, dar, gr["a_log"], gr["dt_bias"] = _gdn_scan_bwd(
        sv["qkn"], sv["qkv"], sv["br"], sv["ar"], W["a_log"], W["dt_bias"], sv["states"], do, name=tag + "_dscan")
    dqkn = jnp.concatenate([dq, dk], axis=1)
    _, (dqk,) = _rowwise_bwd(_l2_fn, [(sv["scale"], "c", DK)], [(sv["qkv"], "c", DK)], [(dqkn, "c", DK)],
                             name=tag + "_dl2", ncol=2 * HK, tm=2048, diff_p=[])
    dqkv = jnp.concatenate([dqk, dv], axis=1)
    dqkvp, gr["conv_w"], _ = _conv_bwd(sv["qkvp"], W["conv_w"], jnp.zeros((1, dqkv.shape[1]), f32), dqkv, name=tag + "_dconv",
                                       dx_dtype=MXU_DT)
    dh = _mm(dz, W["wz"], tb=True, add=dr, add_scale=_alpha(), name=tag + "_dh1")
    dh = _mm(dqkvp, W["wqkv"], tb=True, add=dh, name=tag + "_dh2")
    dh = _mm(dbr, W["wb"], tb=True, add=dh, name=tag + "_dh3")
    dh = _mm(dar, W["wa"], tb=True, add=dh, name=tag + "_dh4")
    gr["wz"] = _mm(h, dz, ta=True, name=tag + "_dwz")
    gr["wqkv"] = _mm(h, dqkvp, ta=True, name=tag + "_dwqkv")
    gr["wb"] = _mm(h, dbr, ta=True, name=tag + "_dwb")
    gr["wa"] = _mm(h, dar, ta=True, name=tag + "_dwa")
    return dh, gr


def _rope_tables(positions):
    half = MLA_ROPE // 2
    inv_freq = ROPE_THETA ** (-jnp.arange(0, MLA_ROPE, 2, dtype=f32) / MLA_ROPE)
    ang = positions.astype(f32)[:, None] * inv_freq
    cos, sin = jnp.cos(ang), jnp.sin(ang)
    S = positions.shape[0]
    pad = jnp.zeros((S, LANES - 2 * half), f32)
    return jnp.concatenate([cos, cos, pad + 1.0], axis=1), jnp.concatenate([sin, sin, pad], axis=1)


def _local_step(x, positions, target, LW, ln_g, ln_b):
    cos, sin = _rope_tables(positions)
    h, hb = x, x.astype(MXU_DT)
    saved = []
    for i in range(DEPTH):
        kind, tag = i % 3, "l%d" % i
        if kind == 0:
            y, sv = _ssd_layer_fwd(hb, LW[i], tag)
        elif kind == 1:
            y, sv = _mla_layer_fwd(hb, LW[i], cos, sin, tag)
        else:
            y, sv = _gdn_layer_fwd(hb, LW[i], tag)
        D = h.shape[1]
        r, h, hb = _rowwise(_res_ln_fn, [(ln_g[i], "a", D), (ln_b[i], "a", D)], [(h, "a", D), (y, "a", D)],
                            [(D, f32, "a", D), (D, f32, "a", D), (D, MXU_DT, "a", D)], name=tag + "_ln")
        sv["r"] = r
        saved.append(sv)
    loss, dh = _loss_head(h, target, name="loss_head")
    grads, dg, db = [None] * DEPTH, [None] * DEPTH, [None] * DEPTH
    for i in reversed(range(DEPTH)):
        kind, tag = i % 3, "l%d" % i
        sv = saved[i]
        D = dh.shape[1]
        (dg[i], db[i]), (dr, drb) = _rowwise_bwd(_ln_fn, [(ln_g[i], "a", D), (ln_b[i], "a", D)], [(sv["r"], "a", D)], [(dh, "a", D)],
                                                 name=tag + "_dln", din_dtypes=[(f32, MXU_DT)])
        if kind == 0:
            dh, grads[i] = _ssd_layer_bwd(sv, LW[i], dr, drb, tag)
        elif kind == 1:
            dh, grads[i] = _mla_layer_bwd(sv, LW[i], cos, sin, dr, drb, tag)
        else:
            dh, grads[i] = _gdn_layer_bwd(sv, LW[i], dr, drb, tag)
    return loss, dh, grads, dg, db


_WEIGHTS = ["ssd_in_w", "ssd_conv_w", "ssd_conv_b", "ssd_dt_bias", "ssd_a_log", "ssd_d", "ssd_norm_w", "ssd_out_w",
            "mla_in_w", "mla_q_norm_w", "mla_q_up_w", "mla_kv_norm_w", "mla_kv_up_w", "mla_out_w",
            "gdn_in_w", "gdn_conv_w", "gdn_a_log", "gdn_dt_bias", "gdn_norm_w", "gdn_out_w", "ln_g", "ln_b"]
_BIG = {"ssd_in_w": "col", "ssd_out_w": "row", "mla_in_w": "col", "mla_q_up_w": "col", "mla_kv_up_w": "col",
        "mla_out_w": "row", "gdn_in_w": "col", "gdn_out_w": "row"}
_SMALL_SHARDED = ["ssd_conv_w", "ssd_conv_b", "ssd_norm_w", "gdn_conv_w"]
_PACK_ROWS = 16


def _gathered_to_full(g, kind, nl):
    if kind == "col":
        _, RK, Ns = g.shape
        return g.reshape(4, nl, RK // nl, Ns).transpose(1, 2, 0, 3).reshape(nl, RK // nl, 4 * Ns)
    _, RK, N = g.shape
    return g.reshape(4, nl, RK // nl, N).transpose(1, 0, 2, 3).reshape(nl, 4 * (RK // nl), N)


def _full_to_slots(f, kind):
    nl, K, N = f.shape
    if kind == "col":
        return f.reshape(nl, K, 4, N // 4).transpose(2, 0, 1, 3).reshape(4, nl * K, N // 4)
    return f.reshape(nl, 4, K // 4, N).transpose(1, 0, 2, 3).reshape(4, nl * (K // 4), N)


def _pack(arrs):
    flat = jnp.concatenate([a.reshape(-1).astype(f32) for a in arrs])
    unit = _PACK_ROWS * LANES
    n = -(-flat.shape[0] // unit) * unit
    return jnp.pad(flat, (0, n - flat.shape[0])).reshape(_PACK_ROWS, n // _PACK_ROWS)


def _unpack(packed, shapes):
    flat = packed.reshape(-1)
    out, off = [], 0
    for sh in shapes:
        n = math.prod(sh)
        out.append(flat[off:off + n].reshape(sh))
        off += n
    return out


def _pad_lanes(a):
    return jnp.pad(a, [(0, 0)] * (a.ndim - 1) + [(0, LANES - a.shape[-1])])


def _layer_weights(full, D):
    G, N, P = SSD_N_GROUPS, SSD_D_STATE, SSD_HEAD_DIM
    LW = []
    for i in range(DEPTH):
        kind, j = i % 3, i // 3
        if kind == 0:
            H = full["ssd_dt_bias"].shape[1]
            DI = H * P
            CD = DI + 2 * G * N
            win = full["ssd_in_w"][j]
            LW.append(dict(wz=win[:, :DI], wxbc=win[:, DI:DI + CD], wdt=win[:, DI + CD:], conv_w=full["ssd_conv_w"][j],
                           conv_b=full["ssd_conv_b"][j][None], dt_bias=full["ssd_dt_bias"][j][None], a_log=full["ssd_a_log"][j][None],
                           d=full["ssd_d"][j][None], norm_w=full["ssd_norm_w"][j][None], wout=full["ssd_out_w"][j]))
        elif kind == 1:
            QR, KR = MLA_Q_RANK, MLA_KV_RANK
            win = full["mla_in_w"][j]
            Hh = full["mla_q_up_w"].shape[2] // (MLA_NOPE + MLA_ROPE)
            qup = full["mla_q_up_w"][j].reshape(QR, Hh, MLA_NOPE + MLA_ROPE)
            kvup = full["mla_kv_up_w"][j].reshape(KR, Hh, MLA_NOPE + MLA_V)
            LW.append(dict(wqc=win[:, :QR], wkvc=win[:, QR:QR + KR], wkr=_pad_lanes(win[:, QR + KR:QR + KR + MLA_ROPE]),
                           wz=win[:, QR + KR + MLA_ROPE:], q_norm=full["mla_q_norm_w"][j][None], kv_norm=full["mla_kv_norm_w"][j][None],
                           wqn=qup[:, :, :MLA_NOPE].reshape(QR, Hh * MLA_NOPE), wqr=_pad_lanes(qup[:, :, MLA_NOPE:]).reshape(QR, Hh * LANES),
                           wkn=kvup[:, :, :MLA_NOPE].reshape(KR, Hh * MLA_NOPE), wv=kvup[:, :, MLA_NOPE:].reshape(KR, Hh * MLA_V),
                           wout=full["mla_out_w"][j]))
        else:
            KD, VD, HV = GDN_N_QK_HEADS * GDN_DK, GDN_N_V_HEADS * GDN_DV, GDN_N_V_HEADS
            win = full["gdn_in_w"][j]
            c0, c1 = 2 * KD + VD, 2 * KD + 2 * VD
            LW.append(dict(wqkv=win[:, :c0], wz=win[:, c0:c1], wb=win[:, c1:c1 + HV], wa=win[:, c1 + HV:], conv_w=full["gdn_conv_w"][j],
                           a_log=full["gdn_a_log"][j][None], dt_bias=full["gdn_dt_bias"][j][None], norm_w=full["gdn_norm_w"][j][None],
                           wout=full["gdn_out_w"][j]))
    return LW


def _full_grads(grads, dg, db):
    per = {n: [] for n in _WEIGHTS}
    for i in range(DEPTH):
        kind, g = i % 3, grads[i]
        if kind == 0:
            per["ssd_in_w"].append(jnp.concatenate([g["wz"], g["wxbc"], g["wdt"]], axis=1))
            per["ssd_conv_w"].append(g["conv_w"])
            for n in ("conv_b", "dt_bias", "a_log", "d", "norm_w"):
                per["ssd_" + n].append(g[n][0])
            per["ssd_out_w"].append(g["wout"])
        elif kind == 1:
            QR, KR = g["wqn"].shape[0], g["wkn"].shape[0]
            Hh = g["wqn"].shape[1] // MLA_NOPE
            per["mla_in_w"].append(jnp.concatenate([g["wqc"], g["wkvc"], g["wkr"][:, :MLA_ROPE], g["wz"]], axis=1))
            per["mla_q_up_w"].append(jnp.concatenate(
                [g["wqn"].reshape(QR, Hh, MLA_NOPE), g["wqr"].reshape(QR, Hh, LANES)[:, :, :MLA_ROPE]], axis=2).reshape(QR, -1))
            per["mla_kv_up_w"].append(jnp.concatenate(
                [g["wkn"].reshape(KR, Hh, MLA_NOPE), g["wv"].reshape(KR, Hh, MLA_V)], axis=2).reshape(KR, -1))
            per["mla_q_norm_w"].append(g["q_norm"][0])
            per["mla_kv_norm_w"].append(g["kv_norm"][0])
            per["mla_out_w"].append(g["wout"])
        else:
            per["gdn_in_w"].append(jnp.concatenate([g["wqkv"], g["wz"], g["wb"], g["wa"]], axis=1))
            per["gdn_conv_w"].append(g["conv_w"])
            for n in ("a_log", "dt_bias", "norm_w"):
                per["gdn_" + n].append(g[n][0])
            per["gdn_out_w"].append(g["wout"])
        per["ln_g"].append(dg[i][0])
        per["ln_b"].append(db[i][0])
    return {n: jnp.stack(v) for n, v in per.items()}


def kernel(x, positions, ssd_in_w, ssd_conv_w, ssd_conv_b, ssd_dt_bias, ssd_a_log, ssd_d, ssd_norm_w, ssd_out_w, mla_in_w, mla_q_norm_w, mla_q_up_w, mla_kv_norm_w, mla_kv_up_w, mla_out_w, gdn_in_w, gdn_conv_w, gdn_a_log, gdn_dt_bias, gdn_norm_w, gdn_out_w, ln_g, ln_b, loss_target, m_ssd_in_w, m_ssd_conv_w, m_ssd_conv_b, m_ssd_dt_bias, m_ssd_a_log, m_ssd_d, m_ssd_norm_w, m_ssd_out_w, m_mla_in_w, m_mla_q_norm_w, m_mla_q_up_w, m_mla_kv_norm_w, m_mla_kv_up_w, m_mla_out_w, m_gdn_in_w, m_gdn_conv_w, m_gdn_a_log, m_gdn_dt_bias, m_gdn_norm_w, m_gdn_out_w, m_ln_g, m_ln_b, v_ssd_in_w, v_ssd_conv_w, v_ssd_conv_b, v_ssd_dt_bias, v_ssd_a_log, v_ssd_d, v_ssd_norm_w, v_ssd_out_w, v_mla_in_w, v_mla_q_norm_w, v_mla_q_up_w, v_mla_kv_norm_w, v_mla_kv_up_w, v_mla_out_w, v_gdn_in_w, v_gdn_conv_w, v_gdn_a_log, v_gdn_dt_bias, v_gdn_norm_w, v_gdn_out_w, v_ln_g, v_ln_b):
    args = dict(locals())
    w = {n: args[n] for n in _WEIGHTS}
    mom = {n: args["m_" + n] for n in _WEIGHTS}
    vel = {n: args["v_" + n] for n in _WEIGHTS}
    D = x.shape[-1]
    mx, my, mc = _me()
    chip = 2 * mx + my
    chip_arr = jnp.reshape(chip, (1,)).astype(jnp.int32)
    core_arr = jnp.reshape(mc, (1,)).astype(jnp.int32)
    small = [n for n in _WEIGHTS if n not in _BIG]

    big = list(_BIG)
    send = [w[n].astype(MXU_DT).reshape(-1, w[n].shape[-1]) for n in big] + [_pack([w[n] for n in _SMALL_SHARDED])]
    got = _allgather_chips(send, name="gather_weights")
    got = [lax.dynamic_update_slice(g, s[None], (chip, 0, 0)) for g, s in zip(got, send)]
    full = {n: _gathered_to_full(g, _BIG[n], w[n].shape[0]) for n, g in zip(big, got[:-1])}
    parts = [_unpack(got[-1][k], [w[n].shape for n in _SMALL_SHARDED]) for k in range(4)]
    for t, n in enumerate(_SMALL_SHARDED):
        full[n] = jnp.concatenate([parts[k][t] for k in range(4)], axis=-1)
    for n in small:
        if n not in full:
            full[n] = w[n]

    loss, gx, grads, dg, db = _local_step(x[0], positions[0], loss_target[0], _layer_weights(full, D), [full["ln_g"][i][None] for i in range(DEPTH)],
                                          [full["ln_b"][i][None] for i in range(DEPTH)])
    fg = _full_grads(grads, dg, db)

    g4 = [_full_to_slots(fg[n], _BIG[n]) for n in big]
    r1 = _pair_send_halves(g4, name="grad_pair_send")
    p4 = [_add_half(a, b, core_arr, name="grad_pair_add_" + n) for a, b, n in zip(g4, r1, big)]
    r2 = _chip_scatter([p[1] for p in p4], name="grad_chip_scatter")
    fin = [_sum_chips(a[0], b, chip_arr, name="grad_chip_sum_" + n) for a, b, n in zip(p4, r2, big)]
    gsh = _pair_exchange_halves(fin, name="grad_pair_share")
    gsh = [lax.dynamic_update_slice(g, f, (mc * f.shape[0], 0)) for g, f in zip(gsh, fin)]

    out_g, out_d, out_m, out_v = {}, {}, {}, {}
    for n, g in zip(big, gsh):
        sh = w[n].shape
        to2 = lambda a: a.reshape(-1, sh[-1])
        d_, m_, v_ = _adamw(to2(w[n]), g, to2(mom[n]), to2(vel[n]), name="adamw_" + n)
        out_g[n], out_d[n], out_m[n], out_v[n] = g.reshape(sh), d_.reshape(sh), m_.reshape(sh), v_.reshape(sh)

    summed = _sum8(_allgather_all(_pack([fg[n] for n in small] + [loss[0, :1]]), name="gather_small"), name="sum_small")
    sg = _unpack(summed, [fg[n].shape for n in small] + [(1,)])
    loss_total = sg[-1][0]
    gs = {}
    for n, g in zip(small, sg[:-1]):
        if n in _SMALL_SHARDED:
            ws = w[n].shape[-1]
            g = lax.dynamic_slice_in_dim(g, chip * ws, ws, axis=g.ndim - 1)
        gs[n] = g
    shapes = [w[n].shape for n in small]
    d_, m_, v_ = _adamw(_pack([w[n] for n in small]), _pack([gs[n] for n in small]), _pack([mom[n] for n in small]),
                        _pack([vel[n] for n in small]), name="adamw_small")
    for n, a, b, c in zip(small, _unpack(d_, shapes), _unpack(m_, shapes), _unpack(v_, shapes)):
        out_g[n], out_d[n], out_m[n], out_v[n] = gs[n], a, b, c

    return (loss_total, gx[None], *[out_g[n] for n in _WEIGHTS], *[out_d[n] for n in _WEIGHTS],
            *[out_m[n] for n in _WEIGHTS], *[out_v[n] for n in _WEIGHTS])
```

```python
import functools
import math

import jax
import jax.numpy as jnp
from jax import lax
from jax.experimental import pallas as pl
from jax.experimental.pallas import tpu as pltpu

f32 = jnp.float32
HI = lax.Precision.HIGHEST
MXU_DT = jnp.bfloat16
COMM_DT = jnp.bfloat16
MESH = pl.DeviceIdType.MESH

DEPTH = 4
LN_EPS = 1e-5
RMS_EPS = 1e-6
SSD_HEAD_DIM = 64
SSD_N_GROUPS = 8
SSD_D_STATE = 128
SSD_CONV = 4
SSD_CHUNK = 128
MLA_Q_RANK = 768
MLA_KV_RANK = 512
MLA_NOPE = 128
MLA_ROPE = 64
MLA_V = 128
ROPE_THETA = 10000.0
GDN_N_QK_HEADS = 16
GDN_N_V_HEADS = 32
GDN_DK = 128
GDN_DV = 128
GDN_CONV = 4
GDN_CHUNK = 64
ADAM_LR = 0.001
ADAM_B1 = 0.9
ADAM_B2 = 0.999
ADAM_EPS = 1e-08
ADAM_WD = 0.01
ADAM_STEP = 10

LANES = 128
VMEM_LIMIT = 48 * 1024 * 1024
ATT_TILE = 512
ROW_TILE = 256
MM_TILE_M = 1024
MM_TILE_N = 1024
MM_TILE_K = 2048
MM_VMEM_BUDGET = 40 * 1024 * 1024
GDN_HEADS_PER_STEP = 16


def _alpha():
    return (2.0 * DEPTH) ** 0.25


def _tile(n, pref, align=LANES):
    t = min(pref, n) // align * align
    while t >= align:
        if n % t == 0:
            return t
        t -= align
    return n


def _cp(sem=None):
    return pltpu.CompilerParams(dimension_semantics=sem, vmem_limit_bytes=VMEM_LIMIT)


def _iota(shape, dim):
    return lax.broadcasted_iota(jnp.int32, shape, dim)


def _div_pow2(x, p):
    assert p & (p - 1) == 0
    return lax.shift_right_logical(x, jnp.int32(p.bit_length() - 1))


def _dot(a, b, dims=((1,), (0,)), hi=False):
    if hi:
        return lax.dot_general(a.astype(f32), b.astype(f32), (dims, ((), ())), precision=HI, preferred_element_type=f32)
    return lax.dot_general(a.astype(MXU_DT), b.astype(MXU_DT), (dims, ((), ())), preferred_element_type=f32)


_NT = ((1,), (1,))
_TN = ((0,), (0,))


def _split3(x):
    hi = x.astype(jnp.bfloat16)
    r = x - hi.astype(f32)
    mid = r.astype(jnp.bfloat16)
    return hi, mid, (r - mid.astype(f32)).astype(jnp.bfloat16)


def _seldot_impl(a, b, dims, exact):
    def d(x, y):
        return lax.dot_general(x, y, (dims, ((), ())), preferred_element_type=f32)

    if exact == 0:
        a01 = a.astype(jnp.bfloat16)
        t = _split3(b.astype(f32))
        return (d(a01, t[0]) + d(a01, t[1])) + d(a01, t[2])
    b01 = b.astype(jnp.bfloat16)
    t = _split3(a.astype(f32))
    return (d(t[0], b01) + d(t[1], b01)) + d(t[2], b01)


@functools.partial(jax.custom_vjp, nondiff_argnums=(2, 3))
def _seldot(a, b, dims, exact):
    return _seldot_impl(a, b, dims, exact)


def _seldot_fwd(a, b, dims, exact):
    return _seldot_impl(a, b, dims, exact), (a, b)


def _seldot_bwd(dims, exact, res, dy):
    a, b = res
    (ca,), (cb,) = dims
    if exact == 0:
        assert ca == 1
        db = _seldot_impl(a, dy, _TN, 0) if cb == 0 else _seldot_impl(dy, a, _TN, 1)
        return jnp.zeros_like(a), db
    assert ca == 1 and cb == 0
    return _seldot_impl(dy, b, _NT, 1), jnp.zeros_like(b)


_seldot.defvjp(_seldot_fwd, _seldot_bwd)


def _softplus(x):
    return jnp.maximum(x, 0.0) + jnp.log1p(jnp.exp(-jnp.abs(x)))


def _silu(x):
    return x * jax.nn.sigmoid(x)


def _mm(a, b, *, name, ta=False, tb=False, add=None, add_scale=1.0, out_dtype=f32):
    M, K = (a.shape[1], a.shape[0]) if ta else a.shape
    N = b.shape[0] if tb else b.shape[1]
    assert (b.shape[1] if tb else b.shape[0]) == K, (a.shape, b.shape, ta, tb)
    tm, tn, tk = _tile(M, MM_TILE_M), _tile(N, MM_TILE_N), _tile(K, MM_TILE_K)
    ab, bb = jnp.dtype(a.dtype).itemsize, jnp.dtype(b.dtype).itemsize
    while 2 * tk * (tm * ab + tn * bb) + 12 * tm * tn > MM_VMEM_BUDGET and tk % (2 * LANES) == 0:
        tk //= 2
    nk = K // tk
    a_spec = pl.BlockSpec((tk, tm), lambda i, j, k: (k, i)) if ta else pl.BlockSpec((tm, tk), lambda i, j, k: (i, k))
    b_spec = pl.BlockSpec((tn, tk), lambda i, j, k: (j, k)) if tb else pl.BlockSpec((tk, tn), lambda i, j, k: (k, j))
    o_spec = pl.BlockSpec((tm, tn), lambda i, j, k: (i, j))
    dims = ((0 if ta else 1,), (1 if tb else 0,))
    has_add = add is not None

    def body(*refs):
        a_ref, b_ref = refs[:2]
        add_ref = refs[2] if has_add else None
        o_ref = refs[3 if has_add else 2]

        def finish(r):
            if has_add:
                r = r + add_scale * add_ref[...].astype(f32)
            o_ref[...] = r.astype(out_dtype)

        if nk == 1:
            finish(_dot(a_ref[...], b_ref[...], dims))
            return
        acc = refs[-1]
        k = pl.program_id(2)

        @pl.when(k == 0)
        def _():
            acc[...] = jnp.zeros_like(acc)

        acc[...] += _dot(a_ref[...], b_ref[...], dims)

        @pl.when(k == nk - 1)
        def _():
            finish(acc[...])

    ins = [a, b] + ([add] if has_add else [])
    specs = [a_spec, b_spec] + ([o_spec] if has_add else [])
    return pl.pallas_call(
        body, name=name, grid=(M // tm, N // tn, nk), in_specs=specs, out_specs=o_spec,
        out_shape=jax.ShapeDtypeStruct((M, N), out_dtype), scratch_shapes=[pltpu.VMEM((tm, tn), f32)] if nk > 1 else [],
        compiler_params=_cp(("parallel", "parallel", "arbitrary")))(*ins)


def _rw_specs(params, ins, ncol, tm):
    specs = []
    for arr, mode, bw, coff in params:
        if mode == "c":
            specs.append(pl.BlockSpec((1, bw), lambda c, r, coff=coff: (0, c + coff)))
        else:
            specs.append(pl.BlockSpec((1, bw), lambda c, r, coff=coff: (0, coff)))
    for arr, mode, bw, coff in ins:
        if mode == "c":
            specs.append(pl.BlockSpec((tm, bw), lambda c, r, coff=coff: (r, c + coff)))
        else:
            specs.append(pl.BlockSpec((tm, bw), lambda c, r, coff=coff: (r, coff)))
    return specs


def _norm_spec(lst):
    out = []
    for t in lst:
        arr, mode, bw = t[0], t[1], t[2]
        coff = t[3] if len(t) > 3 else 0
        out.append((arr, mode, bw, coff))
    return out


def _rowwise(fn, params, ins, outs, *, name, ncol=1, tm=None):
    params, ins = _norm_spec(params), _norm_spec(ins)
    S = ins[0][0].shape[0]
    tm = _tile(S, tm or ROW_TILE, 8)
    npar, nin = len(params), len(ins)

    def body(*refs):
        pv = [r[...].astype(f32) for r in refs[:npar]]
        iv = [r[...].astype(f32) for r in refs[npar:npar + nin]]
        res = fn(*pv, *iv)
        for o_ref, val in zip(refs[npar + nin:], res):
            o_ref[...] = val.astype(o_ref.dtype)

    out_specs, out_shapes = [], []
    for W, dt, mode, bw in outs:
        out_shapes.append(jax.ShapeDtypeStruct((S, W), dt))
        if mode == "c":
            out_specs.append(pl.BlockSpec((tm, bw), lambda c, r: (r, c)))
        else:
            out_specs.append(pl.BlockSpec((tm, bw), lambda c, r: (r, 0)))
    return pl.pallas_call(
        body, name=name, grid=(ncol, S // tm), in_specs=_rw_specs(params, ins, ncol, tm), out_specs=out_specs,
        out_shape=out_shapes, compiler_params=_cp(("parallel", "parallel")))(*[p[0] for p in params], *[i[0] for i in ins])


def _rowwise_bwd(fn, params, ins, couts, *, name, ncol=1, tm=None, diff_p=None, diff_i=None, din_dtypes=None):
    params, ins, couts = _norm_spec(params), _norm_spec(ins), _norm_spec(couts)
    S = ins[0][0].shape[0]
    tm = _tile(S, tm or ROW_TILE, 8)
    npar, nin, nco = len(params), len(ins), len(couts)
    diff_p = list(range(npar)) if diff_p is None else diff_p
    diff_i = list(range(nin)) if diff_i is None else diff_i
    din_dtypes = [(f32,)] * len(diff_i) if din_dtypes is None else din_dtypes

    def body(*refs):
        c, r = pl.program_id(0), pl.program_id(1)
        pv = [x[...].astype(f32) for x in refs[:npar]]
        iv = [x[...].astype(f32) for x in refs[npar:npar + nin]]
        cv = [x[...].astype(f32) for x in refs[npar + nin:npar + nin + nco]]
        orefs = refs[npar + nin + nco:]

        def g(*dargs):
            p2, i2 = list(pv), list(iv)
            for n, k in enumerate(diff_p):
                p2[k] = dargs[n]
            for n, k in enumerate(diff_i):
                i2[k] = dargs[len(diff_p) + n]
            return tuple(fn(*p2, *i2))

        _, vjp = jax.vjp(g, *[pv[k] for k in diff_p], *[iv[k] for k in diff_i])
        grads = vjp(tuple(cv))
        for n, k in enumerate(diff_p):
            o_ref = orefs[n]
            first = (r == 0) if params[k][1] == "c" else jnp.logical_and(r == 0, c == 0)

            @pl.when(first)
            def _(o_ref=o_ref):
                o_ref[...] = jnp.zeros_like(o_ref)

            o_ref[...] += grads[n]
        pos = len(diff_p)
        for n, k in enumerate(diff_i):
            for _ in din_dtypes[n]:
                orefs[pos][...] = grads[len(diff_p) + n].astype(orefs[pos].dtype)
                pos += 1

    out_specs, out_shapes = [], []
    for k in diff_p:
        arr, mode, bw, coff = params[k]
        W = bw * ncol if mode == "c" else bw
        out_shapes.append(jax.ShapeDtypeStruct((1, W), f32))
        out_specs.append(pl.BlockSpec((1, bw), (lambda c, r: (0, c)) if mode == "c" else (lambda c, r: (0, 0))))
    for n, k in enumerate(diff_i):
        arr, mode, bw, coff = ins[k]
        W = bw * ncol if mode == "c" else bw
        for dt in din_dtypes[n]:
            out_shapes.append(jax.ShapeDtypeStruct((S, W), dt))
            out_specs.append(pl.BlockSpec((tm, bw), (lambda c, r: (r, c)) if mode == "c" else (lambda c, r: (r, 0))))
    res = pl.pallas_call(
        body, name=name, grid=(ncol, S // tm), in_specs=_rw_specs(params, ins + couts, ncol, tm), out_specs=out_specs,
        out_shape=out_shapes, compiler_params=_cp(("arbitrary", "arbitrary")))(
            *[p[0] for p in params], *[i[0] for i in ins], *[c[0] for c in couts])
    return list(res[:len(diff_p)]), list(res[len(diff_p):])


def _ln_fn(g, b, r):
    mu = jnp.mean(r, -1, keepdims=True)
    xc = r - mu
    var = jnp.mean(xc * xc, -1, keepdims=True)
    return (xc * lax.rsqrt(var + LN_EPS) * g + b,)


def _res_ln_fn(g, b, h, y):
    r = _alpha() * h + y
    hn = _ln_fn(g, b, r)
    return (r,) + hn + hn


def _rms_fn(w, x):
    return (x * lax.rsqrt(jnp.mean(x * x, -1, keepdims=True) + RMS_EPS) * w,)


def _ssd_gate_fn(w, y, z):
    yg = y * _silu(z)
    return (yg * lax.rsqrt(jnp.mean(yg * yg, -1, keepdims=True) + RMS_EPS) * w,)


def _mul_silu_fn(o, z):
    return (o * _silu(z),)


def _gdn_gate_fn(w, o, z):
    return (o * lax.rsqrt(jnp.mean(o * o, -1, keepdims=True) + RMS_EPS) * w * _silu(z),)


def _l2_fn(scale, x):
    return (x * lax.rsqrt(jnp.sum(x * x, -1, keepdims=True) + RMS_EPS) * scale,)


def _rope_fn(cos, sin, x):
    half = MLA_ROPE // 2
    i = _iota((LANES, LANES), 0)
    j = _iota((LANES, LANES), 1)
    pm = jnp.where((i == j + half) & (j < half), -1.0, 0.0) + jnp.where((i + half == j) & (j < 2 * half), 1.0, 0.0)
    return (x * cos + _seldot(x, pm.astype(f32), ((1,), (0,)), 1) * sin,)


def _conv_taps(x, K):
    S = x.shape[0]
    rows = _iota(x.shape, 0)
    return [x] + [jnp.where(rows < j, 0.0, pltpu.roll(x, j, 0)) for j in range(1, K)]


def _conv_fwd(x, w, b, *, name):
    S, C = x.shape
    K = w.shape[0]
    cw = _tile(C, LANES)

    def body(x_ref, w_ref, b_ref, o_ref):
        taps = _conv_taps(x_ref[...], K)
        wv = w_ref[...]
        pre = b_ref[...] + taps[0] * wv[K - 1:K, :]
        for j in range(1, K):
            pre = pre + taps[j] * wv[K - 1 - j:K - j, :]
        o_ref[...] = _silu(pre)

    return pl.pallas_call(
        body, name=name, grid=(C // cw,),
        in_specs=[pl.BlockSpec((S, cw), lambda c: (0, c)), pl.BlockSpec((K, cw), lambda c: (0, c)), pl.BlockSpec((1, cw), lambda c: (0, c))],
        out_specs=pl.BlockSpec((S, cw), lambda c: (0, c)), out_shape=jax.ShapeDtypeStruct((S, C), f32),
        compiler_params=_cp(("parallel",)))(x, w, b)


def _conv_bwd(x, w, b, dy, *, name, dx_dtype=f32):
    S, C = x.shape
    K = w.shape[0]
    cw = _tile(C, LANES)

    def body(x_ref, w_ref, b_ref, dy_ref, dx_ref, dw_ref, db_ref):
        taps = _conv_taps(x_ref[...], K)
        wv = w_ref[...]
        pre = b_ref[...] + taps[0] * wv[K - 1:K, :]
        for j in range(1, K):
            pre = pre + taps[j] * wv[K - 1 - j:K - j, :]
        sg = jax.nn.sigmoid(pre)
        dpre = dy_ref[...] * sg * (1.0 + pre * (1.0 - sg))
        db_ref[...] = jnp.sum(dpre, axis=0, keepdims=True)
        rows = _iota(dpre.shape, 0)
        dx = dpre * wv[K - 1:K, :]
        dw_ref[K - 1:K, :] = jnp.sum(dpre * taps[0], axis=0, keepdims=True)
        for j in range(1, K):
            dw_ref[K - 1 - j:K - j, :] = jnp.sum(dpre * taps[j], axis=0, keepdims=True)
            up = jnp.where(rows >= S - j, 0.0, pltpu.roll(dpre, S - j, 0))
            dx = dx + up * wv[K - 1 - j:K - j, :]
        dx_ref[...] = dx.astype(dx_dtype)

    col = lambda c: (0, c)
    return pl.pallas_call(
        body, name=name, grid=(C // cw,),
        in_specs=[pl.BlockSpec((S, cw), col), pl.BlockSpec((K, cw), col), pl.BlockSpec((1, cw), col), pl.BlockSpec((S, cw), col)],
        out_specs=[pl.BlockSpec((S, cw), col), pl.BlockSpec((K, cw), col), pl.BlockSpec((1, cw), col)],
        out_shape=[jax.ShapeDtypeStruct((S, C), dx_dtype), jax.ShapeDtypeStruct((K, C), f32), jax.ShapeDtypeStruct((1, C), f32)],
        compiler_params=_cp(("parallel",)))(x, w, b, dy)


def _ssd_chunk(prev, xs, Bm, Cm, dtr, dtb, alog, dsk, g, *, R, P):
    L, GW = xs.shape
    H = dtr.shape[1]
    tril = _iota((L, L), 0) >= _iota((L, L), 1)
    dt = _softplus(dtr + dtb)
    acs = _seldot(tril.astype(f32), dt * (-jnp.exp(alog)), ((1,), (0,)), 0)
    expand = (_iota((H, GW), 0) == g * R + _div_pow2(_iota((H, GW), 1), P)).astype(f32)
    dt_e = _seldot(dt, expand, ((1,), (0,)), 1)
    acs_e = _seldot(acs, expand, ((1,), (0,)), 1)
    d_e = jnp.sum(_seldot(jnp.broadcast_to(dsk, (8, H)), expand, ((1,), (0,)), 1), axis=0, keepdims=True) * 0.125
    last = jnp.sum(jnp.where(_iota((L, GW), 0) == L - 1, acs_e, 0.0), axis=0, keepdims=True)
    xdt = xs * dt_e
    cb = _dot(Cm, Bm, _NT)
    nsel = max(R, 8)
    sel = (_iota((nsel, H), 1) == g * R + _iota((nsel, H), 0)).astype(f32)
    acs_t = _seldot(sel, acs, _NT, 0)
    hp = LANES // P
    pieces = []
    for p in range(GW // LANES):
        xp = xdt[:, p * LANES:(p + 1) * LANES]
        acc = None
        for q in range(hp):
            r = p * hp + q
            col = jnp.sum(jnp.where(_iota((L, H), 1) == g * R + r, acs, 0.0), axis=1, keepdims=True)
            row = jnp.sum(jnp.where(_iota((nsel, L), 0) == r, acs_t, 0.0), axis=0, keepdims=True)
            dec = jnp.where(tril, jnp.exp(jnp.where(tril, col - row, 0.0)), 0.0)
            xm = jnp.where(_div_pow2(_iota((L, LANES), 1), P) == q, xp, 0.0)
            t = _dot(cb * dec, xm)
            acc = t if acc is None else acc + t
        pieces.append(acc)
    y_diag = pieces[0] if len(pieces) == 1 else jnp.concatenate(pieces, axis=1)
    st = _dot(Bm, xdt * jnp.exp(last - acs_e), _TN)
    y_off = _dot(Cm, prev) * jnp.exp(acs_e)
    new = prev * jnp.exp(last) + st
    return y_diag + y_off + xs * d_e, new


def _ssd_dims(xbc, dtr):
    S, CD = xbc.shape
    H = dtr.shape[1]
    G, N, P = SSD_N_GROUPS, SSD_D_STATE, SSD_HEAD_DIM
    DI = H * P
    R = H // G
    assert CD == DI + 2 * G * N and DI % N == 0
    return S, H, G, N, P, DI, R, R * P, SSD_CHUNK


def _ssd_scan_fwd(xbc, dtr, dtb, alog, dsk, *, name, comm=None):
    S, H, G, N, P, DI, R, GW, L = _ssd_dims(xbc, dtr)
    nc = S // L
    boff, coff = DI // N, DI // N + G

    def body(xs_ref, b_ref, c_ref, dtr_ref, dtb_ref, alog_ref, dsk_ref, y_ref, st_ref, state):
        c, g = pl.program_id(0), pl.program_id(1)

        @pl.when(c == 0)
        def _():
            state[g] = jnp.zeros((N, GW), f32)

        prev = state[g]
        st_ref[0, 0] = prev
        y, new = _ssd_chunk(prev, xs_ref[...], b_ref[...], c_ref[...], dtr_ref[...], dtb_ref[...], alog_ref[...],
                            dsk_ref[...], g, R=R, P=P)
        y_ref[...] = y
        state[g] = new

    par = pl.BlockSpec((1, H), lambda c, g: (0, 0))
    return _call(
        body, comm, name=name, grid=(nc, G),
        in_specs=[pl.BlockSpec((L, GW), lambda c, g: (c, g)), pl.BlockSpec((L, N), lambda c, g: (c, boff + g)),
                  pl.BlockSpec((L, N), lambda c, g: (c, coff + g)), pl.BlockSpec((L, H), lambda c, g: (c, 0)), par, par, par],
        out_specs=[pl.BlockSpec((L, GW), lambda c, g: (c, g)), pl.BlockSpec((1, 1, N, GW), lambda c, g: (c, g, 0, 0))],
        out_shape=[jax.ShapeDtypeStruct((S, DI), f32), jax.ShapeDtypeStruct((nc, G, N, GW), f32)],
        scratch_shapes=[pltpu.VMEM((G, N, GW), f32)],
        sem=("arbitrary", "arbitrary"), args=(xbc, xbc, xbc, dtr, dtb, alog, dsk))


def _ssd_scan_bwd(xbc, dtr, dtb, alog, dsk, states, dy, *, name, comm=None):
    S, H, G, N, P, DI, R, GW, L = _ssd_dims(xbc, dtr)
    nc = S // L
    boff, coff = DI // N, DI // N + G

    def body(xs_ref, b_ref, c_ref, dtr_ref, dtb_ref, alog_ref, dsk_ref, st_ref, dy_ref,
             dxs_ref, db_ref, dc_ref, ddtr_ref, ddtb_ref, dalog_ref, ddsk_ref, dstate):
        c, g = pl.program_id(0), pl.program_id(1)

        @pl.when(c == 0)
        def _():
            dstate[g] = jnp.zeros((N, GW), f32)

        @pl.when(jnp.logical_and(c == 0, g == 0))
        def _():
            ddtb_ref[...] = jnp.zeros_like(ddtb_ref)
            dalog_ref[...] = jnp.zeros_like(dalog_ref)
            ddsk_ref[...] = jnp.zeros_like(ddsk_ref)

        @pl.when(g == 0)
        def _():
            ddtr_ref[...] = jnp.zeros_like(ddtr_ref)

        fn = functools.partial(_ssd_chunk, g=g, R=R, P=P)
        _, vjp = jax.vjp(fn, st_ref[0, 0], xs_ref[...], b_ref[...], c_ref[...], dtr_ref[...], dtb_ref[...],
                         alog_ref[...], dsk_ref[...])
        dprev, dxs, dB, dC, ddtr, ddtb, dalog, ddsk = vjp((dy_ref[...], dstate[g]))
        dstate[g] = dprev
        dxs_ref[...] = dxs
        db_ref[...] = dB
        dc_ref[...] = dC
        ddtr_ref[...] += ddtr
        ddtb_ref[...] += ddtb
        dalog_ref[...] += dalog
        ddsk_ref[...] += ddsk

    rc = lambda c: nc - 1 - c
    par = pl.BlockSpec((1, H), lambda c, g: (0, 0))
    return _call(
        body, comm, name=name, grid=(nc, G),
        in_specs=[pl.BlockSpec((L, GW), lambda c, g: (rc(c), g)), pl.BlockSpec((L, N), lambda c, g: (rc(c), boff + g)),
                  pl.BlockSpec((L, N), lambda c, g: (rc(c), coff + g)), pl.BlockSpec((L, H), lambda c, g: (rc(c), 0)),
                  par, par, par, pl.BlockSpec((1, 1, N, GW), lambda c, g: (rc(c), g, 0, 0)),
                  pl.BlockSpec((L, GW), lambda c, g: (rc(c), g))],
        out_specs=[pl.BlockSpec((L, GW), lambda c, g: (rc(c), g)), pl.BlockSpec((L, N), lambda c, g: (rc(c), g)),
                   pl.BlockSpec((L, N), lambda c, g: (rc(c), g)), pl.BlockSpec((L, H), lambda c, g: (rc(c), 0)), par, par, par],
        out_shape=[jax.ShapeDtypeStruct((S, DI), f32), jax.ShapeDtypeStruct((S, G * N), f32), jax.ShapeDtypeStruct((S, G * N), f32),
                   jax.ShapeDtypeStruct((S, H), f32)] + [jax.ShapeDtypeStruct((1, H), f32)] * 3,
        scratch_shapes=[pltpu.VMEM((G, N, GW), f32)],
        sem=("arbitrary", "arbitrary"), args=(xbc, xbc, xbc, dtr, dtb, alog, dsk, states, dy))


def _dot3(a, b, dims=((1,), (0,))):
    def split(x):
        hi = x.astype(jnp.bfloat16)
        return hi, (x - hi.astype(f32)).astype(jnp.bfloat16)

    def d(x, y):
        return lax.dot_general(x, y, (dims, ((), ())), preferred_element_type=f32)

    ah, al = split(a)
    bh, bl = split(b)
    return d(ah, bh) + (d(ah, bl) + d(al, bh))


def _neumann_inverses(As):
    L = As[0].shape[0]
    eye = (_iota((L, L), 0) == _iota((L, L), 1)).astype(f32)
    X = [-A for A in As]
    P = [eye + x for x in X]
    n = 1
    while 2 * n < L:
        X = [_dot3(x, x) for x in X]
        P = [p + _dot3(p, x) for p, x in zip(P, X)]
        n *= 2
    return P


@jax.custom_vjp
def _unit_lower_solves(As, Rs):
    return tuple(_dot3(T, R) for T, R in zip(_neumann_inverses(As), Rs))


def _uls_fwd(As, Rs):
    Ts = _neumann_inverses(As)
    Xs = tuple(_dot3(T, R) for T, R in zip(Ts, Rs))
    return Xs, (tuple(Ts), Xs)


def _uls_bwd(res, dXs):
    Ts, Xs = res
    dRs = tuple(_dot3(T, dX, _TN) for T, dX in zip(Ts, dXs))
    dAs = tuple(-_dot3(dR, X, _NT) for dR, X in zip(dRs, Xs))
    return dAs, dRs


_unit_lower_solves.defvjp(_uls_fwd, _uls_bwd)


def _gdn_step(states, qb, kb_, vb, br, ar, alog, dtb, h0, *, rep):
    HB = len(states)
    L = qb.shape[0]
    DK, DV = states[0].shape
    HV = br.shape[1]
    incl = _iota((L, L), 0) >= _iota((L, L), 1)
    strict = _iota((L, L), 0) > _iota((L, L), 1)
    lane = _iota((L, HV), 1)
    g_all = -jnp.exp(alog) * _softplus(ar + dtb)
    gcs = _seldot(incl.astype(f32), g_all, ((1,), (0,)), 0)
    beta_all = jax.nn.sigmoid(br)
    nsel = max(HV, 8)
    gcs_t = _seldot((_iota((nsel, HV), 0) == _iota((nsel, HV), 1)).astype(f32), gcs, _NT, 0)
    hs = range(HB)
    q = [qb[:, (hh // rep) * DK:(hh // rep + 1) * DK] for hh in hs]
    k = [kb_[:, (hh // rep) * DK:(hh // rep + 1) * DK] for hh in hs]
    v = [vb[:, hh * DV:(hh + 1) * DV] for hh in hs]
    gc = [jnp.sum(jnp.where(lane == h0 + hh, gcs, 0.0), axis=1, keepdims=True) for hh in hs]
    beta = [jnp.sum(jnp.where(lane == h0 + hh, beta_all, 0.0), axis=1, keepdims=True) for hh in hs]
    gc_row = [jnp.sum(jnp.where(_iota((nsel, L), 0) == h0 + hh, gcs_t, 0.0), axis=0, keepdims=True) for hh in hs]
    decay = [jnp.where(incl, jnp.exp(jnp.where(incl, gc[hh] - gc_row[hh], 0.0)), 0.0) for hh in hs]
    kbeta = [k[hh] * beta[hh] for hh in hs]
    a_mat = [jnp.where(strict, _dot(kbeta[hh], k[hh], _NT) * decay[hh], 0.0) for hh in hs]
    eg = [jnp.exp(gc[hh]) for hh in hs]
    sol = _unit_lower_solves(tuple(a_mat), tuple(jnp.concatenate([v[hh] * beta[hh], kbeta[hh] * eg[hh]], axis=1) for hh in hs))
    qk = [jnp.where(incl, _dot(q[hh], k[hh], _NT) * decay[hh], 0.0) for hh in hs]
    g_last = [jnp.sum(jnp.where(_iota((L, 1), 0) == L - 1, gc[hh], 0.0), axis=0, keepdims=True) for hh in hs]
    v_new = [sol[hh][:, :DV] - _dot(sol[hh][:, DV:], states[hh]) for hh in hs]
    outs = [_dot(q[hh] * eg[hh], states[hh]) + _dot(qk[hh], v_new[hh]) for hh in hs]
    news = [states[hh] * jnp.exp(g_last[hh]) + _dot(k[hh] * jnp.exp(g_last[hh] - gc[hh]), v_new[hh], _TN) for hh in hs]
    return (outs[0] if HB == 1 else jnp.concatenate(outs, axis=1)), tuple(news)


def _gdn_dims():
    HK, HV = GDN_N_QK_HEADS, GDN_N_V_HEADS
    rep = HV // HK
    HB = min(GDN_HEADS_PER_STEP, HV)
    assert HV % HB == 0 and HB % rep == 0
    return HK, HV, GDN_DK, GDN_DV, GDN_CHUNK, rep, HB


def _gdn_scan_fwd(qkn, qkv, br, ar, alog, dtb, *, name, comm=None):
    S = qkn.shape[0]
    HK, HV, DK, DV, L, rep, HB = _gdn_dims()
    nc = S // L
    QW = HB // rep * DK
    koff = HK * DK // QW
    voff = 2 * HK * DK // (HB * DV)

    def body(q_ref, k_ref, v_ref, br_ref, ar_ref, alog_ref, dtb_ref, o_ref, st_ref, state):
        c, hb = pl.program_id(0), pl.program_id(1)
        h0 = hb * HB

        @pl.when(c == 0)
        def _():
            for hh in range(HB):
                state[h0 + hh] = jnp.zeros((DK, DV), f32)

        prev = tuple(state[h0 + hh] for hh in range(HB))
        for hh in range(HB):
            st_ref[0, hh] = prev[hh]
        o, new = _gdn_step(prev, q_ref[...], k_ref[...], v_ref[...], br_ref[...], ar_ref[...], alog_ref[...], dtb_ref[...],
                           h0, rep=rep)
        o_ref[...] = o
        for hh in range(HB):
            state[h0 + hh] = new[hh]

    par = pl.BlockSpec((1, HV), lambda c, h: (0, 0))
    return _call(
        body, comm, name=name, grid=(nc, HV // HB),
        in_specs=[pl.BlockSpec((L, QW), lambda c, h: (c, h)), pl.BlockSpec((L, QW), lambda c, h: (c, koff + h)),
                  pl.BlockSpec((L, HB * DV), lambda c, h: (c, voff + h)), pl.BlockSpec((L, HV), lambda c, h: (c, 0)),
                  pl.BlockSpec((L, HV), lambda c, h: (c, 0)), par, par],
        out_specs=[pl.BlockSpec((L, HB * DV), lambda c, h: (c, h)), pl.BlockSpec((1, HB, DK, DV), lambda c, h: (c, h, 0, 0))],
        out_shape=[jax.ShapeDtypeStruct((S, HV * DV), f32), jax.ShapeDtypeStruct((nc, HV, DK, DV), f32)],
        scratch_shapes=[pltpu.VMEM((HV, DK, DV), f32)],
        sem=("arbitrary", "arbitrary"), args=(qkn, qkn, qkv, br, ar, alog, dtb))


def _gdn_scan_bwd(qkn, qkv, br, ar, alog, dtb, states, do, *, name, comm=None):
    S = qkn.shape[0]
    HK, HV, DK, DV, L, rep, HB = _gdn_dims()
    nc = S // L
    QW = HB // rep * DK
    koff = HK * DK // QW
    voff = 2 * HK * DK // (HB * DV)

    def body(q_ref, k_ref, v_ref, br_ref, ar_ref, alog_ref, dtb_ref, st_ref, do_ref,
             dq_ref, dk_ref, dv_ref, dbr_ref, dar_ref, dalog_ref, ddtb_ref, dstate):
        c, hb = pl.program_id(0), pl.program_id(1)
        h0 = hb * HB

        @pl.when(c == 0)
        def _():
            for hh in range(HB):
                dstate[h0 + hh] = jnp.zeros((DK, DV), f32)

        @pl.when(jnp.logical_and(c == 0, hb == 0))
        def _():
            dalog_ref[...] = jnp.zeros_like(dalog_ref)
            ddtb_ref[...] = jnp.zeros_like(ddtb_ref)

        @pl.when(hb == 0)
        def _():
            dbr_ref[...] = jnp.zeros_like(dbr_ref)
            dar_ref[...] = jnp.zeros_like(dar_ref)

        fn = functools.partial(_gdn_step, h0=h0, rep=rep)
        prev = tuple(st_ref[0, hh] for hh in range(HB))
        _, vjp = jax.vjp(fn, prev, q_ref[...], k_ref[...], v_ref[...], br_ref[...], ar_ref[...], alog_ref[...], dtb_ref[...])
        dprev, dq, dk, dv, dbr, dar, dalog, ddtb = vjp((do_ref[...], tuple(dstate[h0 + hh] for hh in range(HB))))
        for hh in range(HB):
            dstate[h0 + hh] = dprev[hh]
        dq_ref[...] = dq
        dk_ref[...] = dk
        dv_ref[...] = dv
        dbr_ref[...] += dbr
        dar_ref[...] += dar
        dalog_ref[...] += dalog
        ddtb_ref[...] += ddtb

    rc = lambda c: nc - 1 - c
    par = pl.BlockSpec((1, HV), lambda c, h: (0, 0))
    blk = lambda W: pl.BlockSpec((L, W), lambda c, h: (rc(c), h))
    return _call(
        body, comm, name=name, grid=(nc, HV // HB),
        in_specs=[pl.BlockSpec((L, QW), lambda c, h: (rc(c), h)), pl.BlockSpec((L, QW), lambda c, h: (rc(c), koff + h)),
                  pl.BlockSpec((L, HB * DV), lambda c, h: (rc(c), voff + h)), pl.BlockSpec((L, HV), lambda c, h: (rc(c), 0)),
                  pl.BlockSpec((L, HV), lambda c, h: (rc(c), 0)), par, par,
                  pl.BlockSpec((1, HB, DK, DV), lambda c, h: (rc(c), h, 0, 0)), blk(HB * DV)],
        out_specs=[blk(QW), blk(QW), blk(HB * DV), pl.BlockSpec((L, HV), lambda c, h: (rc(c), 0)),
                   pl.BlockSpec((L, HV), lambda c, h: (rc(c), 0)), par, par],
        out_shape=[jax.ShapeDtypeStruct((S, HK * DK), f32), jax.ShapeDtypeStruct((S, HK * DK), f32), jax.ShapeDtypeStruct((S, HV * DV), f32),
                   jax.ShapeDtypeStruct((S, HV), f32), jax.ShapeDtypeStruct((S, HV), f32),
                   jax.ShapeDtypeStruct((1, HV), f32), jax.ShapeDtypeStruct((1, HV), f32)],
        scratch_shapes=[pltpu.VMEM((HV, DK, DV), f32)],
        sem=("arbitrary", "arbitrary"), args=(qkn, qkn, qkv, br, ar, alog, dtb, states, do))


def _att_scale():
    return (MLA_NOPE + MLA_ROPE) ** -0.5


def _causal(s, i, j, t):
    qpos = i * t + _iota(s.shape, 0)
    kpos = j * t + _iota(s.shape, 1)
    return kpos <= qpos


def _attn_fwd(qn, qr, kn, kr, v, *, name, comm=None):
    S, W = qn.shape
    H = W // LANES
    t = _tile(S, ATT_TILE)
    scale = _att_scale()

    def body(qn_ref, qr_ref, kn_ref, kr_ref, v_ref, o_ref, lse_ref):
        i = pl.program_id(1)
        qc = jnp.concatenate([qn_ref[...], qr_ref[...]], axis=1)

        def step(j, carry, masked):
            m, l, acc = carry
            rows = pl.ds(pl.multiple_of(j * t, t), t)
            s = _dot(qc, jnp.concatenate([kn_ref[rows, :], kr_ref[rows, :]], axis=1), _NT) * scale
            if masked:
                s = jnp.where(_causal(s, i, j, t), s, -1e30)
            m_new = jnp.maximum(m, jnp.max(s, axis=1, keepdims=True))
            p = jnp.exp(s - m_new)
            a = jnp.exp(m - m_new)
            return m_new, a * l + jnp.sum(p, axis=1, keepdims=True), a * acc + _dot(p, v_ref[rows, :])

        carry = lax.fori_loop(0, i, functools.partial(step, masked=False),
                              (jnp.full((t, 1), -1e30, f32), jnp.zeros((t, 1), f32), jnp.zeros((t, LANES), f32)))
        m, l, acc = step(i, carry, True)
        o_ref[...] = acc / l
        lse_ref[...] = jnp.broadcast_to(m + jnp.log(l), (t, LANES))

    qb = pl.BlockSpec((t, LANES), lambda h, i: (i, h))
    kb = pl.BlockSpec((S, LANES), lambda h, i: (0, h))
    return _call(
        body, comm, name=name, grid=(H, S // t),
        in_specs=[qb, qb, kb, pl.BlockSpec((S, LANES), lambda h, i: (0, 0)), kb],
        out_specs=[qb, qb], out_shape=[jax.ShapeDtypeStruct((S, W), f32), jax.ShapeDtypeStruct((S, W), f32)],
        scratch_shapes=[], sem=("parallel", "arbitrary"), args=(qn, qr, kn, kr, v))


def _attn_bwd_dq(qn, qr, kn, kr, v, o, lse, do, *, name):
    S, W = qn.shape
    H = W // LANES
    t = _tile(S, ATT_TILE)
    scale = _att_scale()

    def body(qn_ref, qr_ref, kn_ref, kr_ref, v_ref, o_ref, lse_ref, do_ref, dqn_ref, dqr_ref):
        i = pl.program_id(1)
        qc = jnp.concatenate([qn_ref[...], qr_ref[...]], axis=1)
        dov = do_ref[...]
        delta = jnp.sum(dov * o_ref[...], axis=1, keepdims=True)
        lsev = lse_ref[...][:, :1]

        def step(j, dq, masked):
            rows = pl.ds(pl.multiple_of(j * t, t), t)
            kc = jnp.concatenate([kn_ref[rows, :], kr_ref[rows, :]], axis=1)
            s = _dot(qc, kc, _NT) * scale
            p = jnp.exp(s - lsev)
            if masked:
                p = jnp.where(_causal(s, i, j, t), p, 0.0)
            ds = p * (_dot(dov, v_ref[rows, :], _NT) - delta) * scale
            return dq + _dot(ds, kc)

        dq = lax.fori_loop(0, i, functools.partial(step, masked=False), jnp.zeros((t, 2 * LANES), f32))
        dq = step(i, dq, True)
        dqn_ref[...] = dq[:, :LANES].astype(MXU_DT)
        dqr_ref[...] = dq[:, LANES:]

    qb = pl.BlockSpec((t, LANES), lambda h, i: (i, h))
    kb = pl.BlockSpec((S, LANES), lambda h, i: (0, h))
    return pl.pallas_call(
        body, name=name, grid=(H, S // t),
        in_specs=[qb, qb, kb, pl.BlockSpec((S, LANES), lambda h, i: (0, 0)), kb, qb, qb, qb],
        out_specs=[qb, qb], out_shape=[jax.ShapeDtypeStruct((S, W), MXU_DT), jax.ShapeDtypeStruct((S, W), f32)],
        compiler_params=_cp(("parallel", "arbitrary")))(qn, qr, kn, kr, v, o, lse, do)


def _attn_bwd_dkv(qn, qr, kn, kr, v, o, lse, do, *, name, comm=None):
    S, W = qn.shape
    H = W // LANES
    t = _tile(S, ATT_TILE)
    nb = S // t
    scale = _att_scale()

    def body(qn_ref, qr_ref, kn_ref, kr_ref, v_ref, o_ref, lse_ref, do_ref, dkn_ref, dkr_ref, dv_ref):
        j, h = pl.program_id(0), pl.program_id(1)
        kc = jnp.concatenate([kn_ref[...], kr_ref[...]], axis=1)
        vv = v_ref[...]

        def step(i, carry, masked):
            dk, dv = carry
            rows = pl.ds(pl.multiple_of(i * t, t), t)
            qc = jnp.concatenate([qn_ref[rows, :], qr_ref[rows, :]], axis=1)
            dov = do_ref[rows, :]
            delta = jnp.sum(dov * o_ref[rows, :], axis=1, keepdims=True)
            s = _dot(qc, kc, _NT) * scale
            p = jnp.exp(s - lse_ref[rows, :][:, :1])
            if masked:
                p = jnp.where(_causal(s, i, j, t), p, 0.0)
            ds = p * (_dot(dov, vv, _NT) - delta) * scale
            return dk + _dot(ds, qc, _TN), dv + _dot(p, dov, _TN)

        carry = step(j, (jnp.zeros((t, 2 * LANES), f32), jnp.zeros((t, LANES), f32)), True)
        dk, dv = lax.fori_loop(j + 1, nb, functools.partial(step, masked=False), carry)
        dkn_ref[...] = dk[:, :LANES].astype(MXU_DT)
        dv_ref[...] = dv.astype(MXU_DT)

        @pl.when(h == 0)
        def _():
            dkr_ref[...] = jnp.zeros_like(dkr_ref)

        dkr_ref[...] += dk[:, LANES:]

    full = pl.BlockSpec((S, LANES), lambda j, h: (0, h))
    kb = pl.BlockSpec((t, LANES), lambda j, h: (j, h))
    k0 = pl.BlockSpec((t, LANES), lambda j, h: (j, 0))
    return _call(
        body, comm, name=name, grid=(nb, H),
        in_specs=[full, full, kb, k0, kb, full, full, full],
        out_specs=[kb, k0, kb],
        out_shape=[jax.ShapeDtypeStruct((S, W), MXU_DT), jax.ShapeDtypeStruct((S, LANES), f32), jax.ShapeDtypeStruct((S, W), MXU_DT)],
        scratch_shapes=[], sem=("arbitrary", "arbitrary"), args=(qn, qr, kn, kr, v, o, lse, do))


def _loss_head(y, target, *, name):
    S, D = y.shape
    tm = _tile(S, ROW_TILE, 8)

    def body(y_ref, t_ref, loss_ref, dy_ref):
        @pl.when(pl.program_id(0) == 0)
        def _():
            loss_ref[...] = jnp.zeros_like(loss_ref)

        e = y_ref[...] - t_ref[...]
        dy_ref[...] = e / D
        part = 0.5 * jnp.sum(jnp.mean(e * e, axis=1, keepdims=True), axis=0, keepdims=True)
        loss_ref[...] += jnp.broadcast_to(part, loss_ref.shape)

    rb = pl.BlockSpec((tm, D), lambda r: (r, 0))
    return pl.pallas_call(
        body, name=name, grid=(S // tm,), in_specs=[rb, rb],
        out_specs=[pl.BlockSpec((1, LANES), lambda r: (0, 0)), rb],
        out_shape=[jax.ShapeDtypeStruct((1, LANES), f32), jax.ShapeDtypeStruct((S, D), f32)],
        compiler_params=_cp(("arbitrary",)))(y, target)


def _adamw(w, g, m, v, *, name):
    R, C = w.shape
    tm = _tile(R, max(8, (1 << 19) // max(C, 1) // 8 * 8), 8)

    def body(w_ref, g_ref, m_ref, v_ref, d_ref, nm_ref, nv_ref):
        gv = g_ref[...]
        nm = ADAM_B1 * m_ref[...] + (1.0 - ADAM_B1) * gv
        nv = ADAM_B2 * v_ref[...] + (1.0 - ADAM_B2) * (gv * gv)
        m_hat = nm / (1.0 - ADAM_B1 ** ADAM_STEP)
        v_hat = nv / (1.0 - ADAM_B2 ** ADAM_STEP)
        d_ref[...] = -ADAM_LR * (m_hat / (jnp.sqrt(v_hat) + ADAM_EPS) + ADAM_WD * w_ref[...])
        nm_ref[...] = nm
        nv_ref[...] = nv

    rb = pl.BlockSpec((tm, C), lambda r: (r, 0))
    sh = jax.ShapeDtypeStruct((R, C), f32)
    return pl.pallas_call(body, name=name, grid=(R // tm,), in_specs=[rb] * 4, out_specs=[rb] * 3, out_shape=[sh] * 3,
                          compiler_params=_cp(("parallel",)))(w, g, m, v)


def _me():
    return lax.axis_index("x"), lax.axis_index("y"), lax.axis_index("c")


def _other_chips(mx, my):
    return [(1 - mx, my), (mx, 1 - my), (1 - mx, 1 - my)]


_ANY = pl.BlockSpec(memory_space=pl.ANY)


class _GatherChips:
    def __init__(self, xs):
        self.arrays = list(xs)
        n = len(xs)
        for x in xs:
            assert x.shape[0] % 2 == 0
        self.halves = [x.shape[0] // 2 for x in xs]
        self.out_shapes = [jax.ShapeDtypeStruct((4,) + x.shape, x.dtype) for x in xs]
        self.scratch = [pltpu.SemaphoreType.DMA((n, 6)), pltpu.SemaphoreType.DMA((n, 6))]

    def _sends(self, x_refs, o_refs, send, recv):
        mx, my, mc = _me()
        me = 2 * mx + my
        out = []
        for t, hf in enumerate(self.halves):
            mine = pl.ds(mc * hf, hf)
            for j, (cx, cy) in enumerate(_other_chips(mx, my)):
                out.append(pltpu.make_async_remote_copy(x_refs[t].at[mine], o_refs[t].at[me, mine], send.at[t, j], recv.at[t, j],
                                                        device_id=(cx, cy, mc), device_id_type=MESH))
        return out

    def start(self, x_refs, o_refs, scr):
        for cp in self._sends(x_refs, o_refs, *scr):
            cp.start()

    def finish(self, x_refs, o_refs, scr):
        send, recv = scr
        mx, my, mc = _me()
        chips = _other_chips(mx, my)
        fwd = []
        for t, hf in enumerate(self.halves):
            mine = pl.ds(mc * hf, hf)
            for j, (cx, cy) in enumerate(chips):
                k = 2 * cx + cy
                pltpu.make_async_remote_copy(x_refs[t].at[mine], o_refs[t].at[k, mine], send.at[t, j], recv.at[t, j],
                                             device_id=(cx, cy, mc), device_id_type=MESH).wait_recv()
                cp = pltpu.make_async_remote_copy(o_refs[t].at[k, mine], o_refs[t].at[k, mine], send.at[t, 3 + j], recv.at[t, 3 + j],
                                                  device_id=(mx, my, 1 - mc), device_id_type=MESH)
                cp.start()
                fwd.append(cp)
        for t, hf in enumerate(self.halves):
            theirs = pl.ds((1 - mc) * hf, hf)
            for j, (cx, cy) in enumerate(chips):
                k = 2 * cx + cy
                pltpu.make_async_remote_copy(o_refs[t].at[k, theirs], o_refs[t].at[k, theirs], send.at[t, 3 + j], recv.at[t, 3 + j],
                                             device_id=(mx, my, 1 - mc), device_id_type=MESH).wait_recv()
        for cp in self._sends(x_refs, o_refs, send, recv) + fwd:
            cp.wait_send()


class _ScatterChips:
    def __init__(self, ps):
        self.arrays = list(ps)
        n = len(ps)
        self.out_shapes = [jax.ShapeDtypeStruct((3,) + p.shape[1:], p.dtype) for p in ps]
        self.scratch = [pltpu.SemaphoreType.DMA((n, 3)), pltpu.SemaphoreType.DMA((n, 3))]

    def _copies(self, p_refs, o_refs, send, recv):
        mx, my, mc = _me()
        return [pltpu.make_async_remote_copy(p_refs[t].at[2 * cx + cy], o_refs[t].at[j], send.at[t, j], recv.at[t, j],
                                             device_id=(cx, cy, mc), device_id_type=MESH)
                for t in range(len(self.arrays)) for j, (cx, cy) in enumerate(_other_chips(mx, my))]

    def start(self, p_refs, o_refs, scr):
        for cp in self._copies(p_refs, o_refs, *scr):
            cp.start()

    def finish(self, p_refs, o_refs, scr):
        for cp in self._copies(p_refs, o_refs, *scr):
            cp.wait()


def _run_comm(comm, *, name):
    n = len(comm.arrays)

    def body(*refs):
        ins, outs, scr = refs[:n], refs[n:2 * n], refs[2 * n:]
        comm.start(ins, outs, scr)
        comm.finish(ins, outs, scr)

    return pl.pallas_call(body, name=name, in_specs=[_ANY] * n, out_specs=[_ANY] * n, out_shape=comm.out_shapes,
                          scratch_shapes=comm.scratch, compiler_params=pltpu.CompilerParams(has_side_effects=True))(*comm.arrays)


def _call(body, comm, *, name, grid, in_specs, out_specs, out_shape, scratch_shapes, sem, args):
    if comm is None:
        res = pl.pallas_call(body, name=name, grid=grid, in_specs=in_specs, out_specs=out_specs, out_shape=out_shape,
                             scratch_shapes=scratch_shapes, compiler_params=_cp(sem))(*args)
        return list(res), None
    n_in, n_out, n_scr, nc = len(in_specs), len(out_specs), len(scratch_shapes), len(comm.arrays)

    def wrapped(*refs):
        ins, cins = refs[:n_in], refs[n_in:n_in + nc]
        outs, couts = refs[n_in + nc:n_in + nc + n_out], refs[n_in + nc + n_out:n_in + 2 * nc + n_out]
        scr, cscr = refs[n_in + 2 * nc + n_out:n_in + 2 * nc + n_out + n_scr], refs[n_in + 2 * nc + n_out + n_scr:]
        ids = [pl.program_id(d) for d in range(len(grid))]
        first = functools.reduce(jnp.logical_and, [i == 0 for i in ids])
        last = functools.reduce(jnp.logical_and, [i == g - 1 for i, g in zip(ids, grid)])

        @pl.when(first)
        def _():
            comm.start(cins, couts, cscr)

        body(*ins, *outs, *scr)

        @pl.when(last)
        def _():
            comm.finish(cins, couts, cscr)

    res = pl.pallas_call(
        wrapped, name=name, grid=grid, in_specs=list(in_specs) + [_ANY] * nc, out_specs=list(out_specs) + [_ANY] * nc,
        out_shape=list(out_shape) + comm.out_shapes, scratch_shapes=list(scratch_shapes) + comm.scratch,
        compiler_params=_cp(("arbitrary",) * len(grid)))(*args, *comm.arrays)
    return list(res[:n_out]), list(res[n_out:])


def _pair_send_halves(gs, *, name):
    n = len(gs)
    halves = [g.shape[1] // 2 for g in gs]

    def body(*refs):
        g_refs, o_refs = refs[:n], refs[n:2 * n]
        send, recv = refs[2 * n:]
        mx, my, mc = _me()
        cps = []
        for t in range(n):
            hf = halves[t]
            theirs = pl.ds((1 - mc) * hf, hf)
            for k in range(4):
                cp = pltpu.make_async_remote_copy(g_refs[t].at[k, theirs], o_refs[t].at[k], send.at[t, k], recv.at[t, k],
                                                  device_id=(mx, my, 1 - mc), device_id_type=MESH)
                cp.start()
                cps.append(cp)
        for cp in cps:
            cp.wait()

    return pl.pallas_call(
        body, name=name, in_specs=[_ANY] * n, out_specs=[_ANY] * n,
        out_shape=[jax.ShapeDtypeStruct((4, g.shape[1] // 2, g.shape[2]), g.dtype) for g in gs],
        scratch_shapes=[pltpu.SemaphoreType.DMA((n, 4)), pltpu.SemaphoreType.DMA((n, 4))],
        compiler_params=pltpu.CompilerParams(has_side_effects=True))(*gs)


def _pair_exchange_halves(fs, *, name):
    n = len(fs)

    def body(*refs):
        f_refs, o_refs = refs[:n], refs[n:2 * n]
        send, recv = refs[2 * n:]
        mx, my, mc = _me()
        cps = []
        for t in range(n):
            hf = f_refs[t].shape[0]
            mine = pl.ds(mc * hf, hf)
            cp = pltpu.make_async_remote_copy(f_refs[t], o_refs[t].at[mine], send.at[t], recv.at[t],
                                              device_id=(mx, my, 1 - mc), device_id_type=MESH)
            cp.start()
            cps.append(cp)
        for t in range(n):
            hf = f_refs[t].shape[0]
            theirs = pl.ds((1 - mc) * hf, hf)
            cps[t].wait_send()
            pltpu.make_async_remote_copy(f_refs[t], o_refs[t].at[theirs], send.at[t], recv.at[t],
                                         device_id=(mx, my, 1 - mc), device_id_type=MESH).wait_recv()

    return pl.pallas_call(
        body, name=name, in_specs=[_ANY] * n, out_specs=[_ANY] * n,
        out_shape=[jax.ShapeDtypeStruct((2 * f.shape[0], f.shape[1]), f.dtype) for f in fs],
        scratch_shapes=[pltpu.SemaphoreType.DMA((n,)), pltpu.SemaphoreType.DMA((n,))],
        compiler_params=pltpu.CompilerParams(has_side_effects=True))(*fs)


def _allgather_all(x, *, name):
    def body(x_ref, o_ref, send, recv, lsem):
        mx, my, mc = _me()
        me = 4 * mx + 2 * my + mc
        local = pltpu.make_async_copy(x_ref, o_ref.at[me], lsem)
        local.start()
        cps = []
        for j in range(1, 8):
            px, py, pc = mx ^ (j >> 2), my ^ ((j >> 1) & 1), mc ^ (j & 1)
            cp = pltpu.make_async_remote_copy(x_ref, o_ref.at[me], send.at[j - 1], recv.at[j - 1],
                                              device_id=(px, py, pc), device_id_type=MESH)
            cp.start()
            cps.append(cp)
        for j in range(1, 8):
            px, py, pc = mx ^ (j >> 2), my ^ ((j >> 1) & 1), mc ^ (j & 1)
            pltpu.make_async_remote_copy(x_ref, o_ref.at[4 * px + 2 * py + pc], send.at[j - 1], recv.at[j - 1],
                                         device_id=(px, py, pc), device_id_type=MESH).wait_recv()
        for cp in cps:
            cp.wait_send()
        local.wait()

    return pl.pallas_call(
        body, name=name, in_specs=[_ANY], out_specs=_ANY, out_shape=jax.ShapeDtypeStruct((8,) + x.shape, x.dtype),
        scratch_shapes=[pltpu.SemaphoreType.DMA((7,)), pltpu.SemaphoreType.DMA((7,)), pltpu.SemaphoreType.DMA],
        compiler_params=pltpu.CompilerParams(has_side_effects=True))(x)


def _add_half(g4, recv, mc, *, name):
    _, R, C = g4.shape
    hf = R // 2
    tm = _tile(hf, max(16, (1 << 19) // C // 16 * 16), 16)
    nb = hf // tm

    def body(mc_ref, g_ref, r_ref, o_ref, ob_ref):
        s = g_ref[...] + r_ref[...]
        o_ref[...] = s
        ob_ref[...] = s.astype(COMM_DT)

    ospec = pl.BlockSpec((1, tm, C), lambda k, i, mc_ref: (k, i, 0))
    return pl.pallas_call(
        body, name=name,
        grid_spec=pltpu.PrefetchScalarGridSpec(
            num_scalar_prefetch=1, grid=(4, nb),
            in_specs=[pl.BlockSpec((1, tm, C), lambda k, i, mc_ref: (k, mc_ref[0] * nb + i, 0)),
                      pl.BlockSpec((1, tm, C), lambda k, i, mc_ref: (k, i, 0))],
            out_specs=[ospec, ospec]),
        out_shape=[jax.ShapeDtypeStruct((4, hf, C), f32), jax.ShapeDtypeStruct((4, hf, C), COMM_DT)],
        compiler_params=_cp(("parallel", "parallel")))(mc, g4, recv)


def _sum_chips(p4, recv3, me, *, name):
    _, Rh, C = p4.shape
    tm = _tile(Rh, max(16, (1 << 19) // C // 16 * 16), 16)

    def body(me_ref, p_ref, r_ref, o_ref):
        o_ref[...] = ((p_ref[0] + r_ref[0].astype(f32)) + r_ref[1].astype(f32)) + r_ref[2].astype(f32)

    return pl.pallas_call(
        body, name=name,
        grid_spec=pltpu.PrefetchScalarGridSpec(
            num_scalar_prefetch=1, grid=(Rh // tm,),
            in_specs=[pl.BlockSpec((1, tm, C), lambda i, me_ref: (me_ref[0], i, 0)),
                      pl.BlockSpec((3, tm, C), lambda i, me_ref: (0, i, 0))],
            out_specs=pl.BlockSpec((tm, C), lambda i, me_ref: (i, 0))),
        out_shape=jax.ShapeDtypeStruct((Rh, C), f32),
        compiler_params=_cp(("parallel",)))(me, p4, recv3)


def _sum8(x8, *, name):
    _, R, C = x8.shape
    tm = _tile(R, 64, 8)

    def body(x_ref, o_ref):
        acc = x_ref[0]
        for k in range(1, 8):
            acc = acc + x_ref[k]
        o_ref[...] = acc

    return pl.pallas_call(body, name=name, grid=(R // tm,), in_specs=[pl.BlockSpec((8, tm, C), lambda i: (0, i, 0))],
                          out_specs=pl.BlockSpec((tm, C), lambda i: (i, 0)), out_shape=jax.ShapeDtypeStruct((R, C), f32),
                          compiler_params=_cp(("parallel",)))(x8)


def _ssd_layer_fwd(h, W, tag, comm=None):
    z = _mm(h, W["wz"], name=tag + "_z")
    xp = _mm(h, W["wxbc"], name=tag + "_xbc")
    dtr = _mm(h, W["wdt"], name=tag + "_dt")
    xbc = _conv_fwd(xp, W["conv_w"], W["conv_b"], name=tag + "_conv")
    (y, states), cres = _ssd_scan_fwd(xbc, dtr, W["dt_bias"], W["a_log"], W["d"], name=tag + "_scan", comm=comm)
    DI = y.shape[1]
    G = SSD_N_GROUPS
    gs = DI // G
    (yn,) = _rowwise(_ssd_gate_fn, [(W["norm_w"], "c", gs)], [(y, "c", gs), (z, "c", gs)], [(DI, MXU_DT, "c", gs)],
                     name=tag + "_gate", ncol=G, tm=512)
    out = _mm(yn, W["wout"], name=tag + "_out")
    return out, dict(h=h, z=z, xp=xp, dtr=dtr, xbc=xbc, states=states, y=y, yn=yn), cres


def _ssd_layer_bwd(sv, W, dr, drb, tag, comm=None):
    h = sv["h"]
    DI = sv["y"].shape[1]
    G = SSD_N_GROUPS
    gs = DI // G
    gr = {}
    dyn = _mm(drb, W["wout"], tb=True, name=tag + "_dyn")
    gr["wout"] = _mm(sv["yn"], drb, ta=True, name=tag + "_dwout")
    (dnw,), (dy, dz) = _rowwise_bwd(_ssd_gate_fn, [(W["norm_w"], "c", gs)], [(sv["y"], "c", gs), (sv["z"], "c", gs)],
                                     [(dyn, "c", gs)], name=tag + "_dgate", ncol=G, tm=512, din_dtypes=[(f32,), (MXU_DT,)])
    gr["norm_w"] = dnw
    (dxs, dB, dC, ddtr, gr["dt_bias"], gr["a_log"], gr["d"]), cres = _ssd_scan_bwd(
        sv["xbc"], sv["dtr"], W["dt_bias"], W["a_log"], W["d"], sv["states"], dy, name=tag + "_dscan", comm=comm)
    dxbc = jnp.concatenate([dxs, dB, dC], axis=1)
    dxp, gr["conv_w"], gr["conv_b"] = _conv_bwd(sv["xp"], W["conv_w"], W["conv_b"], dxbc, name=tag + "_dconv", dx_dtype=MXU_DT)
    dh = _mm(dz, W["wz"], tb=True, add=dr, add_scale=_alpha(), name=tag + "_dh1")
    dh = _mm(dxp, W["wxbc"], tb=True, add=dh, name=tag + "_dh2")
    dh = _mm(ddtr, W["wdt"], tb=True, add=dh, name=tag + "_dh3")
    gr["wz"] = _mm(h, dz, ta=True, name=tag + "_dwz")
    gr["wxbc"] = _mm(h, dxp, ta=True, name=tag + "_dwxbc")
    gr["wdt"] = _mm(h, ddtr, ta=True, name=tag + "_dwdt")
    return dh, gr, cres


def _mla_layer_fwd(h, W, cos, sin, tag, comm=None):
    QR, KR = W["wqc"].shape[1], W["wkvc"].shape[1]
    HW = W["wqn"].shape[1]
    H = HW // LANES
    qc = _mm(h, W["wqc"], name=tag + "_qc")
    kvc = _mm(h, W["wkvc"], name=tag + "_kvc")
    krp = _mm(h, W["wkr"], name=tag + "_krp")
    z = _mm(h, W["wz"], name=tag + "_z")
    (qcn,) = _rowwise(_rms_fn, [(W["q_norm"], "a", QR)], [(qc, "a", QR)], [(QR, MXU_DT, "a", QR)], name=tag + "_qnorm")
    (kvn,) = _rowwise(_rms_fn, [(W["kv_norm"], "a", KR)], [(kvc, "a", KR)], [(KR, MXU_DT, "a", KR)], name=tag + "_kvnorm")
    qn = _mm(qcn, W["wqn"], name=tag + "_qn", out_dtype=MXU_DT)
    qrp = _mm(qcn, W["wqr"], name=tag + "_qrp")
    kn = _mm(kvn, W["wkn"], name=tag + "_kn", out_dtype=MXU_DT)
    v = _mm(kvn, W["wv"], name=tag + "_v", out_dtype=MXU_DT)
    (qr,) = _rowwise(_rope_fn, [], [(cos, "a", LANES), (sin, "a", LANES), (qrp, "c", LANES)], [(HW, MXU_DT, "c", LANES)],
                     name=tag + "_qrope", ncol=H, tm=1024)
    (kr,) = _rowwise(_rope_fn, [], [(cos, "a", LANES), (sin, "a", LANES), (krp, "a", LANES)], [(LANES, MXU_DT, "a", LANES)],
                     name=tag + "_krope")
    (o, lse), cres = _attn_fwd(qn, qr, kn, kr, v, name=tag + "_attn", comm=comm)
    (og,) = _rowwise(_mul_silu_fn, [], [(o, "a", HW), (z, "a", HW)], [(HW, MXU_DT, "a", HW)], name=tag + "_ogate")
    out = _mm(og, W["wout"], name=tag + "_out")
    return out, dict(h=h, qc=qc, kvc=kvc, z=z, qcn=qcn, kvn=kvn, qn=qn, qr=qr, kn=kn, kr=kr, v=v, o=o, lse=lse, og=og), cres


def _mla_layer_bwd(sv, W, cos, sin, dr, drb, tag, comm=None):
    h = sv["h"]
    QR, KR = W["wqc"].shape[1], W["wkvc"].shape[1]
    HW = W["wqn"].shape[1]
    H = HW // LANES
    gr = {}
    dog = _mm(drb, W["wout"], tb=True, name=tag + "_dog")
    gr["wout"] = _mm(sv["og"], drb, ta=True, name=tag + "_dwout")
    _, (do, dz) = _rowwise_bwd(_mul_silu_fn, [], [(sv["o"], "a", HW), (sv["z"], "a", HW)], [(dog, "a", HW)], name=tag + "_dogate",
                               din_dtypes=[(f32,), (MXU_DT,)])
    att = (sv["qn"], sv["qr"], sv["kn"], sv["kr"], sv["v"], sv["o"], sv["lse"], do)
    dqn, dqr = _attn_bwd_dq(*att, name=tag + "_dq")
    (dkn, dkr, dv), cres = _attn_bwd_dkv(*att, name=tag + "_dkv", comm=comm)
    _, (dqrp,) = _rowwise_bwd(_rope_fn, [], [(cos, "a", LANES), (sin, "a", LANES), (dqr, "c", LANES)], [(dqr, "c", LANES)],
                              name=tag + "_dqrope", ncol=H, tm=1024, diff_i=[2], din_dtypes=[(MXU_DT,)])
    _, (dkrp,) = _rowwise_bwd(_rope_fn, [], [(cos, "a", LANES), (sin, "a", LANES), (dkr, "a", LANES)], [(dkr, "a", LANES)],
                              name=tag + "_dkrope", diff_i=[2], din_dtypes=[(MXU_DT,)])
    dqcn = _mm(dqn, W["wqn"], tb=True, name=tag + "_dqcn1")
    dqcn = _mm(dqrp, W["wqr"], tb=True, add=dqcn, name=tag + "_dqcn2")
    dkvn = _mm(dkn, W["wkn"], tb=True, name=tag + "_dkvn1")
    dkvn = _mm(dv, W["wv"], tb=True, add=dkvn, name=tag + "_dkvn2")
    gr["wqn"] = _mm(sv["qcn"], dqn, ta=True, name=tag + "_dwqn")
    gr["wqr"] = _mm(sv["qcn"], dqrp, ta=True, name=tag + "_dwqr")
    gr["wkn"] = _mm(sv["kvn"], dkn, ta=True, name=tag + "_dwkn")
    gr["wv"] = _mm(sv["kvn"], dv, ta=True, name=tag + "_dwv")
    (gr["q_norm"],), (dqc,) = _rowwise_bwd(_rms_fn, [(W["q_norm"], "a", QR)], [(sv["qc"], "a", QR)], [(dqcn, "a", QR)], name=tag + "_dqnorm",
                                           din_dtypes=[(MXU_DT,)])
    (gr["kv_norm"],), (dkvc,) = _rowwise_bwd(_rms_fn, [(W["kv_norm"], "a", KR)], [(sv["kvc"], "a", KR)], [(dkvn, "a", KR)], name=tag + "_dkvnorm",
                                             din_dtypes=[(MXU_DT,)])
    dh = _mm(dz, W["wz"], tb=True, add=dr, add_scale=_alpha(), name=tag + "_dh1")
    dh = _mm(dqc, W["wqc"], tb=True, add=dh, name=tag + "_dh2")
    dh = _mm(dkvc, W["wkvc"], tb=True, add=dh, name=tag + "_dh3")
    dh = _mm(dkrp, W["wkr"], tb=True, add=dh, name=tag + "_dh4")
    gr["wz"] = _mm(h, dz, ta=True, name=tag + "_dwz")
    gr["wqc"] = _mm(h, dqc, ta=True, name=tag + "_dwqc")
    gr["wkvc"] = _mm(h, dkvc, ta=True, name=tag + "_dwkvc")
    gr["wkr"] = _mm(h, dkrp, ta=True, name=tag + "_dwkr")
    return dh, gr, cres


def _gdn_layer_fwd(h, W, tag, comm=None):
    HK, HV, DK, DV = GDN_N_QK_HEADS, GDN_N_V_HEADS, GDN_DK, GDN_DV
    KD, VD = HK * DK, HV * DV
    qkvp = _mm(h, W["wqkv"], name=tag + "_qkv")
    z = _mm(h, W["wz"], name=tag + "_z")
    br = _mm(h, W["wb"], name=tag + "_b")
    ar = _mm(h, W["wa"], name=tag + "_a")
    qkv = _conv_fwd(qkvp, W["conv_w"], jnp.zeros((1, qkvp.shape[1]), f32), name=tag + "_conv")
    scale = jnp.concatenate([jnp.full((1, KD), DK ** -0.5, f32), jnp.ones((1, KD), f32)], axis=1)
    (qkn,) = _rowwise(_l2_fn, [(scale, "c", DK)], [(qkv, "c", DK)], [(2 * KD, f32, "c", DK)], name=tag + "_l2", ncol=2 * HK, tm=2048)
    (o, states), cres = _gdn_scan_fwd(qkn, qkv, br, ar, W["a_log"], W["dt_bias"], name=tag + "_scan", comm=comm)
    (on,) = _rowwise(_gdn_gate_fn, [(W["norm_w"], "a", DV)], [(o, "c", DV), (z, "c", DV)], [(VD, MXU_DT, "c", DV)],
                     name=tag + "_gate", ncol=HV, tm=1024)
    out = _mm(on, W["wout"], name=tag + "_out")
    return out, dict(h=h, qkvp=qkvp, z=z, br=br, ar=ar, qkv=qkv, qkn=qkn, o=o, states=states, on=on, scale=scale), cres


def _gdn_layer_bwd(sv, W, dr, drb, tag, comm=None):
    h = sv["h"]
    HK, HV, DK, DV = GDN_N_QK_HEADS, GDN_N_V_HEADS, GDN_DK, GDN_DV
    KD, VD = HK * DK, HV * DV
    gr = {}
    don = _mm(drb, W["wout"], tb=True, name=tag + "_don")
    gr["wout"] = _mm(sv["on"], drb, ta=True, name=tag + "_dwout")
    (gr["norm_w"],), (do, dz) = _rowwise_bwd(_gdn_gate_fn, [(W["norm_w"], "a", DV)], [(sv["o"], "c", DV), (sv["z"], "c", DV)],
                                              [(don, "c", DV)], name=tag + "_dgate", ncol=HV, tm=1024, din_dtypes=[(f32,), (MXU_DT,)])
    (dq, dk, dv, dbr, dar, gr["a_log"], gr["dt_bias"]), cres = _gdn_scan_bwd(
        sv["qkn"], sv["qkv"], sv["br"], sv["ar"], W["a_log"], W["dt_bias"], sv["states"], do, name=tag + "_dscan", comm=comm)
    dqkn = jnp.concatenate([dq, dk], axis=1)
    _, (dqk,) = _rowwise_bwd(_l2_fn, [(sv["scale"], "c", DK)], [(sv["qkv"], "c", DK)], [(dqkn, "c", DK)],
                             name=tag + "_dl2", ncol=2 * HK, tm=2048, diff_p=[])
    dqkv = jnp.concatenate([dqk, dv], axis=1)
    dqkvp, gr["conv_w"], _ = _conv_bwd(sv["qkvp"], W["conv_w"], jnp.zeros((1, dqkv.shape[1]), f32), dqkv, name=tag + "_dconv",
                                       dx_dtype=MXU_DT)
    dh = _mm(dz, W["wz"], tb=True, add=dr, add_scale=_alpha(), name=tag + "_dh1")
    dh = _mm(dqkvp, W["wqkv"], tb=True, add=dh, name=tag + "_dh2")
    dh = _mm(dbr, W["wb"], tb=True, add=dh, name=tag + "_dh3")
    dh = _mm(dar, W["wa"], tb=True, add=dh, name=tag + "_dh4")
    gr["wz"] = _mm(h, dz, ta=True, name=tag + "_dwz")
    gr["wqkv"] = _mm(h, dqkvp, ta=True, name=tag + "_dwqkv")
    gr["wb"] = _mm(h, dbr, ta=True, name=tag + "_dwb")
    gr["wa"] = _mm(h, dar, ta=True, name=tag + "_dwa")
    return dh, gr, cres


def _rope_tables(positions):
    half = MLA_ROPE // 2
    inv_freq = ROPE_THETA ** (-jnp.arange(0, MLA_ROPE, 2, dtype=f32) / MLA_ROPE)
    ang = positions.astype(f32)[:, None] * inv_freq
    cos, sin = jnp.cos(ang), jnp.sin(ang)
    S = positions.shape[0]
    pad = jnp.zeros((S, LANES - 2 * half), f32)
    return jnp.concatenate([cos, cos, pad + 1.0], axis=1), jnp.concatenate([sin, sin, pad], axis=1)


class _LocalPlan:
    def __init__(self, LW):
        self.LW, self.grads = LW, [None] * DEPTH

    def weights(self, i):
        return self.LW[i]

    def fwd_comm(self, i):
        return None

    def fwd_done(self, i, res):
        pass

    def bwd_comm(self, i):
        return None

    def bwd_done(self, i, res):
        pass

    def layer_grads(self, i, gr):
        self.grads[i] = gr


def _local_step(x, positions, target, ln_g, ln_b, plan):
    cos, sin = _rope_tables(positions)
    h, hb = x, x.astype(MXU_DT)
    saved, LW = [], []
    for i in range(DEPTH):
        kind, tag = i % 3, "l%d" % i
        LW.append(plan.weights(i))
        comm = plan.fwd_comm(i)
        if kind == 0:
            y, sv, cres = _ssd_layer_fwd(hb, LW[i], tag, comm)
        elif kind == 1:
            y, sv, cres = _mla_layer_fwd(hb, LW[i], cos, sin, tag, comm)
        else:
            y, sv, cres = _gdn_layer_fwd(hb, LW[i], tag, comm)
        plan.fwd_done(i, cres)
        D = h.shape[1]
        r, h, hb = _rowwise(_res_ln_fn, [(ln_g[i], "a", D), (ln_b[i], "a", D)], [(h, "a", D), (y, "a", D)],
                            [(D, f32, "a", D), (D, f32, "a", D), (D, MXU_DT, "a", D)], name=tag + "_ln")
        sv["r"] = r
        saved.append(sv)
    loss, dh = _loss_head(h, target, name="loss_head")
    dg, db = [None] * DEPTH, [None] * DEPTH
    for i in reversed(range(DEPTH)):
        kind, tag = i % 3, "l%d" % i
        sv = saved[i]
        D = dh.shape[1]
        (dg[i], db[i]), (dr, drb) = _rowwise_bwd(_ln_fn, [(ln_g[i], "a", D), (ln_b[i], "a", D)], [(sv["r"], "a", D)], [(dh, "a", D)],
                                                 name=tag + "_dln", din_dtypes=[(f32, MXU_DT)])
        comm = plan.bwd_comm(i)
        if kind == 0:
            dh, gr, cres = _ssd_layer_bwd(sv, LW[i], dr, drb, tag, comm)
        elif kind == 1:
            dh, gr, cres = _mla_layer_bwd(sv, LW[i], cos, sin, dr, drb, tag, comm)
        else:
            dh, gr, cres = _gdn_layer_bwd(sv, LW[i], dr, drb, tag, comm)
        plan.bwd_done(i, cres)
        plan.layer_grads(i, gr)
    return loss, dh, dg, db


_WEIGHTS = ["ssd_in_w", "ssd_conv_w", "ssd_conv_b", "ssd_dt_bias", "ssd_a_log", "ssd_d", "ssd_norm_w", "ssd_out_w",
            "mla_in_w", "mla_q_norm_w", "mla_q_up_w", "mla_kv_norm_w", "mla_kv_up_w", "mla_out_w",
            "gdn_in_w", "gdn_conv_w", "gdn_a_log", "gdn_dt_bias", "gdn_norm_w", "gdn_out_w", "ln_g", "ln_b"]
_BIG = {"ssd_in_w": "col", "ssd_out_w": "row", "mla_in_w": "col", "mla_q_up_w": "col", "mla_kv_up_w": "col",
        "mla_out_w": "row", "gdn_in_w": "col", "gdn_out_w": "row"}
_SMALL_SHARDED = ["ssd_conv_w", "ssd_conv_b", "ssd_norm_w", "gdn_conv_w"]
_PACK_ROWS = 16


def _gathered_to_full(g, kind, nl):
    if kind == "col":
        _, RK, Ns = g.shape
        return g.reshape(4, nl, RK // nl, Ns).transpose(1, 2, 0, 3).reshape(nl, RK // nl, 4 * Ns)
    _, RK, N = g.shape
    return g.reshape(4, nl, RK // nl, N).transpose(1, 0, 2, 3).reshape(nl, 4 * (RK // nl), N)


def _full_to_slots(f, kind):
    nl, K, N = f.shape
    if kind == "col":
        return f.reshape(nl, K, 4, N // 4).transpose(2, 0, 1, 3).reshape(4, nl * K, N // 4)
    return f.reshape(nl, 4, K // 4, N).transpose(1, 0, 2, 3).reshape(4, nl * (K // 4), N)


def _pack(arrs):
    flat = jnp.concatenate([a.reshape(-1).astype(f32) for a in arrs])
    unit = _PACK_ROWS * LANES
    n = -(-flat.shape[0] // unit) * unit
    return jnp.pad(flat, (0, n - flat.shape[0])).reshape(_PACK_ROWS, n // _PACK_ROWS)


def _unpack(packed, shapes):
    flat = packed.reshape(-1)
    out, off = [], 0
    for sh in shapes:
        n = math.prod(sh)
        out.append(flat[off:off + n].reshape(sh))
        off += n
    return out


def _pad_lanes(a):
    return jnp.pad(a, [(0, 0)] * (a.ndim - 1) + [(0, LANES - a.shape[-1])])


def _layer_dict(i, full):
    G, N, P = SSD_N_GROUPS, SSD_D_STATE, SSD_HEAD_DIM
    kind, j = i % 3, i // 3
    if kind == 0:
        H = full["ssd_dt_bias"][j].shape[0]
        DI = H * P
        CD = DI + 2 * G * N
        win = full["ssd_in_w"][j]
        return dict(wz=win[:, :DI], wxbc=win[:, DI:DI + CD], wdt=win[:, DI + CD:], conv_w=full["ssd_conv_w"][j],
                    conv_b=full["ssd_conv_b"][j][None], dt_bias=full["ssd_dt_bias"][j][None], a_log=full["ssd_a_log"][j][None],
                    d=full["ssd_d"][j][None], norm_w=full["ssd_norm_w"][j][None], wout=full["ssd_out_w"][j])
    if kind == 1:
        QR, KR = MLA_Q_RANK, MLA_KV_RANK
        win = full["mla_in_w"][j]
        Hh = full["mla_q_up_w"][j].shape[1] // (MLA_NOPE + MLA_ROPE)
        qup = full["mla_q_up_w"][j].reshape(QR, Hh, MLA_NOPE + MLA_ROPE)
        kvup = full["mla_kv_up_w"][j].reshape(KR, Hh, MLA_NOPE + MLA_V)
        return dict(wqc=win[:, :QR], wkvc=win[:, QR:QR + KR], wkr=_pad_lanes(win[:, QR + KR:QR + KR + MLA_ROPE]),
                    wz=win[:, QR + KR + MLA_ROPE:], q_norm=full["mla_q_norm_w"][j][None], kv_norm=full["mla_kv_norm_w"][j][None],
                    wqn=qup[:, :, :MLA_NOPE].reshape(QR, Hh * MLA_NOPE), wqr=_pad_lanes(qup[:, :, MLA_NOPE:]).reshape(QR, Hh * LANES),
                    wkn=kvup[:, :, :MLA_NOPE].reshape(KR, Hh * MLA_NOPE), wv=kvup[:, :, MLA_NOPE:].reshape(KR, Hh * MLA_V),
                    wout=full["mla_out_w"][j])
    KD, VD, HV = GDN_N_QK_HEADS * GDN_DK, GDN_N_V_HEADS * GDN_DV, GDN_N_V_HEADS
    win = full["gdn_in_w"][j]
    c0, c1 = 2 * KD + VD, 2 * KD + 2 * VD
    return dict(wqkv=win[:, :c0], wz=win[:, c0:c1], wb=win[:, c1:c1 + HV], wa=win[:, c1 + HV:], conv_w=full["gdn_conv_w"][j],
                a_log=full["gdn_a_log"][j][None], dt_bias=full["gdn_dt_bias"][j][None], norm_w=full["gdn_norm_w"][j][None],
                wout=full["gdn_out_w"][j])


def _layer_weights(full, D):
    return [_layer_dict(i, full) for i in range(DEPTH)]


def _layer_full_grads(i, g):
    kind = i % 3
    if kind == 0:
        out = {"ssd_in_w": jnp.concatenate([g["wz"], g["wxbc"], g["wdt"]], axis=1), "ssd_conv_w": g["conv_w"], "ssd_out_w": g["wout"]}
        for n in ("conv_b", "dt_bias", "a_log", "d", "norm_w"):
            out["ssd_" + n] = g[n][0]
        return out
    if kind == 1:
        QR, KR = g["wqn"].shape[0], g["wkn"].shape[0]
        Hh = g["wqn"].shape[1] // MLA_NOPE
        return {"mla_in_w": jnp.concatenate([g["wqc"], g["wkvc"], g["wkr"][:, :MLA_ROPE], g["wz"]], axis=1),
                "mla_q_up_w": jnp.concatenate([g["wqn"].reshape(QR, Hh, MLA_NOPE), g["wqr"].reshape(QR, Hh, LANES)[:, :, :MLA_ROPE]],
                                              axis=2).reshape(QR, -1),
                "mla_kv_up_w": jnp.concatenate([g["wkn"].reshape(KR, Hh, MLA_NOPE), g["wv"].reshape(KR, Hh, MLA_V)], axis=2).reshape(KR, -1),
                "mla_q_norm_w": g["q_norm"][0], "mla_kv_norm_w": g["kv_norm"][0], "mla_out_w": g["wout"]}
    out = {"gdn_in_w": jnp.concatenate([g["wqkv"], g["wz"], g["wb"], g["wa"]], axis=1), "gdn_conv_w": g["conv_w"], "gdn_out_w": g["wout"]}
    for n in ("a_log", "dt_bias", "norm_w"):
        out["gdn_" + n] = g[n][0]
    return out


def _full_grads(grads, dg, db):
    per = {n: [] for n in _WEIGHTS}
    for i in range(DEPTH):
        for n, a in _layer_full_grads(i, grads[i]).items():
            per[n].append(a)
        per["ln_g"].append(dg[i][0])
        per["ln_b"].append(db[i][0])
    return {n: jnp.stack(v) for n, v in per.items()}


class _DistPlan:
    def __init__(self, w, chip, core):
        self.w, self.chip = w, chip
        self.chip_arr = jnp.reshape(chip, (1,)).astype(jnp.int32)
        self.core_arr = jnp.reshape(core, (1,)).astype(jnp.int32)
        self.full = {n: {} for n in _BIG}
        self.p4, self.fin, self.small_grads = {}, {}, [None] * DEPTH
        got = self._gather(0, extra=[_pack([w[n] for n in _SMALL_SHARDED])], name="gather_l0")
        parts = [_unpack(got[k], [w[n].shape for n in _SMALL_SHARDED]) for k in range(4)]
        for t, n in enumerate(_SMALL_SHARDED):
            self.full[n] = jnp.concatenate([parts[k][t] for k in range(4)], axis=-1)
        for n in _WEIGHTS:
            if n not in self.full:
                self.full[n] = w[n]

    @staticmethod
    def names(i):
        return [["ssd_in_w", "ssd_out_w"], ["mla_in_w", "mla_q_up_w", "mla_kv_up_w", "mla_out_w"], ["gdn_in_w", "gdn_out_w"]][i % 3]

    def _shards(self, i):
        return [self.w[n][i // 3].astype(MXU_DT) for n in self.names(i)]

    def _fill(self, i, shards, got):
        for n, s, g in zip(self.names(i), shards, got):
            g = lax.dynamic_update_slice(g, s[None], (self.chip, 0, 0))
            self.full[n][i // 3] = _gathered_to_full(g, _BIG[n], 1)[0]

    def _gather(self, i, extra, name):
        shards = self._shards(i)
        got = _run_comm(_GatherChips(shards + extra), name=name)
        self._fill(i, shards, got[:len(shards)])
        return [lax.dynamic_update_slice(g, s[None], (self.chip, 0, 0)) for g, s in zip(got[len(shards):], extra)][0]

    def weights(self, i):
        return _layer_dict(i, self.full)

    def fwd_comm(self, i):
        if i + 1 >= DEPTH:
            return None
        self._pending_shards = self._shards(i + 1)
        return _GatherChips(self._pending_shards)

    def fwd_done(self, i, res):
        if res is not None:
            self._fill(i + 1, self._pending_shards, res)

    def layer_grads(self, i, gr):
        fg = _layer_full_grads(i, gr)
        names = self.names(i)
        self.small_grads[i] = {n: a for n, a in fg.items() if n not in _BIG}
        g4 = [_full_to_slots(fg[n][None], _BIG[n]) for n in names]
        r1 = _pair_send_halves(g4, name="grad_pair_send_l%d" % i)
        self.p4[i] = [_add_half(a, b, self.core_arr, name="grad_pair_add_l%d_%s" % (i, n)) for a, b, n in zip(g4, r1, names)]
        if i == 0:
            self._sum(0, _run_comm(self._scatter(0), name="grad_chip_scatter_l0"))

    def _scatter(self, i):
        return _ScatterChips([p[1] for p in self.p4[i]])

    def _sum(self, i, r2):
        self.fin[i] = [_sum_chips(p[0], b, self.chip_arr, name="grad_chip_sum_l%d_%s" % (i, n))
                       for p, b, n in zip(self.p4[i], r2, self.names(i))]

    def bwd_comm(self, i):
        return self._scatter(i + 1) if i + 1 < DEPTH else None

    def bwd_done(self, i, res):
        if res is not None:
            self._sum(i + 1, res)

    def grad_shards(self, core):
        order = [(i, t) for i in range(DEPTH) for t in range(len(self.names(i)))]
        fins = [self.fin[i][t] for i, t in order]
        got = _pair_exchange_halves(fins, name="grad_pair_share")
        got = [lax.dynamic_update_slice(g, f, (core * f.shape[0], 0)) for g, f in zip(got, fins)]
        per = {n: [] for n in _BIG}
        for (i, t), g in zip(order, got):
            per[self.names(i)[t]].append(g)
        return {n: (v[0] if len(v) == 1 else jnp.concatenate(v, axis=0)) for n, v in per.items()}


def kernel(x, positions, ssd_in_w, ssd_conv_w, ssd_conv_b, ssd_dt_bias, ssd_a_log, ssd_d, ssd_norm_w, ssd_out_w, mla_in_w, mla_q_norm_w, mla_q_up_w, mla_kv_norm_w, mla_kv_up_w, mla_out_w, gdn_in_w, gdn_conv_w, gdn_a_log, gdn_dt_bias, gdn_norm_w, gdn_out_w, ln_g, ln_b, loss_target, m_ssd_in_w, m_ssd_conv_w, m_ssd_conv_b, m_ssd_dt_bias, m_ssd_a_log, m_ssd_d, m_ssd_norm_w, m_ssd_out_w, m_mla_in_w, m_mla_q_norm_w, m_mla_q_up_w, m_mla_kv_norm_w, m_mla_kv_up_w, m_mla_out_w, m_gdn_in_w, m_gdn_conv_w, m_gdn_a_log, m_gdn_dt_bias, m_gdn_norm_w, m_gdn_out_w, m_ln_g, m_ln_b, v_ssd_in_w, v_ssd_conv_w, v_ssd_conv_b, v_ssd_dt_bias, v_ssd_a_log, v_ssd_d, v_ssd_norm_w, v_ssd_out_w, v_mla_in_w, v_mla_q_norm_w, v_mla_q_up_w, v_mla_kv_norm_w, v_mla_kv_up_w, v_mla_out_w, v_gdn_in_w, v_gdn_conv_w, v_gdn_a_log, v_gdn_dt_bias, v_gdn_norm_w, v_gdn_out_w, v_ln_g, v_ln_b):
    args = dict(locals())
    w = {n: args[n] for n in _WEIGHTS}
    mom = {n: args["m_" + n] for n in _WEIGHTS}
    vel = {n: args["v_" + n] for n in _WEIGHTS}
    mx, my, mc = _me()
    chip = 2 * mx + my
    small = [n for n in _WEIGHTS if n not in _BIG]
    big = list(_BIG)

    plan = _DistPlan(w, chip, mc)
    loss, gx, dg, db = _local_step(x[0], positions[0], loss_target[0], [plan.full["ln_g"][i][None] for i in range(DEPTH)],
                                   [plan.full["ln_b"][i][None] for i in range(DEPTH)], plan)
    per = {n: [] for n in small}
    for i in range(DEPTH):
        for n, a in plan.small_grads[i].items():
            per[n].append(a)
        per["ln_g"].append(dg[i][0])
        per["ln_b"].append(db[i][0])
    fg = {n: jnp.stack(v) for n, v in per.items()}
    gsh = plan.grad_shards(mc)

    out_g, out_d, out_m, out_v = {}, {}, {}, {}
    for n in big:
        g = gsh[n]
        sh = w[n].shape
        to2 = lambda a: a.reshape(-1, sh[-1])
        d_, m_, v_ = _adamw(to2(w[n]), g, to2(mom[n]), to2(vel[n]), name="adamw_" + n)
        out_g[n], out_d[n], out_m[n], out_v[n] = g.reshape(sh), d_.reshape(sh), m_.reshape(sh), v_.reshape(sh)

    summed = _sum8(_allgather_all(_pack([fg[n] for n in small] + [loss[0, :1]]), name="gather_small"), name="sum_small")
    sg = _unpack(summed, [fg[n].shape for n in small] + [(1,)])
    loss_total = sg[-1][0]
    gs = {}
    for n, g in zip(small, sg[:-1]):
        if n in _SMALL_SHARDED:
            ws = w[n].shape[-1]
            g = lax.dynamic_slice_in_dim(g, chip * ws, ws, axis=g.ndim - 1)
        gs[n] = g
    shapes = [w[n].shape for n in small]
    d_, m_, v_ = _adamw(_pack([w[n] for n in small]), _pack([gs[n] for n in small]), _pack([mom[n] for n in small]),
                        _pack([vel[n] for n in small]), name="adamw_small")
    for n, a, b, c in zip(small, _unpack(d_, shapes), _unpack(m_, shapes), _unpack(v_, shapes)):
        out_g[n], out_d[n], out_m[n], out_v[n] = gs[n], a, b, c

    return (loss_total, gx[None], *[out_g[n] for n in _WEIGHTS], *[out_d[n] for n in _WEIGHTS],
            *[out_m[n] for n in _WEIGHTS], *[out_v[n] for n in _WEIGHTS])
```

```python
import functools
import math

import jax
import jax.numpy as jnp
from jax import lax
from jax.experimental import pallas as pl
from jax.experimental.pallas import tpu as pltpu

f32 = jnp.float32
HI = lax.Precision.HIGHEST
MXU_DT = jnp.bfloat16
COMM_DT = jnp.bfloat16
MESH = pl.DeviceIdType.MESH

DEPTH = 4
LN_EPS = 1e-5
RMS_EPS = 1e-6
SSD_HEAD_DIM = 64
SSD_N_GROUPS = 8
SSD_D_STATE = 128
SSD_CONV = 4
SSD_CHUNK = 128
MLA_Q_RANK = 768
MLA_KV_RANK = 512
MLA_NOPE = 128
MLA_ROPE = 64
MLA_V = 128
ROPE_THETA = 10000.0
GDN_N_QK_HEADS = 16
GDN_N_V_HEADS = 32
GDN_DK = 128
GDN_DV = 128
GDN_CONV = 4
GDN_CHUNK = 64
ADAM_LR = 0.001
ADAM_B1 = 0.9
ADAM_B2 = 0.999
ADAM_EPS = 1e-08
ADAM_WD = 0.01
ADAM_STEP = 10

LANES = 128
VMEM_LIMIT = 48 * 1024 * 1024
ATT_TILE = 512
ROW_TILE = 256
MM_TILE_M = 1024
MM_TILE_N = 1024
MM_TILE_K = 2048
MM_VMEM_BUDGET = 40 * 1024 * 1024
GDN_HEADS_PER_STEP = 16


def _alpha():
    return (2.0 * DEPTH) ** 0.25


def _tile(n, pref, align=LANES):
    t = min(pref, n) // align * align
    while t >= align:
        if n % t == 0:
            return t
        t -= align
    return n


def _cp(sem=None):
    return pltpu.CompilerParams(dimension_semantics=sem, vmem_limit_bytes=VMEM_LIMIT)


def _iota(shape, dim):
    return lax.broadcasted_iota(jnp.int32, shape, dim)


def _div_pow2(x, p):
    assert p & (p - 1) == 0
    return lax.shift_right_logical(x, jnp.int32(p.bit_length() - 1))


def _dot(a, b, dims=((1,), (0,)), hi=False):
    if hi:
        return lax.dot_general(a.astype(f32), b.astype(f32), (dims, ((), ())), precision=HI, preferred_element_type=f32)
    return lax.dot_general(a.astype(MXU_DT), b.astype(MXU_DT), (dims, ((), ())), preferred_element_type=f32)


_NT = ((1,), (1,))
_TN = ((0,), (0,))


def _split3(x):
    hi = x.astype(jnp.bfloat16)
    r = x - hi.astype(f32)
    mid = r.astype(jnp.bfloat16)
    return hi, mid, (r - mid.astype(f32)).astype(jnp.bfloat16)


def _seldot_impl(a, b, dims, exact):
    def d(x, y):
        return lax.dot_general(x, y, (dims, ((), ())), preferred_element_type=f32)

    if exact == 0:
        a01 = a.astype(jnp.bfloat16)
        t = _split3(b.astype(f32))
        return (d(a01, t[0]) + d(a01, t[1])) + d(a01, t[2])
    b01 = b.astype(jnp.bfloat16)
    t = _split3(a.astype(f32))
    return (d(t[0], b01) + d(t[1], b01)) + d(t[2], b01)


@functools.partial(jax.custom_vjp, nondiff_argnums=(2, 3))
def _seldot(a, b, dims, exact):
    return _seldot_impl(a, b, dims, exact)


def _seldot_fwd(a, b, dims, exact):
    return _seldot_impl(a, b, dims, exact), (a, b)


def _seldot_bwd(dims, exact, res, dy):
    a, b = res
    (ca,), (cb,) = dims
    if exact == 0:
        assert ca == 1
        db = _seldot_impl(a, dy, _TN, 0) if cb == 0 else _seldot_impl(dy, a, _TN, 1)
        return jnp.zeros_like(a), db
    assert ca == 1 and cb == 0
    return _seldot_impl(dy, b, _NT, 1), jnp.zeros_like(b)


_seldot.defvjp(_seldot_fwd, _seldot_bwd)


def _softplus(x):
    return jnp.maximum(x, 0.0) + jnp.log1p(jnp.exp(-jnp.abs(x)))


def _silu(x):
    return x * jax.nn.sigmoid(x)


def _mm(a, b, *, name, ta=False, tb=False, add=None, add_scale=1.0, out_dtype=f32):
    M, K = (a.shape[1], a.shape[0]) if ta else a.shape
    N = b.shape[0] if tb else b.shape[1]
    assert (b.shape[1] if tb else b.shape[0]) == K, (a.shape, b.shape, ta, tb)
    tm, tn, tk = _tile(M, MM_TILE_M), _tile(N, MM_TILE_N), _tile(K, MM_TILE_K)
    ab, bb = jnp.dtype(a.dtype).itemsize, jnp.dtype(b.dtype).itemsize
    while 2 * tk * (tm * ab + tn * bb) + 12 * tm * tn > MM_VMEM_BUDGET and tk % (2 * LANES) == 0:
        tk //= 2
    nk = K // tk
    a_spec = pl.BlockSpec((tk, tm), lambda i, j, k: (k, i)) if ta else pl.BlockSpec((tm, tk), lambda i, j, k: (i, k))
    b_spec = pl.BlockSpec((tn, tk), lambda i, j, k: (j, k)) if tb else pl.BlockSpec((tk, tn), lambda i, j, k: (k, j))
    o_spec = pl.BlockSpec((tm, tn), lambda i, j, k: (i, j))
    dims = ((0 if ta else 1,), (1 if tb else 0,))
    has_add = add is not None

    def body(*refs):
        a_ref, b_ref = refs[:2]
        add_ref = refs[2] if has_add else None
        o_ref = refs[3 if has_add else 2]

        def finish(r):
            if has_add:
                r = r + add_scale * add_ref[...].astype(f32)
            o_ref[...] = r.astype(out_dtype)

        if nk == 1:
            finish(_dot(a_ref[...], b_ref[...], dims))
            return
        acc = refs[-1]
        k = pl.program_id(2)

        @pl.when(k == 0)
        def _():
            acc[...] = jnp.zeros_like(acc)

        acc[...] += _dot(a_ref[...], b_ref[...], dims)

        @pl.when(k == nk - 1)
        def _():
            finish(acc[...])

    ins = [a, b] + ([add] if has_add else [])
    specs = [a_spec, b_spec] + ([o_spec] if has_add else [])
    return pl.pallas_call(
        body, name=name, grid=(M // tm, N // tn, nk), in_specs=specs, out_specs=o_spec,
        out_shape=jax.ShapeDtypeStruct((M, N), out_dtype), scratch_shapes=[pltpu.VMEM((tm, tn), f32)] if nk > 1 else [],
        compiler_params=_cp(("parallel", "parallel", "arbitrary")))(*ins)


def _rw_specs(params, ins, ncol, tm):
    specs = []
    for arr, mode, bw, coff in params:
        if mode == "c":
            specs.append(pl.BlockSpec((1, bw), lambda c, r, coff=coff: (0, c + coff)))
        else:
            specs.append(pl.BlockSpec((1, bw), lambda c, r, coff=coff: (0, coff)))
    for arr, mode, bw, coff in ins:
        if mode == "c":
            specs.append(pl.BlockSpec((tm, bw), lambda c, r, coff=coff: (r, c + coff)))
        else:
            specs.append(pl.BlockSpec((tm, bw), lambda c, r, coff=coff: (r, coff)))
    return specs


def _norm_spec(lst):
    out = []
    for t in lst:
        arr, mode, bw = t[0], t[1], t[2]
        coff = t[3] if len(t) > 3 else 0
        out.append((arr, mode, bw, coff))
    return out


def _rowwise(fn, params, ins, outs, *, name, ncol=1, tm=None):
    params, ins = _norm_spec(params), _norm_spec(ins)
    S = ins[0][0].shape[0]
    tm = _tile(S, tm or ROW_TILE, 8)
    npar, nin = len(params), len(ins)

    def body(*refs):
        pv = [r[...].astype(f32) for r in refs[:npar]]
        iv = [r[...].astype(f32) for r in refs[npar:npar + nin]]
        res = fn(*pv, *iv)
        for o_ref, val in zip(refs[npar + nin:], res):
            o_ref[...] = val.astype(o_ref.dtype)

    out_specs, out_shapes = [], []
    for W, dt, mode, bw in outs:
        out_shapes.append(jax.ShapeDtypeStruct((S, W), dt))
        if mode == "c":
            out_specs.append(pl.BlockSpec((tm, bw), lambda c, r: (r, c)))
        else:
            out_specs.append(pl.BlockSpec((tm, bw), lambda c, r: (r, 0)))
    return pl.pallas_call(
        body, name=name, grid=(ncol, S // tm), in_specs=_rw_specs(params, ins, ncol, tm), out_specs=out_specs,
        out_shape=out_shapes, compiler_params=_cp(("parallel", "parallel")))(*[p[0] for p in params], *[i[0] for i in ins])


def _rowwise_bwd(fn, params, ins, couts, *, name, ncol=1, tm=None, diff_p=None, diff_i=None, din_dtypes=None, comm=None):
    params, ins, couts = _norm_spec(params), _norm_spec(ins), _norm_spec(couts)
    S = ins[0][0].shape[0]
    tm = _tile(S, tm or ROW_TILE, 8)
    npar, nin, nco = len(params), len(ins), len(couts)
    diff_p = list(range(npar)) if diff_p is None else diff_p
    diff_i = list(range(nin)) if diff_i is None else diff_i
    din_dtypes = [(f32,)] * len(diff_i) if din_dtypes is None else din_dtypes

    def body(*refs):
        c, r = pl.program_id(0), pl.program_id(1)
        pv = [x[...].astype(f32) for x in refs[:npar]]
        iv = [x[...].astype(f32) for x in refs[npar:npar + nin]]
        cv = [x[...].astype(f32) for x in refs[npar + nin:npar + nin + nco]]
        orefs = refs[npar + nin + nco:]

        def g(*dargs):
            p2, i2 = list(pv), list(iv)
            for n, k in enumerate(diff_p):
                p2[k] = dargs[n]
            for n, k in enumerate(diff_i):
                i2[k] = dargs[len(diff_p) + n]
            return tuple(fn(*p2, *i2))

        _, vjp = jax.vjp(g, *[pv[k] for k in diff_p], *[iv[k] for k in diff_i])
        grads = vjp(tuple(cv))
        for n, k in enumerate(diff_p):
            o_ref = orefs[n]
            first = (r == 0) if params[k][1] == "c" else jnp.logical_and(r == 0, c == 0)

            @pl.when(first)
            def _(o_ref=o_ref):
                o_ref[...] = jnp.zeros_like(o_ref)

            o_ref[...] += grads[n]
        pos = len(diff_p)
        for n, k in enumerate(diff_i):
            for _ in din_dtypes[n]:
                orefs[pos][...] = grads[len(diff_p) + n].astype(orefs[pos].dtype)
                pos += 1

    out_specs, out_shapes = [], []
    for k in diff_p:
        arr, mode, bw, coff = params[k]
        W = bw * ncol if mode == "c" else bw
        out_shapes.append(jax.ShapeDtypeStruct((1, W), f32))
        out_specs.append(pl.BlockSpec((1, bw), (lambda c, r: (0, c)) if mode == "c" else (lambda c, r: (0, 0))))
    for n, k in enumerate(diff_i):
        arr, mode, bw, coff = ins[k]
        W = bw * ncol if mode == "c" else bw
        for dt in din_dtypes[n]:
            out_shapes.append(jax.ShapeDtypeStruct((S, W), dt))
            out_specs.append(pl.BlockSpec((tm, bw), (lambda c, r: (r, c)) if mode == "c" else (lambda c, r: (r, 0))))
    res, cres = _call(
        body, comm, name=name, grid=(ncol, S // tm), in_specs=_rw_specs(params, ins + couts, ncol, tm), out_specs=out_specs,
        out_shape=out_shapes, scratch_shapes=[], sem=("arbitrary", "arbitrary"),
        args=(*[p[0] for p in params], *[i[0] for i in ins], *[c[0] for c in couts]))
    if comm is None:
        return list(res[:len(diff_p)]), list(res[len(diff_p):])
    return list(res[:len(diff_p)]), list(res[len(diff_p):]), cres


def _ln_fn(g, b, r):
    mu = jnp.mean(r, -1, keepdims=True)
    xc = r - mu
    var = jnp.mean(xc * xc, -1, keepdims=True)
    return (xc * lax.rsqrt(var + LN_EPS) * g + b,)


def _res_ln_fn(g, b, h, y):
    r = _alpha() * h + y
    hn = _ln_fn(g, b, r)
    return (r,) + hn + hn


def _rms_fn(w, x):
    return (x * lax.rsqrt(jnp.mean(x * x, -1, keepdims=True) + RMS_EPS) * w,)


def _ssd_gate_fn(w, y, z):
    yg = y * _silu(z)
    return (yg * lax.rsqrt(jnp.mean(yg * yg, -1, keepdims=True) + RMS_EPS) * w,)


def _mul_silu_fn(o, z):
    return (o * _silu(z),)


def _gdn_gate_fn(w, o, z):
    return (o * lax.rsqrt(jnp.mean(o * o, -1, keepdims=True) + RMS_EPS) * w * _silu(z),)


def _l2_fn(scale, x):
    return (x * lax.rsqrt(jnp.sum(x * x, -1, keepdims=True) + RMS_EPS) * scale,)


def _rope_fn(cos, sin, x):
    half = MLA_ROPE // 2
    i = _iota((LANES, LANES), 0)
    j = _iota((LANES, LANES), 1)
    pm = jnp.where((i == j + half) & (j < half), -1.0, 0.0) + jnp.where((i + half == j) & (j < 2 * half), 1.0, 0.0)
    return (x * cos + _seldot(x, pm.astype(f32), ((1,), (0,)), 1) * sin,)


def _conv_taps(x, K):
    S = x.shape[0]
    rows = _iota(x.shape, 0)
    return [x] + [jnp.where(rows < j, 0.0, pltpu.roll(x, j, 0)) for j in range(1, K)]


def _conv_fwd(x, w, b, *, name):
    S, C = x.shape
    K = w.shape[0]
    cw = _tile(C, LANES)

    def body(x_ref, w_ref, b_ref, o_ref):
        taps = _conv_taps(x_ref[...], K)
        wv = w_ref[...]
        pre = b_ref[...] + taps[0] * wv[K - 1:K, :]
        for j in range(1, K):
            pre = pre + taps[j] * wv[K - 1 - j:K - j, :]
        o_ref[...] = _silu(pre)

    return pl.pallas_call(
        body, name=name, grid=(C // cw,),
        in_specs=[pl.BlockSpec((S, cw), lambda c: (0, c)), pl.BlockSpec((K, cw), lambda c: (0, c)), pl.BlockSpec((1, cw), lambda c: (0, c))],
        out_specs=pl.BlockSpec((S, cw), lambda c: (0, c)), out_shape=jax.ShapeDtypeStruct((S, C), f32),
        compiler_params=_cp(("parallel",)))(x, w, b)


def _conv_bwd(x, w, b, dys, *, name, dx_dtype=f32):
    S, C = x.shape
    K = w.shape[0]
    cw = _tile(C, LANES)
    nblk = [d.shape[1] // cw for d in dys]
    offs = [sum(nblk[:p]) for p in range(len(dys))]
    assert sum(nblk) == C // cw and all(d.shape[1] % cw == 0 for d in dys)
    npc = len(dys)

    def body(x_ref, w_ref, b_ref, *refs):
        dy_refs, (dx_ref, dw_ref, db_ref) = refs[:npc], refs[npc:]
        c = pl.program_id(0)
        for p in range(npc):
            @pl.when(jnp.logical_and(c >= offs[p], c < offs[p] + nblk[p]))
            def _(p=p):
                _conv_bwd_block(x_ref, w_ref, b_ref, dy_refs[p], dx_ref, dw_ref, db_ref, K, S, dx_dtype)

    col = lambda c: (0, c)
    dy_specs = [pl.BlockSpec((S, cw), lambda c, o=offs[p], n=nblk[p]: (0, jnp.clip(c - o, 0, n - 1))) for p in range(npc)]
    return pl.pallas_call(
        body, name=name, grid=(C // cw,),
        in_specs=[pl.BlockSpec((S, cw), col), pl.BlockSpec((K, cw), col), pl.BlockSpec((1, cw), col)] + dy_specs,
        out_specs=[pl.BlockSpec((S, cw), col), pl.BlockSpec((K, cw), col), pl.BlockSpec((1, cw), col)],
        out_shape=[jax.ShapeDtypeStruct((S, C), dx_dtype), jax.ShapeDtypeStruct((K, C), f32), jax.ShapeDtypeStruct((1, C), f32)],
        compiler_params=_cp(("parallel",)))(x, w, b, *dys)


def _conv_bwd_block(x_ref, w_ref, b_ref, dy_ref, dx_ref, dw_ref, db_ref, K, S, dx_dtype):
    taps = _conv_taps(x_ref[...], K)
    wv = w_ref[...]
    pre = b_ref[...] + taps[0] * wv[K - 1:K, :]
    for j in range(1, K):
        pre = pre + taps[j] * wv[K - 1 - j:K - j, :]
    sg = jax.nn.sigmoid(pre)
    dpre = dy_ref[...] * sg * (1.0 + pre * (1.0 - sg))
    db_ref[...] = jnp.sum(dpre, axis=0, keepdims=True)
    rows = _iota(dpre.shape, 0)
    dx = dpre * wv[K - 1:K, :]
    dw_ref[K - 1:K, :] = jnp.sum(dpre * taps[0], axis=0, keepdims=True)
    for j in range(1, K):
        dw_ref[K - 1 - j:K - j, :] = jnp.sum(dpre * taps[j], axis=0, keepdims=True)
        up = jnp.where(rows >= S - j, 0.0, pltpu.roll(dpre, S - j, 0))
        dx = dx + up * wv[K - 1 - j:K - j, :]
    dx_ref[...] = dx.astype(dx_dtype)


def _ssd_chunk(prev, xs, Bm, Cm, dtr, dtb, alog, dsk, g, *, R, P):
    L, GW = xs.shape
    H = dtr.shape[1]
    tril = _iota((L, L), 0) >= _iota((L, L), 1)
    dt = _softplus(dtr + dtb)
    acs = _seldot(tril.astype(f32), dt * (-jnp.exp(alog)), ((1,), (0,)), 0)
    expand = (_iota((H, GW), 0) == g * R + _div_pow2(_iota((H, GW), 1), P)).astype(f32)
    dt_e = _seldot(dt, expand, ((1,), (0,)), 1)
    acs_e = _seldot(acs, expand, ((1,), (0,)), 1)
    d_e = jnp.sum(_seldot(jnp.broadcast_to(dsk, (8, H)), expand, ((1,), (0,)), 1), axis=0, keepdims=True) * 0.125
    last = jnp.sum(jnp.where(_iota((L, GW), 0) == L - 1, acs_e, 0.0), axis=0, keepdims=True)
    xdt = xs * dt_e
    cb = _dot(Cm, Bm, _NT)
    nsel = max(R, 8)
    sel = (_iota((nsel, H), 1) == g * R + _iota((nsel, H), 0)).astype(f32)
    acs_t = _seldot(sel, acs, _NT, 0)
    hp = LANES // P
    pieces = []
    for p in range(GW // LANES):
        xp = xdt[:, p * LANES:(p + 1) * LANES]
        acc = None
        for q in range(hp):
            r = p * hp + q
            col = jnp.sum(jnp.where(_iota((L, H), 1) == g * R + r, acs, 0.0), axis=1, keepdims=True)
            row = jnp.sum(jnp.where(_iota((nsel, L), 0) == r, acs_t, 0.0), axis=0, keepdims=True)
            dec = jnp.where(tril, jnp.exp(jnp.where(tril, col - row, 0.0)), 0.0)
            xm = jnp.where(_div_pow2(_iota((L, LANES), 1), P) == q, xp, 0.0)
            t = _dot(cb * dec, xm)
            acc = t if acc is None else acc + t
        pieces.append(acc)
    y_diag = pieces[0] if len(pieces) == 1 else jnp.concatenate(pieces, axis=1)
    st = _dot(Bm, xdt * jnp.exp(last - acs_e), _TN)
    y_off = _dot(Cm, prev) * jnp.exp(acs_e)
    new = prev * jnp.exp(last) + st
    return y_diag + y_off + xs * d_e, new


def _ssd_dims(xbc, dtr):
    S, CD = xbc.shape
    H = dtr.shape[1]
    G, N, P = SSD_N_GROUPS, SSD_D_STATE, SSD_HEAD_DIM
    DI = H * P
    R = H // G
    assert CD == DI + 2 * G * N and DI % N == 0
    return S, H, G, N, P, DI, R, R * P, SSD_CHUNK


def _ssd_scan_fwd(xbc, dtr, dtb, alog, dsk, *, name, comm=None):
    S, H, G, N, P, DI, R, GW, L = _ssd_dims(xbc, dtr)
    nc = S // L
    boff, coff = DI // N, DI // N + G

    def body(xs_ref, b_ref, c_ref, dtr_ref, dtb_ref, alog_ref, dsk_ref, y_ref, st_ref, state):
        c, g = pl.program_id(0), pl.program_id(1)

        @pl.when(c == 0)
        def _():
            state[g] = jnp.zeros((N, GW), f32)

        prev = state[g]
        st_ref[0, 0] = prev
        y, new = _ssd_chunk(prev, xs_ref[...], b_ref[...], c_ref[...], dtr_ref[...], dtb_ref[...], alog_ref[...],
                            dsk_ref[...], g, R=R, P=P)
        y_ref[...] = y
        state[g] = new

    par = pl.BlockSpec((1, H), lambda c, g: (0, 0))
    return _call(
        body, comm, name=name, grid=(nc, G),
        in_specs=[pl.BlockSpec((L, GW), lambda c, g: (c, g)), pl.BlockSpec((L, N), lambda c, g: (c, boff + g)),
                  pl.BlockSpec((L, N), lambda c, g: (c, coff + g)), pl.BlockSpec((L, H), lambda c, g: (c, 0)), par, par, par],
        out_specs=[pl.BlockSpec((L, GW), lambda c, g: (c, g)), pl.BlockSpec((1, 1, N, GW), lambda c, g: (c, g, 0, 0))],
        out_shape=[jax.ShapeDtypeStruct((S, DI), f32), jax.ShapeDtypeStruct((nc, G, N, GW), f32)],
        scratch_shapes=[pltpu.VMEM((G, N, GW), f32)],
        sem=("arbitrary", "arbitrary"), args=(xbc, xbc, xbc, dtr, dtb, alog, dsk))


def _ssd_scan_bwd(xbc, dtr, dtb, alog, dsk, states, dy, *, name, comm=None):
    S, H, G, N, P, DI, R, GW, L = _ssd_dims(xbc, dtr)
    nc = S // L
    boff, coff = DI // N, DI // N + G

    def body(xs_ref, b_ref, c_ref, dtr_ref, dtb_ref, alog_ref, dsk_ref, st_ref, dy_ref,
             dxs_ref, db_ref, dc_ref, ddtr_ref, ddtb_ref, dalog_ref, ddsk_ref, dstate):
        c, g = pl.program_id(0), pl.program_id(1)

        @pl.when(c == 0)
        def _():
            dstate[g] = jnp.zeros((N, GW), f32)

        @pl.when(jnp.logical_and(c == 0, g == 0))
        def _():
            ddtb_ref[...] = jnp.zeros_like(ddtb_ref)
            dalog_ref[...] = jnp.zeros_like(dalog_ref)
            ddsk_ref[...] = jnp.zeros_like(ddsk_ref)

        @pl.when(g == 0)
        def _():
            ddtr_ref[...] = jnp.zeros_like(ddtr_ref)

        fn = functools.partial(_ssd_chunk, g=g, R=R, P=P)
        _, vjp = jax.vjp(fn, st_ref[0, 0], xs_ref[...], b_ref[...], c_ref[...], dtr_ref[...], dtb_ref[...],
                         alog_ref[...], dsk_ref[...])
        dprev, dxs, dB, dC, ddtr, ddtb, dalog, ddsk = vjp((dy_ref[...], dstate[g]))
        dstate[g] = dprev
        dxs_ref[...] = dxs
        db_ref[...] = dB
        dc_ref[...] = dC
        ddtr_ref[...] += ddtr
        ddtb_ref[...] += ddtb
        dalog_ref[...] += dalog
        ddsk_ref[...] += ddsk

    rc = lambda c: nc - 1 - c
    par = pl.BlockSpec((1, H), lambda c, g: (0, 0))
    return _call(
        body, comm, name=name, grid=(nc, G),
        in_specs=[pl.BlockSpec((L, GW), lambda c, g: (rc(c), g)), pl.BlockSpec((L, N), lambda c, g: (rc(c), boff + g)),
                  pl.BlockSpec((L, N), lambda c, g: (rc(c), coff + g)), pl.BlockSpec((L, H), lambda c, g: (rc(c), 0)),
                  par, par, par, pl.BlockSpec((1, 1, N, GW), lambda c, g: (rc(c), g, 0, 0)),
                  pl.BlockSpec((L, GW), lambda c, g: (rc(c), g))],
        out_specs=[pl.BlockSpec((L, GW), lambda c, g: (rc(c), g)), pl.BlockSpec((L, N), lambda c, g: (rc(c), g)),
                   pl.BlockSpec((L, N), lambda c, g: (rc(c), g)), pl.BlockSpec((L, H), lambda c, g: (rc(c), 0)), par, par, par],
        out_shape=[jax.ShapeDtypeStruct((S, DI), f32), jax.ShapeDtypeStruct((S, G * N), f32), jax.ShapeDtypeStruct((S, G * N), f32),
                   jax.ShapeDtypeStruct((S, H), f32)] + [jax.ShapeDtypeStruct((1, H), f32)] * 3,
        scratch_shapes=[pltpu.VMEM((G, N, GW), f32)],
        sem=("arbitrary", "arbitrary"), args=(xbc, xbc, xbc, dtr, dtb, alog, dsk, states, dy))


def _dot3(a, b, dims=((1,), (0,))):
    def split(x):
        hi = x.astype(jnp.bfloat16)
        return hi, (x - hi.astype(f32)).astype(jnp.bfloat16)

    def d(x, y):
        return lax.dot_general(x, y, (dims, ((), ())), preferred_element_type=f32)

    ah, al = split(a)
    bh, bl = split(b)
    return d(ah, bh) + (d(ah, bl) + d(al, bh))


def _neumann_inverses(As):
    L = As[0].shape[0]
    eye = (_iota((L, L), 0) == _iota((L, L), 1)).astype(f32)
    X = [-A for A in As]
    P = [eye + x for x in X]
    n = 1
    while 2 * n < L:
        X = [_dot3(x, x) for x in X]
        P = [p + _dot3(p, x) for p, x in zip(P, X)]
        n *= 2
    return P


@jax.custom_vjp
def _unit_lower_solves(As, Rs):
    return tuple(_dot3(T, R) for T, R in zip(_neumann_inverses(As), Rs))


def _uls_fwd(As, Rs):
    Ts = _neumann_inverses(As)
    Xs = tuple(_dot3(T, R) for T, R in zip(Ts, Rs))
    return Xs, (tuple(Ts), Xs)


def _uls_bwd(res, dXs):
    Ts, Xs = res
    dRs = tuple(_dot3(T, dX, _TN) for T, dX in zip(Ts, dXs))
    dAs = tuple(-_dot3(dR, X, _NT) for dR, X in zip(dRs, Xs))
    return dAs, dRs


_unit_lower_solves.defvjp(_uls_fwd, _uls_bwd)


def _gdn_step(states, qb, kb_, vb, br, ar, alog, dtb, h0, *, rep):
    HB = len(states)
    L = qb.shape[0]
    DK, DV = states[0].shape
    HV = br.shape[1]
    incl = _iota((L, L), 0) >= _iota((L, L), 1)
    strict = _iota((L, L), 0) > _iota((L, L), 1)
    lane = _iota((L, HV), 1)
    g_all = -jnp.exp(alog) * _softplus(ar + dtb)
    gcs = _seldot(incl.astype(f32), g_all, ((1,), (0,)), 0)
    beta_all = jax.nn.sigmoid(br)
    nsel = max(HV, 8)
    gcs_t = _seldot((_iota((nsel, HV), 0) == _iota((nsel, HV), 1)).astype(f32), gcs, _NT, 0)
    hs = range(HB)
    q = [qb[:, (hh // rep) * DK:(hh // rep + 1) * DK] for hh in hs]
    k = [kb_[:, (hh // rep) * DK:(hh // rep + 1) * DK] for hh in hs]
    v = [vb[:, hh * DV:(hh + 1) * DV] for hh in hs]
    gc = [jnp.sum(jnp.where(lane == h0 + hh, gcs, 0.0), axis=1, keepdims=True) for hh in hs]
    beta = [jnp.sum(jnp.where(lane == h0 + hh, beta_all, 0.0), axis=1, keepdims=True) for hh in hs]
    gc_row = [jnp.sum(jnp.where(_iota((nsel, L), 0) == h0 + hh, gcs_t, 0.0), axis=0, keepdims=True) for hh in hs]
    decay = [jnp.where(incl, jnp.exp(jnp.where(incl, gc[hh] - gc_row[hh], 0.0)), 0.0) for hh in hs]
    kbeta = [k[hh] * beta[hh] for hh in hs]
    a_mat = [jnp.where(strict, _dot(kbeta[hh], k[hh], _NT) * decay[hh], 0.0) for hh in hs]
    eg = [jnp.exp(gc[hh]) for hh in hs]
    sol = _unit_lower_solves(tuple(a_mat), tuple(jnp.concatenate([v[hh] * beta[hh], kbeta[hh] * eg[hh]], axis=1) for hh in hs))
    qk = [jnp.where(incl, _dot(q[hh], k[hh], _NT) * decay[hh], 0.0) for hh in hs]
    g_last = [jnp.sum(jnp.where(_iota((L, 1), 0) == L - 1, gc[hh], 0.0), axis=0, keepdims=True) for hh in hs]
    v_new = [sol[hh][:, :DV] - _dot(sol[hh][:, DV:], states[hh]) for hh in hs]
    outs = [_dot(q[hh] * eg[hh], states[hh]) + _dot(qk[hh], v_new[hh]) for hh in hs]
    news = [states[hh] * jnp.exp(g_last[hh]) + _dot(k[hh] * jnp.exp(g_last[hh] - gc[hh]), v_new[hh], _TN) for hh in hs]
    return (outs[0] if HB == 1 else jnp.concatenate(outs, axis=1)), tuple(news)


def _gdn_dims():
    HK, HV = GDN_N_QK_HEADS, GDN_N_V_HEADS
    rep = HV // HK
    HB = min(GDN_HEADS_PER_STEP, HV)
    assert HV % HB == 0 and HB % rep == 0
    return HK, HV, GDN_DK, GDN_DV, GDN_CHUNK, rep, HB


def _gdn_scan_fwd(qkn, qkv, br, ar, alog, dtb, *, name, comm=None):
    S = qkn.shape[0]
    HK, HV, DK, DV, L, rep, HB = _gdn_dims()
    nc = S // L
    QW = HB // rep * DK
    koff = HK * DK // QW
    voff = 2 * HK * DK // (HB * DV)

    def body(q_ref, k_ref, v_ref, br_ref, ar_ref, alog_ref, dtb_ref, o_ref, st_ref, state):
        c, hb = pl.program_id(0), pl.program_id(1)
        h0 = hb * HB

        @pl.when(c == 0)
        def _():
            for hh in range(HB):
                state[h0 + hh] = jnp.zeros((DK, DV), f32)

        prev = tuple(state[h0 + hh] for hh in range(HB))
        for hh in range(HB):
            st_ref[0, hh] = prev[hh]
        o, new = _gdn_step(prev, q_ref[...], k_ref[...], v_ref[...], br_ref[...], ar_ref[...], alog_ref[...], dtb_ref[...],
                           h0, rep=rep)
        o_ref[...] = o
        for hh in range(HB):
            state[h0 + hh] = new[hh]

    par = pl.BlockSpec((1, HV), lambda c, h: (0, 0))
    return _call(
        body, comm, name=name, grid=(nc, HV // HB),
        in_specs=[pl.BlockSpec((L, QW), lambda c, h: (c, h)), pl.BlockSpec((L, QW), lambda c, h: (c, koff + h)),
                  pl.BlockSpec((L, HB * DV), lambda c, h: (c, voff + h)), pl.BlockSpec((L, HV), lambda c, h: (c, 0)),
                  pl.BlockSpec((L, HV), lambda c, h: (c, 0)), par, par],
        out_specs=[pl.BlockSpec((L, HB * DV), lambda c, h: (c, h)), pl.BlockSpec((1, HB, DK, DV), lambda c, h: (c, h, 0, 0))],
        out_shape=[jax.ShapeDtypeStruct((S, HV * DV), f32), jax.ShapeDtypeStruct((nc, HV, DK, DV), f32)],
        scratch_shapes=[pltpu.VMEM((HV, DK, DV), f32)],
        sem=("arbitrary", "arbitrary"), args=(qkn, qkn, qkv, br, ar, alog, dtb))


def _gdn_scan_bwd(qkn, qkv, br, ar, alog, dtb, states, do, *, name, comm=None):
    S = qkn.shape[0]
    HK, HV, DK, DV, L, rep, HB = _gdn_dims()
    nc = S // L
    QW = HB // rep * DK
    koff = HK * DK // QW
    voff = 2 * HK * DK // (HB * DV)

    def body(q_ref, k_ref, v_ref, br_ref, ar_ref, alog_ref, dtb_ref, st_ref, do_ref,
             dq_ref, dk_ref, dv_ref, dbr_ref, dar_ref, dalog_ref, ddtb_ref, dstate):
        c, hb = pl.program_id(0), pl.program_id(1)
        h0 = hb * HB

        @pl.when(c == 0)
        def _():
            for hh in range(HB):
                dstate[h0 + hh] = jnp.zeros((DK, DV), f32)

        @pl.when(jnp.logical_and(c == 0, hb == 0))
        def _():
            dalog_ref[...] = jnp.zeros_like(dalog_ref)
            ddtb_ref[...] = jnp.zeros_like(ddtb_ref)

        @pl.when(hb == 0)
        def _():
            dbr_ref[...] = jnp.zeros_like(dbr_ref)
            dar_ref[...] = jnp.zeros_like(dar_ref)

        fn = functools.partial(_gdn_step, h0=h0, rep=rep)
        prev = tuple(st_ref[0, hh] for hh in range(HB))
        _, vjp = jax.vjp(fn, prev, q_ref[...], k_ref[...], v_ref[...], br_ref[...], ar_ref[...], alog_ref[...], dtb_ref[...])
        dprev, dq, dk, dv, dbr, dar, dalog, ddtb = vjp((do_ref[...], tuple(dstate[h0 + hh] for hh in range(HB))))
        for hh in range(HB):
            dstate[h0 + hh] = dprev[hh]
        dq_ref[...] = dq
        dk_ref[...] = dk
        dv_ref[...] = dv
        dbr_ref[...] += dbr
        dar_ref[...] += dar
        dalog_ref[...] += dalog
        ddtb_ref[...] += ddtb

    rc = lambda c: nc - 1 - c
    par = pl.BlockSpec((1, HV), lambda c, h: (0, 0))
    blk = lambda W: pl.BlockSpec((L, W), lambda c, h: (rc(c), h))
    return _call(
        body, comm, name=name, grid=(nc, HV // HB),
        in_specs=[pl.BlockSpec((L, QW), lambda c, h: (rc(c), h)), pl.BlockSpec((L, QW), lambda c, h: (rc(c), koff + h)),
                  pl.BlockSpec((L, HB * DV), lambda c, h: (rc(c), voff + h)), pl.BlockSpec((L, HV), lambda c, h: (rc(c), 0)),
                  pl.BlockSpec((L, HV), lambda c, h: (rc(c), 0)), par, par,
                  pl.BlockSpec((1, HB, DK, DV), lambda c, h: (rc(c), h, 0, 0)), blk(HB * DV)],
        out_specs=[blk(QW), blk(QW), blk(HB * DV), pl.BlockSpec((L, HV), lambda c, h: (rc(c), 0)),
                   pl.BlockSpec((L, HV), lambda c, h: (rc(c), 0)), par, par],
        out_shape=[jax.ShapeDtypeStruct((S, HK * DK), f32), jax.ShapeDtypeStruct((S, HK * DK), f32), jax.ShapeDtypeStruct((S, HV * DV), f32),
                   jax.ShapeDtypeStruct((S, HV), f32), jax.ShapeDtypeStruct((S, HV), f32),
                   jax.ShapeDtypeStruct((1, HV), f32), jax.ShapeDtypeStruct((1, HV), f32)],
        scratch_shapes=[pltpu.VMEM((HV, DK, DV), f32)],
        sem=("arbitrary", "arbitrary"), args=(qkn, qkn, qkv, br, ar, alog, dtb, states, do))


def _att_scale():
    return (MLA_NOPE + MLA_ROPE) ** -0.5


def _causal(s, i, j, t):
    qpos = i * t + _iota(s.shape, 0)
    kpos = j * t + _iota(s.shape, 1)
    return kpos <= qpos


def _attn_fwd(qn, qr, kn, kr, v, *, name, comm=None):
    S, W = qn.shape
    H = W // LANES
    t = _tile(S, ATT_TILE)
    scale = _att_scale()

    def body(qn_ref, qr_ref, kn_ref, kr_ref, v_ref, o_ref, lse_ref):
        i = pl.program_id(1)
        qc = jnp.concatenate([qn_ref[...], qr_ref[...]], axis=1)

        def step(j, carry, masked):
            m, l, acc = carry
            rows = pl.ds(pl.multiple_of(j * t, t), t)
            s = _dot(qc, jnp.concatenate([kn_ref[rows, :], kr_ref[rows, :]], axis=1), _NT) * scale
            if masked:
                s = jnp.where(_causal(s, i, j, t), s, -1e30)
            m_new = jnp.maximum(m, jnp.max(s, axis=1, keepdims=True))
            p = jnp.exp(s - m_new)
            a = jnp.exp(m - m_new)
            return m_new, a * l + jnp.sum(p, axis=1, keepdims=True), a * acc + _dot(p, v_ref[rows, :])

        carry = lax.fori_loop(0, i, functools.partial(step, masked=False),
                              (jnp.full((t, 1), -1e30, f32), jnp.zeros((t, 1), f32), jnp.zeros((t, LANES), f32)))
        m, l, acc = step(i, carry, True)
        o_ref[...] = acc / l
        lse_ref[...] = jnp.broadcast_to(m + jnp.log(l), (t, LANES))

    qb = pl.BlockSpec((t, LANES), lambda h, i: (i, h))
    kb = pl.BlockSpec((S, LANES), lambda h, i: (0, h))
    return _call(
        body, comm, name=name, grid=(H, S // t),
        in_specs=[qb, qb, kb, pl.BlockSpec((S, LANES), lambda h, i: (0, 0)), kb],
        out_specs=[qb, qb], out_shape=[jax.ShapeDtypeStruct((S, W), f32), jax.ShapeDtypeStruct((S, W), f32)],
        scratch_shapes=[], sem=("parallel", "arbitrary"), args=(qn, qr, kn, kr, v))


def _attn_bwd_dq(qn, qr, kn, kr, v, o, lse, do, *, name, comm=None):
    S, W = qn.shape
    H = W // LANES
    t = _tile(S, ATT_TILE)
    scale = _att_scale()

    def body(qn_ref, qr_ref, kn_ref, kr_ref, v_ref, o_ref, lse_ref, do_ref, dqn_ref, dqr_ref):
        i = pl.program_id(1)
        qc = jnp.concatenate([qn_ref[...], qr_ref[...]], axis=1)
        dov = do_ref[...]
        delta = jnp.sum(dov * o_ref[...], axis=1, keepdims=True)
        lsev = lse_ref[...][:, :1]

        def step(j, dq, masked):
            rows = pl.ds(pl.multiple_of(j * t, t), t)
            kc = jnp.concatenate([kn_ref[rows, :], kr_ref[rows, :]], axis=1)
            s = _dot(qc, kc, _NT) * scale
            p = jnp.exp(s - lsev)
            if masked:
                p = jnp.where(_causal(s, i, j, t), p, 0.0)
            ds = p * (_dot(dov, v_ref[rows, :], _NT) - delta) * scale
            return dq + _dot(ds, kc)

        dq = lax.fori_loop(0, i, functools.partial(step, masked=False), jnp.zeros((t, 2 * LANES), f32))
        dq = step(i, dq, True)
        dqn_ref[...] = dq[:, :LANES].astype(MXU_DT)
        dqr_ref[...] = dq[:, LANES:]

    qb = pl.BlockSpec((t, LANES), lambda h, i: (i, h))
    kb = pl.BlockSpec((S, LANES), lambda h, i: (0, h))
    return _call(
        body, comm, name=name, grid=(H, S // t),
        in_specs=[qb, qb, kb, pl.BlockSpec((S, LANES), lambda h, i: (0, 0)), kb, qb, qb, qb],
        out_specs=[qb, qb], out_shape=[jax.ShapeDtypeStruct((S, W), MXU_DT), jax.ShapeDtypeStruct((S, W), f32)],
        scratch_shapes=[], sem=("parallel", "arbitrary"), args=(qn, qr, kn, kr, v, o, lse, do))


def _attn_bwd_dkv(qn, qr, kn, kr, v, o, lse, do, *, name, comm=None):
    S, W = qn.shape
    H = W // LANES
    t = _tile(S, ATT_TILE)
    nb = S // t
    scale = _att_scale()

    def body(qn_ref, qr_ref, kn_ref, kr_ref, v_ref, o_ref, lse_ref, do_ref, dkn_ref, dkr_ref, dv_ref):
        j, h = pl.program_id(0), pl.program_id(1)
        kc = jnp.concatenate([kn_ref[...], kr_ref[...]], axis=1)
        vv = v_ref[...]

        def step(i, carry, masked):
            dk, dv = carry
            rows = pl.ds(pl.multiple_of(i * t, t), t)
            qc = jnp.concatenate([qn_ref[rows, :], qr_ref[rows, :]], axis=1)
            dov = do_ref[rows, :]
            delta = jnp.sum(dov * o_ref[rows, :], axis=1, keepdims=True)
            s = _dot(qc, kc, _NT) * scale
            p = jnp.exp(s - lse_ref[rows, :][:, :1])
            if masked:
                p = jnp.where(_causal(s, i, j, t), p, 0.0)
            ds = p * (_dot(dov, vv, _NT) - delta) * scale
            return dk + _dot(ds, qc, _TN), dv + _dot(p, dov, _TN)

        carry = step(j, (jnp.zeros((t, 2 * LANES), f32), jnp.zeros((t, LANES), f32)), True)
        dk, dv = lax.fori_loop(j + 1, nb, functools.partial(step, masked=False), carry)
        dkn_ref[...] = dk[:, :LANES].astype(MXU_DT)
        dv_ref[...] = dv.astype(MXU_DT)

        @pl.when(h == 0)
        def _():
            dkr_ref[...] = jnp.zeros_like(dkr_ref)

        dkr_ref[...] += dk[:, LANES:]

    full = pl.BlockSpec((S, LANES), lambda j, h: (0, h))
    kb = pl.BlockSpec((t, LANES), lambda j, h: (j, h))
    k0 = pl.BlockSpec((t, LANES), lambda j, h: (j, 0))
    return _call(
        body, comm, name=name, grid=(nb, H),
        in_specs=[full, full, kb, k0, kb, full, full, full],
        out_specs=[kb, k0, kb],
        out_shape=[jax.ShapeDtypeStruct((S, W), MXU_DT), jax.ShapeDtypeStruct((S, LANES), f32), jax.ShapeDtypeStruct((S, W), MXU_DT)],
        scratch_shapes=[], sem=("arbitrary", "arbitrary"), args=(qn, qr, kn, kr, v, o, lse, do))


def _loss_head(y, target, *, name):
    S, D = y.shape
    tm = _tile(S, ROW_TILE, 8)

    def body(y_ref, t_ref, loss_ref, dy_ref):
        @pl.when(pl.program_id(0) == 0)
        def _():
            loss_ref[...] = jnp.zeros_like(loss_ref)

        e = y_ref[...] - t_ref[...]
        dy_ref[...] = e / D
        part = 0.5 * jnp.sum(jnp.mean(e * e, axis=1, keepdims=True), axis=0, keepdims=True)
        loss_ref[...] += jnp.broadcast_to(part, loss_ref.shape)

    rb = pl.BlockSpec((tm, D), lambda r: (r, 0))
    return pl.pallas_call(
        body, name=name, grid=(S // tm,), in_specs=[rb, rb],
        out_specs=[pl.BlockSpec((1, LANES), lambda r: (0, 0)), rb],
        out_shape=[jax.ShapeDtypeStruct((1, LANES), f32), jax.ShapeDtypeStruct((S, D), f32)],
        compiler_params=_cp(("arbitrary",)))(y, target)


def _adamw(w, g, m, v, *, name):
    R, C = w.shape
    tm = _tile(R, max(8, (1 << 19) // max(C, 1) // 8 * 8), 8)

    def body(w_ref, g_ref, m_ref, v_ref, d_ref, nm_ref, nv_ref):
        gv = g_ref[...]
        nm = ADAM_B1 * m_ref[...] + (1.0 - ADAM_B1) * gv
        nv = ADAM_B2 * v_ref[...] + (1.0 - ADAM_B2) * (gv * gv)
        m_hat = nm / (1.0 - ADAM_B1 ** ADAM_STEP)
        v_hat = nv / (1.0 - ADAM_B2 ** ADAM_STEP)
        d_ref[...] = -ADAM_LR * (m_hat / (jnp.sqrt(v_hat) + ADAM_EPS) + ADAM_WD * w_ref[...])
        nm_ref[...] = nm
        nv_ref[...] = nv

    rb = pl.BlockSpec((tm, C), lambda r: (r, 0))
    sh = jax.ShapeDtypeStruct((R, C), f32)
    return pl.pallas_call(body, name=name, grid=(R // tm,), in_specs=[rb] * 4, out_specs=[rb] * 3, out_shape=[sh] * 3,
                          compiler_params=_cp(("parallel",)))(w, g, m, v)


def _me():
    return lax.axis_index("x"), lax.axis_index("y"), lax.axis_index("c")


def _other_chips(mx, my):
    return [(1 - mx, my), (mx, 1 - my), (1 - mx, 1 - my)]


_ANY = pl.BlockSpec(memory_space=pl.ANY)


class _GatherChips:
    def __init__(self, xs):
        self.arrays = list(xs)
        n = len(xs)
        for x in xs:
            assert x.shape[0] % 2 == 0
        self.halves = [x.shape[0] // 2 for x in xs]
        self.out_shapes = [jax.ShapeDtypeStruct((4,) + x.shape, x.dtype) for x in xs]
        self.scratch = [pltpu.SemaphoreType.DMA((n, 6)), pltpu.SemaphoreType.DMA((n, 6))]

    def _sends(self, x_refs, o_refs, send, recv):
        mx, my, mc = _me()
        me = 2 * mx + my
        out = []
        for t, hf in enumerate(self.halves):
            mine = pl.ds(mc * hf, hf)
            for j, (cx, cy) in enumerate(_other_chips(mx, my)):
                out.append(pltpu.make_async_remote_copy(x_refs[t].at[mine], o_refs[t].at[me, mine], send.at[t, j], recv.at[t, j],
                                                        device_id=(cx, cy, mc), device_id_type=MESH))
        return out

    def start(self, x_refs, o_refs, scr):
        for cp in self._sends(x_refs, o_refs, *scr):
            cp.start()

    def finish(self, x_refs, o_refs, scr):
        send, recv = scr
        mx, my, mc = _me()
        chips = _other_chips(mx, my)
        fwd = []
        for t, hf in enumerate(self.halves):
            mine = pl.ds(mc * hf, hf)
            for j, (cx, cy) in enumerate(chips):
                k = 2 * cx + cy
                pltpu.make_async_remote_copy(x_refs[t].at[mine], o_refs[t].at[k, mine], send.at[t, j], recv.at[t, j],
                                             device_id=(cx, cy, mc), device_id_type=MESH).wait_recv()
                cp = pltpu.make_async_remote_copy(o_refs[t].at[k, mine], o_refs[t].at[k, mine], send.at[t, 3 + j], recv.at[t, 3 + j],
                                                  device_id=(mx, my, 1 - mc), device_id_type=MESH)
                cp.start()
                fwd.append(cp)
        for t, hf in enumerate(self.halves):
            theirs = pl.ds((1 - mc) * hf, hf)
            for j, (cx, cy) in enumerate(chips):
                k = 2 * cx + cy
                pltpu.make_async_remote_copy(o_refs[t].at[k, theirs], o_refs[t].at[k, theirs], send.at[t, 3 + j], recv.at[t, 3 + j],
                                             device_id=(mx, my, 1 - mc), device_id_type=MESH).wait_recv()
        for cp in self._sends(x_refs, o_refs, send, recv) + fwd:
            cp.wait_send()


class _ScatterChips:
    def __init__(self, ps):
        self.arrays = list(ps)
        n = len(ps)
        self.out_shapes = [jax.ShapeDtypeStruct((3,) + p.shape[1:], p.dtype) for p in ps]
        self.scratch = [pltpu.SemaphoreType.DMA((n, 3)), pltpu.SemaphoreType.DMA((n, 3))]

    def _copies(self, p_refs, o_refs, send, recv):
        mx, my, mc = _me()
        return [pltpu.make_async_remote_copy(p_refs[t].at[2 * cx + cy], o_refs[t].at[j], send.at[t, j], recv.at[t, j],
                                             device_id=(cx, cy, mc), device_id_type=MESH)
                for t in range(len(self.arrays)) for j, (cx, cy) in enumerate(_other_chips(mx, my))]

    def start(self, p_refs, o_refs, scr):
        for cp in self._copies(p_refs, o_refs, *scr):
            cp.start()

    def finish(self, p_refs, o_refs, scr):
        for cp in self._copies(p_refs, o_refs, *scr):
            cp.wait()


def _run_comm(comm, *, name):
    n = len(comm.arrays)

    def body(*refs):
        ins, outs, scr = refs[:n], refs[n:2 * n], refs[2 * n:]
        comm.start(ins, outs, scr)
        comm.finish(ins, outs, scr)

    return pl.pallas_call(body, name=name, in_specs=[_ANY] * n, out_specs=[_ANY] * n, out_shape=comm.out_shapes,
                          scratch_shapes=comm.scratch, compiler_params=pltpu.CompilerParams(has_side_effects=True))(*comm.arrays)


def _call(body, comm, *, name, grid, in_specs, out_specs, out_shape, scratch_shapes, sem, args):
    if comm is None:
        res = pl.pallas_call(body, name=name, grid=grid, in_specs=in_specs, out_specs=out_specs, out_shape=out_shape,
                             scratch_shapes=scratch_shapes, compiler_params=_cp(sem))(*args)
        return list(res), None
    n_in, n_out, n_scr, nc = len(in_specs), len(out_specs), len(scratch_shapes), len(comm.arrays)

    def wrapped(*refs):
        ins, cins = refs[:n_in], refs[n_in:n_in + nc]
        outs, couts = refs[n_in + nc:n_in + nc + n_out], refs[n_in + nc + n_out:n_in + 2 * nc + n_out]
        scr, cscr = refs[n_in + 2 * nc + n_out:n_in + 2 * nc + n_out + n_scr], refs[n_in + 2 * nc + n_out + n_scr:]
        ids = [pl.program_id(d) for d in range(len(grid))]
        first = functools.reduce(jnp.logical_and, [i == 0 for i in ids])
        last = functools.reduce(jnp.logical_and, [i == g - 1 for i, g in zip(ids, grid)])

        @pl.when(first)
        def _():
            comm.start(cins, couts, cscr)

        body(*ins, *outs, *scr)

        @pl.when(last)
        def _():
            comm.finish(cins, couts, cscr)

    res = pl.pallas_call(
        wrapped, name=name, grid=grid, in_specs=list(in_specs) + [_ANY] * nc, out_specs=list(out_specs) + [_ANY] * nc,
        out_shape=list(out_shape) + comm.out_shapes, scratch_shapes=list(scratch_shapes) + comm.scratch,
        compiler_params=_cp(("arbitrary",) * len(grid)))(*args, *comm.arrays)
    return list(res[:n_out]), list(res[n_out:])


class _PairSend:
    def __init__(self, gs):
        self.arrays = list(gs)
        n = len(gs)
        self.halves = [g.shape[1] // 2 for g in gs]
        self.out_shapes = [jax.ShapeDtypeStruct((4, g.shape[1] // 2, g.shape[2]), g.dtype) for g in gs]
        self.scratch = [pltpu.SemaphoreType.DMA((n, 4)), pltpu.SemaphoreType.DMA((n, 4))]

    def _copies(self, g_refs, o_refs, send, recv):
        mx, my, mc = _me()
        return [pltpu.make_async_remote_copy(g_refs[t].at[k, pl.ds((1 - mc) * hf, hf)], o_refs[t].at[k], send.at[t, k], recv.at[t, k],
                                             device_id=(mx, my, 1 - mc), device_id_type=MESH)
                for t, hf in enumerate(self.halves) for k in range(4)]

    def start(self, g_refs, o_refs, scr):
        for cp in self._copies(g_refs, o_refs, *scr):
            cp.start()

    def finish(self, g_refs, o_refs, scr):
        for cp in self._copies(g_refs, o_refs, *scr):
            cp.wait()


def _pair_exchange_halves(fs, *, name):
    n = len(fs)

    def body(*refs):
        f_refs, o_refs = refs[:n], refs[n:2 * n]
        send, recv = refs[2 * n:]
        mx, my, mc = _me()
        cps = []
        for t in range(n):
            hf = f_refs[t].shape[0]
            mine = pl.ds(mc * hf, hf)
            cp = pltpu.make_async_remote_copy(f_refs[t], o_refs[t].at[mine], send.at[t], recv.at[t],
                                              device_id=(mx, my, 1 - mc), device_id_type=MESH)
            cp.start()
            cps.append(cp)
        for t in range(n):
            hf = f_refs[t].shape[0]
            theirs = pl.ds((1 - mc) * hf, hf)
            cps[t].wait_send()
            pltpu.make_async_remote_copy(f_refs[t], o_refs[t].at[theirs], send.at[t], recv.at[t],
                                         device_id=(mx, my, 1 - mc), device_id_type=MESH).wait_recv()

    return pl.pallas_call(
        body, name=name, in_specs=[_ANY] * n, out_specs=[_ANY] * n,
        out_shape=[jax.ShapeDtypeStruct((2 * f.shape[0], f.shape[1]), f.dtype) for f in fs],
        scratch_shapes=[pltpu.SemaphoreType.DMA((n,)), pltpu.SemaphoreType.DMA((n,))],
        compiler_params=pltpu.CompilerParams(has_side_effects=True))(*fs)


def _allgather_all(x, *, name):
    def body(x_ref, o_ref, send, recv, lsem):
        mx, my, mc = _me()
        me = 4 * mx + 2 * my + mc
        local = pltpu.make_async_copy(x_ref, o_ref.at[me], lsem)
        local.start()
        cps = []
        for j in range(1, 8):
            px, py, pc = mx ^ (j >> 2), my ^ ((j >> 1) & 1), mc ^ (j & 1)
            cp = pltpu.make_async_remote_copy(x_ref, o_ref.at[me], send.at[j - 1], recv.at[j - 1],
                                              device_id=(px, py, pc), device_id_type=MESH)
            cp.start()
            cps.append(cp)
        for j in range(1, 8):
            px, py, pc = mx ^ (j >> 2), my ^ ((j >> 1) & 1), mc ^ (j & 1)
            pltpu.make_async_remote_copy(x_ref, o_ref.at[4 * px + 2 * py + pc], send.at[j - 1], recv.at[j - 1],
                                         device_id=(px, py, pc), device_id_type=MESH).wait_recv()
        for cp in cps:
            cp.wait_send()
        local.wait()

    return pl.pallas_call(
        body, name=name, in_specs=[_ANY], out_specs=_ANY, out_shape=jax.ShapeDtypeStruct((8,) + x.shape, x.dtype),
        scratch_shapes=[pltpu.SemaphoreType.DMA((7,)), pltpu.SemaphoreType.DMA((7,)), pltpu.SemaphoreType.DMA],
        compiler_params=pltpu.CompilerParams(has_side_effects=True))(x)


def _add_half(g4, recv, mc, *, name):
    _, R, C = g4.shape
    hf = R // 2
    tm = _tile(hf, max(16, (1 << 19) // C // 16 * 16), 16)
    nb = hf // tm

    def body(mc_ref, g_ref, r_ref, o_ref, ob_ref):
        s = g_ref[...] + r_ref[...]
        o_ref[...] = s
        ob_ref[...] = s.astype(COMM_DT)

    ospec = pl.BlockSpec((1, tm, C), lambda k, i, mc_ref: (k, i, 0))
    return pl.pallas_call(
        body, name=name,
        grid_spec=pltpu.PrefetchScalarGridSpec(
            num_scalar_prefetch=1, grid=(4, nb),
            in_specs=[pl.BlockSpec((1, tm, C), lambda k, i, mc_ref: (k, mc_ref[0] * nb + i, 0)),
                      pl.BlockSpec((1, tm, C), lambda k, i, mc_ref: (k, i, 0))],
            out_specs=[ospec, ospec]),
        out_shape=[jax.ShapeDtypeStruct((4, hf, C), f32), jax.ShapeDtypeStruct((4, hf, C), COMM_DT)],
        compiler_params=_cp(("parallel", "parallel")))(mc, g4, recv)


def _sum_chips(p4, recv3, me, *, name):
    _, Rh, C = p4.shape
    tm = _tile(Rh, max(16, (1 << 19) // C // 16 * 16), 16)

    def body(me_ref, p_ref, r_ref, o_ref):
        o_ref[...] = ((p_ref[0] + r_ref[0].astype(f32)) + r_ref[1].astype(f32)) + r_ref[2].astype(f32)

    return pl.pallas_call(
        body, name=name,
        grid_spec=pltpu.PrefetchScalarGridSpec(
            num_scalar_prefetch=1, grid=(Rh // tm,),
            in_specs=[pl.BlockSpec((1, tm, C), lambda i, me_ref: (me_ref[0], i, 0)),
                      pl.BlockSpec((3, tm, C), lambda i, me_ref: (0, i, 0))],
            out_specs=pl.BlockSpec((tm, C), lambda i, me_ref: (i, 0))),
        out_shape=jax.ShapeDtypeStruct((Rh, C), f32),
        compiler_params=_cp(("parallel",)))(me, p4, recv3)


def _sum8(x8, *, name):
    _, R, C = x8.shape
    tm = _tile(R, 64, 8)

    def body(x_ref, o_ref):
        acc = x_ref[0]
        for k in range(1, 8):
            acc = acc + x_ref[k]
        o_ref[...] = acc

    return pl.pallas_call(body, name=name, grid=(R // tm,), in_specs=[pl.BlockSpec((8, tm, C), lambda i: (0, i, 0))],
                          out_specs=pl.BlockSpec((tm, C), lambda i: (i, 0)), out_shape=jax.ShapeDtypeStruct((R, C), f32),
                          compiler_params=_cp(("parallel",)))(x8)


def _ssd_layer_fwd(h, W, tag, comm=None):
    z = _mm(h, W["wz"], name=tag + "_z")
    xp = _mm(h, W["wxbc"], name=tag + "_xbc")
    dtr = _mm(h, W["wdt"], name=tag + "_dt")
    xbc = _conv_fwd(xp, W["conv_w"], W["conv_b"], name=tag + "_conv")
    (y, states), cres = _ssd_scan_fwd(xbc, dtr, W["dt_bias"], W["a_log"], W["d"], name=tag + "_scan", comm=comm)
    DI = y.shape[1]
    G = SSD_N_GROUPS
    gs = DI // G
    (yn,) = _rowwise(_ssd_gate_fn, [(W["norm_w"], "c", gs)], [(y, "c", gs), (z, "c", gs)], [(DI, MXU_DT, "c", gs)],
                     name=tag + "_gate", ncol=G, tm=512)
    out = _mm(yn, W["wout"], name=tag + "_out")
    return out, dict(h=h, z=z, xp=xp, dtr=dtr, xbc=xbc, states=states, y=y, yn=yn), cres


def _carried_rowwise_bwd(plan, i, *a, **kw):
    early = plan.bwd_early_comm(i)
    if early is None:
        return _rowwise_bwd(*a, **kw)
    dp, di, cres = _rowwise_bwd(*a, comm=early, **kw)
    plan.bwd_early_done(i, cres)
    return dp, di


def _ssd_layer_bwd(sv, W, dr, drb, tag, plan, i):
    h = sv["h"]
    DI = sv["y"].shape[1]
    G = SSD_N_GROUPS
    gs = DI // G
    gr = {}
    dyn = _mm(drb, W["wout"], tb=True, name=tag + "_dyn")
    gr["wout"] = _mm(sv["yn"], drb, ta=True, name=tag + "_dwout")
    (dnw,), (dy, dz) = _carried_rowwise_bwd(plan, i, _ssd_gate_fn, [(W["norm_w"], "c", gs)], [(sv["y"], "c", gs), (sv["z"], "c", gs)],
                                             [(dyn, "c", gs)], name=tag + "_dgate", ncol=G, tm=512, din_dtypes=[(f32,), (MXU_DT,)])
    gr["norm_w"] = dnw
    (dxs, dB, dC, ddtr, gr["dt_bias"], gr["a_log"], gr["d"]), cres = _ssd_scan_bwd(
        sv["xbc"], sv["dtr"], W["dt_bias"], W["a_log"], W["d"], sv["states"], dy, name=tag + "_dscan", comm=plan.bwd_comm(i))
    plan.bwd_done(i, cres)
    dxp, gr["conv_w"], gr["conv_b"] = _conv_bwd(sv["xp"], W["conv_w"], W["conv_b"], [dxs, dB, dC], name=tag + "_dconv", dx_dtype=MXU_DT)
    dh = _mm(dz, W["wz"], tb=True, add=dr, add_scale=_alpha(), name=tag + "_dh1")
    dh = _mm(dxp, W["wxbc"], tb=True, add=dh, name=tag + "_dh2")
    dh = _mm(ddtr, W["wdt"], tb=True, add=dh, name=tag + "_dh3")
    gr["wz"] = _mm(h, dz, ta=True, name=tag + "_dwz")
    gr["wxbc"] = _mm(h, dxp, ta=True, name=tag + "_dwxbc")
    gr["wdt"] = _mm(h, ddtr, ta=True, name=tag + "_dwdt")
    return dh, gr


def _mla_layer_fwd(h, W, cos, sin, tag, comm=None):
    QR, KR = W["wqc"].shape[1], W["wkvc"].shape[1]
    HW = W["wqn"].shape[1]
    H = HW // LANES
    qc = _mm(h, W["wqc"], name=tag + "_qc")
    kvc = _mm(h, W["wkvc"], name=tag + "_kvc")
    krp = _mm(h, W["wkr"], name=tag + "_krp")
    z = _mm(h, W["wz"], name=tag + "_z")
    (qcn,) = _rowwise(_rms_fn, [(W["q_norm"], "a", QR)], [(qc, "a", QR)], [(QR, MXU_DT, "a", QR)], name=tag + "_qnorm")
    (kvn,) = _rowwise(_rms_fn, [(W["kv_norm"], "a", KR)], [(kvc, "a", KR)], [(KR, MXU_DT, "a", KR)], name=tag + "_kvnorm")
    qn = _mm(qcn, W["wqn"], name=tag + "_qn", out_dtype=MXU_DT)
    qrp = _mm(qcn, W["wqr"], name=tag + "_qrp")
    kn = _mm(kvn, W["wkn"], name=tag + "_kn", out_dtype=MXU_DT)
    v = _mm(kvn, W["wv"], name=tag + "_v", out_dtype=MXU_DT)
    (qr,) = _rowwise(_rope_fn, [], [(cos, "a", LANES), (sin, "a", LANES), (qrp, "c", LANES)], [(HW, MXU_DT, "c", LANES)],
                     name=tag + "_qrope", ncol=H, tm=1024)
    (kr,) = _rowwise(_rope_fn, [], [(cos, "a", LANES), (sin, "a", LANES), (krp, "a", LANES)], [(LANES, MXU_DT, "a", LANES)],
                     name=tag + "_krope")
    (o, lse), cres = _attn_fwd(qn, qr, kn, kr, v, name=tag + "_attn", comm=comm)
    (og,) = _rowwise(_mul_silu_fn, [], [(o, "a", HW), (z, "a", HW)], [(HW, MXU_DT, "a", HW)], name=tag + "_ogate")
    out = _mm(og, W["wout"], name=tag + "_out")
    return out, dict(h=h, qc=qc, kvc=kvc, z=z, qcn=qcn, kvn=kvn, qn=qn, qr=qr, kn=kn, kr=kr, v=v, o=o, lse=lse, og=og), cres


def _mla_layer_bwd(sv, W, cos, sin, dr, drb, tag, plan, i):
    h = sv["h"]
    QR, KR = W["wqc"].shape[1], W["wkvc"].shape[1]
    HW = W["wqn"].shape[1]
    H = HW // LANES
    gr = {}
    dog = _mm(drb, W["wout"], tb=True, name=tag + "_dog")
    gr["wout"] = _mm(sv["og"], drb, ta=True, name=tag + "_dwout")
    _, (do, dz) = _rowwise_bwd(_mul_silu_fn, [], [(sv["o"], "a", HW), (sv["z"], "a", HW)], [(dog, "a", HW)], name=tag + "_dogate",
                               din_dtypes=[(f32,), (MXU_DT,)])
    att = (sv["qn"], sv["qr"], sv["kn"], sv["kr"], sv["v"], sv["o"], sv["lse"], do)
    (dqn, dqr), cres = _attn_bwd_dq(*att, name=tag + "_dq", comm=plan.bwd_early_comm(i))
    plan.bwd_early_done(i, cres)
    (dkn, dkr, dv), cres = _attn_bwd_dkv(*att, name=tag + "_dkv", comm=plan.bwd_comm(i))
    plan.bwd_done(i, cres)
    _, (dqrp,) = _rowwise_bwd(_rope_fn, [], [(cos, "a", LANES), (sin, "a", LANES), (dqr, "c", LANES)], [(dqr, "c", LANES)],
                              name=tag + "_dqrope", ncol=H, tm=1024, diff_i=[2], din_dtypes=[(MXU_DT,)])
    _, (dkrp,) = _rowwise_bwd(_rope_fn, [], [(cos, "a", LANES), (sin, "a", LANES), (dkr, "a", LANES)], [(dkr, "a", LANES)],
                              name=tag + "_dkrope", diff_i=[2], din_dtypes=[(MXU_DT,)])
    dqcn = _mm(dqn, W["wqn"], tb=True, name=tag + "_dqcn1")
    dqcn = _mm(dqrp, W["wqr"], tb=True, add=dqcn, name=tag + "_dqcn2")
    dkvn = _mm(dkn, W["wkn"], tb=True, name=tag + "_dkvn1")
    dkvn = _mm(dv, W["wv"], tb=True, add=dkvn, name=tag + "_dkvn2")
    gr["wqn"] = _mm(sv["qcn"], dqn, ta=True, name=tag + "_dwqn")
    gr["wqr"] = _mm(sv["qcn"], dqrp, ta=True, name=tag + "_dwqr")
    gr["wkn"] = _mm(sv["kvn"], dkn, ta=True, name=tag + "_dwkn")
    gr["wv"] = _mm(sv["kvn"], dv, ta=True, name=tag + "_dwv")
    (gr["q_norm"],), (dqc,) = _rowwise_bwd(_rms_fn, [(W["q_norm"], "a", QR)], [(sv["qc"], "a", QR)], [(dqcn, "a", QR)], name=tag + "_dqnorm",
                                           din_dtypes=[(MXU_DT,)])
    (gr["kv_norm"],), (dkvc,) = _rowwise_bwd(_rms_fn, [(W["kv_norm"], "a", KR)], [(sv["kvc"], "a", KR)], [(dkvn, "a", KR)], name=tag + "_dkvnorm",
                                             din_dtypes=[(MXU_DT,)])
    dh = _mm(dz, W["wz"], tb=True, add=dr, add_scale=_alpha(), name=tag + "_dh1")
    dh = _mm(dqc, W["wqc"], tb=True, add=dh, name=tag + "_dh2")
    dh = _mm(dkvc, W["wkvc"], tb=True, add=dh, name=tag + "_dh3")
    dh = _mm(dkrp, W["wkr"], tb=True, add=dh, name=tag + "_dh4")
    gr["wz"] = _mm(h, dz, ta=True, name=tag + "_dwz")
    gr["wqc"] = _mm(h, dqc, ta=True, name=tag + "_dwqc")
    gr["wkvc"] = _mm(h, dkvc, ta=True, name=tag + "_dwkvc")
    gr["wkr"] = _mm(h, dkrp, ta=True, name=tag + "_dwkr")
    return dh, gr


def _gdn_layer_fwd(h, W, tag, comm=None):
    HK, HV, DK, DV = GDN_N_QK_HEADS, GDN_N_V_HEADS, GDN_DK, GDN_DV
    KD, VD = HK * DK, HV * DV
    qkvp = _mm(h, W["wqkv"], name=tag + "_qkv")
    z = _mm(h, W["wz"], name=tag + "_z")
    br = _mm(h, W["wb"], name=tag + "_b")
    ar = _mm(h, W["wa"], name=tag + "_a")
    qkv = _conv_fwd(qkvp, W["conv_w"], jnp.zeros((1, qkvp.shape[1]), f32), name=tag + "_conv")
    scale = jnp.concatenate([jnp.full((1, KD), DK ** -0.5, f32), jnp.ones((1, KD), f32)], axis=1)
    (qkn,) = _rowwise(_l2_fn, [(scale, "c", DK)], [(qkv, "c", DK)], [(2 * KD, f32, "c", DK)], name=tag + "_l2", ncol=2 * HK, tm=2048)
    (o, states), cres = _gdn_scan_fwd(qkn, qkv, br, ar, W["a_log"], W["dt_bias"], name=tag + "_scan", comm=comm)
    (on,) = _rowwise(_gdn_gate_fn, [(W["norm_w"], "a", DV)], [(o, "c", DV), (z, "c", DV)], [(VD, MXU_DT, "c", DV)],
                     name=tag + "_gate", ncol=HV, tm=1024)
    out = _mm(on, W["wout"], name=tag + "_out")
    return out, dict(h=h, qkvp=qkvp, z=z, br=br, ar=ar, qkv=qkv, qkn=qkn, o=o, states=states, on=on, scale=scale), cres


def _gdn_layer_bwd(sv, W, dr, drb, tag, plan, i):
    h = sv["h"]
    HK, HV, DK, DV = GDN_N_QK_HEADS, GDN_N_V_HEADS, GDN_DK, GDN_DV
    KD, VD = HK * DK, HV * DV
    gr = {}
    don = _mm(drb, W["wout"], tb=True, name=tag + "_don")
    gr["wout"] = _mm(sv["on"], drb, ta=True, name=tag + "_dwout")
    (gr["norm_w"],), (do, dz) = _carried_rowwise_bwd(plan, i, _gdn_gate_fn, [(W["norm_w"], "a", DV)], [(sv["o"], "c", DV), (sv["z"], "c", DV)],
                                                      [(don, "c", DV)], name=tag + "_dgate", ncol=HV, tm=1024, din_dtypes=[(f32,), (MXU_DT,)])
    (dq, dk, dv, dbr, dar, gr["a_log"], gr["dt_bias"]), cres = _gdn_scan_bwd(
        sv["qkn"], sv["qkv"], sv["br"], sv["ar"], W["a_log"], W["dt_bias"], sv["states"], do, name=tag + "_dscan", comm=plan.bwd_comm(i))
    plan.bwd_done(i, cres)
    _, (dqq,) = _rowwise_bwd(_l2_fn, [(sv["scale"], "c", DK)], [(sv["qkv"], "c", DK)], [(dq, "c", DK)],
                             name=tag + "_dl2q", ncol=HK, tm=2048, diff_p=[])
    _, (dqk,) = _rowwise_bwd(_l2_fn, [(sv["scale"], "c", DK, HK)], [(sv["qkv"], "c", DK, HK)], [(dk, "c", DK)],
                             name=tag + "_dl2k", ncol=HK, tm=2048, diff_p=[])
    dqkvp, gr["conv_w"], _ = _conv_bwd(sv["qkvp"], W["conv_w"], jnp.zeros((1, sv["qkvp"].shape[1]), f32), [dqq, dqk, dv],
                                       name=tag + "_dconv", dx_dtype=MXU_DT)
    dh = _mm(dz, W["wz"], tb=True, add=dr, add_scale=_alpha(), name=tag + "_dh1")
    dh = _mm(dqkvp, W["wqkv"], tb=True, add=dh, name=tag + "_dh2")
    dh = _mm(dbr, W["wb"], tb=True, add=dh, name=tag + "_dh3")
    dh = _mm(dar, W["wa"], tb=True, add=dh, name=tag + "_dh4")
    gr["wz"] = _mm(h, dz, ta=True, name=tag + "_dwz")
    gr["wqkv"] = _mm(h, dqkvp, ta=True, name=tag + "_dwqkv")
    gr["wb"] = _mm(h, dbr, ta=True, name=tag + "_dwb")
    gr["wa"] = _mm(h, dar, ta=True, name=tag + "_dwa")
    return dh, gr


def _rope_tables(positions):
    half = MLA_ROPE // 2
    inv_freq = ROPE_THETA ** (-jnp.arange(0, MLA_ROPE, 2, dtype=f32) / MLA_ROPE)
    ang = positions.astype(f32)[:, None] * inv_freq
    cos, sin = jnp.cos(ang), jnp.sin(ang)
    S = positions.shape[0]
    pad = jnp.zeros((S, LANES - 2 * half), f32)
    return jnp.concatenate([cos, cos, pad + 1.0], axis=1), jnp.concatenate([sin, sin, pad], axis=1)


class _LocalPlan:
    def __init__(self, LW):
        self.LW, self.grads = LW, [None] * DEPTH

    def weights(self, i):
        return self.LW[i]

    def fwd_comm(self, i):
        return None

    def fwd_done(self, i, res):
        pass

    def bwd_early_comm(self, i):
        return None

    def bwd_early_done(self, i, res):
        pass

    def bwd_comm(self, i):
        return None

    def bwd_done(self, i, res):
        pass

    def layer_grads(self, i, gr):
        self.grads[i] = gr


def _local_step(x, positions, target, ln_g, ln_b, plan):
    cos, sin = _rope_tables(positions)
    h, hb = x, x.astype(MXU_DT)
    saved, LW = [], []
    for i in range(DEPTH):
        kind, tag = i % 3, "l%d" % i
        LW.append(plan.weights(i))
        comm = plan.fwd_comm(i)
        if kind == 0:
            y, sv, cres = _ssd_layer_fwd(hb, LW[i], tag, comm)
        elif kind == 1:
            y, sv, cres = _mla_layer_fwd(hb, LW[i], cos, sin, tag, comm)
        else:
            y, sv, cres = _gdn_layer_fwd(hb, LW[i], tag, comm)
        plan.fwd_done(i, cres)
        D = h.shape[1]
        r, h, hb = _rowwise(_res_ln_fn, [(ln_g[i], "a", D), (ln_b[i], "a", D)], [(h, "a", D), (y, "a", D)],
                            [(D, f32, "a", D), (D, f32, "a", D), (D, MXU_DT, "a", D)], name=tag + "_ln")
        sv["r"] = r
        saved.append(sv)
    loss, dh = _loss_head(h, target, name="loss_head")
    dg, db = [None] * DEPTH, [None] * DEPTH
    for i in reversed(range(DEPTH)):
        kind, tag = i % 3, "l%d" % i
        sv = saved[i]
        D = dh.shape[1]
        (dg[i], db[i]), (dr, drb) = _rowwise_bwd(_ln_fn, [(ln_g[i], "a", D), (ln_b[i], "a", D)], [(sv["r"], "a", D)], [(dh, "a", D)],
                                                 name=tag + "_dln", din_dtypes=[(f32, MXU_DT)])
        if kind == 0:
            dh, gr = _ssd_layer_bwd(sv, LW[i], dr, drb, tag, plan, i)
        elif kind == 1:
            dh, gr = _mla_layer_bwd(sv, LW[i], cos, sin, dr, drb, tag, plan, i)
        else:
            dh, gr = _gdn_layer_bwd(sv, LW[i], dr, drb, tag, plan, i)
        plan.layer_grads(i, gr)
    return loss, dh, dg, db


_WEIGHTS = ["ssd_in_w", "ssd_conv_w", "ssd_conv_b", "ssd_dt_bias", "ssd_a_log", "ssd_d", "ssd_norm_w", "ssd_out_w",
            "mla_in_w", "mla_q_norm_w", "mla_q_up_w", "mla_kv_norm_w", "mla_kv_up_w", "mla_out_w",
            "gdn_in_w", "gdn_conv_w", "gdn_a_log", "gdn_dt_bias", "gdn_norm_w", "gdn_out_w", "ln_g", "ln_b"]
_BIG = {"ssd_in_w": "col", "ssd_out_w": "row", "mla_in_w": "col", "mla_q_up_w": "col", "mla_kv_up_w": "col",
        "mla_out_w": "row", "gdn_in_w": "col", "gdn_out_w": "row"}
_SMALL_SHARDED = ["ssd_conv_w", "ssd_conv_b", "ssd_norm_w", "gdn_conv_w"]
_PACK_ROWS = 16


def _gathered_to_full(g, kind, nl):
    if kind == "col":
        _, RK, Ns = g.shape
        return g.reshape(4, nl, RK // nl, Ns).transpose(1, 2, 0, 3).reshape(nl, RK // nl, 4 * Ns)
    _, RK, N = g.shape
    return g.reshape(4, nl, RK // nl, N).transpose(1, 0, 2, 3).reshape(nl, 4 * (RK // nl), N)


def _full_to_slots(f, kind):
    nl, K, N = f.shape
    if kind == "col":
        return f.reshape(nl, K, 4, N // 4).transpose(2, 0, 1, 3).reshape(4, nl * K, N // 4)
    return f.reshape(nl, 4, K // 4, N).transpose(1, 0, 2, 3).reshape(4, nl * (K // 4), N)


def _pack(arrs):
    flat = jnp.concatenate([a.reshape(-1).astype(f32) for a in arrs])
    unit = _PACK_ROWS * LANES
    n = -(-flat.shape[0] // unit) * unit
    return jnp.pad(flat, (0, n - flat.shape[0])).reshape(_PACK_ROWS, n // _PACK_ROWS)


def _unpack(packed, shapes):
    flat = packed.reshape(-1)
    out, off = [], 0
    for sh in shapes:
        n = math.prod(sh)
        out.append(flat[off:off + n].reshape(sh))
        off += n
    return out


def _pad_lanes(a):
    return jnp.pad(a, [(0, 0)] * (a.ndim - 1) + [(0, LANES - a.shape[-1])])


_IN_PROJ = ("ssd_in_w", "mla_in_w", "gdn_in_w")


class _ColSlots:
    def __init__(self, slots):
        self.slots = slots
        self.shape = (slots.shape[1], 4 * slots.shape[2])

    def __getitem__(self, idx):
        _, cols = idx
        ns = self.slots.shape[2]
        a = cols.start or 0
        b = self.shape[1] if cols.stop is None else cols.stop
        parts = [self.slots[k][:, max(a, k * ns) - k * ns:min(b, (k + 1) * ns) - k * ns]
                 for k in range(4) if max(a, k * ns) < min(b, (k + 1) * ns)]
        return parts[0] if len(parts) == 1 else jnp.concatenate(parts, axis=1)


def _col_slots(pieces):
    widths = [p.shape[1] for p in pieces]
    ns = sum(widths) // 4
    slots = []
    for k in range(4):
        lo, hi, off, parts = k * ns, (k + 1) * ns, 0, []
        for p, wd in zip(pieces, widths):
            if max(lo, off) < min(hi, off + wd):
                parts.append(p[:, max(lo, off) - off:min(hi, off + wd) - off])
            off += wd
        slots.append(parts[0] if len(parts) == 1 else jnp.concatenate(parts, axis=1))
    return jnp.stack(slots)


def _layer_dict(i, full):
    G, N, P = SSD_N_GROUPS, SSD_D_STATE, SSD_HEAD_DIM
    kind, j = i % 3, i // 3
    if kind == 0:
        H = full["ssd_dt_bias"][j].shape[0]
        DI = H * P
        CD = DI + 2 * G * N
        win = full["ssd_in_w"][j]
        return dict(wz=win[:, :DI], wxbc=win[:, DI:DI + CD], wdt=win[:, DI + CD:], conv_w=full["ssd_conv_w"][j],
                    conv_b=full["ssd_conv_b"][j][None], dt_bias=full["ssd_dt_bias"][j][None], a_log=full["ssd_a_log"][j][None],
                    d=full["ssd_d"][j][None], norm_w=full["ssd_norm_w"][j][None], wout=full["ssd_out_w"][j])
    if kind == 1:
        QR, KR = MLA_Q_RANK, MLA_KV_RANK
        win = full["mla_in_w"][j]
        Hh = full["mla_q_up_w"][j].shape[1] // (MLA_NOPE + MLA_ROPE)
        qup = full["mla_q_up_w"][j].reshape(QR, Hh, MLA_NOPE + MLA_ROPE)
        kvup = full["mla_kv_up_w"][j].reshape(KR, Hh, MLA_NOPE + MLA_V)
        return dict(wqc=win[:, :QR], wkvc=win[:, QR:QR + KR], wkr=_pad_lanes(win[:, QR + KR:QR + KR + MLA_ROPE]),
                    wz=win[:, QR + KR + MLA_ROPE:], q_norm=full["mla_q_norm_w"][j][None], kv_norm=full["mla_kv_norm_w"][j][None],
                    wqn=qup[:, :, :MLA_NOPE].reshape(QR, Hh * MLA_NOPE), wqr=_pad_lanes(qup[:, :, MLA_NOPE:]).reshape(QR, Hh * LANES),
                    wkn=kvup[:, :, :MLA_NOPE].reshape(KR, Hh * MLA_NOPE), wv=kvup[:, :, MLA_NOPE:].reshape(KR, Hh * MLA_V),
                    wout=full["mla_out_w"][j])
    KD, VD, HV = GDN_N_QK_HEADS * GDN_DK, GDN_N_V_HEADS * GDN_DV, GDN_N_V_HEADS
    win = full["gdn_in_w"][j]
    c0, c1 = 2 * KD + VD, 2 * KD + 2 * VD
    return dict(wqkv=win[:, :c0], wz=win[:, c0:c1], wb=win[:, c1:c1 + HV], wa=win[:, c1 + HV:], conv_w=full["gdn_conv_w"][j],
                a_log=full["gdn_a_log"][j][None], dt_bias=full["gdn_dt_bias"][j][None], norm_w=full["gdn_norm_w"][j][None],
                wout=full["gdn_out_w"][j])


def _layer_weights(full, D):
    return [_layer_dict(i, full) for i in range(DEPTH)]


def _layer_full_grads(i, g, slots=False):
    kind = i % 3
    join = _col_slots if slots else (lambda pieces: jnp.concatenate(pieces, axis=1))
    if kind == 0:
        out = {"ssd_in_w": join([g["wz"], g["wxbc"], g["wdt"]]), "ssd_conv_w": g["conv_w"], "ssd_out_w": g["wout"]}
        for n in ("conv_b", "dt_bias", "a_log", "d", "norm_w"):
            out["ssd_" + n] = g[n][0]
        return out
    if kind == 1:
        QR, KR = g["wqn"].shape[0], g["wkn"].shape[0]
        Hh = g["wqn"].shape[1] // MLA_NOPE
        return {"mla_in_w": join([g["wqc"], g["wkvc"], g["wkr"][:, :MLA_ROPE], g["wz"]]),
                "mla_q_up_w": jnp.concatenate([g["wqn"].reshape(QR, Hh, MLA_NOPE), g["wqr"].reshape(QR, Hh, LANES)[:, :, :MLA_ROPE]],
                                              axis=2).reshape(QR, -1),
                "mla_kv_up_w": jnp.concatenate([g["wkn"].reshape(KR, Hh, MLA_NOPE), g["wv"].reshape(KR, Hh, MLA_V)], axis=2).reshape(KR, -1),
                "mla_q_norm_w": g["q_norm"][0], "mla_kv_norm_w": g["kv_norm"][0], "mla_out_w": g["wout"]}
    out = {"gdn_in_w": join([g["wqkv"], g["wz"], g["wb"], g["wa"]]), "gdn_conv_w": g["conv_w"], "gdn_out_w": g["wout"]}
    for n in ("a_log", "dt_bias", "norm_w"):
        out["gdn_" + n] = g[n][0]
    return out


def _full_grads(grads, dg, db):
    per = {n: [] for n in _WEIGHTS}
    for i in range(DEPTH):
        for n, a in _layer_full_grads(i, grads[i]).items():
            per[n].append(a)
        per["ln_g"].append(dg[i][0])
        per["ln_b"].append(db[i][0])
    return {n: jnp.stack(v) for n, v in per.items()}


class _DistPlan:
    def __init__(self, w, chip, core):
        self.w, self.chip = w, chip
        self.chip_arr = jnp.reshape(chip, (1,)).astype(jnp.int32)
        self.core_arr = jnp.reshape(core, (1,)).astype(jnp.int32)
        self.full = {n: {} for n in _BIG}
        self.g4, self.p4, self.fin, self.small_grads = {}, {}, {}, [None] * DEPTH
        got = self._gather(0, extra=[_pack([w[n] for n in _SMALL_SHARDED])], name="gather_l0")
        parts = [_unpack(got[k], [w[n].shape for n in _SMALL_SHARDED]) for k in range(4)]
        for t, n in enumerate(_SMALL_SHARDED):
            self.full[n] = jnp.concatenate([parts[k][t] for k in range(4)], axis=-1)
        for n in _WEIGHTS:
            if n not in self.full:
                self.full[n] = w[n]

    @staticmethod
    def names(i):
        return [["ssd_in_w", "ssd_out_w"], ["mla_in_w", "mla_q_up_w", "mla_kv_up_w", "mla_out_w"], ["gdn_in_w", "gdn_out_w"]][i % 3]

    def _shards(self, i):
        return [self.w[n][i // 3].astype(MXU_DT) for n in self.names(i)]

    def _fill(self, i, shards, got):
        for n, s, g in zip(self.names(i), shards, got):
            g = lax.dynamic_update_slice(g, s[None], (self.chip, 0, 0))
            self.full[n][i // 3] = _ColSlots(g) if n in _IN_PROJ else _gathered_to_full(g, _BIG[n], 1)[0]

    def _gather(self, i, extra, name):
        shards = self._shards(i)
        got = _run_comm(_GatherChips(shards + extra), name=name)
        self._fill(i, shards, got[:len(shards)])
        return [lax.dynamic_update_slice(g, s[None], (self.chip, 0, 0)) for g, s in zip(got[len(shards):], extra)][0]

    def weights(self, i):
        return _layer_dict(i, self.full)

    def fwd_comm(self, i):
        if i + 1 >= DEPTH:
            return None
        self._pending_shards = self._shards(i + 1)
        return _GatherChips(self._pending_shards)

    def fwd_done(self, i, res):
        if res is not None:
            self._fill(i + 1, self._pending_shards, res)

    def layer_grads(self, i, gr):
        fg = _layer_full_grads(i, gr, slots=True)
        names = self.names(i)
        self.small_grads[i] = {n: a for n, a in fg.items() if n not in _BIG}
        self.g4[i] = [fg[n] if n in _IN_PROJ else _full_to_slots(fg[n][None], _BIG[n]) for n in names]
        if i == 0:
            self._pair_add(0, _run_comm(_PairSend(self.g4[0]), name="grad_pair_send_l0"))
            self._sum(0, _run_comm(self._scatter(0), name="grad_chip_scatter_l0"))

    def _pair_add(self, i, r1):
        self.p4[i] = [_add_half(a, b, self.core_arr, name="grad_pair_add_l%d_%s" % (i, n))
                      for a, b, n in zip(self.g4[i], r1, self.names(i))]

    def bwd_early_comm(self, i):
        return _PairSend(self.g4[i + 1]) if i + 1 < DEPTH else None

    def bwd_early_done(self, i, res):
        if res is not None:
            self._pair_add(i + 1, res)

    def _scatter(self, i):
        return _ScatterChips([p[1] for p in self.p4[i]])

    def _sum(self, i, r2):
        self.fin[i] = [_sum_chips(p[0], b, self.chip_arr, name="grad_chip_sum_l%d_%s" % (i, n))
                       for p, b, n in zip(self.p4[i], r2, self.names(i))]

    def bwd_comm(self, i):
        return self._scatter(i + 1) if i + 1 < DEPTH else None

    def bwd_done(self, i, res):
        if res is not None:
            self._sum(i + 1, res)

    def grad_shards(self, core):
        order = [(i, t) for i in range(DEPTH) for t in range(len(self.names(i)))]
        fins = [self.fin[i][t] for i, t in order]
        got = _pair_exchange_halves(fins, name="grad_pair_share")
        got = [lax.dynamic_update_slice(g, f, (core * f.shape[0], 0)) for g, f in zip(got, fins)]
        per = {n: [] for n in _BIG}
        for (i, t), g in zip(order, got):
            per[self.names(i)[t]].append(g)
        return {n: (v[0] if len(v) == 1 else jnp.concatenate(v, axis=0)) for n, v in per.items()}


def kernel(x, positions, ssd_in_w, ssd_conv_w, ssd_conv_b, ssd_dt_bias, ssd_a_log, ssd_d, ssd_norm_w, ssd_out_w, mla_in_w, mla_q_norm_w, mla_q_up_w, mla_kv_norm_w, mla_kv_up_w, mla_out_w, gdn_in_w, gdn_conv_w, gdn_a_log, gdn_dt_bias, gdn_norm_w, gdn_out_w, ln_g, ln_b, loss_target, m_ssd_in_w, m_ssd_conv_w, m_ssd_conv_b, m_ssd_dt_bias, m_ssd_a_log, m_ssd_d, m_ssd_norm_w, m_ssd_out_w, m_mla_in_w, m_mla_q_norm_w, m_mla_q_up_w, m_mla_kv_norm_w, m_mla_kv_up_w, m_mla_out_w, m_gdn_in_w, m_gdn_conv_w, m_gdn_a_log, m_gdn_dt_bias, m_gdn_norm_w, m_gdn_out_w, m_ln_g, m_ln_b, v_ssd_in_w, v_ssd_conv_w, v_ssd_conv_b, v_ssd_dt_bias, v_ssd_a_log, v_ssd_d, v_ssd_norm_w, v_ssd_out_w, v_mla_in_w, v_mla_q_norm_w, v_mla_q_up_w, v_mla_kv_norm_w, v_mla_kv_up_w, v_mla_out_w, v_gdn_in_w, v_gdn_conv_w, v_gdn_a_log, v_gdn_dt_bias, v_gdn_norm_w, v_gdn_out_w, v_ln_g, v_ln_b):
    args = dict(locals())
    w = {n: args[n] for n in _WEIGHTS}
    mom = {n: args["m_" + n] for n in _WEIGHTS}
    vel = {n: args["v_" + n] for n in _WEIGHTS}
    mx, my, mc = _me()
    chip = 2 * mx + my
    small = [n for n in _WEIGHTS if n not in _BIG]
    big = list(_BIG)

    plan = _DistPlan(w, chip, mc)
    loss, gx, dg, db = _local_step(x[0], positions[0], loss_target[0], [plan.full["ln_g"][i][None] for i in range(DEPTH)],
                                   [plan.full["ln_b"][i][None] for i in range(DEPTH)], plan)
    per = {n: [] for n in small}
    for i in range(DEPTH):
        for n, a in plan.small_grads[i].items():
            per[n].append(a)
        per["ln_g"].append(dg[i][0])
        per["ln_b"].append(db[i][0])
    fg = {n: jnp.stack(v) for n, v in per.items()}
    gsh = plan.grad_shards(mc)

    out_g, out_d, out_m, out_v = {}, {}, {}, {}
    for n in big:
        g = gsh[n]
        sh = w[n].shape
        to2 = lambda a: a.reshape(-1, sh[-1])
        d_, m_, v_ = _adamw(to2(w[n]), g, to2(mom[n]), to2(vel[n]), name="adamw_" + n)
        out_g[n], out_d[n], out_m[n], out_v[n] = g.reshape(sh), d_.reshape(sh), m_.reshape(sh), v_.reshape(sh)

    summed = _sum8(_allgather_all(_pack([fg[n] for n in small] + [loss[0, :1]]), name="gather_small"), name="sum_small")
    sg = _unpack(summed, [fg[n].shape for n in small] + [(1,)])
    loss_total = sg[-1][0]
    gs = {}
    for n, g in zip(small, sg[:-1]):
        if n in _SMALL_SHARDED:
            ws = w[n].shape[-1]
            g = lax.dynamic_slice_in_dim(g, chip * ws, ws, axis=g.ndim - 1)
        gs[n] = g
    shapes = [w[n].shape for n in small]
    d_, m_, v_ = _adamw(_pack([w[n] for n in small]), _pack([gs[n] for n in small]), _pack([mom[n] for n in small]),
                        _pack([vel[n] for n in small]), name="adamw_small")
    for n, a, b, c in zip(small, _unpack(d_, shapes), _unpack(m_, shapes), _unpack(v_, shapes)):
        out_g[n], out_d[n], out_m[n], out_v[n] = gs[n], a, b, c

    return (loss_total, gx[None], *[out_g[n] for n in _WEIGHTS], *[out_d[n] for n in _WEIGHTS],
            *[out_m[n] for n in _WEIGHTS], *[out_v[n] for n in _WEIGHTS])
```

```python
import functools
import math

import jax
import jax.numpy as jnp
from jax import lax
from jax.experimental import pallas as pl
from jax.experimental.pallas import tpu as pltpu

f32 = jnp.float32
HI = lax.Precision.HIGHEST
MXU_DT = jnp.bfloat16
COMM_DT = jnp.bfloat16
MESH = pl.DeviceIdType.MESH

DEPTH = 4
LN_EPS = 1e-5
RMS_EPS = 1e-6
SSD_HEAD_DIM = 64
SSD_N_GROUPS = 8
SSD_D_STATE = 128
SSD_CONV = 4
SSD_CHUNK = 128
MLA_Q_RANK = 768
MLA_KV_RANK = 512
MLA_NOPE = 128
MLA_ROPE = 64
MLA_V = 128
ROPE_THETA = 10000.0
GDN_N_QK_HEADS = 16
GDN_N_V_HEADS = 32
GDN_DK = 128
GDN_DV = 128
GDN_CONV = 4
GDN_CHUNK = 64
ADAM_LR = 0.001
ADAM_B1 = 0.9
ADAM_B2 = 0.999
ADAM_EPS = 1e-08
ADAM_WD = 0.01
ADAM_STEP = 10

LANES = 128
VMEM_LIMIT = 48 * 1024 * 1024
ATT_TILE = 512
ATT_Q_TILE = 512
ATT_KEY_TILE = 512
ROW_TILE = 256
MM_TILE_M = 1024
MM_TILE_N = 1024
MM_TILE_K = 2048
MM_VMEM_BUDGET = 40 * 1024 * 1024
GDN_HEADS_PER_STEP = 16


def _alpha():
    return (2.0 * DEPTH) ** 0.25


def _tile(n, pref, align=LANES):
    t = min(pref, n) // align * align
    while t >= align:
        if n % t == 0:
            return t
        t -= align
    return n


def _cp(sem=None):
    return pltpu.CompilerParams(dimension_semantics=sem, vmem_limit_bytes=VMEM_LIMIT)


def _iota(shape, dim):
    return lax.broadcasted_iota(jnp.int32, shape, dim)


def _div_pow2(x, p):
    assert p & (p - 1) == 0
    return lax.shift_right_logical(x, jnp.int32(p.bit_length() - 1))


def _dot(a, b, dims=((1,), (0,)), hi=False):
    if hi:
        return lax.dot_general(a.astype(f32), b.astype(f32), (dims, ((), ())), precision=HI, preferred_element_type=f32)
    return lax.dot_general(a.astype(MXU_DT), b.astype(MXU_DT), (dims, ((), ())), preferred_element_type=f32)


_NT = ((1,), (1,))
_TN = ((0,), (0,))


def _split3(x):
    hi = x.astype(jnp.bfloat16)
    r = x - hi.astype(f32)
    mid = r.astype(jnp.bfloat16)
    return hi, mid, (r - mid.astype(f32)).astype(jnp.bfloat16)


def _seldot_impl(a, b, dims, exact):
    def d(x, y):
        return lax.dot_general(x, y, (dims, ((), ())), preferred_element_type=f32)

    if exact == 0:
        a01 = a.astype(jnp.bfloat16)
        t = _split3(b.astype(f32))
        return (d(a01, t[0]) + d(a01, t[1])) + d(a01, t[2])
    b01 = b.astype(jnp.bfloat16)
    t = _split3(a.astype(f32))
    return (d(t[0], b01) + d(t[1], b01)) + d(t[2], b01)


@functools.partial(jax.custom_vjp, nondiff_argnums=(2, 3))
def _seldot(a, b, dims, exact):
    return _seldot_impl(a, b, dims, exact)


def _seldot_fwd(a, b, dims, exact):
    return _seldot_impl(a, b, dims, exact), (a, b)


def _seldot_bwd(dims, exact, res, dy):
    a, b = res
    (ca,), (cb,) = dims
    if exact == 0:
        assert ca == 1
        db = _seldot_impl(a, dy, _TN, 0) if cb == 0 else _seldot_impl(dy, a, _TN, 1)
        return jnp.zeros_like(a), db
    assert ca == 1 and cb == 0
    return _seldot_impl(dy, b, _NT, 1), jnp.zeros_like(b)


_seldot.defvjp(_seldot_fwd, _seldot_bwd)


def _softplus(x):
    return jnp.maximum(x, 0.0) + jnp.log1p(jnp.exp(-jnp.abs(x)))


def _silu(x):
    return x * jax.nn.sigmoid(x)


def _mm(a, b, *, name, ta=False, tb=False, add=None, add_scale=1.0, out_dtype=f32):
    M, K = (a.shape[1], a.shape[0]) if ta else a.shape
    N = b.shape[0] if tb else b.shape[1]
    assert (b.shape[1] if tb else b.shape[0]) == K, (a.shape, b.shape, ta, tb)
    tm, tn, tk = _tile(M, MM_TILE_M), _tile(N, MM_TILE_N), _tile(K, MM_TILE_K)
    ab, bb = jnp.dtype(a.dtype).itemsize, jnp.dtype(b.dtype).itemsize
    while 2 * tk * (tm * ab + tn * bb) + 12 * tm * tn > MM_VMEM_BUDGET and tk % (2 * LANES) == 0:
        tk //= 2
    nk = K // tk
    a_spec = pl.BlockSpec((tk, tm), lambda i, j, k: (k, i)) if ta else pl.BlockSpec((tm, tk), lambda i, j, k: (i, k))
    b_spec = pl.BlockSpec((tn, tk), lambda i, j, k: (j, k)) if tb else pl.BlockSpec((tk, tn), lambda i, j, k: (k, j))
    o_spec = pl.BlockSpec((tm, tn), lambda i, j, k: (i, j))
    dims = ((0 if ta else 1,), (1 if tb else 0,))
    has_add = add is not None

    def body(*refs):
        a_ref, b_ref = refs[:2]
        add_ref = refs[2] if has_add else None
        o_ref = refs[3 if has_add else 2]

        def finish(r):
            if has_add:
                r = r + add_scale * add_ref[...].astype(f32)
            o_ref[...] = r.astype(out_dtype)

        if nk == 1:
            finish(_dot(a_ref[...], b_ref[...], dims))
            return
        acc = refs[-1]
        k = pl.program_id(2)

        @pl.when(k == 0)
        def _():
            acc[...] = jnp.zeros_like(acc)

        acc[...] += _dot(a_ref[...], b_ref[...], dims)

        @pl.when(k == nk - 1)
        def _():
            finish(acc[...])

    ins = [a, b] + ([add] if has_add else [])
    specs = [a_spec, b_spec] + ([o_spec] if has_add else [])
    return pl.pallas_call(
        body, name=name, grid=(M // tm, N // tn, nk), in_specs=specs, out_specs=o_spec,
        out_shape=jax.ShapeDtypeStruct((M, N), out_dtype), scratch_shapes=[pltpu.VMEM((tm, tn), f32)] if nk > 1 else [],
        compiler_params=_cp(("parallel", "parallel", "arbitrary")))(*ins)


def _rw_specs(params, ins, ncol, tm):
    specs = []
    for arr, mode, bw, coff in params:
        if mode == "c":
            specs.append(pl.BlockSpec((1, bw), lambda c, r, coff=coff: (0, c + coff)))
        else:
            specs.append(pl.BlockSpec((1, bw), lambda c, r, coff=coff: (0, coff)))
    for arr, mode, bw, coff in ins:
        if mode == "c":
            specs.append(pl.BlockSpec((tm, bw), lambda c, r, coff=coff: (r, c + coff)))
        else:
            specs.append(pl.BlockSpec((tm, bw), lambda c, r, coff=coff: (r, coff)))
    return specs


def _norm_spec(lst):
    out = []
    for t in lst:
        arr, mode, bw = t[0], t[1], t[2]
        coff = t[3] if len(t) > 3 else 0
        out.append((arr, mode, bw, coff))
    return out


def _rowwise(fn, params, ins, outs, *, name, ncol=1, tm=None):
    params, ins = _norm_spec(params), _norm_spec(ins)
    S = ins[0][0].shape[0]
    tm = _tile(S, tm or ROW_TILE, 8)
    npar, nin = len(params), len(ins)

    def body(*refs):
        pv = [r[...].astype(f32) for r in refs[:npar]]
        iv = [r[...].astype(f32) for r in refs[npar:npar + nin]]
        res = fn(*pv, *iv)
        for o_ref, val in zip(refs[npar + nin:], res):
            o_ref[...] = val.astype(o_ref.dtype)

    out_specs, out_shapes = [], []
    for W, dt, mode, bw in outs:
        out_shapes.append(jax.ShapeDtypeStruct((S, W), dt))
        if mode == "c":
            out_specs.append(pl.BlockSpec((tm, bw), lambda c, r: (r, c)))
        else:
            out_specs.append(pl.BlockSpec((tm, bw), lambda c, r: (r, 0)))
    return pl.pallas_call(
        body, name=name, grid=(ncol, S // tm), in_specs=_rw_specs(params, ins, ncol, tm), out_specs=out_specs,
        out_shape=out_shapes, compiler_params=_cp(("parallel", "parallel")))(*[p[0] for p in params], *[i[0] for i in ins])


def _rowwise_bwd(fn, params, ins, couts, *, name, ncol=1, tm=None, diff_p=None, diff_i=None, din_dtypes=None, comm=None):
    params, ins, couts = _norm_spec(params), _norm_spec(ins), _norm_spec(couts)
    S = ins[0][0].shape[0]
    tm = _tile(S, tm or ROW_TILE, 8)
    npar, nin, nco = len(params), len(ins), len(couts)
    diff_p = list(range(npar)) if diff_p is None else diff_p
    diff_i = list(range(nin)) if diff_i is None else diff_i
    din_dtypes = [(f32,)] * len(diff_i) if din_dtypes is None else din_dtypes

    def body(*refs):
        c, r = pl.program_id(0), pl.program_id(1)
        pv = [x[...].astype(f32) for x in refs[:npar]]
        iv = [x[...].astype(f32) for x in refs[npar:npar + nin]]
        cv = [x[...].astype(f32) for x in refs[npar + nin:npar + nin + nco]]
        orefs = refs[npar + nin + nco:]

        def g(*dargs):
            p2, i2 = list(pv), list(iv)
            for n, k in enumerate(diff_p):
                p2[k] = dargs[n]
            for n, k in enumerate(diff_i):
                i2[k] = dargs[len(diff_p) + n]
            return tuple(fn(*p2, *i2))

        _, vjp = jax.vjp(g, *[pv[k] for k in diff_p], *[iv[k] for k in diff_i])
        grads = vjp(tuple(cv))
        for n, k in enumerate(diff_p):
            o_ref = orefs[n]
            first = (r == 0) if params[k][1] == "c" else jnp.logical_and(r == 0, c == 0)

            @pl.when(first)
            def _(o_ref=o_ref):
                o_ref[...] = jnp.zeros_like(o_ref)

            o_ref[...] += grads[n]
        pos = len(diff_p)
        for n, k in enumerate(diff_i):
            for _ in din_dtypes[n]:
                orefs[pos][...] = grads[len(diff_p) + n].astype(orefs[pos].dtype)
                pos += 1

    out_specs, out_shapes = [], []
    for k in diff_p:
        arr, mode, bw, coff = params[k]
        W = bw * ncol if mode == "c" else bw
        out_shapes.append(jax.ShapeDtypeStruct((1, W), f32))
        out_specs.append(pl.BlockSpec((1, bw), (lambda c, r: (0, c)) if mode == "c" else (lambda c, r: (0, 0))))
    for n, k in enumerate(diff_i):
        arr, mode, bw, coff = ins[k]
        W = bw * ncol if mode == "c" else bw
        for dt in din_dtypes[n]:
            out_shapes.append(jax.ShapeDtypeStruct((S, W), dt))
            out_specs.append(pl.BlockSpec((tm, bw), (lambda c, r: (r, c)) if mode == "c" else (lambda c, r: (r, 0))))
    res, cres = _call(
        body, comm, name=name, grid=(ncol, S // tm), in_specs=_rw_specs(params, ins + couts, ncol, tm), out_specs=out_specs,
        out_shape=out_shapes, scratch_shapes=[], sem=("arbitrary", "arbitrary"),
        args=(*[p[0] for p in params], *[i[0] for i in ins], *[c[0] for c in couts]))
    if comm is None:
        return list(res[:len(diff_p)]), list(res[len(diff_p):])
    return list(res[:len(diff_p)]), list(res[len(diff_p):]), cres


def _ln_fn(g, b, r):
    mu = jnp.mean(r, -1, keepdims=True)
    xc = r - mu
    var = jnp.mean(xc * xc, -1, keepdims=True)
    return (xc * lax.rsqrt(var + LN_EPS) * g + b,)


def _res_ln_fn(g, b, h, y):
    r = _alpha() * h + y
    hn = _ln_fn(g, b, r)
    return (r,) + hn + hn


def _rms_fn(w, x):
    return (x * lax.rsqrt(jnp.mean(x * x, -1, keepdims=True) + RMS_EPS) * w,)


def _ssd_gate_fn(w, y, z):
    yg = y * _silu(z)
    return (yg * lax.rsqrt(jnp.mean(yg * yg, -1, keepdims=True) + RMS_EPS) * w,)


def _mul_silu_fn(o, z):
    return (o * _silu(z),)


def _gdn_gate_fn(w, o, z):
    return (o * lax.rsqrt(jnp.mean(o * o, -1, keepdims=True) + RMS_EPS) * w * _silu(z),)


def _l2_fn(scale, x):
    return (x * lax.rsqrt(jnp.sum(x * x, -1, keepdims=True) + RMS_EPS) * scale,)


def _rope_fn(cos, sin, x):
    half = MLA_ROPE // 2
    i = _iota((LANES, LANES), 0)
    j = _iota((LANES, LANES), 1)
    pm = jnp.where((i == j + half) & (j < half), -1.0, 0.0) + jnp.where((i + half == j) & (j < 2 * half), 1.0, 0.0)
    return (x * cos + _seldot(x, pm.astype(f32), ((1,), (0,)), 1) * sin,)


def _conv_taps(x, K):
    S = x.shape[0]
    rows = _iota(x.shape, 0)
    return [x] + [jnp.where(rows < j, 0.0, pltpu.roll(x, j, 0)) for j in range(1, K)]


def _conv_fwd(x, w, b, *, name):
    S, C = x.shape
    K = w.shape[0]
    cw = _tile(C, LANES)

    def body(x_ref, w_ref, b_ref, o_ref):
        taps = _conv_taps(x_ref[...], K)
        wv = w_ref[...]
        pre = b_ref[...] + taps[0] * wv[K - 1:K, :]
        for j in range(1, K):
            pre = pre + taps[j] * wv[K - 1 - j:K - j, :]
        o_ref[...] = _silu(pre)

    return pl.pallas_call(
        body, name=name, grid=(C // cw,),
        in_specs=[pl.BlockSpec((S, cw), lambda c: (0, c)), pl.BlockSpec((K, cw), lambda c: (0, c)), pl.BlockSpec((1, cw), lambda c: (0, c))],
        out_specs=pl.BlockSpec((S, cw), lambda c: (0, c)), out_shape=jax.ShapeDtypeStruct((S, C), f32),
        compiler_params=_cp(("parallel",)))(x, w, b)


def _conv_bwd(x, w, b, dys, *, name, dx_dtype=f32):
    S, C = x.shape
    K = w.shape[0]
    cw = _tile(C, LANES)
    nblk = [d.shape[1] // cw for d in dys]
    offs = [sum(nblk[:p]) for p in range(len(dys))]
    assert sum(nblk) == C // cw and all(d.shape[1] % cw == 0 for d in dys)
    npc = len(dys)

    def body(x_ref, w_ref, b_ref, *refs):
        dy_refs, (dx_ref, dw_ref, db_ref) = refs[:npc], refs[npc:]
        c = pl.program_id(0)
        for p in range(npc):
            @pl.when(jnp.logical_and(c >= offs[p], c < offs[p] + nblk[p]))
            def _(p=p):
                _conv_bwd_block(x_ref, w_ref, b_ref, dy_refs[p], dx_ref, dw_ref, db_ref, K, S, dx_dtype)

    col = lambda c: (0, c)
    dy_specs = [pl.BlockSpec((S, cw), lambda c, o=offs[p], n=nblk[p]: (0, jnp.clip(c - o, 0, n - 1))) for p in range(npc)]
    return pl.pallas_call(
        body, name=name, grid=(C // cw,),
        in_specs=[pl.BlockSpec((S, cw), col), pl.BlockSpec((K, cw), col), pl.BlockSpec((1, cw), col)] + dy_specs,
        out_specs=[pl.BlockSpec((S, cw), col), pl.BlockSpec((K, cw), col), pl.BlockSpec((1, cw), col)],
        out_shape=[jax.ShapeDtypeStruct((S, C), dx_dtype), jax.ShapeDtypeStruct((K, C), f32), jax.ShapeDtypeStruct((1, C), f32)],
        compiler_params=_cp(("parallel",)))(x, w, b, *dys)


def _conv_bwd_block(x_ref, w_ref, b_ref, dy_ref, dx_ref, dw_ref, db_ref, K, S, dx_dtype):
    taps = _conv_taps(x_ref[...], K)
    wv = w_ref[...]
    pre = b_ref[...] + taps[0] * wv[K - 1:K, :]
    for j in range(1, K):
        pre = pre + taps[j] * wv[K - 1 - j:K - j, :]
    sg = jax.nn.sigmoid(pre)
    dpre = dy_ref[...] * sg * (1.0 + pre * (1.0 - sg))
    db_ref[...] = jnp.sum(dpre, axis=0, keepdims=True)
    rows = _iota(dpre.shape, 0)
    dx = dpre * wv[K - 1:K, :]
    dw_ref[K - 1:K, :] = jnp.sum(dpre * taps[0], axis=0, keepdims=True)
    for j in range(1, K):
        dw_ref[K - 1 - j:K - j, :] = jnp.sum(dpre * taps[j], axis=0, keepdims=True)
        up = jnp.where(rows >= S - j, 0.0, pltpu.roll(dpre, S - j, 0))
        dx = dx + up * wv[K - 1 - j:K - j, :]
    dx_ref[...] = dx.astype(dx_dtype)


def _ssd_chunk(prev, xs, Bm, Cm, dtr, dtb, alog, dsk, g, *, R, P):
    L, GW = xs.shape
    H = dtr.shape[1]
    tril = _iota((L, L), 0) >= _iota((L, L), 1)
    dt = _softplus(dtr + dtb)
    acs = _seldot(tril.astype(f32), dt * (-jnp.exp(alog)), ((1,), (0,)), 0)
    expand = (_iota((H, GW), 0) == g * R + _div_pow2(_iota((H, GW), 1), P)).astype(f32)
    dt_e = _seldot(dt, expand, ((1,), (0,)), 1)
    acs_e = _seldot(acs, expand, ((1,), (0,)), 1)
    d_e = jnp.sum(_seldot(jnp.broadcast_to(dsk, (8, H)), expand, ((1,), (0,)), 1), axis=0, keepdims=True) * 0.125
    last = jnp.sum(jnp.where(_iota((L, GW), 0) == L - 1, acs_e, 0.0), axis=0, keepdims=True)
    xdt = xs * dt_e
    cb = _dot(Cm, Bm, _NT)
    nsel = max(R, 8)
    sel = (_iota((nsel, H), 1) == g * R + _iota((nsel, H), 0)).astype(f32)
    acs_t = _seldot(sel, acs, _NT, 0)
    hp = LANES // P
    pieces = []
    for p in range(GW // LANES):
        xp = xdt[:, p * LANES:(p + 1) * LANES]
        acc = None
        for q in range(hp):
            r = p * hp + q
            col = jnp.sum(jnp.where(_iota((L, H), 1) == g * R + r, acs, 0.0), axis=1, keepdims=True)
            row = jnp.sum(jnp.where(_iota((nsel, L), 0) == r, acs_t, 0.0), axis=0, keepdims=True)
            dec = jnp.where(tril, jnp.exp(jnp.where(tril, col - row, 0.0)), 0.0)
            xm = jnp.where(_div_pow2(_iota((L, LANES), 1), P) == q, xp, 0.0)
            t = _dot(cb * dec, xm)
            acc = t if acc is None else acc + t
        pieces.append(acc)
    y_diag = pieces[0] if len(pieces) == 1 else jnp.concatenate(pieces, axis=1)
    st = _dot(Bm, xdt * jnp.exp(last - acs_e), _TN)
    y_off = _dot(Cm, prev) * jnp.exp(acs_e)
    new = prev * jnp.exp(last) + st
    return y_diag + y_off + xs * d_e, new


def _ssd_dims(xbc, dtr):
    S, CD = xbc.shape
    H = dtr.shape[1]
    G, N, P = SSD_N_GROUPS, SSD_D_STATE, SSD_HEAD_DIM
    DI = H * P
    R = H // G
    assert CD == DI + 2 * G * N and DI % N == 0
    return S, H, G, N, P, DI, R, R * P, SSD_CHUNK


def _ssd_scan_fwd(xbc, dtr, dtb, alog, dsk, *, name, comm=None):
    S, H, G, N, P, DI, R, GW, L = _ssd_dims(xbc, dtr)
    nc = S // L
    boff, coff = DI // N, DI // N + G

    def body(xs_ref, b_ref, c_ref, dtr_ref, dtb_ref, alog_ref, dsk_ref, y_ref, st_ref, state):
        c, g = pl.program_id(0), pl.program_id(1)

        @pl.when(c == 0)
        def _():
            state[g] = jnp.zeros((N, GW), f32)

        prev = state[g]
        st_ref[0, 0] = prev
        y, new = _ssd_chunk(prev, xs_ref[...], b_ref[...], c_ref[...], dtr_ref[...], dtb_ref[...], alog_ref[...],
                            dsk_ref[...], g, R=R, P=P)
        y_ref[...] = y
        state[g] = new

    par = pl.BlockSpec((1, H), lambda c, g: (0, 0))
    return _call(
        body, comm, name=name, grid=(nc, G),
        in_specs=[pl.BlockSpec((L, GW), lambda c, g: (c, g)), pl.BlockSpec((L, N), lambda c, g: (c, boff + g)),
                  pl.BlockSpec((L, N), lambda c, g: (c, coff + g)), pl.BlockSpec((L, H), lambda c, g: (c, 0)), par, par, par],
        out_specs=[pl.BlockSpec((L, GW), lambda c, g: (c, g)), pl.BlockSpec((1, 1, N, GW), lambda c, g: (c, g, 0, 0))],
        out_shape=[jax.ShapeDtypeStruct((S, DI), f32), jax.ShapeDtypeStruct((nc, G, N, GW), f32)],
        scratch_shapes=[pltpu.VMEM((G, N, GW), f32)],
        sem=("arbitrary", "arbitrary"), args=(xbc, xbc, xbc, dtr, dtb, alog, dsk))


def _ssd_scan_bwd(xbc, dtr, dtb, alog, dsk, states, dy, *, name, comm=None):
    S, H, G, N, P, DI, R, GW, L = _ssd_dims(xbc, dtr)
    nc = S // L
    boff, coff = DI // N, DI // N + G

    def body(xs_ref, b_ref, c_ref, dtr_ref, dtb_ref, alog_ref, dsk_ref, st_ref, dy_ref,
             dxs_ref, db_ref, dc_ref, ddtr_ref, ddtb_ref, dalog_ref, ddsk_ref, dstate):
        c, g = pl.program_id(0), pl.program_id(1)

        @pl.when(c == 0)
        def _():
            dstate[g] = jnp.zeros((N, GW), f32)

        @pl.when(jnp.logical_and(c == 0, g == 0))
        def _():
            ddtb_ref[...] = jnp.zeros_like(ddtb_ref)
            dalog_ref[...] = jnp.zeros_like(dalog_ref)
            ddsk_ref[...] = jnp.zeros_like(ddsk_ref)

        @pl.when(g == 0)
        def _():
            ddtr_ref[...] = jnp.zeros_like(ddtr_ref)

        fn = functools.partial(_ssd_chunk, g=g, R=R, P=P)
        _, vjp = jax.vjp(fn, st_ref[0, 0], xs_ref[...], b_ref[...], c_ref[...], dtr_ref[...], dtb_ref[...],
                         alog_ref[...], dsk_ref[...])
        dprev, dxs, dB, dC, ddtr, ddtb, dalog, ddsk = vjp((dy_ref[...], dstate[g]))
        dstate[g] = dprev
        dxs_ref[...] = dxs
        db_ref[...] = dB
        dc_ref[...] = dC
        ddtr_ref[...] += ddtr
        ddtb_ref[...] += ddtb
        dalog_ref[...] += dalog
        ddsk_ref[...] += ddsk

    rc = lambda c: nc - 1 - c
    par = pl.BlockSpec((1, H), lambda c, g: (0, 0))
    return _call(
        body, comm, name=name, grid=(nc, G),
        in_specs=[pl.BlockSpec((L, GW), lambda c, g: (rc(c), g)), pl.BlockSpec((L, N), lambda c, g: (rc(c), boff + g)),
                  pl.BlockSpec((L, N), lambda c, g: (rc(c), coff + g)), pl.BlockSpec((L, H), lambda c, g: (rc(c), 0)),
                  par, par, par, pl.BlockSpec((1, 1, N, GW), lambda c, g: (rc(c), g, 0, 0)),
                  pl.BlockSpec((L, GW), lambda c, g: (rc(c), g))],
        out_specs=[pl.BlockSpec((L, GW), lambda c, g: (rc(c), g)), pl.BlockSpec((L, N), lambda c, g: (rc(c), g)),
                   pl.BlockSpec((L, N), lambda c, g: (rc(c), g)), pl.BlockSpec((L, H), lambda c, g: (rc(c), 0)), par, par, par],
        out_shape=[jax.ShapeDtypeStruct((S, DI), f32), jax.ShapeDtypeStruct((S, G * N), f32), jax.ShapeDtypeStruct((S, G * N), f32),
                   jax.ShapeDtypeStruct((S, H), f32)] + [jax.ShapeDtypeStruct((1, H), f32)] * 3,
        scratch_shapes=[pltpu.VMEM((G, N, GW), f32)],
        sem=("arbitrary", "arbitrary"), args=(xbc, xbc, xbc, dtr, dtb, alog, dsk, states, dy))


def _dot3(a, b, dims=((1,), (0,))):
    def split(x):
        hi = x.astype(jnp.bfloat16)
        return hi, (x - hi.astype(f32)).astype(jnp.bfloat16)

    def d(x, y):
        return lax.dot_general(x, y, (dims, ((), ())), preferred_element_type=f32)

    ah, al = split(a)
    bh, bl = split(b)
    return d(ah, bh) + (d(ah, bl) + d(al, bh))


def _neumann_inverses(As):
    L = As[0].shape[0]
    eye = (_iota((L, L), 0) == _iota((L, L), 1)).astype(f32)
    X = [-A for A in As]
    P = [eye + x for x in X]
    n = 1
    while 2 * n < L:
        X = [_dot3(x, x) for x in X]
        P = [p + _dot3(p, x) for p, x in zip(P, X)]
        n *= 2
    return P


@jax.custom_vjp
def _unit_lower_solves(Ts, As, Rs):
    return tuple(_dot3(T, R) for T, R in zip(Ts, Rs))


def _uls_fwd(Ts, As, Rs):
    Xs = tuple(_dot3(T, R) for T, R in zip(Ts, Rs))
    return Xs, (Ts, Xs)


def _uls_bwd(res, dXs):
    Ts, Xs = res
    dRs = tuple(_dot3(T, dX, _TN) for T, dX in zip(Ts, dXs))
    dAs = tuple(-_dot3(dR, X, _NT) for dR, X in zip(dRs, Xs))
    return tuple(jnp.zeros_like(T) for T in Ts), dAs, dRs


_unit_lower_solves.defvjp(_uls_fwd, _uls_bwd)


def _gdn_step(states, qb, kb_, vb, br, ar, alog, dtb, h0, *, rep, inverses=None):
    HB = len(states)
    L = qb.shape[0]
    DK, DV = states[0].shape
    HV = br.shape[1]
    incl = _iota((L, L), 0) >= _iota((L, L), 1)
    strict = _iota((L, L), 0) > _iota((L, L), 1)
    lane = _iota((L, HV), 1)
    g_all = -jnp.exp(alog) * _softplus(ar + dtb)
    gcs = _seldot(incl.astype(f32), g_all, ((1,), (0,)), 0)
    beta_all = jax.nn.sigmoid(br)
    nsel = max(HV, 8)
    gcs_t = _seldot((_iota((nsel, HV), 0) == _iota((nsel, HV), 1)).astype(f32), gcs, _NT, 0)
    hs = range(HB)
    q = [qb[:, (hh // rep) * DK:(hh // rep + 1) * DK] for hh in hs]
    k = [kb_[:, (hh // rep) * DK:(hh // rep + 1) * DK] for hh in hs]
    v = [vb[:, hh * DV:(hh + 1) * DV] for hh in hs]
    gc = [jnp.sum(jnp.where(lane == h0 + hh, gcs, 0.0), axis=1, keepdims=True) for hh in hs]
    beta = [jnp.sum(jnp.where(lane == h0 + hh, beta_all, 0.0), axis=1, keepdims=True) for hh in hs]
    gc_row = [jnp.sum(jnp.where(_iota((nsel, L), 0) == h0 + hh, gcs_t, 0.0), axis=0, keepdims=True) for hh in hs]
    decay = [jnp.where(incl, jnp.exp(jnp.where(incl, gc[hh] - gc_row[hh], 0.0)), 0.0) for hh in hs]
    kbeta = [k[hh] * beta[hh] for hh in hs]
    a_mat = [jnp.where(strict, _dot(kbeta[hh], k[hh], _NT) * decay[hh], 0.0) for hh in hs]
    eg = [jnp.exp(gc[hh]) for hh in hs]
    rhs = tuple(jnp.concatenate([v[hh] * beta[hh], kbeta[hh] * eg[hh]], axis=1) for hh in hs)
    if inverses is None:
        made = tuple(_neumann_inverses(a_mat))
        sol = tuple(_dot3(T, R) for T, R in zip(made, rhs))
    else:
        sol = _unit_lower_solves(tuple(inverses), tuple(a_mat), rhs)
    qk = [jnp.where(incl, _dot(q[hh], k[hh], _NT) * decay[hh], 0.0) for hh in hs]
    g_last = [jnp.sum(jnp.where(_iota((L, 1), 0) == L - 1, gc[hh], 0.0), axis=0, keepdims=True) for hh in hs]
    v_new = [sol[hh][:, :DV] - _dot(sol[hh][:, DV:], states[hh]) for hh in hs]
    outs = [_dot(q[hh] * eg[hh], states[hh]) + _dot(qk[hh], v_new[hh]) for hh in hs]
    news = [states[hh] * jnp.exp(g_last[hh]) + _dot(k[hh] * jnp.exp(g_last[hh] - gc[hh]), v_new[hh], _TN) for hh in hs]
    o = outs[0] if HB == 1 else jnp.concatenate(outs, axis=1)
    return (o, tuple(news), made) if inverses is None else (o, tuple(news))


def _gdn_dims():
    HK, HV = GDN_N_QK_HEADS, GDN_N_V_HEADS
    rep = HV // HK
    HB = min(GDN_HEADS_PER_STEP, HV)
    assert HV % HB == 0 and HB % rep == 0
    return HK, HV, GDN_DK, GDN_DV, GDN_CHUNK, rep, HB


def _gdn_scan_fwd(qkn, qkv, br, ar, alog, dtb, *, name, comm=None):
    S = qkn.shape[0]
    HK, HV, DK, DV, L, rep, HB = _gdn_dims()
    nc = S // L
    QW = HB // rep * DK
    koff = HK * DK // QW
    voff = 2 * HK * DK // (HB * DV)

    def body(q_ref, k_ref, v_ref, br_ref, ar_ref, alog_ref, dtb_ref, o_ref, st_ref, inv_ref, state):
        c, hb = pl.program_id(0), pl.program_id(1)
        h0 = hb * HB

        @pl.when(c == 0)
        def _():
            for hh in range(HB):
                state[h0 + hh] = jnp.zeros((DK, DV), f32)

        prev = tuple(state[h0 + hh] for hh in range(HB))
        for hh in range(HB):
            st_ref[0, hh] = prev[hh]
        o, new, inv = _gdn_step(prev, q_ref[...], k_ref[...], v_ref[...], br_ref[...], ar_ref[...], alog_ref[...], dtb_ref[...],
                                h0, rep=rep)
        o_ref[...] = o
        for hh in range(HB):
            state[h0 + hh] = new[hh]
            inv_ref[0, hh] = inv[hh]

    par = pl.BlockSpec((1, HV), lambda c, h: (0, 0))
    return _call(
        body, comm, name=name, grid=(nc, HV // HB),
        in_specs=[pl.BlockSpec((L, QW), lambda c, h: (c, h)), pl.BlockSpec((L, QW), lambda c, h: (c, koff + h)),
                  pl.BlockSpec((L, HB * DV), lambda c, h: (c, voff + h)), pl.BlockSpec((L, HV), lambda c, h: (c, 0)),
                  pl.BlockSpec((L, HV), lambda c, h: (c, 0)), par, par],
        out_specs=[pl.BlockSpec((L, HB * DV), lambda c, h: (c, h)), pl.BlockSpec((1, HB, DK, DV), lambda c, h: (c, h, 0, 0)),
                   pl.BlockSpec((1, HB, L, L), lambda c, h: (c, h, 0, 0))],
        out_shape=[jax.ShapeDtypeStruct((S, HV * DV), f32), jax.ShapeDtypeStruct((nc, HV, DK, DV), f32),
                   jax.ShapeDtypeStruct((nc, HV, L, L), f32)],
        scratch_shapes=[pltpu.VMEM((HV, DK, DV), f32)],
        sem=("arbitrary", "arbitrary"), args=(qkn, qkn, qkv, br, ar, alog, dtb))


def _gdn_scan_bwd(qkn, qkv, br, ar, alog, dtb, states, inverses, do, *, name, comm=None):
    S = qkn.shape[0]
    HK, HV, DK, DV, L, rep, HB = _gdn_dims()
    nc = S // L
    QW = HB // rep * DK
    koff = HK * DK // QW
    voff = 2 * HK * DK // (HB * DV)

    def body(q_ref, k_ref, v_ref, br_ref, ar_ref, alog_ref, dtb_ref, st_ref, inv_ref, do_ref,
             dq_ref, dk_ref, dv_ref, dbr_ref, dar_ref, dalog_ref, ddtb_ref, dstate):
        c, hb = pl.program_id(0), pl.program_id(1)
        h0 = hb * HB

        @pl.when(c == 0)
        def _():
            for hh in range(HB):
                dstate[h0 + hh] = jnp.zeros((DK, DV), f32)

        @pl.when(jnp.logical_and(c == 0, hb == 0))
        def _():
            dalog_ref[...] = jnp.zeros_like(dalog_ref)
            ddtb_ref[...] = jnp.zeros_like(ddtb_ref)

        @pl.when(hb == 0)
        def _():
            dbr_ref[...] = jnp.zeros_like(dbr_ref)
            dar_ref[...] = jnp.zeros_like(dar_ref)

        fn = functools.partial(_gdn_step, h0=h0, rep=rep, inverses=tuple(inv_ref[0, hh] for hh in range(HB)))
        prev = tuple(st_ref[0, hh] for hh in range(HB))
        _, vjp = jax.vjp(fn, prev, q_ref[...], k_ref[...], v_ref[...], br_ref[...], ar_ref[...], alog_ref[...], dtb_ref[...])
        dprev, dq, dk, dv, dbr, dar, dalog, ddtb = vjp((do_ref[...], tuple(dstate[h0 + hh] for hh in range(HB))))
        for hh in range(HB):
            dstate[h0 + hh] = dprev[hh]
        dq_ref[...] = dq
        dk_ref[...] = dk
        dv_ref[...] = dv
        dbr_ref[...] += dbr
        dar_ref[...] += dar
        dalog_ref[...] += dalog
        ddtb_ref[...] += ddtb

    rc = lambda c: nc - 1 - c
    par = pl.BlockSpec((1, HV), lambda c, h: (0, 0))
    blk = lambda W: pl.BlockSpec((L, W), lambda c, h: (rc(c), h))
    return _call(
        body, comm, name=name, grid=(nc, HV // HB),
        in_specs=[pl.BlockSpec((L, QW), lambda c, h: (rc(c), h)), pl.BlockSpec((L, QW), lambda c, h: (rc(c), koff + h)),
                  pl.BlockSpec((L, HB * DV), lambda c, h: (rc(c), voff + h)), pl.BlockSpec((L, HV), lambda c, h: (rc(c), 0)),
                  pl.BlockSpec((L, HV), lambda c, h: (rc(c), 0)), par, par,
                  pl.BlockSpec((1, HB, DK, DV), lambda c, h: (rc(c), h, 0, 0)),
                  pl.BlockSpec((1, HB, L, L), lambda c, h: (rc(c), h, 0, 0)), blk(HB * DV)],
        out_specs=[blk(QW), blk(QW), blk(HB * DV), pl.BlockSpec((L, HV), lambda c, h: (rc(c), 0)),
                   pl.BlockSpec((L, HV), lambda c, h: (rc(c), 0)), par, par],
        out_shape=[jax.ShapeDtypeStruct((S, HK * DK), f32), jax.ShapeDtypeStruct((S, HK * DK), f32), jax.ShapeDtypeStruct((S, HV * DV), f32),
                   jax.ShapeDtypeStruct((S, HV), f32), jax.ShapeDtypeStruct((S, HV), f32),
                   jax.ShapeDtypeStruct((1, HV), f32), jax.ShapeDtypeStruct((1, HV), f32)],
        scratch_shapes=[pltpu.VMEM((HV, DK, DV), f32)],
        sem=("arbitrary", "arbitrary"), args=(qkn, qkn, qkv, br, ar, alog, dtb, states, inverses, do))


def _att_scale():
    return (MLA_NOPE + MLA_ROPE) ** -0.5


def _causal(s, i, j, t, tk=None):
    qpos = i * t + _iota(s.shape, 0)
    kpos = j * (t if tk is None else tk) + _iota(s.shape, 1)
    return kpos <= qpos


def _attn_fwd(qn, qr, kn, kr, v, *, name, comm=None):
    S, W = qn.shape
    H = W // LANES
    t = _tile(S, ATT_Q_TILE)
    tk = _tile(S, ATT_KEY_TILE)
    assert tk % t == 0
    scale = _att_scale()

    def body(qn_ref, qr_ref, kn_ref, kr_ref, v_ref, o_ref, lse_ref):
        i = pl.program_id(1)
        qc = jnp.concatenate([qn_ref[...], qr_ref[...]], axis=1)

        def step(j, carry, masked):
            m, l, acc = carry
            rows = pl.ds(pl.multiple_of(j * tk, tk), tk)
            s = _dot(qc, jnp.concatenate([kn_ref[rows, :], kr_ref[rows, :]], axis=1), _NT) * scale
            if masked:
                s = jnp.where(_causal(s, i, j, t, tk), s, -1e30)
            m_new = jnp.maximum(m, jnp.max(s, axis=1, keepdims=True))
            p = jnp.exp(s - m_new)
            a = jnp.exp(m - m_new)
            return m_new, a * l + jnp.sum(p, axis=1, keepdims=True), a * acc + _dot(p, v_ref[rows, :])

        nfull = lax.div(i * t, tk)
        carry = lax.fori_loop(0, nfull, functools.partial(step, masked=False),
                              (jnp.full((t, 1), -1e30, f32), jnp.zeros((t, 1), f32), jnp.zeros((t, LANES), f32)))
        m, l, acc = step(nfull, carry, True)
        o_ref[...] = acc / l
        lse_ref[...] = jnp.broadcast_to(m + jnp.log(l), (t, LANES))

    qb = pl.BlockSpec((t, LANES), lambda h, i: (i, h))
    kb = pl.BlockSpec((S, LANES), lambda h, i: (0, h))
    return _call(
        body, comm, name=name, grid=(H, S // t),
        in_specs=[qb, qb, kb, pl.BlockSpec((S, LANES), lambda h, i: (0, 0)), kb],
        out_specs=[qb, qb], out_shape=[jax.ShapeDtypeStruct((S, W), f32), jax.ShapeDtypeStruct((S, W), f32)],
        scratch_shapes=[], sem=("parallel", "arbitrary"), args=(qn, qr, kn, kr, v))


def _attn_bwd_dq(qn, qr, kn, kr, v, o, lse, do, *, name, comm=None):
    S, W = qn.shape
    H = W // LANES
    t = _tile(S, ATT_TILE)
    scale = _att_scale()

    def body(qn_ref, qr_ref, kn_ref, kr_ref, v_ref, o_ref, lse_ref, do_ref, dqn_ref, dqr_ref):
        i = pl.program_id(1)
        qc = jnp.concatenate([qn_ref[...], qr_ref[...]], axis=1)
        dov = do_ref[...]
        delta = jnp.sum(dov * o_ref[...], axis=1, keepdims=True)
        lsev = lse_ref[...][:, :1]

        def step(j, dq, masked):
            rows = pl.ds(pl.multiple_of(j * t, t), t)
            kc = jnp.concatenate([kn_ref[rows, :], kr_ref[rows, :]], axis=1)
            s = _dot(qc, kc, _NT) * scale
            p = jnp.exp(s - lsev)
            if masked:
                p = jnp.where(_causal(s, i, j, t), p, 0.0)
            ds = p * (_dot(dov, v_ref[rows, :], _NT) - delta)
            return dq + _dot(ds, kc)

        dq = lax.fori_loop(0, i, functools.partial(step, masked=False), jnp.zeros((t, 2 * LANES), f32))
        dq = step(i, dq, True)
        dq = dq * scale
        dqn_ref[...] = dq[:, :LANES].astype(MXU_DT)
        dqr_ref[...] = dq[:, LANES:]

    qb = pl.BlockSpec((t, LANES), lambda h, i: (i, h))
    kb = pl.BlockSpec((S, LANES), lambda h, i: (0, h))
    return _call(
        body, comm, name=name, grid=(H, S // t),
        in_specs=[qb, qb, kb, pl.BlockSpec((S, LANES), lambda h, i: (0, 0)), kb, qb, qb, qb],
        out_specs=[qb, qb], out_shape=[jax.ShapeDtypeStruct((S, W), MXU_DT), jax.ShapeDtypeStruct((S, W), f32)],
        scratch_shapes=[], sem=("parallel", "arbitrary"), args=(qn, qr, kn, kr, v, o, lse, do))


def _attn_bwd_dkv(qn, qr, kn, kr, v, o, lse, do, *, name, comm=None):
    S, W = qn.shape
    H = W // LANES
    t = _tile(S, ATT_TILE)
    nb = S // t
    scale = _att_scale()

    def body(qn_ref, qr_ref, kn_ref, kr_ref, v_ref, o_ref, lse_ref, do_ref, dkn_ref, dkr_ref, dv_ref):
        j, h = pl.program_id(0), pl.program_id(1)
        kc = jnp.concatenate([kn_ref[...], kr_ref[...]], axis=1)
        vv = v_ref[...]

        def step(i, carry, masked):
            dk, dv = carry
            rows = pl.ds(pl.multiple_of(i * t, t), t)
            qc = jnp.concatenate([qn_ref[rows, :], qr_ref[rows, :]], axis=1)
            dov = do_ref[rows, :]
            delta = jnp.sum(dov * o_ref[rows, :], axis=1, keepdims=True)
            s = _dot(qc, kc, _NT) * scale
            p = jnp.exp(s - lse_ref[rows, :][:, :1])
            if masked:
                p = jnp.where(_causal(s, i, j, t), p, 0.0)
            ds = p * (_dot(dov, vv, _NT) - delta)
            return dk + _dot(ds, qc, _TN), dv + _dot(p, dov, _TN)

        carry = step(j, (jnp.zeros((t, 2 * LANES), f32), jnp.zeros((t, LANES), f32)), True)
        dk, dv = lax.fori_loop(j + 1, nb, functools.partial(step, masked=False), carry)
        dk = dk * scale
        dkn_ref[...] = dk[:, :LANES].astype(MXU_DT)
        dv_ref[...] = dv.astype(MXU_DT)

        @pl.when(h == 0)
        def _():
            dkr_ref[...] = jnp.zeros_like(dkr_ref)

        dkr_ref[...] += dk[:, LANES:]

    full = pl.BlockSpec((S, LANES), lambda j, h: (0, h))
    kb = pl.BlockSpec((t, LANES), lambda j, h: (j, h))
    k0 = pl.BlockSpec((t, LANES), lambda j, h: (j, 0))
    return _call(
        body, comm, name=name, grid=(nb, H),
        in_specs=[full, full, kb, k0, kb, full, full, full],
        out_specs=[kb, k0, kb],
        out_shape=[jax.ShapeDtypeStruct((S, W), MXU_DT), jax.ShapeDtypeStruct((S, LANES), f32), jax.ShapeDtypeStruct((S, W), MXU_DT)],
        scratch_shapes=[], sem=("arbitrary", "arbitrary"), args=(qn, qr, kn, kr, v, o, lse, do))


def _loss_head(y, target, *, name):
    S, D = y.shape
    tm = _tile(S, ROW_TILE, 8)

    def body(y_ref, t_ref, loss_ref, dy_ref):
        @pl.when(pl.program_id(0) == 0)
        def _():
            loss_ref[...] = jnp.zeros_like(loss_ref)

        e = y_ref[...] - t_ref[...]
        dy_ref[...] = e / D
        part = 0.5 * jnp.sum(jnp.mean(e * e, axis=1, keepdims=True), axis=0, keepdims=True)
        loss_ref[...] += jnp.broadcast_to(part, loss_ref.shape)

    rb = pl.BlockSpec((tm, D), lambda r: (r, 0))
    return pl.pallas_call(
        body, name=name, grid=(S // tm,), in_specs=[rb, rb],
        out_specs=[pl.BlockSpec((1, LANES), lambda r: (0, 0)), rb],
        out_shape=[jax.ShapeDtypeStruct((1, LANES), f32), jax.ShapeDtypeStruct((S, D), f32)],
        compiler_params=_cp(("arbitrary",)))(y, target)


def _adamw(w, g, m, v, *, name):
    R, C = w.shape
    tm = _tile(R, max(8, (1 << 19) // max(C, 1) // 8 * 8), 8)

    def body(w_ref, g_ref, m_ref, v_ref, d_ref, nm_ref, nv_ref):
        gv = g_ref[...]
        nm = ADAM_B1 * m_ref[...] + (1.0 - ADAM_B1) * gv
        nv = ADAM_B2 * v_ref[...] + (1.0 - ADAM_B2) * (gv * gv)
        m_hat = nm / (1.0 - ADAM_B1 ** ADAM_STEP)
        v_hat = nv / (1.0 - ADAM_B2 ** ADAM_STEP)
        d_ref[...] = -ADAM_LR * (m_hat / (jnp.sqrt(v_hat) + ADAM_EPS) + ADAM_WD * w_ref[...])
        nm_ref[...] = nm
        nv_ref[...] = nv

    rb = pl.BlockSpec((tm, C), lambda r: (r, 0))
    sh = jax.ShapeDtypeStruct((R, C), f32)
    return pl.pallas_call(body, name=name, grid=(R // tm,), in_specs=[rb] * 4, out_specs=[rb] * 3, out_shape=[sh] * 3,
                          compiler_params=_cp(("parallel",)))(w, g, m, v)


def _me():
    return lax.axis_index("x"), lax.axis_index("y"), lax.axis_index("c")


def _other_chips(mx, my):
    return [(1 - mx, my), (mx, 1 - my), (1 - mx, 1 - my)]


_ANY = pl.BlockSpec(memory_space=pl.ANY)


class _GatherChips:
    def __init__(self, xs):
        self.arrays = list(xs)
        n = len(xs)
        for x in xs:
            assert x.shape[0] % 2 == 0
        self.halves = [x.shape[0] // 2 for x in xs]
        self.out_shapes = [jax.ShapeDtypeStruct((4,) + x.shape, x.dtype) for x in xs]
        self.scratch = [pltpu.SemaphoreType.DMA((n, 6)), pltpu.SemaphoreType.DMA((n, 6))]

    def _sends(self, x_refs, o_refs, send, recv):
        mx, my, mc = _me()
        me = 2 * mx + my
        out = []
        for t, hf in enumerate(self.halves):
            mine = pl.ds(mc * hf, hf)
            for j, (cx, cy) in enumerate(_other_chips(mx, my)):
                out.append(pltpu.make_async_remote_copy(x_refs[t].at[mine], o_refs[t].at[me, mine], send.at[t, j], recv.at[t, j],
                                                        device_id=(cx, cy, mc), device_id_type=MESH))
        return out

    def start(self, x_refs, o_refs, scr):
        for cp in self._sends(x_refs, o_refs, *scr):
            cp.start()

    def finish(self, x_refs, o_refs, scr):
        send, recv = scr
        mx, my, mc = _me()
        chips = _other_chips(mx, my)
        fwd = []
        for t, hf in enumerate(self.halves):
            mine = pl.ds(mc * hf, hf)
            for j, (cx, cy) in enumerate(chips):
                k = 2 * cx + cy
                pltpu.make_async_remote_copy(x_refs[t].at[mine], o_refs[t].at[k, mine], send.at[t, j], recv.at[t, j],
                                             device_id=(cx, cy, mc), device_id_type=MESH).wait_recv()
                cp = pltpu.make_async_remote_copy(o_refs[t].at[k, mine], o_refs[t].at[k, mine], send.at[t, 3 + j], recv.at[t, 3 + j],
                                                  device_id=(mx, my, 1 - mc), device_id_type=MESH)
                cp.start()
                fwd.append(cp)
        for t, hf in enumerate(self.halves):
            theirs = pl.ds((1 - mc) * hf, hf)
            for j, (cx, cy) in enumerate(chips):
                k = 2 * cx + cy
                pltpu.make_async_remote_copy(o_refs[t].at[k, theirs], o_refs[t].at[k, theirs], send.at[t, 3 + j], recv.at[t, 3 + j],
                                             device_id=(mx, my, 1 - mc), device_id_type=MESH).wait_recv()
        for cp in self._sends(x_refs, o_refs, send, recv) + fwd:
            cp.wait_send()


class _ScatterChips:
    def __init__(self, ps):
        self.arrays = list(ps)
        n = len(ps)
        self.out_shapes = [jax.ShapeDtypeStruct((3,) + p.shape[1:], p.dtype) for p in ps]
        self.scratch = [pltpu.SemaphoreType.DMA((n, 3)), pltpu.SemaphoreType.DMA((n, 3))]

    def _copies(self, p_refs, o_refs, send, recv):
        mx, my, mc = _me()
        return [pltpu.make_async_remote_copy(p_refs[t].at[2 * cx + cy], o_refs[t].at[j], send.at[t, j], recv.at[t, j],
                                             device_id=(cx, cy, mc), device_id_type=MESH)
                for t in range(len(self.arrays)) for j, (cx, cy) in enumerate(_other_chips(mx, my))]

    def start(self, p_refs, o_refs, scr):
        for cp in self._copies(p_refs, o_refs, *scr):
            cp.start()

    def finish(self, p_refs, o_refs, scr):
        for cp in self._copies(p_refs, o_refs, *scr):
            cp.wait()


def _run_comm(comm, *, name):
    n = len(comm.arrays)

    def body(*refs):
        ins, outs, scr = refs[:n], refs[n:2 * n], refs[2 * n:]
        comm.start(ins, outs, scr)
        comm.finish(ins, outs, scr)

    return pl.pallas_call(body, name=name, in_specs=[_ANY] * n, out_specs=[_ANY] * n, out_shape=comm.out_shapes,
                          scratch_shapes=comm.scratch, compiler_params=pltpu.CompilerParams(has_side_effects=True))(*comm.arrays)


def _call(body, comm, *, name, grid, in_specs, out_specs, out_shape, scratch_shapes, sem, args):
    if comm is None:
        res = pl.pallas_call(body, name=name, grid=grid, in_specs=in_specs, out_specs=out_specs, out_shape=out_shape,
                             scratch_shapes=scratch_shapes, compiler_params=_cp(sem))(*args)
        return list(res), None
    n_in, n_out, n_scr, nc = len(in_specs), len(out_specs), len(scratch_shapes), len(comm.arrays)

    def wrapped(*refs):
        ins, cins = refs[:n_in], refs[n_in:n_in + nc]
        outs, couts = refs[n_in + nc:n_in + nc + n_out], refs[n_in + nc + n_out:n_in + 2 * nc + n_out]
        scr, cscr = refs[n_in + 2 * nc + n_out:n_in + 2 * nc + n_out + n_scr], refs[n_in + 2 * nc + n_out + n_scr:]
        ids = [pl.program_id(d) for d in range(len(grid))]
        first = functools.reduce(jnp.logical_and, [i == 0 for i in ids])
        last = functools.reduce(jnp.logical_and, [i == g - 1 for i, g in zip(ids, grid)])

        @pl.when(first)
        def _():
            comm.start(cins, couts, cscr)

        body(*ins, *outs, *scr)

        @pl.when(last)
        def _():
            comm.finish(cins, couts, cscr)

    res = pl.pallas_call(
        wrapped, name=name, grid=grid, in_specs=list(in_specs) + [_ANY] * nc, out_specs=list(out_specs) + [_ANY] * nc,
        out_shape=list(out_shape) + comm.out_shapes, scratch_shapes=list(scratch_shapes) + comm.scratch,
        compiler_params=_cp(("arbitrary",) * len(grid)))(*args, *comm.arrays)
    return list(res[:n_out]), list(res[n_out:])


class _PairSend:
    def __init__(self, gs):
        self.arrays = list(gs)
        n = len(gs)
        self.halves = [g.shape[1] // 2 for g in gs]
        self.out_shapes = [jax.ShapeDtypeStruct((4, g.shape[1] // 2, g.shape[2]), g.dtype) for g in gs]
        self.scratch = [pltpu.SemaphoreType.DMA((n, 4)), pltpu.SemaphoreType.DMA((n, 4))]

    def _copies(self, g_refs, o_refs, send, recv):
        mx, my, mc = _me()
        return [pltpu.make_async_remote_copy(g_refs[t].at[k, pl.ds((1 - mc) * hf, hf)], o_refs[t].at[k], send.at[t, k], recv.at[t, k],
                                             device_id=(mx, my, 1 - mc), device_id_type=MESH)
                for t, hf in enumerate(self.halves) for k in range(4)]

    def start(self, g_refs, o_refs, scr):
        for cp in self._copies(g_refs, o_refs, *scr):
            cp.start()

    def finish(self, g_refs, o_refs, scr):
        for cp in self._copies(g_refs, o_refs, *scr):
            cp.wait()


def _pair_exchange_halves(fs, *, name):
    n = len(fs)

    def body(*refs):
        f_refs, o_refs = refs[:n], refs[n:2 * n]
        send, recv = refs[2 * n:]
        mx, my, mc = _me()
        cps = []
        for t in range(n):
            hf = f_refs[t].shape[0]
            mine = pl.ds(mc * hf, hf)
            cp = pltpu.make_async_remote_copy(f_refs[t], o_refs[t].at[mine], send.at[t], recv.at[t],
                                              device_id=(mx, my, 1 - mc), device_id_type=MESH)
            cp.start()
            cps.append(cp)
        for t in range(n):
            hf = f_refs[t].shape[0]
            theirs = pl.ds((1 - mc) * hf, hf)
            cps[t].wait_send()
            pltpu.make_async_remote_copy(f_refs[t], o_refs[t].at[theirs], send.at[t], recv.at[t],
                                         device_id=(mx, my, 1 - mc), device_id_type=MESH).wait_recv()

    return pl.pallas_call(
        body, name=name, in_specs=[_ANY] * n, out_specs=[_ANY] * n,
        out_shape=[jax.ShapeDtypeStruct((2 * f.shape[0], f.shape[1]), f.dtype) for f in fs],
        scratch_shapes=[pltpu.SemaphoreType.DMA((n,)), pltpu.SemaphoreType.DMA((n,))],
        compiler_params=pltpu.CompilerParams(has_side_effects=True))(*fs)


def _allgather_all(x, *, name):
    def body(x_ref, o_ref, send, recv, lsem):
        mx, my, mc = _me()
        me = 4 * mx + 2 * my + mc
        local = pltpu.make_async_copy(x_ref, o_ref.at[me], lsem)
        local.start()
        cps = []
        for j in range(1, 8):
            px, py, pc = mx ^ (j >> 2), my ^ ((j >> 1) & 1), mc ^ (j & 1)
            cp = pltpu.make_async_remote_copy(x_ref, o_ref.at[me], send.at[j - 1], recv.at[j - 1],
                                              device_id=(px, py, pc), device_id_type=MESH)
            cp.start()
            cps.append(cp)
        for j in range(1, 8):
            px, py, pc = mx ^ (j >> 2), my ^ ((j >> 1) & 1), mc ^ (j & 1)
            pltpu.make_async_remote_copy(x_ref, o_ref.at[4 * px + 2 * py + pc], send.at[j - 1], recv.at[j - 1],
                                         device_id=(px, py, pc), device_id_type=MESH).wait_recv()
        for cp in cps:
            cp.wait_send()
        local.wait()

    return pl.pallas_call(
        body, name=name, in_specs=[_ANY], out_specs=_ANY, out_shape=jax.ShapeDtypeStruct((8,) + x.shape, x.dtype),
        scratch_shapes=[pltpu.SemaphoreType.DMA((7,)), pltpu.SemaphoreType.DMA((7,)), pltpu.SemaphoreType.DMA],
        compiler_params=pltpu.CompilerParams(has_side_effects=True))(x)


def _add_half(g4, recv, mc, *, name):
    _, R, C = g4.shape
    hf = R // 2
    tm = _tile(hf, max(16, (1 << 19) // C // 16 * 16), 16)
    nb = hf // tm

    def body(mc_ref, g_ref, r_ref, o_ref, ob_ref):
        s = g_ref[...] + r_ref[...]
        o_ref[...] = s
        ob_ref[...] = s.astype(COMM_DT)

    ospec = pl.BlockSpec((1, tm, C), lambda k, i, mc_ref: (k, i, 0))
    return pl.pallas_call(
        body, name=name,
        grid_spec=pltpu.PrefetchScalarGridSpec(
            num_scalar_prefetch=1, grid=(4, nb),
            in_specs=[pl.BlockSpec((1, tm, C), lambda k, i, mc_ref: (k, mc_ref[0] * nb + i, 0)),
                      pl.BlockSpec((1, tm, C), lambda k, i, mc_ref: (k, i, 0))],
            out_specs=[ospec, ospec]),
        out_shape=[jax.ShapeDtypeStruct((4, hf, C), f32), jax.ShapeDtypeStruct((4, hf, C), COMM_DT)],
        compiler_params=_cp(("parallel", "parallel")))(mc, g4, recv)


def _sum_chips(p4, recv3, me, *, name):
    _, Rh, C = p4.shape
    tm = _tile(Rh, max(16, (1 << 19) // C // 16 * 16), 16)

    def body(me_ref, p_ref, r_ref, o_ref):
        o_ref[...] = ((p_ref[0] + r_ref[0].astype(f32)) + r_ref[1].astype(f32)) + r_ref[2].astype(f32)

    return pl.pallas_call(
        body, name=name,
        grid_spec=pltpu.PrefetchScalarGridSpec(
            num_scalar_prefetch=1, grid=(Rh // tm,),
            in_specs=[pl.BlockSpec((1, tm, C), lambda i, me_ref: (me_ref[0], i, 0)),
                      pl.BlockSpec((3, tm, C), lambda i, me_ref: (0, i, 0))],
            out_specs=pl.BlockSpec((tm, C), lambda i, me_ref: (i, 0))),
        out_shape=jax.ShapeDtypeStruct((Rh, C), f32),
        compiler_params=_cp(("parallel",)))(me, p4, recv3)


def _sum8(x8, *, name):
    _, R, C = x8.shape
    tm = _tile(R, 64, 8)

    def body(x_ref, o_ref):
        acc = x_ref[0]
        for k in range(1, 8):
            acc = acc + x_ref[k]
        o_ref[...] = acc

    return pl.pallas_call(body, name=name, grid=(R // tm,), in_specs=[pl.BlockSpec((8, tm, C), lambda i: (0, i, 0))],
                          out_specs=pl.BlockSpec((tm, C), lambda i: (i, 0)), out_shape=jax.ShapeDtypeStruct((R, C), f32),
                          compiler_params=_cp(("parallel",)))(x8)


def _ssd_layer_fwd(h, W, tag, comm=None):
    z = _mm(h, W["wz"], name=tag + "_z")
    xp = _mm(h, W["wxbc"], name=tag + "_xbc")
    dtr = _mm(h, W["wdt"], name=tag + "_dt")
    xbc = _conv_fwd(xp, W["conv_w"], W["conv_b"], name=tag + "_conv")
    (y, states), cres = _ssd_scan_fwd(xbc, dtr, W["dt_bias"], W["a_log"], W["d"], name=tag + "_scan", comm=comm)
    DI = y.shape[1]
    G = SSD_N_GROUPS
    gs = DI // G
    (yn,) = _rowwise(_ssd_gate_fn, [(W["norm_w"], "c", gs)], [(y, "c", gs), (z, "c", gs)], [(DI, MXU_DT, "c", gs)],
                     name=tag + "_gate", ncol=G, tm=512)
    out = _mm(yn, W["wout"], name=tag + "_out")
    return out, dict(h=h, z=z, xp=xp, dtr=dtr, xbc=xbc, states=states, y=y, yn=yn), cres


def _carried_rowwise_bwd(plan, i, *a, **kw):
    early = plan.bwd_early_comm(i)
    if early is None:
        return _rowwise_bwd(*a, **kw)
    dp, di, cres = _rowwise_bwd(*a, comm=early, **kw)
    plan.bwd_early_done(i, cres)
    return dp, di


def _ssd_layer_bwd(sv, W, dr, drb, tag, plan, i):
    h = sv["h"]
    DI = sv["y"].shape[1]
    G = SSD_N_GROUPS
    gs = DI // G
    gr = {}
    dyn = _mm(drb, W["wout"], tb=True, name=tag + "_dyn")
    gr["wout"] = _mm(sv["yn"], drb, ta=True, name=tag + "_dwout")
    (dnw,), (dy, dz) = _carried_rowwise_bwd(plan, i, _ssd_gate_fn, [(W["norm_w"], "c", gs)], [(sv["y"], "c", gs), (sv["z"], "c", gs)],
                                             [(dyn, "c", gs)], name=tag + "_dgate", ncol=G, tm=512, din_dtypes=[(f32,), (MXU_DT,)])
    gr["norm_w"] = dnw
    (dxs, dB, dC, ddtr, gr["dt_bias"], gr["a_log"], gr["d"]), cres = _ssd_scan_bwd(
        sv["xbc"], sv["dtr"], W["dt_bias"], W["a_log"], W["d"], sv["states"], dy, name=tag + "_dscan", comm=plan.bwd_comm(i))
    plan.bwd_done(i, cres)
    dxp, gr["conv_w"], gr["conv_b"] = _conv_bwd(sv["xp"], W["conv_w"], W["conv_b"], [dxs, dB, dC], name=tag + "_dconv", dx_dtype=MXU_DT)
    dh = _mm(dz, W["wz"], tb=True, add=dr, add_scale=_alpha(), name=tag + "_dh1")
    dh = _mm(dxp, W["wxbc"], tb=True, add=dh, name=tag + "_dh2")
    dh = _mm(ddtr, W["wdt"], tb=True, add=dh, name=tag + "_dh3")
    gr["wz"] = _mm(h, dz, ta=True, name=tag + "_dwz")
    gr["wxbc"] = _mm(h, dxp, ta=True, name=tag + "_dwxbc")
    gr["wdt"] = _mm(h, ddtr, ta=True, name=tag + "_dwdt")
    return dh, gr


def _mla_layer_fwd(h, W, cos, sin, tag, comm=None):
    QR, KR = W["wqc"].shape[1], W["wkvc"].shape[1]
    HW = W["wqn"].shape[1]
    H = HW // LANES
    qc = _mm(h, W["wqc"], name=tag + "_qc")
    kvc = _mm(h, W["wkvc"], name=tag + "_kvc")
    krp = _mm(h, W["wkr"], name=tag + "_krp")
    z = _mm(h, W["wz"], name=tag + "_z")
    (qcn,) = _rowwise(_rms_fn, [(W["q_norm"], "a", QR)], [(qc, "a", QR)], [(QR, MXU_DT, "a", QR)], name=tag + "_qnorm")
    (kvn,) = _rowwise(_rms_fn, [(W["kv_norm"], "a", KR)], [(kvc, "a", KR)], [(KR, MXU_DT, "a", KR)], name=tag + "_kvnorm")
    qn = _mm(qcn, W["wqn"], name=tag + "_qn", out_dtype=MXU_DT)
    qrp = _mm(qcn, W["wqr"], name=tag + "_qrp")
    kn = _mm(kvn, W["wkn"], name=tag + "_kn", out_dtype=MXU_DT)
    v = _mm(kvn, W["wv"], name=tag + "_v", out_dtype=MXU_DT)
    (qr,) = _rowwise(_rope_fn, [], [(cos, "a", LANES), (sin, "a", LANES), (qrp, "c", LANES)], [(HW, MXU_DT, "c", LANES)],
                     name=tag + "_qrope", ncol=H, tm=1024)
    (kr,) = _rowwise(_rope_fn, [], [(cos, "a", LANES), (sin, "a", LANES), (krp, "a", LANES)], [(LANES, MXU_DT, "a", LANES)],
                     name=tag + "_krope")
    (o, lse), cres = _attn_fwd(qn, qr, kn, kr, v, name=tag + "_attn", comm=comm)
    (og,) = _rowwise(_mul_silu_fn, [], [(o, "a", HW), (z, "a", HW)], [(HW, MXU_DT, "a", HW)], name=tag + "_ogate")
    out = _mm(og, W["wout"], name=tag + "_out")
    return out, dict(h=h, qc=qc, kvc=kvc, z=z, qcn=qcn, kvn=kvn, qn=qn, qr=qr, kn=kn, kr=kr, v=v, o=o, lse=lse, og=og), cres


def _mla_layer_bwd(sv, W, cos, sin, dr, drb, tag, plan, i):
    h = sv["h"]
    QR, KR = W["wqc"].shape[1], W["wkvc"].shape[1]
    HW = W["wqn"].shape[1]
    H = HW // LANES
    gr = {}
    dog = _mm(drb, W["wout"], tb=True, name=tag + "_dog")
    gr["wout"] = _mm(sv["og"], drb, ta=True, name=tag + "_dwout")
    _, (do, dz) = _rowwise_bwd(_mul_silu_fn, [], [(sv["o"], "a", HW), (sv["z"], "a", HW)], [(dog, "a", HW)], name=tag + "_dogate",
                               din_dtypes=[(f32,), (MXU_DT,)])
    att = (sv["qn"], sv["qr"], sv["kn"], sv["kr"], sv["v"], sv["o"], sv["lse"], do)
    (dqn, dqr), cres = _attn_bwd_dq(*att, name=tag + "_dq", comm=plan.bwd_early_comm(i))
    plan.bwd_early_done(i, cres)
    (dkn, dkr, dv), cres = _attn_bwd_dkv(*att, name=tag + "_dkv", comm=plan.bwd_comm(i))
    plan.bwd_done(i, cres)
    _, (dqrp,) = _rowwise_bwd(_rope_fn, [], [(cos, "a", LANES), (sin, "a", LANES), (dqr, "c", LANES)], [(dqr, "c", LANES)],
                              name=tag + "_dqrope", ncol=H, tm=1024, diff_i=[2], din_dtypes=[(MXU_DT,)])
    _, (dkrp,) = _rowwise_bwd(_rope_fn, [], [(cos, "a", LANES), (sin, "a", LANES), (dkr, "a", LANES)], [(dkr, "a", LANES)],
                              name=tag + "_dkrope", diff_i=[2], din_dtypes=[(MXU_DT,)])
    dqcn = _mm(dqn, W["wqn"], tb=True, name=tag + "_dqcn1")
    dqcn = _mm(dqrp, W["wqr"], tb=True, add=dqcn, name=tag + "_dqcn2")
    dkvn = _mm(dkn, W["wkn"], tb=True, name=tag + "_dkvn1")
    dkvn = _mm(dv, W["wv"], tb=True, add=dkvn, name=tag + "_dkvn2")
    gr["wqn"] = _mm(sv["qcn"], dqn, ta=True, name=tag + "_dwqn")
    gr["wqr"] = _mm(sv["qcn"], dqrp, ta=True, name=tag + "_dwqr")
    gr["wkn"] = _mm(sv["kvn"], dkn, ta=True, name=tag + "_dwkn")
    gr["wv"] = _mm(sv["kvn"], dv, ta=True, name=tag + "_dwv")
    (gr["q_norm"],), (dqc,) = _rowwise_bwd(_rms_fn, [(W["q_norm"], "a", QR)], [(sv["qc"], "a", QR)], [(dqcn, "a", QR)], name=tag + "_dqnorm",
                                           din_dtypes=[(MXU_DT,)])
    (gr["kv_norm"],), (dkvc,) = _rowwise_bwd(_rms_fn, [(W["kv_norm"], "a", KR)], [(sv["kvc"], "a", KR)], [(dkvn, "a", KR)], name=tag + "_dkvnorm",
                                             din_dtypes=[(MXU_DT,)])
    dh = _mm(dz, W["wz"], tb=True, add=dr, add_scale=_alpha(), name=tag + "_dh1")
    dh = _mm(dqc, W["wqc"], tb=True, add=dh, name=tag + "_dh2")
    dh = _mm(dkvc, W["wkvc"], tb=True, add=dh, name=tag + "_dh3")
    dh = _mm(dkrp, W["wkr"], tb=True, add=dh, name=tag + "_dh4")
    gr["wz"] = _mm(h, dz, ta=True, name=tag + "_dwz")
    gr["wqc"] = _mm(h, dqc, ta=True, name=tag + "_dwqc")
    gr["wkvc"] = _mm(h, dkvc, ta=True, name=tag + "_dwkvc")
    gr["wkr"] = _mm(h, dkrp, ta=True, name=tag + "_dwkr")
    return dh, gr


def _gdn_layer_fwd(h, W, tag, comm=None):
    HK, HV, DK, DV = GDN_N_QK_HEADS, GDN_N_V_HEADS, GDN_DK, GDN_DV
    KD, VD = HK * DK, HV * DV
    qkvp = _mm(h, W["wqkv"], name=tag + "_qkv")
    z = _mm(h, W["wz"], name=tag + "_z")
    br = _mm(h, W["wb"], name=tag + "_b")
    ar = _mm(h, W["wa"], name=tag + "_a")
    qkv = _conv_fwd(qkvp, W["conv_w"], jnp.zeros((1, qkvp.shape[1]), f32), name=tag + "_conv")
    scale = jnp.concatenate([jnp.full((1, KD), DK ** -0.5, f32), jnp.ones((1, KD), f32)], axis=1)
    (qkn,) = _rowwise(_l2_fn, [(scale, "c", DK)], [(qkv, "c", DK)], [(2 * KD, f32, "c", DK)], name=tag + "_l2", ncol=2 * HK, tm=2048)
    (o, states, inverses), cres = _gdn_scan_fwd(qkn, qkv, br, ar, W["a_log"], W["dt_bias"], name=tag + "_scan", comm=comm)
    (on,) = _rowwise(_gdn_gate_fn, [(W["norm_w"], "a", DV)], [(o, "c", DV), (z, "c", DV)], [(VD, MXU_DT, "c", DV)],
                     name=tag + "_gate", ncol=HV, tm=1024)
    out = _mm(on, W["wout"], name=tag + "_out")
    return out, dict(h=h, qkvp=qkvp, z=z, br=br, ar=ar, qkv=qkv, qkn=qkn, o=o, states=states, inverses=inverses, on=on, scale=scale), cres


def _gdn_layer_bwd(sv, W, dr, drb, tag, plan, i):
    h = sv["h"]
    HK, HV, DK, DV = GDN_N_QK_HEADS, GDN_N_V_HEADS, GDN_DK, GDN_DV
    KD, VD = HK * DK, HV * DV
    gr = {}
    don = _mm(drb, W["wout"], tb=True, name=tag + "_don")
    gr["wout"] = _mm(sv["on"], drb, ta=True, name=tag + "_dwout")
    (gr["norm_w"],), (do, dz) = _carried_rowwise_bwd(plan, i, _gdn_gate_fn, [(W["norm_w"], "a", DV)], [(sv["o"], "c", DV), (sv["z"], "c", DV)],
                                                      [(don, "c", DV)], name=tag + "_dgate", ncol=HV, tm=1024, din_dtypes=[(f32,), (MXU_DT,)])
    (dq, dk, dv, dbr, dar, gr["a_log"], gr["dt_bias"]), cres = _gdn_scan_bwd(
        sv["qkn"], sv["qkv"], sv["br"], sv["ar"], W["a_log"], W["dt_bias"], sv["states"], sv["inverses"], do, name=tag + "_dscan",
        comm=plan.bwd_comm(i))
    plan.bwd_done(i, cres)
    _, (dqq,) = _rowwise_bwd(_l2_fn, [(sv["scale"], "c", DK)], [(sv["qkv"], "c", DK)], [(dq, "c", DK)],
                             name=tag + "_dl2q", ncol=HK, tm=2048, diff_p=[])
    _, (dqk,) = _rowwise_bwd(_l2_fn, [(sv["scale"], "c", DK, HK)], [(sv["qkv"], "c", DK, HK)], [(dk, "c", DK)],
                             name=tag + "_dl2k", ncol=HK, tm=2048, diff_p=[])
    dqkvp, gr["conv_w"], _ = _conv_bwd(sv["qkvp"], W["conv_w"], jnp.zeros((1, sv["qkvp"].shape[1]), f32), [dqq, dqk, dv],
                                       name=tag + "_dconv", dx_dtype=MXU_DT)
    dh = _mm(dz, W["wz"], tb=True, add=dr, add_scale=_alpha(), name=tag + "_dh1")
    dh = _mm(dqkvp, W["wqkv"], tb=True, add=dh, name=tag + "_dh2")
    dh = _mm(dbr, W["wb"], tb=True, add=dh, name=tag + "_dh3")
    dh = _mm(dar, W["wa"], tb=True, add=dh, name=tag + "_dh4")
    gr["wz"] = _mm(h, dz, ta=True, name=tag + "_dwz")
    gr["wqkv"] = _mm(h, dqkvp, ta=True, name=tag + "_dwqkv")
    gr["wb"] = _mm(h, dbr, ta=True, name=tag + "_dwb")
    gr["wa"] = _mm(h, dar, ta=True, name=tag + "_dwa")
    return dh, gr


def _rope_tables(positions):
    half = MLA_ROPE // 2
    inv_freq = ROPE_THETA ** (-jnp.arange(0, MLA_ROPE, 2, dtype=f32) / MLA_ROPE)
    ang = positions.astype(f32)[:, None] * inv_freq
    cos, sin = jnp.cos(ang), jnp.sin(ang)
    S = positions.shape[0]
    pad = jnp.zeros((S, LANES - 2 * half), f32)
    return jnp.concatenate([cos, cos, pad + 1.0], axis=1), jnp.concatenate([sin, sin, pad], axis=1)


class _LocalPlan:
    def __init__(self, LW):
        self.LW, self.grads = LW, [None] * DEPTH

    def weights(self, i):
        return self.LW[i]

    def fwd_comm(self, i):
        return None

    def fwd_done(self, i, res):
        pass

    def bwd_early_comm(self, i):
        return None

    def bwd_early_done(self, i, res):
        pass

    def bwd_comm(self, i):
        return None

    def bwd_done(self, i, res):
        pass

    def layer_grads(self, i, gr):
        self.grads[i] = gr


def _local_step(x, positions, target, ln_g, ln_b, plan):
    cos, sin = _rope_tables(positions)
    h, hb = x, x.astype(MXU_DT)
    saved, LW = [], []
    for i in range(DEPTH):
        kind, tag = i % 3, "l%d" % i
        LW.append(plan.weights(i))
        comm = plan.fwd_comm(i)
        if kind == 0:
            y, sv, cres = _ssd_layer_fwd(hb, LW[i], tag, comm)
        elif kind == 1:
            y, sv, cres = _mla_layer_fwd(hb, LW[i], cos, sin, tag, comm)
        else:
            y, sv, cres = _gdn_layer_fwd(hb, LW[i], tag, comm)
        plan.fwd_done(i, cres)
        D = h.shape[1]
        r, h, hb = _rowwise(_res_ln_fn, [(ln_g[i], "a", D), (ln_b[i], "a", D)], [(h, "a", D), (y, "a", D)],
                            [(D, f32, "a", D), (D, f32, "a", D), (D, MXU_DT, "a", D)], name=tag + "_ln")
        sv["r"] = r
        saved.append(sv)
    loss, dh = _loss_head(h, target, name="loss_head")
    dg, db = [None] * DEPTH, [None] * DEPTH
    for i in reversed(range(DEPTH)):
        kind, tag = i % 3, "l%d" % i
        sv = saved[i]
        D = dh.shape[1]
        (dg[i], db[i]), (dr, drb) = _rowwise_bwd(_ln_fn, [(ln_g[i], "a", D), (ln_b[i], "a", D)], [(sv["r"], "a", D)], [(dh, "a", D)],
                                                 name=tag + "_dln", din_dtypes=[(f32, MXU_DT)])
        if kind == 0:
            dh, gr = _ssd_layer_bwd(sv, LW[i], dr, drb, tag, plan, i)
        elif kind == 1:
            dh, gr = _mla_layer_bwd(sv, LW[i], cos, sin, dr, drb, tag, plan, i)
        else:
            dh, gr = _gdn_layer_bwd(sv, LW[i], dr, drb, tag, plan, i)
        plan.layer_grads(i, gr)
    return loss, dh, dg, db


_WEIGHTS = ["ssd_in_w", "ssd_conv_w", "ssd_conv_b", "ssd_dt_bias", "ssd_a_log", "ssd_d", "ssd_norm_w", "ssd_out_w",
            "mla_in_w", "mla_q_norm_w", "mla_q_up_w", "mla_kv_norm_w", "mla_kv_up_w", "mla_out_w",
            "gdn_in_w", "gdn_conv_w", "gdn_a_log", "gdn_dt_bias", "gdn_norm_w", "gdn_out_w", "ln_g", "ln_b"]
_BIG = {"ssd_in_w": "col", "ssd_out_w": "row", "mla_in_w": "col", "mla_q_up_w": "col", "mla_kv_up_w": "col",
        "mla_out_w": "row", "gdn_in_w": "col", "gdn_out_w": "row"}
_SMALL_SHARDED = ["ssd_conv_w", "ssd_conv_b", "ssd_norm_w", "gdn_conv_w"]
_PACK_ROWS = 16


def _gathered_to_full(g, kind, nl):
    if kind == "col":
        _, RK, Ns = g.shape
        return g.reshape(4, nl, RK // nl, Ns).transpose(1, 2, 0, 3).reshape(nl, RK // nl, 4 * Ns)
    _, RK, N = g.shape
    return g.reshape(4, nl, RK // nl, N).transpose(1, 0, 2, 3).reshape(nl, 4 * (RK // nl), N)


def _full_to_slots(f, kind):
    nl, K, N = f.shape
    if kind == "col":
        return f.reshape(nl, K, 4, N // 4).transpose(2, 0, 1, 3).reshape(4, nl * K, N // 4)
    return f.reshape(nl, 4, K // 4, N).transpose(1, 0, 2, 3).reshape(4, nl * (K // 4), N)


def _pack(arrs):
    flat = jnp.concatenate([a.reshape(-1).astype(f32) for a in arrs])
    unit = _PACK_ROWS * LANES
    n = -(-flat.shape[0] // unit) * unit
    return jnp.pad(flat, (0, n - flat.shape[0])).reshape(_PACK_ROWS, n // _PACK_ROWS)


def _unpack(packed, shapes):
    flat = packed.reshape(-1)
    out, off = [], 0
    for sh in shapes:
        n = math.prod(sh)
        out.append(flat[off:off + n].reshape(sh))
        off += n
    return out


def _pad_lanes(a):
    return jnp.pad(a, [(0, 0)] * (a.ndim - 1) + [(0, LANES - a.shape[-1])])


_IN_PROJ = ("ssd_in_w", "mla_in_w", "gdn_in_w")


class _ColSlots:
    def __init__(self, slots):
        self.slots = slots
        self.shape = (slots.shape[1], 4 * slots.shape[2])

    def __getitem__(self, idx):
        _, cols = idx
        ns = self.slots.shape[2]
        a = cols.start or 0
        b = self.shape[1] if cols.stop is None else cols.stop
        parts = [self.slots[k][:, max(a, k * ns) - k * ns:min(b, (k + 1) * ns) - k * ns]
                 for k in range(4) if max(a, k * ns) < min(b, (k + 1) * ns)]
        return parts[0] if len(parts) == 1 else jnp.concatenate(parts, axis=1)


def _col_slots(pieces):
    widths = [p.shape[1] for p in pieces]
    ns = sum(widths) // 4
    slots = []
    for k in range(4):
        lo, hi, off, parts = k * ns, (k + 1) * ns, 0, []
        for p, wd in zip(pieces, widths):
            if max(lo, off) < min(hi, off + wd):
                parts.append(p[:, max(lo, off) - off:min(hi, off + wd) - off])
            off += wd
        slots.append(parts[0] if len(parts) == 1 else jnp.concatenate(parts, axis=1))
    return jnp.stack(slots)


def _layer_dict(i, full):
    G, N, P = SSD_N_GROUPS, SSD_D_STATE, SSD_HEAD_DIM
    kind, j = i % 3, i // 3
    if kind == 0:
        H = full["ssd_dt_bias"][j].shape[0]
        DI = H * P
        CD = DI + 2 * G * N
        win = full["ssd_in_w"][j]
        return dict(wz=win[:, :DI], wxbc=win[:, DI:DI + CD], wdt=win[:, DI + CD:], conv_w=full["ssd_conv_w"][j],
                    conv_b=full["ssd_conv_b"][j][None], dt_bias=full["ssd_dt_bias"][j][None], a_log=full["ssd_a_log"][j][None],
                    d=full["ssd_d"][j][None], norm_w=full["ssd_norm_w"][j][None], wout=full["ssd_out_w"][j])
    if kind == 1:
        QR, KR = MLA_Q_RANK, MLA_KV_RANK
        win = full["mla_in_w"][j]
        Hh = full["mla_q_up_w"][j].shape[1] // (MLA_NOPE + MLA_ROPE)
        qup = full["mla_q_up_w"][j].reshape(QR, Hh, MLA_NOPE + MLA_ROPE)
        kvup = full["mla_kv_up_w"][j].reshape(KR, Hh, MLA_NOPE + MLA_V)
        return dict(wqc=win[:, :QR], wkvc=win[:, QR:QR + KR], wkr=_pad_lanes(win[:, QR + KR:QR + KR + MLA_ROPE]),
                    wz=win[:, QR + KR + MLA_ROPE:], q_norm=full["mla_q_norm_w"][j][None], kv_norm=full["mla_kv_norm_w"][j][None],
                    wqn=qup[:, :, :MLA_NOPE].reshape(QR, Hh * MLA_NOPE), wqr=_pad_lanes(qup[:, :, MLA_NOPE:]).reshape(QR, Hh * LANES),
                    wkn=kvup[:, :, :MLA_NOPE].reshape(KR, Hh * MLA_NOPE), wv=kvup[:, :, MLA_NOPE:].reshape(KR, Hh * MLA_V),
                    wout=full["mla_out_w"][j])
    KD, VD, HV = GDN_N_QK_HEADS * GDN_DK, GDN_N_V_HEADS * GDN_DV, GDN_N_V_HEADS
    win = full["gdn_in_w"][j]
    c0, c1 = 2 * KD + VD, 2 * KD + 2 * VD
    return dict(wqkv=win[:, :c0], wz=win[:, c0:c1], wb=win[:, c1:c1 + HV], wa=win[:, c1 + HV:], conv_w=full["gdn_conv_w"][j],
                a_log=full["gdn_a_log"][j][None], dt_bias=full["gdn_dt_bias"][j][None], norm_w=full["gdn_norm_w"][j][None],
                wout=full["gdn_out_w"][j])


def _layer_weights(full, D):
    return [_layer_dict(i, full) for i in range(DEPTH)]


def _layer_full_grads(i, g, slots=False):
    kind = i % 3
    join = _col_slots if slots else (lambda pieces: jnp.concatenate(pieces, axis=1))
    if kind == 0:
        out = {"ssd_in_w": join([g["wz"], g["wxbc"], g["wdt"]]), "ssd_conv_w": g["conv_w"], "ssd_out_w": g["wout"]}
        for n in ("conv_b", "dt_bias", "a_log", "d", "norm_w"):
            out["ssd_" + n] = g[n][0]
        return out
    if kind == 1:
        QR, KR = g["wqn"].shape[0], g["wkn"].shape[0]
        Hh = g["wqn"].shape[1] // MLA_NOPE
        return {"mla_in_w": join([g["wqc"], g["wkvc"], g["wkr"][:, :MLA_ROPE], g["wz"]]),
                "mla_q_up_w": jnp.concatenate([g["wqn"].reshape(QR, Hh, MLA_NOPE), g["wqr"].reshape(QR, Hh, LANES)[:, :, :MLA_ROPE]],
                                              axis=2).reshape(QR, -1),
                "mla_kv_up_w": jnp.concatenate([g["wkn"].reshape(KR, Hh, MLA_NOPE), g["wv"].reshape(KR, Hh, MLA_V)], axis=2).reshape(KR, -1),
                "mla_q_norm_w": g["q_norm"][0], "mla_kv_norm_w": g["kv_norm"][0], "mla_out_w": g["wout"]}
    out = {"gdn_in_w": join([g["wqkv"], g["wz"], g["wb"], g["wa"]]), "gdn_conv_w": g["conv_w"], "gdn_out_w": g["wout"]}
    for n in ("a_log", "dt_bias", "norm_w"):
        out["gdn_" + n] = g[n][0]
    return out


def _full_grads(grads, dg, db):
    per = {n: [] for n in _WEIGHTS}
    for i in range(DEPTH):
        for n, a in _layer_full_grads(i, grads[i]).items():
            per[n].append(a)
        per["ln_g"].append(dg[i][0])
        per["ln_b"].append(db[i][0])
    return {n: jnp.stack(v) for n, v in per.items()}


class _DistPlan:
    def __init__(self, w, chip, core):
        self.w, self.chip = w, chip
        self.chip_arr = jnp.reshape(chip, (1,)).astype(jnp.int32)
        self.core_arr = jnp.reshape(core, (1,)).astype(jnp.int32)
        self.full = {n: {} for n in _BIG}
        self.g4, self.p4, self.fin, self.small_grads = {}, {}, {}, [None] * DEPTH
        got = self._gather(0, extra=[_pack([w[n] for n in _SMALL_SHARDED])], name="gather_l0")
        parts = [_unpack(got[k], [w[n].shape for n in _SMALL_SHARDED]) for k in range(4)]
        for t, n in enumerate(_SMALL_SHARDED):
            self.full[n] = jnp.concatenate([parts[k][t] for k in range(4)], axis=-1)
        for n in _WEIGHTS:
            if n not in self.full:
                self.full[n] = w[n]

    @staticmethod
    def names(i):
        return [["ssd_in_w", "ssd_out_w"], ["mla_in_w", "mla_q_up_w", "mla_kv_up_w", "mla_out_w"], ["gdn_in_w", "gdn_out_w"]][i % 3]

    def _shards(self, i):
        return [self.w[n][i // 3].astype(MXU_DT) for n in self.names(i)]

    def _fill(self, i, shards, got):
        for n, s, g in zip(self.names(i), shards, got):
            g = lax.dynamic_update_slice(g, s[None], (self.chip, 0, 0))
            self.full[n][i // 3] = _ColSlots(g) if n in _IN_PROJ else _gathered_to_full(g, _BIG[n], 1)[0]

    def _gather(self, i, extra, name):
        shards = self._shards(i)
        got = _run_comm(_GatherChips(shards + extra), name=name)
        self._fill(i, shards, got[:len(shards)])
        return [lax.dynamic_update_slice(g, s[None], (self.chip, 0, 0)) for g, s in zip(got[len(shards):], extra)][0]

    def weights(self, i):
        return _layer_dict(i, self.full)

    def fwd_comm(self, i):
        if i + 1 >= DEPTH:
            return None
        self._pending_shards = self._shards(i + 1)
        return _GatherChips(self._pending_shards)

    def fwd_done(self, i, res):
        if res is not None:
            self._fill(i + 1, self._pending_shards, res)

    def layer_grads(self, i, gr):
        fg = _layer_full_grads(i, gr, slots=True)
        names = self.names(i)
        self.small_grads[i] = {n: a for n, a in fg.items() if n not in _BIG}
        self.g4[i] = [fg[n] if n in _IN_PROJ else _full_to_slots(fg[n][None], _BIG[n]) for n in names]
        if i == 0:
            self._pair_add(0, _run_comm(_PairSend(self.g4[0]), name="grad_pair_send_l0"))
            self._sum(0, _run_comm(self._scatter(0), name="grad_chip_scatter_l0"))

    def _pair_add(self, i, r1):
        self.p4[i] = [_add_half(a, b, self.core_arr, name="grad_pair_add_l%d_%s" % (i, n))
                      for a, b, n in zip(self.g4[i], r1, self.names(i))]

    def bwd_early_comm(self, i):
        return _PairSend(self.g4[i + 1]) if i + 1 < DEPTH else None

    def bwd_early_done(self, i, res):
        if res is not None:
            self._pair_add(i + 1, res)

    def _scatter(self, i):
        return _ScatterChips([p[1] for p in self.p4[i]])

    def _sum(self, i, r2):
        self.fin[i] = [_sum_chips(p[0], b, self.chip_arr, name="grad_chip_sum_l%d_%s" % (i, n))
                       for p, b, n in zip(self.p4[i], r2, self.names(i))]

    def bwd_comm(self, i):
        return self._scatter(i + 1) if i + 1 < DEPTH else None

    def bwd_done(self, i, res):
        if res is not None:
            self._sum(i + 1, res)

    def grad_shards(self, core):
        order = [(i, t) for i in range(DEPTH) for t in range(len(self.names(i)))]
        fins = [self.fin[i][t] for i, t in order]
        got = _pair_exchange_halves(fins, name="grad_pair_share")
        got = [lax.dynamic_update_slice(g, f, (core * f.shape[0], 0)) for g, f in zip(got, fins)]
        per = {n: [] for n in _BIG}
        for (i, t), g in zip(order, got):
            per[self.names(i)[t]].append(g)
        return {n: (v[0] if len(v) == 1 else jnp.concatenate(v, axis=0)) for n, v in per.items()}


def kernel(x, positions, ssd_in_w, ssd_conv_w, ssd_conv_b, ssd_dt_bias, ssd_a_log, ssd_d, ssd_norm_w, ssd_out_w, mla_in_w, mla_q_norm_w, mla_q_up_w, mla_kv_norm_w, mla_kv_up_w, mla_out_w, gdn_in_w, gdn_conv_w, gdn_a_log, gdn_dt_bias, gdn_norm_w, gdn_out_w, ln_g, ln_b, loss_target, m_ssd_in_w, m_ssd_conv_w, m_ssd_conv_b, m_ssd_dt_bias, m_ssd_a_log, m_ssd_d, m_ssd_norm_w, m_ssd_out_w, m_mla_in_w, m_mla_q_norm_w, m_mla_q_up_w, m_mla_kv_norm_w, m_mla_kv_up_w, m_mla_out_w, m_gdn_in_w, m_gdn_conv_w, m_gdn_a_log, m_gdn_dt_bias, m_gdn_norm_w, m_gdn_out_w, m_ln_g, m_ln_b, v_ssd_in_w, v_ssd_conv_w, v_ssd_conv_b, v_ssd_dt_bias, v_ssd_a_log, v_ssd_d, v_ssd_norm_w, v_ssd_out_w, v_mla_in_w, v_mla_q_norm_w, v_mla_q_up_w, v_mla_kv_norm_w, v_mla_kv_up_w, v_mla_out_w, v_gdn_in_w, v_gdn_conv_w, v_gdn_a_log, v_gdn_dt_bias, v_gdn_norm_w, v_gdn_out_w, v_ln_g, v_ln_b):
    args = dict(locals())
    w = {n: args[n] for n in _WEIGHTS}
    mom = {n: args["m_" + n] for n in _WEIGHTS}
    vel = {n: args["v_" + n] for n in _WEIGHTS}
    mx, my, mc = _me()
    chip = 2 * mx + my
    small = [n for n in _WEIGHTS if n not in _BIG]
    big = list(_BIG)

    plan = _DistPlan(w, chip, mc)
    loss, gx, dg, db = _local_step(x[0], positions[0], loss_target[0], [plan.full["ln_g"][i][None] for i in range(DEPTH)],
                                   [plan.full["ln_b"][i][None] for i in range(DEPTH)], plan)
    per = {n: [] for n in small}
    for i in range(DEPTH):
        for n, a in plan.small_grads[i].items():
            per[n].append(a)
        per["ln_g"].append(dg[i][0])
        per["ln_b"].append(db[i][0])
    fg = {n: jnp.stack(v) for n, v in per.items()}
    gsh = plan.grad_shards(mc)

    out_g, out_d, out_m, out_v = {}, {}, {}, {}
    for n in big:
        g = gsh[n]
        sh = w[n].shape
        to2 = lambda a: a.reshape(-1, sh[-1])
        d_, m_, v_ = _adamw(to2(w[n]), g, to2(mom[n]), to2(vel[n]), name="adamw_" + n)
        out_g[n], out_d[n], out_m[n], out_v[n] = g.reshape(sh), d_.reshape(sh), m_.reshape(sh), v_.reshape(sh)

    summed = _sum8(_allgather_all(_pack([fg[n] for n in small] + [loss[0, :1]]), name="gather_small"), name="sum_small")
    sg = _unpack(summed, [fg[n].shape for n in small] + [(1,)])
    loss_total = sg[-1][0]
    gs = {}
    for n, g in zip(small, sg[:-1]):
        if n in _SMALL_SHARDED:
            ws = w[n].shape[-1]
            g = lax.dynamic_slice_in_dim(g, chip * ws, ws, axis=g.ndim - 1)
        gs[n] = g
    shapes = [w[n].shape for n in small]
    d_, m_, v_ = _adamw(_pack([w[n] for n in small]), _pack([gs[n] for n in small]), _pack([mom[n] for n in small]),
                        _pack([vel[n] for n in small]), name="adamw_small")
    for n, a, b, c in zip(small, _unpack(d_, shapes), _unpack(m_, shapes), _unpack(v_, shapes)):
        out_g[n], out_d[n], out_m[n], out_v[n] = gs[n], a, b, c

    return (loss_total, gx[None], *[out_g[n] for n in _WEIGHTS], *[out_d[n] for n in _WEIGHTS],
            *[out_m[n] for n in _WEIGHTS], *[out_v[n] for n in _WEIGHTS])
```

```python
import functools
import math

import jax
import jax.numpy as jnp
from jax import lax
from jax.experimental import pallas as pl
from jax.experimental.pallas import tpu as pltpu

f32 = jnp.float32
HI = lax.Precision.HIGHEST
MXU_DT = jnp.bfloat16
COMM_DT = jnp.bfloat16
MESH = pl.DeviceIdType.MESH

DEPTH = 4
LN_EPS = 1e-5
RMS_EPS = 1e-6
SSD_HEAD_DIM = 64
SSD_N_GROUPS = 8
SSD_D_STATE = 128
SSD_CONV = 4
SSD_CHUNK = 128
MLA_Q_RANK = 768
MLA_KV_RANK = 512
MLA_NOPE = 128
MLA_ROPE = 64
MLA_V = 128
ROPE_THETA = 10000.0
GDN_N_QK_HEADS = 16
GDN_N_V_HEADS = 32
GDN_DK = 128
GDN_DV = 128
GDN_CONV = 4
GDN_CHUNK = 64
ADAM_LR = 0.001
ADAM_B1 = 0.9
ADAM_B2 = 0.999
ADAM_EPS = 1e-08
ADAM_WD = 0.01
ADAM_STEP = 10

LANES = 128
VMEM_LIMIT = 48 * 1024 * 1024
ATT_TILE = 512
ATT_Q_TILE = 512
ATT_KEY_TILE = 512
ROW_TILE = 256
MM_TILE_M = 1024
MM_TILE_N = 1024
MM_TILE_K = 2048
MM_VMEM_BUDGET = 40 * 1024 * 1024
GDN_HEADS_PER_STEP = 16


def _alpha():
    return (2.0 * DEPTH) ** 0.25


def _tile(n, pref, align=LANES):
    t = min(pref, n) // align * align
    while t >= align:
        if n % t == 0:
            return t
        t -= align
    return n


def _cp(sem=None):
    return pltpu.CompilerParams(dimension_semantics=sem, vmem_limit_bytes=VMEM_LIMIT)


def _iota(shape, dim):
    return lax.broadcasted_iota(jnp.int32, shape, dim)


def _div_pow2(x, p):
    assert p & (p - 1) == 0
    return lax.shift_right_logical(x, jnp.int32(p.bit_length() - 1))


def _dot(a, b, dims=((1,), (0,)), hi=False):
    if hi:
        return lax.dot_general(a.astype(f32), b.astype(f32), (dims, ((), ())), precision=HI, preferred_element_type=f32)
    return lax.dot_general(a.astype(MXU_DT), b.astype(MXU_DT), (dims, ((), ())), preferred_element_type=f32)


_NT = ((1,), (1,))
_TN = ((0,), (0,))


def _split3(x):
    hi = x.astype(jnp.bfloat16)
    r = x - hi.astype(f32)
    mid = r.astype(jnp.bfloat16)
    return hi, mid, (r - mid.astype(f32)).astype(jnp.bfloat16)


def _seldot_impl(a, b, dims, exact):
    def d(x, y):
        return lax.dot_general(x, y, (dims, ((), ())), preferred_element_type=f32)

    if exact == 0:
        a01 = a.astype(jnp.bfloat16)
        t = _split3(b.astype(f32))
        return (d(a01, t[0]) + d(a01, t[1])) + d(a01, t[2])
    b01 = b.astype(jnp.bfloat16)
    t = _split3(a.astype(f32))
    return (d(t[0], b01) + d(t[1], b01)) + d(t[2], b01)


@functools.partial(jax.custom_vjp, nondiff_argnums=(2, 3))
def _seldot(a, b, dims, exact):
    return _seldot_impl(a, b, dims, exact)


def _seldot_fwd(a, b, dims, exact):
    return _seldot_impl(a, b, dims, exact), (a, b)


def _seldot_bwd(dims, exact, res, dy):
    a, b = res
    (ca,), (cb,) = dims
    if exact == 0:
        assert ca == 1
        db = _seldot_impl(a, dy, _TN, 0) if cb == 0 else _seldot_impl(dy, a, _TN, 1)
        return jnp.zeros_like(a), db
    assert ca == 1 and cb == 0
    return _seldot_impl(dy, b, _NT, 1), jnp.zeros_like(b)


_seldot.defvjp(_seldot_fwd, _seldot_bwd)


def _softplus(x):
    return jnp.maximum(x, 0.0) + jnp.log1p(jnp.exp(-jnp.abs(x)))


def _silu(x):
    return x * jax.nn.sigmoid(x)


def _mm(a, b, *, name, ta=False, tb=False, add=None, add_scale=1.0, out_dtype=f32, comm=None):
    M, K = (a.shape[1], a.shape[0]) if ta else a.shape
    N = b.shape[0] if tb else b.shape[1]
    assert (b.shape[1] if tb else b.shape[0]) == K, (a.shape, b.shape, ta, tb)
    tm, tn, tk = _tile(M, MM_TILE_M), _tile(N, MM_TILE_N), _tile(K, MM_TILE_K)
    ab, bb = jnp.dtype(a.dtype).itemsize, jnp.dtype(b.dtype).itemsize
    while 2 * tk * (tm * ab + tn * bb) + 12 * tm * tn > MM_VMEM_BUDGET and tk % (2 * LANES) == 0:
        tk //= 2
    nk = K // tk
    a_spec = pl.BlockSpec((tk, tm), lambda i, j, k: (k, i)) if ta else pl.BlockSpec((tm, tk), lambda i, j, k: (i, k))
    b_spec = pl.BlockSpec((tn, tk), lambda i, j, k: (j, k)) if tb else pl.BlockSpec((tk, tn), lambda i, j, k: (k, j))
    o_spec = pl.BlockSpec((tm, tn), lambda i, j, k: (i, j))
    dims = ((0 if ta else 1,), (1 if tb else 0,))
    has_add = add is not None

    def body(*refs):
        a_ref, b_ref = refs[:2]
        add_ref = refs[2] if has_add else None
        o_ref = refs[3 if has_add else 2]

        def finish(r):
            if has_add:
                r = r + add_scale * add_ref[...].astype(f32)
            o_ref[...] = r.astype(out_dtype)

        if nk == 1:
            finish(_dot(a_ref[...], b_ref[...], dims))
            return
        acc = refs[-1]
        k = pl.program_id(2)

        @pl.when(k == 0)
        def _():
            acc[...] = jnp.zeros_like(acc)

        acc[...] += _dot(a_ref[...], b_ref[...], dims)

        @pl.when(k == nk - 1)
        def _():
            finish(acc[...])

    ins = [a, b] + ([add] if has_add else [])
    specs = [a_spec, b_spec] + ([o_spec] if has_add else [])
    (out,), cres = _call(
        body, comm, name=name, grid=(M // tm, N // tn, nk), in_specs=specs, out_specs=[o_spec],
        out_shape=[jax.ShapeDtypeStruct((M, N), out_dtype)], scratch_shapes=[pltpu.VMEM((tm, tn), f32)] if nk > 1 else [],
        sem=("parallel", "parallel", "arbitrary"), args=ins)
    return out if comm is None else (out, cres)


def _rw_specs(params, ins, ncol, tm):
    specs = []
    for arr, mode, bw, coff in params:
        if mode == "c":
            specs.append(pl.BlockSpec((1, bw), lambda c, r, coff=coff: (0, c + coff)))
        else:
            specs.append(pl.BlockSpec((1, bw), lambda c, r, coff=coff: (0, coff)))
    for arr, mode, bw, coff in ins:
        if mode == "c":
            specs.append(pl.BlockSpec((tm, bw), lambda c, r, coff=coff: (r, c + coff)))
        else:
            specs.append(pl.BlockSpec((tm, bw), lambda c, r, coff=coff: (r, coff)))
    return specs


def _norm_spec(lst):
    out = []
    for t in lst:
        arr, mode, bw = t[0], t[1], t[2]
        coff = t[3] if len(t) > 3 else 0
        out.append((arr, mode, bw, coff))
    return out


def _rowwise(fn, params, ins, outs, *, name, ncol=1, tm=None):
    params, ins = _norm_spec(params), _norm_spec(ins)
    S = ins[0][0].shape[0]
    tm = _tile(S, tm or ROW_TILE, 8)
    npar, nin = len(params), len(ins)

    def body(*refs):
        pv = [r[...].astype(f32) for r in refs[:npar]]
        iv = [r[...].astype(f32) for r in refs[npar:npar + nin]]
        res = fn(*pv, *iv)
        for o_ref, val in zip(refs[npar + nin:], res):
            o_ref[...] = val.astype(o_ref.dtype)

    out_specs, out_shapes = [], []
    for W, dt, mode, bw in outs:
        out_shapes.append(jax.ShapeDtypeStruct((S, W), dt))
        if mode == "c":
            out_specs.append(pl.BlockSpec((tm, bw), lambda c, r: (r, c)))
        else:
            out_specs.append(pl.BlockSpec((tm, bw), lambda c, r: (r, 0)))
    return pl.pallas_call(
        body, name=name, grid=(ncol, S // tm), in_specs=_rw_specs(params, ins, ncol, tm), out_specs=out_specs,
        out_shape=out_shapes, compiler_params=_cp(("parallel", "parallel")))(*[p[0] for p in params], *[i[0] for i in ins])


def _rowwise_bwd(fn, params, ins, couts, *, name, ncol=1, tm=None, diff_p=None, diff_i=None, din_dtypes=None, comm=None):
    params, ins, couts = _norm_spec(params), _norm_spec(ins), _norm_spec(couts)
    S = ins[0][0].shape[0]
    tm = _tile(S, tm or ROW_TILE, 8)
    npar, nin, nco = len(params), len(ins), len(couts)
    diff_p = list(range(npar)) if diff_p is None else diff_p
    diff_i = list(range(nin)) if diff_i is None else diff_i
    din_dtypes = [(f32,)] * len(diff_i) if din_dtypes is None else din_dtypes

    def body(*refs):
        c, r = pl.program_id(0), pl.program_id(1)
        pv = [x[...].astype(f32) for x in refs[:npar]]
        iv = [x[...].astype(f32) for x in refs[npar:npar + nin]]
        cv = [x[...].astype(f32) for x in refs[npar + nin:npar + nin + nco]]
        orefs = refs[npar + nin + nco:]

        def g(*dargs):
            p2, i2 = list(pv), list(iv)
            for n, k in enumerate(diff_p):
                p2[k] = dargs[n]
            for n, k in enumerate(diff_i):
                i2[k] = dargs[len(diff_p) + n]
            return tuple(fn(*p2, *i2))

        _, vjp = jax.vjp(g, *[pv[k] for k in diff_p], *[iv[k] for k in diff_i])
        grads = vjp(tuple(cv))
        for n, k in enumerate(diff_p):
            o_ref = orefs[n]
            first = (r == 0) if params[k][1] == "c" else jnp.logical_and(r == 0, c == 0)

            @pl.when(first)
            def _(o_ref=o_ref):
                o_ref[...] = jnp.zeros_like(o_ref)

            o_ref[...] += grads[n]
        pos = len(diff_p)
        for n, k in enumerate(diff_i):
            for _ in din_dtypes[n]:
                orefs[pos][...] = grads[len(diff_p) + n].astype(orefs[pos].dtype)
                pos += 1

    out_specs, out_shapes = [], []
    for k in diff_p:
        arr, mode, bw, coff = params[k]
        W = bw * ncol if mode == "c" else bw
        out_shapes.append(jax.ShapeDtypeStruct((1, W), f32))
        out_specs.append(pl.BlockSpec((1, bw), (lambda c, r: (0, c)) if mode == "c" else (lambda c, r: (0, 0))))
    for n, k in enumerate(diff_i):
        arr, mode, bw, coff = ins[k]
        W = bw * ncol if mode == "c" else bw
        for dt in din_dtypes[n]:
            out_shapes.append(jax.ShapeDtypeStruct((S, W), dt))
            out_specs.append(pl.BlockSpec((tm, bw), (lambda c, r: (r, c)) if mode == "c" else (lambda c, r: (r, 0))))
    res, cres = _call(
        body, comm, name=name, grid=(ncol, S // tm), in_specs=_rw_specs(params, ins + couts, ncol, tm), out_specs=out_specs,
        out_shape=out_shapes, scratch_shapes=[], sem=("arbitrary", "arbitrary"),
        args=(*[p[0] for p in params], *[i[0] for i in ins], *[c[0] for c in couts]))
    if comm is None:
        return list(res[:len(diff_p)]), list(res[len(diff_p):])
    return list(res[:len(diff_p)]), list(res[len(diff_p):]), cres


def _ln_fn(g, b, r):
    mu = jnp.mean(r, -1, keepdims=True)
    xc = r - mu
    var = jnp.mean(xc * xc, -1, keepdims=True)
    return (xc * lax.rsqrt(var + LN_EPS) * g + b,)


def _res_ln_fn(g, b, h, y):
    r = _alpha() * h + y
    hn = _ln_fn(g, b, r)
    return (r,) + hn + hn


def _rms_fn(w, x):
    return (x * lax.rsqrt(jnp.mean(x * x, -1, keepdims=True) + RMS_EPS) * w,)


def _ssd_gate_fn(w, y, z):
    yg = y * _silu(z)
    return (yg * lax.rsqrt(jnp.mean(yg * yg, -1, keepdims=True) + RMS_EPS) * w,)


def _mul_silu_fn(o, z):
    return (o * _silu(z),)


def _gdn_gate_fn(w, o, z):
    return (o * lax.rsqrt(jnp.mean(o * o, -1, keepdims=True) + RMS_EPS) * w * _silu(z),)


def _l2_fn(scale, x):
    return (x * lax.rsqrt(jnp.sum(x * x, -1, keepdims=True) + RMS_EPS) * scale,)


def _rope_fn(cos, sin, x):
    half = MLA_ROPE // 2
    i = _iota((LANES, LANES), 0)
    j = _iota((LANES, LANES), 1)
    pm = jnp.where((i == j + half) & (j < half), -1.0, 0.0) + jnp.where((i + half == j) & (j < 2 * half), 1.0, 0.0)
    return (x * cos + _seldot(x, pm.astype(f32), ((1,), (0,)), 1) * sin,)


def _conv_taps(x, K):
    S = x.shape[0]
    rows = _iota(x.shape, 0)
    return [x] + [jnp.where(rows < j, 0.0, pltpu.roll(x, j, 0)) for j in range(1, K)]


def _conv_fwd(x, w, b, *, name):
    S, C = x.shape
    K = w.shape[0]
    cw = _tile(C, LANES)

    def body(x_ref, w_ref, b_ref, o_ref):
        taps = _conv_taps(x_ref[...], K)
        wv = w_ref[...]
        pre = b_ref[...] + taps[0] * wv[K - 1:K, :]
        for j in range(1, K):
            pre = pre + taps[j] * wv[K - 1 - j:K - j, :]
        o_ref[...] = _silu(pre)

    return pl.pallas_call(
        body, name=name, grid=(C // cw,),
        in_specs=[pl.BlockSpec((S, cw), lambda c: (0, c)), pl.BlockSpec((K, cw), lambda c: (0, c)), pl.BlockSpec((1, cw), lambda c: (0, c))],
        out_specs=pl.BlockSpec((S, cw), lambda c: (0, c)), out_shape=jax.ShapeDtypeStruct((S, C), f32),
        compiler_params=_cp(("parallel",)))(x, w, b)


def _ssd_grouped_block(c, G, GW, N, cw):
    assert N == cw and GW % cw == 0
    nq = GW // cw
    per, nx = nq + 2, G * nq
    in_x = lax.div(c, nq) * per + lax.rem(c, nq)
    return jnp.where(c < nx, in_x, jnp.where(c < nx + G, (c - nx) * per + nq, (c - nx - G) * per + nq + 1))


def _conv_bwd(x, w, b, dys, *, name, dx_dtype=f32, dy_block=None):
    S, C = x.shape
    K = w.shape[0]
    cw = _tile(C, LANES)
    nblk = [d.shape[1] // cw for d in dys]
    offs = [sum(nblk[:p]) for p in range(len(dys))]
    assert sum(nblk) == C // cw and all(d.shape[1] % cw == 0 for d in dys)
    npc = len(dys)
    assert dy_block is None or npc == 1

    def body(x_ref, w_ref, b_ref, *refs):
        dy_refs, (dx_ref, dw_ref, db_ref) = refs[:npc], refs[npc:]
        if npc == 1:
            _conv_bwd_block(x_ref, w_ref, b_ref, dy_refs[0], dx_ref, dw_ref, db_ref, K, S, dx_dtype)
            return
        c = pl.program_id(0)
        for p in range(npc):
            @pl.when(jnp.logical_and(c >= offs[p], c < offs[p] + nblk[p]))
            def _(p=p):
                _conv_bwd_block(x_ref, w_ref, b_ref, dy_refs[p], dx_ref, dw_ref, db_ref, K, S, dx_dtype)

    col = lambda c: (0, c)
    if dy_block is not None:
        dy_specs = [pl.BlockSpec((S, cw), lambda c: (0, dy_block(c, cw)))]
    else:
        dy_specs = [pl.BlockSpec((S, cw), lambda c, o=offs[p], n=nblk[p]: (0, jnp.clip(c - o, 0, n - 1))) for p in range(npc)]
    return pl.pallas_call(
        body, name=name, grid=(C // cw,),
        in_specs=[pl.BlockSpec((S, cw), col), pl.BlockSpec((K, cw), col), pl.BlockSpec((1, cw), col)] + dy_specs,
        out_specs=[pl.BlockSpec((S, cw), col), pl.BlockSpec((K, cw), col), pl.BlockSpec((1, cw), col)],
        out_shape=[jax.ShapeDtypeStruct((S, C), dx_dtype), jax.ShapeDtypeStruct((K, C), f32), jax.ShapeDtypeStruct((1, C), f32)],
        compiler_params=_cp(("parallel",)))(x, w, b, *dys)


def _conv_bwd_block(x_ref, w_ref, b_ref, dy_ref, dx_ref, dw_ref, db_ref, K, S, dx_dtype):
    taps = _conv_taps(x_ref[...], K)
    wv = w_ref[...]
    pre = b_ref[...] + taps[0] * wv[K - 1:K, :]
    for j in range(1, K):
        pre = pre + taps[j] * wv[K - 1 - j:K - j, :]
    sg = jax.nn.sigmoid(pre)
    dpre = dy_ref[...] * sg * (1.0 + pre * (1.0 - sg))
    db_ref[...] = jnp.sum(dpre, axis=0, keepdims=True)
    rows = _iota(dpre.shape, 0)
    dx = dpre * wv[K - 1:K, :]
    dw_ref[K - 1:K, :] = jnp.sum(dpre * taps[0], axis=0, keepdims=True)
    for j in range(1, K):
        dw_ref[K - 1 - j:K - j, :] = jnp.sum(dpre * taps[j], axis=0, keepdims=True)
        up = jnp.where(rows >= S - j, 0.0, pltpu.roll(dpre, S - j, 0))
        dx = dx + up * wv[K - 1 - j:K - j, :]
    dx_ref[...] = dx.astype(dx_dtype)


def _ssd_chunk(prev, xs, Bm, Cm, dtr, dtb, alog, dsk, g, *, R, P):
    L, GW = xs.shape
    H = dtr.shape[1]
    tril = _iota((L, L), 0) >= _iota((L, L), 1)
    dt = _softplus(dtr + dtb)
    acs = _seldot(tril.astype(f32), dt * (-jnp.exp(alog)), ((1,), (0,)), 0)
    expand = (_iota((H, GW), 0) == g * R + _div_pow2(_iota((H, GW), 1), P)).astype(f32)
    dt_e = _seldot(dt, expand, ((1,), (0,)), 1)
    acs_e = _seldot(acs, expand, ((1,), (0,)), 1)
    d_e = jnp.sum(_seldot(jnp.broadcast_to(dsk, (8, H)), expand, ((1,), (0,)), 1), axis=0, keepdims=True) * 0.125
    last = jnp.sum(jnp.where(_iota((L, GW), 0) == L - 1, acs_e, 0.0), axis=0, keepdims=True)
    xdt = xs * dt_e
    cb = _dot(Cm, Bm, _NT)
    nsel = max(R, 8)
    sel = (_iota((nsel, H), 1) == g * R + _iota((nsel, H), 0)).astype(f32)
    acs_t = _seldot(sel, acs, _NT, 0)
    hp = LANES // P
    pieces = []
    for p in range(GW // LANES):
        xp = xdt[:, p * LANES:(p + 1) * LANES]
        acc = None
        for q in range(hp):
            r = p * hp + q
            col = jnp.sum(jnp.where(_iota((L, H), 1) == g * R + r, acs, 0.0), axis=1, keepdims=True)
            row = jnp.sum(jnp.where(_iota((nsel, L), 0) == r, acs_t, 0.0), axis=0, keepdims=True)
            dec = jnp.where(tril, jnp.exp(jnp.where(tril, col - row, 0.0)), 0.0)
            xm = jnp.where(_div_pow2(_iota((L, LANES), 1), P) == q, xp, 0.0)
            t = _dot(cb * dec, xm)
            acc = t if acc is None else acc + t
        pieces.append(acc)
    y_diag = pieces[0] if len(pieces) == 1 else jnp.concatenate(pieces, axis=1)
    st = _dot(Bm, xdt * jnp.exp(last - acs_e), _TN)
    y_off = _dot(Cm, prev) * jnp.exp(acs_e)
    new = prev * jnp.exp(last) + st
    return y_diag + y_off + xs * d_e, new


def _ssd_dims(xbc, dtr):
    S, CD = xbc.shape
    H = dtr.shape[1]
    G, N, P = SSD_N_GROUPS, SSD_D_STATE, SSD_HEAD_DIM
    DI = H * P
    R = H // G
    assert CD == DI + 2 * G * N and DI % N == 0
    return S, H, G, N, P, DI, R, R * P, SSD_CHUNK


def _ssd_scan_fwd(xbc, dtr, dtb, alog, dsk, *, name, comm=None):
    S, H, G, N, P, DI, R, GW, L = _ssd_dims(xbc, dtr)
    nc = S // L
    boff, coff = DI // N, DI // N + G

    def body(xs_ref, b_ref, c_ref, dtr_ref, dtb_ref, alog_ref, dsk_ref, y_ref, st_ref, state):
        c, g = pl.program_id(0), pl.program_id(1)

        @pl.when(c == 0)
        def _():
            state[g] = jnp.zeros((N, GW), f32)

        prev = state[g]
        st_ref[0, 0] = prev
        y, new = _ssd_chunk(prev, xs_ref[...], b_ref[...], c_ref[...], dtr_ref[...], dtb_ref[...], alog_ref[...],
                            dsk_ref[...], g, R=R, P=P)
        y_ref[...] = y
        state[g] = new

    par = pl.BlockSpec((1, H), lambda c, g: (0, 0))
    return _call(
        body, comm, name=name, grid=(nc, G),
        in_specs=[pl.BlockSpec((L, GW), lambda c, g: (c, g)), pl.BlockSpec((L, N), lambda c, g: (c, boff + g)),
                  pl.BlockSpec((L, N), lambda c, g: (c, coff + g)), pl.BlockSpec((L, H), lambda c, g: (c, 0)), par, par, par],
        out_specs=[pl.BlockSpec((L, GW), lambda c, g: (c, g)), pl.BlockSpec((1, 1, N, GW), lambda c, g: (c, g, 0, 0))],
        out_shape=[jax.ShapeDtypeStruct((S, DI), f32), jax.ShapeDtypeStruct((nc, G, N, GW), f32)],
        scratch_shapes=[pltpu.VMEM((G, N, GW), f32)],
        sem=("arbitrary", "arbitrary"), args=(xbc, xbc, xbc, dtr, dtb, alog, dsk))


def _ssd_scan_bwd(xbc, dtr, dtb, alog, dsk, states, dy, *, name, comm=None):
    S, H, G, N, P, DI, R, GW, L = _ssd_dims(xbc, dtr)
    nc = S // L
    boff, coff = DI // N, DI // N + G

    def body(xs_ref, b_ref, c_ref, dtr_ref, dtb_ref, alog_ref, dsk_ref, st_ref, dy_ref,
             dg_ref, ddtr_ref, ddtb_ref, dalog_ref, ddsk_ref, dstate):
        c, g = pl.program_id(0), pl.program_id(1)

        @pl.when(c == 0)
        def _():
            dstate[g] = jnp.zeros((N, GW), f32)

        @pl.when(jnp.logical_and(c == 0, g == 0))
        def _():
            ddtb_ref[...] = jnp.zeros_like(ddtb_ref)
            dalog_ref[...] = jnp.zeros_like(dalog_ref)
            ddsk_ref[...] = jnp.zeros_like(ddsk_ref)

        @pl.when(g == 0)
        def _():
            ddtr_ref[...] = jnp.zeros_like(ddtr_ref)

        fn = functools.partial(_ssd_chunk, g=g, R=R, P=P)
        _, vjp = jax.vjp(fn, st_ref[0, 0], xs_ref[...], b_ref[...], c_ref[...], dtr_ref[...], dtb_ref[...],
                         alog_ref[...], dsk_ref[...])
        dprev, dxs, dB, dC, ddtr, ddtb, dalog, ddsk = vjp((dy_ref[...], dstate[g]))
        dstate[g] = dprev
        dg_ref[:, :GW] = dxs
        dg_ref[:, GW:GW + N] = dB
        dg_ref[:, GW + N:] = dC
        ddtr_ref[...] += ddtr
        ddtb_ref[...] += ddtb
        dalog_ref[...] += dalog
        ddsk_ref[...] += ddsk

    rc = lambda c: nc - 1 - c
    par = pl.BlockSpec((1, H), lambda c, g: (0, 0))
    return _call(
        body, comm, name=name, grid=(nc, G),
        in_specs=[pl.BlockSpec((L, GW), lambda c, g: (rc(c), g)), pl.BlockSpec((L, N), lambda c, g: (rc(c), boff + g)),
                  pl.BlockSpec((L, N), lambda c, g: (rc(c), coff + g)), pl.BlockSpec((L, H), lambda c, g: (rc(c), 0)),
                  par, par, par, pl.BlockSpec((1, 1, N, GW), lambda c, g: (rc(c), g, 0, 0)),
                  pl.BlockSpec((L, GW), lambda c, g: (rc(c), g))],
        out_specs=[pl.BlockSpec((L, GW + 2 * N), lambda c, g: (rc(c), g)), pl.BlockSpec((L, H), lambda c, g: (rc(c), 0)), par, par, par],
        out_shape=[jax.ShapeDtypeStruct((S, G * (GW + 2 * N)), f32), jax.ShapeDtypeStruct((S, H), f32)] + [jax.ShapeDtypeStruct((1, H), f32)] * 3,
        scratch_shapes=[pltpu.VMEM((G, N, GW), f32)],
        sem=("arbitrary", "arbitrary"), args=(xbc, xbc, xbc, dtr, dtb, alog, dsk, states, dy))


def _dot3(a, b, dims=((1,), (0,))):
    def split(x):
        hi = x.astype(jnp.bfloat16)
        return hi, (x - hi.astype(f32)).astype(jnp.bfloat16)

    def d(x, y):
        return lax.dot_general(x, y, (dims, ((), ())), preferred_element_type=f32)

    ah, al = split(a)
    bh, bl = split(b)
    return d(ah, bh) + (d(ah, bl) + d(al, bh))


def _neumann_inverses(As):
    L = As[0].shape[0]
    eye = (_iota((L, L), 0) == _iota((L, L), 1)).astype(f32)
    X = [-A for A in As]
    P = [eye + x for x in X]
    n = 1
    while 2 * n < L:
        X = [_dot3(x, x) for x in X]
        P = [p + _dot3(p, x) for p, x in zip(P, X)]
        n *= 2
    return P


@jax.custom_vjp
def _unit_lower_solves(Ts, As, Rs):
    return tuple(_dot3(T, R) for T, R in zip(Ts, Rs))


def _uls_fwd(Ts, As, Rs):
    Xs = tuple(_dot3(T, R) for T, R in zip(Ts, Rs))
    return Xs, (Ts, Xs)


def _uls_bwd(res, dXs):
    Ts, Xs = res
    dRs = tuple(_dot3(T, dX, _TN) for T, dX in zip(Ts, dXs))
    dAs = tuple(-_dot3(dR, X, _NT) for dR, X in zip(dRs, Xs))
    return tuple(jnp.zeros_like(T) for T in Ts), dAs, dRs


_unit_lower_solves.defvjp(_uls_fwd, _uls_bwd)


def _gdn_step(states, qb, kb_, vb, br, ar, alog, dtb, h0, *, rep, inverses=None):
    HB = len(states)
    L = qb.shape[0]
    DK, DV = states[0].shape
    HV = br.shape[1]
    incl = _iota((L, L), 0) >= _iota((L, L), 1)
    strict = _iota((L, L), 0) > _iota((L, L), 1)
    lane = _iota((L, HV), 1)
    g_all = -jnp.exp(alog) * _softplus(ar + dtb)
    gcs = _seldot(incl.astype(f32), g_all, ((1,), (0,)), 0)
    beta_all = jax.nn.sigmoid(br)
    nsel = max(HV, 8)
    gcs_t = _seldot((_iota((nsel, HV), 0) == _iota((nsel, HV), 1)).astype(f32), gcs, _NT, 0)
    hs = range(HB)
    q = [qb[:, (hh // rep) * DK:(hh // rep + 1) * DK] for hh in hs]
    k = [kb_[:, (hh // rep) * DK:(hh // rep + 1) * DK] for hh in hs]
    v = [vb[:, hh * DV:(hh + 1) * DV] for hh in hs]
    gc = [jnp.sum(jnp.where(lane == h0 + hh, gcs, 0.0), axis=1, keepdims=True) for hh in hs]
    beta = [jnp.sum(jnp.where(lane == h0 + hh, beta_all, 0.0), axis=1, keepdims=True) for hh in hs]
    gc_row = [jnp.sum(jnp.where(_iota((nsel, L), 0) == h0 + hh, gcs_t, 0.0), axis=0, keepdims=True) for hh in hs]
    decay = [jnp.where(incl, jnp.exp(jnp.where(incl, gc[hh] - gc_row[hh], 0.0)), 0.0) for hh in hs]
    kbeta = [k[hh] * beta[hh] for hh in hs]
    a_mat = [jnp.where(strict, _dot(kbeta[hh], k[hh], _NT) * decay[hh], 0.0) for hh in hs]
    eg = [jnp.exp(gc[hh]) for hh in hs]
    rhs = tuple(jnp.concatenate([v[hh] * beta[hh], kbeta[hh] * eg[hh]], axis=1) for hh in hs)
    if inverses is None:
        made = tuple(_neumann_inverses(a_mat))
        sol = tuple(_dot3(T, R) for T, R in zip(made, rhs))
    else:
        sol = _unit_lower_solves(tuple(inverses), tuple(a_mat), rhs)
    qk = [jnp.where(incl, _dot(q[hh], k[hh], _NT) * decay[hh], 0.0) for hh in hs]
    g_last = [jnp.sum(jnp.where(_iota((L, 1), 0) == L - 1, gc[hh], 0.0), axis=0, keepdims=True) for hh in hs]
    v_new = [sol[hh][:, :DV] - _dot(sol[hh][:, DV:], states[hh]) for hh in hs]
    outs = [_dot(q[hh] * eg[hh], states[hh]) + _dot(qk[hh], v_new[hh]) for hh in hs]
    news = [states[hh] * jnp.exp(g_last[hh]) + _dot(k[hh] * jnp.exp(g_last[hh] - gc[hh]), v_new[hh], _TN) for hh in hs]
    o = outs[0] if HB == 1 else jnp.concatenate(outs, axis=1)
    return (o, tuple(news), made) if inverses is None else (o, tuple(news))


def _gdn_dims():
    HK, HV = GDN_N_QK_HEADS, GDN_N_V_HEADS
    rep = HV // HK
    HB = min(GDN_HEADS_PER_STEP, HV)
    assert HV % HB == 0 and HB % rep == 0
    return HK, HV, GDN_DK, GDN_DV, GDN_CHUNK, rep, HB


def _gdn_scan_fwd(qkn, qkv, br, ar, alog, dtb, *, name, comm=None):
    S = qkn.shape[0]
    HK, HV, DK, DV, L, rep, HB = _gdn_dims()
    nc = S // L
    QW = HB // rep * DK
    koff = HK * DK // QW
    voff = 2 * HK * DK // (HB * DV)

    def body(q_ref, k_ref, v_ref, br_ref, ar_ref, alog_ref, dtb_ref, o_ref, st_ref, inv_ref, state):
        c, hb = pl.program_id(0), pl.program_id(1)
        h0 = hb * HB

        @pl.when(c == 0)
        def _():
            for hh in range(HB):
                state[h0 + hh] = jnp.zeros((DK, DV), f32)

        prev = tuple(state[h0 + hh] for hh in range(HB))
        for hh in range(HB):
            st_ref[0, hh] = prev[hh]
        o, new, inv = _gdn_step(prev, q_ref[...], k_ref[...], v_ref[...], br_ref[...], ar_ref[...], alog_ref[...], dtb_ref[...],
                                h0, rep=rep)
        o_ref[...] = o
        for hh in range(HB):
            state[h0 + hh] = new[hh]
            inv_ref[0, hh] = inv[hh]

    par = pl.BlockSpec((1, HV), lambda c, h: (0, 0))
    return _call(
        body, comm, name=name, grid=(nc, HV // HB),
        in_specs=[pl.BlockSpec((L, QW), lambda c, h: (c, h)), pl.BlockSpec((L, QW), lambda c, h: (c, koff + h)),
                  pl.BlockSpec((L, HB * DV), lambda c, h: (c, voff + h)), pl.BlockSpec((L, HV), lambda c, h: (c, 0)),
                  pl.BlockSpec((L, HV), lambda c, h: (c, 0)), par, par],
        out_specs=[pl.BlockSpec((L, HB * DV), lambda c, h: (c, h)), pl.BlockSpec((1, HB, DK, DV), lambda c, h: (c, h, 0, 0)),
                   pl.BlockSpec((1, HB, L, L), lambda c, h: (c, h, 0, 0))],
        out_shape=[jax.ShapeDtypeStruct((S, HV * DV), f32), jax.ShapeDtypeStruct((nc, HV, DK, DV), f32),
                   jax.ShapeDtypeStruct((nc, HV, L, L), f32)],
        scratch_shapes=[pltpu.VMEM((HV, DK, DV), f32)],
        sem=("arbitrary", "arbitrary"), args=(qkn, qkn, qkv, br, ar, alog, dtb))


def _gdn_scan_bwd(qkn, qkv, br, ar, alog, dtb, states, inverses, do, *, name, comm=None):
    S = qkn.shape[0]
    HK, HV, DK, DV, L, rep, HB = _gdn_dims()
    nc = S // L
    QW = HB // rep * DK
    koff = HK * DK // QW
    voff = 2 * HK * DK // (HB * DV)

    def body(q_ref, k_ref, v_ref, br_ref, ar_ref, alog_ref, dtb_ref, st_ref, inv_ref, do_ref,
             dq_ref, dk_ref, dv_ref, dbr_ref, dar_ref, dalog_ref, ddtb_ref, dstate):
        c, hb = pl.program_id(0), pl.program_id(1)
        h0 = hb * HB

        @pl.when(c == 0)
        def _():
            for hh in range(HB):
                dstate[h0 + hh] = jnp.zeros((DK, DV), f32)

        @pl.when(jnp.logical_and(c == 0, hb == 0))
        def _():
            dalog_ref[...] = jnp.zeros_like(dalog_ref)
            ddtb_ref[...] = jnp.zeros_like(ddtb_ref)

        @pl.when(hb == 0)
        def _():
            dbr_ref[...] = jnp.zeros_like(dbr_ref)
            dar_ref[...] = jnp.zeros_like(dar_ref)

        fn = functools.partial(_gdn_step, h0=h0, rep=rep, inverses=tuple(inv_ref[0, hh] for hh in range(HB)))
        prev = tuple(st_ref[0, hh] for hh in range(HB))
        _, vjp = jax.vjp(fn, prev, q_ref[...], k_ref[...], v_ref[...], br_ref[...], ar_ref[...], alog_ref[...], dtb_ref[...])
        dprev, dq, dk, dv, dbr, dar, dalog, ddtb = vjp((do_ref[...], tuple(dstate[h0 + hh] for hh in range(HB))))
        for hh in range(HB):
            dstate[h0 + hh] = dprev[hh]
        dq_ref[...] = dq
        dk_ref[...] = dk
        dv_ref[...] = dv
        dbr_ref[...] += dbr
        dar_ref[...] += dar
        dalog_ref[...] += dalog
        ddtb_ref[...] += ddtb

    rc = lambda c: nc - 1 - c
    par = pl.BlockSpec((1, HV), lambda c, h: (0, 0))
    blk = lambda W: pl.BlockSpec((L, W), lambda c, h: (rc(c), h))
    return _call(
        body, comm, name=name, grid=(nc, HV // HB),
        in_specs=[pl.BlockSpec((L, QW), lambda c, h: (rc(c), h)), pl.BlockSpec((L, QW), lambda c, h: (rc(c), koff + h)),
                  pl.BlockSpec((L, HB * DV), lambda c, h: (rc(c), voff + h)), pl.BlockSpec((L, HV), lambda c, h: (rc(c), 0)),
                  pl.BlockSpec((L, HV), lambda c, h: (rc(c), 0)), par, par,
                  pl.BlockSpec((1, HB, DK, DV), lambda c, h: (rc(c), h, 0, 0)),
                  pl.BlockSpec((1, HB, L, L), lambda c, h: (rc(c), h, 0, 0)), blk(HB * DV)],
        out_specs=[blk(QW), blk(QW), blk(HB * DV), pl.BlockSpec((L, HV), lambda c, h: (rc(c), 0)),
                   pl.BlockSpec((L, HV), lambda c, h: (rc(c), 0)), par, par],
        out_shape=[jax.ShapeDtypeStruct((S, HK * DK), f32), jax.ShapeDtypeStruct((S, HK * DK), f32), jax.ShapeDtypeStruct((S, HV * DV), f32),
                   jax.ShapeDtypeStruct((S, HV), f32), jax.ShapeDtypeStruct((S, HV), f32),
                   jax.ShapeDtypeStruct((1, HV), f32), jax.ShapeDtypeStruct((1, HV), f32)],
        scratch_shapes=[pltpu.VMEM((HV, DK, DV), f32)],
        sem=("arbitrary", "arbitrary"), args=(qkn, qkn, qkv, br, ar, alog, dtb, states, inverses, do))


def _att_scale():
    return (MLA_NOPE + MLA_ROPE) ** -0.5


def _causal(s, i, j, t, tk=None):
    qpos = i * t + _iota(s.shape, 0)
    kpos = j * (t if tk is None else tk) + _iota(s.shape, 1)
    return kpos <= qpos


def _attn_fwd(qn, qr, kn, kr, v, *, name, comm=None):
    S, W = qn.shape
    H = W // LANES
    t = _tile(S, ATT_Q_TILE)
    tk = _tile(S, ATT_KEY_TILE)
    assert tk % t == 0
    scale = _att_scale()

    def body(qn_ref, qr_ref, kn_ref, kr_ref, v_ref, o_ref, lse_ref):
        i = pl.program_id(1)
        qc = jnp.concatenate([qn_ref[...], qr_ref[...]], axis=1)

        def step(j, carry, masked):
            m, l, acc = carry
            rows = pl.ds(pl.multiple_of(j * tk, tk), tk)
            s = _dot(qc, jnp.concatenate([kn_ref[rows, :], kr_ref[rows, :]], axis=1), _NT) * scale
            if masked:
                s = jnp.where(_causal(s, i, j, t, tk), s, -1e30)
            m_new = jnp.maximum(m, jnp.max(s, axis=1, keepdims=True))
            p = jnp.exp(s - m_new)
            a = jnp.exp(m - m_new)
            return m_new, a * l + jnp.sum(p, axis=1, keepdims=True), a * acc + _dot(p, v_ref[rows, :])

        nfull = lax.div(i * t, tk)
        carry = lax.fori_loop(0, nfull, functools.partial(step, masked=False),
                              (jnp.full((t, 1), -1e30, f32), jnp.zeros((t, 1), f32), jnp.zeros((t, LANES), f32)))
        m, l, acc = step(nfull, carry, True)
        o_ref[...] = acc / l
        lse_ref[...] = jnp.broadcast_to(m + jnp.log(l), (t, LANES))

    qb = pl.BlockSpec((t, LANES), lambda h, i: (i, h))
    kb = pl.BlockSpec((S, LANES), lambda h, i: (0, h))
    return _call(
        body, comm, name=name, grid=(H, S // t),
        in_specs=[qb, qb, kb, pl.BlockSpec((S, LANES), lambda h, i: (0, 0)), kb],
        out_specs=[qb, qb], out_shape=[jax.ShapeDtypeStruct((S, W), f32), jax.ShapeDtypeStruct((S, W), f32)],
        scratch_shapes=[], sem=("parallel", "arbitrary"), args=(qn, qr, kn, kr, v))


def _attn_bwd_dq(qn, qr, kn, kr, v, o, lse, do, *, name, comm=None):
    S, W = qn.shape
    H = W // LANES
    t = _tile(S, ATT_TILE)
    scale = _att_scale()

    def body(qn_ref, qr_ref, kn_ref, kr_ref, v_ref, o_ref, lse_ref, do_ref, dqn_ref, dqr_ref):
        i = pl.program_id(1)
        qc = jnp.concatenate([qn_ref[...], qr_ref[...]], axis=1)
        dov = do_ref[...]
        delta = jnp.sum(dov * o_ref[...], axis=1, keepdims=True)
        lsev = lse_ref[...][:, :1]

        def step(j, dq, masked):
            rows = pl.ds(pl.multiple_of(j * t, t), t)
            kc = jnp.concatenate([kn_ref[rows, :], kr_ref[rows, :]], axis=1)
            s = _dot(qc, kc, _NT) * scale
            p = jnp.exp(s - lsev)
            if masked:
                p = jnp.where(_causal(s, i, j, t), p, 0.0)
            ds = p * (_dot(dov, v_ref[rows, :], _NT) - delta)
            return dq + _dot(ds, kc)

        dq = lax.fori_loop(0, i, functools.partial(step, masked=False), jnp.zeros((t, 2 * LANES), f32))
        dq = step(i, dq, True)
        dq = dq * scale
        dqn_ref[...] = dq[:, :LANES].astype(MXU_DT)
        dqr_ref[...] = dq[:, LANES:]

    qb = pl.BlockSpec((t, LANES), lambda h, i: (i, h))
    kb = pl.BlockSpec((S, LANES), lambda h, i: (0, h))
    return _call(
        body, comm, name=name, grid=(H, S // t),
        in_specs=[qb, qb, kb, pl.BlockSpec((S, LANES), lambda h, i: (0, 0)), kb, qb, qb, qb],
        out_specs=[qb, qb], out_shape=[jax.ShapeDtypeStruct((S, W), MXU_DT), jax.ShapeDtypeStruct((S, W), f32)],
        scratch_shapes=[], sem=("parallel", "arbitrary"), args=(qn, qr, kn, kr, v, o, lse, do))


def _attn_bwd_dkv(qn, qr, kn, kr, v, o, lse, do, *, name, comm=None):
    S, W = qn.shape
    H = W // LANES
    t = _tile(S, ATT_TILE)
    nb = S // t
    scale = _att_scale()

    def body(qn_ref, qr_ref, kn_ref, kr_ref, v_ref, o_ref, lse_ref, do_ref, dkn_ref, dkr_ref, dv_ref):
        j, h = pl.program_id(0), pl.program_id(1)
        kc = jnp.concatenate([kn_ref[...], kr_ref[...]], axis=1)
        vv = v_ref[...]

        def step(i, carry, masked):
            dk, dv = carry
            rows = pl.ds(pl.multiple_of(i * t, t), t)
            qc = jnp.concatenate([qn_ref[rows, :], qr_ref[rows, :]], axis=1)
            dov = do_ref[rows, :]
            delta = jnp.sum(dov * o_ref[rows, :], axis=1, keepdims=True)
            s = _dot(qc, kc, _NT) * scale
            p = jnp.exp(s - lse_ref[rows, :][:, :1])
            if masked:
                p = jnp.where(_causal(s, i, j, t), p, 0.0)
            ds = p * (_dot(dov, vv, _NT) - delta)
            return dk + _dot(ds, qc, _TN), dv + _dot(p, dov, _TN)

        carry = step(j, (jnp.zeros((t, 2 * LANES), f32), jnp.zeros((t, LANES), f32)), True)
        dk, dv = lax.fori_loop(j + 1, nb, functools.partial(step, masked=False), carry)
        dk = dk * scale
        dkn_ref[...] = dk[:, :LANES].astype(MXU_DT)
        dv_ref[...] = dv.astype(MXU_DT)

        @pl.when(h == 0)
        def _():
            dkr_ref[...] = jnp.zeros_like(dkr_ref)

        dkr_ref[...] += dk[:, LANES:]

    full = pl.BlockSpec((S, LANES), lambda j, h: (0, h))
    kb = pl.BlockSpec((t, LANES), lambda j, h: (j, h))
    k0 = pl.BlockSpec((t, LANES), lambda j, h: (j, 0))
    return _call(
        body, comm, name=name, grid=(nb, H),
        in_specs=[full, full, kb, k0, kb, full, full, full],
        out_specs=[kb, k0, kb],
        out_shape=[jax.ShapeDtypeStruct((S, W), MXU_DT), jax.ShapeDtypeStruct((S, LANES), f32), jax.ShapeDtypeStruct((S, W), MXU_DT)],
        scratch_shapes=[], sem=("arbitrary", "arbitrary"), args=(qn, qr, kn, kr, v, o, lse, do))


def _loss_head(y, target, *, name):
    S, D = y.shape
    tm = _tile(S, ROW_TILE, 8)

    def body(y_ref, t_ref, loss_ref, dy_ref):
        @pl.when(pl.program_id(0) == 0)
        def _():
            loss_ref[...] = jnp.zeros_like(loss_ref)

        e = y_ref[...] - t_ref[...]
        dy_ref[...] = e / D
        part = 0.5 * jnp.sum(jnp.mean(e * e, axis=1, keepdims=True), axis=0, keepdims=True)
        loss_ref[...] += jnp.broadcast_to(part, loss_ref.shape)

    rb = pl.BlockSpec((tm, D), lambda r: (r, 0))
    return pl.pallas_call(
        body, name=name, grid=(S // tm,), in_specs=[rb, rb],
        out_specs=[pl.BlockSpec((1, LANES), lambda r: (0, 0)), rb],
        out_shape=[jax.ShapeDtypeStruct((1, LANES), f32), jax.ShapeDtypeStruct((S, D), f32)],
        compiler_params=_cp(("arbitrary",)))(y, target)


def _adamw(w, g, m, v, *, name):
    R, C = w.shape
    tm = _tile(R, max(8, (1 << 19) // max(C, 1) // 8 * 8), 8)

    def body(w_ref, g_ref, m_ref, v_ref, d_ref, nm_ref, nv_ref):
        gv = g_ref[...]
        nm = ADAM_B1 * m_ref[...] + (1.0 - ADAM_B1) * gv
        nv = ADAM_B2 * v_ref[...] + (1.0 - ADAM_B2) * (gv * gv)
        m_hat = nm / (1.0 - ADAM_B1 ** ADAM_STEP)
        v_hat = nv / (1.0 - ADAM_B2 ** ADAM_STEP)
        d_ref[...] = -ADAM_LR * (m_hat / (jnp.sqrt(v_hat) + ADAM_EPS) + ADAM_WD * w_ref[...])
        nm_ref[...] = nm
        nv_ref[...] = nv

    rb = pl.BlockSpec((tm, C), lambda r: (r, 0))
    sh = jax.ShapeDtypeStruct((R, C), f32)
    return pl.pallas_call(body, name=name, grid=(R // tm,), in_specs=[rb] * 4, out_specs=[rb] * 3, out_shape=[sh] * 3,
                          compiler_params=_cp(("parallel",)))(w, g, m, v)


def _me():
    return lax.axis_index("x"), lax.axis_index("y"), lax.axis_index("c")


def _other_chips(mx, my):
    return [(1 - mx, my), (mx, 1 - my), (1 - mx, 1 - my)]


_ANY = pl.BlockSpec(memory_space=pl.ANY)


class _GatherChips:
    def __init__(self, xs):
        self.arrays = list(xs)
        n = len(xs)
        for x in xs:
            assert x.shape[0] % 2 == 0
        self.halves = [x.shape[0] // 2 for x in xs]
        self.out_shapes = [jax.ShapeDtypeStruct((4,) + x.shape, x.dtype) for x in xs]
        self.scratch = [pltpu.SemaphoreType.DMA((n, 6)), pltpu.SemaphoreType.DMA((n, 6))]

    def _sends(self, x_refs, o_refs, send, recv):
        mx, my, mc = _me()
        me = 2 * mx + my
        out = []
        for t, hf in enumerate(self.halves):
            mine = pl.ds(mc * hf, hf)
            for j, (cx, cy) in enumerate(_other_chips(mx, my)):
                out.append(pltpu.make_async_remote_copy(x_refs[t].at[mine], o_refs[t].at[me, mine], send.at[t, j], recv.at[t, j],
                                                        device_id=(cx, cy, mc), device_id_type=MESH))
        return out

    def start(self, x_refs, o_refs, scr):
        for cp in self._sends(x_refs, o_refs, *scr):
            cp.start()

    def finish(self, x_refs, o_refs, scr):
        send, recv = scr
        mx, my, mc = _me()
        chips = _other_chips(mx, my)
        fwd = []
        for t, hf in enumerate(self.halves):
            mine = pl.ds(mc * hf, hf)
            for j, (cx, cy) in enumerate(chips):
                k = 2 * cx + cy
                pltpu.make_async_remote_copy(x_refs[t].at[mine], o_refs[t].at[k, mine], send.at[t, j], recv.at[t, j],
                                             device_id=(cx, cy, mc), device_id_type=MESH).wait_recv()
                cp = pltpu.make_async_remote_copy(o_refs[t].at[k, mine], o_refs[t].at[k, mine], send.at[t, 3 + j], recv.at[t, 3 + j],
                                                  device_id=(mx, my, 1 - mc), device_id_type=MESH)
                cp.start()
                fwd.append(cp)
        for t, hf in enumerate(self.halves):
            theirs = pl.ds((1 - mc) * hf, hf)
            for j, (cx, cy) in enumerate(chips):
                k = 2 * cx + cy
                pltpu.make_async_remote_copy(o_refs[t].at[k, theirs], o_refs[t].at[k, theirs], send.at[t, 3 + j], recv.at[t, 3 + j],
                                             device_id=(mx, my, 1 - mc), device_id_type=MESH).wait_recv()
        for cp in self._sends(x_refs, o_refs, send, recv) + fwd:
            cp.wait_send()


class _ScatterChips:
    def __init__(self, ps):
        self.arrays = list(ps)
        n = len(ps)
        self.out_shapes = [jax.ShapeDtypeStruct((3,) + p.shape[1:], p.dtype) for p in ps]
        self.scratch = [pltpu.SemaphoreType.DMA((n, 3)), pltpu.SemaphoreType.DMA((n, 3))]

    def _copies(self, p_refs, o_refs, send, recv):
        mx, my, mc = _me()
        return [pltpu.make_async_remote_copy(p_refs[t].at[2 * cx + cy], o_refs[t].at[j], send.at[t, j], recv.at[t, j],
                                             device_id=(cx, cy, mc), device_id_type=MESH)
                for t in range(len(self.arrays)) for j, (cx, cy) in enumerate(_other_chips(mx, my))]

    def start(self, p_refs, o_refs, scr):
        for cp in self._copies(p_refs, o_refs, *scr):
            cp.start()

    def finish(self, p_refs, o_refs, scr):
        for cp in self._copies(p_refs, o_refs, *scr):
            cp.wait()


def _run_comm(comm, *, name):
    n = len(comm.arrays)

    def body(*refs):
        ins, outs, scr = refs[:n], refs[n:2 * n], refs[2 * n:]
        comm.start(ins, outs, scr)
        comm.finish(ins, outs, scr)

    return pl.pallas_call(body, name=name, in_specs=[_ANY] * n, out_specs=[_ANY] * n, out_shape=comm.out_shapes,
                          scratch_shapes=comm.scratch, compiler_params=pltpu.CompilerParams(has_side_effects=True))(*comm.arrays)


def _call(body, comm, *, name, grid, in_specs, out_specs, out_shape, scratch_shapes, sem, args):
    if comm is None:
        res = pl.pallas_call(body, name=name, grid=grid, in_specs=in_specs, out_specs=out_specs, out_shape=out_shape,
                             scratch_shapes=scratch_shapes, compiler_params=_cp(sem))(*args)
        return list(res), None
    n_in, n_out, n_scr, nc = len(in_specs), len(out_specs), len(scratch_shapes), len(comm.arrays)

    def wrapped(*refs):
        ins, cins = refs[:n_in], refs[n_in:n_in + nc]
        outs, couts = refs[n_in + nc:n_in + nc + n_out], refs[n_in + nc + n_out:n_in + 2 * nc + n_out]
        scr, cscr = refs[n_in + 2 * nc + n_out:n_in + 2 * nc + n_out + n_scr], refs[n_in + 2 * nc + n_out + n_scr:]
        ids = [pl.program_id(d) for d in range(len(grid))]
        first = functools.reduce(jnp.logical_and, [i == 0 for i in ids])
        last = functools.reduce(jnp.logical_and, [i == g - 1 for i, g in zip(ids, grid)])

        @pl.when(first)
        def _():
            comm.start(cins, couts, cscr)

        body(*ins, *outs, *scr)

        @pl.when(last)
        def _():
            comm.finish(cins, couts, cscr)

    res = pl.pallas_call(
        wrapped, name=name, grid=grid, in_specs=list(in_specs) + [_ANY] * nc, out_specs=list(out_specs) + [_ANY] * nc,
        out_shape=list(out_shape) + comm.out_shapes, scratch_shapes=list(scratch_shapes) + comm.scratch,
        compiler_params=_cp(("arbitrary",) * len(grid)))(*args, *comm.arrays)
    return list(res[:n_out]), list(res[n_out:])


class _PairSend:
    def __init__(self, gs):
        self.arrays = list(gs)
        n = len(gs)
        self.halves = [g.shape[1] // 2 for g in gs]
        self.out_shapes = [jax.ShapeDtypeStruct((4, g.shape[1] // 2, g.shape[2]), g.dtype) for g in gs]
        self.scratch = [pltpu.SemaphoreType.DMA((n, 4)), pltpu.SemaphoreType.DMA((n, 4))]

    def _copies(self, g_refs, o_refs, send, recv):
        mx, my, mc = _me()
        return [pltpu.make_async_remote_copy(g_refs[t].at[k, pl.ds((1 - mc) * hf, hf)], o_refs[t].at[k], send.at[t, k], recv.at[t, k],
                                             device_id=(mx, my, 1 - mc), device_id_type=MESH)
                for t, hf in enumerate(self.halves) for k in range(4)]

    def start(self, g_refs, o_refs, scr):
        for cp in self._copies(g_refs, o_refs, *scr):
            cp.start()

    def finish(self, g_refs, o_refs, scr):
        for cp in self._copies(g_refs, o_refs, *scr):
            cp.wait()


def _pair_exchange_halves(fs, *, name):
    n = len(fs)

    def body(*refs):
        f_refs, o_refs = refs[:n], refs[n:2 * n]
        send, recv = refs[2 * n:]
        mx, my, mc = _me()
        cps = []
        for t in range(n):
            hf = f_refs[t].shape[0]
            mine = pl.ds(mc * hf, hf)
            cp = pltpu.make_async_remote_copy(f_refs[t], o_refs[t].at[mine], send.at[t], recv.at[t],
                                              device_id=(mx, my, 1 - mc), device_id_type=MESH)
            cp.start()
            cps.append(cp)
        for t in range(n):
            hf = f_refs[t].shape[0]
            theirs = pl.ds((1 - mc) * hf, hf)
            cps[t].wait_send()
            pltpu.make_async_remote_copy(f_refs[t], o_refs[t].at[theirs], send.at[t], recv.at[t],
                                         device_id=(mx, my, 1 - mc), device_id_type=MESH).wait_recv()

    return pl.pallas_call(
        body, name=name, in_specs=[_ANY] * n, out_specs=[_ANY] * n,
        out_shape=[jax.ShapeDtypeStruct((2 * f.shape[0], f.shape[1]), f.dtype) for f in fs],
        scratch_shapes=[pltpu.SemaphoreType.DMA((n,)), pltpu.SemaphoreType.DMA((n,))],
        compiler_params=pltpu.CompilerParams(has_side_effects=True))(*fs)


def _allgather_all(x, *, name):
    def body(x_ref, o_ref, send, recv, lsem):
        mx, my, mc = _me()
        me = 4 * mx + 2 * my + mc
        local = pltpu.make_async_copy(x_ref, o_ref.at[me], lsem)
        local.start()
        cps = []
        for j in range(1, 8):
            px, py, pc = mx ^ (j >> 2), my ^ ((j >> 1) & 1), mc ^ (j & 1)
            cp = pltpu.make_async_remote_copy(x_ref, o_ref.at[me], send.at[j - 1], recv.at[j - 1],
                                              device_id=(px, py, pc), device_id_type=MESH)
            cp.start()
            cps.append(cp)
        for j in range(1, 8):
            px, py, pc = mx ^ (j >> 2), my ^ ((j >> 1) & 1), mc ^ (j & 1)
            pltpu.make_async_remote_copy(x_ref, o_ref.at[4 * px + 2 * py + pc], send.at[j - 1], recv.at[j - 1],
                                         device_id=(px, py, pc), device_id_type=MESH).wait_recv()
        for cp in cps:
            cp.wait_send()
        local.wait()

    return pl.pallas_call(
        body, name=name, in_specs=[_ANY], out_specs=_ANY, out_shape=jax.ShapeDtypeStruct((8,) + x.shape, x.dtype),
        scratch_shapes=[pltpu.SemaphoreType.DMA((7,)), pltpu.SemaphoreType.DMA((7,)), pltpu.SemaphoreType.DMA],
        compiler_params=pltpu.CompilerParams(has_side_effects=True))(x)


def _add_half(g4, recv, mc, *, name):
    _, R, C = g4.shape
    hf = R // 2
    tm = _tile(hf, max(16, (1 << 19) // C // 16 * 16), 16)
    nb = hf // tm

    def body(mc_ref, g_ref, r_ref, o_ref, ob_ref):
        s = g_ref[...] + r_ref[...]
        o_ref[...] = s
        ob_ref[...] = s.astype(COMM_DT)

    ospec = pl.BlockSpec((1, tm, C), lambda k, i, mc_ref: (k, i, 0))
    return pl.pallas_call(
        body, name=name,
        grid_spec=pltpu.PrefetchScalarGridSpec(
            num_scalar_prefetch=1, grid=(4, nb),
            in_specs=[pl.BlockSpec((1, tm, C), lambda k, i, mc_ref: (k, mc_ref[0] * nb + i, 0)),
                      pl.BlockSpec((1, tm, C), lambda k, i, mc_ref: (k, i, 0))],
            out_specs=[ospec, ospec]),
        out_shape=[jax.ShapeDtypeStruct((4, hf, C), f32), jax.ShapeDtypeStruct((4, hf, C), COMM_DT)],
        compiler_params=_cp(("parallel", "parallel")))(mc, g4, recv)


def _sum_chips(p4, recv3, me, *, name):
    _, Rh, C = p4.shape
    tm = _tile(Rh, max(16, (1 << 19) // C // 16 * 16), 16)

    def body(me_ref, p_ref, r_ref, o_ref):
        o_ref[...] = ((p_ref[0] + r_ref[0].astype(f32)) + r_ref[1].astype(f32)) + r_ref[2].astype(f32)

    return pl.pallas_call(
        body, name=name,
        grid_spec=pltpu.PrefetchScalarGridSpec(
            num_scalar_prefetch=1, grid=(Rh // tm,),
            in_specs=[pl.BlockSpec((1, tm, C), lambda i, me_ref: (me_ref[0], i, 0)),
                      pl.BlockSpec((3, tm, C), lambda i, me_ref: (0, i, 0))],
            out_specs=pl.BlockSpec((tm, C), lambda i, me_ref: (i, 0))),
        out_shape=jax.ShapeDtypeStruct((Rh, C), f32),
        compiler_params=_cp(("parallel",)))(me, p4, recv3)


def _sum8(x8, *, name):
    _, R, C = x8.shape
    tm = _tile(R, 64, 8)

    def body(x_ref, o_ref):
        acc = x_ref[0]
        for k in range(1, 8):
            acc = acc + x_ref[k]
        o_ref[...] = acc

    return pl.pallas_call(body, name=name, grid=(R // tm,), in_specs=[pl.BlockSpec((8, tm, C), lambda i: (0, i, 0))],
                          out_specs=pl.BlockSpec((tm, C), lambda i: (i, 0)), out_shape=jax.ShapeDtypeStruct((R, C), f32),
                          compiler_params=_cp(("parallel",)))(x8)


def _ssd_layer_fwd(h, W, tag, plan, i):
    z = _mm(h, W["wz"], name=tag + "_z")
    early = plan.fwd_early_comm(i)
    if early is None:
        xp = _mm(h, W["wxbc"], name=tag + "_xbc")
    else:
        xp, eres = _mm(h, W["wxbc"], name=tag + "_xbc", comm=early)
        plan.fwd_early_done(i, eres)
    dtr = _mm(h, W["wdt"], name=tag + "_dt")
    xbc = _conv_fwd(xp, W["conv_w"], W["conv_b"], name=tag + "_conv")
    (y, states), cres = _ssd_scan_fwd(xbc, dtr, W["dt_bias"], W["a_log"], W["d"], name=tag + "_scan", comm=plan.fwd_comm(i))
    DI = y.shape[1]
    G = SSD_N_GROUPS
    gs = DI // G
    (yn,) = _rowwise(_ssd_gate_fn, [(W["norm_w"], "c", gs)], [(y, "c", gs), (z, "c", gs)], [(DI, MXU_DT, "c", gs)],
                     name=tag + "_gate", ncol=G, tm=512)
    out = _mm(yn, W["wout"], name=tag + "_out")
    return out, dict(h=h, z=z, xp=xp, dtr=dtr, xbc=xbc, states=states, y=y, yn=yn), cres


def _carried_rowwise_bwd(plan, i, *a, **kw):
    early = plan.bwd_early_comm(i)
    if early is None:
        return _rowwise_bwd(*a, **kw)
    dp, di, cres = _rowwise_bwd(*a, comm=early, **kw)
    plan.bwd_early_done(i, cres)
    return dp, di


def _ssd_layer_bwd(sv, W, dr, drb, tag, plan, i):
    h = sv["h"]
    DI = sv["y"].shape[1]
    G = SSD_N_GROUPS
    gs = DI // G
    gr = {}
    dyn = _mm(drb, W["wout"], tb=True, name=tag + "_dyn")
    gr["wout"] = _mm(sv["yn"], drb, ta=True, name=tag + "_dwout")
    plan.early_grad(i, "ssd_out_w", gr["wout"])
    (dnw,), (dy, dz) = _carried_rowwise_bwd(plan, i, _ssd_gate_fn, [(W["norm_w"], "c", gs)], [(sv["y"], "c", gs), (sv["z"], "c", gs)],
                                             [(dyn, "c", gs)], name=tag + "_dgate", ncol=G, tm=512, din_dtypes=[(f32,), (MXU_DT,)])
    gr["norm_w"] = dnw
    (dxbc, ddtr, gr["dt_bias"], gr["a_log"], gr["d"]), cres = _ssd_scan_bwd(
        sv["xbc"], sv["dtr"], W["dt_bias"], W["a_log"], W["d"], sv["states"], dy, name=tag + "_dscan", comm=plan.bwd_comm(i))
    plan.bwd_done(i, cres)
    dxp, gr["conv_w"], gr["conv_b"] = _conv_bwd(
        sv["xp"], W["conv_w"], W["conv_b"], [dxbc], name=tag + "_dconv", dx_dtype=MXU_DT,
        dy_block=lambda c, cw: _ssd_grouped_block(c, G, gs, SSD_D_STATE, cw))
    dh = _mm(dz, W["wz"], tb=True, add=dr, add_scale=_alpha(), name=tag + "_dh1")
    dh = _mm(dxp, W["wxbc"], tb=True, add=dh, name=tag + "_dh2")
    dh = _mm(ddtr, W["wdt"], tb=True, add=dh, name=tag + "_dh3")
    gr["wz"] = _mm(h, dz, ta=True, name=tag + "_dwz")
    gr["wxbc"] = _mm(h, dxp, ta=True, name=tag + "_dwxbc")
    gr["wdt"] = _mm(h, ddtr, ta=True, name=tag + "_dwdt")
    return dh, gr


def _mla_layer_fwd(h, W, cos, sin, tag, comm=None):
    QR, KR = W["wqc"].shape[1], W["wkvc"].shape[1]
    HW = W["wqn"].shape[1]
    H = HW // LANES
    qc = _mm(h, W["wqc"], name=tag + "_qc")
    kvc = _mm(h, W["wkvc"], name=tag + "_kvc")
    krp = _mm(h, W["wkr"], name=tag + "_krp")
    z = _mm(h, W["wz"], name=tag + "_z")
    (qcn,) = _rowwise(_rms_fn, [(W["q_norm"], "a", QR)], [(qc, "a", QR)], [(QR, MXU_DT, "a", QR)], name=tag + "_qnorm")
    (kvn,) = _rowwise(_rms_fn, [(W["kv_norm"], "a", KR)], [(kvc, "a", KR)], [(KR, MXU_DT, "a", KR)], name=tag + "_kvnorm")
    qn = _mm(qcn, W["wqn"], name=tag + "_qn", out_dtype=MXU_DT)
    qrp = _mm(qcn, W["wqr"], name=tag + "_qrp")
    kn = _mm(kvn, W["wkn"], name=tag + "_kn", out_dtype=MXU_DT)
    v = _mm(kvn, W["wv"], name=tag + "_v", out_dtype=MXU_DT)
    (qr,) = _rowwise(_rope_fn, [], [(cos, "a", LANES), (sin, "a", LANES), (qrp, "c", LANES)], [(HW, MXU_DT, "c", LANES)],
                     name=tag + "_qrope", ncol=H, tm=1024)
    (kr,) = _rowwise(_rope_fn, [], [(cos, "a", LANES), (sin, "a", LANES), (krp, "a", LANES)], [(LANES, MXU_DT, "a", LANES)],
                     name=tag + "_krope")
    (o, lse), cres = _attn_fwd(qn, qr, kn, kr, v, name=tag + "_attn", comm=comm)
    (og,) = _rowwise(_mul_silu_fn, [], [(o, "a", HW), (z, "a", HW)], [(HW, MXU_DT, "a", HW)], name=tag + "_ogate")
    out = _mm(og, W["wout"], name=tag + "_out")
    return out, dict(h=h, qc=qc, kvc=kvc, z=z, qcn=qcn, kvn=kvn, qn=qn, qr=qr, kn=kn, kr=kr, v=v, o=o, lse=lse, og=og), cres


def _mla_layer_bwd(sv, W, cos, sin, dr, drb, tag, plan, i):
    h = sv["h"]
    QR, KR = W["wqc"].shape[1], W["wkvc"].shape[1]
    HW = W["wqn"].shape[1]
    H = HW // LANES
    gr = {}
    dog = _mm(drb, W["wout"], tb=True, name=tag + "_dog")
    gr["wout"] = _mm(sv["og"], drb, ta=True, name=tag + "_dwout")
    _, (do, dz) = _rowwise_bwd(_mul_silu_fn, [], [(sv["o"], "a", HW), (sv["z"], "a", HW)], [(dog, "a", HW)], name=tag + "_dogate",
                               din_dtypes=[(f32,), (MXU_DT,)])
    att = (sv["qn"], sv["qr"], sv["kn"], sv["kr"], sv["v"], sv["o"], sv["lse"], do)
    (dqn, dqr), cres = _attn_bwd_dq(*att, name=tag + "_dq", comm=plan.bwd_early_comm(i))
    plan.bwd_early_done(i, cres)
    (dkn, dkr, dv), cres = _attn_bwd_dkv(*att, name=tag + "_dkv", comm=plan.bwd_comm(i))
    plan.bwd_done(i, cres)
    _, (dqrp,) = _rowwise_bwd(_rope_fn, [], [(cos, "a", LANES), (sin, "a", LANES), (dqr, "c", LANES)], [(dqr, "c", LANES)],
                              name=tag + "_dqrope", ncol=H, tm=1024, diff_i=[2], din_dtypes=[(MXU_DT,)])
    _, (dkrp,) = _rowwise_bwd(_rope_fn, [], [(cos, "a", LANES), (sin, "a", LANES), (dkr, "a", LANES)], [(dkr, "a", LANES)],
                              name=tag + "_dkrope", diff_i=[2], din_dtypes=[(MXU_DT,)])
    dqcn = _mm(dqn, W["wqn"], tb=True, name=tag + "_dqcn1")
    dqcn = _mm(dqrp, W["wqr"], tb=True, add=dqcn, name=tag + "_dqcn2")
    dkvn = _mm(dkn, W["wkn"], tb=True, name=tag + "_dkvn1")
    dkvn = _mm(dv, W["wv"], tb=True, add=dkvn, name=tag + "_dkvn2")
    gr["wqn"] = _mm(sv["qcn"], dqn, ta=True, name=tag + "_dwqn")
    gr["wqr"] = _mm(sv["qcn"], dqrp, ta=True, name=tag + "_dwqr")
    gr["wkn"] = _mm(sv["kvn"], dkn, ta=True, name=tag + "_dwkn")
    gr["wv"] = _mm(sv["kvn"], dv, ta=True, name=tag + "_dwv")
    (gr["q_norm"],), (dqc,) = _rowwise_bwd(_rms_fn, [(W["q_norm"], "a", QR)], [(sv["qc"], "a", QR)], [(dqcn, "a", QR)], name=tag + "_dqnorm",
                                           din_dtypes=[(MXU_DT,)])
    (gr["kv_norm"],), (dkvc,) = _rowwise_bwd(_rms_fn, [(W["kv_norm"], "a", KR)], [(sv["kvc"], "a", KR)], [(dkvn, "a", KR)], name=tag + "_dkvnorm",
                                             din_dtypes=[(MXU_DT,)])
    dh = _mm(dz, W["wz"], tb=True, add=dr, add_scale=_alpha(), name=tag + "_dh1")
    dh = _mm(dqc, W["wqc"], tb=True, add=dh, name=tag + "_dh2")
    dh = _mm(dkvc, W["wkvc"], tb=True, add=dh, name=tag + "_dh3")
    dh = _mm(dkrp, W["wkr"], tb=True, add=dh, name=tag + "_dh4")
    gr["wz"] = _mm(h, dz, ta=True, name=tag + "_dwz")
    gr["wqc"] = _mm(h, dqc, ta=True, name=tag + "_dwqc")
    gr["wkvc"] = _mm(h, dkvc, ta=True, name=tag + "_dwkvc")
    gr["wkr"] = _mm(h, dkrp, ta=True, name=tag + "_dwkr")
    return dh, gr


def _gdn_layer_fwd(h, W, tag, comm=None):
    HK, HV, DK, DV = GDN_N_QK_HEADS, GDN_N_V_HEADS, GDN_DK, GDN_DV
    KD, VD = HK * DK, HV * DV
    qkvp = _mm(h, W["wqkv"], name=tag + "_qkv")
    z = _mm(h, W["wz"], name=tag + "_z")
    br = _mm(h, W["wb"], name=tag + "_b")
    ar = _mm(h, W["wa"], name=tag + "_a")
    qkv = _conv_fwd(qkvp, W["conv_w"], jnp.zeros((1, qkvp.shape[1]), f32), name=tag + "_conv")
    scale = jnp.concatenate([jnp.full((1, KD), DK ** -0.5, f32), jnp.ones((1, KD), f32)], axis=1)
    (qkn,) = _rowwise(_l2_fn, [(scale, "c", DK)], [(qkv, "c", DK)], [(2 * KD, f32, "c", DK)], name=tag + "_l2", ncol=2 * HK, tm=2048)
    (o, states, inverses), cres = _gdn_scan_fwd(qkn, qkv, br, ar, W["a_log"], W["dt_bias"], name=tag + "_scan", comm=comm)
    (on,) = _rowwise(_gdn_gate_fn, [(W["norm_w"], "a", DV)], [(o, "c", DV), (z, "c", DV)], [(VD, MXU_DT, "c", DV)],
                     name=tag + "_gate", ncol=HV, tm=1024)
    out = _mm(on, W["wout"], name=tag + "_out")
    return out, dict(h=h, qkvp=qkvp, z=z, br=br, ar=ar, qkv=qkv, qkn=qkn, o=o, states=states, inverses=inverses, on=on, scale=scale), cres


def _gdn_layer_bwd(sv, W, dr, drb, tag, plan, i):
    h = sv["h"]
    HK, HV, DK, DV = GDN_N_QK_HEADS, GDN_N_V_HEADS, GDN_DK, GDN_DV
    KD, VD = HK * DK, HV * DV
    gr = {}
    don = _mm(drb, W["wout"], tb=True, name=tag + "_don")
    gr["wout"] = _mm(sv["on"], drb, ta=True, name=tag + "_dwout")
    (gr["norm_w"],), (do, dz) = _carried_rowwise_bwd(plan, i, _gdn_gate_fn, [(W["norm_w"], "a", DV)], [(sv["o"], "c", DV), (sv["z"], "c", DV)],
                                                      [(don, "c", DV)], name=tag + "_dgate", ncol=HV, tm=1024, din_dtypes=[(f32,), (MXU_DT,)])
    (dq, dk, dv, dbr, dar, gr["a_log"], gr["dt_bias"]), cres = _gdn_scan_bwd(
        sv["qkn"], sv["qkv"], sv["br"], sv["ar"], W["a_log"], W["dt_bias"], sv["states"], sv["inverses"], do, name=tag + "_dscan",
        comm=plan.bwd_comm(i))
    plan.bwd_done(i, cres)
    _, (dqq,) = _rowwise_bwd(_l2_fn, [(sv["scale"], "c", DK)], [(sv["qkv"], "c", DK)], [(dq, "c", DK)],
                             name=tag + "_dl2q", ncol=HK, tm=2048, diff_p=[])
    _, (dqk,) = _rowwise_bwd(_l2_fn, [(sv["scale"], "c", DK, HK)], [(sv["qkv"], "c", DK, HK)], [(dk, "c", DK)],
                             name=tag + "_dl2k", ncol=HK, tm=2048, diff_p=[])
    dqkvp, gr["conv_w"], _ = _conv_bwd(sv["qkvp"], W["conv_w"], jnp.zeros((1, sv["qkvp"].shape[1]), f32), [dqq, dqk, dv],
                                       name=tag + "_dconv", dx_dtype=MXU_DT)
    dh = _mm(dz, W["wz"], tb=True, add=dr, add_scale=_alpha(), name=tag + "_dh1")
    dh = _mm(dqkvp, W["wqkv"], tb=True, add=dh, name=tag + "_dh2")
    dh = _mm(dbr, W["wb"], tb=True, add=dh, name=tag + "_dh3")
    dh = _mm(dar, W["wa"], tb=True, add=dh, name=tag + "_dh4")
    gr["wz"] = _mm(h, dz, ta=True, name=tag + "_dwz")
    gr["wqkv"] = _mm(h, dqkvp, ta=True, name=tag + "_dwqkv")
    gr["wb"] = _mm(h, dbr, ta=True, name=tag + "_dwb")
    gr["wa"] = _mm(h, dar, ta=True, name=tag + "_dwa")
    return dh, gr


def _rope_tables(positions):
    half = MLA_ROPE // 2
    inv_freq = ROPE_THETA ** (-jnp.arange(0, MLA_ROPE, 2, dtype=f32) / MLA_ROPE)
    ang = positions.astype(f32)[:, None] * inv_freq
    cos, sin = jnp.cos(ang), jnp.sin(ang)
    S = positions.shape[0]
    pad = jnp.zeros((S, LANES - 2 * half), f32)
    return jnp.concatenate([cos, cos, pad + 1.0], axis=1), jnp.concatenate([sin, sin, pad], axis=1)


class _LocalPlan:
    def __init__(self, LW):
        self.LW, self.grads = LW, [None] * DEPTH

    def weights(self, i):
        return self.LW[i]

    def fwd_early_comm(self, i):
        return None

    def fwd_early_done(self, i, res):
        pass

    def fwd_comm(self, i):
        return None

    def fwd_done(self, i, res):
        pass

    def early_grad(self, i, name, g):
        pass

    def bwd_early_comm(self, i):
        return None

    def bwd_early_done(self, i, res):
        pass

    def bwd_comm(self, i):
        return None

    def bwd_done(self, i, res):
        pass

    def layer_grads(self, i, gr):
        self.grads[i] = gr


def _local_step(x, positions, target, ln_g, ln_b, plan):
    cos, sin = _rope_tables(positions)
    h, hb = x, x.astype(MXU_DT)
    saved, LW = [], []
    for i in range(DEPTH):
        kind, tag = i % 3, "l%d" % i
        LW.append(plan.weights(i))
        if kind == 0:
            y, sv, cres = _ssd_layer_fwd(hb, LW[i], tag, plan, i)
        elif kind == 1:
            y, sv, cres = _mla_layer_fwd(hb, LW[i], cos, sin, tag, plan.fwd_comm(i))
        else:
            y, sv, cres = _gdn_layer_fwd(hb, LW[i], tag, plan.fwd_comm(i))
        plan.fwd_done(i, cres)
        D = h.shape[1]
        r, h, hb = _rowwise(_res_ln_fn, [(ln_g[i], "a", D), (ln_b[i], "a", D)], [(h, "a", D), (y, "a", D)],
                            [(D, f32, "a", D), (D, f32, "a", D), (D, MXU_DT, "a", D)], name=tag + "_ln")
        sv["r"] = r
        saved.append(sv)
    loss, dh = _loss_head(h, target, name="loss_head")
    dg, db = [None] * DEPTH, [None] * DEPTH
    for i in reversed(range(DEPTH)):
        kind, tag = i % 3, "l%d" % i
        sv = saved[i]
        D = dh.shape[1]
        (dg[i], db[i]), (dr, drb) = _rowwise_bwd(_ln_fn, [(ln_g[i], "a", D), (ln_b[i], "a", D)], [(sv["r"], "a", D)], [(dh, "a", D)],
                                                 name=tag + "_dln", din_dtypes=[(f32, MXU_DT)])
        if kind == 0:
            dh, gr = _ssd_layer_bwd(sv, LW[i], dr, drb, tag, plan, i)
        elif kind == 1:
            dh, gr = _mla_layer_bwd(sv, LW[i], cos, sin, dr, drb, tag, plan, i)
        else:
            dh, gr = _gdn_layer_bwd(sv, LW[i], dr, drb, tag, plan, i)
        plan.layer_grads(i, gr)
    return loss, dh, dg, db


_WEIGHTS = ["ssd_in_w", "ssd_conv_w", "ssd_conv_b", "ssd_dt_bias", "ssd_a_log", "ssd_d", "ssd_norm_w", "ssd_out_w",
            "mla_in_w", "mla_q_norm_w", "mla_q_up_w", "mla_kv_norm_w", "mla_kv_up_w", "mla_out_w",
            "gdn_in_w", "gdn_conv_w", "gdn_a_log", "gdn_dt_bias", "gdn_norm_w", "gdn_out_w", "ln_g", "ln_b"]
_BIG = {"ssd_in_w": "col", "ssd_out_w": "row", "mla_in_w": "col", "mla_q_up_w": "col", "mla_kv_up_w": "col",
        "mla_out_w": "row", "gdn_in_w": "col", "gdn_out_w": "row"}
_SMALL_SHARDED = ["ssd_conv_w", "ssd_conv_b", "ssd_norm_w", "gdn_conv_w"]
_PACK_ROWS = 16


def _gathered_to_full(g, kind, nl):
    if kind == "col":
        _, RK, Ns = g.shape
        return g.reshape(4, nl, RK // nl, Ns).transpose(1, 2, 0, 3).reshape(nl, RK // nl, 4 * Ns)
    _, RK, N = g.shape
    return g.reshape(4, nl, RK // nl, N).transpose(1, 0, 2, 3).reshape(nl, 4 * (RK // nl), N)


def _full_to_slots(f, kind):
    nl, K, N = f.shape
    if kind == "col":
        return f.reshape(nl, K, 4, N // 4).transpose(2, 0, 1, 3).reshape(4, nl * K, N // 4)
    return f.reshape(nl, 4, K // 4, N).transpose(1, 0, 2, 3).reshape(4, nl * (K // 4), N)


def _pack(arrs):
    flat = jnp.concatenate([a.reshape(-1).astype(f32) for a in arrs])
    unit = _PACK_ROWS * LANES
    n = -(-flat.shape[0] // unit) * unit
    return jnp.pad(flat, (0, n - flat.shape[0])).reshape(_PACK_ROWS, n // _PACK_ROWS)


def _unpack(packed, shapes):
    flat = packed.reshape(-1)
    out, off = [], 0
    for sh in shapes:
        n = math.prod(sh)
        out.append(flat[off:off + n].reshape(sh))
        off += n
    return out


def _pad_lanes(a):
    return jnp.pad(a, [(0, 0)] * (a.ndim - 1) + [(0, LANES - a.shape[-1])])


_IN_PROJ = ("ssd_in_w", "mla_in_w", "gdn_in_w")


class _ColSlots:
    def __init__(self, slots):
        self.slots = slots
        self.shape = (slots.shape[1], 4 * slots.shape[2])

    def __getitem__(self, idx):
        _, cols = idx
        ns = self.slots.shape[2]
        a = cols.start or 0
        b = self.shape[1] if cols.stop is None else cols.stop
        parts = [self.slots[k][:, max(a, k * ns) - k * ns:min(b, (k + 1) * ns) - k * ns]
                 for k in range(4) if max(a, k * ns) < min(b, (k + 1) * ns)]
        return parts[0] if len(parts) == 1 else jnp.concatenate(parts, axis=1)


def _col_slots(pieces):
    widths = [p.shape[1] for p in pieces]
    ns = sum(widths) // 4
    slots = []
    for k in range(4):
        lo, hi, off, parts = k * ns, (k + 1) * ns, 0, []
        for p, wd in zip(pieces, widths):
            if max(lo, off) < min(hi, off + wd):
                parts.append(p[:, max(lo, off) - off:min(hi, off + wd) - off])
            off += wd
        slots.append(parts[0] if len(parts) == 1 else jnp.concatenate(parts, axis=1))
    return jnp.stack(slots)


def _layer_dict(i, full):
    G, N, P = SSD_N_GROUPS, SSD_D_STATE, SSD_HEAD_DIM
    kind, j = i % 3, i // 3
    if kind == 0:
        H = full["ssd_dt_bias"][j].shape[0]
        DI = H * P
        CD = DI + 2 * G * N
        win = full["ssd_in_w"][j]
        return dict(wz=win[:, :DI], wxbc=win[:, DI:DI + CD], wdt=win[:, DI + CD:], conv_w=full["ssd_conv_w"][j],
                    conv_b=full["ssd_conv_b"][j][None], dt_bias=full["ssd_dt_bias"][j][None], a_log=full["ssd_a_log"][j][None],
                    d=full["ssd_d"][j][None], norm_w=full["ssd_norm_w"][j][None], wout=full["ssd_out_w"][j])
    if kind == 1:
        QR, KR = MLA_Q_RANK, MLA_KV_RANK
        win = full["mla_in_w"][j]
        Hh = full["mla_q_up_w"][j].shape[1] // (MLA_NOPE + MLA_ROPE)
        qup = full["mla_q_up_w"][j].reshape(QR, Hh, MLA_NOPE + MLA_ROPE)
        kvup = full["mla_kv_up_w"][j].reshape(KR, Hh, MLA_NOPE + MLA_V)
        return dict(wqc=win[:, :QR], wkvc=win[:, QR:QR + KR], wkr=_pad_lanes(win[:, QR + KR:QR + KR + MLA_ROPE]),
                    wz=win[:, QR + KR + MLA_ROPE:], q_norm=full["mla_q_norm_w"][j][None], kv_norm=full["mla_kv_norm_w"][j][None],
                    wqn=qup[:, :, :MLA_NOPE].reshape(QR, Hh * MLA_NOPE), wqr=_pad_lanes(qup[:, :, MLA_NOPE:]).reshape(QR, Hh * LANES),
                    wkn=kvup[:, :, :MLA_NOPE].reshape(KR, Hh * MLA_NOPE), wv=kvup[:, :, MLA_NOPE:].reshape(KR, Hh * MLA_V),
                    wout=full["mla_out_w"][j])
    KD, VD, HV = GDN_N_QK_HEADS * GDN_DK, GDN_N_V_HEADS * GDN_DV, GDN_N_V_HEADS
    win = full["gdn_in_w"][j]
    c0, c1 = 2 * KD + VD, 2 * KD + 2 * VD
    return dict(wqkv=win[:, :c0], wz=win[:, c0:c1], wb=win[:, c1:c1 + HV], wa=win[:, c1 + HV:], conv_w=full["gdn_conv_w"][j],
                a_log=full["gdn_a_log"][j][None], dt_bias=full["gdn_dt_bias"][j][None], norm_w=full["gdn_norm_w"][j][None],
                wout=full["gdn_out_w"][j])


def _layer_weights(full, D):
    return [_layer_dict(i, full) for i in range(DEPTH)]


def _layer_full_grads(i, g, slots=False):
    kind = i % 3
    join = _col_slots if slots else (lambda pieces: jnp.concatenate(pieces, axis=1))
    if kind == 0:
        out = {"ssd_in_w": join([g["wz"], g["wxbc"], g["wdt"]]), "ssd_conv_w": g["conv_w"], "ssd_out_w": g["wout"]}
        for n in ("conv_b", "dt_bias", "a_log", "d", "norm_w"):
            out["ssd_" + n] = g[n][0]
        return out
    if kind == 1:
        QR, KR = g["wqn"].shape[0], g["wkn"].shape[0]
        Hh = g["wqn"].shape[1] // MLA_NOPE
        return {"mla_in_w": join([g["wqc"], g["wkvc"], g["wkr"][:, :MLA_ROPE], g["wz"]]),
                "mla_q_up_w": jnp.concatenate([g["wqn"].reshape(QR, Hh, MLA_NOPE), g["wqr"].reshape(QR, Hh, LANES)[:, :, :MLA_ROPE]],
                                              axis=2).reshape(QR, -1),
                "mla_kv_up_w": jnp.concatenate([g["wkn"].reshape(KR, Hh, MLA_NOPE), g["wv"].reshape(KR, Hh, MLA_V)], axis=2).reshape(KR, -1),
                "mla_q_norm_w": g["q_norm"][0], "mla_kv_norm_w": g["kv_norm"][0], "mla_out_w": g["wout"]}
    out = {"gdn_in_w": join([g["wqkv"], g["wz"], g["wb"], g["wa"]]), "gdn_conv_w": g["conv_w"], "gdn_out_w": g["wout"]}
    for n in ("a_log", "dt_bias", "norm_w"):
        out["gdn_" + n] = g[n][0]
    return out


def _full_grads(grads, dg, db):
    per = {n: [] for n in _WEIGHTS}
    for i in range(DEPTH):
        for n, a in _layer_full_grads(i, grads[i]).items():
            per[n].append(a)
        per["ln_g"].append(dg[i][0])
        per["ln_b"].append(db[i][0])
    return {n: jnp.stack(v) for n, v in per.items()}


class _DistPlan:
    def __init__(self, w, chip, core):
        self.w, self.chip = w, chip
        self.chip_arr = jnp.reshape(chip, (1,)).astype(jnp.int32)
        self.core_arr = jnp.reshape(core, (1,)).astype(jnp.int32)
        self.full = {n: {} for n in _BIG}
        self.gkeys, self.g4, self.p4, self.fin, self.small_grads = {}, {}, {}, {}, [None] * DEPTH
        keys = [("ssd_in_w", 0)]
        shards = self._shards(keys)
        got = _run_comm(_GatherChips(shards + [_pack([w[n] for n in _SMALL_SHARDED])]), name="gather_l0")
        self._fill(keys, shards, got[:1])
        small = lax.dynamic_update_slice(got[1], _pack([w[n] for n in _SMALL_SHARDED])[None], (chip, 0, 0))
        parts = [_unpack(small[k], [w[n].shape for n in _SMALL_SHARDED]) for k in range(4)]
        for t, n in enumerate(_SMALL_SHARDED):
            self.full[n] = jnp.concatenate([parts[k][t] for k in range(4)], axis=-1)
        for n in _WEIGHTS:
            if n not in self.full:
                self.full[n] = w[n]

    @staticmethod
    def keys(i):
        names = [["ssd_in_w", "ssd_out_w"], ["mla_in_w", "mla_q_up_w", "mla_kv_up_w", "mla_out_w"], ["gdn_in_w", "gdn_out_w"]][i % 3]
        return [(n, i // 3) for n in names]

    def _shards(self, keys):
        return [self.w[n][j].astype(MXU_DT) for n, j in keys]

    def _fill(self, keys, shards, got):
        for (n, j), s, g in zip(keys, shards, got):
            g = lax.dynamic_update_slice(g, s[None], (self.chip, 0, 0))
            self.full[n][j] = _ColSlots(g) if n in _IN_PROJ else _gathered_to_full(g, _BIG[n], 1)[0]

    def weights(self, i):
        if i == 0:
            self.full["ssd_out_w"][0] = None
        self._w = _layer_dict(i, self.full)
        return self._w

    def _start_gather(self, keys):
        self._pending = (keys, self._shards(keys))
        return _GatherChips(self._pending[1])

    def _end_gather(self, res):
        self._fill(self._pending[0], self._pending[1], res)

    def fwd_early_comm(self, i):
        return self._start_gather([("ssd_out_w", 0)]) if i == 0 else None

    def fwd_early_done(self, i, res):
        self._end_gather(res)
        self._w["wout"] = self.full["ssd_out_w"][0]

    def fwd_comm(self, i):
        return self._start_gather(self.keys(i + 1)) if i + 1 < DEPTH else None

    def fwd_done(self, i, res):
        if res is not None:
            self._end_gather(res)

    def _slots(self, n, g):
        return g if n in _IN_PROJ else _full_to_slots(g[None], _BIG[n])

    def early_grad(self, i, name, g):
        if i == 0:
            self.gkeys[1].append((name, 0))
            self.g4[1].append(self._slots(name, g))

    def layer_grads(self, i, gr):
        fg = _layer_full_grads(i, gr, slots=True)
        self.small_grads[i] = {n: a for n, a in fg.items() if n not in _BIG}
        self.gkeys[i] = [k for k in self.keys(i) if not (i == 0 and k[0] == "ssd_out_w")]
        self.g4[i] = [self._slots(n, fg[n]) for n, _ in self.gkeys[i]]
        if i == 0:
            self._pair_add(0, _run_comm(_PairSend(self.g4[0]), name="grad_pair_send_l0"))
            self._sum(0, _run_comm(self._scatter(0), name="grad_chip_scatter_l0"))

    def _pair_add(self, g, r1):
        self.p4[g] = [_add_half(a, b, self.core_arr, name="grad_pair_add_g%d_%s%d" % (g, n, j))
                      for a, b, (n, j) in zip(self.g4[g], r1, self.gkeys[g])]

    def bwd_early_comm(self, i):
        return _PairSend(self.g4[i + 1]) if i + 1 < DEPTH else None

    def bwd_early_done(self, i, res):
        if res is not None:
            self._pair_add(i + 1, res)

    def _scatter(self, g):
        return _ScatterChips([p[1] for p in self.p4[g]])

    def _sum(self, g, r2):
        self.fin[g] = [_sum_chips(p[0], b, self.chip_arr, name="grad_chip_sum_g%d_%s%d" % (g, n, j))
                       for p, b, (n, j) in zip(self.p4[g], r2, self.gkeys[g])]

    def bwd_comm(self, i):
        return self._scatter(i + 1) if i + 1 < DEPTH else None

    def bwd_done(self, i, res):
        if res is not None:
            self._sum(i + 1, res)

    def grad_shards(self, core):
        order = [(g, t) for g in range(DEPTH) for t in range(len(self.gkeys[g]))]
        fins = [self.fin[g][t] for g, t in order]
        got = _pair_exchange_halves(fins, name="grad_pair_share")
        got = [lax.dynamic_update_slice(a, f, (core * f.shape[0], 0)) for a, f in zip(got, fins)]
        per = {n: {} for n in _BIG}
        for (g, t), a in zip(order, got):
            n, j = self.gkeys[g][t]
            per[n][j] = a
        return {n: (v[0] if len(v) == 1 else jnp.concatenate([v[j] for j in sorted(v)], axis=0)) for n, v in per.items()}


def kernel(x, positions, ssd_in_w, ssd_conv_w, ssd_conv_b, ssd_dt_bias, ssd_a_log, ssd_d, ssd_norm_w, ssd_out_w, mla_in_w, mla_q_norm_w, mla_q_up_w, mla_kv_norm_w, mla_kv_up_w, mla_out_w, gdn_in_w, gdn_conv_w, gdn_a_log, gdn_dt_bias, gdn_norm_w, gdn_out_w, ln_g, ln_b, loss_target, m_ssd_in_w, m_ssd_conv_w, m_ssd_conv_b, m_ssd_dt_bias, m_ssd_a_log, m_ssd_d, m_ssd_norm_w, m_ssd_out_w, m_mla_in_w, m_mla_q_norm_w, m_mla_q_up_w, m_mla_kv_norm_w, m_mla_kv_up_w, m_mla_out_w, m_gdn_in_w, m_gdn_conv_w, m_gdn_a_log, m_gdn_dt_bias, m_gdn_norm_w, m_gdn_out_w, m_ln_g, m_ln_b, v_ssd_in_w, v_ssd_conv_w, v_ssd_conv_b, v_ssd_dt_bias, v_ssd_a_log, v_ssd_d, v_ssd_norm_w, v_ssd_out_w, v_mla_in_w, v_mla_q_norm_w, v_mla_q_up_w, v_mla_kv_norm_w, v_mla_kv_up_w, v_mla_out_w, v_gdn_in_w, v_gdn_conv_w, v_gdn_a_log, v_gdn_dt_bias, v_gdn_norm_w, v_gdn_out_w, v_ln_g, v_ln_b):
    args = dict(locals())
    w = {n: args[n] for n in _WEIGHTS}
    mom = {n: args["m_" + n] for n in _WEIGHTS}
    vel = {n: args["v_" + n] for n in _WEIGHTS}
    mx, my, mc = _me()
    chip = 2 * mx + my
    small = [n for n in _WEIGHTS if n not in _BIG]
    big = list(_BIG)

    plan = _DistPlan(w, chip, mc)
    loss, gx, dg, db = _local_step(x[0], positions[0], loss_target[0], [plan.full["ln_g"][i][None] for i in range(DEPTH)],
                                   [plan.full["ln_b"][i][None] for i in range(DEPTH)], plan)
    per = {n: [] for n in small}
    for i in range(DEPTH):
        for n, a in plan.small_grads[i].items():
            per[n].append(a)
        per["ln_g"].append(dg[i][0])
        per["ln_b"].append(db[i][0])
    fg = {n: jnp.stack(v) for n, v in per.items()}
    gsh = plan.grad_shards(mc)

    out_g, out_d, out_m, out_v = {}, {}, {}, {}
    for n in big:
        g = gsh[n]
        sh = w[n].shape
        to2 = lambda a: a.reshape(-1, sh[-1])
        d_, m_, v_ = _adamw(to2(w[n]), g, to2(mom[n]), to2(vel[n]), name="adamw_" + n)
        out_g[n], out_d[n], out_m[n], out_v[n] = g.reshape(sh), d_.reshape(sh), m_.reshape(sh), v_.reshape(sh)

    summed = _sum8(_allgather_all(_pack([fg[n] for n in small] + [loss[0, :1]]), name="gather_small"), name="sum_small")
    sg = _unpack(summed, [fg[n].shape for n in small] + [(1,)])
    loss_total = sg[-1][0]
    gs = {}
    for n, g in zip(small, sg[:-1]):
        if n in _SMALL_SHARDED:
            ws = w[n].shape[-1]
            g = lax.dynamic_slice_in_dim(g, chip * ws, ws, axis=g.ndim - 1)
        gs[n] = g
    shapes = [w[n].shape for n in small]
    d_, m_, v_ = _adamw(_pack([w[n] for n in small]), _pack([gs[n] for n in small]), _pack([mom[n] for n in small]),
                        _pack([vel[n] for n in small]), name="adamw_small")
    for n, a, b, c in zip(small, _unpack(d_, shapes), _unpack(m_, shapes), _unpack(v_, shapes)):
        out_g[n], out_d[n], out_m[n], out_v[n] = gs[n], a, b, c

    return (loss_total, gx[None], *[out_g[n] for n in _WEIGHTS], *[out_d[n] for n in _WEIGHTS],
            *[out_m[n] for n in _WEIGHTS], *[out_v[n] for n in _WEIGHTS])
```

```python
import functools
import math

import jax
import jax.numpy as jnp
from jax import lax
from jax.experimental import pallas as pl
from jax.experimental.pallas import tpu as pltpu

f32 = jnp.float32
HI = lax.Precision.HIGHEST
MXU_DT = jnp.bfloat16
COMM_DT = jnp.bfloat16
MESH = pl.DeviceIdType.MESH

DEPTH = 4
LN_EPS = 1e-5
RMS_EPS = 1e-6
SSD_HEAD_DIM = 64
SSD_N_GROUPS = 8
SSD_D_STATE = 128
SSD_CONV = 4
SSD_CHUNK = 128
MLA_Q_RANK = 768
MLA_KV_RANK = 512
MLA_NOPE = 128
MLA_ROPE = 64
MLA_V = 128
ROPE_THETA = 10000.0
GDN_N_QK_HEADS = 16
GDN_N_V_HEADS = 32
GDN_DK = 128
GDN_DV = 128
GDN_CONV = 4
GDN_CHUNK = 64
ADAM_LR = 0.001
ADAM_B1 = 0.9
ADAM_B2 = 0.999
ADAM_EPS = 1e-08
ADAM_WD = 0.01
ADAM_STEP = 10

LANES = 128
VMEM_LIMIT = 48 * 1024 * 1024
ATT_TILE = 512
ATT_Q_TILE = 512
ATT_KEY_TILE = 512
ROW_TILE = 256
MM_TILE_M = 1024
MM_TILE_N = 1024
MM_TILE_K = 2048
MM_VMEM_BUDGET = 40 * 1024 * 1024
GDN_HEADS_PER_STEP = 16


def _alpha():
    return (2.0 * DEPTH) ** 0.25


def _tile(n, pref, align=LANES):
    t = min(pref, n) // align * align
    while t >= align:
        if n % t == 0:
            return t
        t -= align
    return n


def _cp(sem=None):
    return pltpu.CompilerParams(dimension_semantics=sem, vmem_limit_bytes=VMEM_LIMIT)


def _iota(shape, dim):
    return lax.broadcasted_iota(jnp.int32, shape, dim)


def _div_pow2(x, p):
    assert p & (p - 1) == 0
    return lax.shift_right_logical(x, jnp.int32(p.bit_length() - 1))


def _dot(a, b, dims=((1,), (0,)), hi=False):
    if hi:
        return lax.dot_general(a.astype(f32), b.astype(f32), (dims, ((), ())), precision=HI, preferred_element_type=f32)
    return lax.dot_general(a.astype(MXU_DT), b.astype(MXU_DT), (dims, ((), ())), preferred_element_type=f32)


_NT = ((1,), (1,))
_TN = ((0,), (0,))


def _split3(x):
    hi = x.astype(jnp.bfloat16)
    r = x - hi.astype(f32)
    mid = r.astype(jnp.bfloat16)
    return hi, mid, (r - mid.astype(f32)).astype(jnp.bfloat16)


def _seldot_impl(a, b, dims, exact):
    def d(x, y):
        return lax.dot_general(x, y, (dims, ((), ())), preferred_element_type=f32)

    if exact == 0:
        a01 = a.astype(jnp.bfloat16)
        t = _split3(b.astype(f32))
        return (d(a01, t[0]) + d(a01, t[1])) + d(a01, t[2])
    b01 = b.astype(jnp.bfloat16)
    t = _split3(a.astype(f32))
    return (d(t[0], b01) + d(t[1], b01)) + d(t[2], b01)


@functools.partial(jax.custom_vjp, nondiff_argnums=(2, 3))
def _seldot(a, b, dims, exact):
    return _seldot_impl(a, b, dims, exact)


def _seldot_fwd(a, b, dims, exact):
    return _seldot_impl(a, b, dims, exact), (a, b)


def _seldot_bwd(dims, exact, res, dy):
    a, b = res
    (ca,), (cb,) = dims
    if exact == 0:
        assert ca == 1
        db = _seldot_impl(a, dy, _TN, 0) if cb == 0 else _seldot_impl(dy, a, _TN, 1)
        return jnp.zeros_like(a), db
    assert ca == 1 and cb == 0
    return _seldot_impl(dy, b, _NT, 1), jnp.zeros_like(b)


_seldot.defvjp(_seldot_fwd, _seldot_bwd)


def _softplus(x):
    return jnp.maximum(x, 0.0) + jnp.log1p(jnp.exp(-jnp.abs(x)))


def _silu(x):
    return x * jax.nn.sigmoid(x)


def _mm(a, b, *, name, ta=False, tb=False, add=None, add_scale=1.0, out_dtype=f32, comm=None):
    M, K = (a.shape[1], a.shape[0]) if ta else a.shape
    N = b.shape[0] if tb else b.shape[1]
    assert (b.shape[1] if tb else b.shape[0]) == K, (a.shape, b.shape, ta, tb)
    tm, tn, tk = _tile(M, MM_TILE_M), _tile(N, MM_TILE_N), _tile(K, MM_TILE_K)
    ab, bb = jnp.dtype(a.dtype).itemsize, jnp.dtype(b.dtype).itemsize
    while 2 * tk * (tm * ab + tn * bb) + 12 * tm * tn > MM_VMEM_BUDGET and tk % (2 * LANES) == 0:
        tk //= 2
    nk = K // tk
    a_spec = pl.BlockSpec((tk, tm), lambda i, j, k: (k, i)) if ta else pl.BlockSpec((tm, tk), lambda i, j, k: (i, k))
    b_spec = pl.BlockSpec((tn, tk), lambda i, j, k: (j, k)) if tb else pl.BlockSpec((tk, tn), lambda i, j, k: (k, j))
    o_spec = pl.BlockSpec((tm, tn), lambda i, j, k: (i, j))
    dims = ((0 if ta else 1,), (1 if tb else 0,))
    has_add = add is not None

    def body(*refs):
        a_ref, b_ref = refs[:2]
        add_ref = refs[2] if has_add else None
        o_ref = refs[3 if has_add else 2]

        def finish(r):
            if has_add:
                r = r + add_scale * add_ref[...].astype(f32)
            o_ref[...] = r.astype(out_dtype)

        if nk == 1:
            finish(_dot(a_ref[...], b_ref[...], dims))
            return
        acc = refs[-1]
        k = pl.program_id(2)

        @pl.when(k == 0)
        def _():
            acc[...] = jnp.zeros_like(acc)

        acc[...] += _dot(a_ref[...], b_ref[...], dims)

        @pl.when(k == nk - 1)
        def _():
            finish(acc[...])

    ins = [a, b] + ([add] if has_add else [])
    specs = [a_spec, b_spec] + ([o_spec] if has_add else [])
    (out,), cres = _call(
        body, comm, name=name, grid=(M // tm, N // tn, nk), in_specs=specs, out_specs=[o_spec],
        out_shape=[jax.ShapeDtypeStruct((M, N), out_dtype)], scratch_shapes=[pltpu.VMEM((tm, tn), f32)] if nk > 1 else [],
        sem=("parallel", "parallel", "arbitrary"), args=ins)
    return out if comm is None else (out, cres)


def _rw_specs(params, ins, ncol, tm):
    specs = []
    for arr, mode, bw, coff in params:
        if mode == "c":
            specs.append(pl.BlockSpec((1, bw), lambda c, r, coff=coff: (0, c + coff)))
        else:
            specs.append(pl.BlockSpec((1, bw), lambda c, r, coff=coff: (0, coff)))
    for arr, mode, bw, coff in ins:
        if mode == "c":
            specs.append(pl.BlockSpec((tm, bw), lambda c, r, coff=coff: (r, c + coff)))
        else:
            specs.append(pl.BlockSpec((tm, bw), lambda c, r, coff=coff: (r, coff)))
    return specs


def _norm_spec(lst):
    out = []
    for t in lst:
        arr, mode, bw = t[0], t[1], t[2]
        coff = t[3] if len(t) > 3 else 0
        out.append((arr, mode, bw, coff))
    return out


def _rowwise(fn, params, ins, outs, *, name, ncol=1, tm=None):
    params, ins = _norm_spec(params), _norm_spec(ins)
    S = ins[0][0].shape[0]
    tm = _tile(S, tm or ROW_TILE, 8)
    npar, nin = len(params), len(ins)

    def body(*refs):
        pv = [r[...].astype(f32) for r in refs[:npar]]
        iv = [r[...].astype(f32) for r in refs[npar:npar + nin]]
        res = fn(*pv, *iv)
        for o_ref, val in zip(refs[npar + nin:], res):
            o_ref[...] = val.astype(o_ref.dtype)

    out_specs, out_shapes = [], []
    for W, dt, mode, bw in outs:
        out_shapes.append(jax.ShapeDtypeStruct((S, W), dt))
        if mode == "c":
            out_specs.append(pl.BlockSpec((tm, bw), lambda c, r: (r, c)))
        else:
            out_specs.append(pl.BlockSpec((tm, bw), lambda c, r: (r, 0)))
    return pl.pallas_call(
        body, name=name, grid=(ncol, S // tm), in_specs=_rw_specs(params, ins, ncol, tm), out_specs=out_specs,
        out_shape=out_shapes, compiler_params=_cp(("parallel", "parallel")))(*[p[0] for p in params], *[i[0] for i in ins])


def _rowwise_bwd(fn, params, ins, couts, *, name, ncol=1, tm=None, diff_p=None, diff_i=None, din_dtypes=None, comm=None):
    params, ins, couts = _norm_spec(params), _norm_spec(ins), _norm_spec(couts)
    S = ins[0][0].shape[0]
    tm = _tile(S, tm or ROW_TILE, 8)
    npar, nin, nco = len(params), len(ins), len(couts)
    diff_p = list(range(npar)) if diff_p is None else diff_p
    diff_i = list(range(nin)) if diff_i is None else diff_i
    din_dtypes = [(f32,)] * len(diff_i) if din_dtypes is None else din_dtypes

    def body(*refs):
        c, r = pl.program_id(0), pl.program_id(1)
        pv = [x[...].astype(f32) for x in refs[:npar]]
        iv = [x[...].astype(f32) for x in refs[npar:npar + nin]]
        cv = [x[...].astype(f32) for x in refs[npar + nin:npar + nin + nco]]
        orefs = refs[npar + nin + nco:]

        def g(*dargs):
            p2, i2 = list(pv), list(iv)
            for n, k in enumerate(diff_p):
                p2[k] = dargs[n]
            for n, k in enumerate(diff_i):
                i2[k] = dargs[len(diff_p) + n]
            return tuple(fn(*p2, *i2))

        _, vjp = jax.vjp(g, *[pv[k] for k in diff_p], *[iv[k] for k in diff_i])
        grads = vjp(tuple(cv))
        for n, k in enumerate(diff_p):
            o_ref = orefs[n]
            first = (r == 0) if params[k][1] == "c" else jnp.logical_and(r == 0, c == 0)

            @pl.when(first)
            def _(o_ref=o_ref):
                o_ref[...] = jnp.zeros_like(o_ref)

            o_ref[...] += grads[n]
        pos = len(diff_p)
        for n, k in enumerate(diff_i):
            for _ in din_dtypes[n]:
                orefs[pos][...] = grads[len(diff_p) + n].astype(orefs[pos].dtype)
                pos += 1

    out_specs, out_shapes = [], []
    for k in diff_p:
        arr, mode, bw, coff = params[k]
        W = bw * ncol if mode == "c" else bw
        out_shapes.append(jax.ShapeDtypeStruct((1, W), f32))
        out_specs.append(pl.BlockSpec((1, bw), (lambda c, r: (0, c)) if mode == "c" else (lambda c, r: (0, 0))))
    for n, k in enumerate(diff_i):
        arr, mode, bw, coff = ins[k]
        W = bw * ncol if mode == "c" else bw
        for dt in din_dtypes[n]:
            out_shapes.append(jax.ShapeDtypeStruct((S, W), dt))
            out_specs.append(pl.BlockSpec((tm, bw), (lambda c, r: (r, c)) if mode == "c" else (lambda c, r: (r, 0))))
    res, cres = _call(
        body, comm, name=name, grid=(ncol, S // tm), in_specs=_rw_specs(params, ins + couts, ncol, tm), out_specs=out_specs,
        out_shape=out_shapes, scratch_shapes=[], sem=("arbitrary", "arbitrary"),
        args=(*[p[0] for p in params], *[i[0] for i in ins], *[c[0] for c in couts]))
    if comm is None:
        return list(res[:len(diff_p)]), list(res[len(diff_p):])
    return list(res[:len(diff_p)]), list(res[len(diff_p):]), cres


def _ln_fn(g, b, r):
    mu = jnp.mean(r, -1, keepdims=True)
    xc = r - mu
    var = jnp.mean(xc * xc, -1, keepdims=True)
    return (xc * lax.rsqrt(var + LN_EPS) * g + b,)


def _res_ln_fn(g, b, h, y):
    r = _alpha() * h + y
    hn = _ln_fn(g, b, r)
    return (r,) + hn + hn


def _rms_fn(w, x):
    return (x * lax.rsqrt(jnp.mean(x * x, -1, keepdims=True) + RMS_EPS) * w,)


def _ssd_gate_fn(w, y, z):
    yg = y * _silu(z)
    return (yg * lax.rsqrt(jnp.mean(yg * yg, -1, keepdims=True) + RMS_EPS) * w,)


def _mul_silu_fn(o, z):
    return (o * _silu(z),)


def _gdn_gate_fn(w, o, z):
    return (o * lax.rsqrt(jnp.mean(o * o, -1, keepdims=True) + RMS_EPS) * w * _silu(z),)


def _l2_fn(scale, x):
    return (x * lax.rsqrt(jnp.sum(x * x, -1, keepdims=True) + RMS_EPS) * scale,)


def _rope_fn(cos, sin, x):
    half = MLA_ROPE // 2
    i = _iota((LANES, LANES), 0)
    j = _iota((LANES, LANES), 1)
    pm = jnp.where((i == j + half) & (j < half), -1.0, 0.0) + jnp.where((i + half == j) & (j < 2 * half), 1.0, 0.0)
    return (x * cos + _seldot(x, pm.astype(f32), ((1,), (0,)), 1) * sin,)


def _conv_taps(x, K):
    S = x.shape[0]
    rows = _iota(x.shape, 0)
    return [x] + [jnp.where(rows < j, 0.0, pltpu.roll(x, j, 0)) for j in range(1, K)]


def _conv_fwd(x, w, b, *, name):
    S, C = x.shape
    K = w.shape[0]
    cw = _tile(C, LANES)

    def body(x_ref, w_ref, b_ref, o_ref):
        taps = _conv_taps(x_ref[...], K)
        wv = w_ref[...]
        pre = b_ref[...] + taps[0] * wv[K - 1:K, :]
        for j in range(1, K):
            pre = pre + taps[j] * wv[K - 1 - j:K - j, :]
        o_ref[...] = _silu(pre)

    return pl.pallas_call(
        body, name=name, grid=(C // cw,),
        in_specs=[pl.BlockSpec((S, cw), lambda c: (0, c)), pl.BlockSpec((K, cw), lambda c: (0, c)), pl.BlockSpec((1, cw), lambda c: (0, c))],
        out_specs=pl.BlockSpec((S, cw), lambda c: (0, c)), out_shape=jax.ShapeDtypeStruct((S, C), f32),
        compiler_params=_cp(("parallel",)))(x, w, b)


def _ssd_grouped_block(c, G, GW, N, cw):
    assert N == cw and GW % cw == 0
    nq = GW // cw
    per, nx = nq + 2, G * nq
    in_x = lax.div(c, nq) * per + lax.rem(c, nq)
    return jnp.where(c < nx, in_x, jnp.where(c < nx + G, (c - nx) * per + nq, (c - nx - G) * per + nq + 1))


def _conv_bwd(x, w, b, dys, *, name, dx_dtype=f32, dy_block=None):
    S, C = x.shape
    K = w.shape[0]
    cw = _tile(C, LANES)
    nblk = [d.shape[1] // cw for d in dys]
    offs = [sum(nblk[:p]) for p in range(len(dys))]
    assert sum(nblk) == C // cw and all(d.shape[1] % cw == 0 for d in dys)
    npc = len(dys)
    assert dy_block is None or npc == 1

    def body(x_ref, w_ref, b_ref, *refs):
        dy_refs, (dx_ref, dw_ref, db_ref) = refs[:npc], refs[npc:]
        if npc == 1:
            _conv_bwd_block(x_ref, w_ref, b_ref, dy_refs[0], dx_ref, dw_ref, db_ref, K, S, dx_dtype)
            return
        c = pl.program_id(0)
        for p in range(npc):
            @pl.when(jnp.logical_and(c >= offs[p], c < offs[p] + nblk[p]))
            def _(p=p):
                _conv_bwd_block(x_ref, w_ref, b_ref, dy_refs[p], dx_ref, dw_ref, db_ref, K, S, dx_dtype)

    col = lambda c: (0, c)
    if dy_block is not None:
        dy_specs = [pl.BlockSpec((S, cw), lambda c: (0, dy_block(c, cw)))]
    else:
        dy_specs = [pl.BlockSpec((S, cw), lambda c, o=offs[p], n=nblk[p]: (0, jnp.clip(c - o, 0, n - 1))) for p in range(npc)]
    return pl.pallas_call(
        body, name=name, grid=(C // cw,),
        in_specs=[pl.BlockSpec((S, cw), col), pl.BlockSpec((K, cw), col), pl.BlockSpec((1, cw), col)] + dy_specs,
        out_specs=[pl.BlockSpec((S, cw), col), pl.BlockSpec((K, cw), col), pl.BlockSpec((1, cw), col)],
        out_shape=[jax.ShapeDtypeStruct((S, C), dx_dtype), jax.ShapeDtypeStruct((K, C), f32), jax.ShapeDtypeStruct((1, C), f32)],
        compiler_params=_cp(("parallel",)))(x, w, b, *dys)


def _conv_bwd_block(x_ref, w_ref, b_ref, dy_ref, dx_ref, dw_ref, db_ref, K, S, dx_dtype):
    taps = _conv_taps(x_ref[...], K)
    wv = w_ref[...]
    pre = b_ref[...] + taps[0] * wv[K - 1:K, :]
    for j in range(1, K):
        pre = pre + taps[j] * wv[K - 1 - j:K - j, :]
    sg = jax.nn.sigmoid(pre)
    dpre = dy_ref[...] * sg * (1.0 + pre * (1.0 - sg))
    db_ref[...] = jnp.sum(dpre, axis=0, keepdims=True)
    rows = _iota(dpre.shape, 0)
    dx = dpre * wv[K - 1:K, :]
    dw_ref[K - 1:K, :] = jnp.sum(dpre * taps[0], axis=0, keepdims=True)
    for j in range(1, K):
        dw_ref[K - 1 - j:K - j, :] = jnp.sum(dpre * taps[j], axis=0, keepdims=True)
        up = jnp.where(rows >= S - j, 0.0, pltpu.roll(dpre, S - j, 0))
        dx = dx + up * wv[K - 1 - j:K - j, :]
    dx_ref[...] = dx.astype(dx_dtype)


def _ssd_chunk(prev, xs, Bm, Cm, dtr, dtb, alog, dsk, g, *, R, P):
    L, GW = xs.shape
    H = dtr.shape[1]
    tril = _iota((L, L), 0) >= _iota((L, L), 1)
    dt = _softplus(dtr + dtb)
    acs = _seldot(tril.astype(f32), dt * (-jnp.exp(alog)), ((1,), (0,)), 0)
    expand = (_iota((H, GW), 0) == g * R + _div_pow2(_iota((H, GW), 1), P)).astype(f32)
    dt_e = _seldot(dt, expand, ((1,), (0,)), 1)
    acs_e = _seldot(acs, expand, ((1,), (0,)), 1)
    d_e = jnp.sum(_seldot(jnp.broadcast_to(dsk, (8, H)), expand, ((1,), (0,)), 1), axis=0, keepdims=True) * 0.125
    last = jnp.sum(jnp.where(_iota((L, GW), 0) == L - 1, acs_e, 0.0), axis=0, keepdims=True)
    xdt = xs * dt_e
    cb = _dot(Cm, Bm, _NT)
    nsel = max(R, 8)
    sel = (_iota((nsel, H), 1) == g * R + _iota((nsel, H), 0)).astype(f32)
    acs_t = _seldot(sel, acs, _NT, 0)
    hp = LANES // P
    pieces = []
    for p in range(GW // LANES):
        xp = xdt[:, p * LANES:(p + 1) * LANES]
        acc = None
        for q in range(hp):
            r = p * hp + q
            col = jnp.sum(jnp.where(_iota((L, H), 1) == g * R + r, acs, 0.0), axis=1, keepdims=True)
            row = jnp.sum(jnp.where(_iota((nsel, L), 0) == r, acs_t, 0.0), axis=0, keepdims=True)
            dec = jnp.where(tril, jnp.exp(jnp.where(tril, col - row, 0.0)), 0.0)
            xm = jnp.where(_div_pow2(_iota((L, LANES), 1), P) == q, xp, 0.0)
            t = _dot(cb * dec, xm)
            acc = t if acc is None else acc + t
        pieces.append(acc)
    y_diag = pieces[0] if len(pieces) == 1 else jnp.concatenate(pieces, axis=1)
    st = _dot(Bm, xdt * jnp.exp(last - acs_e), _TN)
    y_off = _dot(Cm, prev) * jnp.exp(acs_e)
    new = prev * jnp.exp(last) + st
    return y_diag + y_off + xs * d_e, new


def _ssd_dims(xbc, dtr):
    S, CD = xbc.shape
    H = dtr.shape[1]
    G, N, P = SSD_N_GROUPS, SSD_D_STATE, SSD_HEAD_DIM
    DI = H * P
    R = H // G
    assert CD == DI + 2 * G * N and DI % N == 0
    return S, H, G, N, P, DI, R, R * P, SSD_CHUNK


def _ssd_scan_fwd(xbc, dtr, dtb, alog, dsk, *, name, comm=None):
    S, H, G, N, P, DI, R, GW, L = _ssd_dims(xbc, dtr)
    nc = S // L
    boff, coff = DI // N, DI // N + G

    def body(xs_ref, b_ref, c_ref, dtr_ref, dtb_ref, alog_ref, dsk_ref, y_ref, st_ref, state):
        c, g = pl.program_id(0), pl.program_id(1)

        @pl.when(c == 0)
        def _():
            state[g] = jnp.zeros((N, GW), f32)

        prev = state[g]
        st_ref[0, 0] = prev
        y, new = _ssd_chunk(prev, xs_ref[...], b_ref[...], c_ref[...], dtr_ref[...], dtb_ref[...], alog_ref[...],
                            dsk_ref[...], g, R=R, P=P)
        y_ref[...] = y
        state[g] = new

    par = pl.BlockSpec((1, H), lambda c, g: (0, 0))
    return _call(
        body, comm, name=name, grid=(nc, G),
        in_specs=[pl.BlockSpec((L, GW), lambda c, g: (c, g)), pl.BlockSpec((L, N), lambda c, g: (c, boff + g)),
                  pl.BlockSpec((L, N), lambda c, g: (c, coff + g)), pl.BlockSpec((L, H), lambda c, g: (c, 0)), par, par, par],
        out_specs=[pl.BlockSpec((L, GW), lambda c, g: (c, g)), pl.BlockSpec((1, 1, N, GW), lambda c, g: (c, g, 0, 0))],
        out_shape=[jax.ShapeDtypeStruct((S, DI), f32), jax.ShapeDtypeStruct((nc, G, N, GW), f32)],
        scratch_shapes=[pltpu.VMEM((G, N, GW), f32)],
        sem=("arbitrary", "arbitrary"), args=(xbc, xbc, xbc, dtr, dtb, alog, dsk))


def _ssd_scan_bwd(xbc, dtr, dtb, alog, dsk, states, dy, *, name, comm=None):
    S, H, G, N, P, DI, R, GW, L = _ssd_dims(xbc, dtr)
    nc = S // L
    boff, coff = DI // N, DI // N + G

    def body(xs_ref, b_ref, c_ref, dtr_ref, dtb_ref, alog_ref, dsk_ref, st_ref, dy_ref,
             dg_ref, ddtr_ref, ddtb_ref, dalog_ref, ddsk_ref, dstate):
        c, g = pl.program_id(0), pl.program_id(1)

        @pl.when(c == 0)
        def _():
            dstate[g] = jnp.zeros((N, GW), f32)

        @pl.when(jnp.logical_and(c == 0, g == 0))
        def _():
            ddtb_ref[...] = jnp.zeros_like(ddtb_ref)
            dalog_ref[...] = jnp.zeros_like(dalog_ref)
            ddsk_ref[...] = jnp.zeros_like(ddsk_ref)

        @pl.when(g == 0)
        def _():
            ddtr_ref[...] = jnp.zeros_like(ddtr_ref)

        fn = functools.partial(_ssd_chunk, g=g, R=R, P=P)
        _, vjp = jax.vjp(fn, st_ref[0, 0], xs_ref[...], b_ref[...], c_ref[...], dtr_ref[...], dtb_ref[...],
                         alog_ref[...], dsk_ref[...])
        dprev, dxs, dB, dC, ddtr, ddtb, dalog, ddsk = vjp((dy_ref[...], dstate[g]))
        dstate[g] = dprev
        dg_ref[:, :GW] = dxs
        dg_ref[:, GW:GW + N] = dB
        dg_ref[:, GW + N:] = dC
        ddtr_ref[...] += ddtr
        ddtb_ref[...] += ddtb
        dalog_ref[...] += dalog
        ddsk_ref[...] += ddsk

    rc = lambda c: nc - 1 - c
    par = pl.BlockSpec((1, H), lambda c, g: (0, 0))
    return _call(
        body, comm, name=name, grid=(nc, G),
        in_specs=[pl.BlockSpec((L, GW), lambda c, g: (rc(c), g)), pl.BlockSpec((L, N), lambda c, g: (rc(c), boff + g)),
                  pl.BlockSpec((L, N), lambda c, g: (rc(c), coff + g)), pl.BlockSpec((L, H), lambda c, g: (rc(c), 0)),
                  par, par, par, pl.BlockSpec((1, 1, N, GW), lambda c, g: (rc(c), g, 0, 0)),
                  pl.BlockSpec((L, GW), lambda c, g: (rc(c), g))],
        out_specs=[pl.BlockSpec((L, GW + 2 * N), lambda c, g: (rc(c), g)), pl.BlockSpec((L, H), lambda c, g: (rc(c), 0)), par, par, par],
        out_shape=[jax.ShapeDtypeStruct((S, G * (GW + 2 * N)), f32), jax.ShapeDtypeStruct((S, H), f32)] + [jax.ShapeDtypeStruct((1, H), f32)] * 3,
        scratch_shapes=[pltpu.VMEM((G, N, GW), f32)],
        sem=("arbitrary", "arbitrary"), args=(xbc, xbc, xbc, dtr, dtb, alog, dsk, states, dy))


def _dot3(a, b, dims=((1,), (0,))):
    def split(x):
        hi = x.astype(jnp.bfloat16)
        return hi, (x - hi.astype(f32)).astype(jnp.bfloat16)

    def d(x, y):
        return lax.dot_general(x, y, (dims, ((), ())), preferred_element_type=f32)

    ah, al = split(a)
    bh, bl = split(b)
    return d(ah, bh) + (d(ah, bl) + d(al, bh))


def _neumann_inverses(As):
    L = As[0].shape[0]
    eye = (_iota((L, L), 0) == _iota((L, L), 1)).astype(f32)
    X = [-A for A in As]
    P = [eye + x for x in X]
    n = 1
    while 2 * n < L:
        X = [_dot3(x, x) for x in X]
        P = [p + _dot3(p, x) for p, x in zip(P, X)]
        n *= 2
    return P


@jax.custom_vjp
def _unit_lower_solves(Ts, As, Rs):
    return tuple(_dot3(T, R) for T, R in zip(Ts, Rs))


def _uls_fwd(Ts, As, Rs):
    Xs = tuple(_dot3(T, R) for T, R in zip(Ts, Rs))
    return Xs, (Ts, Xs)


def _uls_bwd(res, dXs):
    Ts, Xs = res
    dRs = tuple(_dot3(T, dX, _TN) for T, dX in zip(Ts, dXs))
    dAs = tuple(-_dot3(dR, X, _NT) for dR, X in zip(dRs, Xs))
    return tuple(jnp.zeros_like(T) for T in Ts), dAs, dRs


_unit_lower_solves.defvjp(_uls_fwd, _uls_bwd)


def _gdn_step(states, qb, kb_, vb, br, ar, alog, dtb, h0, *, rep, inverses=None):
    HB = len(states)
    L = qb.shape[0]
    DK, DV = states[0].shape
    HV = br.shape[1]
    incl = _iota((L, L), 0) >= _iota((L, L), 1)
    strict = _iota((L, L), 0) > _iota((L, L), 1)
    lane = _iota((L, HV), 1)
    g_all = -jnp.exp(alog) * _softplus(ar + dtb)
    gcs = _seldot(incl.astype(f32), g_all, ((1,), (0,)), 0)
    beta_all = jax.nn.sigmoid(br)
    nsel = max(HV, 8)
    gcs_t = _seldot((_iota((nsel, HV), 0) == _iota((nsel, HV), 1)).astype(f32), gcs, _NT, 0)
    hs = range(HB)
    q = [qb[:, (hh // rep) * DK:(hh // rep + 1) * DK] for hh in hs]
    k = [kb_[:, (hh // rep) * DK:(hh // rep + 1) * DK] for hh in hs]
    v = [vb[:, hh * DV:(hh + 1) * DV] for hh in hs]
    gc = [jnp.sum(jnp.where(lane == h0 + hh, gcs, 0.0), axis=1, keepdims=True) for hh in hs]
    beta = [jnp.sum(jnp.where(lane == h0 + hh, beta_all, 0.0), axis=1, keepdims=True) for hh in hs]
    gc_row = [jnp.sum(jnp.where(_iota((nsel, L), 0) == h0 + hh, gcs_t, 0.0), axis=0, keepdims=True) for hh in hs]
    decay = [jnp.where(incl, jnp.exp(jnp.where(incl, gc[hh] - gc_row[hh], 0.0)), 0.0) for hh in hs]
    kbeta = [k[hh] * beta[hh] for hh in hs]
    a_mat = [jnp.where(strict, _dot(kbeta[hh], k[hh], _NT) * decay[hh], 0.0) for hh in hs]
    eg = [jnp.exp(gc[hh]) for hh in hs]
    rhs = tuple(jnp.concatenate([v[hh] * beta[hh], kbeta[hh] * eg[hh]], axis=1) for hh in hs)
    if inverses is None:
        made = tuple(_neumann_inverses(a_mat))
        sol = tuple(_dot3(T, R) for T, R in zip(made, rhs))
    else:
        sol = _unit_lower_solves(tuple(inverses), tuple(a_mat), rhs)
    qk = [jnp.where(incl, _dot(q[hh], k[hh], _NT) * decay[hh], 0.0) for hh in hs]
    g_last = [jnp.sum(jnp.where(_iota((L, 1), 0) == L - 1, gc[hh], 0.0), axis=0, keepdims=True) for hh in hs]
    v_new = [sol[hh][:, :DV] - _dot(sol[hh][:, DV:], states[hh]) for hh in hs]
    outs = [_dot(q[hh] * eg[hh], states[hh]) + _dot(qk[hh], v_new[hh]) for hh in hs]
    news = [states[hh] * jnp.exp(g_last[hh]) + _dot(k[hh] * jnp.exp(g_last[hh] - gc[hh]), v_new[hh], _TN) for hh in hs]
    o = outs[0] if HB == 1 else jnp.concatenate(outs, axis=1)
    return (o, tuple(news), made) if inverses is None else (o, tuple(news))


def _gdn_dims():
    HK, HV = GDN_N_QK_HEADS, GDN_N_V_HEADS
    rep = HV // HK
    HB = min(GDN_HEADS_PER_STEP, HV)
    assert HV % HB == 0 and HB % rep == 0
    return HK, HV, GDN_DK, GDN_DV, GDN_CHUNK, rep, HB


def _gdn_scan_fwd(qkn, qkv, br, ar, alog, dtb, *, name, comm=None):
    S = qkn.shape[0]
    HK, HV, DK, DV, L, rep, HB = _gdn_dims()
    nc = S // L
    QW = HB // rep * DK
    koff = HK * DK // QW
    voff = 2 * HK * DK // (HB * DV)

    def body(q_ref, k_ref, v_ref, br_ref, ar_ref, alog_ref, dtb_ref, o_ref, st_ref, inv_ref, state):
        c, hb = pl.program_id(0), pl.program_id(1)
        h0 = hb * HB

        @pl.when(c == 0)
        def _():
            for hh in range(HB):
                state[h0 + hh] = jnp.zeros((DK, DV), f32)

        prev = tuple(state[h0 + hh] for hh in range(HB))
        for hh in range(HB):
            st_ref[0, hh] = prev[hh]
        o, new, inv = _gdn_step(prev, q_ref[...], k_ref[...], v_ref[...], br_ref[...], ar_ref[...], alog_ref[...], dtb_ref[...],
                                h0, rep=rep)
        o_ref[...] = o
        for hh in range(HB):
            state[h0 + hh] = new[hh]
            inv_ref[0, hh] = inv[hh]

    par = pl.BlockSpec((1, HV), lambda c, h: (0, 0))
    return _call(
        body, comm, name=name, grid=(nc, HV // HB),
        in_specs=[pl.BlockSpec((L, QW), lambda c, h: (c, h)), pl.BlockSpec((L, QW), lambda c, h: (c, koff + h)),
                  pl.BlockSpec((L, HB * DV), lambda c, h: (c, voff + h)), pl.BlockSpec((L, HV), lambda c, h: (c, 0)),
                  pl.BlockSpec((L, HV), lambda c, h: (c, 0)), par, par],
        out_specs=[pl.BlockSpec((L, HB * DV), lambda c, h: (c, h)), pl.BlockSpec((1, HB, DK, DV), lambda c, h: (c, h, 0, 0)),
                   pl.BlockSpec((1, HB, L, L), lambda c, h: (c, h, 0, 0))],
        out_shape=[jax.ShapeDtypeStruct((S, HV * DV), f32), jax.ShapeDtypeStruct((nc, HV, DK, DV), f32),
                   jax.ShapeDtypeStruct((nc, HV, L, L), f32)],
        scratch_shapes=[pltpu.VMEM((HV, DK, DV), f32)],
        sem=("arbitrary", "arbitrary"), args=(qkn, qkn, qkv, br, ar, alog, dtb))


def _gdn_scan_bwd(qkn, qkv, br, ar, alog, dtb, states, inverses, do, *, name, comm=None):
    S = qkn.shape[0]
    HK, HV, DK, DV, L, rep, HB = _gdn_dims()
    nc = S // L
    QW = HB // rep * DK
    koff = HK * DK // QW
    voff = 2 * HK * DK // (HB * DV)

    def body(q_ref, k_ref, v_ref, br_ref, ar_ref, alog_ref, dtb_ref, st_ref, inv_ref, do_ref,
             dq_ref, dk_ref, dv_ref, dbr_ref, dar_ref, dalog_ref, ddtb_ref, dstate):
        c, hb = pl.program_id(0), pl.program_id(1)
        h0 = hb * HB

        @pl.when(c == 0)
        def _():
            for hh in range(HB):
                dstate[h0 + hh] = jnp.zeros((DK, DV), f32)

        @pl.when(jnp.logical_and(c == 0, hb == 0))
        def _():
            dalog_ref[...] = jnp.zeros_like(dalog_ref)
            ddtb_ref[...] = jnp.zeros_like(ddtb_ref)

        @pl.when(hb == 0)
        def _():
            dbr_ref[...] = jnp.zeros_like(dbr_ref)
            dar_ref[...] = jnp.zeros_like(dar_ref)

        fn = functools.partial(_gdn_step, h0=h0, rep=rep, inverses=tuple(inv_ref[0, hh] for hh in range(HB)))
        prev = tuple(st_ref[0, hh] for hh in range(HB))
        _, vjp = jax.vjp(fn, prev, q_ref[...], k_ref[...], v_ref[...], br_ref[...], ar_ref[...], alog_ref[...], dtb_ref[...])
        dprev, dq, dk, dv, dbr, dar, dalog, ddtb = vjp((do_ref[...], tuple(dstate[h0 + hh] for hh in range(HB))))
        for hh in range(HB):
            dstate[h0 + hh] = dprev[hh]
        dq_ref[...] = dq
        dk_ref[...] = dk
        dv_ref[...] = dv
        dbr_ref[...] += dbr
        dar_ref[...] += dar
        dalog_ref[...] += dalog
        ddtb_ref[...] += ddtb

    rc = lambda c: nc - 1 - c
    par = pl.BlockSpec((1, HV), lambda c, h: (0, 0))
    blk = lambda W: pl.BlockSpec((L, W), lambda c, h: (rc(c), h))
    return _call(
        body, comm, name=name, grid=(nc, HV // HB),
        in_specs=[pl.BlockSpec((L, QW), lambda c, h: (rc(c), h)), pl.BlockSpec((L, QW), lambda c, h: (rc(c), koff + h)),
                  pl.BlockSpec((L, HB * DV), lambda c, h: (rc(c), voff + h)), pl.BlockSpec((L, HV), lambda c, h: (rc(c), 0)),
                  pl.BlockSpec((L, HV), lambda c, h: (rc(c), 0)), par, par,
                  pl.BlockSpec((1, HB, DK, DV), lambda c, h: (rc(c), h, 0, 0)),
                  pl.BlockSpec((1, HB, L, L), lambda c, h: (rc(c), h, 0, 0)), blk(HB * DV)],
        out_specs=[blk(QW), blk(QW), blk(HB * DV), pl.BlockSpec((L, HV), lambda c, h: (rc(c), 0)),
                   pl.BlockSpec((L, HV), lambda c, h: (rc(c), 0)), par, par],
        out_shape=[jax.ShapeDtypeStruct((S, HK * DK), f32), jax.ShapeDtypeStruct((S, HK * DK), f32), jax.ShapeDtypeStruct((S, HV * DV), f32),
                   jax.ShapeDtypeStruct((S, HV), f32), jax.ShapeDtypeStruct((S, HV), f32),
                   jax.ShapeDtypeStruct((1, HV), f32), jax.ShapeDtypeStruct((1, HV), f32)],
        scratch_shapes=[pltpu.VMEM((HV, DK, DV), f32)],
        sem=("arbitrary", "arbitrary"), args=(qkn, qkn, qkv, br, ar, alog, dtb, states, inverses, do))


def _att_scale():
    return (MLA_NOPE + MLA_ROPE) ** -0.5


def _causal(s, i, j, t, tk=None):
    qpos = i * t + _iota(s.shape, 0)
    kpos = j * (t if tk is None else tk) + _iota(s.shape, 1)
    return kpos <= qpos


def _attn_fwd(qn, qr, kn, kr, v, *, name, comm=None):
    S, W = qn.shape
    H = W // LANES
    t = _tile(S, ATT_Q_TILE)
    tk = _tile(S, ATT_KEY_TILE)
    assert tk % t == 0
    scale = _att_scale()

    def body(qn_ref, qr_ref, kn_ref, kr_ref, v_ref, o_ref, lse_ref):
        i = pl.program_id(1)
        qc = jnp.concatenate([qn_ref[...], qr_ref[...]], axis=1)

        def step(j, carry, masked):
            m, l, acc = carry
            rows = pl.ds(pl.multiple_of(j * tk, tk), tk)
            s = _dot(qc, jnp.concatenate([kn_ref[rows, :], kr_ref[rows, :]], axis=1), _NT) * scale
            if masked:
                s = jnp.where(_causal(s, i, j, t, tk), s, -1e30)
            m_new = jnp.maximum(m, jnp.max(s, axis=1, keepdims=True))
            p = jnp.exp(s - m_new)
            a = jnp.exp(m - m_new)
            return m_new, a * l + jnp.sum(p, axis=1, keepdims=True), a * acc + _dot(p, v_ref[rows, :])

        nfull = lax.div(i * t, tk)
        carry = lax.fori_loop(0, nfull, functools.partial(step, masked=False),
                              (jnp.full((t, 1), -1e30, f32), jnp.zeros((t, 1), f32), jnp.zeros((t, LANES), f32)))
        m, l, acc = step(nfull, carry, True)
        o_ref[...] = acc / l
        lse_ref[...] = jnp.broadcast_to(m + jnp.log(l), (t, LANES))

    qb = pl.BlockSpec((t, LANES), lambda h, i: (i, h))
    kb = pl.BlockSpec((S, LANES), lambda h, i: (0, h))
    return _call(
        body, comm, name=name, grid=(H, S // t),
        in_specs=[qb, qb, kb, pl.BlockSpec((S, LANES), lambda h, i: (0, 0)), kb],
        out_specs=[qb, qb], out_shape=[jax.ShapeDtypeStruct((S, W), f32), jax.ShapeDtypeStruct((S, W), f32)],
        scratch_shapes=[], sem=("parallel", "arbitrary"), args=(qn, qr, kn, kr, v))


def _attn_bwd_dq(qn, qr, kn, kr, v, o, lse, do, *, name, comm=None):
    S, W = qn.shape
    H = W // LANES
    t = _tile(S, ATT_TILE)
    scale = _att_scale()

    def body(qn_ref, qr_ref, kn_ref, kr_ref, v_ref, o_ref, lse_ref, do_ref, dqn_ref, dqr_ref):
        i = pl.program_id(1)
        qc = jnp.concatenate([qn_ref[...], qr_ref[...]], axis=1)
        dov = do_ref[...]
        delta = jnp.sum(dov * o_ref[...], axis=1, keepdims=True)
        lsev = lse_ref[...][:, :1]

        def step(j, dq, masked):
            rows = pl.ds(pl.multiple_of(j * t, t), t)
            kc = jnp.concatenate([kn_ref[rows, :], kr_ref[rows, :]], axis=1)
            s = _dot(qc, kc, _NT) * scale
            p = jnp.exp(s - lsev)
            if masked:
                p = jnp.where(_causal(s, i, j, t), p, 0.0)
            ds = p * (_dot(dov, v_ref[rows, :], _NT) - delta)
            return dq + _dot(ds, kc)

        dq = lax.fori_loop(0, i, functools.partial(step, masked=False), jnp.zeros((t, 2 * LANES), f32))
        dq = step(i, dq, True)
        dq = dq * scale
        dqn_ref[...] = dq[:, :LANES].astype(MXU_DT)
        dqr_ref[...] = dq[:, LANES:]

    qb = pl.BlockSpec((t, LANES), lambda h, i: (i, h))
    kb = pl.BlockSpec((S, LANES), lambda h, i: (0, h))
    return _call(
        body, comm, name=name, grid=(H, S // t),
        in_specs=[qb, qb, kb, pl.BlockSpec((S, LANES), lambda h, i: (0, 0)), kb, qb, qb, qb],
        out_specs=[qb, qb], out_shape=[jax.ShapeDtypeStruct((S, W), MXU_DT), jax.ShapeDtypeStruct((S, W), f32)],
        scratch_shapes=[], sem=("parallel", "arbitrary"), args=(qn, qr, kn, kr, v, o, lse, do))


def _attn_bwd_dkv(qn, qr, kn, kr, v, o, lse, do, *, name, comm=None):
    S, W = qn.shape
    H = W // LANES
    t = _tile(S, ATT_TILE)
    nb = S // t
    scale = _att_scale()

    def body(qn_ref, qr_ref, kn_ref, kr_ref, v_ref, o_ref, lse_ref, do_ref, dkn_ref, dkr_ref, dv_ref):
        j, h = pl.program_id(0), pl.program_id(1)
        kc = jnp.concatenate([kn_ref[...], kr_ref[...]], axis=1)
        vv = v_ref[...]

        def step(i, carry, masked):
            dk, dv = carry
            rows = pl.ds(pl.multiple_of(i * t, t), t)
            qc = jnp.concatenate([qn_ref[rows, :], qr_ref[rows, :]], axis=1)
            dov = do_ref[rows, :]
            delta = jnp.sum(dov * o_ref[rows, :], axis=1, keepdims=True)
            s = _dot(qc, kc, _NT) * scale
            p = jnp.exp(s - lse_ref[rows, :][:, :1])
            if masked:
                p = jnp.where(_causal(s, i, j, t), p, 0.0)
            ds = p * (_dot(dov, vv, _NT) - delta)
            return dk + _dot(ds, qc, _TN), dv + _dot(p, dov, _TN)

        carry = step(j, (jnp.zeros((t, 2 * LANES), f32), jnp.zeros((t, LANES), f32)), True)
        dk, dv = lax.fori_loop(j + 1, nb, functools.partial(step, masked=False), carry)
        dk = dk * scale
        dkn_ref[...] = dk[:, :LANES].astype(MXU_DT)
        dv_ref[...] = dv.astype(MXU_DT)

        @pl.when(h == 0)
        def _():
            dkr_ref[...] = jnp.zeros_like(dkr_ref)

        dkr_ref[...] += dk[:, LANES:]

    full = pl.BlockSpec((S, LANES), lambda j, h: (0, h))
    kb = pl.BlockSpec((t, LANES), lambda j, h: (j, h))
    k0 = pl.BlockSpec((t, LANES), lambda j, h: (j, 0))
    return _call(
        body, comm, name=name, grid=(nb, H),
        in_specs=[full, full, kb, k0, kb, full, full, full],
        out_specs=[kb, k0, kb],
        out_shape=[jax.ShapeDtypeStruct((S, W), MXU_DT), jax.ShapeDtypeStruct((S, LANES), f32), jax.ShapeDtypeStruct((S, W), MXU_DT)],
        scratch_shapes=[], sem=("arbitrary", "arbitrary"), args=(qn, qr, kn, kr, v, o, lse, do))


def _loss_head(y, target, *, name):
    S, D = y.shape
    tm = _tile(S, ROW_TILE, 8)

    def body(y_ref, t_ref, loss_ref, dy_ref):
        @pl.when(pl.program_id(0) == 0)
        def _():
            loss_ref[...] = jnp.zeros_like(loss_ref)

        e = y_ref[...] - t_ref[...]
        dy_ref[...] = e / D
        part = 0.5 * jnp.sum(jnp.mean(e * e, axis=1, keepdims=True), axis=0, keepdims=True)
        loss_ref[...] += jnp.broadcast_to(part, loss_ref.shape)

    rb = pl.BlockSpec((tm, D), lambda r: (r, 0))
    return pl.pallas_call(
        body, name=name, grid=(S // tm,), in_specs=[rb, rb],
        out_specs=[pl.BlockSpec((1, LANES), lambda r: (0, 0)), rb],
        out_shape=[jax.ShapeDtypeStruct((1, LANES), f32), jax.ShapeDtypeStruct((S, D), f32)],
        compiler_params=_cp(("arbitrary",)))(y, target)


def _adamw(w, g, m, v, *, name):
    R, C = w.shape
    tm = _tile(R, max(8, (1 << 19) // max(C, 1) // 8 * 8), 8)

    def body(w_ref, g_ref, m_ref, v_ref, d_ref, nm_ref, nv_ref):
        gv = g_ref[...]
        nm = ADAM_B1 * m_ref[...] + (1.0 - ADAM_B1) * gv
        nv = ADAM_B2 * v_ref[...] + (1.0 - ADAM_B2) * (gv * gv)
        m_hat = nm / (1.0 - ADAM_B1 ** ADAM_STEP)
        v_hat = nv / (1.0 - ADAM_B2 ** ADAM_STEP)
        d_ref[...] = -ADAM_LR * (m_hat / (jnp.sqrt(v_hat) + ADAM_EPS) + ADAM_WD * w_ref[...])
        nm_ref[...] = nm
        nv_ref[...] = nv

    rb = pl.BlockSpec((tm, C), lambda r: (r, 0))
    sh = jax.ShapeDtypeStruct((R, C), f32)
    return pl.pallas_call(body, name=name, grid=(R // tm,), in_specs=[rb] * 4, out_specs=[rb] * 3, out_shape=[sh] * 3,
                          compiler_params=_cp(("parallel",)))(w, g, m, v)


def _me():
    return lax.axis_index("x"), lax.axis_index("y"), lax.axis_index("c")


def _other_chips(mx, my):
    return [(1 - mx, my), (mx, 1 - my), (1 - mx, 1 - my)]


_ANY = pl.BlockSpec(memory_space=pl.ANY)


class _GatherChips:
    def __init__(self, xs):
        self.arrays = list(xs)
        n = len(xs)
        for x in xs:
            assert x.shape[0] % 2 == 0
        self.halves = [x.shape[0] // 2 for x in xs]
        self.out_shapes = [jax.ShapeDtypeStruct((4,) + x.shape, x.dtype) for x in xs]
        self.scratch = [pltpu.SemaphoreType.DMA((n, 6)), pltpu.SemaphoreType.DMA((n, 6))]

    def _sends(self, x_refs, o_refs, send, recv):
        mx, my, mc = _me()
        me = 2 * mx + my
        out = []
        for t, hf in enumerate(self.halves):
            mine = pl.ds(mc * hf, hf)
            for j, (cx, cy) in enumerate(_other_chips(mx, my)):
                out.append(pltpu.make_async_remote_copy(x_refs[t].at[mine], o_refs[t].at[me, mine], send.at[t, j], recv.at[t, j],
                                                        device_id=(cx, cy, mc), device_id_type=MESH))
        return out

    def start(self, x_refs, o_refs, scr):
        for cp in self._sends(x_refs, o_refs, *scr):
            cp.start()

    def finish(self, x_refs, o_refs, scr):
        send, recv = scr
        mx, my, mc = _me()
        chips = _other_chips(mx, my)
        fwd = []
        for t, hf in enumerate(self.halves):
            mine = pl.ds(mc * hf, hf)
            for j, (cx, cy) in enumerate(chips):
                k = 2 * cx + cy
                pltpu.make_async_remote_copy(x_refs[t].at[mine], o_refs[t].at[k, mine], send.at[t, j], recv.at[t, j],
                                             device_id=(cx, cy, mc), device_id_type=MESH).wait_recv()
                cp = pltpu.make_async_remote_copy(o_refs[t].at[k, mine], o_refs[t].at[k, mine], send.at[t, 3 + j], recv.at[t, 3 + j],
                                                  device_id=(mx, my, 1 - mc), device_id_type=MESH)
                cp.start()
                fwd.append(cp)
        for t, hf in enumerate(self.halves):
            theirs = pl.ds((1 - mc) * hf, hf)
            for j, (cx, cy) in enumerate(chips):
                k = 2 * cx + cy
                pltpu.make_async_remote_copy(o_refs[t].at[k, theirs], o_refs[t].at[k, theirs], send.at[t, 3 + j], recv.at[t, 3 + j],
                                             device_id=(mx, my, 1 - mc), device_id_type=MESH).wait_recv()
        for cp in self._sends(x_refs, o_refs, send, recv) + fwd:
            cp.wait_send()


class _ScatterChips:
    def __init__(self, ps):
        self.arrays = list(ps)
        n = len(ps)
        self.out_shapes = [jax.ShapeDtypeStruct((3,) + p.shape[1:], p.dtype) for p in ps]
        self.scratch = [pltpu.SemaphoreType.DMA((n, 3)), pltpu.SemaphoreType.DMA((n, 3))]

    def _copies(self, p_refs, o_refs, send, recv):
        mx, my, mc = _me()
        return [pltpu.make_async_remote_copy(p_refs[t].at[2 * cx + cy], o_refs[t].at[j], send.at[t, j], recv.at[t, j],
                                             device_id=(cx, cy, mc), device_id_type=MESH)
                for t in range(len(self.arrays)) for j, (cx, cy) in enumerate(_other_chips(mx, my))]

    def start(self, p_refs, o_refs, scr):
        for cp in self._copies(p_refs, o_refs, *scr):
            cp.start()

    def finish(self, p_refs, o_refs, scr):
        for cp in self._copies(p_refs, o_refs, *scr):
            cp.wait()


def _run_comm(comm, *, name):
    n = len(comm.arrays)

    def body(*refs):
        ins, outs, scr = refs[:n], refs[n:2 * n], refs[2 * n:]
        comm.start(ins, outs, scr)
        comm.finish(ins, outs, scr)

    return pl.pallas_call(body, name=name, in_specs=[_ANY] * n, out_specs=[_ANY] * n, out_shape=comm.out_shapes,
                          scratch_shapes=comm.scratch, compiler_params=pltpu.CompilerParams(has_side_effects=True))(*comm.arrays)


def _call(body, comm, *, name, grid, in_specs, out_specs, out_shape, scratch_shapes, sem, args):
    if comm is None:
        res = pl.pallas_call(body, name=name, grid=grid, in_specs=in_specs, out_specs=out_specs, out_shape=out_shape,
                             scratch_shapes=scratch_shapes, compiler_params=_cp(sem))(*args)
        return list(res), None
    n_in, n_out, n_scr, nc = len(in_specs), len(out_specs), len(scratch_shapes), len(comm.arrays)

    def wrapped(*refs):
        ins, cins = refs[:n_in], refs[n_in:n_in + nc]
        outs, couts = refs[n_in + nc:n_in + nc + n_out], refs[n_in + nc + n_out:n_in + 2 * nc + n_out]
        scr, cscr = refs[n_in + 2 * nc + n_out:n_in + 2 * nc + n_out + n_scr], refs[n_in + 2 * nc + n_out + n_scr:]
        ids = [pl.program_id(d) for d in range(len(grid))]
        first = functools.reduce(jnp.logical_and, [i == 0 for i in ids])
        last = functools.reduce(jnp.logical_and, [i == g - 1 for i, g in zip(ids, grid)])

        @pl.when(first)
        def _():
            comm.start(cins, couts, cscr)

        body(*ins, *outs, *scr)

        @pl.when(last)
        def _():
            comm.finish(cins, couts, cscr)

    res = pl.pallas_call(
        wrapped, name=name, grid=grid, in_specs=list(in_specs) + [_ANY] * nc, out_specs=list(out_specs) + [_ANY] * nc,
        out_shape=list(out_shape) + comm.out_shapes, scratch_shapes=list(scratch_shapes) + comm.scratch,
        compiler_params=_cp(("arbitrary",) * len(grid)))(*args, *comm.arrays)
    return list(res[:n_out]), list(res[n_out:])


class _PairSend:
    def __init__(self, gs):
        self.arrays = list(gs)
        n = len(gs)
        self.halves = [g.shape[1] // 2 for g in gs]
        self.out_shapes = [jax.ShapeDtypeStruct((4, g.shape[1] // 2, g.shape[2]), g.dtype) for g in gs]
        self.scratch = [pltpu.SemaphoreType.DMA((n, 4)), pltpu.SemaphoreType.DMA((n, 4))]

    def _copies(self, g_refs, o_refs, send, recv):
        mx, my, mc = _me()
        return [pltpu.make_async_remote_copy(g_refs[t].at[k, pl.ds((1 - mc) * hf, hf)], o_refs[t].at[k], send.at[t, k], recv.at[t, k],
                                             device_id=(mx, my, 1 - mc), device_id_type=MESH)
                for t, hf in enumerate(self.halves) for k in range(4)]

    def start(self, g_refs, o_refs, scr):
        for cp in self._copies(g_refs, o_refs, *scr):
            cp.start()

    def finish(self, g_refs, o_refs, scr):
        for cp in self._copies(g_refs, o_refs, *scr):
            cp.wait()


def _pair_exchange_halves(fs, *, name):
    n = len(fs)

    def body(*refs):
        f_refs, o_refs = refs[:n], refs[n:2 * n]
        send, recv = refs[2 * n:]
        mx, my, mc = _me()
        cps = []
        for t in range(n):
            hf = f_refs[t].shape[0]
            mine = pl.ds(mc * hf, hf)
            cp = pltpu.make_async_remote_copy(f_refs[t], o_refs[t].at[mine], send.at[t], recv.at[t],
                                              device_id=(mx, my, 1 - mc), device_id_type=MESH)
            cp.start()
            cps.append(cp)
        for t in range(n):
            hf = f_refs[t].shape[0]
            theirs = pl.ds((1 - mc) * hf, hf)
            cps[t].wait_send()
            pltpu.make_async_remote_copy(f_refs[t], o_refs[t].at[theirs], send.at[t], recv.at[t],
                                         device_id=(mx, my, 1 - mc), device_id_type=MESH).wait_recv()

    return pl.pallas_call(
        body, name=name, in_specs=[_ANY] * n, out_specs=[_ANY] * n,
        out_shape=[jax.ShapeDtypeStruct((2 * f.shape[0], f.shape[1]), f.dtype) for f in fs],
        scratch_shapes=[pltpu.SemaphoreType.DMA((n,)), pltpu.SemaphoreType.DMA((n,))],
        compiler_params=pltpu.CompilerParams(has_side_effects=True))(*fs)


def _allgather_all(x, *, name):
    def body(x_ref, o_ref, send, recv, lsem):
        mx, my, mc = _me()
        me = 4 * mx + 2 * my + mc
        local = pltpu.make_async_copy(x_ref, o_ref.at[me], lsem)
        local.start()
        cps = []
        for j in range(1, 8):
            px, py, pc = mx ^ (j >> 2), my ^ ((j >> 1) & 1), mc ^ (j & 1)
            cp = pltpu.make_async_remote_copy(x_ref, o_ref.at[me], send.at[j - 1], recv.at[j - 1],
                                              device_id=(px, py, pc), device_id_type=MESH)
            cp.start()
            cps.append(cp)
        for j in range(1, 8):
            px, py, pc = mx ^ (j >> 2), my ^ ((j >> 1) & 1), mc ^ (j & 1)
            pltpu.make_async_remote_copy(x_ref, o_ref.at[4 * px + 2 * py + pc], send.at[j - 1], recv.at[j - 1],
                                         device_id=(px, py, pc), device_id_type=MESH).wait_recv()
        for cp in cps:
            cp.wait_send()
        local.wait()

    return pl.pallas_call(
        body, name=name, in_specs=[_ANY], out_specs=_ANY, out_shape=jax.ShapeDtypeStruct((8,) + x.shape, x.dtype),
        scratch_shapes=[pltpu.SemaphoreType.DMA((7,)), pltpu.SemaphoreType.DMA((7,)), pltpu.SemaphoreType.DMA],
        compiler_params=pltpu.CompilerParams(has_side_effects=True))(x)


def _add_half(g4, recv, mc, *, name):
    _, R, C = g4.shape
    hf = R // 2
    tm = _tile(hf, max(16, (1 << 19) // C // 16 * 16), 16)
    nb = hf // tm

    def body(mc_ref, g_ref, r_ref, o_ref, ob_ref):
        s = g_ref[...] + r_ref[...]
        o_ref[...] = s
        ob_ref[...] = s.astype(COMM_DT)

    ospec = pl.BlockSpec((1, tm, C), lambda k, i, mc_ref: (k, i, 0))
    return pl.pallas_call(
        body, name=name,
        grid_spec=pltpu.PrefetchScalarGridSpec(
            num_scalar_prefetch=1, grid=(4, nb),
            in_specs=[pl.BlockSpec((1, tm, C), lambda k, i, mc_ref: (k, mc_ref[0] * nb + i, 0)),
                      pl.BlockSpec((1, tm, C), lambda k, i, mc_ref: (k, i, 0))],
            out_specs=[ospec, ospec]),
        out_shape=[jax.ShapeDtypeStruct((4, hf, C), f32), jax.ShapeDtypeStruct((4, hf, C), COMM_DT)],
        compiler_params=_cp(("parallel", "parallel")))(mc, g4, recv)


def _sum_chips(p4, recv3, me, *, name):
    _, Rh, C = p4.shape
    tm = _tile(Rh, max(16, (1 << 19) // C // 16 * 16), 16)

    def body(me_ref, p_ref, r_ref, o_ref):
        o_ref[...] = ((p_ref[0] + r_ref[0].astype(f32)) + r_ref[1].astype(f32)) + r_ref[2].astype(f32)

    return pl.pallas_call(
        body, name=name,
        grid_spec=pltpu.PrefetchScalarGridSpec(
            num_scalar_prefetch=1, grid=(Rh // tm,),
            in_specs=[pl.BlockSpec((1, tm, C), lambda i, me_ref: (me_ref[0], i, 0)),
                      pl.BlockSpec((3, tm, C), lambda i, me_ref: (0, i, 0))],
            out_specs=pl.BlockSpec((tm, C), lambda i, me_ref: (i, 0))),
        out_shape=jax.ShapeDtypeStruct((Rh, C), f32),
        compiler_params=_cp(("parallel",)))(me, p4, recv3)


def _sum8(x8, *, name):
    _, R, C = x8.shape
    tm = _tile(R, 64, 8)

    def body(x_ref, o_ref):
        acc = x_ref[0]
        for k in range(1, 8):
            acc = acc + x_ref[k]
        o_ref[...] = acc

    return pl.pallas_call(body, name=name, grid=(R // tm,), in_specs=[pl.BlockSpec((8, tm, C), lambda i: (0, i, 0))],
                          out_specs=pl.BlockSpec((tm, C), lambda i: (i, 0)), out_shape=jax.ShapeDtypeStruct((R, C), f32),
                          compiler_params=_cp(("parallel",)))(x8)


def _ssd_layer_fwd(h, W, tag, plan, i):
    z = _mm(h, W["wz"], name=tag + "_z")
    early = plan.fwd_early_comm(i)
    if early is None:
        xp = _mm(h, W["wxbc"], name=tag + "_xbc")
    else:
        xp, eres = _mm(h, W["wxbc"], name=tag + "_xbc", comm=early)
        plan.fwd_early_done(i, eres)
    dtr = _mm(h, W["wdt"], name=tag + "_dt")
    xbc = _conv_fwd(xp, W["conv_w"], W["conv_b"], name=tag + "_conv")
    (y, states), cres = _ssd_scan_fwd(xbc, dtr, W["dt_bias"], W["a_log"], W["d"], name=tag + "_scan", comm=plan.fwd_comm(i))
    DI = y.shape[1]
    G = SSD_N_GROUPS
    gs = DI // G
    (yn,) = _rowwise(_ssd_gate_fn, [(W["norm_w"], "c", gs)], [(y, "c", gs), (z, "c", gs)], [(DI, MXU_DT, "c", gs)],
                     name=tag + "_gate", ncol=G, tm=512)
    out = _mm(yn, W["wout"], name=tag + "_out")
    return out, dict(h=h, z=z, xp=xp, dtr=dtr, xbc=xbc, states=states, y=y, yn=yn), cres


def _carried_rowwise_bwd(plan, i, *a, **kw):
    early = plan.bwd_early_comm(i)
    if early is None:
        return _rowwise_bwd(*a, **kw)
    dp, di, cres = _rowwise_bwd(*a, comm=early, **kw)
    plan.bwd_early_done(i, cres)
    return dp, di


def _carried_mm(comm, done, i, *a, **kw):
    if comm is None:
        return _mm(*a, **kw)
    out, cres = _mm(*a, comm=comm, **kw)
    done(i, cres)
    return out


def _ssd_layer_bwd(sv, W, dr, drb, tag, plan, i):
    h = sv["h"]
    DI = sv["y"].shape[1]
    G = SSD_N_GROUPS
    gs = DI // G
    gr = {}
    dyn = _mm(drb, W["wout"], tb=True, name=tag + "_dyn")
    gr["wout"] = _mm(sv["yn"], drb, ta=True, name=tag + "_dwout")
    plan.early_grad(i, "ssd_out_w", gr["wout"])
    (dnw,), (dy, dz) = _carried_rowwise_bwd(plan, i, _ssd_gate_fn, [(W["norm_w"], "c", gs)], [(sv["y"], "c", gs), (sv["z"], "c", gs)],
                                             [(dyn, "c", gs)], name=tag + "_dgate", ncol=G, tm=512, din_dtypes=[(f32,), (MXU_DT,)])
    gr["norm_w"] = dnw
    (dxbc, ddtr, gr["dt_bias"], gr["a_log"], gr["d"]), cres = _ssd_scan_bwd(
        sv["xbc"], sv["dtr"], W["dt_bias"], W["a_log"], W["d"], sv["states"], dy, name=tag + "_dscan", comm=plan.bwd_comm(i))
    plan.bwd_done(i, cres)
    dxp, gr["conv_w"], gr["conv_b"] = _conv_bwd(
        sv["xp"], W["conv_w"], W["conv_b"], [dxbc], name=tag + "_dconv", dx_dtype=MXU_DT,
        dy_block=lambda c, cw: _ssd_grouped_block(c, G, gs, SSD_D_STATE, cw))
    gr["wz"] = _mm(h, dz, ta=True, name=tag + "_dwz")
    gr["wxbc"] = _mm(h, dxp, ta=True, name=tag + "_dwxbc")
    gr["wdt"] = _mm(h, ddtr, ta=True, name=tag + "_dwdt")
    dh = _carried_mm(plan.tail_early_comm(i, gr), plan.tail_early_done, i,
                     dz, W["wz"], tb=True, add=dr, add_scale=_alpha(), name=tag + "_dh1")
    dh = _carried_mm(plan.tail_comm(i), plan.tail_done, i, dxp, W["wxbc"], tb=True, add=dh, name=tag + "_dh2")
    dh = _mm(ddtr, W["wdt"], tb=True, add=dh, name=tag + "_dh3")
    return dh, gr


def _mla_layer_fwd(h, W, cos, sin, tag, comm=None):
    QR, KR = W["wqc"].shape[1], W["wkvc"].shape[1]
    HW = W["wqn"].shape[1]
    H = HW // LANES
    qc = _mm(h, W["wqc"], name=tag + "_qc")
    kvc = _mm(h, W["wkvc"], name=tag + "_kvc")
    krp = _mm(h, W["wkr"], name=tag + "_krp")
    z = _mm(h, W["wz"], name=tag + "_z")
    (qcn,) = _rowwise(_rms_fn, [(W["q_norm"], "a", QR)], [(qc, "a", QR)], [(QR, MXU_DT, "a", QR)], name=tag + "_qnorm")
    (kvn,) = _rowwise(_rms_fn, [(W["kv_norm"], "a", KR)], [(kvc, "a", KR)], [(KR, MXU_DT, "a", KR)], name=tag + "_kvnorm")
    qn = _mm(qcn, W["wqn"], name=tag + "_qn", out_dtype=MXU_DT)
    qrp = _mm(qcn, W["wqr"], name=tag + "_qrp")
    kn = _mm(kvn, W["wkn"], name=tag + "_kn", out_dtype=MXU_DT)
    v = _mm(kvn, W["wv"], name=tag + "_v", out_dtype=MXU_DT)
    (qr,) = _rowwise(_rope_fn, [], [(cos, "a", LANES), (sin, "a", LANES), (qrp, "c", LANES)], [(HW, MXU_DT, "c", LANES)],
                     name=tag + "_qrope", ncol=H, tm=1024)
    (kr,) = _rowwise(_rope_fn, [], [(cos, "a", LANES), (sin, "a", LANES), (krp, "a", LANES)], [(LANES, MXU_DT, "a", LANES)],
                     name=tag + "_krope")
    (o, lse), cres = _attn_fwd(qn, qr, kn, kr, v, name=tag + "_attn", comm=comm)
    (og,) = _rowwise(_mul_silu_fn, [], [(o, "a", HW), (z, "a", HW)], [(HW, MXU_DT, "a", HW)], name=tag + "_ogate")
    out = _mm(og, W["wout"], name=tag + "_out")
    return out, dict(h=h, qc=qc, kvc=kvc, z=z, qcn=qcn, kvn=kvn, qn=qn, qr=qr, kn=kn, kr=kr, v=v, o=o, lse=lse, og=og), cres


def _mla_layer_bwd(sv, W, cos, sin, dr, drb, tag, plan, i):
    h = sv["h"]
    QR, KR = W["wqc"].shape[1], W["wkvc"].shape[1]
    HW = W["wqn"].shape[1]
    H = HW // LANES
    gr = {}
    dog = _mm(drb, W["wout"], tb=True, name=tag + "_dog")
    gr["wout"] = _mm(sv["og"], drb, ta=True, name=tag + "_dwout")
    _, (do, dz) = _rowwise_bwd(_mul_silu_fn, [], [(sv["o"], "a", HW), (sv["z"], "a", HW)], [(dog, "a", HW)], name=tag + "_dogate",
                               din_dtypes=[(f32,), (MXU_DT,)])
    att = (sv["qn"], sv["qr"], sv["kn"], sv["kr"], sv["v"], sv["o"], sv["lse"], do)
    (dqn, dqr), cres = _attn_bwd_dq(*att, name=tag + "_dq", comm=plan.bwd_early_comm(i))
    plan.bwd_early_done(i, cres)
    (dkn, dkr, dv), cres = _attn_bwd_dkv(*att, name=tag + "_dkv", comm=plan.bwd_comm(i))
    plan.bwd_done(i, cres)
    _, (dqrp,) = _rowwise_bwd(_rope_fn, [], [(cos, "a", LANES), (sin, "a", LANES), (dqr, "c", LANES)], [(dqr, "c", LANES)],
                              name=tag + "_dqrope", ncol=H, tm=1024, diff_i=[2], din_dtypes=[(MXU_DT,)])
    _, (dkrp,) = _rowwise_bwd(_rope_fn, [], [(cos, "a", LANES), (sin, "a", LANES), (dkr, "a", LANES)], [(dkr, "a", LANES)],
                              name=tag + "_dkrope", diff_i=[2], din_dtypes=[(MXU_DT,)])
    dqcn = _mm(dqn, W["wqn"], tb=True, name=tag + "_dqcn1")
    dqcn = _mm(dqrp, W["wqr"], tb=True, add=dqcn, name=tag + "_dqcn2")
    dkvn = _mm(dkn, W["wkn"], tb=True, name=tag + "_dkvn1")
    dkvn = _mm(dv, W["wv"], tb=True, add=dkvn, name=tag + "_dkvn2")
    gr["wqn"] = _mm(sv["qcn"], dqn, ta=True, name=tag + "_dwqn")
    gr["wqr"] = _mm(sv["qcn"], dqrp, ta=True, name=tag + "_dwqr")
    gr["wkn"] = _mm(sv["kvn"], dkn, ta=True, name=tag + "_dwkn")
    gr["wv"] = _mm(sv["kvn"], dv, ta=True, name=tag + "_dwv")
    (gr["q_norm"],), (dqc,) = _rowwise_bwd(_rms_fn, [(W["q_norm"], "a", QR)], [(sv["qc"], "a", QR)], [(dqcn, "a", QR)], name=tag + "_dqnorm",
                                           din_dtypes=[(MXU_DT,)])
    (gr["kv_norm"],), (dkvc,) = _rowwise_bwd(_rms_fn, [(W["kv_norm"], "a", KR)], [(sv["kvc"], "a", KR)], [(dkvn, "a", KR)], name=tag + "_dkvnorm",
                                             din_dtypes=[(MXU_DT,)])
    dh = _mm(dz, W["wz"], tb=True, add=dr, add_scale=_alpha(), name=tag + "_dh1")
    dh = _mm(dqc, W["wqc"], tb=True, add=dh, name=tag + "_dh2")
    dh = _mm(dkvc, W["wkvc"], tb=True, add=dh, name=tag + "_dh3")
    dh = _mm(dkrp, W["wkr"], tb=True, add=dh, name=tag + "_dh4")
    gr["wz"] = _mm(h, dz, ta=True, name=tag + "_dwz")
    gr["wqc"] = _mm(h, dqc, ta=True, name=tag + "_dwqc")
    gr["wkvc"] = _mm(h, dkvc, ta=True, name=tag + "_dwkvc")
    gr["wkr"] = _mm(h, dkrp, ta=True, name=tag + "_dwkr")
    return dh, gr


def _gdn_layer_fwd(h, W, tag, comm=None):
    HK, HV, DK, DV = GDN_N_QK_HEADS, GDN_N_V_HEADS, GDN_DK, GDN_DV
    KD, VD = HK * DK, HV * DV
    qkvp = _mm(h, W["wqkv"], name=tag + "_qkv")
    z = _mm(h, W["wz"], name=tag + "_z")
    br = _mm(h, W["wb"], name=tag + "_b")
    ar = _mm(h, W["wa"], name=tag + "_a")
    qkv = _conv_fwd(qkvp, W["conv_w"], jnp.zeros((1, qkvp.shape[1]), f32), name=tag + "_conv")
    scale = jnp.concatenate([jnp.full((1, KD), DK ** -0.5, f32), jnp.ones((1, KD), f32)], axis=1)
    (qkn,) = _rowwise(_l2_fn, [(scale, "c", DK)], [(qkv, "c", DK)], [(2 * KD, f32, "c", DK)], name=tag + "_l2", ncol=2 * HK, tm=2048)
    (o, states, inverses), cres = _gdn_scan_fwd(qkn, qkv, br, ar, W["a_log"], W["dt_bias"], name=tag + "_scan", comm=comm)
    (on,) = _rowwise(_gdn_gate_fn, [(W["norm_w"], "a", DV)], [(o, "c", DV), (z, "c", DV)], [(VD, MXU_DT, "c", DV)],
                     name=tag + "_gate", ncol=HV, tm=1024)
    out = _mm(on, W["wout"], name=tag + "_out")
    return out, dict(h=h, qkvp=qkvp, z=z, br=br, ar=ar, qkv=qkv, qkn=qkn, o=o, states=states, inverses=inverses, on=on, scale=scale), cres


def _gdn_layer_bwd(sv, W, dr, drb, tag, plan, i):
    h = sv["h"]
    HK, HV, DK, DV = GDN_N_QK_HEADS, GDN_N_V_HEADS, GDN_DK, GDN_DV
    KD, VD = HK * DK, HV * DV
    gr = {}
    don = _mm(drb, W["wout"], tb=True, name=tag + "_don")
    gr["wout"] = _mm(sv["on"], drb, ta=True, name=tag + "_dwout")
    (gr["norm_w"],), (do, dz) = _carried_rowwise_bwd(plan, i, _gdn_gate_fn, [(W["norm_w"], "a", DV)], [(sv["o"], "c", DV), (sv["z"], "c", DV)],
                                                      [(don, "c", DV)], name=tag + "_dgate", ncol=HV, tm=1024, din_dtypes=[(f32,), (MXU_DT,)])
    (dq, dk, dv, dbr, dar, gr["a_log"], gr["dt_bias"]), cres = _gdn_scan_bwd(
        sv["qkn"], sv["qkv"], sv["br"], sv["ar"], W["a_log"], W["dt_bias"], sv["states"], sv["inverses"], do, name=tag + "_dscan",
        comm=plan.bwd_comm(i))
    plan.bwd_done(i, cres)
    _, (dqq,) = _rowwise_bwd(_l2_fn, [(sv["scale"], "c", DK)], [(sv["qkv"], "c", DK)], [(dq, "c", DK)],
                             name=tag + "_dl2q", ncol=HK, tm=2048, diff_p=[])
    _, (dqk,) = _rowwise_bwd(_l2_fn, [(sv["scale"], "c", DK, HK)], [(sv["qkv"], "c", DK, HK)], [(dk, "c", DK)],
                             name=tag + "_dl2k", ncol=HK, tm=2048, diff_p=[])
    dqkvp, gr["conv_w"], _ = _conv_bwd(sv["qkvp"], W["conv_w"], jnp.zeros((1, sv["qkvp"].shape[1]), f32), [dqq, dqk, dv],
                                       name=tag + "_dconv", dx_dtype=MXU_DT)
    dh = _mm(dz, W["wz"], tb=True, add=dr, add_scale=_alpha(), name=tag + "_dh1")
    dh = _mm(dqkvp, W["wqkv"], tb=True, add=dh, name=tag + "_dh2")
    dh = _mm(dbr, W["wb"], tb=True, add=dh, name=tag + "_dh3")
    dh = _mm(dar, W["wa"], tb=True, add=dh, name=tag + "_dh4")
    gr["wz"] = _mm(h, dz, ta=True, name=tag + "_dwz")
    gr["wqkv"] = _mm(h, dqkvp, ta=True, name=tag + "_dwqkv")
    gr["wb"] = _mm(h, dbr, ta=True, name=tag + "_dwb")
    gr["wa"] = _mm(h, dar, ta=True, name=tag + "_dwa")
    return dh, gr


def _rope_tables(positions):
    half = MLA_ROPE // 2
    inv_freq = ROPE_THETA ** (-jnp.arange(0, MLA_ROPE, 2, dtype=f32) / MLA_ROPE)
    ang = positions.astype(f32)[:, None] * inv_freq
    cos, sin = jnp.cos(ang), jnp.sin(ang)
    S = positions.shape[0]
    pad = jnp.zeros((S, LANES - 2 * half), f32)
    return jnp.concatenate([cos, cos, pad + 1.0], axis=1), jnp.concatenate([sin, sin, pad], axis=1)


class _LocalPlan:
    def __init__(self, LW):
        self.LW, self.grads = LW, [None] * DEPTH

    def weights(self, i):
        return self.LW[i]

    def fwd_early_comm(self, i):
        return None

    def fwd_early_done(self, i, res):
        pass

    def fwd_comm(self, i):
        return None

    def fwd_done(self, i, res):
        pass

    def early_grad(self, i, name, g):
        pass

    def tail_early_comm(self, i, gr):
        return None

    def tail_early_done(self, i, res):
        pass

    def tail_comm(self, i):
        return None

    def tail_done(self, i, res):
        pass

    def bwd_early_comm(self, i):
        return None

    def bwd_early_done(self, i, res):
        pass

    def bwd_comm(self, i):
        return None

    def bwd_done(self, i, res):
        pass

    def layer_grads(self, i, gr):
        self.grads[i] = gr


def _local_step(x, positions, target, ln_g, ln_b, plan):
    cos, sin = _rope_tables(positions)
    h, hb = x, x.astype(MXU_DT)
    saved, LW = [], []
    for i in range(DEPTH):
        kind, tag = i % 3, "l%d" % i
        LW.append(plan.weights(i))
        if kind == 0:
            y, sv, cres = _ssd_layer_fwd(hb, LW[i], tag, plan, i)
        elif kind == 1:
            y, sv, cres = _mla_layer_fwd(hb, LW[i], cos, sin, tag, plan.fwd_comm(i))
        else:
            y, sv, cres = _gdn_layer_fwd(hb, LW[i], tag, plan.fwd_comm(i))
        plan.fwd_done(i, cres)
        D = h.shape[1]
        r, h, hb = _rowwise(_res_ln_fn, [(ln_g[i], "a", D), (ln_b[i], "a", D)], [(h, "a", D), (y, "a", D)],
                            [(D, f32, "a", D), (D, f32, "a", D), (D, MXU_DT, "a", D)], name=tag + "_ln")
        sv["r"] = r
        saved.append(sv)
    loss, dh = _loss_head(h, target, name="loss_head")
    dg, db = [None] * DEPTH, [None] * DEPTH
    for i in reversed(range(DEPTH)):
        kind, tag = i % 3, "l%d" % i
        sv = saved[i]
        D = dh.shape[1]
        (dg[i], db[i]), (dr, drb) = _rowwise_bwd(_ln_fn, [(ln_g[i], "a", D), (ln_b[i], "a", D)], [(sv["r"], "a", D)], [(dh, "a", D)],
                                                 name=tag + "_dln", din_dtypes=[(f32, MXU_DT)])
        if kind == 0:
            dh, gr = _ssd_layer_bwd(sv, LW[i], dr, drb, tag, plan, i)
        elif kind == 1:
            dh, gr = _mla_layer_bwd(sv, LW[i], cos, sin, dr, drb, tag, plan, i)
        else:
            dh, gr = _gdn_layer_bwd(sv, LW[i], dr, drb, tag, plan, i)
        plan.layer_grads(i, gr)
    return loss, dh, dg, db


_WEIGHTS = ["ssd_in_w", "ssd_conv_w", "ssd_conv_b", "ssd_dt_bias", "ssd_a_log", "ssd_d", "ssd_norm_w", "ssd_out_w",
            "mla_in_w", "mla_q_norm_w", "mla_q_up_w", "mla_kv_norm_w", "mla_kv_up_w", "mla_out_w",
            "gdn_in_w", "gdn_conv_w", "gdn_a_log", "gdn_dt_bias", "gdn_norm_w", "gdn_out_w", "ln_g", "ln_b"]
_BIG = {"ssd_in_w": "col", "ssd_out_w": "row", "mla_in_w": "col", "mla_q_up_w": "col", "mla_kv_up_w": "col",
        "mla_out_w": "row", "gdn_in_w": "col", "gdn_out_w": "row"}
_SMALL_SHARDED = ["ssd_conv_w", "ssd_conv_b", "ssd_norm_w", "gdn_conv_w"]
_PACK_ROWS = 16


def _gathered_to_full(g, kind, nl):
    if kind == "col":
        _, RK, Ns = g.shape
        return g.reshape(4, nl, RK // nl, Ns).transpose(1, 2, 0, 3).reshape(nl, RK // nl, 4 * Ns)
    _, RK, N = g.shape
    return g.reshape(4, nl, RK // nl, N).transpose(1, 0, 2, 3).reshape(nl, 4 * (RK // nl), N)


def _full_to_slots(f, kind):
    nl, K, N = f.shape
    if kind == "col":
        return f.reshape(nl, K, 4, N // 4).transpose(2, 0, 1, 3).reshape(4, nl * K, N // 4)
    return f.reshape(nl, 4, K // 4, N).transpose(1, 0, 2, 3).reshape(4, nl * (K // 4), N)


def _pack(arrs):
    flat = jnp.concatenate([a.reshape(-1).astype(f32) for a in arrs])
    unit = _PACK_ROWS * LANES
    n = -(-flat.shape[0] // unit) * unit
    return jnp.pad(flat, (0, n - flat.shape[0])).reshape(_PACK_ROWS, n // _PACK_ROWS)


def _unpack(packed, shapes):
    flat = packed.reshape(-1)
    out, off = [], 0
    for sh in shapes:
        n = math.prod(sh)
        out.append(flat[off:off + n].reshape(sh))
        off += n
    return out


def _pad_lanes(a):
    return jnp.pad(a, [(0, 0)] * (a.ndim - 1) + [(0, LANES - a.shape[-1])])


_IN_PROJ = ("ssd_in_w", "mla_in_w", "gdn_in_w")


class _ColSlots:
    def __init__(self, slots):
        self.slots = slots
        self.shape = (slots.shape[1], 4 * slots.shape[2])

    def __getitem__(self, idx):
        _, cols = idx
        ns = self.slots.shape[2]
        a = cols.start or 0
        b = self.shape[1] if cols.stop is None else cols.stop
        parts = [self.slots[k][:, max(a, k * ns) - k * ns:min(b, (k + 1) * ns) - k * ns]
                 for k in range(4) if max(a, k * ns) < min(b, (k + 1) * ns)]
        return parts[0] if len(parts) == 1 else jnp.concatenate(parts, axis=1)


def _col_slots(pieces):
    widths = [p.shape[1] for p in pieces]
    ns = sum(widths) // 4
    slots = []
    for k in range(4):
        lo, hi, off, parts = k * ns, (k + 1) * ns, 0, []
        for p, wd in zip(pieces, widths):
            if max(lo, off) < min(hi, off + wd):
                parts.append(p[:, max(lo, off) - off:min(hi, off + wd) - off])
            off += wd
        slots.append(parts[0] if len(parts) == 1 else jnp.concatenate(parts, axis=1))
    return jnp.stack(slots)


def _layer_dict(i, full):
    G, N, P = SSD_N_GROUPS, SSD_D_STATE, SSD_HEAD_DIM
    kind, j = i % 3, i // 3
    if kind == 0:
        H = full["ssd_dt_bias"][j].shape[0]
        DI = H * P
        CD = DI + 2 * G * N
        win = full["ssd_in_w"][j]
        return dict(wz=win[:, :DI], wxbc=win[:, DI:DI + CD], wdt=win[:, DI + CD:], conv_w=full["ssd_conv_w"][j],
                    conv_b=full["ssd_conv_b"][j][None], dt_bias=full["ssd_dt_bias"][j][None], a_log=full["ssd_a_log"][j][None],
                    d=full["ssd_d"][j][None], norm_w=full["ssd_norm_w"][j][None], wout=full["ssd_out_w"][j])
    if kind == 1:
        QR, KR = MLA_Q_RANK, MLA_KV_RANK
        win = full["mla_in_w"][j]
        Hh = full["mla_q_up_w"][j].shape[1] // (MLA_NOPE + MLA_ROPE)
        qup = full["mla_q_up_w"][j].reshape(QR, Hh, MLA_NOPE + MLA_ROPE)
        kvup = full["mla_kv_up_w"][j].reshape(KR, Hh, MLA_NOPE + MLA_V)
        return dict(wqc=win[:, :QR], wkvc=win[:, QR:QR + KR], wkr=_pad_lanes(win[:, QR + KR:QR + KR + MLA_ROPE]),
                    wz=win[:, QR + KR + MLA_ROPE:], q_norm=full["mla_q_norm_w"][j][None], kv_norm=full["mla_kv_norm_w"][j][None],
                    wqn=qup[:, :, :MLA_NOPE].reshape(QR, Hh * MLA_NOPE), wqr=_pad_lanes(qup[:, :, MLA_NOPE:]).reshape(QR, Hh * LANES),
                    wkn=kvup[:, :, :MLA_NOPE].reshape(KR, Hh * MLA_NOPE), wv=kvup[:, :, MLA_NOPE:].reshape(KR, Hh * MLA_V),
                    wout=full["mla_out_w"][j])
    KD, VD, HV = GDN_N_QK_HEADS * GDN_DK, GDN_N_V_HEADS * GDN_DV, GDN_N_V_HEADS
    win = full["gdn_in_w"][j]
    c0, c1 = 2 * KD + VD, 2 * KD + 2 * VD
    return dict(wqkv=win[:, :c0], wz=win[:, c0:c1], wb=win[:, c1:c1 + HV], wa=win[:, c1 + HV:], conv_w=full["gdn_conv_w"][j],
                a_log=full["gdn_a_log"][j][None], dt_bias=full["gdn_dt_bias"][j][None], norm_w=full["gdn_norm_w"][j][None],
                wout=full["gdn_out_w"][j])


def _layer_weights(full, D):
    return [_layer_dict(i, full) for i in range(DEPTH)]


def _layer_full_grads(i, g, slots=False):
    kind = i % 3
    join = _col_slots if slots else (lambda pieces: jnp.concatenate(pieces, axis=1))
    if kind == 0:
        out = {"ssd_in_w": join([g["wz"], g["wxbc"], g["wdt"]]), "ssd_conv_w": g["conv_w"], "ssd_out_w": g["wout"]}
        for n in ("conv_b", "dt_bias", "a_log", "d", "norm_w"):
            out["ssd_" + n] = g[n][0]
        return out
    if kind == 1:
        QR, KR = g["wqn"].shape[0], g["wkn"].shape[0]
        Hh = g["wqn"].shape[1] // MLA_NOPE
        return {"mla_in_w": join([g["wqc"], g["wkvc"], g["wkr"][:, :MLA_ROPE], g["wz"]]),
                "mla_q_up_w": jnp.concatenate([g["wqn"].reshape(QR, Hh, MLA_NOPE), g["wqr"].reshape(QR, Hh, LANES)[:, :, :MLA_ROPE]],
                                              axis=2).reshape(QR, -1),
                "mla_kv_up_w": jnp.concatenate([g["wkn"].reshape(KR, Hh, MLA_NOPE), g["wv"].reshape(KR, Hh, MLA_V)], axis=2).reshape(KR, -1),
                "mla_q_norm_w": g["q_norm"][0], "mla_kv_norm_w": g["kv_norm"][0], "mla_out_w": g["wout"]}
    out = {"gdn_in_w": join([g["wqkv"], g["wz"], g["wb"], g["wa"]]), "gdn_conv_w": g["conv_w"], "gdn_out_w": g["wout"]}
    for n in ("a_log", "dt_bias", "norm_w"):
        out["gdn_" + n] = g[n][0]
    return out


def _full_grads(grads, dg, db):
    per = {n: [] for n in _WEIGHTS}
    for i in range(DEPTH):
        for n, a in _layer_full_grads(i, grads[i]).items():
            per[n].append(a)
        per["ln_g"].append(dg[i][0])
        per["ln_b"].append(db[i][0])
    return {n: jnp.stack(v) for n, v in per.items()}


class _DistPlan:
    def __init__(self, w, chip, core):
        self.w, self.chip = w, chip
        self.chip_arr = jnp.reshape(chip, (1,)).astype(jnp.int32)
        self.core_arr = jnp.reshape(core, (1,)).astype(jnp.int32)
        self.full = {n: {} for n in _BIG}
        self.gkeys, self.g4, self.p4, self.fin, self.small_grads = {}, {}, {}, {}, [None] * DEPTH
        keys = [("ssd_in_w", 0)]
        shards = self._shards(keys)
        got = _run_comm(_GatherChips(shards + [_pack([w[n] for n in _SMALL_SHARDED])]), name="gather_l0")
        self._fill(keys, shards, got[:1])
        small = lax.dynamic_update_slice(got[1], _pack([w[n] for n in _SMALL_SHARDED])[None], (chip, 0, 0))
        parts = [_unpack(small[k], [w[n].shape for n in _SMALL_SHARDED]) for k in range(4)]
        for t, n in enumerate(_SMALL_SHARDED):
            self.full[n] = jnp.concatenate([parts[k][t] for k in range(4)], axis=-1)
        for n in _WEIGHTS:
            if n not in self.full:
                self.full[n] = w[n]

    @staticmethod
    def keys(i):
        names = [["ssd_in_w", "ssd_out_w"], ["mla_in_w", "mla_q_up_w", "mla_kv_up_w", "mla_out_w"], ["gdn_in_w", "gdn_out_w"]][i % 3]
        return [(n, i // 3) for n in names]

    def _shards(self, keys):
        return [self.w[n][j].astype(MXU_DT) for n, j in keys]

    def _fill(self, keys, shards, got):
        for (n, j), s, g in zip(keys, shards, got):
            g = lax.dynamic_update_slice(g, s[None], (self.chip, 0, 0))
            self.full[n][j] = _ColSlots(g) if n in _IN_PROJ else _gathered_to_full(g, _BIG[n], 1)[0]

    def weights(self, i):
        if i == 0:
            self.full["ssd_out_w"][0] = None
        self._w = _layer_dict(i, self.full)
        return self._w

    def _start_gather(self, keys):
        self._pending = (keys, self._shards(keys))
        return _GatherChips(self._pending[1])

    def _end_gather(self, res):
        self._fill(self._pending[0], self._pending[1], res)

    def fwd_early_comm(self, i):
        return self._start_gather([("ssd_out_w", 0)]) if i == 0 else None

    def fwd_early_done(self, i, res):
        self._end_gather(res)
        self._w["wout"] = self.full["ssd_out_w"][0]

    def fwd_comm(self, i):
        return self._start_gather(self.keys(i + 1)) if i + 1 < DEPTH else None

    def fwd_done(self, i, res):
        if res is not None:
            self._end_gather(res)

    def _slots(self, n, g):
        return g if n in _IN_PROJ else _full_to_slots(g[None], _BIG[n])

    def early_grad(self, i, name, g):
        if i == 0:
            self.gkeys[1].append((name, 0))
            self.g4[1].append(self._slots(name, g))

    def _make_group(self, i, gr):
        fg = _layer_full_grads(i, gr, slots=True)
        self.small_grads[i] = {n: a for n, a in fg.items() if n not in _BIG}
        self.gkeys[i] = [k for k in self.keys(i) if not (i == 0 and k[0] == "ssd_out_w")]
        self.g4[i] = [self._slots(n, fg[n]) for n, _ in self.gkeys[i]]

    def layer_grads(self, i, gr):
        if i > 0:
            self._make_group(i, gr)

    def tail_early_comm(self, i, gr):
        if i > 0:
            return None
        self._make_group(0, gr)
        return _PairSend(self.g4[0])

    def tail_early_done(self, i, res):
        self._pair_add(0, res)

    def tail_comm(self, i):
        return self._scatter(0) if i == 0 else None

    def tail_done(self, i, res):
        self._sum(0, res)

    def _pair_add(self, g, r1):
        self.p4[g] = [_add_half(a, b, self.core_arr, name="grad_pair_add_g%d_%s%d" % (g, n, j))
                      for a, b, (n, j) in zip(self.g4[g], r1, self.gkeys[g])]

    def bwd_early_comm(self, i):
        return _PairSend(self.g4[i + 1]) if i + 1 < DEPTH else None

    def bwd_early_done(self, i, res):
        if res is not None:
            self._pair_add(i + 1, res)

    def _scatter(self, g):
        return _ScatterChips([p[1] for p in self.p4[g]])

    def _sum(self, g, r2):
        self.fin[g] = [_sum_chips(p[0], b, self.chip_arr, name="grad_chip_sum_g%d_%s%d" % (g, n, j))
                       for p, b, (n, j) in zip(self.p4[g], r2, self.gkeys[g])]

    def bwd_comm(self, i):
        return self._scatter(i + 1) if i + 1 < DEPTH else None

    def bwd_done(self, i, res):
        if res is not None:
            self._sum(i + 1, res)

    def grad_shards(self, core):
        order = [(g, t) for g in range(DEPTH) for t in range(len(self.gkeys[g]))]
        fins = [self.fin[g][t] for g, t in order]
        got = _pair_exchange_halves(fins, name="grad_pair_share")
        got = [lax.dynamic_update_slice(a, f, (core * f.shape[0], 0)) for a, f in zip(got, fins)]
        per = {n: {} for n in _BIG}
        for (g, t), a in zip(order, got):
            n, j = self.gkeys[g][t]
            per[n][j] = a
        return {n: (v[0] if len(v) == 1 else jnp.concatenate([v[j] for j in sorted(v)], axis=0)) for n, v in per.items()}


def kernel(x, positions, ssd_in_w, ssd_conv_w, ssd_conv_b, ssd_dt_bias, ssd_a_log, ssd_d, ssd_norm_w, ssd_out_w, mla_in_w, mla_q_norm_w, mla_q_up_w, mla_kv_norm_w, mla_kv_up_w, mla_out_w, gdn_in_w, gdn_conv_w, gdn_a_log, gdn_dt_bias, gdn_norm_w, gdn_out_w, ln_g, ln_b, loss_target, m_ssd_in_w, m_ssd_conv_w, m_ssd_conv_b, m_ssd_dt_bias, m_ssd_a_log, m_ssd_d, m_ssd_norm_w, m_ssd_out_w, m_mla_in_w, m_mla_q_norm_w, m_mla_q_up_w, m_mla_kv_norm_w, m_mla_kv_up_w, m_mla_out_w, m_gdn_in_w, m_gdn_conv_w, m_gdn_a_log, m_gdn_dt_bias, m_gdn_norm_w, m_gdn_out_w, m_ln_g, m_ln_b, v_ssd_in_w, v_ssd_conv_w, v_ssd_conv_b, v_ssd_dt_bias, v_ssd_a_log, v_ssd_d, v_ssd_norm_w, v_ssd_out_w, v_mla_in_w, v_mla_q_norm_w, v_mla_q_up_w, v_mla_kv_norm_w, v_mla_kv_up_w, v_mla_out_w, v_gdn_in_w, v_gdn_conv_w, v_gdn_a_log, v_gdn_dt_bias, v_gdn_norm_w, v_gdn_out_w, v_ln_g, v_ln_b):
    args = dict(locals())
    w = {n: args[n] for n in _WEIGHTS}
    mom = {n: args["m_" + n] for n in _WEIGHTS}
    vel = {n: args["v_" + n] for n in _WEIGHTS}
    mx, my, mc = _me()
    chip = 2 * mx + my
    small = [n for n in _WEIGHTS if n not in _BIG]
    big = list(_BIG)

    plan = _DistPlan(w, chip, mc)
    loss, gx, dg, db = _local_step(x[0], positions[0], loss_target[0], [plan.full["ln_g"][i][None] for i in range(DEPTH)],
                                   [plan.full["ln_b"][i][None] for i in range(DEPTH)], plan)
    per = {n: [] for n in small}
    for i in range(DEPTH):
        for n, a in plan.small_grads[i].items():
            per[n].append(a)
        per["ln_g"].append(dg[i][0])
        per["ln_b"].append(db[i][0])
    fg = {n: jnp.stack(v) for n, v in per.items()}
    gsh = plan.grad_shards(mc)

    out_g, out_d, out_m, out_v = {}, {}, {}, {}
    for n in big:
        g = gsh[n]
        sh = w[n].shape
        to2 = lambda a: a.reshape(-1, sh[-1])
        d_, m_, v_ = _adamw(to2(w[n]), g, to2(mom[n]), to2(vel[n]), name="adamw_" + n)
        out_g[n], out_d[n], out_m[n], out_v[n] = g.reshape(sh), d_.reshape(sh), m_.reshape(sh), v_.reshape(sh)

    summed = _sum8(_allgather_all(_pack([fg[n] for n in small] + [loss[0, :1]]), name="gather_small"), name="sum_small")
    sg = _unpack(summed, [fg[n].shape for n in small] + [(1,)])
    loss_total = sg[-1][0]
    gs = {}
    for n, g in zip(small, sg[:-1]):
        if n in _SMALL_SHARDED:
            ws = w[n].shape[-1]
            g = lax.dynamic_slice_in_dim(g, chip * ws, ws, axis=g.ndim - 1)
        gs[n] = g
    shapes = [w[n].shape for n in small]
    d_, m_, v_ = _adamw(_pack([w[n] for n in small]), _pack([gs[n] for n in small]), _pack([mom[n] for n in small]),
                        _pack([vel[n] for n in small]), name="adamw_small")
    for n, a, b, c in zip(small, _unpack(d_, shapes), _unpack(m_, shapes), _unpack(v_, shapes)):
        out_g[n], out_d[n], out_m[n], out_v[n] = gs[n], a, b, c

    return (loss_total, gx[None], *[out_g[n] for n in _WEIGHTS], *[out_d[n] for n in _WEIGHTS],
            *[out_m[n] for n in _WEIGHTS], *[out_v[n] for n in _WEIGHTS])
```

```python
import functools
import math

import jax
import jax.numpy as jnp
from jax import lax
from jax.experimental import pallas as pl
from jax.experimental.pallas import tpu as pltpu

f32 = jnp.float32
HI = lax.Precision.HIGHEST
MXU_DT = jnp.bfloat16
COMM_DT = jnp.bfloat16
MESH = pl.DeviceIdType.MESH

DEPTH = 4
LN_EPS = 1e-5
RMS_EPS = 1e-6
SSD_HEAD_DIM = 64
SSD_N_GROUPS = 8
SSD_D_STATE = 128
SSD_CONV = 4
SSD_CHUNK = 128
MLA_Q_RANK = 768
MLA_KV_RANK = 512
MLA_NOPE = 128
MLA_ROPE = 64
MLA_V = 128
ROPE_THETA = 10000.0
GDN_N_QK_HEADS = 16
GDN_N_V_HEADS = 32
GDN_DK = 128
GDN_DV = 128
GDN_CONV = 4
GDN_CHUNK = 64
ADAM_LR = 0.001
ADAM_B1 = 0.9
ADAM_B2 = 0.999
ADAM_EPS = 1e-08
ADAM_WD = 0.01
ADAM_STEP = 10

LANES = 128
VMEM_LIMIT = 48 * 1024 * 1024
ATT_TILE = 512
ATT_Q_TILE = 512
ATT_KEY_TILE = 512
ROW_TILE = 256
MM_TILE_M = 1024
MM_TILE_N = 1024
MM_TILE_K = 2048
MM_VMEM_BUDGET = 40 * 1024 * 1024
SSD_GROUPS_PER_STEP = 2
GDN_HEADS_PER_STEP = 16
GDN_FWD_HEADS_PER_STEP = 32


def _alpha():
    return (2.0 * DEPTH) ** 0.25


def _tile(n, pref, align=LANES):
    t = min(pref, n) // align * align
    while t >= align:
        if n % t == 0:
            return t
        t -= align
    return n


def _cp(sem=None):
    return pltpu.CompilerParams(dimension_semantics=sem, vmem_limit_bytes=VMEM_LIMIT)


def _iota(shape, dim):
    return lax.broadcasted_iota(jnp.int32, shape, dim)


def _div_pow2(x, p):
    assert p & (p - 1) == 0
    return lax.shift_right_logical(x, jnp.int32(p.bit_length() - 1))


def _dot(a, b, dims=((1,), (0,)), hi=False):
    if hi:
        return lax.dot_general(a.astype(f32), b.astype(f32), (dims, ((), ())), precision=HI, preferred_element_type=f32)
    return lax.dot_general(a.astype(MXU_DT), b.astype(MXU_DT), (dims, ((), ())), preferred_element_type=f32)


_NT = ((1,), (1,))
_TN = ((0,), (0,))


def _split3(x):
    hi = x.astype(jnp.bfloat16)
    r = x - hi.astype(f32)
    mid = r.astype(jnp.bfloat16)
    return hi, mid, (r - mid.astype(f32)).astype(jnp.bfloat16)


def _seldot_impl(a, b, dims, exact):
    def d(x, y):
        return lax.dot_general(x, y, (dims, ((), ())), preferred_element_type=f32)

    if exact == 0:
        a01 = a.astype(jnp.bfloat16)
        t = _split3(b.astype(f32))
        return (d(a01, t[0]) + d(a01, t[1])) + d(a01, t[2])
    b01 = b.astype(jnp.bfloat16)
    t = _split3(a.astype(f32))
    return (d(t[0], b01) + d(t[1], b01)) + d(t[2], b01)


@functools.partial(jax.custom_vjp, nondiff_argnums=(2, 3))
def _seldot(a, b, dims, exact):
    return _seldot_impl(a, b, dims, exact)


def _seldot_fwd(a, b, dims, exact):
    return _seldot_impl(a, b, dims, exact), (a, b)


def _seldot_bwd(dims, exact, res, dy):
    a, b = res
    (ca,), (cb,) = dims
    if exact == 0:
        assert ca == 1
        db = _seldot_impl(a, dy, _TN, 0) if cb == 0 else _seldot_impl(dy, a, _TN, 1)
        return jnp.zeros_like(a), db
    assert ca == 1 and cb == 0
    return _seldot_impl(dy, b, _NT, 1), jnp.zeros_like(b)


_seldot.defvjp(_seldot_fwd, _seldot_bwd)


def _softplus(x):
    return jnp.maximum(x, 0.0) + jnp.log1p(jnp.exp(-jnp.abs(x)))


def _silu(x):
    return x * jax.nn.sigmoid(x)


def _mm(a, b, *, name, ta=False, tb=False, add=None, add_scale=1.0, out_dtype=f32, comm=None):
    M, K = (a.shape[1], a.shape[0]) if ta else a.shape
    N = b.shape[0] if tb else b.shape[1]
    assert (b.shape[1] if tb else b.shape[0]) == K, (a.shape, b.shape, ta, tb)
    tm, tn, tk = _tile(M, MM_TILE_M), _tile(N, MM_TILE_N), _tile(K, MM_TILE_K)
    ab, bb = jnp.dtype(a.dtype).itemsize, jnp.dtype(b.dtype).itemsize
    while 2 * tk * (tm * ab + tn * bb) + 12 * tm * tn > MM_VMEM_BUDGET and tk % (2 * LANES) == 0:
        tk //= 2
    nk = K // tk
    a_spec = pl.BlockSpec((tk, tm), lambda i, j, k: (k, i)) if ta else pl.BlockSpec((tm, tk), lambda i, j, k: (i, k))
    b_spec = pl.BlockSpec((tn, tk), lambda i, j, k: (j, k)) if tb else pl.BlockSpec((tk, tn), lambda i, j, k: (k, j))
    o_spec = pl.BlockSpec((tm, tn), lambda i, j, k: (i, j))
    dims = ((0 if ta else 1,), (1 if tb else 0,))
    has_add = add is not None

    def body(*refs):
        a_ref, b_ref = refs[:2]
        add_ref = refs[2] if has_add else None
        o_ref = refs[3 if has_add else 2]

        def finish(r):
            if has_add:
                r = r + add_scale * add_ref[...].astype(f32)
            o_ref[...] = r.astype(out_dtype)

        if nk == 1:
            finish(_dot(a_ref[...], b_ref[...], dims))
            return
        acc = refs[-1]
        k = pl.program_id(2)

        @pl.when(k == 0)
        def _():
            acc[...] = jnp.zeros_like(acc)

        acc[...] += _dot(a_ref[...], b_ref[...], dims)

        @pl.when(k == nk - 1)
        def _():
            finish(acc[...])

    ins = [a, b] + ([add] if has_add else [])
    specs = [a_spec, b_spec] + ([o_spec] if has_add else [])
    (out,), cres = _call(
        body, comm, name=name, grid=(M // tm, N // tn, nk), in_specs=specs, out_specs=[o_spec],
        out_shape=[jax.ShapeDtypeStruct((M, N), out_dtype)], scratch_shapes=[pltpu.VMEM((tm, tn), f32)] if nk > 1 else [],
        sem=("parallel", "parallel", "arbitrary"), args=ins)
    return out if comm is None else (out, cres)


def _rw_specs(params, ins, ncol, tm):
    specs = []
    for arr, mode, bw, coff in params:
        if mode == "c":
            specs.append(pl.BlockSpec((1, bw), lambda c, r, coff=coff: (0, c + coff)))
        else:
            specs.append(pl.BlockSpec((1, bw), lambda c, r, coff=coff: (0, coff)))
    for arr, mode, bw, coff in ins:
        if mode == "c":
            specs.append(pl.BlockSpec((tm, bw), lambda c, r, coff=coff: (r, c + coff)))
        else:
            specs.append(pl.BlockSpec((tm, bw), lambda c, r, coff=coff: (r, coff)))
    return specs


def _norm_spec(lst):
    out = []
    for t in lst:
        arr, mode, bw = t[0], t[1], t[2]
        coff = t[3] if len(t) > 3 else 0
        out.append((arr, mode, bw, coff))
    return out


def _rowwise(fn, params, ins, outs, *, name, ncol=1, tm=None):
    params, ins = _norm_spec(params), _norm_spec(ins)
    S = ins[0][0].shape[0]
    tm = _tile(S, tm or ROW_TILE, 8)
    npar, nin = len(params), len(ins)

    def body(*refs):
        pv = [r[...].astype(f32) for r in refs[:npar]]
        iv = [r[...].astype(f32) for r in refs[npar:npar + nin]]
        res = fn(*pv, *iv)
        for o_ref, val in zip(refs[npar + nin:], res):
            o_ref[...] = val.astype(o_ref.dtype)

    out_specs, out_shapes = [], []
    for W, dt, mode, bw in outs:
        out_shapes.append(jax.ShapeDtypeStruct((S, W), dt))
        if mode == "c":
            out_specs.append(pl.BlockSpec((tm, bw), lambda c, r: (r, c)))
        else:
            out_specs.append(pl.BlockSpec((tm, bw), lambda c, r: (r, 0)))
    return pl.pallas_call(
        body, name=name, grid=(ncol, S // tm), in_specs=_rw_specs(params, ins, ncol, tm), out_specs=out_specs,
        out_shape=out_shapes, compiler_params=_cp(("parallel", "parallel")))(*[p[0] for p in params], *[i[0] for i in ins])


def _rowwise_bwd(fn, params, ins, couts, *, name, ncol=1, tm=None, diff_p=None, diff_i=None, din_dtypes=None, comm=None):
    params, ins, couts = _norm_spec(params), _norm_spec(ins), _norm_spec(couts)
    S = ins[0][0].shape[0]
    tm = _tile(S, tm or ROW_TILE, 8)
    npar, nin, nco = len(params), len(ins), len(couts)
    diff_p = list(range(npar)) if diff_p is None else diff_p
    diff_i = list(range(nin)) if diff_i is None else diff_i
    din_dtypes = [(f32,)] * len(diff_i) if din_dtypes is None else din_dtypes

    def body(*refs):
        c, r = pl.program_id(0), pl.program_id(1)
        pv = [x[...].astype(f32) for x in refs[:npar]]
        iv = [x[...].astype(f32) for x in refs[npar:npar + nin]]
        cv = [x[...].astype(f32) for x in refs[npar + nin:npar + nin + nco]]
        orefs = refs[npar + nin + nco:]

        def g(*dargs):
            p2, i2 = list(pv), list(iv)
            for n, k in enumerate(diff_p):
                p2[k] = dargs[n]
            for n, k in enumerate(diff_i):
                i2[k] = dargs[len(diff_p) + n]
            return tuple(fn(*p2, *i2))

        _, vjp = jax.vjp(g, *[pv[k] for k in diff_p], *[iv[k] for k in diff_i])
        grads = vjp(tuple(cv))
        for n, k in enumerate(diff_p):
            o_ref = orefs[n]
            first = (r == 0) if params[k][1] == "c" else jnp.logical_and(r == 0, c == 0)

            @pl.when(first)
            def _(o_ref=o_ref):
                o_ref[...] = jnp.zeros_like(o_ref)

            o_ref[...] += grads[n]
        pos = len(diff_p)
        for n, k in enumerate(diff_i):
            for _ in din_dtypes[n]:
                orefs[pos][...] = grads[len(diff_p) + n].astype(orefs[pos].dtype)
                pos += 1

    out_specs, out_shapes = [], []
    for k in diff_p:
        arr, mode, bw, coff = params[k]
        W = bw * ncol if mode == "c" else bw
        out_shapes.append(jax.ShapeDtypeStruct((1, W), f32))
        out_specs.append(pl.BlockSpec((1, bw), (lambda c, r: (0, c)) if mode == "c" else (lambda c, r: (0, 0))))
    for n, k in enumerate(diff_i):
        arr, mode, bw, coff = ins[k]
        W = bw * ncol if mode == "c" else bw
        for dt in din_dtypes[n]:
            out_shapes.append(jax.ShapeDtypeStruct((S, W), dt))
            out_specs.append(pl.BlockSpec((tm, bw), (lambda c, r: (r, c)) if mode == "c" else (lambda c, r: (r, 0))))
    res, cres = _call(
        body, comm, name=name, grid=(ncol, S // tm), in_specs=_rw_specs(params, ins + couts, ncol, tm), out_specs=out_specs,
        out_shape=out_shapes, scratch_shapes=[], sem=("arbitrary", "arbitrary"),
        args=(*[p[0] for p in params], *[i[0] for i in ins], *[c[0] for c in couts]))
    if comm is None:
        return list(res[:len(diff_p)]), list(res[len(diff_p):])
    return list(res[:len(diff_p)]), list(res[len(diff_p):]), cres


def _ln_fn(g, b, r):
    mu = jnp.mean(r, -1, keepdims=True)
    xc = r - mu
    var = jnp.mean(xc * xc, -1, keepdims=True)
    return (xc * lax.rsqrt(var + LN_EPS) * g + b,)


def _res_ln_fn(g, b, h, y):
    r = _alpha() * h + y
    hn = _ln_fn(g, b, r)
    return (r,) + hn + hn


def _rms_fn(w, x):
    return (x * lax.rsqrt(jnp.mean(x * x, -1, keepdims=True) + RMS_EPS) * w,)


def _ssd_gate_fn(w, y, z):
    yg = y * _silu(z)
    return (yg * lax.rsqrt(jnp.mean(yg * yg, -1, keepdims=True) + RMS_EPS) * w,)


def _mul_silu_fn(o, z):
    return (o * _silu(z),)


def _gdn_gate_fn(w, o, z):
    return (o * lax.rsqrt(jnp.mean(o * o, -1, keepdims=True) + RMS_EPS) * w * _silu(z),)


def _l2_fn(scale, x):
    return (x * lax.rsqrt(jnp.sum(x * x, -1, keepdims=True) + RMS_EPS) * scale,)


def _rope_fn(cos, sin, x):
    half = MLA_ROPE // 2
    i = _iota((LANES, LANES), 0)
    j = _iota((LANES, LANES), 1)
    pm = jnp.where((i == j + half) & (j < half), -1.0, 0.0) + jnp.where((i + half == j) & (j < 2 * half), 1.0, 0.0)
    return (x * cos + _seldot(x, pm.astype(f32), ((1,), (0,)), 1) * sin,)


def _conv_taps(x, K):
    S = x.shape[0]
    rows = _iota(x.shape, 0)
    return [x] + [jnp.where(rows < j, 0.0, pltpu.roll(x, j, 0)) for j in range(1, K)]


def _conv_fwd(x, w, b, *, name):
    S, C = x.shape
    K = w.shape[0]
    cw = _tile(C, LANES)

    def body(x_ref, w_ref, b_ref, o_ref):
        taps = _conv_taps(x_ref[...], K)
        wv = w_ref[...]
        pre = b_ref[...] + taps[0] * wv[K - 1:K, :]
        for j in range(1, K):
            pre = pre + taps[j] * wv[K - 1 - j:K - j, :]
        o_ref[...] = _silu(pre)

    return pl.pallas_call(
        body, name=name, grid=(C // cw,),
        in_specs=[pl.BlockSpec((S, cw), lambda c: (0, c)), pl.BlockSpec((K, cw), lambda c: (0, c)), pl.BlockSpec((1, cw), lambda c: (0, c))],
        out_specs=pl.BlockSpec((S, cw), lambda c: (0, c)), out_shape=jax.ShapeDtypeStruct((S, C), f32),
        compiler_params=_cp(("parallel",)))(x, w, b)


def _ssd_grouped_block(c, G, GW, N, cw):
    assert N == cw and GW % cw == 0
    nq = GW // cw
    per, nx = nq + 2, G * nq
    in_x = lax.div(c, nq) * per + lax.rem(c, nq)
    return jnp.where(c < nx, in_x, jnp.where(c < nx + G, (c - nx) * per + nq, (c - nx - G) * per + nq + 1))


def _conv_bwd(x, w, b, dys, *, name, dx_dtype=f32, dy_block=None):
    S, C = x.shape
    K = w.shape[0]
    cw = _tile(C, LANES)
    nblk = [d.shape[1] // cw for d in dys]
    offs = [sum(nblk[:p]) for p in range(len(dys))]
    assert sum(nblk) == C // cw and all(d.shape[1] % cw == 0 for d in dys)
    npc = len(dys)
    assert dy_block is None or npc == 1

    def body(x_ref, w_ref, b_ref, *refs):
        dy_refs, (dx_ref, dw_ref, db_ref) = refs[:npc], refs[npc:]
        if npc == 1:
            _conv_bwd_block(x_ref, w_ref, b_ref, dy_refs[0], dx_ref, dw_ref, db_ref, K, S, dx_dtype)
            return
        c = pl.program_id(0)
        for p in range(npc):
            @pl.when(jnp.logical_and(c >= offs[p], c < offs[p] + nblk[p]))
            def _(p=p):
                _conv_bwd_block(x_ref, w_ref, b_ref, dy_refs[p], dx_ref, dw_ref, db_ref, K, S, dx_dtype)

    col = lambda c: (0, c)
    if dy_block is not None:
        dy_specs = [pl.BlockSpec((S, cw), lambda c: (0, dy_block(c, cw)))]
    else:
        dy_specs = [pl.BlockSpec((S, cw), lambda c, o=offs[p], n=nblk[p]: (0, jnp.clip(c - o, 0, n - 1))) for p in range(npc)]
    return pl.pallas_call(
        body, name=name, grid=(C // cw,),
        in_specs=[pl.BlockSpec((S, cw), col), pl.BlockSpec((K, cw), col), pl.BlockSpec((1, cw), col)] + dy_specs,
        out_specs=[pl.BlockSpec((S, cw), col), pl.BlockSpec((K, cw), col), pl.BlockSpec((1, cw), col)],
        out_shape=[jax.ShapeDtypeStruct((S, C), dx_dtype), jax.ShapeDtypeStruct((K, C), f32), jax.ShapeDtypeStruct((1, C), f32)],
        compiler_params=_cp(("parallel",)))(x, w, b, *dys)


def _conv_bwd_block(x_ref, w_ref, b_ref, dy_ref, dx_ref, dw_ref, db_ref, K, S, dx_dtype):
    taps = _conv_taps(x_ref[...], K)
    wv = w_ref[...]
    pre = b_ref[...] + taps[0] * wv[K - 1:K, :]
    for j in range(1, K):
        pre = pre + taps[j] * wv[K - 1 - j:K - j, :]
    sg = jax.nn.sigmoid(pre)
    dpre = dy_ref[...] * sg * (1.0 + pre * (1.0 - sg))
    db_ref[...] = jnp.sum(dpre, axis=0, keepdims=True)
    rows = _iota(dpre.shape, 0)
    dx = dpre * wv[K - 1:K, :]
    dw_ref[K - 1:K, :] = jnp.sum(dpre * taps[0], axis=0, keepdims=True)
    for j in range(1, K):
        dw_ref[K - 1 - j:K - j, :] = jnp.sum(dpre * taps[j], axis=0, keepdims=True)
        up = jnp.where(rows >= S - j, 0.0, pltpu.roll(dpre, S - j, 0))
        dx = dx + up * wv[K - 1 - j:K - j, :]
    dx_ref[...] = dx.astype(dx_dtype)


def _ssd_chunk(prev, xs, Bm, Cm, dtr, dtb, alog, dsk, g, *, R, P):
    L, GW = xs.shape
    H = dtr.shape[1]
    tril = _iota((L, L), 0) >= _iota((L, L), 1)
    dt = _softplus(dtr + dtb)
    acs = _seldot(tril.astype(f32), dt * (-jnp.exp(alog)), ((1,), (0,)), 0)
    expand = (_iota((H, GW), 0) == g * R + _div_pow2(_iota((H, GW), 1), P)).astype(f32)
    dt_e = _seldot(dt, expand, ((1,), (0,)), 1)
    acs_e = _seldot(acs, expand, ((1,), (0,)), 1)
    d_e = jnp.sum(_seldot(jnp.broadcast_to(dsk, (8, H)), expand, ((1,), (0,)), 1), axis=0, keepdims=True) * 0.125
    last = jnp.sum(jnp.where(_iota((L, GW), 0) == L - 1, acs_e, 0.0), axis=0, keepdims=True)
    xdt = xs * dt_e
    cb = _dot(Cm, Bm, _NT)
    nsel = max(R, 8)
    sel = (_iota((nsel, H), 1) == g * R + _iota((nsel, H), 0)).astype(f32)
    acs_t = _seldot(sel, acs, _NT, 0)
    hp = LANES // P
    pieces = []
    for p in range(GW // LANES):
        xp = xdt[:, p * LANES:(p + 1) * LANES]
        acc = None
        for q in range(hp):
            r = p * hp + q
            col = jnp.sum(jnp.where(_iota((L, H), 1) == g * R + r, acs, 0.0), axis=1, keepdims=True)
            row = jnp.sum(jnp.where(_iota((nsel, L), 0) == r, acs_t, 0.0), axis=0, keepdims=True)
            dec = jnp.where(tril, jnp.exp(jnp.where(tril, col - row, 0.0)), 0.0)
            xm = jnp.where(_div_pow2(_iota((L, LANES), 1), P) == q, xp, 0.0)
            t = _dot(cb * dec, xm)
            acc = t if acc is None else acc + t
        pieces.append(acc)
    y_diag = pieces[0] if len(pieces) == 1 else jnp.concatenate(pieces, axis=1)
    st = _dot(Bm, xdt * jnp.exp(last - acs_e), _TN)
    y_off = _dot(Cm, prev) * jnp.exp(acs_e)
    new = prev * jnp.exp(last) + st
    return y_diag + y_off + xs * d_e, new


def _ssd_dims(xbc, dtr):
    S, CD = xbc.shape
    H = dtr.shape[1]
    G, N, P = SSD_N_GROUPS, SSD_D_STATE, SSD_HEAD_DIM
    DI = H * P
    R = H // G
    assert CD == DI + 2 * G * N and DI % N == 0
    return S, H, G, N, P, DI, R, R * P, SSD_CHUNK


def _ssd_groups_per_step(G, DI, N):
    GB = min(SSD_GROUPS_PER_STEP, G)
    assert G % GB == 0 and (DI // N) % GB == 0
    return GB


def _ssd_scan_fwd(xbc, dtr, dtb, alog, dsk, *, name, comm=None):
    S, H, G, N, P, DI, R, GW, L = _ssd_dims(xbc, dtr)
    nc = S // L
    GB = _ssd_groups_per_step(G, DI, N)
    boff, coff = DI // N // GB, (DI // N + G) // GB

    def body(xs_ref, b_ref, c_ref, dtr_ref, dtb_ref, alog_ref, dsk_ref, y_ref, st_ref, state):
        c, gb = pl.program_id(0), pl.program_id(1)
        for gg in range(GB):
            g = gb * GB + gg

            @pl.when(c == 0)
            def _(g=g):
                state[g] = jnp.zeros((N, GW), f32)

            prev = state[g]
            st_ref[0, gg] = prev
            y, new = _ssd_chunk(prev, xs_ref[:, gg * GW:(gg + 1) * GW], b_ref[:, gg * N:(gg + 1) * N], c_ref[:, gg * N:(gg + 1) * N],
                                dtr_ref[...], dtb_ref[...], alog_ref[...], dsk_ref[...], g, R=R, P=P)
            y_ref[:, gg * GW:(gg + 1) * GW] = y
            state[g] = new

    par = pl.BlockSpec((1, H), lambda c, g: (0, 0))
    return _call(
        body, comm, name=name, grid=(nc, G // GB),
        in_specs=[pl.BlockSpec((L, GB * GW), lambda c, g: (c, g)), pl.BlockSpec((L, GB * N), lambda c, g: (c, boff + g)),
                  pl.BlockSpec((L, GB * N), lambda c, g: (c, coff + g)), pl.BlockSpec((L, H), lambda c, g: (c, 0)), par, par, par],
        out_specs=[pl.BlockSpec((L, GB * GW), lambda c, g: (c, g)), pl.BlockSpec((1, GB, N, GW), lambda c, g: (c, g, 0, 0))],
        out_shape=[jax.ShapeDtypeStruct((S, DI), f32), jax.ShapeDtypeStruct((nc, G, N, GW), f32)],
        scratch_shapes=[pltpu.VMEM((G, N, GW), f32)],
        sem=("arbitrary", "arbitrary"), args=(xbc, xbc, xbc, dtr, dtb, alog, dsk))


def _ssd_scan_bwd(xbc, dtr, dtb, alog, dsk, states, dy, *, name, comm=None):
    S, H, G, N, P, DI, R, GW, L = _ssd_dims(xbc, dtr)
    nc = S // L
    GB = _ssd_groups_per_step(G, DI, N)
    boff, coff = DI // N // GB, (DI // N + G) // GB
    PW = GW + 2 * N

    def body(xs_ref, b_ref, c_ref, dtr_ref, dtb_ref, alog_ref, dsk_ref, st_ref, dy_ref,
             dg_ref, ddtr_ref, ddtb_ref, dalog_ref, ddsk_ref, dstate):
        c, gb = pl.program_id(0), pl.program_id(1)

        @pl.when(jnp.logical_and(c == 0, gb == 0))
        def _():
            ddtb_ref[...] = jnp.zeros_like(ddtb_ref)
            dalog_ref[...] = jnp.zeros_like(dalog_ref)
            ddsk_ref[...] = jnp.zeros_like(ddsk_ref)

        @pl.when(gb == 0)
        def _():
            ddtr_ref[...] = jnp.zeros_like(ddtr_ref)

        for gg in range(GB):
            g = gb * GB + gg

            @pl.when(c == 0)
            def _(g=g):
                dstate[g] = jnp.zeros((N, GW), f32)

            fn = functools.partial(_ssd_chunk, g=g, R=R, P=P)
            _, vjp = jax.vjp(fn, st_ref[0, gg], xs_ref[:, gg * GW:(gg + 1) * GW], b_ref[:, gg * N:(gg + 1) * N],
                             c_ref[:, gg * N:(gg + 1) * N], dtr_ref[...], dtb_ref[...], alog_ref[...], dsk_ref[...])
            dprev, dxs, dB, dC, ddtr, ddtb, dalog, ddsk = vjp((dy_ref[:, gg * GW:(gg + 1) * GW], dstate[g]))
            dstate[g] = dprev
            dg_ref[:, gg * PW:gg * PW + GW] = dxs
            dg_ref[:, gg * PW + GW:gg * PW + GW + N] = dB
            dg_ref[:, gg * PW + GW + N:(gg + 1) * PW] = dC
            ddtr_ref[...] += ddtr
            ddtb_ref[...] += ddtb
            dalog_ref[...] += dalog
            ddsk_ref[...] += ddsk

    rc = lambda c: nc - 1 - c
    par = pl.BlockSpec((1, H), lambda c, g: (0, 0))
    return _call(
        body, comm, name=name, grid=(nc, G // GB),
        in_specs=[pl.BlockSpec((L, GB * GW), lambda c, g: (rc(c), g)), pl.BlockSpec((L, GB * N), lambda c, g: (rc(c), boff + g)),
                  pl.BlockSpec((L, GB * N), lambda c, g: (rc(c), coff + g)), pl.BlockSpec((L, H), lambda c, g: (rc(c), 0)),
                  par, par, par, pl.BlockSpec((1, GB, N, GW), lambda c, g: (rc(c), g, 0, 0)),
                  pl.BlockSpec((L, GB * GW), lambda c, g: (rc(c), g))],
        out_specs=[pl.BlockSpec((L, GB * PW), lambda c, g: (rc(c), g)), pl.BlockSpec((L, H), lambda c, g: (rc(c), 0)), par, par, par],
        out_shape=[jax.ShapeDtypeStruct((S, G * (GW + 2 * N)), f32), jax.ShapeDtypeStruct((S, H), f32)] + [jax.ShapeDtypeStruct((1, H), f32)] * 3,
        scratch_shapes=[pltpu.VMEM((G, N, GW), f32)],
        sem=("arbitrary", "arbitrary"), args=(xbc, xbc, xbc, dtr, dtb, alog, dsk, states, dy))


def _dot3(a, b, dims=((1,), (0,))):
    def split(x):
        hi = x.astype(jnp.bfloat16)
        return hi, (x - hi.astype(f32)).astype(jnp.bfloat16)

    def d(x, y):
        return lax.dot_general(x, y, (dims, ((), ())), preferred_element_type=f32)

    ah, al = split(a)
    bh, bl = split(b)
    return d(ah, bh) + (d(ah, bl) + d(al, bh))


def _neumann_inverses(As):
    L = As[0].shape[0]
    eye = (_iota((L, L), 0) == _iota((L, L), 1)).astype(f32)
    X = [-A for A in As]
    P = [eye + x for x in X]
    n = 1
    while 2 * n < L:
        X = [_dot3(x, x) for x in X]
        P = [p + _dot3(p, x) for p, x in zip(P, X)]
        n *= 2
    return P


@jax.custom_vjp
def _unit_lower_solves(Ts, As, Rs):
    return tuple(_dot3(T, R) for T, R in zip(Ts, Rs))


def _uls_fwd(Ts, As, Rs):
    Xs = tuple(_dot3(T, R) for T, R in zip(Ts, Rs))
    return Xs, (Ts, Xs)


def _uls_bwd(res, dXs):
    Ts, Xs = res
    dRs = tuple(_dot3(T, dX, _TN) for T, dX in zip(Ts, dXs))
    dAs = tuple(-_dot3(dR, X, _NT) for dR, X in zip(dRs, Xs))
    return tuple(jnp.zeros_like(T) for T in Ts), dAs, dRs


_unit_lower_solves.defvjp(_uls_fwd, _uls_bwd)


def _gdn_step(states, qb, kb_, vb, br, ar, alog, dtb, h0, *, rep, inverses=None):
    HB = len(states)
    L = qb.shape[0]
    DK, DV = states[0].shape
    HV = br.shape[1]
    incl = _iota((L, L), 0) >= _iota((L, L), 1)
    strict = _iota((L, L), 0) > _iota((L, L), 1)
    lane = _iota((L, HV), 1)
    g_all = -jnp.exp(alog) * _softplus(ar + dtb)
    gcs = _seldot(incl.astype(f32), g_all, ((1,), (0,)), 0)
    beta_all = jax.nn.sigmoid(br)
    nsel = max(HV, 8)
    gcs_t = _seldot((_iota((nsel, HV), 0) == _iota((nsel, HV), 1)).astype(f32), gcs, _NT, 0)
    hs = range(HB)
    q = [qb[:, (hh // rep) * DK:(hh // rep + 1) * DK] for hh in hs]
    k = [kb_[:, (hh // rep) * DK:(hh // rep + 1) * DK] for hh in hs]
    v = [vb[:, hh * DV:(hh + 1) * DV] for hh in hs]
    gc = [jnp.sum(jnp.where(lane == h0 + hh, gcs, 0.0), axis=1, keepdims=True) for hh in hs]
    beta = [jnp.sum(jnp.where(lane == h0 + hh, beta_all, 0.0), axis=1, keepdims=True) for hh in hs]
    gc_row = [jnp.sum(jnp.where(_iota((nsel, L), 0) == h0 + hh, gcs_t, 0.0), axis=0, keepdims=True) for hh in hs]
    decay = [jnp.where(incl, jnp.exp(jnp.where(incl, gc[hh] - gc_row[hh], 0.0)), 0.0) for hh in hs]
    kbeta = [k[hh] * beta[hh] for hh in hs]
    a_mat = [jnp.where(strict, _dot(kbeta[hh], k[hh], _NT) * decay[hh], 0.0) for hh in hs]
    eg = [jnp.exp(gc[hh]) for hh in hs]
    rhs = tuple(jnp.concatenate([v[hh] * beta[hh], kbeta[hh] * eg[hh]], axis=1) for hh in hs)
    if inverses is None:
        made = tuple(_neumann_inverses(a_mat))
        sol = tuple(_dot3(T, R) for T, R in zip(made, rhs))
    else:
        sol = _unit_lower_solves(tuple(inverses), tuple(a_mat), rhs)
    qk = [jnp.where(incl, _dot(q[hh], k[hh], _NT) * decay[hh], 0.0) for hh in hs]
    g_last = [jnp.sum(jnp.where(_iota((L, 1), 0) == L - 1, gc[hh], 0.0), axis=0, keepdims=True) for hh in hs]
    v_new = [sol[hh][:, :DV] - _dot(sol[hh][:, DV:], states[hh]) for hh in hs]
    outs = [_dot(q[hh] * eg[hh], states[hh]) + _dot(qk[hh], v_new[hh]) for hh in hs]
    news = [states[hh] * jnp.exp(g_last[hh]) + _dot(k[hh] * jnp.exp(g_last[hh] - gc[hh]), v_new[hh], _TN) for hh in hs]
    o = outs[0] if HB == 1 else jnp.concatenate(outs, axis=1)
    return (o, tuple(news), made) if inverses is None else (o, tuple(news))


def _gdn_dims(heads_per_step):
    HK, HV = GDN_N_QK_HEADS, GDN_N_V_HEADS
    rep = HV // HK
    HB = min(heads_per_step, HV)
    assert HV % HB == 0 and HB % rep == 0
    return HK, HV, GDN_DK, GDN_DV, GDN_CHUNK, rep, HB


def _gdn_scan_fwd(qkn, qkv, br, ar, alog, dtb, *, name, comm=None):
    S = qkn.shape[0]
    HK, HV, DK, DV, L, rep, HB = _gdn_dims(GDN_FWD_HEADS_PER_STEP)
    nc = S // L
    QW = HB // rep * DK
    koff = HK * DK // QW
    voff = 2 * HK * DK // (HB * DV)

    def body(q_ref, k_ref, v_ref, br_ref, ar_ref, alog_ref, dtb_ref, o_ref, st_ref, inv_ref, state):
        c, hb = pl.program_id(0), pl.program_id(1)
        h0 = hb * HB

        @pl.when(c == 0)
        def _():
            for hh in range(HB):
                state[h0 + hh] = jnp.zeros((DK, DV), f32)

        prev = tuple(state[h0 + hh] for hh in range(HB))
        for hh in range(HB):
            st_ref[0, hh] = prev[hh]
        o, new, inv = _gdn_step(prev, q_ref[...], k_ref[...], v_ref[...], br_ref[...], ar_ref[...], alog_ref[...], dtb_ref[...],
                                h0, rep=rep)
        o_ref[...] = o
        for hh in range(HB):
            state[h0 + hh] = new[hh]
            inv_ref[0, hh] = inv[hh]

    par = pl.BlockSpec((1, HV), lambda c, h: (0, 0))
    return _call(
        body, comm, name=name, grid=(nc, HV // HB),
        in_specs=[pl.BlockSpec((L, QW), lambda c, h: (c, h)), pl.BlockSpec((L, QW), lambda c, h: (c, koff + h)),
                  pl.BlockSpec((L, HB * DV), lambda c, h: (c, voff + h)), pl.BlockSpec((L, HV), lambda c, h: (c, 0)),
                  pl.BlockSpec((L, HV), lambda c, h: (c, 0)), par, par],
        out_specs=[pl.BlockSpec((L, HB * DV), lambda c, h: (c, h)), pl.BlockSpec((1, HB, DK, DV), lambda c, h: (c, h, 0, 0)),
                   pl.BlockSpec((1, HB, L, L), lambda c, h: (c, h, 0, 0))],
        out_shape=[jax.ShapeDtypeStruct((S, HV * DV), f32), jax.ShapeDtypeStruct((nc, HV, DK, DV), f32),
                   jax.ShapeDtypeStruct((nc, HV, L, L), f32)],
        scratch_shapes=[pltpu.VMEM((HV, DK, DV), f32)],
        sem=("arbitrary", "arbitrary"), args=(qkn, qkn, qkv, br, ar, alog, dtb))


def _gdn_scan_bwd(qkn, qkv, br, ar, alog, dtb, states, inverses, do, *, name, comm=None):
    S = qkn.shape[0]
    HK, HV, DK, DV, L, rep, HB = _gdn_dims(GDN_HEADS_PER_STEP)
    nc = S // L
    QW = HB // rep * DK
    koff = HK * DK // QW
    voff = 2 * HK * DK // (HB * DV)

    def body(q_ref, k_ref, v_ref, br_ref, ar_ref, alog_ref, dtb_ref, st_ref, inv_ref, do_ref,
             dq_ref, dk_ref, dv_ref, dbr_ref, dar_ref, dalog_ref, ddtb_ref, dstate):
        c, hb = pl.program_id(0), pl.program_id(1)
        h0 = hb * HB

        @pl.when(c == 0)
        def _():
            for hh in range(HB):
                dstate[h0 + hh] = jnp.zeros((DK, DV), f32)

        @pl.when(jnp.logical_and(c == 0, hb == 0))
        def _():
            dalog_ref[...] = jnp.zeros_like(dalog_ref)
            ddtb_ref[...] = jnp.zeros_like(ddtb_ref)

        @pl.when(hb == 0)
        def _():
            dbr_ref[...] = jnp.zeros_like(dbr_ref)
            dar_ref[...] = jnp.zeros_like(dar_ref)

        fn = functools.partial(_gdn_step, h0=h0, rep=rep, inverses=tuple(inv_ref[0, hh] for hh in range(HB)))
        prev = tuple(st_ref[0, hh] for hh in range(HB))
        _, vjp = jax.vjp(fn, prev, q_ref[...], k_ref[...], v_ref[...], br_ref[...], ar_ref[...], alog_ref[...], dtb_ref[...])
        dprev, dq, dk, dv, dbr, dar, dalog, ddtb = vjp((do_ref[...], tuple(dstate[h0 + hh] for hh in range(HB))))
        for hh in range(HB):
            dstate[h0 + hh] = dprev[hh]
        dq_ref[...] = dq
        dk_ref[...] = dk
        dv_ref[...] = dv
        dbr_ref[...] += dbr
        dar_ref[...] += dar
        dalog_ref[...] += dalog
        ddtb_ref[...] += ddtb

    rc = lambda c: nc - 1 - c
    par = pl.BlockSpec((1, HV), lambda c, h: (0, 0))
    blk = lambda W: pl.BlockSpec((L, W), lambda c, h: (rc(c), h))
    return _call(
        body, comm, name=name, grid=(nc, HV // HB),
        in_specs=[pl.BlockSpec((L, QW), lambda c, h: (rc(c), h)), pl.BlockSpec((L, QW), lambda c, h: (rc(c), koff + h)),
                  pl.BlockSpec((L, HB * DV), lambda c, h: (rc(c), voff + h)), pl.BlockSpec((L, HV), lambda c, h: (rc(c), 0)),
                  pl.BlockSpec((L, HV), lambda c, h: (rc(c), 0)), par, par,
                  pl.BlockSpec((1, HB, DK, DV), lambda c, h: (rc(c), h, 0, 0)),
                  pl.BlockSpec((1, HB, L, L), lambda c, h: (rc(c), h, 0, 0)), blk(HB * DV)],
        out_specs=[blk(QW), blk(QW), blk(HB * DV), pl.BlockSpec((L, HV), lambda c, h: (rc(c), 0)),
                   pl.BlockSpec((L, HV), lambda c, h: (rc(c), 0)), par, par],
        out_shape=[jax.ShapeDtypeStruct((S, HK * DK), f32), jax.ShapeDtypeStruct((S, HK * DK), f32), jax.ShapeDtypeStruct((S, HV * DV), f32),
                   jax.ShapeDtypeStruct((S, HV), f32), jax.ShapeDtypeStruct((S, HV), f32),
                   jax.ShapeDtypeStruct((1, HV), f32), jax.ShapeDtypeStruct((1, HV), f32)],
        scratch_shapes=[pltpu.VMEM((HV, DK, DV), f32)],
        sem=("arbitrary", "arbitrary"), args=(qkn, qkn, qkv, br, ar, alog, dtb, states, inverses, do))


def _att_scale():
    return (MLA_NOPE + MLA_ROPE) ** -0.5


def _causal(s, i, j, t, tk=None):
    qpos = i * t + _iota(s.shape, 0)
    kpos = j * (t if tk is None else tk) + _iota(s.shape, 1)
    return kpos <= qpos


def _attn_fwd(qn, qr, kn, kr, v, *, name, comm=None):
    S, W = qn.shape
    H = W // LANES
    t = _tile(S, ATT_Q_TILE)
    tk = _tile(S, ATT_KEY_TILE)
    assert tk % t == 0
    scale = _att_scale()

    def body(qn_ref, qr_ref, kn_ref, kr_ref, v_ref, o_ref, lse_ref):
        i = pl.program_id(1)
        qc = jnp.concatenate([qn_ref[...], qr_ref[...]], axis=1)

        def step(j, carry, masked):
            m, l, acc = carry
            rows = pl.ds(pl.multiple_of(j * tk, tk), tk)
            s = _dot(qc, jnp.concatenate([kn_ref[rows, :], kr_ref[rows, :]], axis=1), _NT) * scale
            if masked:
                s = jnp.where(_causal(s, i, j, t, tk), s, -1e30)
            m_new = jnp.maximum(m, jnp.max(s, axis=1, keepdims=True))
            p = jnp.exp(s - m_new)
            a = jnp.exp(m - m_new)
            return m_new, a * l + jnp.sum(p, axis=1, keepdims=True), a * acc + _dot(p, v_ref[rows, :])

        nfull = lax.div(i * t, tk)
        carry = lax.fori_loop(0, nfull, functools.partial(step, masked=False),
                              (jnp.full((t, 1), -1e30, f32), jnp.zeros((t, 1), f32), jnp.zeros((t, LANES), f32)))
        m, l, acc = step(nfull, carry, True)
        o_ref[...] = acc / l
        lse_ref[...] = jnp.broadcast_to(m + jnp.log(l), (t, LANES))

    qb = pl.BlockSpec((t, LANES), lambda h, i: (i, h))
    kb = pl.BlockSpec((S, LANES), lambda h, i: (0, h))
    return _call(
        body, comm, name=name, grid=(H, S // t),
        in_specs=[qb, qb, kb, pl.BlockSpec((S, LANES), lambda h, i: (0, 0)), kb],
        out_specs=[qb, qb], out_shape=[jax.ShapeDtypeStruct((S, W), f32), jax.ShapeDtypeStruct((S, W), f32)],
        scratch_shapes=[], sem=("parallel", "arbitrary"), args=(qn, qr, kn, kr, v))


def _attn_bwd_dq(qn, qr, kn, kr, v, o, lse, do, *, name, comm=None):
    S, W = qn.shape
    H = W // LANES
    t = _tile(S, ATT_TILE)
    scale = _att_scale()

    def body(qn_ref, qr_ref, kn_ref, kr_ref, v_ref, o_ref, lse_ref, do_ref, dqn_ref, dqr_ref):
        i = pl.program_id(1)
        qc = jnp.concatenate([qn_ref[...], qr_ref[...]], axis=1)
        dov = do_ref[...]
        delta = jnp.sum(dov * o_ref[...], axis=1, keepdims=True)
        lsev = lse_ref[...][:, :1]

        def step(j, dq, masked):
            rows = pl.ds(pl.multiple_of(j * t, t), t)
            kc = jnp.concatenate([kn_ref[rows, :], kr_ref[rows, :]], axis=1)
            s = _dot(qc, kc, _NT) * scale
            p = jnp.exp(s - lsev)
            if masked:
                p = jnp.where(_causal(s, i, j, t), p, 0.0)
            ds = p * (_dot(dov, v_ref[rows, :], _NT) - delta)
            return dq + _dot(ds, kc)

        dq = lax.fori_loop(0, i, functools.partial(step, masked=False), jnp.zeros((t, 2 * LANES), f32))
        dq = step(i, dq, True)
        dq = dq * scale
        dqn_ref[...] = dq[:, :LANES].astype(MXU_DT)
        dqr_ref[...] = dq[:, LANES:]

    qb = pl.BlockSpec((t, LANES), lambda h, i: (i, h))
    kb = pl.BlockSpec((S, LANES), lambda h, i: (0, h))
    return _call(
        body, comm, name=name, grid=(H, S // t),
        in_specs=[qb, qb, kb, pl.BlockSpec((S, LANES), lambda h, i: (0, 0)), kb, qb, qb, qb],
        out_specs=[qb, qb], out_shape=[jax.ShapeDtypeStruct((S, W), MXU_DT), jax.ShapeDtypeStruct((S, W), f32)],
        scratch_shapes=[], sem=("parallel", "arbitrary"), args=(qn, qr, kn, kr, v, o, lse, do))


def _attn_bwd_dkv(qn, qr, kn, kr, v, o, lse, do, *, name, comm=None):
    S, W = qn.shape
    H = W // LANES
    t = _tile(S, ATT_TILE)
    nb = S // t
    scale = _att_scale()

    def body(qn_ref, qr_ref, kn_ref, kr_ref, v_ref, o_ref, lse_ref, do_ref, dkn_ref, dkr_ref, dv_ref):
        j, h = pl.program_id(0), pl.program_id(1)
        kc = jnp.concatenate([kn_ref[...], kr_ref[...]], axis=1)
        vv = v_ref[...]

        def step(i, carry, masked):
            dk, dv = carry
            rows = pl.ds(pl.multiple_of(i * t, t), t)
            qc = jnp.concatenate([qn_ref[rows, :], qr_ref[rows, :]], axis=1)
            dov = do_ref[rows, :]
            delta = jnp.sum(dov * o_ref[rows, :], axis=1, keepdims=True)
            s = _dot(qc, kc, _NT) * scale
            p = jnp.exp(s - lse_ref[rows, :][:, :1])
            if masked:
                p = jnp.where(_causal(s, i, j, t), p, 0.0)
            ds = p * (_dot(dov, vv, _NT) - delta)
            return dk + _dot(ds, qc, _TN), dv + _dot(p, dov, _TN)

        carry = step(j, (jnp.zeros((t, 2 * LANES), f32), jnp.zeros((t, LANES), f32)), True)
        dk, dv = lax.fori_loop(j + 1, nb, functools.partial(step, masked=False), carry)
        dk = dk * scale
        dkn_ref[...] = dk[:, :LANES].astype(MXU_DT)
        dv_ref[...] = dv.astype(MXU_DT)

        @pl.when(h == 0)
        def _():
            dkr_ref[...] = jnp.zeros_like(dkr_ref)

        dkr_ref[...] += dk[:, LANES:]

    full = pl.BlockSpec((S, LANES), lambda j, h: (0, h))
    kb = pl.BlockSpec((t, LANES), lambda j, h: (j, h))
    k0 = pl.BlockSpec((t, LANES), lambda j, h: (j, 0))
    return _call(
        body, comm, name=name, grid=(nb, H),
        in_specs=[full, full, kb, k0, kb, full, full, full],
        out_specs=[kb, k0, kb],
        out_shape=[jax.ShapeDtypeStruct((S, W), MXU_DT), jax.ShapeDtypeStruct((S, LANES), f32), jax.ShapeDtypeStruct((S, W), MXU_DT)],
        scratch_shapes=[], sem=("arbitrary", "arbitrary"), args=(qn, qr, kn, kr, v, o, lse, do))


def _loss_head(y, target, *, name):
    S, D = y.shape
    tm = _tile(S, ROW_TILE, 8)

    def body(y_ref, t_ref, loss_ref, dy_ref):
        @pl.when(pl.program_id(0) == 0)
        def _():
            loss_ref[...] = jnp.zeros_like(loss_ref)

        e = y_ref[...] - t_ref[...]
        dy_ref[...] = e / D
        part = 0.5 * jnp.sum(jnp.mean(e * e, axis=1, keepdims=True), axis=0, keepdims=True)
        loss_ref[...] += jnp.broadcast_to(part, loss_ref.shape)

    rb = pl.BlockSpec((tm, D), lambda r: (r, 0))
    return pl.pallas_call(
        body, name=name, grid=(S // tm,), in_specs=[rb, rb],
        out_specs=[pl.BlockSpec((1, LANES), lambda r: (0, 0)), rb],
        out_shape=[jax.ShapeDtypeStruct((1, LANES), f32), jax.ShapeDtypeStruct((S, D), f32)],
        compiler_params=_cp(("arbitrary",)))(y, target)


def _adamw(w, g, m, v, *, name):
    R, C = w.shape
    tm = _tile(R, max(8, (1 << 19) // max(C, 1) // 8 * 8), 8)

    def body(w_ref, g_ref, m_ref, v_ref, d_ref, nm_ref, nv_ref):
        gv = g_ref[...]
        nm = ADAM_B1 * m_ref[...] + (1.0 - ADAM_B1) * gv
        nv = ADAM_B2 * v_ref[...] + (1.0 - ADAM_B2) * (gv * gv)
        m_hat = nm / (1.0 - ADAM_B1 ** ADAM_STEP)
        v_hat = nv / (1.0 - ADAM_B2 ** ADAM_STEP)
        d_ref[...] = -ADAM_LR * (m_hat / (jnp.sqrt(v_hat) + ADAM_EPS) + ADAM_WD * w_ref[...])
        nm_ref[...] = nm
        nv_ref[...] = nv

    rb = pl.BlockSpec((tm, C), lambda r: (r, 0))
    sh = jax.ShapeDtypeStruct((R, C), f32)
    return pl.pallas_call(body, name=name, grid=(R // tm,), in_specs=[rb] * 4, out_specs=[rb] * 3, out_shape=[sh] * 3,
                          compiler_params=_cp(("parallel",)))(w, g, m, v)


def _me():
    return lax.axis_index("x"), lax.axis_index("y"), lax.axis_index("c")


def _other_chips(mx, my):
    return [(1 - mx, my), (mx, 1 - my), (1 - mx, 1 - my)]


_ANY = pl.BlockSpec(memory_space=pl.ANY)


class _GatherChips:
    def __init__(self, xs):
        self.arrays = list(xs)
        n = len(xs)
        for x in xs:
            assert x.shape[0] % 2 == 0
        self.halves = [x.shape[0] // 2 for x in xs]
        self.out_shapes = [jax.ShapeDtypeStruct((4,) + x.shape, x.dtype) for x in xs]
        self.scratch = [pltpu.SemaphoreType.DMA((n, 6)), pltpu.SemaphoreType.DMA((n, 6))]

    def _sends(self, x_refs, o_refs, send, recv):
        mx, my, mc = _me()
        me = 2 * mx + my
        out = []
        for t, hf in enumerate(self.halves):
            mine = pl.ds(mc * hf, hf)
            for j, (cx, cy) in enumerate(_other_chips(mx, my)):
                out.append(pltpu.make_async_remote_copy(x_refs[t].at[mine], o_refs[t].at[me, mine], send.at[t, j], recv.at[t, j],
                                                        device_id=(cx, cy, mc), device_id_type=MESH))
        return out

    def start(self, x_refs, o_refs, scr):
        for cp in self._sends(x_refs, o_refs, *scr):
            cp.start()

    def finish(self, x_refs, o_refs, scr):
        send, recv = scr
        mx, my, mc = _me()
        chips = _other_chips(mx, my)
        fwd = []
        for t, hf in enumerate(self.halves):
            mine = pl.ds(mc * hf, hf)
            for j, (cx, cy) in enumerate(chips):
                k = 2 * cx + cy
                pltpu.make_async_remote_copy(x_refs[t].at[mine], o_refs[t].at[k, mine], send.at[t, j], recv.at[t, j],
                                             device_id=(cx, cy, mc), device_id_type=MESH).wait_recv()
                cp = pltpu.make_async_remote_copy(o_refs[t].at[k, mine], o_refs[t].at[k, mine], send.at[t, 3 + j], recv.at[t, 3 + j],
                                                  device_id=(mx, my, 1 - mc), device_id_type=MESH)
                cp.start()
                fwd.append(cp)
        for t, hf in enumerate(self.halves):
            theirs = pl.ds((1 - mc) * hf, hf)
            for j, (cx, cy) in enumerate(chips):
                k = 2 * cx + cy
                pltpu.make_async_remote_copy(o_refs[t].at[k, theirs], o_refs[t].at[k, theirs], send.at[t, 3 + j], recv.at[t, 3 + j],
                                             device_id=(mx, my, 1 - mc), device_id_type=MESH).wait_recv()
        for cp in self._sends(x_refs, o_refs, send, recv) + fwd:
            cp.wait_send()


class _ScatterChips:
    def __init__(self, ps):
        self.arrays = list(ps)
        n = len(ps)
        self.out_shapes = [jax.ShapeDtypeStruct((3,) + p.shape[1:], p.dtype) for p in ps]
        self.scratch = [pltpu.SemaphoreType.DMA((n, 3)), pltpu.SemaphoreType.DMA((n, 3))]

    def _copies(self, p_refs, o_refs, send, recv):
        mx, my, mc = _me()
        return [pltpu.make_async_remote_copy(p_refs[t].at[2 * cx + cy], o_refs[t].at[j], send.at[t, j], recv.at[t, j],
                                             device_id=(cx, cy, mc), device_id_type=MESH)
                for t in range(len(self.arrays)) for j, (cx, cy) in enumerate(_other_chips(mx, my))]

    def start(self, p_refs, o_refs, scr):
        for cp in self._copies(p_refs, o_refs, *scr):
            cp.start()

    def finish(self, p_refs, o_refs, scr):
        for cp in self._copies(p_refs, o_refs, *scr):
            cp.wait()


def _run_comm(comm, *, name):
    n = len(comm.arrays)

    def body(*refs):
        ins, outs, scr = refs[:n], refs[n:2 * n], refs[2 * n:]
        comm.start(ins, outs, scr)
        comm.finish(ins, outs, scr)

    return pl.pallas_call(body, name=name, in_specs=[_ANY] * n, out_specs=[_ANY] * n, out_shape=comm.out_shapes,
                          scratch_shapes=comm.scratch, compiler_params=pltpu.CompilerParams(has_side_effects=True))(*comm.arrays)


def _call(body, comm, *, name, grid, in_specs, out_specs, out_shape, scratch_shapes, sem, args):
    if comm is None:
        res = pl.pallas_call(body, name=name, grid=grid, in_specs=in_specs, out_specs=out_specs, out_shape=out_shape,
                             scratch_shapes=scratch_shapes, compiler_params=_cp(sem))(*args)
        return list(res), None
    n_in, n_out, n_scr, nc = len(in_specs), len(out_specs), len(scratch_shapes), len(comm.arrays)

    def wrapped(*refs):
        ins, cins = refs[:n_in], refs[n_in:n_in + nc]
        outs, couts = refs[n_in + nc:n_in + nc + n_out], refs[n_in + nc + n_out:n_in + 2 * nc + n_out]
        scr, cscr = refs[n_in + 2 * nc + n_out:n_in + 2 * nc + n_out + n_scr], refs[n_in + 2 * nc + n_out + n_scr:]
        ids = [pl.program_id(d) for d in range(len(grid))]
        first = functools.reduce(jnp.logical_and, [i == 0 for i in ids])
        last = functools.reduce(jnp.logical_and, [i == g - 1 for i, g in zip(ids, grid)])

        @pl.when(first)
        def _():
            comm.start(cins, couts, cscr)

        body(*ins, *outs, *scr)

        @pl.when(last)
        def _():
            comm.finish(cins, couts, cscr)

    res = pl.pallas_call(
        wrapped, name=name, grid=grid, in_specs=list(in_specs) + [_ANY] * nc, out_specs=list(out_specs) + [_ANY] * nc,
        out_shape=list(out_shape) + comm.out_shapes, scratch_shapes=list(scratch_shapes) + comm.scratch,
        compiler_params=_cp(("arbitrary",) * len(grid)))(*args, *comm.arrays)
    return list(res[:n_out]), list(res[n_out:])


class _PairSend:
    def __init__(self, gs):
        self.arrays = list(gs)
        n = len(gs)
        self.halves = [g.shape[1] // 2 for g in gs]
        self.out_shapes = [jax.ShapeDtypeStruct((4, g.shape[1] // 2, g.shape[2]), g.dtype) for g in gs]
        self.scratch = [pltpu.SemaphoreType.DMA((n, 4)), pltpu.SemaphoreType.DMA((n, 4))]

    def _copies(self, g_refs, o_refs, send, recv):
        mx, my, mc = _me()
        return [pltpu.make_async_remote_copy(g_refs[t].at[k, pl.ds((1 - mc) * hf, hf)], o_refs[t].at[k], send.at[t, k], recv.at[t, k],
                                             device_id=(mx, my, 1 - mc), device_id_type=MESH)
                for t, hf in enumerate(self.halves) for k in range(4)]

    def start(self, g_refs, o_refs, scr):
        for cp in self._copies(g_refs, o_refs, *scr):
            cp.start()

    def finish(self, g_refs, o_refs, scr):
        for cp in self._copies(g_refs, o_refs, *scr):
            cp.wait()


def _pair_exchange_halves(fs, *, name):
    n = len(fs)

    def body(*refs):
        f_refs, o_refs = refs[:n], refs[n:2 * n]
        send, recv = refs[2 * n:]
        mx, my, mc = _me()
        cps = []
        for t in range(n):
            hf = f_refs[t].shape[0]
            mine = pl.ds(mc * hf, hf)
            cp = pltpu.make_async_remote_copy(f_refs[t], o_refs[t].at[mine], send.at[t], recv.at[t],
                                              device_id=(mx, my, 1 - mc), device_id_type=MESH)
            cp.start()
            cps.append(cp)
        for t in range(n):
            hf = f_refs[t].shape[0]
            theirs = pl.ds((1 - mc) * hf, hf)
            cps[t].wait_send()
            pltpu.make_async_remote_copy(f_refs[t], o_refs[t].at[theirs], send.at[t], recv.at[t],
                                         device_id=(mx, my, 1 - mc), device_id_type=MESH).wait_recv()

    return pl.pallas_call(
        body, name=name, in_specs=[_ANY] * n, out_specs=[_ANY] * n,
        out_shape=[jax.ShapeDtypeStruct((2 * f.shape[0], f.shape[1]), f.dtype) for f in fs],
        scratch_shapes=[pltpu.SemaphoreType.DMA((n,)), pltpu.SemaphoreType.DMA((n,))],
        compiler_params=pltpu.CompilerParams(has_side_effects=True))(*fs)


def _allgather_all(x, *, name):
    def body(x_ref, o_ref, send, recv, lsem):
        mx, my, mc = _me()
        me = 4 * mx + 2 * my + mc
        local = pltpu.make_async_copy(x_ref, o_ref.at[me], lsem)
        local.start()
        cps = []
        for j in range(1, 8):
            px, py, pc = mx ^ (j >> 2), my ^ ((j >> 1) & 1), mc ^ (j & 1)
            cp = pltpu.make_async_remote_copy(x_ref, o_ref.at[me], send.at[j - 1], recv.at[j - 1],
                                              device_id=(px, py, pc), device_id_type=MESH)
            cp.start()
            cps.append(cp)
        for j in range(1, 8):
            px, py, pc = mx ^ (j >> 2), my ^ ((j >> 1) & 1), mc ^ (j & 1)
            pltpu.make_async_remote_copy(x_ref, o_ref.at[4 * px + 2 * py + pc], send.at[j - 1], recv.at[j - 1],
                                         device_id=(px, py, pc), device_id_type=MESH).wait_recv()
        for cp in cps:
            cp.wait_send()
        local.wait()

    return pl.pallas_call(
        body, name=name, in_specs=[_ANY], out_specs=_ANY, out_shape=jax.ShapeDtypeStruct((8,) + x.shape, x.dtype),
        scratch_shapes=[pltpu.SemaphoreType.DMA((7,)), pltpu.SemaphoreType.DMA((7,)), pltpu.SemaphoreType.DMA],
        compiler_params=pltpu.CompilerParams(has_side_effects=True))(x)


def _add_half(g4, recv, mc, *, name):
    _, R, C = g4.shape
    hf = R // 2
    tm = _tile(hf, max(16, (1 << 19) // C // 16 * 16), 16)
    nb = hf // tm

    def body(mc_ref, g_ref, r_ref, o_ref, ob_ref):
        s = g_ref[...] + r_ref[...]
        o_ref[...] = s
        ob_ref[...] = s.astype(COMM_DT)

    ospec = pl.BlockSpec((1, tm, C), lambda k, i, mc_ref: (k, i, 0))
    return pl.pallas_call(
        body, name=name,
        grid_spec=pltpu.PrefetchScalarGridSpec(
            num_scalar_prefetch=1, grid=(4, nb),
            in_specs=[pl.BlockSpec((1, tm, C), lambda k, i, mc_ref: (k, mc_ref[0] * nb + i, 0)),
                      pl.BlockSpec((1, tm, C), lambda k, i, mc_ref: (k, i, 0))],
            out_specs=[ospec, ospec]),
        out_shape=[jax.ShapeDtypeStruct((4, hf, C), f32), jax.ShapeDtypeStruct((4, hf, C), COMM_DT)],
        compiler_params=_cp(("parallel", "parallel")))(mc, g4, recv)


def _sum_chips(p4, recv3, me, *, name):
    _, Rh, C = p4.shape
    tm = _tile(Rh, max(16, (1 << 19) // C // 16 * 16), 16)

    def body(me_ref, p_ref, r_ref, o_ref):
        o_ref[...] = ((p_ref[0] + r_ref[0].astype(f32)) + r_ref[1].astype(f32)) + r_ref[2].astype(f32)

    return pl.pallas_call(
        body, name=name,
        grid_spec=pltpu.PrefetchScalarGridSpec(
            num_scalar_prefetch=1, grid=(Rh // tm,),
            in_specs=[pl.BlockSpec((1, tm, C), lambda i, me_ref: (me_ref[0], i, 0)),
                      pl.BlockSpec((3, tm, C), lambda i, me_ref: (0, i, 0))],
            out_specs=pl.BlockSpec((tm, C), lambda i, me_ref: (i, 0))),
        out_shape=jax.ShapeDtypeStruct((Rh, C), f32),
        compiler_params=_cp(("parallel",)))(me, p4, recv3)


def _sum8(x8, *, name):
    _, R, C = x8.shape
    tm = _tile(R, 64, 8)

    def body(x_ref, o_ref):
        acc = x_ref[0]
        for k in range(1, 8):
            acc = acc + x_ref[k]
        o_ref[...] = acc

    return pl.pallas_call(body, name=name, grid=(R // tm,), in_specs=[pl.BlockSpec((8, tm, C), lambda i: (0, i, 0))],
                          out_specs=pl.BlockSpec((tm, C), lambda i: (i, 0)), out_shape=jax.ShapeDtypeStruct((R, C), f32),
                          compiler_params=_cp(("parallel",)))(x8)


def _ssd_layer_fwd(h, W, tag, plan, i):
    z = _mm(h, W["wz"], name=tag + "_z")
    early = plan.fwd_early_comm(i)
    if early is None:
        xp = _mm(h, W["wxbc"], name=tag + "_xbc")
    else:
        xp, eres = _mm(h, W["wxbc"], name=tag + "_xbc", comm=early)
        plan.fwd_early_done(i, eres)
    dtr = _mm(h, W["wdt"], name=tag + "_dt")
    xbc = _conv_fwd(xp, W["conv_w"], W["conv_b"], name=tag + "_conv")
    (y, states), cres = _ssd_scan_fwd(xbc, dtr, W["dt_bias"], W["a_log"], W["d"], name=tag + "_scan", comm=plan.fwd_comm(i))
    DI = y.shape[1]
    G = SSD_N_GROUPS
    gs = DI // G
    (yn,) = _rowwise(_ssd_gate_fn, [(W["norm_w"], "c", gs)], [(y, "c", gs), (z, "c", gs)], [(DI, MXU_DT, "c", gs)],
                     name=tag + "_gate", ncol=G, tm=512)
    out = _mm(yn, W["wout"], name=tag + "_out")
    return out, dict(h=h, z=z, xp=xp, dtr=dtr, xbc=xbc, states=states, y=y, yn=yn), cres


def _carried_rowwise_bwd(plan, i, *a, **kw):
    early = plan.bwd_early_comm(i)
    if early is None:
        return _rowwise_bwd(*a, **kw)
    dp, di, cres = _rowwise_bwd(*a, comm=early, **kw)
    plan.bwd_early_done(i, cres)
    return dp, di


def _carried_mm(comm, done, i, *a, **kw):
    if comm is None:
        return _mm(*a, **kw)
    out, cres = _mm(*a, comm=comm, **kw)
    done(i, cres)
    return out


def _ssd_layer_bwd(sv, W, dr, drb, tag, plan, i):
    h = sv["h"]
    DI = sv["y"].shape[1]
    G = SSD_N_GROUPS
    gs = DI // G
    gr = {}
    dyn = _mm(drb, W["wout"], tb=True, name=tag + "_dyn")
    gr["wout"] = _mm(sv["yn"], drb, ta=True, name=tag + "_dwout")
    plan.early_grad(i, "ssd_out_w", gr["wout"])
    (dnw,), (dy, dz) = _carried_rowwise_bwd(plan, i, _ssd_gate_fn, [(W["norm_w"], "c", gs)], [(sv["y"], "c", gs), (sv["z"], "c", gs)],
                                             [(dyn, "c", gs)], name=tag + "_dgate", ncol=G, tm=512, din_dtypes=[(f32,), (MXU_DT,)])
    gr["norm_w"] = dnw
    (dxbc, ddtr, gr["dt_bias"], gr["a_log"], gr["d"]), cres = _ssd_scan_bwd(
        sv["xbc"], sv["dtr"], W["dt_bias"], W["a_log"], W["d"], sv["states"], dy, name=tag + "_dscan", comm=plan.bwd_comm(i))
    plan.bwd_done(i, cres)
    dxp, gr["conv_w"], gr["conv_b"] = _conv_bwd(
        sv["xp"], W["conv_w"], W["conv_b"], [dxbc], name=tag + "_dconv", dx_dtype=MXU_DT,
        dy_block=lambda c, cw: _ssd_grouped_block(c, G, gs, SSD_D_STATE, cw))
    gr["wz"] = _mm(h, dz, ta=True, name=tag + "_dwz")
    gr["wxbc"] = _mm(h, dxp, ta=True, name=tag + "_dwxbc")
    gr["wdt"] = _mm(h, ddtr, ta=True, name=tag + "_dwdt")
    dh = _carried_mm(plan.tail_early_comm(i, gr), plan.tail_early_done, i,
                     dz, W["wz"], tb=True, add=dr, add_scale=_alpha(), name=tag + "_dh1")
    dh = _carried_mm(plan.tail_comm(i), plan.tail_done, i, dxp, W["wxbc"], tb=True, add=dh, name=tag + "_dh2")
    dh = _mm(ddtr, W["wdt"], tb=True, add=dh, name=tag + "_dh3")
    return dh, gr


def _mla_layer_fwd(h, W, cos, sin, tag, comm=None):
    QR, KR = W["wqc"].shape[1], W["wkvc"].shape[1]
    HW = W["wqn"].shape[1]
    H = HW // LANES
    qc = _mm(h, W["wqc"], name=tag + "_qc")
    kvc = _mm(h, W["wkvc"], name=tag + "_kvc")
    krp = _mm(h, W["wkr"], name=tag + "_krp")
    z = _mm(h, W["wz"], name=tag + "_z")
    (qcn,) = _rowwise(_rms_fn, [(W["q_norm"], "a", QR)], [(qc, "a", QR)], [(QR, MXU_DT, "a", QR)], name=tag + "_qnorm")
    (kvn,) = _rowwise(_rms_fn, [(W["kv_norm"], "a", KR)], [(kvc, "a", KR)], [(KR, MXU_DT, "a", KR)], name=tag + "_kvnorm")
    qn = _mm(qcn, W["wqn"], name=tag + "_qn", out_dtype=MXU_DT)
    qrp = _mm(qcn, W["wqr"], name=tag + "_qrp")
    kn = _mm(kvn, W["wkn"], name=tag + "_kn", out_dtype=MXU_DT)
    v = _mm(kvn, W["wv"], name=tag + "_v", out_dtype=MXU_DT)
    (qr,) = _rowwise(_rope_fn, [], [(cos, "a", LANES), (sin, "a", LANES), (qrp, "c", LANES)], [(HW, MXU_DT, "c", LANES)],
                     name=tag + "_qrope", ncol=H, tm=1024)
    (kr,) = _rowwise(_rope_fn, [], [(cos, "a", LANES), (sin, "a", LANES), (krp, "a", LANES)], [(LANES, MXU_DT, "a", LANES)],
                     name=tag + "_krope")
    (o, lse), cres = _attn_fwd(qn, qr, kn, kr, v, name=tag + "_attn", comm=comm)
    (og,) = _rowwise(_mul_silu_fn, [], [(o, "a", HW), (z, "a", HW)], [(HW, MXU_DT, "a", HW)], name=tag + "_ogate")
    out = _mm(og, W["wout"], name=tag + "_out")
    return out, dict(h=h, qc=qc, kvc=kvc, z=z, qcn=qcn, kvn=kvn, qn=qn, qr=qr, kn=kn, kr=kr, v=v, o=o, lse=lse, og=og), cres


def _mla_layer_bwd(sv, W, cos, sin, dr, drb, tag, plan, i):
    h = sv["h"]
    QR, KR = W["wqc"].shape[1], W["wkvc"].shape[1]
    HW = W["wqn"].shape[1]
    H = HW // LANES
    gr = {}
    dog = _mm(drb, W["wout"], tb=True, name=tag + "_dog")
    gr["wout"] = _mm(sv["og"], drb, ta=True, name=tag + "_dwout")
    _, (do, dz) = _rowwise_bwd(_mul_silu_fn, [], [(sv["o"], "a", HW), (sv["z"], "a", HW)], [(dog, "a", HW)], name=tag + "_dogate",
                               din_dtypes=[(f32,), (MXU_DT,)])
    att = (sv["qn"], sv["qr"], sv["kn"], sv["kr"], sv["v"], sv["o"], sv["lse"], do)
    (dqn, dqr), cres = _attn_bwd_dq(*att, name=tag + "_dq", comm=plan.bwd_early_comm(i))
    plan.bwd_early_done(i, cres)
    (dkn, dkr, dv), cres = _attn_bwd_dkv(*att, name=tag + "_dkv", comm=plan.bwd_comm(i))
    plan.bwd_done(i, cres)
    _, (dqrp,) = _rowwise_bwd(_rope_fn, [], [(cos, "a", LANES), (sin, "a", LANES), (dqr, "c", LANES)], [(dqr, "c", LANES)],
                              name=tag + "_dqrope", ncol=H, tm=1024, diff_i=[2], din_dtypes=[(MXU_DT,)])
    _, (dkrp,) = _rowwise_bwd(_rope_fn, [], [(cos, "a", LANES), (sin, "a", LANES), (dkr, "a", LANES)], [(dkr, "a", LANES)],
                              name=tag + "_dkrope", diff_i=[2], din_dtypes=[(MXU_DT,)])
    dqcn = _mm(dqn, W["wqn"], tb=True, name=tag + "_dqcn1")
    dqcn = _mm(dqrp, W["wqr"], tb=True, add=dqcn, name=tag + "_dqcn2")
    dkvn = _mm(dkn, W["wkn"], tb=True, name=tag + "_dkvn1")
    dkvn = _mm(dv, W["wv"], tb=True, add=dkvn, name=tag + "_dkvn2")
    gr["wqn"] = _mm(sv["qcn"], dqn, ta=True, name=tag + "_dwqn")
    gr["wqr"] = _mm(sv["qcn"], dqrp, ta=True, name=tag + "_dwqr")
    gr["wkn"] = _mm(sv["kvn"], dkn, ta=True, name=tag + "_dwkn")
    gr["wv"] = _mm(sv["kvn"], dv, ta=True, name=tag + "_dwv")
    (gr["q_norm"],), (dqc,) = _rowwise_bwd(_rms_fn, [(W["q_norm"], "a", QR)], [(sv["qc"], "a", QR)], [(dqcn, "a", QR)], name=tag + "_dqnorm",
                                           din_dtypes=[(MXU_DT,)])
    (gr["kv_norm"],), (dkvc,) = _rowwise_bwd(_rms_fn, [(W["kv_norm"], "a", KR)], [(sv["kvc"], "a", KR)], [(dkvn, "a", KR)], name=tag + "_dkvnorm",
                                             din_dtypes=[(MXU_DT,)])
    dh = _mm(dz, W["wz"], tb=True, add=dr, add_scale=_alpha(), name=tag + "_dh1")
    dh = _mm(dqc, W["wqc"], tb=True, add=dh, name=tag + "_dh2")
    dh = _mm(dkvc, W["wkvc"], tb=True, add=dh, name=tag + "_dh3")
    dh = _mm(dkrp, W["wkr"], tb=True, add=dh, name=tag + "_dh4")
    gr["wz"] = _mm(h, dz, ta=True, name=tag + "_dwz")
    gr["wqc"] = _mm(h, dqc, ta=True, name=tag + "_dwqc")
    gr["wkvc"] = _mm(h, dkvc, ta=True, name=tag + "_dwkvc")
    gr["wkr"] = _mm(h, dkrp, ta=True, name=tag + "_dwkr")
    return dh, gr


def _gdn_layer_fwd(h, W, tag, comm=None):
    HK, HV, DK, DV = GDN_N_QK_HEADS, GDN_N_V_HEADS, GDN_DK, GDN_DV
    KD, VD = HK * DK, HV * DV
    qkvp = _mm(h, W["wqkv"], name=tag + "_qkv")
    z = _mm(h, W["wz"], name=tag + "_z")
    br = _mm(h, W["wb"], name=tag + "_b")
    ar = _mm(h, W["wa"], name=tag + "_a")
    qkv = _conv_fwd(qkvp, W["conv_w"], jnp.zeros((1, qkvp.shape[1]), f32), name=tag + "_conv")
    scale = jnp.concatenate([jnp.full((1, KD), DK ** -0.5, f32), jnp.ones((1, KD), f32)], axis=1)
    (qkn,) = _rowwise(_l2_fn, [(scale, "c", DK)], [(qkv, "c", DK)], [(2 * KD, f32, "c", DK)], name=tag + "_l2", ncol=2 * HK, tm=2048)
    (o, states, inverses), cres = _gdn_scan_fwd(qkn, qkv, br, ar, W["a_log"], W["dt_bias"], name=tag + "_scan", comm=comm)
    (on,) = _rowwise(_gdn_gate_fn, [(W["norm_w"], "a", DV)], [(o, "c", DV), (z, "c", DV)], [(VD, MXU_DT, "c", DV)],
                     name=tag + "_gate", ncol=HV, tm=1024)
    out = _mm(on, W["wout"], name=tag + "_out")
    return out, dict(h=h, qkvp=qkvp, z=z, br=br, ar=ar, qkv=qkv, qkn=qkn, o=o, states=states, inverses=inverses, on=on, scale=scale), cres


def _gdn_layer_bwd(sv, W, dr, drb, tag, plan, i):
    h = sv["h"]
    HK, HV, DK, DV = GDN_N_QK_HEADS, GDN_N_V_HEADS, GDN_DK, GDN_DV
    KD, VD = HK * DK, HV * DV
    gr = {}
    don = _mm(drb, W["wout"], tb=True, name=tag + "_don")
    gr["wout"] = _mm(sv["on"], drb, ta=True, name=tag + "_dwout")
    (gr["norm_w"],), (do, dz) = _carried_rowwise_bwd(plan, i, _gdn_gate_fn, [(W["norm_w"], "a", DV)], [(sv["o"], "c", DV), (sv["z"], "c", DV)],
                                                      [(don, "c", DV)], name=tag + "_dgate", ncol=HV, tm=1024, din_dtypes=[(f32,), (MXU_DT,)])
    (dq, dk, dv, dbr, dar, gr["a_log"], gr["dt_bias"]), cres = _gdn_scan_bwd(
        sv["qkn"], sv["qkv"], sv["br"], sv["ar"], W["a_log"], W["dt_bias"], sv["states"], sv["inverses"], do, name=tag + "_dscan",
        comm=plan.bwd_comm(i))
    plan.bwd_done(i, cres)
    _, (dqq,) = _rowwise_bwd(_l2_fn, [(sv["scale"], "c", DK)], [(sv["qkv"], "c", DK)], [(dq, "c", DK)],
                             name=tag + "_dl2q", ncol=HK, tm=2048, diff_p=[])
    _, (dqk,) = _rowwise_bwd(_l2_fn, [(sv["scale"], "c", DK, HK)], [(sv["qkv"], "c", DK, HK)], [(dk, "c", DK)],
                             name=tag + "_dl2k", ncol=HK, tm=2048, diff_p=[])
    dqkvp, gr["conv_w"], _ = _conv_bwd(sv["qkvp"], W["conv_w"], jnp.zeros((1, sv["qkvp"].shape[1]), f32), [dqq, dqk, dv],
                                       name=tag + "_dconv", dx_dtype=MXU_DT)
    dh = _mm(dz, W["wz"], tb=True, add=dr, add_scale=_alpha(), name=tag + "_dh1")
    dh = _mm(dqkvp, W["wqkv"], tb=True, add=dh, name=tag + "_dh2")
    dh = _mm(dbr, W["wb"], tb=True, add=dh, name=tag + "_dh3")
    dh = _mm(dar, W["wa"], tb=True, add=dh, name=tag + "_dh4")
    gr["wz"] = _mm(h, dz, ta=True, name=tag + "_dwz")
    gr["wqkv"] = _mm(h, dqkvp, ta=True, name=tag + "_dwqkv")
    gr["wb"] = _mm(h, dbr, ta=True, name=tag + "_dwb")
    gr["wa"] = _mm(h, dar, ta=True, name=tag + "_dwa")
    return dh, gr


def _rope_tables(positions):
    half = MLA_ROPE // 2
    inv_freq = ROPE_THETA ** (-jnp.arange(0, MLA_ROPE, 2, dtype=f32) / MLA_ROPE)
    ang = positions.astype(f32)[:, None] * inv_freq
    cos, sin = jnp.cos(ang), jnp.sin(ang)
    S = positions.shape[0]
    pad = jnp.zeros((S, LANES - 2 * half), f32)
    return jnp.concatenate([cos, cos, pad + 1.0], axis=1), jnp.concatenate([sin, sin, pad], axis=1)


class _LocalPlan:
    def __init__(self, LW):
        self.LW, self.grads = LW, [None] * DEPTH

    def weights(self, i):
        return self.LW[i]

    def fwd_early_comm(self, i):
        return None

    def fwd_early_done(self, i, res):
        pass

    def fwd_comm(self, i):
        return None

    def fwd_done(self, i, res):
        pass

    def early_grad(self, i, name, g):
        pass

    def tail_early_comm(self, i, gr):
        return None

    def tail_early_done(self, i, res):
        pass

    def tail_comm(self, i):
        return None

    def tail_done(self, i, res):
        pass

    def bwd_early_comm(self, i):
        return None

    def bwd_early_done(self, i, res):
        pass

    def bwd_comm(self, i):
        return None

    def bwd_done(self, i, res):
        pass

    def layer_grads(self, i, gr):
        self.grads[i] = gr


def _local_step(x, positions, target, ln_g, ln_b, plan):
    cos, sin = _rope_tables(positions)
    h, hb = x, x.astype(MXU_DT)
    saved, LW = [], []
    for i in range(DEPTH):
        kind, tag = i % 3, "l%d" % i
        LW.append(plan.weights(i))
        if kind == 0:
            y, sv, cres = _ssd_layer_fwd(hb, LW[i], tag, plan, i)
        elif kind == 1:
            y, sv, cres = _mla_layer_fwd(hb, LW[i], cos, sin, tag, plan.fwd_comm(i))
        else:
            y, sv, cres = _gdn_layer_fwd(hb, LW[i], tag, plan.fwd_comm(i))
        plan.fwd_done(i, cres)
        D = h.shape[1]
        r, h, hb = _rowwise(_res_ln_fn, [(ln_g[i], "a", D), (ln_b[i], "a", D)], [(h, "a", D), (y, "a", D)],
                            [(D, f32, "a", D), (D, f32, "a", D), (D, MXU_DT, "a", D)], name=tag + "_ln")
        sv["r"] = r
        saved.append(sv)
    loss, dh = _loss_head(h, target, name="loss_head")
    dg, db = [None] * DEPTH, [None] * DEPTH
    for i in reversed(range(DEPTH)):
        kind, tag = i % 3, "l%d" % i
        sv = saved[i]
        D = dh.shape[1]
        (dg[i], db[i]), (dr, drb) = _rowwise_bwd(_ln_fn, [(ln_g[i], "a", D), (ln_b[i], "a", D)], [(sv["r"], "a", D)], [(dh, "a", D)],
                                                 name=tag + "_dln", din_dtypes=[(f32, MXU_DT)])
        if kind == 0:
            dh, gr = _ssd_layer_bwd(sv, LW[i], dr, drb, tag, plan, i)
        elif kind == 1:
            dh, gr = _mla_layer_bwd(sv, LW[i], cos, sin, dr, drb, tag, plan, i)
        else:
            dh, gr = _gdn_layer_bwd(sv, LW[i], dr, drb, tag, plan, i)
        plan.layer_grads(i, gr)
    return loss, dh, dg, db


_WEIGHTS = ["ssd_in_w", "ssd_conv_w", "ssd_conv_b", "ssd_dt_bias", "ssd_a_log", "ssd_d", "ssd_norm_w", "ssd_out_w",
            "mla_in_w", "mla_q_norm_w", "mla_q_up_w", "mla_kv_norm_w", "mla_kv_up_w", "mla_out_w",
            "gdn_in_w", "gdn_conv_w", "gdn_a_log", "gdn_dt_bias", "gdn_norm_w", "gdn_out_w", "ln_g", "ln_b"]
_BIG = {"ssd_in_w": "col", "ssd_out_w": "row", "mla_in_w": "col", "mla_q_up_w": "col", "mla_kv_up_w": "col",
        "mla_out_w": "row", "gdn_in_w": "col", "gdn_out_w": "row"}
_SMALL_SHARDED = ["ssd_conv_w", "ssd_conv_b", "ssd_norm_w", "gdn_conv_w"]
_PACK_ROWS = 16


def _gathered_to_full(g, kind, nl):
    if kind == "col":
        _, RK, Ns = g.shape
        return g.reshape(4, nl, RK // nl, Ns).transpose(1, 2, 0, 3).reshape(nl, RK // nl, 4 * Ns)
    _, RK, N = g.shape
    return g.reshape(4, nl, RK // nl, N).transpose(1, 0, 2, 3).reshape(nl, 4 * (RK // nl), N)


def _full_to_slots(f, kind):
    nl, K, N = f.shape
    if kind == "col":
        return f.reshape(nl, K, 4, N // 4).transpose(2, 0, 1, 3).reshape(4, nl * K, N // 4)
    return f.reshape(nl, 4, K // 4, N).transpose(1, 0, 2, 3).reshape(4, nl * (K // 4), N)


def _pack(arrs):
    flat = jnp.concatenate([a.reshape(-1).astype(f32) for a in arrs])
    unit = _PACK_ROWS * LANES
    n = -(-flat.shape[0] // unit) * unit
    return jnp.pad(flat, (0, n - flat.shape[0])).reshape(_PACK_ROWS, n // _PACK_ROWS)


def _unpack(packed, shapes):
    flat = packed.reshape(-1)
    out, off = [], 0
    for sh in shapes:
        n = math.prod(sh)
        out.append(flat[off:off + n].reshape(sh))
        off += n
    return out


def _pad_lanes(a):
    return jnp.pad(a, [(0, 0)] * (a.ndim - 1) + [(0, LANES - a.shape[-1])])


_IN_PROJ = ("ssd_in_w", "mla_in_w", "gdn_in_w")


class _ColSlots:
    def __init__(self, slots):
        self.slots = slots
        self.shape = (slots.shape[1], 4 * slots.shape[2])

    def __getitem__(self, idx):
        _, cols = idx
        ns = self.slots.shape[2]
        a = cols.start or 0
        b = self.shape[1] if cols.stop is None else cols.stop
        parts = [self.slots[k][:, max(a, k * ns) - k * ns:min(b, (k + 1) * ns) - k * ns]
                 for k in range(4) if max(a, k * ns) < min(b, (k + 1) * ns)]
        return parts[0] if len(parts) == 1 else jnp.concatenate(parts, axis=1)


def _col_slots(pieces):
    widths = [p.shape[1] for p in pieces]
    ns = sum(widths) // 4
    slots = []
    for k in range(4):
        lo, hi, off, parts = k * ns, (k + 1) * ns, 0, []
        for p, wd in zip(pieces, widths):
            if max(lo, off) < min(hi, off + wd):
                parts.append(p[:, max(lo, off) - off:min(hi, off + wd) - off])
            off += wd
        slots.append(parts[0] if len(parts) == 1 else jnp.concatenate(parts, axis=1))
    return jnp.stack(slots)


def _layer_dict(i, full):
    G, N, P = SSD_N_GROUPS, SSD_D_STATE, SSD_HEAD_DIM
    kind, j = i % 3, i // 3
    if kind == 0:
        H = full["ssd_dt_bias"][j].shape[0]
        DI = H * P
        CD = DI + 2 * G * N
        win = full["ssd_in_w"][j]
        return dict(wz=win[:, :DI], wxbc=win[:, DI:DI + CD], wdt=win[:, DI + CD:], conv_w=full["ssd_conv_w"][j],
                    conv_b=full["ssd_conv_b"][j][None], dt_bias=full["ssd_dt_bias"][j][None], a_log=full["ssd_a_log"][j][None],
                    d=full["ssd_d"][j][None], norm_w=full["ssd_norm_w"][j][None], wout=full["ssd_out_w"][j])
    if kind == 1:
        QR, KR = MLA_Q_RANK, MLA_KV_RANK
        win = full["mla_in_w"][j]
        Hh = full["mla_q_up_w"][j].shape[1] // (MLA_NOPE + MLA_ROPE)
        qup = full["mla_q_up_w"][j].reshape(QR, Hh, MLA_NOPE + MLA_ROPE)
        kvup = full["mla_kv_up_w"][j].reshape(KR, Hh, MLA_NOPE + MLA_V)
        return dict(wqc=win[:, :QR], wkvc=win[:, QR:QR + KR], wkr=_pad_lanes(win[:, QR + KR:QR + KR + MLA_ROPE]),
                    wz=win[:, QR + KR + MLA_ROPE:], q_norm=full["mla_q_norm_w"][j][None], kv_norm=full["mla_kv_norm_w"][j][None],
                    wqn=qup[:, :, :MLA_NOPE].reshape(QR, Hh * MLA_NOPE), wqr=_pad_lanes(qup[:, :, MLA_NOPE:]).reshape(QR, Hh * LANES),
                    wkn=kvup[:, :, :MLA_NOPE].reshape(KR, Hh * MLA_NOPE), wv=kvup[:, :, MLA_NOPE:].reshape(KR, Hh * MLA_V),
                    wout=full["mla_out_w"][j])
    KD, VD, HV = GDN_N_QK_HEADS * GDN_DK, GDN_N_V_HEADS * GDN_DV, GDN_N_V_HEADS
    win = full["gdn_in_w"][j]
    c0, c1 = 2 * KD + VD, 2 * KD + 2 * VD
    return dict(wqkv=win[:, :c0], wz=win[:, c0:c1], wb=win[:, c1:c1 + HV], wa=win[:, c1 + HV:], conv_w=full["gdn_conv_w"][j],
                a_log=full["gdn_a_log"][j][None], dt_bias=full["gdn_dt_bias"][j][None], norm_w=full["gdn_norm_w"][j][None],
                wout=full["gdn_out_w"][j])


def _layer_weights(full, D):
    return [_layer_dict(i, full) for i in range(DEPTH)]


def _layer_full_grads(i, g, slots=False):
    kind = i % 3
    join = _col_slots if slots else (lambda pieces: jnp.concatenate(pieces, axis=1))
    if kind == 0:
        out = {"ssd_in_w": join([g["wz"], g["wxbc"], g["wdt"]]), "ssd_conv_w": g["conv_w"], "ssd_out_w": g["wout"]}
        for n in ("conv_b", "dt_bias", "a_log", "d", "norm_w"):
            out["ssd_" + n] = g[n][0]
        return out
    if kind == 1:
        QR, KR = g["wqn"].shape[0], g["wkn"].shape[0]
        Hh = g["wqn"].shape[1] // MLA_NOPE
        return {"mla_in_w": join([g["wqc"], g["wkvc"], g["wkr"][:, :MLA_ROPE], g["wz"]]),
                "mla_q_up_w": jnp.concatenate([g["wqn"].reshape(QR, Hh, MLA_NOPE), g["wqr"].reshape(QR, Hh, LANES)[:, :, :MLA_ROPE]],
                                              axis=2).reshape(QR, -1),
                "mla_kv_up_w": jnp.concatenate([g["wkn"].reshape(KR, Hh, MLA_NOPE), g["wv"].reshape(KR, Hh, MLA_V)], axis=2).reshape(KR, -1),
                "mla_q_norm_w": g["q_norm"][0], "mla_kv_norm_w": g["kv_norm"][0], "mla_out_w": g["wout"]}
    out = {"gdn_in_w": join([g["wqkv"], g["wz"], g["wb"], g["wa"]]), "gdn_conv_w": g["conv_w"], "gdn_out_w": g["wout"]}
    for n in ("a_log", "dt_bias", "norm_w"):
        out["gdn_" + n] = g[n][0]
    return out


def _full_grads(grads, dg, db):
    per = {n: [] for n in _WEIGHTS}
    for i in range(DEPTH):
        for n, a in _layer_full_grads(i, grads[i]).items():
            per[n].append(a)
        per["ln_g"].append(dg[i][0])
        per["ln_b"].append(db[i][0])
    return {n: jnp.stack(v) for n, v in per.items()}


class _DistPlan:
    def __init__(self, w, chip, core):
        self.w, self.chip = w, chip
        self.chip_arr = jnp.reshape(chip, (1,)).astype(jnp.int32)
        self.core_arr = jnp.reshape(core, (1,)).astype(jnp.int32)
        self.full = {n: {} for n in _BIG}
        self.gkeys, self.g4, self.p4, self.fin, self.small_grads = {}, {}, {}, {}, [None] * DEPTH
        keys = [("ssd_in_w", 0)]
        shards = self._shards(keys)
        got = _run_comm(_GatherChips(shards + [_pack([w[n] for n in _SMALL_SHARDED])]), name="gather_l0")
        self._fill(keys, shards, got[:1])
        small = lax.dynamic_update_slice(got[1], _pack([w[n] for n in _SMALL_SHARDED])[None], (chip, 0, 0))
        parts = [_unpack(small[k], [w[n].shape for n in _SMALL_SHARDED]) for k in range(4)]
        for t, n in enumerate(_SMALL_SHARDED):
            self.full[n] = jnp.concatenate([parts[k][t] for k in range(4)], axis=-1)
        for n in _WEIGHTS:
            if n not in self.full:
                self.full[n] = w[n]

    @staticmethod
    def keys(i):
        names = [["ssd_in_w", "ssd_out_w"], ["mla_in_w", "mla_q_up_w", "mla_kv_up_w", "mla_out_w"], ["gdn_in_w", "gdn_out_w"]][i % 3]
        return [(n, i // 3) for n in names]

    def _shards(self, keys):
        return [self.w[n][j].astype(MXU_DT) for n, j in keys]

    def _fill(self, keys, shards, got):
        for (n, j), s, g in zip(keys, shards, got):
            g = lax.dynamic_update_slice(g, s[None], (self.chip, 0, 0))
            self.full[n][j] = _ColSlots(g) if n in _IN_PROJ else _gathered_to_full(g, _BIG[n], 1)[0]

    def weights(self, i):
        if i == 0:
            self.full["ssd_out_w"][0] = None
        self._w = _layer_dict(i, self.full)
        return self._w

    def _start_gather(self, keys):
        self._pending = (keys, self._shards(keys))
        return _GatherChips(self._pending[1])

    def _end_gather(self, res):
        self._fill(self._pending[0], self._pending[1], res)

    def fwd_early_comm(self, i):
        return self._start_gather([("ssd_out_w", 0)]) if i == 0 else None

    def fwd_early_done(self, i, res):
        self._end_gather(res)
        self._w["wout"] = self.full["ssd_out_w"][0]

    def fwd_comm(self, i):
        return self._start_gather(self.keys(i + 1)) if i + 1 < DEPTH else None

    def fwd_done(self, i, res):
        if res is not None:
            self._end_gather(res)

    def _slots(self, n, g):
        return g if n in _IN_PROJ else _full_to_slots(g[None], _BIG[n])

    def early_grad(self, i, name, g):
        if i == 0:
            self.gkeys[1].append((name, 0))
            self.g4[1].append(self._slots(name, g))

    def _make_group(self, i, gr):
        fg = _layer_full_grads(i, gr, slots=True)
        self.small_grads[i] = {n: a for n, a in fg.items() if n not in _BIG}
        self.gkeys[i] = [k for k in self.keys(i) if not (i == 0 and k[0] == "ssd_out_w")]
        self.g4[i] = [self._slots(n, fg[n]) for n, _ in self.gkeys[i]]

    def layer_grads(self, i, gr):
        if i > 0:
            self._make_group(i, gr)

    def tail_early_comm(self, i, gr):
        if i > 0:
            return None
        self._make_group(0, gr)
        return _PairSend(self.g4[0])

    def tail_early_done(self, i, res):
        self._pair_add(0, res)

    def tail_comm(self, i):
        return self._scatter(0) if i == 0 else None

    def tail_done(self, i, res):
        self._sum(0, res)

    def _pair_add(self, g, r1):
        self.p4[g] = [_add_half(a, b, self.core_arr, name="grad_pair_add_g%d_%s%d" % (g, n, j))
                      for a, b, (n, j) in zip(self.g4[g], r1, self.gkeys[g])]

    def bwd_early_comm(self, i):
        return _PairSend(self.g4[i + 1]) if i + 1 < DEPTH else None

    def bwd_early_done(self, i, res):
        if res is not None:
            self._pair_add(i + 1, res)

    def _scatter(self, g):
        return _ScatterChips([p[1] for p in self.p4[g]])

    def _sum(self, g, r2):
        self.fin[g] = [_sum_chips(p[0], b, self.chip_arr, name="grad_chip_sum_g%d_%s%d" % (g, n, j))
                       for p, b, (n, j) in zip(self.p4[g], r2, self.gkeys[g])]

    def bwd_comm(self, i):
        return self._scatter(i + 1) if i + 1 < DEPTH else None

    def bwd_done(self, i, res):
        if res is not None:
            self._sum(i + 1, res)

    def grad_shards(self, core):
        order = [(g, t) for g in range(DEPTH) for t in range(len(self.gkeys[g]))]
        fins = [self.fin[g][t] for g, t in order]
        got = _pair_exchange_halves(fins, name="grad_pair_share")
        got = [lax.dynamic_update_slice(a, f, (core * f.shape[0], 0)) for a, f in zip(got, fins)]
        per = {n: {} for n in _BIG}
        for (g, t), a in zip(order, got):
            n, j = self.gkeys[g][t]
            per[n][j] = a
        return {n: (v[0] if len(v) == 1 else jnp.concatenate([v[j] for j in sorted(v)], axis=0)) for n, v in per.items()}


def kernel(x, positions, ssd_in_w, ssd_conv_w, ssd_conv_b, ssd_dt_bias, ssd_a_log, ssd_d, ssd_norm_w, ssd_out_w, mla_in_w, mla_q_norm_w, mla_q_up_w, mla_kv_norm_w, mla_kv_up_w, mla_out_w, gdn_in_w, gdn_conv_w, gdn_a_log, gdn_dt_bias, gdn_norm_w, gdn_out_w, ln_g, ln_b, loss_target, m_ssd_in_w, m_ssd_conv_w, m_ssd_conv_b, m_ssd_dt_bias, m_ssd_a_log, m_ssd_d, m_ssd_norm_w, m_ssd_out_w, m_mla_in_w, m_mla_q_norm_w, m_mla_q_up_w, m_mla_kv_norm_w, m_mla_kv_up_w, m_mla_out_w, m_gdn_in_w, m_gdn_conv_w, m_gdn_a_log, m_gdn_dt_bias, m_gdn_norm_w, m_gdn_out_w, m_ln_g, m_ln_b, v_ssd_in_w, v_ssd_conv_w, v_ssd_conv_b, v_ssd_dt_bias, v_ssd_a_log, v_ssd_d, v_ssd_norm_w, v_ssd_out_w, v_mla_in_w, v_mla_q_norm_w, v_mla_q_up_w, v_mla_kv_norm_w, v_mla_kv_up_w, v_mla_out_w, v_gdn_in_w, v_gdn_conv_w, v_gdn_a_log, v_gdn_dt_bias, v_gdn_norm_w, v_gdn_out_w, v_ln_g, v_ln_b):
    args = dict(locals())
    w = {n: args[n] for n in _WEIGHTS}
    mom = {n: args["m_" + n] for n in _WEIGHTS}
    vel = {n: args["v_" + n] for n in _WEIGHTS}
    mx, my, mc = _me()
    chip = 2 * mx + my
    small = [n for n in _WEIGHTS if n not in _BIG]
    big = list(_BIG)

    plan = _DistPlan(w, chip, mc)
    loss, gx, dg, db = _local_step(x[0], positions[0], loss_target[0], [plan.full["ln_g"][i][None] for i in range(DEPTH)],
                                   [plan.full["ln_b"][i][None] for i in range(DEPTH)], plan)
    per = {n: [] for n in small}
    for i in range(DEPTH):
        for n, a in plan.small_grads[i].items():
            per[n].append(a)
        per["ln_g"].append(dg[i][0])
        per["ln_b"].append(db[i][0])
    fg = {n: jnp.stack(v) for n, v in per.items()}
    gsh = plan.grad_shards(mc)

    out_g, out_d, out_m, out_v = {}, {}, {}, {}
    for n in big:
        g = gsh[n]
        sh = w[n].shape
        to2 = lambda a: a.reshape(-1, sh[-1])
        d_, m_, v_ = _adamw(to2(w[n]), g, to2(mom[n]), to2(vel[n]), name="adamw_" + n)
        out_g[n], out_d[n], out_m[n], out_v[n] = g.reshape(sh), d_.reshape(sh), m_.reshape(sh), v_.reshape(sh)

    summed = _sum8(_allgather_all(_pack([fg[n] for n in small] + [loss[0, :1]]), name="gather_small"), name="sum_small")
    sg = _unpack(summed, [fg[n].shape for n in small] + [(1,)])
    loss_total = sg[-1][0]
    gs = {}
    for n, g in zip(small, sg[:-1]):
        if n in _SMALL_SHARDED:
            ws = w[n].shape[-1]
            g = lax.dynamic_slice_in_dim(g, chip * ws, ws, axis=g.ndim - 1)
        gs[n] = g
    shapes = [w[n].shape for n in small]
    d_, m_, v_ = _adamw(_pack([w[n] for n in small]), _pack([gs[n] for n in small]), _pack([mom[n] for n in small]),
                        _pack([vel[n] for n in small]), name="adamw_small")
    for n, a, b, c in zip(small, _unpack(d_, shapes), _unpack(m_, shapes), _unpack(v_, shapes)):
        out_g[n], out_d[n], out_m[n], out_v[n] = gs[n], a, b, c

    return (loss_total, gx[None], *[out_g[n] for n in _WEIGHTS], *[out_d[n] for n in _WEIGHTS],
            *[out_m[n] for n in _WEIGHTS], *[out_v[n] for n in _WEIGHTS])
```

```python
import functools
import math

import jax
import jax.numpy as jnp
from jax import lax
from jax.experimental import pallas as pl
from jax.experimental.pallas import tpu as pltpu

f32 = jnp.float32
HI = lax.Precision.HIGHEST
MXU_DT = jnp.bfloat16
COMM_DT = jnp.bfloat16
MESH = pl.DeviceIdType.MESH

DEPTH = 4
LN_EPS = 1e-5
RMS_EPS = 1e-6
SSD_HEAD_DIM = 64
SSD_N_GROUPS = 8
SSD_D_STATE = 128
SSD_CONV = 4
SSD_CHUNK = 128
MLA_Q_RANK = 768
MLA_KV_RANK = 512
MLA_NOPE = 128
MLA_ROPE = 64
MLA_V = 128
ROPE_THETA = 10000.0
GDN_N_QK_HEADS = 16
GDN_N_V_HEADS = 32
GDN_DK = 128
GDN_DV = 128
GDN_CONV = 4
GDN_CHUNK = 64
ADAM_LR = 0.001
ADAM_B1 = 0.9
ADAM_B2 = 0.999
ADAM_EPS = 1e-08
ADAM_WD = 0.01
ADAM_STEP = 10

LANES = 128
VMEM_LIMIT = 48 * 1024 * 1024
ATT_TILE = 512
ATT_Q_TILE = 512
ATT_KEY_TILE = 512
ROW_TILE = 256
MM_TILE_M = 1024
MM_TILE_N = 1024
MM_TILE_K = 2048
MM_VMEM_BUDGET = 40 * 1024 * 1024
SSD_GROUPS_PER_STEP = 4
GDN_HEADS_PER_STEP = 16
GDN_FWD_HEADS_PER_STEP = 32


def _alpha():
    return (2.0 * DEPTH) ** 0.25


def _tile(n, pref, align=LANES):
    t = min(pref, n) // align * align
    while t >= align:
        if n % t == 0:
            return t
        t -= align
    return n


def _cp(sem=None):
    return pltpu.CompilerParams(dimension_semantics=sem, vmem_limit_bytes=VMEM_LIMIT)


def _iota(shape, dim):
    return lax.broadcasted_iota(jnp.int32, shape, dim)


def _div_pow2(x, p):
    assert p & (p - 1) == 0
    return lax.shift_right_logical(x, jnp.int32(p.bit_length() - 1))


def _dot(a, b, dims=((1,), (0,)), hi=False):
    if hi:
        return lax.dot_general(a.astype(f32), b.astype(f32), (dims, ((), ())), precision=HI, preferred_element_type=f32)
    return lax.dot_general(a.astype(MXU_DT), b.astype(MXU_DT), (dims, ((), ())), preferred_element_type=f32)


_NT = ((1,), (1,))
_TN = ((0,), (0,))


def _split3(x):
    hi = x.astype(jnp.bfloat16)
    r = x - hi.astype(f32)
    mid = r.astype(jnp.bfloat16)
    return hi, mid, (r - mid.astype(f32)).astype(jnp.bfloat16)


def _seldot_impl(a, b, dims, exact):
    def d(x, y):
        return lax.dot_general(x, y, (dims, ((), ())), preferred_element_type=f32)

    if exact == 0:
        a01 = a.astype(jnp.bfloat16)
        t = _split3(b.astype(f32))
        return (d(a01, t[0]) + d(a01, t[1])) + d(a01, t[2])
    b01 = b.astype(jnp.bfloat16)
    t = _split3(a.astype(f32))
    return (d(t[0], b01) + d(t[1], b01)) + d(t[2], b01)


@functools.partial(jax.custom_vjp, nondiff_argnums=(2, 3))
def _seldot(a, b, dims, exact):
    return _seldot_impl(a, b, dims, exact)


def _seldot_fwd(a, b, dims, exact):
    return _seldot_impl(a, b, dims, exact), (a, b)


def _seldot_bwd(dims, exact, res, dy):
    a, b = res
    (ca,), (cb,) = dims
    if exact == 0:
        assert ca == 1
        db = _seldot_impl(a, dy, _TN, 0) if cb == 0 else _seldot_impl(dy, a, _TN, 1)
        return jnp.zeros_like(a), db
    assert ca == 1 and cb == 0
    return _seldot_impl(dy, b, _NT, 1), jnp.zeros_like(b)


_seldot.defvjp(_seldot_fwd, _seldot_bwd)


def _softplus(x):
    return jnp.maximum(x, 0.0) + jnp.log1p(jnp.exp(-jnp.abs(x)))


def _silu(x):
    return x * jax.nn.sigmoid(x)


def _mm(a, b, *, name, ta=False, tb=False, add=None, add_scale=1.0, out_dtype=f32, comm=None):
    M, K = (a.shape[1], a.shape[0]) if ta else a.shape
    N = b.shape[0] if tb else b.shape[1]
    assert (b.shape[1] if tb else b.shape[0]) == K, (a.shape, b.shape, ta, tb)
    tm, tn, tk = _tile(M, MM_TILE_M), _tile(N, MM_TILE_N), _tile(K, MM_TILE_K)
    ab, bb = jnp.dtype(a.dtype).itemsize, jnp.dtype(b.dtype).itemsize
    while 2 * tk * (tm * ab + tn * bb) + 12 * tm * tn > MM_VMEM_BUDGET and tk % (2 * LANES) == 0:
        tk //= 2
    nk = K // tk
    a_spec = pl.BlockSpec((tk, tm), lambda i, j, k: (k, i)) if ta else pl.BlockSpec((tm, tk), lambda i, j, k: (i, k))
    b_spec = pl.BlockSpec((tn, tk), lambda i, j, k: (j, k)) if tb else pl.BlockSpec((tk, tn), lambda i, j, k: (k, j))
    o_spec = pl.BlockSpec((tm, tn), lambda i, j, k: (i, j))
    dims = ((0 if ta else 1,), (1 if tb else 0,))
    has_add = add is not None

    def body(*refs):
        a_ref, b_ref = refs[:2]
        add_ref = refs[2] if has_add else None
        o_ref = refs[3 if has_add else 2]

        def finish(r):
            if has_add:
                r = r + add_scale * add_ref[...].astype(f32)
            o_ref[...] = r.astype(out_dtype)

        if nk == 1:
            finish(_dot(a_ref[...], b_ref[...], dims))
            return
        acc = refs[-1]
        k = pl.program_id(2)

        @pl.when(k == 0)
        def _():
            acc[...] = jnp.zeros_like(acc)

        acc[...] += _dot(a_ref[...], b_ref[...], dims)

        @pl.when(k == nk - 1)
        def _():
            finish(acc[...])

    ins = [a, b] + ([add] if has_add else [])
    specs = [a_spec, b_spec] + ([o_spec] if has_add else [])
    (out,), cres = _call(
        body, comm, name=name, grid=(M // tm, N // tn, nk), in_specs=specs, out_specs=[o_spec],
        out_shape=[jax.ShapeDtypeStruct((M, N), out_dtype)], scratch_shapes=[pltpu.VMEM((tm, tn), f32)] if nk > 1 else [],
        sem=("parallel", "parallel", "arbitrary"), args=ins)
    return out if comm is None else (out, cres)


def _rw_specs(params, ins, ncol, tm):
    specs = []
    for arr, mode, bw, coff in params:
        if mode == "c":
            specs.append(pl.BlockSpec((1, bw), lambda c, r, coff=coff: (0, c + coff)))
        else:
            specs.append(pl.BlockSpec((1, bw), lambda c, r, coff=coff: (0, coff)))
    for arr, mode, bw, coff in ins:
        if mode == "c":
            specs.append(pl.BlockSpec((tm, bw), lambda c, r, coff=coff: (r, c + coff)))
        else:
            specs.append(pl.BlockSpec((tm, bw), lambda c, r, coff=coff: (r, coff)))
    return specs


def _norm_spec(lst):
    out = []
    for t in lst:
        arr, mode, bw = t[0], t[1], t[2]
        coff = t[3] if len(t) > 3 else 0
        out.append((arr, mode, bw, coff))
    return out


def _rowwise(fn, params, ins, outs, *, name, ncol=1, tm=None):
    params, ins = _norm_spec(params), _norm_spec(ins)
    S = ins[0][0].shape[0]
    tm = _tile(S, tm or ROW_TILE, 8)
    npar, nin = len(params), len(ins)

    def body(*refs):
        pv = [r[...].astype(f32) for r in refs[:npar]]
        iv = [r[...].astype(f32) for r in refs[npar:npar + nin]]
        res = fn(*pv, *iv)
        for o_ref, val in zip(refs[npar + nin:], res):
            o_ref[...] = val.astype(o_ref.dtype)

    out_specs, out_shapes = [], []
    for W, dt, mode, bw in outs:
        out_shapes.append(jax.ShapeDtypeStruct((S, W), dt))
        if mode == "c":
            out_specs.append(pl.BlockSpec((tm, bw), lambda c, r: (r, c)))
        else:
            out_specs.append(pl.BlockSpec((tm, bw), lambda c, r: (r, 0)))
    return pl.pallas_call(
        body, name=name, grid=(ncol, S // tm), in_specs=_rw_specs(params, ins, ncol, tm), out_specs=out_specs,
        out_shape=out_shapes, compiler_params=_cp(("parallel", "parallel")))(*[p[0] for p in params], *[i[0] for i in ins])


def _rowwise_bwd(fn, params, ins, couts, *, name, ncol=1, tm=None, diff_p=None, diff_i=None, din_dtypes=None, comm=None):
    params, ins, couts = _norm_spec(params), _norm_spec(ins), _norm_spec(couts)
    S = ins[0][0].shape[0]
    tm = _tile(S, tm or ROW_TILE, 8)
    npar, nin, nco = len(params), len(ins), len(couts)
    diff_p = list(range(npar)) if diff_p is None else diff_p
    diff_i = list(range(nin)) if diff_i is None else diff_i
    din_dtypes = [(f32,)] * len(diff_i) if din_dtypes is None else din_dtypes

    def body(*refs):
        c, r = pl.program_id(0), pl.program_id(1)
        pv = [x[...].astype(f32) for x in refs[:npar]]
        iv = [x[...].astype(f32) for x in refs[npar:npar + nin]]
        cv = [x[...].astype(f32) for x in refs[npar + nin:npar + nin + nco]]
        orefs = refs[npar + nin + nco:]

        def g(*dargs):
            p2, i2 = list(pv), list(iv)
            for n, k in enumerate(diff_p):
                p2[k] = dargs[n]
            for n, k in enumerate(diff_i):
                i2[k] = dargs[len(diff_p) + n]
            return tuple(fn(*p2, *i2))

        _, vjp = jax.vjp(g, *[pv[k] for k in diff_p], *[iv[k] for k in diff_i])
        grads = vjp(tuple(cv))
        for n, k in enumerate(diff_p):
            o_ref = orefs[n]
            first = (r == 0) if params[k][1] == "c" else jnp.logical_and(r == 0, c == 0)

            @pl.when(first)
            def _(o_ref=o_ref):
                o_ref[...] = jnp.zeros_like(o_ref)

            o_ref[...] += grads[n]
        pos = len(diff_p)
        for n, k in enumerate(diff_i):
            for _ in din_dtypes[n]:
                orefs[pos][...] = grads[len(diff_p) + n].astype(orefs[pos].dtype)
                pos += 1

    out_specs, out_shapes = [], []
    for k in diff_p:
        arr, mode, bw, coff = params[k]
        W = bw * ncol if mode == "c" else bw
        out_shapes.append(jax.ShapeDtypeStruct((1, W), f32))
        out_specs.append(pl.BlockSpec((1, bw), (lambda c, r: (0, c)) if mode == "c" else (lambda c, r: (0, 0))))
    for n, k in enumerate(diff_i):
        arr, mode, bw, coff = ins[k]
        W = bw * ncol if mode == "c" else bw
        for dt in din_dtypes[n]:
            out_shapes.append(jax.ShapeDtypeStruct((S, W), dt))
            out_specs.append(pl.BlockSpec((tm, bw), (lambda c, r: (r, c)) if mode == "c" else (lambda c, r: (r, 0))))
    res, cres = _call(
        body, comm, name=name, grid=(ncol, S // tm), in_specs=_rw_specs(params, ins + couts, ncol, tm), out_specs=out_specs,
        out_shape=out_shapes, scratch_shapes=[], sem=("arbitrary", "arbitrary"),
        args=(*[p[0] for p in params], *[i[0] for i in ins], *[c[0] for c in couts]))
    if comm is None:
        return list(res[:len(diff_p)]), list(res[len(diff_p):])
    return list(res[:len(diff_p)]), list(res[len(diff_p):]), cres


def _ln_fn(g, b, r):
    mu = jnp.mean(r, -1, keepdims=True)
    xc = r - mu
    var = jnp.mean(xc * xc, -1, keepdims=True)
    return (xc * lax.rsqrt(var + LN_EPS) * g + b,)


def _res_ln_fn(g, b, h, y):
    r = _alpha() * h + y
    hn = _ln_fn(g, b, r)
    return (r,) + hn + hn


def _rms_fn(w, x):
    return (x * lax.rsqrt(jnp.mean(x * x, -1, keepdims=True) + RMS_EPS) * w,)


def _ssd_gate_fn(w, y, z):
    yg = y * _silu(z)
    return (yg * lax.rsqrt(jnp.mean(yg * yg, -1, keepdims=True) + RMS_EPS) * w,)


def _mul_silu_fn(o, z):
    return (o * _silu(z),)


def _gdn_gate_fn(w, o, z):
    return (o * lax.rsqrt(jnp.mean(o * o, -1, keepdims=True) + RMS_EPS) * w * _silu(z),)


def _l2_fn(scale, x):
    return (x * lax.rsqrt(jnp.sum(x * x, -1, keepdims=True) + RMS_EPS) * scale,)


def _rope_fn(cos, sin, x):
    half = MLA_ROPE // 2
    i = _iota((LANES, LANES), 0)
    j = _iota((LANES, LANES), 1)
    pm = jnp.where((i == j + half) & (j < half), -1.0, 0.0) + jnp.where((i + half == j) & (j < 2 * half), 1.0, 0.0)
    return (x * cos + _seldot(x, pm.astype(f32), ((1,), (0,)), 1) * sin,)


def _conv_taps(x, K):
    S = x.shape[0]
    rows = _iota(x.shape, 0)
    return [x] + [jnp.where(rows < j, 0.0, pltpu.roll(x, j, 0)) for j in range(1, K)]


def _conv_fwd(x, w, b, *, name):
    S, C = x.shape
    K = w.shape[0]
    cw = _tile(C, LANES)

    def body(x_ref, w_ref, b_ref, o_ref):
        taps = _conv_taps(x_ref[...], K)
        wv = w_ref[...]
        pre = b_ref[...] + taps[0] * wv[K - 1:K, :]
        for j in range(1, K):
            pre = pre + taps[j] * wv[K - 1 - j:K - j, :]
        o_ref[...] = _silu(pre)

    return pl.pallas_call(
        body, name=name, grid=(C // cw,),
        in_specs=[pl.BlockSpec((S, cw), lambda c: (0, c)), pl.BlockSpec((K, cw), lambda c: (0, c)), pl.BlockSpec((1, cw), lambda c: (0, c))],
        out_specs=pl.BlockSpec((S, cw), lambda c: (0, c)), out_shape=jax.ShapeDtypeStruct((S, C), f32),
        compiler_params=_cp(("parallel",)))(x, w, b)


def _ssd_grouped_block(c, G, GW, N, cw):
    assert N == cw and GW % cw == 0
    nq = GW // cw
    per, nx = nq + 2, G * nq
    in_x = lax.div(c, nq) * per + lax.rem(c, nq)
    return jnp.where(c < nx, in_x, jnp.where(c < nx + G, (c - nx) * per + nq, (c - nx - G) * per + nq + 1))


def _conv_bwd(x, w, b, dys, *, name, dx_dtype=f32, dy_block=None):
    S, C = x.shape
    K = w.shape[0]
    cw = _tile(C, LANES)
    nblk = [d.shape[1] // cw for d in dys]
    offs = [sum(nblk[:p]) for p in range(len(dys))]
    assert sum(nblk) == C // cw and all(d.shape[1] % cw == 0 for d in dys)
    npc = len(dys)
    assert dy_block is None or npc == 1

    def body(x_ref, w_ref, b_ref, *refs):
        dy_refs, (dx_ref, dw_ref, db_ref) = refs[:npc], refs[npc:]
        if npc == 1:
            _conv_bwd_block(x_ref, w_ref, b_ref, dy_refs[0], dx_ref, dw_ref, db_ref, K, S, dx_dtype)
            return
        c = pl.program_id(0)
        for p in range(npc):
            @pl.when(jnp.logical_and(c >= offs[p], c < offs[p] + nblk[p]))
            def _(p=p):
                _conv_bwd_block(x_ref, w_ref, b_ref, dy_refs[p], dx_ref, dw_ref, db_ref, K, S, dx_dtype)

    col = lambda c: (0, c)
    if dy_block is not None:
        dy_specs = [pl.BlockSpec((S, cw), lambda c: (0, dy_block(c, cw)))]
    else:
        dy_specs = [pl.BlockSpec((S, cw), lambda c, o=offs[p], n=nblk[p]: (0, jnp.clip(c - o, 0, n - 1))) for p in range(npc)]
    return pl.pallas_call(
        body, name=name, grid=(C // cw,),
        in_specs=[pl.BlockSpec((S, cw), col), pl.BlockSpec((K, cw), col), pl.BlockSpec((1, cw), col)] + dy_specs,
        out_specs=[pl.BlockSpec((S, cw), col), pl.BlockSpec((K, cw), col), pl.BlockSpec((1, cw), col)],
        out_shape=[jax.ShapeDtypeStruct((S, C), dx_dtype), jax.ShapeDtypeStruct((K, C), f32), jax.ShapeDtypeStruct((1, C), f32)],
        compiler_params=_cp(("parallel",)))(x, w, b, *dys)


def _conv_bwd_block(x_ref, w_ref, b_ref, dy_ref, dx_ref, dw_ref, db_ref, K, S, dx_dtype):
    taps = _conv_taps(x_ref[...], K)
    wv = w_ref[...]
    pre = b_ref[...] + taps[0] * wv[K - 1:K, :]
    for j in range(1, K):
        pre = pre + taps[j] * wv[K - 1 - j:K - j, :]
    sg = jax.nn.sigmoid(pre)
    dpre = dy_ref[...] * sg * (1.0 + pre * (1.0 - sg))
    db_ref[...] = jnp.sum(dpre, axis=0, keepdims=True)
    rows = _iota(dpre.shape, 0)
    dx = dpre * wv[K - 1:K, :]
    dw_ref[K - 1:K, :] = jnp.sum(dpre * taps[0], axis=0, keepdims=True)
    for j in range(1, K):
        dw_ref[K - 1 - j:K - j, :] = jnp.sum(dpre * taps[j], axis=0, keepdims=True)
        up = jnp.where(rows >= S - j, 0.0, pltpu.roll(dpre, S - j, 0))
        dx = dx + up * wv[K - 1 - j:K - j, :]
    dx_ref[...] = dx.astype(dx_dtype)


def _ssd_chunk(prev, xs, Bm, Cm, dtr, dtb, alog, dsk, g, *, R, P):
    L, GW = xs.shape
    H = dtr.shape[1]
    tril = _iota((L, L), 0) >= _iota((L, L), 1)
    dt = _softplus(dtr + dtb)
    acs = _seldot(tril.astype(f32), dt * (-jnp.exp(alog)), ((1,), (0,)), 0)
    expand = (_iota((H, GW), 0) == g * R + _div_pow2(_iota((H, GW), 1), P)).astype(f32)
    dt_e = _seldot(dt, expand, ((1,), (0,)), 1)
    acs_e = _seldot(acs, expand, ((1,), (0,)), 1)
    d_e = jnp.sum(_seldot(jnp.broadcast_to(dsk, (8, H)), expand, ((1,), (0,)), 1), axis=0, keepdims=True) * 0.125
    last = jnp.sum(jnp.where(_iota((L, GW), 0) == L - 1, acs_e, 0.0), axis=0, keepdims=True)
    xdt = xs * dt_e
    cb = _dot(Cm, Bm, _NT)
    nsel = max(R, 8)
    sel = (_iota((nsel, H), 1) == g * R + _iota((nsel, H), 0)).astype(f32)
    acs_t = _seldot(sel, acs, _NT, 0)
    hp = LANES // P
    pieces = []
    for p in range(GW // LANES):
        xp = xdt[:, p * LANES:(p + 1) * LANES]
        acc = None
        for q in range(hp):
            r = p * hp + q
            col = jnp.sum(jnp.where(_iota((L, H), 1) == g * R + r, acs, 0.0), axis=1, keepdims=True)
            row = jnp.sum(jnp.where(_iota((nsel, L), 0) == r, acs_t, 0.0), axis=0, keepdims=True)
            dec = jnp.where(tril, jnp.exp(jnp.where(tril, col - row, 0.0)), 0.0)
            xm = jnp.where(_div_pow2(_iota((L, LANES), 1), P) == q, xp, 0.0)
            t = _dot(cb * dec, xm)
            acc = t if acc is None else acc + t
        pieces.append(acc)
    y_diag = pieces[0] if len(pieces) == 1 else jnp.concatenate(pieces, axis=1)
    st = _dot(Bm, xdt * jnp.exp(last - acs_e), _TN)
    y_off = _dot(Cm, prev) * jnp.exp(acs_e)
    new = prev * jnp.exp(last) + st
    return y_diag + y_off + xs * d_e, new


def _ssd_dims(xbc, dtr):
    S, CD = xbc.shape
    H = dtr.shape[1]
    G, N, P = SSD_N_GROUPS, SSD_D_STATE, SSD_HEAD_DIM
    DI = H * P
    R = H // G
    assert CD == DI + 2 * G * N and DI % N == 0
    return S, H, G, N, P, DI, R, R * P, SSD_CHUNK


def _ssd_groups_per_step(G, DI, N):
    GB = min(SSD_GROUPS_PER_STEP, G)
    assert G % GB == 0 and (DI // N) % GB == 0
    return GB


def _ssd_scan_fwd(xbc, dtr, dtb, alog, dsk, *, name, comm=None):
    S, H, G, N, P, DI, R, GW, L = _ssd_dims(xbc, dtr)
    nc = S // L
    GB = _ssd_groups_per_step(G, DI, N)
    boff, coff = DI // N // GB, (DI // N + G) // GB

    def body(xs_ref, b_ref, c_ref, dtr_ref, dtb_ref, alog_ref, dsk_ref, y_ref, st_ref, state):
        c, gb = pl.program_id(0), pl.program_id(1)
        for gg in range(GB):
            g = gb * GB + gg

            @pl.when(c == 0)
            def _(g=g):
                state[g] = jnp.zeros((N, GW), f32)

            prev = state[g]
            st_ref[0, gg] = prev
            y, new = _ssd_chunk(prev, xs_ref[:, gg * GW:(gg + 1) * GW], b_ref[:, gg * N:(gg + 1) * N], c_ref[:, gg * N:(gg + 1) * N],
                                dtr_ref[...], dtb_ref[...], alog_ref[...], dsk_ref[...], g, R=R, P=P)
            y_ref[:, gg * GW:(gg + 1) * GW] = y
            state[g] = new

    par = pl.BlockSpec((1, H), lambda c, g: (0, 0))
    return _call(
        body, comm, name=name, grid=(nc, G // GB),
        in_specs=[pl.BlockSpec((L, GB * GW), lambda c, g: (c, g)), pl.BlockSpec((L, GB * N), lambda c, g: (c, boff + g)),
                  pl.BlockSpec((L, GB * N), lambda c, g: (c, coff + g)), pl.BlockSpec((L, H), lambda c, g: (c, 0)), par, par, par],
        out_specs=[pl.BlockSpec((L, GB * GW), lambda c, g: (c, g)), pl.BlockSpec((1, GB, N, GW), lambda c, g: (c, g, 0, 0))],
        out_shape=[jax.ShapeDtypeStruct((S, DI), f32), jax.ShapeDtypeStruct((nc, G, N, GW), f32)],
        scratch_shapes=[pltpu.VMEM((G, N, GW), f32)],
        sem=("arbitrary", "arbitrary"), args=(xbc, xbc, xbc, dtr, dtb, alog, dsk))


def _ssd_scan_bwd(xbc, dtr, dtb, alog, dsk, states, dy, *, name, comm=None):
    S, H, G, N, P, DI, R, GW, L = _ssd_dims(xbc, dtr)
    nc = S // L
    GB = _ssd_groups_per_step(G, DI, N)
    boff, coff = DI // N // GB, (DI // N + G) // GB
    PW = GW + 2 * N

    def body(xs_ref, b_ref, c_ref, dtr_ref, dtb_ref, alog_ref, dsk_ref, st_ref, dy_ref,
             dg_ref, ddtr_ref, ddtb_ref, dalog_ref, ddsk_ref, dstate):
        c, gb = pl.program_id(0), pl.program_id(1)

        @pl.when(jnp.logical_and(c == 0, gb == 0))
        def _():
            ddtb_ref[...] = jnp.zeros_like(ddtb_ref)
            dalog_ref[...] = jnp.zeros_like(dalog_ref)
            ddsk_ref[...] = jnp.zeros_like(ddsk_ref)

        @pl.when(gb == 0)
        def _():
            ddtr_ref[...] = jnp.zeros_like(ddtr_ref)

        for gg in range(GB):
            g = gb * GB + gg

            @pl.when(c == 0)
            def _(g=g):
                dstate[g] = jnp.zeros((N, GW), f32)

            fn = functools.partial(_ssd_chunk, g=g, R=R, P=P)
            _, vjp = jax.vjp(fn, st_ref[0, gg], xs_ref[:, gg * GW:(gg + 1) * GW], b_ref[:, gg * N:(gg + 1) * N],
                             c_ref[:, gg * N:(gg + 1) * N], dtr_ref[...], dtb_ref[...], alog_ref[...], dsk_ref[...])
            dprev, dxs, dB, dC, ddtr, ddtb, dalog, ddsk = vjp((dy_ref[:, gg * GW:(gg + 1) * GW], dstate[g]))
            dstate[g] = dprev
            dg_ref[:, gg * PW:gg * PW + GW] = dxs
            dg_ref[:, gg * PW + GW:gg * PW + GW + N] = dB
            dg_ref[:, gg * PW + GW + N:(gg + 1) * PW] = dC
            ddtr_ref[...] += ddtr
            ddtb_ref[...] += ddtb
            dalog_ref[...] += dalog
            ddsk_ref[...] += ddsk

    rc = lambda c: nc - 1 - c
    par = pl.BlockSpec((1, H), lambda c, g: (0, 0))
    return _call(
        body, comm, name=name, grid=(nc, G // GB),
        in_specs=[pl.BlockSpec((L, GB * GW), lambda c, g: (rc(c), g)), pl.BlockSpec((L, GB * N), lambda c, g: (rc(c), boff + g)),
                  pl.BlockSpec((L, GB * N), lambda c, g: (rc(c), coff + g)), pl.BlockSpec((L, H), lambda c, g: (rc(c), 0)),
                  par, par, par, pl.BlockSpec((1, GB, N, GW), lambda c, g: (rc(c), g, 0, 0)),
                  pl.BlockSpec((L, GB * GW), lambda c, g: (rc(c), g))],
        out_specs=[pl.BlockSpec((L, GB * PW), lambda c, g: (rc(c), g)), pl.BlockSpec((L, H), lambda c, g: (rc(c), 0)), par, par, par],
        out_shape=[jax.ShapeDtypeStruct((S, G * (GW + 2 * N)), f32), jax.ShapeDtypeStruct((S, H), f32)] + [jax.ShapeDtypeStruct((1, H), f32)] * 3,
        scratch_shapes=[pltpu.VMEM((G, N, GW), f32)],
        sem=("arbitrary", "arbitrary"), args=(xbc, xbc, xbc, dtr, dtb, alog, dsk, states, dy))


def _dot3(a, b, dims=((1,), (0,))):
    def split(x):
        hi = x.astype(jnp.bfloat16)
        return hi, (x - hi.astype(f32)).astype(jnp.bfloat16)

    def d(x, y):
        return lax.dot_general(x, y, (dims, ((), ())), preferred_element_type=f32)

    ah, al = split(a)
    bh, bl = split(b)
    return d(ah, bh) + (d(ah, bl) + d(al, bh))


def _neumann_inverses(As):
    L = As[0].shape[0]
    eye = (_iota((L, L), 0) == _iota((L, L), 1)).astype(f32)
    X = [-A for A in As]
    P = [eye + x for x in X]
    n = 1
    while 2 * n < L:
        X = [_dot3(x, x) for x in X]
        P = [p + _dot3(p, x) for p, x in zip(P, X)]
        n *= 2
    return P


@jax.custom_vjp
def _unit_lower_solves(Ts, As, Rs):
    return tuple(_dot3(T, R) for T, R in zip(Ts, Rs))


def _uls_fwd(Ts, As, Rs):
    Xs = tuple(_dot3(T, R) for T, R in zip(Ts, Rs))
    return Xs, (Ts, Xs)


def _uls_bwd(res, dXs):
    Ts, Xs = res
    dRs = tuple(_dot3(T, dX, _TN) for T, dX in zip(Ts, dXs))
    dAs = tuple(-_dot3(dR, X, _NT) for dR, X in zip(dRs, Xs))
    return tuple(jnp.zeros_like(T) for T in Ts), dAs, dRs


_unit_lower_solves.defvjp(_uls_fwd, _uls_bwd)


def _gdn_step(states, qb, kb_, vb, br, ar, alog, dtb, h0, *, rep, inverses=None):
    HB = len(states)
    L = qb.shape[0]
    DK, DV = states[0].shape
    HV = br.shape[1]
    incl = _iota((L, L), 0) >= _iota((L, L), 1)
    strict = _iota((L, L), 0) > _iota((L, L), 1)
    lane = _iota((L, HV), 1)
    g_all = -jnp.exp(alog) * _softplus(ar + dtb)
    gcs = _seldot(incl.astype(f32), g_all, ((1,), (0,)), 0)
    beta_all = jax.nn.sigmoid(br)
    nsel = max(HV, 8)
    gcs_t = _seldot((_iota((nsel, HV), 0) == _iota((nsel, HV), 1)).astype(f32), gcs, _NT, 0)
    hs = range(HB)
    q = [qb[:, (hh // rep) * DK:(hh // rep + 1) * DK] for hh in hs]
    k = [kb_[:, (hh // rep) * DK:(hh // rep + 1) * DK] for hh in hs]
    v = [vb[:, hh * DV:(hh + 1) * DV] for hh in hs]
    gc = [jnp.sum(jnp.where(lane == h0 + hh, gcs, 0.0), axis=1, keepdims=True) for hh in hs]
    beta = [jnp.sum(jnp.where(lane == h0 + hh, beta_all, 0.0), axis=1, keepdims=True) for hh in hs]
    gc_row = [jnp.sum(jnp.where(_iota((nsel, L), 0) == h0 + hh, gcs_t, 0.0), axis=0, keepdims=True) for hh in hs]
    decay = [jnp.where(incl, jnp.exp(jnp.where(incl, gc[hh] - gc_row[hh], 0.0)), 0.0) for hh in hs]
    kbeta = [k[hh] * beta[hh] for hh in hs]
    a_mat = [jnp.where(strict, _dot(kbeta[hh], k[hh], _NT) * decay[hh], 0.0) for hh in hs]
    eg = [jnp.exp(gc[hh]) for hh in hs]
    rhs = tuple(jnp.concatenate([v[hh] * beta[hh], kbeta[hh] * eg[hh]], axis=1) for hh in hs)
    if inverses is None:
        made = tuple(_neumann_inverses(a_mat))
        sol = tuple(_dot3(T, R) for T, R in zip(made, rhs))
    else:
        sol = _unit_lower_solves(tuple(inverses), tuple(a_mat), rhs)
    qk = [jnp.where(incl, _dot(q[hh], k[hh], _NT) * decay[hh], 0.0) for hh in hs]
    g_last = [jnp.sum(jnp.where(_iota((L, 1), 0) == L - 1, gc[hh], 0.0), axis=0, keepdims=True) for hh in hs]
    v_new = [sol[hh][:, :DV] - _dot(sol[hh][:, DV:], states[hh]) for hh in hs]
    outs = [_dot(q[hh] * eg[hh], states[hh]) + _dot(qk[hh], v_new[hh]) for hh in hs]
    news = [states[hh] * jnp.exp(g_last[hh]) + _dot(k[hh] * jnp.exp(g_last[hh] - gc[hh]), v_new[hh], _TN) for hh in hs]
    o = outs[0] if HB == 1 else jnp.concatenate(outs, axis=1)
    return (o, tuple(news), made) if inverses is None else (o, tuple(news))


def _gdn_dims(heads_per_step):
    HK, HV = GDN_N_QK_HEADS, GDN_N_V_HEADS
    rep = HV // HK
    HB = min(heads_per_step, HV)
    assert HV % HB == 0 and HB % rep == 0
    return HK, HV, GDN_DK, GDN_DV, GDN_CHUNK, rep, HB


def _gdn_scan_fwd(qkn, qkv, br, ar, alog, dtb, *, name, comm=None):
    S = qkn.shape[0]
    HK, HV, DK, DV, L, rep, HB = _gdn_dims(GDN_FWD_HEADS_PER_STEP)
    nc = S // L
    QW = HB // rep * DK
    koff = HK * DK // QW
    voff = 2 * HK * DK // (HB * DV)

    def body(q_ref, k_ref, v_ref, br_ref, ar_ref, alog_ref, dtb_ref, o_ref, st_ref, inv_ref, state):
        c, hb = pl.program_id(0), pl.program_id(1)
        h0 = hb * HB

        @pl.when(c == 0)
        def _():
            for hh in range(HB):
                state[h0 + hh] = jnp.zeros((DK, DV), f32)

        prev = tuple(state[h0 + hh] for hh in range(HB))
        for hh in range(HB):
            st_ref[0, hh] = prev[hh]
        o, new, inv = _gdn_step(prev, q_ref[...], k_ref[...], v_ref[...], br_ref[...], ar_ref[...], alog_ref[...], dtb_ref[...],
                                h0, rep=rep)
        o_ref[...] = o
        for hh in range(HB):
            state[h0 + hh] = new[hh]
            inv_ref[0, hh] = inv[hh]

    par = pl.BlockSpec((1, HV), lambda c, h: (0, 0))
    return _call(
        body, comm, name=name, grid=(nc, HV // HB),
        in_specs=[pl.BlockSpec((L, QW), lambda c, h: (c, h)), pl.BlockSpec((L, QW), lambda c, h: (c, koff + h)),
                  pl.BlockSpec((L, HB * DV), lambda c, h: (c, voff + h)), pl.BlockSpec((L, HV), lambda c, h: (c, 0)),
                  pl.BlockSpec((L, HV), lambda c, h: (c, 0)), par, par],
        out_specs=[pl.BlockSpec((L, HB * DV), lambda c, h: (c, h)), pl.BlockSpec((1, HB, DK, DV), lambda c, h: (c, h, 0, 0)),
                   pl.BlockSpec((1, HB, L, L), lambda c, h: (c, h, 0, 0))],
        out_shape=[jax.ShapeDtypeStruct((S, HV * DV), f32), jax.ShapeDtypeStruct((nc, HV, DK, DV), f32),
                   jax.ShapeDtypeStruct((nc, HV, L, L), f32)],
        scratch_shapes=[pltpu.VMEM((HV, DK, DV), f32)],
        sem=("arbitrary", "arbitrary"), args=(qkn, qkn, qkv, br, ar, alog, dtb))


def _gdn_scan_bwd(qkn, qkv, br, ar, alog, dtb, states, inverses, do, *, name, comm=None):
    S = qkn.shape[0]
    HK, HV, DK, DV, L, rep, HB = _gdn_dims(GDN_HEADS_PER_STEP)
    nc = S // L
    QW = HB // rep * DK
    koff = HK * DK // QW
    voff = 2 * HK * DK // (HB * DV)

    def body(q_ref, k_ref, v_ref, br_ref, ar_ref, alog_ref, dtb_ref, st_ref, inv_ref, do_ref,
             dq_ref, dk_ref, dv_ref, dbr_ref, dar_ref, dalog_ref, ddtb_ref, dstate):
        c, hb = pl.program_id(0), pl.program_id(1)
        h0 = hb * HB

        @pl.when(c == 0)
        def _():
            for hh in range(HB):
                dstate[h0 + hh] = jnp.zeros((DK, DV), f32)

        @pl.when(jnp.logical_and(c == 0, hb == 0))
        def _():
            dalog_ref[...] = jnp.zeros_like(dalog_ref)
            ddtb_ref[...] = jnp.zeros_like(ddtb_ref)

        @pl.when(hb == 0)
        def _():
            dbr_ref[...] = jnp.zeros_like(dbr_ref)
            dar_ref[...] = jnp.zeros_like(dar_ref)

        fn = functools.partial(_gdn_step, h0=h0, rep=rep, inverses=tuple(inv_ref[0, hh] for hh in range(HB)))
        prev = tuple(st_ref[0, hh] for hh in range(HB))
        _, vjp = jax.vjp(fn, prev, q_ref[...], k_ref[...], v_ref[...], br_ref[...], ar_ref[...], alog_ref[...], dtb_ref[...])
        dprev, dq, dk, dv, dbr, dar, dalog, ddtb = vjp((do_ref[...], tuple(dstate[h0 + hh] for hh in range(HB))))
        for hh in range(HB):
            dstate[h0 + hh] = dprev[hh]
        dq_ref[...] = dq
        dk_ref[...] = dk
        dv_ref[...] = dv
        dbr_ref[...] += dbr
        dar_ref[...] += dar
        dalog_ref[...] += dalog
        ddtb_ref[...] += ddtb

    rc = lambda c: nc - 1 - c
    par = pl.BlockSpec((1, HV), lambda c, h: (0, 0))
    blk = lambda W: pl.BlockSpec((L, W), lambda c, h: (rc(c), h))
    return _call(
        body, comm, name=name, grid=(nc, HV // HB),
        in_specs=[pl.BlockSpec((L, QW), lambda c, h: (rc(c), h)), pl.BlockSpec((L, QW), lambda c, h: (rc(c), koff + h)),
                  pl.BlockSpec((L, HB * DV), lambda c, h: (rc(c), voff + h)), pl.BlockSpec((L, HV), lambda c, h: (rc(c), 0)),
                  pl.BlockSpec((L, HV), lambda c, h: (rc(c), 0)), par, par,
                  pl.BlockSpec((1, HB, DK, DV), lambda c, h: (rc(c), h, 0, 0)),
                  pl.BlockSpec((1, HB, L, L), lambda c, h: (rc(c), h, 0, 0)), blk(HB * DV)],
        out_specs=[blk(QW), blk(QW), blk(HB * DV), pl.BlockSpec((L, HV), lambda c, h: (rc(c), 0)),
                   pl.BlockSpec((L, HV), lambda c, h: (rc(c), 0)), par, par],
        out_shape=[jax.ShapeDtypeStruct((S, HK * DK), f32), jax.ShapeDtypeStruct((S, HK * DK), f32), jax.ShapeDtypeStruct((S, HV * DV), f32),
                   jax.ShapeDtypeStruct((S, HV), f32), jax.ShapeDtypeStruct((S, HV), f32),
                   jax.ShapeDtypeStruct((1, HV), f32), jax.ShapeDtypeStruct((1, HV), f32)],
        scratch_shapes=[pltpu.VMEM((HV, DK, DV), f32)],
        sem=("arbitrary", "arbitrary"), args=(qkn, qkn, qkv, br, ar, alog, dtb, states, inverses, do))


def _att_scale():
    return (MLA_NOPE + MLA_ROPE) ** -0.5


def _causal(s, i, j, t, tk=None):
    qpos = i * t + _iota(s.shape, 0)
    kpos = j * (t if tk is None else tk) + _iota(s.shape, 1)
    return kpos <= qpos


def _attn_fwd(qn, qr, kn, kr, v, *, name, comm=None):
    S, W = qn.shape
    H = W // LANES
    t = _tile(S, ATT_Q_TILE)
    tk = _tile(S, ATT_KEY_TILE)
    assert tk % t == 0
    scale = _att_scale()

    def body(qn_ref, qr_ref, kn_ref, kr_ref, v_ref, o_ref, lse_ref):
        i = pl.program_id(1)
        qc = jnp.concatenate([qn_ref[...], qr_ref[...]], axis=1)

        def step(j, carry, masked):
            m, l, acc = carry
            rows = pl.ds(pl.multiple_of(j * tk, tk), tk)
            s = _dot(qc, jnp.concatenate([kn_ref[rows, :], kr_ref[rows, :]], axis=1), _NT) * scale
            if masked:
                s = jnp.where(_causal(s, i, j, t, tk), s, -1e30)
            m_new = jnp.maximum(m, jnp.max(s, axis=1, keepdims=True))
            p = jnp.exp(s - m_new)
            a = jnp.exp(m - m_new)
            return m_new, a * l + jnp.sum(p, axis=1, keepdims=True), a * acc + _dot(p, v_ref[rows, :])

        nfull = lax.div(i * t, tk)
        carry = lax.fori_loop(0, nfull, functools.partial(step, masked=False),
                              (jnp.full((t, 1), -1e30, f32), jnp.zeros((t, 1), f32), jnp.zeros((t, LANES), f32)))
        m, l, acc = step(nfull, carry, True)
        o_ref[...] = acc / l
        lse_ref[...] = jnp.broadcast_to(m + jnp.log(l), (t, LANES))

    qb = pl.BlockSpec((t, LANES), lambda h, i: (i, h))
    kb = pl.BlockSpec((S, LANES), lambda h, i: (0, h))
    return _call(
        body, comm, name=name, grid=(H, S // t),
        in_specs=[qb, qb, kb, pl.BlockSpec((S, LANES), lambda h, i: (0, 0)), kb],
        out_specs=[qb, qb], out_shape=[jax.ShapeDtypeStruct((S, W), f32), jax.ShapeDtypeStruct((S, W), f32)],
        scratch_shapes=[], sem=("parallel", "arbitrary"), args=(qn, qr, kn, kr, v))


def _attn_bwd_dq(qn, qr, kn, kr, v, o, lse, do, *, name, comm=None):
    S, W = qn.shape
    H = W // LANES
    t = _tile(S, ATT_TILE)
    scale = _att_scale()

    def body(qn_ref, qr_ref, kn_ref, kr_ref, v_ref, o_ref, lse_ref, do_ref, dqn_ref, dqr_ref):
        i = pl.program_id(1)
        qc = jnp.concatenate([qn_ref[...], qr_ref[...]], axis=1)
        dov = do_ref[...]
        delta = jnp.sum(dov * o_ref[...], axis=1, keepdims=True)
        lsev = lse_ref[...][:, :1]

        def step(j, dq, masked):
            rows = pl.ds(pl.multiple_of(j * t, t), t)
            kc = jnp.concatenate([kn_ref[rows, :], kr_ref[rows, :]], axis=1)
            s = _dot(qc, kc, _NT) * scale
            p = jnp.exp(s - lsev)
            if masked:
                p = jnp.where(_causal(s, i, j, t), p, 0.0)
            ds = p * (_dot(dov, v_ref[rows, :], _NT) - delta)
            return dq + _dot(ds, kc)

        dq = lax.fori_loop(0, i, functools.partial(step, masked=False), jnp.zeros((t, 2 * LANES), f32))
        dq = step(i, dq, True)
        dq = dq * scale
        dqn_ref[...] = dq[:, :LANES].astype(MXU_DT)
        dqr_ref[...] = dq[:, LANES:]

    qb = pl.BlockSpec((t, LANES), lambda h, i: (i, h))
    kb = pl.BlockSpec((S, LANES), lambda h, i: (0, h))
    return _call(
        body, comm, name=name, grid=(H, S // t),
        in_specs=[qb, qb, kb, pl.BlockSpec((S, LANES), lambda h, i: (0, 0)), kb, qb, qb, qb],
        out_specs=[qb, qb], out_shape=[jax.ShapeDtypeStruct((S, W), MXU_DT), jax.ShapeDtypeStruct((S, W), f32)],
        scratch_shapes=[], sem=("parallel", "arbitrary"), args=(qn, qr, kn, kr, v, o, lse, do))


def _attn_bwd_dkv(qn, qr, kn, kr, v, o, lse, do, *, name, comm=None):
    S, W = qn.shape
    H = W // LANES
    t = _tile(S, ATT_TILE)
    nb = S // t
    scale = _att_scale()

    def body(qn_ref, qr_ref, kn_ref, kr_ref, v_ref, o_ref, lse_ref, do_ref, dkn_ref, dkr_ref, dv_ref):
        j, h = pl.program_id(0), pl.program_id(1)
        kc = jnp.concatenate([kn_ref[...], kr_ref[...]], axis=1)
        vv = v_ref[...]

        def step(i, carry, masked):
            dk, dv = carry
            rows = pl.ds(pl.multiple_of(i * t, t), t)
            qc = jnp.concatenate([qn_ref[rows, :], qr_ref[rows, :]], axis=1)
            dov = do_ref[rows, :]
            delta = jnp.sum(dov * o_ref[rows, :], axis=1, keepdims=True)
            s = _dot(qc, kc, _NT) * scale
            p = jnp.exp(s - lse_ref[rows, :][:, :1])
            if masked:
                p = jnp.where(_causal(s, i, j, t), p, 0.0)
            ds = p * (_dot(dov, vv, _NT) - delta)
            return dk + _dot(ds, qc, _TN), dv + _dot(p, dov, _TN)

        carry = step(j, (jnp.zeros((t, 2 * LANES), f32), jnp.zeros((t, LANES), f32)), True)
        dk, dv = lax.fori_loop(j + 1, nb, functools.partial(step, masked=False), carry)
        dk = dk * scale
        dkn_ref[...] = dk[:, :LANES].astype(MXU_DT)
        dv_ref[...] = dv.astype(MXU_DT)

        @pl.when(h == 0)
        def _():
            dkr_ref[...] = jnp.zeros_like(dkr_ref)

        dkr_ref[...] += dk[:, LANES:]

    full = pl.BlockSpec((S, LANES), lambda j, h: (0, h))
    kb = pl.BlockSpec((t, LANES), lambda j, h: (j, h))
    k0 = pl.BlockSpec((t, LANES), lambda j, h: (j, 0))
    return _call(
        body, comm, name=name, grid=(nb, H),
        in_specs=[full, full, kb, k0, kb, full, full, full],
        out_specs=[kb, k0, kb],
        out_shape=[jax.ShapeDtypeStruct((S, W), MXU_DT), jax.ShapeDtypeStruct((S, LANES), f32), jax.ShapeDtypeStruct((S, W), MXU_DT)],
        scratch_shapes=[], sem=("arbitrary", "arbitrary"), args=(qn, qr, kn, kr, v, o, lse, do))


def _loss_head(y, target, *, name):
    S, D = y.shape
    tm = _tile(S, ROW_TILE, 8)

    def body(y_ref, t_ref, loss_ref, dy_ref):
        @pl.when(pl.program_id(0) == 0)
        def _():
            loss_ref[...] = jnp.zeros_like(loss_ref)

        e = y_ref[...] - t_ref[...]
        dy_ref[...] = e / D
        part = 0.5 * jnp.sum(jnp.mean(e * e, axis=1, keepdims=True), axis=0, keepdims=True)
        loss_ref[...] += jnp.broadcast_to(part, loss_ref.shape)

    rb = pl.BlockSpec((tm, D), lambda r: (r, 0))
    return pl.pallas_call(
        body, name=name, grid=(S // tm,), in_specs=[rb, rb],
        out_specs=[pl.BlockSpec((1, LANES), lambda r: (0, 0)), rb],
        out_shape=[jax.ShapeDtypeStruct((1, LANES), f32), jax.ShapeDtypeStruct((S, D), f32)],
        compiler_params=_cp(("arbitrary",)))(y, target)


def _adamw(w, g, m, v, *, name):
    R, C = w.shape
    tm = _tile(R, max(8, (1 << 19) // max(C, 1) // 8 * 8), 8)

    def body(w_ref, g_ref, m_ref, v_ref, d_ref, nm_ref, nv_ref):
        gv = g_ref[...]
        nm = ADAM_B1 * m_ref[...] + (1.0 - ADAM_B1) * gv
        nv = ADAM_B2 * v_ref[...] + (1.0 - ADAM_B2) * (gv * gv)
        m_hat = nm / (1.0 - ADAM_B1 ** ADAM_STEP)
        v_hat = nv / (1.0 - ADAM_B2 ** ADAM_STEP)
        d_ref[...] = -ADAM_LR * (m_hat / (jnp.sqrt(v_hat) + ADAM_EPS) + ADAM_WD * w_ref[...])
        nm_ref[...] = nm
        nv_ref[...] = nv

    rb = pl.BlockSpec((tm, C), lambda r: (r, 0))
    sh = jax.ShapeDtypeStruct((R, C), f32)
    return pl.pallas_call(body, name=name, grid=(R // tm,), in_specs=[rb] * 4, out_specs=[rb] * 3, out_shape=[sh] * 3,
                          compiler_params=_cp(("parallel",)))(w, g, m, v)


def _me():
    return lax.axis_index("x"), lax.axis_index("y"), lax.axis_index("c")


def _other_chips(mx, my):
    return [(1 - mx, my), (mx, 1 - my), (1 - mx, 1 - my)]


_ANY = pl.BlockSpec(memory_space=pl.ANY)


class _GatherChips:
    def __init__(self, xs):
        self.arrays = list(xs)
        n = len(xs)
        for x in xs:
            assert x.shape[0] % 2 == 0
        self.halves = [x.shape[0] // 2 for x in xs]
        self.out_shapes = [jax.ShapeDtypeStruct((4,) + x.shape, x.dtype) for x in xs]
        self.scratch = [pltpu.SemaphoreType.DMA((n, 6)), pltpu.SemaphoreType.DMA((n, 6))]

    def _sends(self, x_refs, o_refs, send, recv):
        mx, my, mc = _me()
        me = 2 * mx + my
        out = []
        for t, hf in enumerate(self.halves):
            mine = pl.ds(mc * hf, hf)
            for j, (cx, cy) in enumerate(_other_chips(mx, my)):
                out.append(pltpu.make_async_remote_copy(x_refs[t].at[mine], o_refs[t].at[me, mine], send.at[t, j], recv.at[t, j],
                                                        device_id=(cx, cy, mc), device_id_type=MESH))
        return out

    def start(self, x_refs, o_refs, scr):
        for cp in self._sends(x_refs, o_refs, *scr):
            cp.start()

    def finish(self, x_refs, o_refs, scr):
        send, recv = scr
        mx, my, mc = _me()
        chips = _other_chips(mx, my)
        fwd = []
        for t, hf in enumerate(self.halves):
            mine = pl.ds(mc * hf, hf)
            for j, (cx, cy) in enumerate(chips):
                k = 2 * cx + cy
                pltpu.make_async_remote_copy(x_refs[t].at[mine], o_refs[t].at[k, mine], send.at[t, j], recv.at[t, j],
                                             device_id=(cx, cy, mc), device_id_type=MESH).wait_recv()
                cp = pltpu.make_async_remote_copy(o_refs[t].at[k, mine], o_refs[t].at[k, mine], send.at[t, 3 + j], recv.at[t, 3 + j],
                                                  device_id=(mx, my, 1 - mc), device_id_type=MESH)
                cp.start()
                fwd.append(cp)
        for t, hf in enumerate(self.halves):
            theirs = pl.ds((1 - mc) * hf, hf)
            for j, (cx, cy) in enumerate(chips):
                k = 2 * cx + cy
                pltpu.make_async_remote_copy(o_refs[t].at[k, theirs], o_refs[t].at[k, theirs], send.at[t, 3 + j], recv.at[t, 3 + j],
                                             device_id=(mx, my, 1 - mc), device_id_type=MESH).wait_recv()
        for cp in self._sends(x_refs, o_refs, send, recv) + fwd:
            cp.wait_send()


class _ScatterChips:
    def __init__(self, ps):
        self.arrays = list(ps)
        n = len(ps)
        self.out_shapes = [jax.ShapeDtypeStruct((3,) + p.shape[1:], p.dtype) for p in ps]
        self.scratch = [pltpu.SemaphoreType.DMA((n, 3)), pltpu.SemaphoreType.DMA((n, 3))]

    def _copies(self, p_refs, o_refs, send, recv):
        mx, my, mc = _me()
        return [pltpu.make_async_remote_copy(p_refs[t].at[2 * cx + cy], o_refs[t].at[j], send.at[t, j], recv.at[t, j],
                                             device_id=(cx, cy, mc), device_id_type=MESH)
                for t in range(len(self.arrays)) for j, (cx, cy) in enumerate(_other_chips(mx, my))]

    def start(self, p_refs, o_refs, scr):
        for cp in self._copies(p_refs, o_refs, *scr):
            cp.start()

    def finish(self, p_refs, o_refs, scr):
        for cp in self._copies(p_refs, o_refs, *scr):
            cp.wait()


def _run_comm(comm, *, name):
    n = len(comm.arrays)

    def body(*refs):
        ins, outs, scr = refs[:n], refs[n:2 * n], refs[2 * n:]
        comm.start(ins, outs, scr)
        comm.finish(ins, outs, scr)

    return pl.pallas_call(body, name=name, in_specs=[_ANY] * n, out_specs=[_ANY] * n, out_shape=comm.out_shapes,
                          scratch_shapes=comm.scratch, compiler_params=pltpu.CompilerParams(has_side_effects=True))(*comm.arrays)


def _call(body, comm, *, name, grid, in_specs, out_specs, out_shape, scratch_shapes, sem, args):
    if comm is None:
        res = pl.pallas_call(body, name=name, grid=grid, in_specs=in_specs, out_specs=out_specs, out_shape=out_shape,
                             scratch_shapes=scratch_shapes, compiler_params=_cp(sem))(*args)
        return list(res), None
    n_in, n_out, n_scr, nc = len(in_specs), len(out_specs), len(scratch_shapes), len(comm.arrays)

    def wrapped(*refs):
        ins, cins = refs[:n_in], refs[n_in:n_in + nc]
        outs, couts = refs[n_in + nc:n_in + nc + n_out], refs[n_in + nc + n_out:n_in + 2 * nc + n_out]
        scr, cscr = refs[n_in + 2 * nc + n_out:n_in + 2 * nc + n_out + n_scr], refs[n_in + 2 * nc + n_out + n_scr:]
        ids = [pl.program_id(d) for d in range(len(grid))]
        first = functools.reduce(jnp.logical_and, [i == 0 for i in ids])
        last = functools.reduce(jnp.logical_and, [i == g - 1 for i, g in zip(ids, grid)])

        @pl.when(first)
        def _():
            comm.start(cins, couts, cscr)

        body(*ins, *outs, *scr)

        @pl.when(last)
        def _():
            comm.finish(cins, couts, cscr)

    res = pl.pallas_call(
        wrapped, name=name, grid=grid, in_specs=list(in_specs) + [_ANY] * nc, out_specs=list(out_specs) + [_ANY] * nc,
        out_shape=list(out_shape) + comm.out_shapes, scratch_shapes=list(scratch_shapes) + comm.scratch,
        compiler_params=_cp(("arbitrary",) * len(grid)))(*args, *comm.arrays)
    return list(res[:n_out]), list(res[n_out:])


class _PairSend:
    def __init__(self, gs):
        self.arrays = list(gs)
        n = len(gs)
        self.halves = [g.shape[1] // 2 for g in gs]
        self.out_shapes = [jax.ShapeDtypeStruct((4, g.shape[1] // 2, g.shape[2]), g.dtype) for g in gs]
        self.scratch = [pltpu.SemaphoreType.DMA((n, 4)), pltpu.SemaphoreType.DMA((n, 4))]

    def _copies(self, g_refs, o_refs, send, recv):
        mx, my, mc = _me()
        return [pltpu.make_async_remote_copy(g_refs[t].at[k, pl.ds((1 - mc) * hf, hf)], o_refs[t].at[k], send.at[t, k], recv.at[t, k],
                                             device_id=(mx, my, 1 - mc), device_id_type=MESH)
                for t, hf in enumerate(self.halves) for k in range(4)]

    def start(self, g_refs, o_refs, scr):
        for cp in self._copies(g_refs, o_refs, *scr):
            cp.start()

    def finish(self, g_refs, o_refs, scr):
        for cp in self._copies(g_refs, o_refs, *scr):
            cp.wait()


def _pair_exchange_halves(fs, *, name):
    n = len(fs)

    def body(*refs):
        f_refs, o_refs = refs[:n], refs[n:2 * n]
        send, recv = refs[2 * n:]
        mx, my, mc = _me()
        cps = []
        for t in range(n):
            hf = f_refs[t].shape[0]
            mine = pl.ds(mc * hf, hf)
            cp = pltpu.make_async_remote_copy(f_refs[t], o_refs[t].at[mine], send.at[t], recv.at[t],
                                              device_id=(mx, my, 1 - mc), device_id_type=MESH)
            cp.start()
            cps.append(cp)
        for t in range(n):
            hf = f_refs[t].shape[0]
            theirs = pl.ds((1 - mc) * hf, hf)
            cps[t].wait_send()
            pltpu.make_async_remote_copy(f_refs[t], o_refs[t].at[theirs], send.at[t], recv.at[t],
                                         device_id=(mx, my, 1 - mc), device_id_type=MESH).wait_recv()

    return pl.pallas_call(
        body, name=name, in_specs=[_ANY] * n, out_specs=[_ANY] * n,
        out_shape=[jax.ShapeDtypeStruct((2 * f.shape[0], f.shape[1]), f.dtype) for f in fs],
        scratch_shapes=[pltpu.SemaphoreType.DMA((n,)), pltpu.SemaphoreType.DMA((n,))],
        compiler_params=pltpu.CompilerParams(has_side_effects=True))(*fs)


def _allgather_all(x, *, name):
    def body(x_ref, o_ref, send, recv, lsem):
        mx, my, mc = _me()
        me = 4 * mx + 2 * my + mc
        local = pltpu.make_async_copy(x_ref, o_ref.at[me], lsem)
        local.start()
        cps = []
        for j in range(1, 8):
            px, py, pc = mx ^ (j >> 2), my ^ ((j >> 1) & 1), mc ^ (j & 1)
            cp = pltpu.make_async_remote_copy(x_ref, o_ref.at[me], send.at[j - 1], recv.at[j - 1],
                                              device_id=(px, py, pc), device_id_type=MESH)
            cp.start()
            cps.append(cp)
        for j in range(1, 8):
            px, py, pc = mx ^ (j >> 2), my ^ ((j >> 1) & 1), mc ^ (j & 1)
            pltpu.make_async_remote_copy(x_ref, o_ref.at[4 * px + 2 * py + pc], send.at[j - 1], recv.at[j - 1],
                                         device_id=(px, py, pc), device_id_type=MESH).wait_recv()
        for cp in cps:
            cp.wait_send()
        local.wait()

    return pl.pallas_call(
        body, name=name, in_specs=[_ANY], out_specs=_ANY, out_shape=jax.ShapeDtypeStruct((8,) + x.shape, x.dtype),
        scratch_shapes=[pltpu.SemaphoreType.DMA((7,)), pltpu.SemaphoreType.DMA((7,)), pltpu.SemaphoreType.DMA],
        compiler_params=pltpu.CompilerParams(has_side_effects=True))(x)


def _add_half(g4, recv, mc, *, name):
    _, R, C = g4.shape
    hf = R // 2
    tm = _tile(hf, max(16, (1 << 19) // C // 16 * 16), 16)
    nb = hf // tm

    def body(mc_ref, g_ref, r_ref, o_ref, ob_ref):
        s = g_ref[...] + r_ref[...]
        o_ref[...] = s
        ob_ref[...] = s.astype(COMM_DT)

    ospec = pl.BlockSpec((1, tm, C), lambda k, i, mc_ref: (k, i, 0))
    return pl.pallas_call(
        body, name=name,
        grid_spec=pltpu.PrefetchScalarGridSpec(
            num_scalar_prefetch=1, grid=(4, nb),
            in_specs=[pl.BlockSpec((1, tm, C), lambda k, i, mc_ref: (k, mc_ref[0] * nb + i, 0)),
                      pl.BlockSpec((1, tm, C), lambda k, i, mc_ref: (k, i, 0))],
            out_specs=[ospec, ospec]),
        out_shape=[jax.ShapeDtypeStruct((4, hf, C), f32), jax.ShapeDtypeStruct((4, hf, C), COMM_DT)],
        compiler_params=_cp(("parallel", "parallel")))(mc, g4, recv)


def _sum_chips(p4, recv3, me, *, name):
    _, Rh, C = p4.shape
    tm = _tile(Rh, max(16, (1 << 19) // C // 16 * 16), 16)

    def body(me_ref, p_ref, r_ref, o_ref):
        o_ref[...] = ((p_ref[0] + r_ref[0].astype(f32)) + r_ref[1].astype(f32)) + r_ref[2].astype(f32)

    return pl.pallas_call(
        body, name=name,
        grid_spec=pltpu.PrefetchScalarGridSpec(
            num_scalar_prefetch=1, grid=(Rh // tm,),
            in_specs=[pl.BlockSpec((1, tm, C), lambda i, me_ref: (me_ref[0], i, 0)),
                      pl.BlockSpec((3, tm, C), lambda i, me_ref: (0, i, 0))],
            out_specs=pl.BlockSpec((tm, C), lambda i, me_ref: (i, 0))),
        out_shape=jax.ShapeDtypeStruct((Rh, C), f32),
        compiler_params=_cp(("parallel",)))(me, p4, recv3)


def _sum8(x8, *, name):
    _, R, C = x8.shape
    tm = _tile(R, 64, 8)

    def body(x_ref, o_ref):
        acc = x_ref[0]
        for k in range(1, 8):
            acc = acc + x_ref[k]
        o_ref[...] = acc

    return pl.pallas_call(body, name=name, grid=(R // tm,), in_specs=[pl.BlockSpec((8, tm, C), lambda i: (0, i, 0))],
                          out_specs=pl.BlockSpec((tm, C), lambda i: (i, 0)), out_shape=jax.ShapeDtypeStruct((R, C), f32),
                          compiler_params=_cp(("parallel",)))(x8)


def _ssd_layer_fwd(h, W, tag, plan, i):
    z = _mm(h, W["wz"], name=tag + "_z")
    early = plan.fwd_early_comm(i)
    if early is None:
        xp = _mm(h, W["wxbc"], name=tag + "_xbc")
    else:
        xp, eres = _mm(h, W["wxbc"], name=tag + "_xbc", comm=early)
        plan.fwd_early_done(i, eres)
    dtr = _mm(h, W["wdt"], name=tag + "_dt")
    xbc = _conv_fwd(xp, W["conv_w"], W["conv_b"], name=tag + "_conv")
    (y, states), cres = _ssd_scan_fwd(xbc, dtr, W["dt_bias"], W["a_log"], W["d"], name=tag + "_scan", comm=plan.fwd_comm(i))
    DI = y.shape[1]
    G = SSD_N_GROUPS
    gs = DI // G
    (yn,) = _rowwise(_ssd_gate_fn, [(W["norm_w"], "c", gs)], [(y, "c", gs), (z, "c", gs)], [(DI, MXU_DT, "c", gs)],
                     name=tag + "_gate", ncol=G, tm=512)
    out = _mm(yn, W["wout"], name=tag + "_out")
    return out, dict(h=h, z=z, xp=xp, dtr=dtr, xbc=xbc, states=states, y=y, yn=yn), cres


def _carried_rowwise_bwd(plan, i, *a, **kw):
    early = plan.bwd_early_comm(i)
    if early is None:
        return _rowwise_bwd(*a, **kw)
    dp, di, cres = _rowwise_bwd(*a, comm=early, **kw)
    plan.bwd_early_done(i, cres)
    return dp, di


def _carried_mm(comm, done, i, *a, **kw):
    if comm is None:
        return _mm(*a, **kw)
    out, cres = _mm(*a, comm=comm, **kw)
    done(i, cres)
    return out


def _ssd_layer_bwd(sv, W, dr, drb, tag, plan, i):
    h = sv["h"]
    DI = sv["y"].shape[1]
    G = SSD_N_GROUPS
    gs = DI // G
    gr = {}
    dyn = _mm(drb, W["wout"], tb=True, name=tag + "_dyn")
    gr["wout"] = _mm(sv["yn"], drb, ta=True, name=tag + "_dwout")
    plan.early_grad(i, "ssd_out_w", gr["wout"])
    (dnw,), (dy, dz) = _carried_rowwise_bwd(plan, i, _ssd_gate_fn, [(W["norm_w"], "c", gs)], [(sv["y"], "c", gs), (sv["z"], "c", gs)],
                                             [(dyn, "c", gs)], name=tag + "_dgate", ncol=G, tm=512, din_dtypes=[(f32,), (MXU_DT,)])
    gr["norm_w"] = dnw
    (dxbc, ddtr, gr["dt_bias"], gr["a_log"], gr["d"]), cres = _ssd_scan_bwd(
        sv["xbc"], sv["dtr"], W["dt_bias"], W["a_log"], W["d"], sv["states"], dy, name=tag + "_dscan", comm=plan.bwd_comm(i))
    plan.bwd_done(i, cres)
    dxp, gr["conv_w"], gr["conv_b"] = _conv_bwd(
        sv["xp"], W["conv_w"], W["conv_b"], [dxbc], name=tag + "_dconv", dx_dtype=MXU_DT,
        dy_block=lambda c, cw: _ssd_grouped_block(c, G, gs, SSD_D_STATE, cw))
    gr["wz"] = _mm(h, dz, ta=True, name=tag + "_dwz")
    gr["wxbc"] = _mm(h, dxp, ta=True, name=tag + "_dwxbc")
    gr["wdt"] = _mm(h, ddtr, ta=True, name=tag + "_dwdt")
    dh = _carried_mm(plan.tail_early_comm(i, gr), plan.tail_early_done, i,
                     dz, W["wz"], tb=True, add=dr, add_scale=_alpha(), name=tag + "_dh1")
    dh = _carried_mm(plan.tail_comm(i), plan.tail_done, i, dxp, W["wxbc"], tb=True, add=dh, name=tag + "_dh2")
    dh = _mm(ddtr, W["wdt"], tb=True, add=dh, name=tag + "_dh3")
    return dh, gr


def _mla_layer_fwd(h, W, cos, sin, tag, comm=None):
    QR, KR = W["wqc"].shape[1], W["wkvc"].shape[1]
    HW = W["wqn"].shape[1]
    H = HW // LANES
    qc = _mm(h, W["wqc"], name=tag + "_qc")
    kvc = _mm(h, W["wkvc"], name=tag + "_kvc")
    krp = _mm(h, W["wkr"], name=tag + "_krp")
    z = _mm(h, W["wz"], name=tag + "_z")
    (qcn,) = _rowwise(_rms_fn, [(W["q_norm"], "a", QR)], [(qc, "a", QR)], [(QR, MXU_DT, "a", QR)], name=tag + "_qnorm")
    (kvn,) = _rowwise(_rms_fn, [(W["kv_norm"], "a", KR)], [(kvc, "a", KR)], [(KR, MXU_DT, "a", KR)], name=tag + "_kvnorm")
    qn = _mm(qcn, W["wqn"], name=tag + "_qn", out_dtype=MXU_DT)
    qrp = _mm(qcn, W["wqr"], name=tag + "_qrp")
    kn = _mm(kvn, W["wkn"], name=tag + "_kn", out_dtype=MXU_DT)
    v = _mm(kvn, W["wv"], name=tag + "_v", out_dtype=MXU_DT)
    (qr,) = _rowwise(_rope_fn, [], [(cos, "a", LANES), (sin, "a", LANES), (qrp, "c", LANES)], [(HW, MXU_DT, "c", LANES)],
                     name=tag + "_qrope", ncol=H, tm=1024)
    (kr,) = _rowwise(_rope_fn, [], [(cos, "a", LANES), (sin, "a", LANES), (krp, "a", LANES)], [(LANES, MXU_DT, "a", LANES)],
                     name=tag + "_krope")
    (o, lse), cres = _attn_fwd(qn, qr, kn, kr, v, name=tag + "_attn", comm=comm)
    (og,) = _rowwise(_mul_silu_fn, [], [(o, "a", HW), (z, "a", HW)], [(HW, MXU_DT, "a", HW)], name=tag + "_ogate")
    out = _mm(og, W["wout"], name=tag + "_out")
    return out, dict(h=h, qc=qc, kvc=kvc, z=z, qcn=qcn, kvn=kvn, qn=qn, qr=qr, kn=kn, kr=kr, v=v, o=o, lse=lse, og=og), cres


def _mla_layer_bwd(sv, W, cos, sin, dr, drb, tag, plan, i):
    h = sv["h"]
    QR, KR = W["wqc"].shape[1], W["wkvc"].shape[1]
    HW = W["wqn"].shape[1]
    H = HW // LANES
    gr = {}
    dog = _mm(drb, W["wout"], tb=True, name=tag + "_dog")
    gr["wout"] = _mm(sv["og"], drb, ta=True, name=tag + "_dwout")
    _, (do, dz) = _rowwise_bwd(_mul_silu_fn, [], [(sv["o"], "a", HW), (sv["z"], "a", HW)], [(dog, "a", HW)], name=tag + "_dogate",
                               din_dtypes=[(f32,), (MXU_DT,)])
    att = (sv["qn"], sv["qr"], sv["kn"], sv["kr"], sv["v"], sv["o"], sv["lse"], do)
    (dqn, dqr), cres = _attn_bwd_dq(*att, name=tag + "_dq", comm=plan.bwd_early_comm(i))
    plan.bwd_early_done(i, cres)
    (dkn, dkr, dv), cres = _attn_bwd_dkv(*att, name=tag + "_dkv", comm=plan.bwd_comm(i))
    plan.bwd_done(i, cres)
    _, (dqrp,) = _rowwise_bwd(_rope_fn, [], [(cos, "a", LANES), (sin, "a", LANES), (dqr, "c", LANES)], [(dqr, "c", LANES)],
                              name=tag + "_dqrope", ncol=H, tm=1024, diff_i=[2], din_dtypes=[(MXU_DT,)])
    _, (dkrp,) = _rowwise_bwd(_rope_fn, [], [(cos, "a", LANES), (sin, "a", LANES), (dkr, "a", LANES)], [(dkr, "a", LANES)],
                              name=tag + "_dkrope", diff_i=[2], din_dtypes=[(MXU_DT,)])
    dqcn = _mm(dqn, W["wqn"], tb=True, name=tag + "_dqcn1")
    dqcn = _mm(dqrp, W["wqr"], tb=True, add=dqcn, name=tag + "_dqcn2")
    dkvn = _mm(dkn, W["wkn"], tb=True, name=tag + "_dkvn1")
    dkvn = _mm(dv, W["wv"], tb=True, add=dkvn, name=tag + "_dkvn2")
    gr["wqn"] = _mm(sv["qcn"], dqn, ta=True, name=tag + "_dwqn")
    gr["wqr"] = _mm(sv["qcn"], dqrp, ta=True, name=tag + "_dwqr")
    gr["wkn"] = _mm(sv["kvn"], dkn, ta=True, name=tag + "_dwkn")
    gr["wv"] = _mm(sv["kvn"], dv, ta=True, name=tag + "_dwv")
    (gr["q_norm"],), (dqc,) = _rowwise_bwd(_rms_fn, [(W["q_norm"], "a", QR)], [(sv["qc"], "a", QR)], [(dqcn, "a", QR)], name=tag + "_dqnorm",
                                           din_dtypes=[(MXU_DT,)])
    (gr["kv_norm"],), (dkvc,) = _rowwise_bwd(_rms_fn, [(W["kv_norm"], "a", KR)], [(sv["kvc"], "a", KR)], [(dkvn, "a", KR)], name=tag + "_dkvnorm",
                                             din_dtypes=[(MXU_DT,)])
    dh = _mm(dz, W["wz"], tb=True, add=dr, add_scale=_alpha(), name=tag + "_dh1")
    dh = _mm(dqc, W["wqc"], tb=True, add=dh, name=tag + "_dh2")
    dh = _mm(dkvc, W["wkvc"], tb=True, add=dh, name=tag + "_dh3")
    dh = _mm(dkrp, W["wkr"], tb=True, add=dh, name=tag + "_dh4")
    gr["wz"] = _mm(h, dz, ta=True, name=tag + "_dwz")
    gr["wqc"] = _mm(h, dqc, ta=True, name=tag + "_dwqc")
    gr["wkvc"] = _mm(h, dkvc, ta=True, name=tag + "_dwkvc")
    gr["wkr"] = _mm(h, dkrp, ta=True, name=tag + "_dwkr")
    return dh, gr


def _gdn_layer_fwd(h, W, tag, comm=None):
    HK, HV, DK, DV = GDN_N_QK_HEADS, GDN_N_V_HEADS, GDN_DK, GDN_DV
    KD, VD = HK * DK, HV * DV
    qkvp = _mm(h, W["wqkv"], name=tag + "_qkv")
    z = _mm(h, W["wz"], name=tag + "_z")
    br = _mm(h, W["wb"], name=tag + "_b")
    ar = _mm(h, W["wa"], name=tag + "_a")
    qkv = _conv_fwd(qkvp, W["conv_w"], jnp.zeros((1, qkvp.shape[1]), f32), name=tag + "_conv")
    scale = jnp.concatenate([jnp.full((1, KD), DK ** -0.5, f32), jnp.ones((1, KD), f32)], axis=1)
    (qkn,) = _rowwise(_l2_fn, [(scale, "c", DK)], [(qkv, "c", DK)], [(2 * KD, f32, "c", DK)], name=tag + "_l2", ncol=2 * HK, tm=2048)
    (o, states, inverses), cres = _gdn_scan_fwd(qkn, qkv, br, ar, W["a_log"], W["dt_bias"], name=tag + "_scan", comm=comm)
    (on,) = _rowwise(_gdn_gate_fn, [(W["norm_w"], "a", DV)], [(o, "c", DV), (z, "c", DV)], [(VD, MXU_DT, "c", DV)],
                     name=tag + "_gate", ncol=HV, tm=1024)
    out = _mm(on, W["wout"], name=tag + "_out")
    return out, dict(h=h, qkvp=qkvp, z=z, br=br, ar=ar, qkv=qkv, qkn=qkn, o=o, states=states, inverses=inverses, on=on, scale=scale), cres


def _gdn_layer_bwd(sv, W, dr, drb, tag, plan, i):
    h = sv["h"]
    HK, HV, DK, DV = GDN_N_QK_HEADS, GDN_N_V_HEADS, GDN_DK, GDN_DV
    KD, VD = HK * DK, HV * DV
    gr = {}
    don = _mm(drb, W["wout"], tb=True, name=tag + "_don")
    gr["wout"] = _mm(sv["on"], drb, ta=True, name=tag + "_dwout")
    (gr["norm_w"],), (do, dz) = _carried_rowwise_bwd(plan, i, _gdn_gate_fn, [(W["norm_w"], "a", DV)], [(sv["o"], "c", DV), (sv["z"], "c", DV)],
                                                      [(don, "c", DV)], name=tag + "_dgate", ncol=HV, tm=1024, din_dtypes=[(f32,), (MXU_DT,)])
    (dq, dk, dv, dbr, dar, gr["a_log"], gr["dt_bias"]), cres = _gdn_scan_bwd(
        sv["qkn"], sv["qkv"], sv["br"], sv["ar"], W["a_log"], W["dt_bias"], sv["states"], sv["inverses"], do, name=tag + "_dscan",
        comm=plan.bwd_comm(i))
    plan.bwd_done(i, cres)
    _, (dqq,) = _rowwise_bwd(_l2_fn, [(sv["scale"], "c", DK)], [(sv["qkv"], "c", DK)], [(dq, "c", DK)],
                             name=tag + "_dl2q", ncol=HK, tm=2048, diff_p=[])
    _, (dqk,) = _rowwise_bwd(_l2_fn, [(sv["scale"], "c", DK, HK)], [(sv["qkv"], "c", DK, HK)], [(dk, "c", DK)],
                             name=tag + "_dl2k", ncol=HK, tm=2048, diff_p=[])
    dqkvp, gr["conv_w"], _ = _conv_bwd(sv["qkvp"], W["conv_w"], jnp.zeros((1, sv["qkvp"].shape[1]), f32), [dqq, dqk, dv],
                                       name=tag + "_dconv", dx_dtype=MXU_DT)
    dh = _mm(dz, W["wz"], tb=True, add=dr, add_scale=_alpha(), name=tag + "_dh1")
    dh = _mm(dqkvp, W["wqkv"], tb=True, add=dh, name=tag + "_dh2")
    dh = _mm(dbr, W["wb"], tb=True, add=dh, name=tag + "_dh3")
    dh = _mm(dar, W["wa"], tb=True, add=dh, name=tag + "_dh4")
    gr["wz"] = _mm(h, dz, ta=True, name=tag + "_dwz")
    gr["wqkv"] = _mm(h, dqkvp, ta=True, name=tag + "_dwqkv")
    gr["wb"] = _mm(h, dbr, ta=True, name=tag + "_dwb")
    gr["wa"] = _mm(h, dar, ta=True, name=tag + "_dwa")
    return dh, gr


def _rope_tables(positions):
    half = MLA_ROPE // 2
    inv_freq = ROPE_THETA ** (-jnp.arange(0, MLA_ROPE, 2, dtype=f32) / MLA_ROPE)
    ang = positions.astype(f32)[:, None] * inv_freq
    cos, sin = jnp.cos(ang), jnp.sin(ang)
    S = positions.shape[0]
    pad = jnp.zeros((S, LANES - 2 * half), f32)
    return jnp.concatenate([cos, cos, pad + 1.0], axis=1), jnp.concatenate([sin, sin, pad], axis=1)


class _LocalPlan:
    def __init__(self, LW):
        self.LW, self.grads = LW, [None] * DEPTH

    def weights(self, i):
        return self.LW[i]

    def fwd_early_comm(self, i):
        return None

    def fwd_early_done(self, i, res):
        pass

    def fwd_comm(self, i):
        return None

    def fwd_done(self, i, res):
        pass

    def early_grad(self, i, name, g):
        pass

    def tail_early_comm(self, i, gr):
        return None

    def tail_early_done(self, i, res):
        pass

    def tail_comm(self, i):
        return None

    def tail_done(self, i, res):
        pass

    def bwd_early_comm(self, i):
        return None

    def bwd_early_done(self, i, res):
        pass

    def bwd_comm(self, i):
        return None

    def bwd_done(self, i, res):
        pass

    def layer_grads(self, i, gr):
        self.grads[i] = gr


def _local_step(x, positions, target, ln_g, ln_b, plan):
    cos, sin = _rope_tables(positions)
    h, hb = x, x.astype(MXU_DT)
    saved, LW = [], []
    for i in range(DEPTH):
        kind, tag = i % 3, "l%d" % i
        LW.append(plan.weights(i))
        if kind == 0:
            y, sv, cres = _ssd_layer_fwd(hb, LW[i], tag, plan, i)
        elif kind == 1:
            y, sv, cres = _mla_layer_fwd(hb, LW[i], cos, sin, tag, plan.fwd_comm(i))
        else:
            y, sv, cres = _gdn_layer_fwd(hb, LW[i], tag, plan.fwd_comm(i))
        plan.fwd_done(i, cres)
        D = h.shape[1]
        r, h, hb = _rowwise(_res_ln_fn, [(ln_g[i], "a", D), (ln_b[i], "a", D)], [(h, "a", D), (y, "a", D)],
                            [(D, f32, "a", D), (D, f32, "a", D), (D, MXU_DT, "a", D)], name=tag + "_ln")
        sv["r"] = r
        saved.append(sv)
    loss, dh = _loss_head(h, target, name="loss_head")
    dg, db = [None] * DEPTH, [None] * DEPTH
    for i in reversed(range(DEPTH)):
        kind, tag = i % 3, "l%d" % i
        sv = saved[i]
        D = dh.shape[1]
        (dg[i], db[i]), (dr, drb) = _rowwise_bwd(_ln_fn, [(ln_g[i], "a", D), (ln_b[i], "a", D)], [(sv["r"], "a", D)], [(dh, "a", D)],
                                                 name=tag + "_dln", din_dtypes=[(f32, MXU_DT)])
        if kind == 0:
            dh, gr = _ssd_layer_bwd(sv, LW[i], dr, drb, tag, plan, i)
        elif kind == 1:
            dh, gr = _mla_layer_bwd(sv, LW[i], cos, sin, dr, drb, tag, plan, i)
        else:
            dh, gr = _gdn_layer_bwd(sv, LW[i], dr, drb, tag, plan, i)
        plan.layer_grads(i, gr)
    return loss, dh, dg, db


_WEIGHTS = ["ssd_in_w", "ssd_conv_w", "ssd_conv_b", "ssd_dt_bias", "ssd_a_log", "ssd_d", "ssd_norm_w", "ssd_out_w",
            "mla_in_w", "mla_q_norm_w", "mla_q_up_w", "mla_kv_norm_w", "mla_kv_up_w", "mla_out_w",
            "gdn_in_w", "gdn_conv_w", "gdn_a_log", "gdn_dt_bias", "gdn_norm_w", "gdn_out_w", "ln_g", "ln_b"]
_BIG = {"ssd_in_w": "col", "ssd_out_w": "row", "mla_in_w": "col", "mla_q_up_w": "col", "mla_kv_up_w": "col",
        "mla_out_w": "row", "gdn_in_w": "col", "gdn_out_w": "row"}
_SMALL_SHARDED = ["ssd_conv_w", "ssd_conv_b", "ssd_norm_w", "gdn_conv_w"]
_PACK_ROWS = 16


def _gathered_to_full(g, kind, nl):
    if kind == "col":
        _, RK, Ns = g.shape
        return g.reshape(4, nl, RK // nl, Ns).transpose(1, 2, 0, 3).reshape(nl, RK // nl, 4 * Ns)
    _, RK, N = g.shape
    return g.reshape(4, nl, RK // nl, N).transpose(1, 0, 2, 3).reshape(nl, 4 * (RK // nl), N)


def _full_to_slots(f, kind):
    nl, K, N = f.shape
    if kind == "col":
        return f.reshape(nl, K, 4, N // 4).transpose(2, 0, 1, 3).reshape(4, nl * K, N // 4)
    return f.reshape(nl, 4, K // 4, N).transpose(1, 0, 2, 3).reshape(4, nl * (K // 4), N)


def _pack(arrs):
    flat = jnp.concatenate([a.reshape(-1).astype(f32) for a in arrs])
    unit = _PACK_ROWS * LANES
    n = -(-flat.shape[0] // unit) * unit
    return jnp.pad(flat, (0, n - flat.shape[0])).reshape(_PACK_ROWS, n // _PACK_ROWS)


def _unpack(packed, shapes):
    flat = packed.reshape(-1)
    out, off = [], 0
    for sh in shapes:
        n = math.prod(sh)
        out.append(flat[off:off + n].reshape(sh))
        off += n
    return out


def _pad_lanes(a):
    return jnp.pad(a, [(0, 0)] * (a.ndim - 1) + [(0, LANES - a.shape[-1])])


_IN_PROJ = ("ssd_in_w", "mla_in_w", "gdn_in_w")


class _ColSlots:
    def __init__(self, slots):
        self.slots = slots
        self.shape = (slots.shape[1], 4 * slots.shape[2])

    def __getitem__(self, idx):
        _, cols = idx
        ns = self.slots.shape[2]
        a = cols.start or 0
        b = self.shape[1] if cols.stop is None else cols.stop
        parts = [self.slots[k][:, max(a, k * ns) - k * ns:min(b, (k + 1) * ns) - k * ns]
                 for k in range(4) if max(a, k * ns) < min(b, (k + 1) * ns)]
        return parts[0] if len(parts) == 1 else jnp.concatenate(parts, axis=1)


def _col_slots(pieces):
    widths = [p.shape[1] for p in pieces]
    ns = sum(widths) // 4
    slots = []
    for k in range(4):
        lo, hi, off, parts = k * ns, (k + 1) * ns, 0, []
        for p, wd in zip(pieces, widths):
            if max(lo, off) < min(hi, off + wd):
                parts.append(p[:, max(lo, off) - off:min(hi, off + wd) - off])
            off += wd
        slots.append(parts[0] if len(parts) == 1 else jnp.concatenate(parts, axis=1))
    return jnp.stack(slots)


def _layer_dict(i, full):
    G, N, P = SSD_N_GROUPS, SSD_D_STATE, SSD_HEAD_DIM
    kind, j = i % 3, i // 3
    if kind == 0:
        H = full["ssd_dt_bias"][j].shape[0]
        DI = H * P
        CD = DI + 2 * G * N
        win = full["ssd_in_w"][j]
        return dict(wz=win[:, :DI], wxbc=win[:, DI:DI + CD], wdt=win[:, DI + CD:], conv_w=full["ssd_conv_w"][j],
                    conv_b=full["ssd_conv_b"][j][None], dt_bias=full["ssd_dt_bias"][j][None], a_log=full["ssd_a_log"][j][None],
                    d=full["ssd_d"][j][None], norm_w=full["ssd_norm_w"][j][None], wout=full["ssd_out_w"][j])
    if kind == 1:
        QR, KR = MLA_Q_RANK, MLA_KV_RANK
        win = full["mla_in_w"][j]
        Hh = full["mla_q_up_w"][j].shape[1] // (MLA_NOPE + MLA_ROPE)
        qup = full["mla_q_up_w"][j].reshape(QR, Hh, MLA_NOPE + MLA_ROPE)
        kvup = full["mla_kv_up_w"][j].reshape(KR, Hh, MLA_NOPE + MLA_V)
        return dict(wqc=win[:, :QR], wkvc=win[:, QR:QR + KR], wkr=_pad_lanes(win[:, QR + KR:QR + KR + MLA_ROPE]),
                    wz=win[:, QR + KR + MLA_ROPE:], q_norm=full["mla_q_norm_w"][j][None], kv_norm=full["mla_kv_norm_w"][j][None],
                    wqn=qup[:, :, :MLA_NOPE].reshape(QR, Hh * MLA_NOPE), wqr=_pad_lanes(qup[:, :, MLA_NOPE:]).reshape(QR, Hh * LANES),
                    wkn=kvup[:, :, :MLA_NOPE].reshape(KR, Hh * MLA_NOPE), wv=kvup[:, :, MLA_NOPE:].reshape(KR, Hh * MLA_V),
                    wout=full["mla_out_w"][j])
    KD, VD, HV = GDN_N_QK_HEADS * GDN_DK, GDN_N_V_HEADS * GDN_DV, GDN_N_V_HEADS
    win = full["gdn_in_w"][j]
    c0, c1 = 2 * KD + VD, 2 * KD + 2 * VD
    return dict(wqkv=win[:, :c0], wz=win[:, c0:c1], wb=win[:, c1:c1 + HV], wa=win[:, c1 + HV:], conv_w=full["gdn_conv_w"][j],
                a_log=full["gdn_a_log"][j][None], dt_bias=full["gdn_dt_bias"][j][None], norm_w=full["gdn_norm_w"][j][None],
                wout=full["gdn_out_w"][j])


def _layer_weights(full, D):
    return [_layer_dict(i, full) for i in range(DEPTH)]


def _layer_full_grads(i, g, slots=False):
    kind = i % 3
    join = _col_slots if slots else (lambda pieces: jnp.concatenate(pieces, axis=1))
    if kind == 0:
        out = {"ssd_in_w": join([g["wz"], g["wxbc"], g["wdt"]]), "ssd_conv_w": g["conv_w"], "ssd_out_w": g["wout"]}
        for n in ("conv_b", "dt_bias", "a_log", "d", "norm_w"):
            out["ssd_" + n] = g[n][0]
        return out
    if kind == 1:
        QR, KR = g["wqn"].shape[0], g["wkn"].shape[0]
        Hh = g["wqn"].shape[1] // MLA_NOPE
        return {"mla_in_w": join([g["wqc"], g["wkvc"], g["wkr"][:, :MLA_ROPE], g["wz"]]),
                "mla_q_up_w": jnp.concatenate([g["wqn"].reshape(QR, Hh, MLA_NOPE), g["wqr"].reshape(QR, Hh, LANES)[:, :, :MLA_ROPE]],
                                              axis=2).reshape(QR, -1),
                "mla_kv_up_w": jnp.concatenate([g["wkn"].reshape(KR, Hh, MLA_NOPE), g["wv"].reshape(KR, Hh, MLA_V)], axis=2).reshape(KR, -1),
                "mla_q_norm_w": g["q_norm"][0], "mla_kv_norm_w": g["kv_norm"][0], "mla_out_w": g["wout"]}
    out = {"gdn_in_w": join([g["wqkv"], g["wz"], g["wb"], g["wa"]]), "gdn_conv_w": g["conv_w"], "gdn_out_w": g["wout"]}
    for n in ("a_log", "dt_bias", "norm_w"):
        out["gdn_" + n] = g[n][0]
    return out


def _full_grads(grads, dg, db):
    per = {n: [] for n in _WEIGHTS}
    for i in range(DEPTH):
        for n, a in _layer_full_grads(i, grads[i]).items():
            per[n].append(a)
        per["ln_g"].append(dg[i][0])
        per["ln_b"].append(db[i][0])
    return {n: jnp.stack(v) for n, v in per.items()}


class _DistPlan:
    def __init__(self, w, chip, core):
        self.w, self.chip = w, chip
        self.chip_arr = jnp.reshape(chip, (1,)).astype(jnp.int32)
        self.core_arr = jnp.reshape(core, (1,)).astype(jnp.int32)
        self.full = {n: {} for n in _BIG}
        self.gkeys, self.g4, self.p4, self.fin, self.small_grads = {}, {}, {}, {}, [None] * DEPTH
        keys = [("ssd_in_w", 0)]
        shards = self._shards(keys)
        got = _run_comm(_GatherChips(shards + [_pack([w[n] for n in _SMALL_SHARDED])]), name="gather_l0")
        self._fill(keys, shards, got[:1])
        small = lax.dynamic_update_slice(got[1], _pack([w[n] for n in _SMALL_SHARDED])[None], (chip, 0, 0))
        parts = [_unpack(small[k], [w[n].shape for n in _SMALL_SHARDED]) for k in range(4)]
        for t, n in enumerate(_SMALL_SHARDED):
            self.full[n] = jnp.concatenate([parts[k][t] for k in range(4)], axis=-1)
        for n in _WEIGHTS:
            if n not in self.full:
                self.full[n] = w[n]

    @staticmethod
    def keys(i):
        names = [["ssd_in_w", "ssd_out_w"], ["mla_in_w", "mla_q_up_w", "mla_kv_up_w", "mla_out_w"], ["gdn_in_w", "gdn_out_w"]][i % 3]
        return [(n, i // 3) for n in names]

    def _shards(self, keys):
        return [self.w[n][j].astype(MXU_DT) for n, j in keys]

    def _fill(self, keys, shards, got):
        for (n, j), s, g in zip(keys, shards, got):
            g = lax.dynamic_update_slice(g, s[None], (self.chip, 0, 0))
            self.full[n][j] = _ColSlots(g) if n in _IN_PROJ else _gathered_to_full(g, _BIG[n], 1)[0]

    def weights(self, i):
        if i == 0:
            self.full["ssd_out_w"][0] = None
        self._w = _layer_dict(i, self.full)
        return self._w

    def _start_gather(self, keys):
        self._pending = (keys, self._shards(keys))
        return _GatherChips(self._pending[1])

    def _end_gather(self, res):
        self._fill(self._pending[0], self._pending[1], res)

    def fwd_early_comm(self, i):
        return self._start_gather([("ssd_out_w", 0)]) if i == 0 else None

    def fwd_early_done(self, i, res):
        self._end_gather(res)
        self._w["wout"] = self.full["ssd_out_w"][0]

    def fwd_comm(self, i):
        return self._start_gather(self.keys(i + 1)) if i + 1 < DEPTH else None

    def fwd_done(self, i, res):
        if res is not None:
            self._end_gather(res)

    def _slots(self, n, g):
        return g if n in _IN_PROJ else _full_to_slots(g[None], _BIG[n])

    def early_grad(self, i, name, g):
        if i == 0:
            self.gkeys[1].append((name, 0))
            self.g4[1].append(self._slots(name, g))

    def _make_group(self, i, gr):
        fg = _layer_full_grads(i, gr, slots=True)
        self.small_grads[i] = {n: a for n, a in fg.items() if n not in _BIG}
        self.gkeys[i] = [k for k in self.keys(i) if not (i == 0 and k[0] == "ssd_out_w")]
        self.g4[i] = [self._slots(n, fg[n]) for n, _ in self.gkeys[i]]

    def layer_grads(self, i, gr):
        if i > 0:
            self._make_group(i, gr)

    def tail_early_comm(self, i, gr):
        if i > 0:
            return None
        self._make_group(0, gr)
        return _PairSend(self.g4[0])

    def tail_early_done(self, i, res):
        self._pair_add(0, res)

    def tail_comm(self, i):
        return self._scatter(0) if i == 0 else None

    def tail_done(self, i, res):
        self._sum(0, res)

    def _pair_add(self, g, r1):
        self.p4[g] = [_add_half(a, b, self.core_arr, name="grad_pair_add_g%d_%s%d" % (g, n, j))
                      for a, b, (n, j) in zip(self.g4[g], r1, self.gkeys[g])]

    def bwd_early_comm(self, i):
        return _PairSend(self.g4[i + 1]) if i + 1 < DEPTH else None

    def bwd_early_done(self, i, res):
        if res is not None:
            self._pair_add(i + 1, res)

    def _scatter(self, g):
        return _ScatterChips([p[1] for p in self.p4[g]])

    def _sum(self, g, r2):
        self.fin[g] = [_sum_chips(p[0], b, self.chip_arr, name="grad_chip_sum_g%d_%s%d" % (g, n, j))
                       for p, b, (n, j) in zip(self.p4[g], r2, self.gkeys[g])]

    def bwd_comm(self, i):
        return self._scatter(i + 1) if i + 1 < DEPTH else None

    def bwd_done(self, i, res):
        if res is not None:
            self._sum(i + 1, res)

    def grad_shards(self, core):
        order = [(g, t) for g in range(DEPTH) for t in range(len(self.gkeys[g]))]
        fins = [self.fin[g][t] for g, t in order]
        got = _pair_exchange_halves(fins, name="grad_pair_share")
        got = [lax.dynamic_update_slice(a, f, (core * f.shape[0], 0)) for a, f in zip(got, fins)]
        per = {n: {} for n in _BIG}
        for (g, t), a in zip(order, got):
            n, j = self.gkeys[g][t]
            per[n][j] = a
        return {n: (v[0] if len(v) == 1 else jnp.concatenate([v[j] for j in sorted(v)], axis=0)) for n, v in per.items()}


def kernel(x, positions, ssd_in_w, ssd_conv_w, ssd_conv_b, ssd_dt_bias, ssd_a_log, ssd_d, ssd_norm_w, ssd_out_w, mla_in_w, mla_q_norm_w, mla_q_up_w, mla_kv_norm_w, mla_kv_up_w, mla_out_w, gdn_in_w, gdn_conv_w, gdn_a_log, gdn_dt_bias, gdn_norm_w, gdn_out_w, ln_g, ln_b, loss_target, m_ssd_in_w, m_ssd_conv_w, m_ssd_conv_b, m_ssd_dt_bias, m_ssd_a_log, m_ssd_d, m_ssd_norm_w, m_ssd_out_w, m_mla_in_w, m_mla_q_norm_w, m_mla_q_up_w, m_mla_kv_norm_w, m_mla_kv_up_w, m_mla_out_w, m_gdn_in_w, m_gdn_conv_w, m_gdn_a_log, m_gdn_dt_bias, m_gdn_norm_w, m_gdn_out_w, m_ln_g, m_ln_b, v_ssd_in_w, v_ssd_conv_w, v_ssd_conv_b, v_ssd_dt_bias, v_ssd_a_log, v_ssd_d, v_ssd_norm_w, v_ssd_out_w, v_mla_in_w, v_mla_q_norm_w, v_mla_q_up_w, v_mla_kv_norm_w, v_mla_kv_up_w, v_mla_out_w, v_gdn_in_w, v_gdn_conv_w, v_gdn_a_log, v_gdn_dt_bias, v_gdn_norm_w, v_gdn_out_w, v_ln_g, v_ln_b):
    args = dict(locals())
    w = {n: args[n] for n in _WEIGHTS}
    mom = {n: args["m_" + n] for n in _WEIGHTS}
    vel = {n: args["v_" + n] for n in _WEIGHTS}
    mx, my, mc = _me()
    chip = 2 * mx + my
    small = [n for n in _WEIGHTS if n not in _BIG]
    big = list(_BIG)

    plan = _DistPlan(w, chip, mc)
    loss, gx, dg, db = _local_step(x[0], positions[0], loss_target[0], [plan.full["ln_g"][i][None] for i in range(DEPTH)],
                                   [plan.full["ln_b"][i][None] for i in range(DEPTH)], plan)
    per = {n: [] for n in small}
    for i in range(DEPTH):
        for n, a in plan.small_grads[i].items():
            per[n].append(a)
        per["ln_g"].append(dg[i][0])
        per["ln_b"].append(db[i][0])
    fg = {n: jnp.stack(v) for n, v in per.items()}
    gsh = plan.grad_shards(mc)

    out_g, out_d, out_m, out_v = {}, {}, {}, {}
    for n in big:
        g = gsh[n]
        sh = w[n].shape
        to2 = lambda a: a.reshape(-1, sh[-1])
        d_, m_, v_ = _adamw(to2(w[n]), g, to2(mom[n]), to2(vel[n]), name="adamw_" + n)
        out_g[n], out_d[n], out_m[n], out_v[n] = g.reshape(sh), d_.reshape(sh), m_.reshape(sh), v_.reshape(sh)

    summed = _sum8(_allgather_all(_pack([fg[n] for n in small] + [loss[0, :1]]), name="gather_small"), name="sum_small")
    sg = _unpack(summed, [fg[n].shape for n in small] + [(1,)])
    loss_total = sg[-1][0]
    gs = {}
    for n, g in zip(small, sg[:-1]):
        if n in _SMALL_SHARDED:
            ws = w[n].shape[-1]
            g = lax.dynamic_slice_in_dim(g, chip * ws, ws, axis=g.ndim - 1)
        gs[n] = g
    shapes = [w[n].shape for n in small]
    d_, m_, v_ = _adamw(_pack([w[n] for n in small]), _pack([gs[n] for n in small]), _pack([mom[n] for n in small]),
                        _pack([vel[n] for n in small]), name="adamw_small")
    for n, a, b, c in zip(small, _unpack(d_, shapes), _unpack(m_, shapes), _unpack(v_, shapes)):
        out_g[n], out_d[n], out_m[n], out_v[n] = gs[n], a, b, c

    return (loss_total, gx[None], *[out_g[n] for n in _WEIGHTS], *[out_d[n] for n in _WEIGHTS],
            *[out_m[n] for n in _WEIGHTS], *[out_v[n] for n in _WEIGHTS])
```

```python
import functools
import math

import jax
import jax.numpy as jnp
from jax import lax
from jax.experimental import pallas as pl
from jax.experimental.pallas import tpu as pltpu

f32 = jnp.float32
HI = lax.Precision.HIGHEST
MXU_DT = jnp.bfloat16
COMM_DT = jnp.bfloat16
MESH = pl.DeviceIdType.MESH

DEPTH = 4
LN_EPS = 1e-5
RMS_EPS = 1e-6
SSD_HEAD_DIM = 64
SSD_N_GROUPS = 8
SSD_D_STATE = 128
SSD_CONV = 4
SSD_CHUNK = 128
MLA_Q_RANK = 768
MLA_KV_RANK = 512
MLA_NOPE = 128
MLA_ROPE = 64
MLA_V = 128
ROPE_THETA = 10000.0
GDN_N_QK_HEADS = 16
GDN_N_V_HEADS = 32
GDN_DK = 128
GDN_DV = 128
GDN_CONV = 4
GDN_CHUNK = 64
ADAM_LR = 0.001
ADAM_B1 = 0.9
ADAM_B2 = 0.999
ADAM_EPS = 1e-08
ADAM_WD = 0.01
ADAM_STEP = 10

LANES = 128
VMEM_LIMIT = 48 * 1024 * 1024
ATT_TILE = 512
ATT_Q_TILE = 512
ATT_KEY_TILE = 512
ROW_TILE = 256
MM_TILE_M = 1024
MM_TILE_N = 1024
MM_TILE_K = 2048
MM_VMEM_BUDGET = 40 * 1024 * 1024
SSD_GROUPS_PER_STEP = 8
GDN_HEADS_PER_STEP = 16
GDN_FWD_HEADS_PER_STEP = 32


def _alpha():
    return (2.0 * DEPTH) ** 0.25


def _tile(n, pref, align=LANES):
    t = min(pref, n) // align * align
    while t >= align:
        if n % t == 0:
            return t
        t -= align
    return n


def _cp(sem=None):
    return pltpu.CompilerParams(dimension_semantics=sem, vmem_limit_bytes=VMEM_LIMIT)


def _iota(shape, dim):
    return lax.broadcasted_iota(jnp.int32, shape, dim)


def _div_pow2(x, p):
    assert p & (p - 1) == 0
    return lax.shift_right_logical(x, jnp.int32(p.bit_length() - 1))


def _dot(a, b, dims=((1,), (0,)), hi=False):
    if hi:
        return lax.dot_general(a.astype(f32), b.astype(f32), (dims, ((), ())), precision=HI, preferred_element_type=f32)
    return lax.dot_general(a.astype(MXU_DT), b.astype(MXU_DT), (dims, ((), ())), preferred_element_type=f32)


_NT = ((1,), (1,))
_TN = ((0,), (0,))


def _split3(x):
    hi = x.astype(jnp.bfloat16)
    r = x - hi.astype(f32)
    mid = r.astype(jnp.bfloat16)
    return hi, mid, (r - mid.astype(f32)).astype(jnp.bfloat16)


def _seldot_impl(a, b, dims, exact):
    def d(x, y):
        return lax.dot_general(x, y, (dims, ((), ())), preferred_element_type=f32)

    if exact == 0:
        a01 = a.astype(jnp.bfloat16)
        t = _split3(b.astype(f32))
        return (d(a01, t[0]) + d(a01, t[1])) + d(a01, t[2])
    b01 = b.astype(jnp.bfloat16)
    t = _split3(a.astype(f32))
    return (d(t[0], b01) + d(t[1], b01)) + d(t[2], b01)


@functools.partial(jax.custom_vjp, nondiff_argnums=(2, 3))
def _seldot(a, b, dims, exact):
    return _seldot_impl(a, b, dims, exact)


def _seldot_fwd(a, b, dims, exact):
    return _seldot_impl(a, b, dims, exact), (a, b)


def _seldot_bwd(dims, exact, res, dy):
    a, b = res
    (ca,), (cb,) = dims
    if exact == 0:
        assert ca == 1
        db = _seldot_impl(a, dy, _TN, 0) if cb == 0 else _seldot_impl(dy, a, _TN, 1)
        return jnp.zeros_like(a), db
    assert ca == 1 and cb == 0
    return _seldot_impl(dy, b, _NT, 1), jnp.zeros_like(b)


_seldot.defvjp(_seldot_fwd, _seldot_bwd)


def _softplus(x):
    return jnp.maximum(x, 0.0) + jnp.log1p(jnp.exp(-jnp.abs(x)))


def _silu(x):
    return x * jax.nn.sigmoid(x)


def _mm(a, b, *, name, ta=False, tb=False, add=None, add_scale=1.0, out_dtype=f32, comm=None):
    M, K = (a.shape[1], a.shape[0]) if ta else a.shape
    N = b.shape[0] if tb else b.shape[1]
    assert (b.shape[1] if tb else b.shape[0]) == K, (a.shape, b.shape, ta, tb)
    tm, tn, tk = _tile(M, MM_TILE_M), _tile(N, MM_TILE_N), _tile(K, MM_TILE_K)
    ab, bb = jnp.dtype(a.dtype).itemsize, jnp.dtype(b.dtype).itemsize
    while 2 * tk * (tm * ab + tn * bb) + 12 * tm * tn > MM_VMEM_BUDGET and tk % (2 * LANES) == 0:
        tk //= 2
    nk = K // tk
    a_spec = pl.BlockSpec((tk, tm), lambda i, j, k: (k, i)) if ta else pl.BlockSpec((tm, tk), lambda i, j, k: (i, k))
    b_spec = pl.BlockSpec((tn, tk), lambda i, j, k: (j, k)) if tb else pl.BlockSpec((tk, tn), lambda i, j, k: (k, j))
    o_spec = pl.BlockSpec((tm, tn), lambda i, j, k: (i, j))
    dims = ((0 if ta else 1,), (1 if tb else 0,))
    has_add = add is not None

    def body(*refs):
        a_ref, b_ref = refs[:2]
        add_ref = refs[2] if has_add else None
        o_ref = refs[3 if has_add else 2]

        def finish(r):
            if has_add:
                r = r + add_scale * add_ref[...].astype(f32)
            o_ref[...] = r.astype(out_dtype)

        if nk == 1:
            finish(_dot(a_ref[...], b_ref[...], dims))
            return
        acc = refs[-1]
        k = pl.program_id(2)

        @pl.when(k == 0)
        def _():
            acc[...] = jnp.zeros_like(acc)

        acc[...] += _dot(a_ref[...], b_ref[...], dims)

        @pl.when(k == nk - 1)
        def _():
            finish(acc[...])

    ins = [a, b] + ([add] if has_add else [])
    specs = [a_spec, b_spec] + ([o_spec] if has_add else [])
    (out,), cres = _call(
        body, comm, name=name, grid=(M // tm, N // tn, nk), in_specs=specs, out_specs=[o_spec],
        out_shape=[jax.ShapeDtypeStruct((M, N), out_dtype)], scratch_shapes=[pltpu.VMEM((tm, tn), f32)] if nk > 1 else [],
        sem=("parallel", "parallel", "arbitrary"), args=ins)
    return out if comm is None else (out, cres)


def _rw_specs(params, ins, ncol, tm):
    specs = []
    for arr, mode, bw, coff in params:
        if mode == "c":
            specs.append(pl.BlockSpec((1, bw), lambda c, r, coff=coff: (0, c + coff)))
        else:
            specs.append(pl.BlockSpec((1, bw), lambda c, r, coff=coff: (0, coff)))
    for arr, mode, bw, coff in ins:
        if mode == "c":
            specs.append(pl.BlockSpec((tm, bw), lambda c, r, coff=coff: (r, c + coff)))
        else:
            specs.append(pl.BlockSpec((tm, bw), lambda c, r, coff=coff: (r, coff)))
    return specs


def _norm_spec(lst):
    out = []
    for t in lst:
        arr, mode, bw = t[0], t[1], t[2]
        coff = t[3] if len(t) > 3 else 0
        out.append((arr, mode, bw, coff))
    return out


def _rowwise(fn, params, ins, outs, *, name, ncol=1, tm=None):
    params, ins = _norm_spec(params), _norm_spec(ins)
    S = ins[0][0].shape[0]
    tm = _tile(S, tm or ROW_TILE, 8)
    npar, nin = len(params), len(ins)

    def body(*refs):
        pv = [r[...].astype(f32) for r in refs[:npar]]
        iv = [r[...].astype(f32) for r in refs[npar:npar + nin]]
        res = fn(*pv, *iv)
        for o_ref, val in zip(refs[npar + nin:], res):
            o_ref[...] = val.astype(o_ref.dtype)

    out_specs, out_shapes = [], []
    for W, dt, mode, bw in outs:
        out_shapes.append(jax.ShapeDtypeStruct((S, W), dt))
        if mode == "c":
            out_specs.append(pl.BlockSpec((tm, bw), lambda c, r: (r, c)))
        else:
            out_specs.append(pl.BlockSpec((tm, bw), lambda c, r: (r, 0)))
    return pl.pallas_call(
        body, name=name, grid=(ncol, S // tm), in_specs=_rw_specs(params, ins, ncol, tm), out_specs=out_specs,
        out_shape=out_shapes, compiler_params=_cp(("parallel", "parallel")))(*[p[0] for p in params], *[i[0] for i in ins])


def _rowwise_bwd(fn, params, ins, couts, *, name, ncol=1, tm=None, diff_p=None, diff_i=None, din_dtypes=None, comm=None):
    params, ins, couts = _norm_spec(params), _norm_spec(ins), _norm_spec(couts)
    S = ins[0][0].shape[0]
    tm = _tile(S, tm or ROW_TILE, 8)
    npar, nin, nco = len(params), len(ins), len(couts)
    diff_p = list(range(npar)) if diff_p is None else diff_p
    diff_i = list(range(nin)) if diff_i is None else diff_i
    din_dtypes = [(f32,)] * len(diff_i) if din_dtypes is None else din_dtypes

    def body(*refs):
        c, r = pl.program_id(0), pl.program_id(1)
        pv = [x[...].astype(f32) for x in refs[:npar]]
        iv = [x[...].astype(f32) for x in refs[npar:npar + nin]]
        cv = [x[...].astype(f32) for x in refs[npar + nin:npar + nin + nco]]
        orefs = refs[npar + nin + nco:]

        def g(*dargs):
            p2, i2 = list(pv), list(iv)
            for n, k in enumerate(diff_p):
                p2[k] = dargs[n]
            for n, k in enumerate(diff_i):
                i2[k] = dargs[len(diff_p) + n]
            return tuple(fn(*p2, *i2))

        _, vjp = jax.vjp(g, *[pv[k] for k in diff_p], *[iv[k] for k in diff_i])
        grads = vjp(tuple(cv))
        for n, k in enumerate(diff_p):
            o_ref = orefs[n]
            first = (r == 0) if params[k][1] == "c" else jnp.logical_and(r == 0, c == 0)

            @pl.when(first)
            def _(o_ref=o_ref):
                o_ref[...] = jnp.zeros_like(o_ref)

            o_ref[...] += grads[n]
        pos = len(diff_p)
        for n, k in enumerate(diff_i):
            for _ in din_dtypes[n]:
                orefs[pos][...] = grads[len(diff_p) + n].astype(orefs[pos].dtype)
                pos += 1

    out_specs, out_shapes = [], []
    for k in diff_p:
        arr, mode, bw, coff = params[k]
        W = bw * ncol if mode == "c" else bw
        out_shapes.append(jax.ShapeDtypeStruct((1, W), f32))
        out_specs.append(pl.BlockSpec((1, bw), (lambda c, r: (0, c)) if mode == "c" else (lambda c, r: (0, 0))))
    for n, k in enumerate(diff_i):
        arr, mode, bw, coff = ins[k]
        W = bw * ncol if mode == "c" else bw
        for dt in din_dtypes[n]:
            out_shapes.append(jax.ShapeDtypeStruct((S, W), dt))
            out_specs.append(pl.BlockSpec((tm, bw), (lambda c, r: (r, c)) if mode == "c" else (lambda c, r: (r, 0))))
    res, cres = _call(
        body, comm, name=name, grid=(ncol, S // tm), in_specs=_rw_specs(params, ins + couts, ncol, tm), out_specs=out_specs,
        out_shape=out_shapes, scratch_shapes=[], sem=("arbitrary", "arbitrary"),
        args=(*[p[0] for p in params], *[i[0] for i in ins], *[c[0] for c in couts]))
    if comm is None:
        return list(res[:len(diff_p)]), list(res[len(diff_p):])
    return list(res[:len(diff_p)]), list(res[len(diff_p):]), cres


def _ln_fn(g, b, r):
    mu = jnp.mean(r, -1, keepdims=True)
    xc = r - mu
    var = jnp.mean(xc * xc, -1, keepdims=True)
    return (xc * lax.rsqrt(var + LN_EPS) * g + b,)


def _res_ln_fn(g, b, h, y):
    r = _alpha() * h + y
    hn = _ln_fn(g, b, r)
    return (r,) + hn + hn


def _rms_fn(w, x):
    return (x * lax.rsqrt(jnp.mean(x * x, -1, keepdims=True) + RMS_EPS) * w,)


def _ssd_gate_fn(w, y, z):
    yg = y * _silu(z)
    return (yg * lax.rsqrt(jnp.mean(yg * yg, -1, keepdims=True) + RMS_EPS) * w,)


def _mul_silu_fn(o, z):
    return (o * _silu(z),)


def _gdn_gate_fn(w, o, z):
    return (o * lax.rsqrt(jnp.mean(o * o, -1, keepdims=True) + RMS_EPS) * w * _silu(z),)


def _l2_fn(scale, x):
    return (x * lax.rsqrt(jnp.sum(x * x, -1, keepdims=True) + RMS_EPS) * scale,)


def _rope_fn(cos, sin, x):
    half = MLA_ROPE // 2
    i = _iota((LANES, LANES), 0)
    j = _iota((LANES, LANES), 1)
    pm = jnp.where((i == j + half) & (j < half), -1.0, 0.0) + jnp.where((i + half == j) & (j < 2 * half), 1.0, 0.0)
    return (x * cos + _seldot(x, pm.astype(f32), ((1,), (0,)), 1) * sin,)


def _conv_taps(x, K):
    S = x.shape[0]
    rows = _iota(x.shape, 0)
    return [x] + [jnp.where(rows < j, 0.0, pltpu.roll(x, j, 0)) for j in range(1, K)]


def _conv_fwd(x, w, b, *, name):
    S, C = x.shape
    K = w.shape[0]
    cw = _tile(C, LANES)

    def body(x_ref, w_ref, b_ref, o_ref):
        taps = _conv_taps(x_ref[...], K)
        wv = w_ref[...]
        pre = b_ref[...] + taps[0] * wv[K - 1:K, :]
        for j in range(1, K):
            pre = pre + taps[j] * wv[K - 1 - j:K - j, :]
        o_ref[...] = _silu(pre)

    return pl.pallas_call(
        body, name=name, grid=(C // cw,),
        in_specs=[pl.BlockSpec((S, cw), lambda c: (0, c)), pl.BlockSpec((K, cw), lambda c: (0, c)), pl.BlockSpec((1, cw), lambda c: (0, c))],
        out_specs=pl.BlockSpec((S, cw), lambda c: (0, c)), out_shape=jax.ShapeDtypeStruct((S, C), f32),
        compiler_params=_cp(("parallel",)))(x, w, b)


def _ssd_grouped_block(c, G, GW, N, cw):
    assert N == cw and GW % cw == 0
    nq = GW // cw
    per, nx = nq + 2, G * nq
    in_x = lax.div(c, nq) * per + lax.rem(c, nq)
    return jnp.where(c < nx, in_x, jnp.where(c < nx + G, (c - nx) * per + nq, (c - nx - G) * per + nq + 1))


def _conv_bwd(x, w, b, dys, *, name, dx_dtype=f32, dy_block=None):
    S, C = x.shape
    K = w.shape[0]
    cw = _tile(C, LANES)
    nblk = [d.shape[1] // cw for d in dys]
    offs = [sum(nblk[:p]) for p in range(len(dys))]
    assert sum(nblk) == C // cw and all(d.shape[1] % cw == 0 for d in dys)
    npc = len(dys)
    assert dy_block is None or npc == 1

    def body(x_ref, w_ref, b_ref, *refs):
        dy_refs, (dx_ref, dw_ref, db_ref) = refs[:npc], refs[npc:]
        if npc == 1:
            _conv_bwd_block(x_ref, w_ref, b_ref, dy_refs[0], dx_ref, dw_ref, db_ref, K, S, dx_dtype)
            return
        c = pl.program_id(0)
        for p in range(npc):
            @pl.when(jnp.logical_and(c >= offs[p], c < offs[p] + nblk[p]))
            def _(p=p):
                _conv_bwd_block(x_ref, w_ref, b_ref, dy_refs[p], dx_ref, dw_ref, db_ref, K, S, dx_dtype)

    col = lambda c: (0, c)
    if dy_block is not None:
        dy_specs = [pl.BlockSpec((S, cw), lambda c: (0, dy_block(c, cw)))]
    else:
        dy_specs = [pl.BlockSpec((S, cw), lambda c, o=offs[p], n=nblk[p]: (0, jnp.clip(c - o, 0, n - 1))) for p in range(npc)]
    return pl.pallas_call(
        body, name=name, grid=(C // cw,),
        in_specs=[pl.BlockSpec((S, cw), col), pl.BlockSpec((K, cw), col), pl.BlockSpec((1, cw), col)] + dy_specs,
        out_specs=[pl.BlockSpec((S, cw), col), pl.BlockSpec((K, cw), col), pl.BlockSpec((1, cw), col)],
        out_shape=[jax.ShapeDtypeStruct((S, C), dx_dtype), jax.ShapeDtypeStruct((K, C), f32), jax.ShapeDtypeStruct((1, C), f32)],
        compiler_params=_cp(("parallel",)))(x, w, b, *dys)


def _conv_bwd_block(x_ref, w_ref, b_ref, dy_ref, dx_ref, dw_ref, db_ref, K, S, dx_dtype):
    taps = _conv_taps(x_ref[...], K)
    wv = w_ref[...]
    pre = b_ref[...] + taps[0] * wv[K - 1:K, :]
    for j in range(1, K):
        pre = pre + taps[j] * wv[K - 1 - j:K - j, :]
    sg = jax.nn.sigmoid(pre)
    dpre = dy_ref[...] * sg * (1.0 + pre * (1.0 - sg))
    db_ref[...] = jnp.sum(dpre, axis=0, keepdims=True)
    rows = _iota(dpre.shape, 0)
    dx = dpre * wv[K - 1:K, :]
    dw_ref[K - 1:K, :] = jnp.sum(dpre * taps[0], axis=0, keepdims=True)
    for j in range(1, K):
        dw_ref[K - 1 - j:K - j, :] = jnp.sum(dpre * taps[j], axis=0, keepdims=True)
        up = jnp.where(rows >= S - j, 0.0, pltpu.roll(dpre, S - j, 0))
        dx = dx + up * wv[K - 1 - j:K - j, :]
    dx_ref[...] = dx.astype(dx_dtype)


def _ssd_chunk(prev, xs, Bm, Cm, dtr, dtb, alog, dsk, g, *, R, P):
    L, GW = xs.shape
    H = dtr.shape[1]
    tril = _iota((L, L), 0) >= _iota((L, L), 1)
    dt = _softplus(dtr + dtb)
    acs = _seldot(tril.astype(f32), dt * (-jnp.exp(alog)), ((1,), (0,)), 0)
    expand = (_iota((H, GW), 0) == g * R + _div_pow2(_iota((H, GW), 1), P)).astype(f32)
    dt_e = _seldot(dt, expand, ((1,), (0,)), 1)
    acs_e = _seldot(acs, expand, ((1,), (0,)), 1)
    d_e = jnp.sum(_seldot(jnp.broadcast_to(dsk, (8, H)), expand, ((1,), (0,)), 1), axis=0, keepdims=True) * 0.125
    last = jnp.sum(jnp.where(_iota((L, GW), 0) == L - 1, acs_e, 0.0), axis=0, keepdims=True)
    xdt = xs * dt_e
    cb = _dot(Cm, Bm, _NT)
    nsel = max(R, 8)
    sel = (_iota((nsel, H), 1) == g * R + _iota((nsel, H), 0)).astype(f32)
    acs_t = _seldot(sel, acs, _NT, 0)
    hp = LANES // P
    pieces = []
    for p in range(GW // LANES):
        xp = xdt[:, p * LANES:(p + 1) * LANES]
        acc = None
        for q in range(hp):
            r = p * hp + q
            col = jnp.sum(jnp.where(_iota((L, H), 1) == g * R + r, acs, 0.0), axis=1, keepdims=True)
            row = jnp.sum(jnp.where(_iota((nsel, L), 0) == r, acs_t, 0.0), axis=0, keepdims=True)
            dec = jnp.where(tril, jnp.exp(jnp.where(tril, col - row, 0.0)), 0.0)
            xm = jnp.where(_div_pow2(_iota((L, LANES), 1), P) == q, xp, 0.0)
            t = _dot(cb * dec, xm)
            acc = t if acc is None else acc + t
        pieces.append(acc)
    y_diag = pieces[0] if len(pieces) == 1 else jnp.concatenate(pieces, axis=1)
    st = _dot(Bm, xdt * jnp.exp(last - acs_e), _TN)
    y_off = _dot(Cm, prev) * jnp.exp(acs_e)
    new = prev * jnp.exp(last) + st
    return y_diag + y_off + xs * d_e, new


def _ssd_dims(xbc, dtr):
    S, CD = xbc.shape
    H = dtr.shape[1]
    G, N, P = SSD_N_GROUPS, SSD_D_STATE, SSD_HEAD_DIM
    DI = H * P
    R = H // G
    assert CD == DI + 2 * G * N and DI % N == 0
    return S, H, G, N, P, DI, R, R * P, SSD_CHUNK


def _ssd_groups_per_step(G, DI, N):
    GB = min(SSD_GROUPS_PER_STEP, G)
    assert G % GB == 0 and (DI // N) % GB == 0
    return GB


def _ssd_scan_fwd(xbc, dtr, dtb, alog, dsk, *, name, comm=None):
    S, H, G, N, P, DI, R, GW, L = _ssd_dims(xbc, dtr)
    nc = S // L
    GB = _ssd_groups_per_step(G, DI, N)
    boff, coff = DI // N // GB, (DI // N + G) // GB

    def body(xs_ref, b_ref, c_ref, dtr_ref, dtb_ref, alog_ref, dsk_ref, y_ref, st_ref, state):
        c, gb = pl.program_id(0), pl.program_id(1)
        for gg in range(GB):
            g = gb * GB + gg

            @pl.when(c == 0)
            def _(g=g):
                state[g] = jnp.zeros((N, GW), f32)

            prev = state[g]
            st_ref[0, gg] = prev
            y, new = _ssd_chunk(prev, xs_ref[:, gg * GW:(gg + 1) * GW], b_ref[:, gg * N:(gg + 1) * N], c_ref[:, gg * N:(gg + 1) * N],
                                dtr_ref[...], dtb_ref[...], alog_ref[...], dsk_ref[...], g, R=R, P=P)
            y_ref[:, gg * GW:(gg + 1) * GW] = y
            state[g] = new

    par = pl.BlockSpec((1, H), lambda c, g: (0, 0))
    return _call(
        body, comm, name=name, grid=(nc, G // GB),
        in_specs=[pl.BlockSpec((L, GB * GW), lambda c, g: (c, g)), pl.BlockSpec((L, GB * N), lambda c, g: (c, boff + g)),
                  pl.BlockSpec((L, GB * N), lambda c, g: (c, coff + g)), pl.BlockSpec((L, H), lambda c, g: (c, 0)), par, par, par],
        out_specs=[pl.BlockSpec((L, GB * GW), lambda c, g: (c, g)), pl.BlockSpec((1, GB, N, GW), lambda c, g: (c, g, 0, 0))],
        out_shape=[jax.ShapeDtypeStruct((S, DI), f32), jax.ShapeDtypeStruct((nc, G, N, GW), f32)],
        scratch_shapes=[pltpu.VMEM((G, N, GW), f32)],
        sem=("arbitrary", "arbitrary"), args=(xbc, xbc, xbc, dtr, dtb, alog, dsk))


def _ssd_scan_bwd(xbc, dtr, dtb, alog, dsk, states, dy, *, name, comm=None):
    S, H, G, N, P, DI, R, GW, L = _ssd_dims(xbc, dtr)
    nc = S // L
    GB = _ssd_groups_per_step(G, DI, N)
    boff, coff = DI // N // GB, (DI // N + G) // GB
    PW = GW + 2 * N

    def body(xs_ref, b_ref, c_ref, dtr_ref, dtb_ref, alog_ref, dsk_ref, st_ref, dy_ref,
             dg_ref, ddtr_ref, ddtb_ref, dalog_ref, ddsk_ref, dstate):
        c, gb = pl.program_id(0), pl.program_id(1)

        @pl.when(jnp.logical_and(c == 0, gb == 0))
        def _():
            ddtb_ref[...] = jnp.zeros_like(ddtb_ref)
            dalog_ref[...] = jnp.zeros_like(dalog_ref)
            ddsk_ref[...] = jnp.zeros_like(ddsk_ref)

        @pl.when(gb == 0)
        def _():
            ddtr_ref[...] = jnp.zeros_like(ddtr_ref)

        for gg in range(GB):
            g = gb * GB + gg

            @pl.when(c == 0)
            def _(g=g):
                dstate[g] = jnp.zeros((N, GW), f32)

            fn = functools.partial(_ssd_chunk, g=g, R=R, P=P)
            _, vjp = jax.vjp(fn, st_ref[0, gg], xs_ref[:, gg * GW:(gg + 1) * GW], b_ref[:, gg * N:(gg + 1) * N],
                             c_ref[:, gg * N:(gg + 1) * N], dtr_ref[...], dtb_ref[...], alog_ref[...], dsk_ref[...])
            dprev, dxs, dB, dC, ddtr, ddtb, dalog, ddsk = vjp((dy_ref[:, gg * GW:(gg + 1) * GW], dstate[g]))
            dstate[g] = dprev
            dg_ref[:, gg * PW:gg * PW + GW] = dxs
            dg_ref[:, gg * PW + GW:gg * PW + GW + N] = dB
            dg_ref[:, gg * PW + GW + N:(gg + 1) * PW] = dC
            ddtr_ref[...] += ddtr
            ddtb_ref[...] += ddtb
            dalog_ref[...] += dalog
            ddsk_ref[...] += ddsk

    rc = lambda c: nc - 1 - c
    par = pl.BlockSpec((1, H), lambda c, g: (0, 0))
    return _call(
        body, comm, name=name, grid=(nc, G // GB),
        in_specs=[pl.BlockSpec((L, GB * GW), lambda c, g: (rc(c), g)), pl.BlockSpec((L, GB * N), lambda c, g: (rc(c), boff + g)),
                  pl.BlockSpec((L, GB * N), lambda c, g: (rc(c), coff + g)), pl.BlockSpec((L, H), lambda c, g: (rc(c), 0)),
                  par, par, par, pl.BlockSpec((1, GB, N, GW), lambda c, g: (rc(c), g, 0, 0)),
                  pl.BlockSpec((L, GB * GW), lambda c, g: (rc(c), g))],
        out_specs=[pl.BlockSpec((L, GB * PW), lambda c, g: (rc(c), g)), pl.BlockSpec((L, H), lambda c, g: (rc(c), 0)), par, par, par],
        out_shape=[jax.ShapeDtypeStruct((S, G * (GW + 2 * N)), f32), jax.ShapeDtypeStruct((S, H), f32)] + [jax.ShapeDtypeStruct((1, H), f32)] * 3,
        scratch_shapes=[pltpu.VMEM((G, N, GW), f32)],
        sem=("arbitrary", "arbitrary"), args=(xbc, xbc, xbc, dtr, dtb, alog, dsk, states, dy))


def _dot3(a, b, dims=((1,), (0,))):
    def split(x):
        hi = x.astype(jnp.bfloat16)
        return hi, (x - hi.astype(f32)).astype(jnp.bfloat16)

    def d(x, y):
        return lax.dot_general(x, y, (dims, ((), ())), preferred_element_type=f32)

    ah, al = split(a)
    bh, bl = split(b)
    return d(ah, bh) + (d(ah, bl) + d(al, bh))


def _neumann_inverses(As):
    L = As[0].shape[0]
    eye = (_iota((L, L), 0) == _iota((L, L), 1)).astype(f32)
    X = [-A for A in As]
    P = [eye + x for x in X]
    n = 1
    while 2 * n < L:
        X = [_dot3(x, x) for x in X]
        P = [p + _dot3(p, x) for p, x in zip(P, X)]
        n *= 2
    return P


@jax.custom_vjp
def _unit_lower_solves(Ts, As, Rs):
    return tuple(_dot3(T, R) for T, R in zip(Ts, Rs))


def _uls_fwd(Ts, As, Rs):
    Xs = tuple(_dot3(T, R) for T, R in zip(Ts, Rs))
    return Xs, (Ts, Xs)


def _uls_bwd(res, dXs):
    Ts, Xs = res
    dRs = tuple(_dot3(T, dX, _TN) for T, dX in zip(Ts, dXs))
    dAs = tuple(-_dot3(dR, X, _NT) for dR, X in zip(dRs, Xs))
    return tuple(jnp.zeros_like(T) for T in Ts), dAs, dRs


_unit_lower_solves.defvjp(_uls_fwd, _uls_bwd)


def _gdn_step(states, qb, kb_, vb, br, ar, alog, dtb, h0, *, rep, inverses=None):
    HB = len(states)
    L = qb.shape[0]
    DK, DV = states[0].shape
    HV = br.shape[1]
    incl = _iota((L, L), 0) >= _iota((L, L), 1)
    strict = _iota((L, L), 0) > _iota((L, L), 1)
    lane = _iota((L, HV), 1)
    g_all = -jnp.exp(alog) * _softplus(ar + dtb)
    gcs = _seldot(incl.astype(f32), g_all, ((1,), (0,)), 0)
    beta_all = jax.nn.sigmoid(br)
    nsel = max(HV, 8)
    gcs_t = _seldot((_iota((nsel, HV), 0) == _iota((nsel, HV), 1)).astype(f32), gcs, _NT, 0)
    hs = range(HB)
    q = [qb[:, (hh // rep) * DK:(hh // rep + 1) * DK] for hh in hs]
    k = [kb_[:, (hh // rep) * DK:(hh // rep + 1) * DK] for hh in hs]
    v = [vb[:, hh * DV:(hh + 1) * DV] for hh in hs]
    gc = [jnp.sum(jnp.where(lane == h0 + hh, gcs, 0.0), axis=1, keepdims=True) for hh in hs]
    beta = [jnp.sum(jnp.where(lane == h0 + hh, beta_all, 0.0), axis=1, keepdims=True) for hh in hs]
    gc_row = [jnp.sum(jnp.where(_iota((nsel, L), 0) == h0 + hh, gcs_t, 0.0), axis=0, keepdims=True) for hh in hs]
    decay = [jnp.where(incl, jnp.exp(jnp.where(incl, gc[hh] - gc_row[hh], 0.0)), 0.0) for hh in hs]
    kbeta = [k[hh] * beta[hh] for hh in hs]
    a_mat = [jnp.where(strict, _dot(kbeta[hh], k[hh], _NT) * decay[hh], 0.0) for hh in hs]
    eg = [jnp.exp(gc[hh]) for hh in hs]
    rhs = tuple(jnp.concatenate([v[hh] * beta[hh], kbeta[hh] * eg[hh]], axis=1) for hh in hs)
    if inverses is None:
        made = tuple(_neumann_inverses(a_mat))
        sol = tuple(_dot3(T, R) for T, R in zip(made, rhs))
    else:
        sol = _unit_lower_solves(tuple(inverses), tuple(a_mat), rhs)
    qk = [jnp.where(incl, _dot(q[hh], k[hh], _NT) * decay[hh], 0.0) for hh in hs]
    g_last = [jnp.sum(jnp.where(_iota((L, 1), 0) == L - 1, gc[hh], 0.0), axis=0, keepdims=True) for hh in hs]
    v_new = [sol[hh][:, :DV] - _dot(sol[hh][:, DV:], states[hh]) for hh in hs]
    outs = [_dot(q[hh] * eg[hh], states[hh]) + _dot(qk[hh], v_new[hh]) for hh in hs]
    news = [states[hh] * jnp.exp(g_last[hh]) + _dot(k[hh] * jnp.exp(g_last[hh] - gc[hh]), v_new[hh], _TN) for hh in hs]
    o = outs[0] if HB == 1 else jnp.concatenate(outs, axis=1)
    return (o, tuple(news), made) if inverses is None else (o, tuple(news))


def _gdn_dims(heads_per_step):
    HK, HV = GDN_N_QK_HEADS, GDN_N_V_HEADS
    rep = HV // HK
    HB = min(heads_per_step, HV)
    assert HV % HB == 0 and HB % rep == 0
    return HK, HV, GDN_DK, GDN_DV, GDN_CHUNK, rep, HB


def _gdn_scan_fwd(qkn, qkv, br, ar, alog, dtb, *, name, comm=None):
    S = qkn.shape[0]
    HK, HV, DK, DV, L, rep, HB = _gdn_dims(GDN_FWD_HEADS_PER_STEP)
    nc = S // L
    QW = HB // rep * DK
    koff = HK * DK // QW
    voff = 2 * HK * DK // (HB * DV)

    def body(q_ref, k_ref, v_ref, br_ref, ar_ref, alog_ref, dtb_ref, o_ref, st_ref, inv_ref, state):
        c, hb = pl.program_id(0), pl.program_id(1)
        h0 = hb * HB

        @pl.when(c == 0)
        def _():
            for hh in range(HB):
                state[h0 + hh] = jnp.zeros((DK, DV), f32)

        prev = tuple(state[h0 + hh] for hh in range(HB))
        for hh in range(HB):
            st_ref[0, hh] = prev[hh]
        o, new, inv = _gdn_step(prev, q_ref[...], k_ref[...], v_ref[...], br_ref[...], ar_ref[...], alog_ref[...], dtb_ref[...],
                                h0, rep=rep)
        o_ref[...] = o
        for hh in range(HB):
            state[h0 + hh] = new[hh]
            inv_ref[0, hh] = inv[hh]

    par = pl.BlockSpec((1, HV), lambda c, h: (0, 0))
    return _call(
        body, comm, name=name, grid=(nc, HV // HB),
        in_specs=[pl.BlockSpec((L, QW), lambda c, h: (c, h)), pl.BlockSpec((L, QW), lambda c, h: (c, koff + h)),
                  pl.BlockSpec((L, HB * DV), lambda c, h: (c, voff + h)), pl.BlockSpec((L, HV), lambda c, h: (c, 0)),
                  pl.BlockSpec((L, HV), lambda c, h: (c, 0)), par, par],
        out_specs=[pl.BlockSpec((L, HB * DV), lambda c, h: (c, h)), pl.BlockSpec((1, HB, DK, DV), lambda c, h: (c, h, 0, 0)),
                   pl.BlockSpec((1, HB, L, L), lambda c, h: (c, h, 0, 0))],
        out_shape=[jax.ShapeDtypeStruct((S, HV * DV), f32), jax.ShapeDtypeStruct((nc, HV, DK, DV), f32),
                   jax.ShapeDtypeStruct((nc, HV, L, L), f32)],
        scratch_shapes=[pltpu.VMEM((HV, DK, DV), f32)],
        sem=("arbitrary", "arbitrary"), args=(qkn, qkn, qkv, br, ar, alog, dtb))


def _gdn_scan_bwd(qkn, qkv, br, ar, alog, dtb, states, inverses, do, *, name, comm=None):
    S = qkn.shape[0]
    HK, HV, DK, DV, L, rep, HB = _gdn_dims(GDN_HEADS_PER_STEP)
    nc = S // L
    QW = HB // rep * DK
    koff = HK * DK // QW
    voff = 2 * HK * DK // (HB * DV)

    def body(q_ref, k_ref, v_ref, br_ref, ar_ref, alog_ref, dtb_ref, st_ref, inv_ref, do_ref,
             dq_ref, dk_ref, dv_ref, dbr_ref, dar_ref, dalog_ref, ddtb_ref, dstate):
        c, hb = pl.program_id(0), pl.program_id(1)
        h0 = hb * HB

        @pl.when(c == 0)
        def _():
            for hh in range(HB):
                dstate[h0 + hh] = jnp.zeros((DK, DV), f32)

        @pl.when(jnp.logical_and(c == 0, hb == 0))
        def _():
            dalog_ref[...] = jnp.zeros_like(dalog_ref)
            ddtb_ref[...] = jnp.zeros_like(ddtb_ref)

        @pl.when(hb == 0)
        def _():
            dbr_ref[...] = jnp.zeros_like(dbr_ref)
            dar_ref[...] = jnp.zeros_like(dar_ref)

        fn = functools.partial(_gdn_step, h0=h0, rep=rep, inverses=tuple(inv_ref[0, hh] for hh in range(HB)))
        prev = tuple(st_ref[0, hh] for hh in range(HB))
        _, vjp = jax.vjp(fn, prev, q_ref[...], k_ref[...], v_ref[...], br_ref[...], ar_ref[...], alog_ref[...], dtb_ref[...])
        dprev, dq, dk, dv, dbr, dar, dalog, ddtb = vjp((do_ref[...], tuple(dstate[h0 + hh] for hh in range(HB))))
        for hh in range(HB):
            dstate[h0 + hh] = dprev[hh]
        dq_ref[...] = dq
        dk_ref[...] = dk
        dv_ref[...] = dv
        dbr_ref[...] += dbr
        dar_ref[...] += dar
        dalog_ref[...] += dalog
        ddtb_ref[...] += ddtb

    rc = lambda c: nc - 1 - c
    par = pl.BlockSpec((1, HV), lambda c, h: (0, 0))
    blk = lambda W: pl.BlockSpec((L, W), lambda c, h: (rc(c), h))
    return _call(
        body, comm, name=name, grid=(nc, HV // HB),
        in_specs=[pl.BlockSpec((L, QW), lambda c, h: (rc(c), h)), pl.BlockSpec((L, QW), lambda c, h: (rc(c), koff + h)),
                  pl.BlockSpec((L, HB * DV), lambda c, h: (rc(c), voff + h)), pl.BlockSpec((L, HV), lambda c, h: (rc(c), 0)),
                  pl.BlockSpec((L, HV), lambda c, h: (rc(c), 0)), par, par,
                  pl.BlockSpec((1, HB, DK, DV), lambda c, h: (rc(c), h, 0, 0)),
                  pl.BlockSpec((1, HB, L, L), lambda c, h: (rc(c), h, 0, 0)), blk(HB * DV)],
        out_specs=[blk(QW), blk(QW), blk(HB * DV), pl.BlockSpec((L, HV), lambda c, h: (rc(c), 0)),
                   pl.BlockSpec((L, HV), lambda c, h: (rc(c), 0)), par, par],
        out_shape=[jax.ShapeDtypeStruct((S, HK * DK), f32), jax.ShapeDtypeStruct((S, HK * DK), f32), jax.ShapeDtypeStruct((S, HV * DV), f32),
                   jax.ShapeDtypeStruct((S, HV), f32), jax.ShapeDtypeStruct((S, HV), f32),
                   jax.ShapeDtypeStruct((1, HV), f32), jax.ShapeDtypeStruct((1, HV), f32)],
        scratch_shapes=[pltpu.VMEM((HV, DK, DV), f32)],
        sem=("arbitrary", "arbitrary"), args=(qkn, qkn, qkv, br, ar, alog, dtb, states, inverses, do))


def _att_scale():
    return (MLA_NOPE + MLA_ROPE) ** -0.5


def _causal(s, i, j, t, tk=None):
    qpos = i * t + _iota(s.shape, 0)
    kpos = j * (t if tk is None else tk) + _iota(s.shape, 1)
    return kpos <= qpos


def _attn_fwd(qn, qr, kn, kr, v, *, name, comm=None):
    S, W = qn.shape
    H = W // LANES
    t = _tile(S, ATT_Q_TILE)
    tk = _tile(S, ATT_KEY_TILE)
    assert tk % t == 0
    scale = _att_scale()

    def body(qn_ref, qr_ref, kn_ref, kr_ref, v_ref, o_ref, lse_ref):
        i = pl.program_id(1)
        qc = jnp.concatenate([qn_ref[...], qr_ref[...]], axis=1)

        def step(j, carry, masked):
            m, l, acc = carry
            rows = pl.ds(pl.multiple_of(j * tk, tk), tk)
            s = _dot(qc, jnp.concatenate([kn_ref[rows, :], kr_ref[rows, :]], axis=1), _NT) * scale
            if masked:
                s = jnp.where(_causal(s, i, j, t, tk), s, -1e30)
            m_new = jnp.maximum(m, jnp.max(s, axis=1, keepdims=True))
            p = jnp.exp(s - m_new)
            a = jnp.exp(m - m_new)
            return m_new, a * l + jnp.sum(p, axis=1, keepdims=True), a * acc + _dot(p, v_ref[rows, :])

        nfull = lax.div(i * t, tk)
        carry = lax.fori_loop(0, nfull, functools.partial(step, masked=False),
                              (jnp.full((t, 1), -1e30, f32), jnp.zeros((t, 1), f32), jnp.zeros((t, LANES), f32)))
        m, l, acc = step(nfull, carry, True)
        o_ref[...] = acc / l
        lse_ref[...] = jnp.broadcast_to(m + jnp.log(l), (t, LANES))

    qb = pl.BlockSpec((t, LANES), lambda h, i: (i, h))
    kb = pl.BlockSpec((S, LANES), lambda h, i: (0, h))
    return _call(
        body, comm, name=name, grid=(H, S // t),
        in_specs=[qb, qb, kb, pl.BlockSpec((S, LANES), lambda h, i: (0, 0)), kb],
        out_specs=[qb, qb], out_shape=[jax.ShapeDtypeStruct((S, W), f32), jax.ShapeDtypeStruct((S, W), f32)],
        scratch_shapes=[], sem=("parallel", "arbitrary"), args=(qn, qr, kn, kr, v))


def _attn_bwd_dq(qn, qr, kn, kr, v, o, lse, do, *, name, comm=None):
    S, W = qn.shape
    H = W // LANES
    t = _tile(S, ATT_TILE)
    scale = _att_scale()

    def body(qn_ref, qr_ref, kn_ref, kr_ref, v_ref, o_ref, lse_ref, do_ref, dqn_ref, dqr_ref):
        i = pl.program_id(1)
        qc = jnp.concatenate([qn_ref[...], qr_ref[...]], axis=1)
        dov = do_ref[...]
        delta = jnp.sum(dov * o_ref[...], axis=1, keepdims=True)
        lsev = lse_ref[...][:, :1]

        def step(j, dq, masked):
            rows = pl.ds(pl.multiple_of(j * t, t), t)
            kc = jnp.concatenate([kn_ref[rows, :], kr_ref[rows, :]], axis=1)
            s = _dot(qc, kc, _NT) * scale
            p = jnp.exp(s - lsev)
            if masked:
                p = jnp.where(_causal(s, i, j, t), p, 0.0)
            ds = p * (_dot(dov, v_ref[rows, :], _NT) - delta)
            return dq + _dot(ds, kc)

        dq = lax.fori_loop(0, i, functools.partial(step, masked=False), jnp.zeros((t, 2 * LANES), f32))
        dq = step(i, dq, True)
        dq = dq * scale
        dqn_ref[...] = dq[:, :LANES].astype(MXU_DT)
        dqr_ref[...] = dq[:, LANES:]

    qb = pl.BlockSpec((t, LANES), lambda h, i: (i, h))
    kb = pl.BlockSpec((S, LANES), lambda h, i: (0, h))
    return _call(
        body, comm, name=name, grid=(H, S // t),
        in_specs=[qb, qb, kb, pl.BlockSpec((S, LANES), lambda h, i: (0, 0)), kb, qb, qb, qb],
        out_specs=[qb, qb], out_shape=[jax.ShapeDtypeStruct((S, W), MXU_DT), jax.ShapeDtypeStruct((S, W), f32)],
        scratch_shapes=[], sem=("parallel", "arbitrary"), args=(qn, qr, kn, kr, v, o, lse, do))


def _attn_bwd_dkv(qn, qr, kn, kr, v, o, lse, do, *, name, comm=None):
    S, W = qn.shape
    H = W // LANES
    t = _tile(S, ATT_TILE)
    nb = S // t
    scale = _att_scale()

    def body(qn_ref, qr_ref, kn_ref, kr_ref, v_ref, o_ref, lse_ref, do_ref, dkn_ref, dkr_ref, dv_ref):
        j, h = pl.program_id(0), pl.program_id(1)
        kc = jnp.concatenate([kn_ref[...], kr_ref[...]], axis=1)
        vv = v_ref[...]

        def step(i, carry, masked):
            dk, dv = carry
            rows = pl.ds(pl.multiple_of(i * t, t), t)
            qc = jnp.concatenate([qn_ref[rows, :], qr_ref[rows, :]], axis=1)
            dov = do_ref[rows, :]
            delta = jnp.sum(dov * o_ref[rows, :], axis=1, keepdims=True)
            s = _dot(qc, kc, _NT) * scale
            p = jnp.exp(s - lse_ref[rows, :][:, :1])
            if masked:
                p = jnp.where(_causal(s, i, j, t), p, 0.0)
            ds = p * (_dot(dov, vv, _NT) - delta)
            return dk + _dot(ds, qc, _TN), dv + _dot(p, dov, _TN)

        carry = step(j, (jnp.zeros((t, 2 * LANES), f32), jnp.zeros((t, LANES), f32)), True)
        dk, dv = lax.fori_loop(j + 1, nb, functools.partial(step, masked=False), carry)
        dk = dk * scale
        dkn_ref[...] = dk[:, :LANES].astype(MXU_DT)
        dv_ref[...] = dv.astype(MXU_DT)

        @pl.when(h == 0)
        def _():
            dkr_ref[...] = jnp.zeros_like(dkr_ref)

        dkr_ref[...] += dk[:, LANES:]

    full = pl.BlockSpec((S, LANES), lambda j, h: (0, h))
    kb = pl.BlockSpec((t, LANES), lambda j, h: (j, h))
    k0 = pl.BlockSpec((t, LANES), lambda j, h: (j, 0))
    return _call(
        body, comm, name=name, grid=(nb, H),
        in_specs=[full, full, kb, k0, kb, full, full, full],
        out_specs=[kb, k0, kb],
        out_shape=[jax.ShapeDtypeStruct((S, W), MXU_DT), jax.ShapeDtypeStruct((S, LANES), f32), jax.ShapeDtypeStruct((S, W), MXU_DT)],
        scratch_shapes=[], sem=("arbitrary", "arbitrary"), args=(qn, qr, kn, kr, v, o, lse, do))


def _loss_head(y, target, *, name):
    S, D = y.shape
    tm = _tile(S, ROW_TILE, 8)

    def body(y_ref, t_ref, loss_ref, dy_ref):
        @pl.when(pl.program_id(0) == 0)
        def _():
            loss_ref[...] = jnp.zeros_like(loss_ref)

        e = y_ref[...] - t_ref[...]
        dy_ref[...] = e / D
        part = 0.5 * jnp.sum(jnp.mean(e * e, axis=1, keepdims=True), axis=0, keepdims=True)
        loss_ref[...] += jnp.broadcast_to(part, loss_ref.shape)

    rb = pl.BlockSpec((tm, D), lambda r: (r, 0))
    return pl.pallas_call(
        body, name=name, grid=(S // tm,), in_specs=[rb, rb],
        out_specs=[pl.BlockSpec((1, LANES), lambda r: (0, 0)), rb],
        out_shape=[jax.ShapeDtypeStruct((1, LANES), f32), jax.ShapeDtypeStruct((S, D), f32)],
        compiler_params=_cp(("arbitrary",)))(y, target)


def _adamw(w, g, m, v, *, name):
    R, C = w.shape
    tm = _tile(R, max(8, (1 << 19) // max(C, 1) // 8 * 8), 8)

    def body(w_ref, g_ref, m_ref, v_ref, d_ref, nm_ref, nv_ref):
        gv = g_ref[...]
        nm = ADAM_B1 * m_ref[...] + (1.0 - ADAM_B1) * gv
        nv = ADAM_B2 * v_ref[...] + (1.0 - ADAM_B2) * (gv * gv)
        m_hat = nm / (1.0 - ADAM_B1 ** ADAM_STEP)
        v_hat = nv / (1.0 - ADAM_B2 ** ADAM_STEP)
        d_ref[...] = -ADAM_LR * (m_hat / (jnp.sqrt(v_hat) + ADAM_EPS) + ADAM_WD * w_ref[...])
        nm_ref[...] = nm
        nv_ref[...] = nv

    rb = pl.BlockSpec((tm, C), lambda r: (r, 0))
    sh = jax.ShapeDtypeStruct((R, C), f32)
    return pl.pallas_call(body, name=name, grid=(R // tm,), in_specs=[rb] * 4, out_specs=[rb] * 3, out_shape=[sh] * 3,
                          compiler_params=_cp(("parallel",)))(w, g, m, v)


def _me():
    return lax.axis_index("x"), lax.axis_index("y"), lax.axis_index("c")


def _other_chips(mx, my):
    return [(1 - mx, my), (mx, 1 - my), (1 - mx, 1 - my)]


_ANY = pl.BlockSpec(memory_space=pl.ANY)


class _GatherChips:
    def __init__(self, xs):
        self.arrays = list(xs)
        n = len(xs)
        for x in xs:
            assert x.shape[0] % 2 == 0
        self.halves = [x.shape[0] // 2 for x in xs]
        self.out_shapes = [jax.ShapeDtypeStruct((4,) + x.shape, x.dtype) for x in xs]
        self.scratch = [pltpu.SemaphoreType.DMA((n, 6)), pltpu.SemaphoreType.DMA((n, 6))]

    def _sends(self, x_refs, o_refs, send, recv):
        mx, my, mc = _me()
        me = 2 * mx + my
        out = []
        for t, hf in enumerate(self.halves):
            mine = pl.ds(mc * hf, hf)
            for j, (cx, cy) in enumerate(_other_chips(mx, my)):
                out.append(pltpu.make_async_remote_copy(x_refs[t].at[mine], o_refs[t].at[me, mine], send.at[t, j], recv.at[t, j],
                                                        device_id=(cx, cy, mc), device_id_type=MESH))
        return out

    def start(self, x_refs, o_refs, scr):
        for cp in self._sends(x_refs, o_refs, *scr):
            cp.start()

    def finish(self, x_refs, o_refs, scr):
        send, recv = scr
        mx, my, mc = _me()
        chips = _other_chips(mx, my)
        fwd = []
        for t, hf in enumerate(self.halves):
            mine = pl.ds(mc * hf, hf)
            for j, (cx, cy) in enumerate(chips):
                k = 2 * cx + cy
                pltpu.make_async_remote_copy(x_refs[t].at[mine], o_refs[t].at[k, mine], send.at[t, j], recv.at[t, j],
                                             device_id=(cx, cy, mc), device_id_type=MESH).wait_recv()
                cp = pltpu.make_async_remote_copy(o_refs[t].at[k, mine], o_refs[t].at[k, mine], send.at[t, 3 + j], recv.at[t, 3 + j],
                                                  device_id=(mx, my, 1 - mc), device_id_type=MESH)
                cp.start()
                fwd.append(cp)
        for t, hf in enumerate(self.halves):
            theirs = pl.ds((1 - mc) * hf, hf)
            for j, (cx, cy) in enumerate(chips):
                k = 2 * cx + cy
                pltpu.make_async_remote_copy(o_refs[t].at[k, theirs], o_refs[t].at[k, theirs], send.at[t, 3 + j], recv.at[t, 3 + j],
                                             device_id=(mx, my, 1 - mc), device_id_type=MESH).wait_recv()
        for cp in self._sends(x_refs, o_refs, send, recv) + fwd:
            cp.wait_send()


class _ScatterChips:
    def __init__(self, ps):
        self.arrays = list(ps)
        n = len(ps)
        self.out_shapes = [jax.ShapeDtypeStruct((3,) + p.shape[1:], p.dtype) for p in ps]
        self.scratch = [pltpu.SemaphoreType.DMA((n, 3)), pltpu.SemaphoreType.DMA((n, 3))]

    def _copies(self, p_refs, o_refs, send, recv):
        mx, my, mc = _me()
        return [pltpu.make_async_remote_copy(p_refs[t].at[2 * cx + cy], o_refs[t].at[j], send.at[t, j], recv.at[t, j],
                                             device_id=(cx, cy, mc), device_id_type=MESH)
                for t in range(len(self.arrays)) for j, (cx, cy) in enumerate(_other_chips(mx, my))]

    def start(self, p_refs, o_refs, scr):
        for cp in self._copies(p_refs, o_refs, *scr):
            cp.start()

    def finish(self, p_refs, o_refs, scr):
        for cp in self._copies(p_refs, o_refs, *scr):
            cp.wait()


def _run_comm(comm, *, name):
    n = len(comm.arrays)

    def body(*refs):
        ins, outs, scr = refs[:n], refs[n:2 * n], refs[2 * n:]
        comm.start(ins, outs, scr)
        comm.finish(ins, outs, scr)

    return pl.pallas_call(body, name=name, in_specs=[_ANY] * n, out_specs=[_ANY] * n, out_shape=comm.out_shapes,
                          scratch_shapes=comm.scratch, compiler_params=pltpu.CompilerParams(has_side_effects=True))(*comm.arrays)


def _call(body, comm, *, name, grid, in_specs, out_specs, out_shape, scratch_shapes, sem, args):
    if comm is None:
        res = pl.pallas_call(body, name=name, grid=grid, in_specs=in_specs, out_specs=out_specs, out_shape=out_shape,
                             scratch_shapes=scratch_shapes, compiler_params=_cp(sem))(*args)
        return list(res), None
    n_in, n_out, n_scr, nc = len(in_specs), len(out_specs), len(scratch_shapes), len(comm.arrays)

    def wrapped(*refs):
        ins, cins = refs[:n_in], refs[n_in:n_in + nc]
        outs, couts = refs[n_in + nc:n_in + nc + n_out], refs[n_in + nc + n_out:n_in + 2 * nc + n_out]
        scr, cscr = refs[n_in + 2 * nc + n_out:n_in + 2 * nc + n_out + n_scr], refs[n_in + 2 * nc + n_out + n_scr:]
        ids = [pl.program_id(d) for d in range(len(grid))]
        first = functools.reduce(jnp.logical_and, [i == 0 for i in ids])
        last = functools.reduce(jnp.logical_and, [i == g - 1 for i, g in zip(ids, grid)])

        @pl.when(first)
        def _():
            comm.start(cins, couts, cscr)

        body(*ins, *outs, *scr)

        @pl.when(last)
        def _():
            comm.finish(cins, couts, cscr)

    res = pl.pallas_call(
        wrapped, name=name, grid=grid, in_specs=list(in_specs) + [_ANY] * nc, out_specs=list(out_specs) + [_ANY] * nc,
        out_shape=list(out_shape) + comm.out_shapes, scratch_shapes=list(scratch_shapes) + comm.scratch,
        compiler_params=_cp(("arbitrary",) * len(grid)))(*args, *comm.arrays)
    return list(res[:n_out]), list(res[n_out:])


class _PairSend:
    def __init__(self, gs):
        self.arrays = list(gs)
        n = len(gs)
        self.halves = [g.shape[1] // 2 for g in gs]
        self.out_shapes = [jax.ShapeDtypeStruct((4, g.shape[1] // 2, g.shape[2]), g.dtype) for g in gs]
        self.scratch = [pltpu.SemaphoreType.DMA((n, 4)), pltpu.SemaphoreType.DMA((n, 4))]

    def _copies(self, g_refs, o_refs, send, recv):
        mx, my, mc = _me()
        return [pltpu.make_async_remote_copy(g_refs[t].at[k, pl.ds((1 - mc) * hf, hf)], o_refs[t].at[k], send.at[t, k], recv.at[t, k],
                                             device_id=(mx, my, 1 - mc), device_id_type=MESH)
                for t, hf in enumerate(self.halves) for k in range(4)]

    def start(self, g_refs, o_refs, scr):
        for cp in self._copies(g_refs, o_refs, *scr):
            cp.start()

    def finish(self, g_refs, o_refs, scr):
        for cp in self._copies(g_refs, o_refs, *scr):
            cp.wait()


def _pair_exchange_halves(fs, *, name):
    n = len(fs)

    def body(*refs):
        f_refs, o_refs = refs[:n], refs[n:2 * n]
        send, recv = refs[2 * n:]
        mx, my, mc = _me()
        cps = []
        for t in range(n):
            hf = f_refs[t].shape[0]
            mine = pl.ds(mc * hf, hf)
            cp = pltpu.make_async_remote_copy(f_refs[t], o_refs[t].at[mine], send.at[t], recv.at[t],
                                              device_id=(mx, my, 1 - mc), device_id_type=MESH)
            cp.start()
            cps.append(cp)
        for t in range(n):
            hf = f_refs[t].shape[0]
            theirs = pl.ds((1 - mc) * hf, hf)
            cps[t].wait_send()
            pltpu.make_async_remote_copy(f_refs[t], o_refs[t].at[theirs], send.at[t], recv.at[t],
                                         device_id=(mx, my, 1 - mc), device_id_type=MESH).wait_recv()

    return pl.pallas_call(
        body, name=name, in_specs=[_ANY] * n, out_specs=[_ANY] * n,
        out_shape=[jax.ShapeDtypeStruct((2 * f.shape[0], f.shape[1]), f.dtype) for f in fs],
        scratch_shapes=[pltpu.SemaphoreType.DMA((n,)), pltpu.SemaphoreType.DMA((n,))],
        compiler_params=pltpu.CompilerParams(has_side_effects=True))(*fs)


def _allgather_all(x, *, name):
    def body(x_ref, o_ref, send, recv, lsem):
        mx, my, mc = _me()
        me = 4 * mx + 2 * my + mc
        local = pltpu.make_async_copy(x_ref, o_ref.at[me], lsem)
        local.start()
        cps = []
        for j in range(1, 8):
            px, py, pc = mx ^ (j >> 2), my ^ ((j >> 1) & 1), mc ^ (j & 1)
            cp = pltpu.make_async_remote_copy(x_ref, o_ref.at[me], send.at[j - 1], recv.at[j - 1],
                                              device_id=(px, py, pc), device_id_type=MESH)
            cp.start()
            cps.append(cp)
        for j in range(1, 8):
            px, py, pc = mx ^ (j >> 2), my ^ ((j >> 1) & 1), mc ^ (j & 1)
            pltpu.make_async_remote_copy(x_ref, o_ref.at[4 * px + 2 * py + pc], send.at[j - 1], recv.at[j - 1],
                                         device_id=(px, py, pc), device_id_type=MESH).wait_recv()
        for cp in cps:
            cp.wait_send()
        local.wait()

    return pl.pallas_call(
        body, name=name, in_specs=[_ANY], out_specs=_ANY, out_shape=jax.ShapeDtypeStruct((8,) + x.shape, x.dtype),
        scratch_shapes=[pltpu.SemaphoreType.DMA((7,)), pltpu.SemaphoreType.DMA((7,)), pltpu.SemaphoreType.DMA],
        compiler_params=pltpu.CompilerParams(has_side_effects=True))(x)


def _add_half(g4, recv, mc, *, name):
    _, R, C = g4.shape
    hf = R // 2
    tm = _tile(hf, max(16, (1 << 19) // C // 16 * 16), 16)
    nb = hf // tm

    def body(mc_ref, g_ref, r_ref, o_ref, ob_ref):
        s = g_ref[...] + r_ref[...]
        o_ref[...] = s
        ob_ref[...] = s.astype(COMM_DT)

    ospec = pl.BlockSpec((1, tm, C), lambda k, i, mc_ref: (k, i, 0))
    return pl.pallas_call(
        body, name=name,
        grid_spec=pltpu.PrefetchScalarGridSpec(
            num_scalar_prefetch=1, grid=(4, nb),
            in_specs=[pl.BlockSpec((1, tm, C), lambda k, i, mc_ref: (k, mc_ref[0] * nb + i, 0)),
                      pl.BlockSpec((1, tm, C), lambda k, i, mc_ref: (k, i, 0))],
            out_specs=[ospec, ospec]),
        out_shape=[jax.ShapeDtypeStruct((4, hf, C), f32), jax.ShapeDtypeStruct((4, hf, C), COMM_DT)],
        compiler_params=_cp(("parallel", "parallel")))(mc, g4, recv)


def _sum_chips(p4, recv3, me, *, name):
    _, Rh, C = p4.shape
    tm = _tile(Rh, max(16, (1 << 19) // C // 16 * 16), 16)

    def body(me_ref, p_ref, r_ref, o_ref):
        o_ref[...] = ((p_ref[0] + r_ref[0].astype(f32)) + r_ref[1].astype(f32)) + r_ref[2].astype(f32)

    return pl.pallas_call(
        body, name=name,
        grid_spec=pltpu.PrefetchScalarGridSpec(
            num_scalar_prefetch=1, grid=(Rh // tm,),
            in_specs=[pl.BlockSpec((1, tm, C), lambda i, me_ref: (me_ref[0], i, 0)),
                      pl.BlockSpec((3, tm, C), lambda i, me_ref: (0, i, 0))],
            out_specs=pl.BlockSpec((tm, C), lambda i, me_ref: (i, 0))),
        out_shape=jax.ShapeDtypeStruct((Rh, C), f32),
        compiler_params=_cp(("parallel",)))(me, p4, recv3)


def _sum8(x8, *, name):
    _, R, C = x8.shape
    tm = _tile(R, 64, 8)

    def body(x_ref, o_ref):
        acc = x_ref[0]
        for k in range(1, 8):
            acc = acc + x_ref[k]
        o_ref[...] = acc

    return pl.pallas_call(body, name=name, grid=(R // tm,), in_specs=[pl.BlockSpec((8, tm, C), lambda i: (0, i, 0))],
                          out_specs=pl.BlockSpec((tm, C), lambda i: (i, 0)), out_shape=jax.ShapeDtypeStruct((R, C), f32),
                          compiler_params=_cp(("parallel",)))(x8)


def _ssd_layer_fwd(h, W, tag, plan, i):
    z = _mm(h, W["wz"], name=tag + "_z")
    early = plan.fwd_early_comm(i)
    if early is None:
        xp = _mm(h, W["wxbc"], name=tag + "_xbc")
    else:
        xp, eres = _mm(h, W["wxbc"], name=tag + "_xbc", comm=early)
        plan.fwd_early_done(i, eres)
    dtr = _mm(h, W["wdt"], name=tag + "_dt")
    xbc = _conv_fwd(xp, W["conv_w"], W["conv_b"], name=tag + "_conv")
    (y, states), cres = _ssd_scan_fwd(xbc, dtr, W["dt_bias"], W["a_log"], W["d"], name=tag + "_scan", comm=plan.fwd_comm(i))
    DI = y.shape[1]
    G = SSD_N_GROUPS
    gs = DI // G
    (yn,) = _rowwise(_ssd_gate_fn, [(W["norm_w"], "c", gs)], [(y, "c", gs), (z, "c", gs)], [(DI, MXU_DT, "c", gs)],
                     name=tag + "_gate", ncol=G, tm=512)
    out = _mm(yn, W["wout"], name=tag + "_out")
    return out, dict(h=h, z=z, xp=xp, dtr=dtr, xbc=xbc, states=states, y=y, yn=yn), cres


def _carried_rowwise_bwd(plan, i, *a, **kw):
    early = plan.bwd_early_comm(i)
    if early is None:
        return _rowwise_bwd(*a, **kw)
    dp, di, cres = _rowwise_bwd(*a, comm=early, **kw)
    plan.bwd_early_done(i, cres)
    return dp, di


def _carried_mm(comm, done, i, *a, **kw):
    if comm is None:
        return _mm(*a, **kw)
    out, cres = _mm(*a, comm=comm, **kw)
    done(i, cres)
    return out


def _ssd_layer_bwd(sv, W, dr, drb, tag, plan, i):
    h = sv["h"]
    DI = sv["y"].shape[1]
    G = SSD_N_GROUPS
    gs = DI // G
    gr = {}
    dyn = _mm(drb, W["wout"], tb=True, name=tag + "_dyn")
    gr["wout"] = _mm(sv["yn"], drb, ta=True, name=tag + "_dwout")
    plan.early_grad(i, "ssd_out_w", gr["wout"])
    (dnw,), (dy, dz) = _carried_rowwise_bwd(plan, i, _ssd_gate_fn, [(W["norm_w"], "c", gs)], [(sv["y"], "c", gs), (sv["z"], "c", gs)],
                                             [(dyn, "c", gs)], name=tag + "_dgate", ncol=G, tm=512, din_dtypes=[(f32,), (MXU_DT,)])
    gr["norm_w"] = dnw
    (dxbc, ddtr, gr["dt_bias"], gr["a_log"], gr["d"]), cres = _ssd_scan_bwd(
        sv["xbc"], sv["dtr"], W["dt_bias"], W["a_log"], W["d"], sv["states"], dy, name=tag + "_dscan", comm=plan.bwd_comm(i))
    plan.bwd_done(i, cres)
    dxp, gr["conv_w"], gr["conv_b"] = _conv_bwd(
        sv["xp"], W["conv_w"], W["conv_b"], [dxbc], name=tag + "_dconv", dx_dtype=MXU_DT,
        dy_block=lambda c, cw: _ssd_grouped_block(c, G, gs, SSD_D_STATE, cw))
    gr["wz"] = _mm(h, dz, ta=True, name=tag + "_dwz")
    gr["wxbc"] = _mm(h, dxp, ta=True, name=tag + "_dwxbc")
    gr["wdt"] = _mm(h, ddtr, ta=True, name=tag + "_dwdt")
    dh = _carried_mm(plan.tail_early_comm(i, gr), plan.tail_early_done, i,
                     dz, W["wz"], tb=True, add=dr, add_scale=_alpha(), name=tag + "_dh1")
    dh = _carried_mm(plan.tail_comm(i), plan.tail_done, i, dxp, W["wxbc"], tb=True, add=dh, name=tag + "_dh2")
    dh = _mm(ddtr, W["wdt"], tb=True, add=dh, name=tag + "_dh3")
    return dh, gr


def _mla_layer_fwd(h, W, cos, sin, tag, comm=None):
    QR, KR = W["wqc"].shape[1], W["wkvc"].shape[1]
    HW = W["wqn"].shape[1]
    H = HW // LANES
    qc = _mm(h, W["wqc"], name=tag + "_qc")
    kvc = _mm(h, W["wkvc"], name=tag + "_kvc")
    krp = _mm(h, W["wkr"], name=tag + "_krp")
    z = _mm(h, W["wz"], name=tag + "_z")
    (qcn,) = _rowwise(_rms_fn, [(W["q_norm"], "a", QR)], [(qc, "a", QR)], [(QR, MXU_DT, "a", QR)], name=tag + "_qnorm")
    (kvn,) = _rowwise(_rms_fn, [(W["kv_norm"], "a", KR)], [(kvc, "a", KR)], [(KR, MXU_DT, "a", KR)], name=tag + "_kvnorm")
    qn = _mm(qcn, W["wqn"], name=tag + "_qn", out_dtype=MXU_DT)
    qrp = _mm(qcn, W["wqr"], name=tag + "_qrp")
    kn = _mm(kvn, W["wkn"], name=tag + "_kn", out_dtype=MXU_DT)
    v = _mm(kvn, W["wv"], name=tag + "_v", out_dtype=MXU_DT)
    (qr,) = _rowwise(_rope_fn, [], [(cos, "a", LANES), (sin, "a", LANES), (qrp, "c", LANES)], [(HW, MXU_DT, "c", LANES)],
                     name=tag + "_qrope", ncol=H, tm=1024)
    (kr,) = _rowwise(_rope_fn, [], [(cos, "a", LANES), (sin, "a", LANES), (krp, "a", LANES)], [(LANES, MXU_DT, "a", LANES)],
                     name=tag + "_krope")
    (o, lse), cres = _attn_fwd(qn, qr, kn, kr, v, name=tag + "_attn", comm=comm)
    (og,) = _rowwise(_mul_silu_fn, [], [(o, "a", HW), (z, "a", HW)], [(HW, MXU_DT, "a", HW)], name=tag + "_ogate")
    out = _mm(og, W["wout"], name=tag + "_out")
    return out, dict(h=h, qc=qc, kvc=kvc, z=z, qcn=qcn, kvn=kvn, qn=qn, qr=qr, kn=kn, kr=kr, v=v, o=o, lse=lse, og=og), cres


def _mla_layer_bwd(sv, W, cos, sin, dr, drb, tag, plan, i):
    h = sv["h"]
    QR, KR = W["wqc"].shape[1], W["wkvc"].shape[1]
    HW = W["wqn"].shape[1]
    H = HW // LANES
    gr = {}
    dog = _mm(drb, W["wout"], tb=True, name=tag + "_dog")
    gr["wout"] = _mm(sv["og"], drb, ta=True, name=tag + "_dwout")
    _, (do, dz) = _rowwise_bwd(_mul_silu_fn, [], [(sv["o"], "a", HW), (sv["z"], "a", HW)], [(dog, "a", HW)], name=tag + "_dogate",
                               din_dtypes=[(f32,), (MXU_DT,)])
    att = (sv["qn"], sv["qr"], sv["kn"], sv["kr"], sv["v"], sv["o"], sv["lse"], do)
    (dqn, dqr), cres = _attn_bwd_dq(*att, name=tag + "_dq", comm=plan.bwd_early_comm(i))
    plan.bwd_early_done(i, cres)
    (dkn, dkr, dv), cres = _attn_bwd_dkv(*att, name=tag + "_dkv", comm=plan.bwd_comm(i))
    plan.bwd_done(i, cres)
    _, (dqrp,) = _rowwise_bwd(_rope_fn, [], [(cos, "a", LANES), (sin, "a", LANES), (dqr, "c", LANES)], [(dqr, "c", LANES)],
                              name=tag + "_dqrope", ncol=H, tm=1024, diff_i=[2], din_dtypes=[(MXU_DT,)])
    _, (dkrp,) = _rowwise_bwd(_rope_fn, [], [(cos, "a", LANES), (sin, "a", LANES), (dkr, "a", LANES)], [(dkr, "a", LANES)],
                              name=tag + "_dkrope", diff_i=[2], din_dtypes=[(MXU_DT,)])
    dqcn = _mm(dqn, W["wqn"], tb=True, name=tag + "_dqcn1")
    dqcn = _mm(dqrp, W["wqr"], tb=True, add=dqcn, name=tag + "_dqcn2")
    dkvn = _mm(dkn, W["wkn"], tb=True, name=tag + "_dkvn1")
    dkvn = _mm(dv, W["wv"], tb=True, add=dkvn, name=tag + "_dkvn2")
    gr["wqn"] = _mm(sv["qcn"], dqn, ta=True, name=tag + "_dwqn")
    gr["wqr"] = _mm(sv["qcn"], dqrp, ta=True, name=tag + "_dwqr")
    gr["wkn"] = _mm(sv["kvn"], dkn, ta=True, name=tag + "_dwkn")
    gr["wv"] = _mm(sv["kvn"], dv, ta=True, name=tag + "_dwv")
    (gr["q_norm"],), (dqc,) = _rowwise_bwd(_rms_fn, [(W["q_norm"], "a", QR)], [(sv["qc"], "a", QR)], [(dqcn, "a", QR)], name=tag + "_dqnorm",
                                           din_dtypes=[(MXU_DT,)])
    (gr["kv_norm"],), (dkvc,) = _rowwise_bwd(_rms_fn, [(W["kv_norm"], "a", KR)], [(sv["kvc"], "a", KR)], [(dkvn, "a", KR)], name=tag + "_dkvnorm",
                                             din_dtypes=[(MXU_DT,)])
    dh = _mm(dz, W["wz"], tb=True, add=dr, add_scale=_alpha(), name=tag + "_dh1")
    dh = _mm(dqc, W["wqc"], tb=True, add=dh, name=tag + "_dh2")
    dh = _mm(dkvc, W["wkvc"], tb=True, add=dh, name=tag + "_dh3")
    dh = _mm(dkrp, W["wkr"], tb=True, add=dh, name=tag + "_dh4")
    gr["wz"] = _mm(h, dz, ta=True, name=tag + "_dwz")
    gr["wqc"] = _mm(h, dqc, ta=True, name=tag + "_dwqc")
    gr["wkvc"] = _mm(h, dkvc, ta=True, name=tag + "_dwkvc")
    gr["wkr"] = _mm(h, dkrp, ta=True, name=tag + "_dwkr")
    return dh, gr


def _gdn_layer_fwd(h, W, tag, comm=None):
    HK, HV, DK, DV = GDN_N_QK_HEADS, GDN_N_V_HEADS, GDN_DK, GDN_DV
    KD, VD = HK * DK, HV * DV
    qkvp = _mm(h, W["wqkv"], name=tag + "_qkv")
    z = _mm(h, W["wz"], name=tag + "_z")
    br = _mm(h, W["wb"], name=tag + "_b")
    ar = _mm(h, W["wa"], name=tag + "_a")
    qkv = _conv_fwd(qkvp, W["conv_w"], jnp.zeros((1, qkvp.shape[1]), f32), name=tag + "_conv")
    scale = jnp.concatenate([jnp.full((1, KD), DK ** -0.5, f32), jnp.ones((1, KD), f32)], axis=1)
    (qkn,) = _rowwise(_l2_fn, [(scale, "c", DK)], [(qkv, "c", DK)], [(2 * KD, f32, "c", DK)], name=tag + "_l2", ncol=2 * HK, tm=2048)
    (o, states, inverses), cres = _gdn_scan_fwd(qkn, qkv, br, ar, W["a_log"], W["dt_bias"], name=tag + "_scan", comm=comm)
    (on,) = _rowwise(_gdn_gate_fn, [(W["norm_w"], "a", DV)], [(o, "c", DV), (z, "c", DV)], [(VD, MXU_DT, "c", DV)],
                     name=tag + "_gate", ncol=HV, tm=1024)
    out = _mm(on, W["wout"], name=tag + "_out")
    return out, dict(h=h, qkvp=qkvp, z=z, br=br, ar=ar, qkv=qkv, qkn=qkn, o=o, states=states, inverses=inverses, on=on, scale=scale), cres


def _gdn_layer_bwd(sv, W, dr, drb, tag, plan, i):
    h = sv["h"]
    HK, HV, DK, DV = GDN_N_QK_HEADS, GDN_N_V_HEADS, GDN_DK, GDN_DV
    KD, VD = HK * DK, HV * DV
    gr = {}
    don = _mm(drb, W["wout"], tb=True, name=tag + "_don")
    gr["wout"] = _mm(sv["on"], drb, ta=True, name=tag + "_dwout")
    (gr["norm_w"],), (do, dz) = _carried_rowwise_bwd(plan, i, _gdn_gate_fn, [(W["norm_w"], "a", DV)], [(sv["o"], "c", DV), (sv["z"], "c", DV)],
                                                      [(don, "c", DV)], name=tag + "_dgate", ncol=HV, tm=1024, din_dtypes=[(f32,), (MXU_DT,)])
    (dq, dk, dv, dbr, dar, gr["a_log"], gr["dt_bias"]), cres = _gdn_scan_bwd(
        sv["qkn"], sv["qkv"], sv["br"], sv["ar"], W["a_log"], W["dt_bias"], sv["states"], sv["inverses"], do, name=tag + "_dscan",
        comm=plan.bwd_comm(i))
    plan.bwd_done(i, cres)
    _, (dqq,) = _rowwise_bwd(_l2_fn, [(sv["scale"], "c", DK)], [(sv["qkv"], "c", DK)], [(dq, "c", DK)],
                             name=tag + "_dl2q", ncol=HK, tm=2048, diff_p=[])
    _, (dqk,) = _rowwise_bwd(_l2_fn, [(sv["scale"], "c", DK, HK)], [(sv["qkv"], "c", DK, HK)], [(dk, "c", DK)],
                             name=tag + "_dl2k", ncol=HK, tm=2048, diff_p=[])
    dqkvp, gr["conv_w"], _ = _conv_bwd(sv["qkvp"], W["conv_w"], jnp.zeros((1, sv["qkvp"].shape[1]), f32), [dqq, dqk, dv],
                                       name=tag + "_dconv", dx_dtype=MXU_DT)
    dh = _mm(dz, W["wz"], tb=True, add=dr, add_scale=_alpha(), name=tag + "_dh1")
    dh = _mm(dqkvp, W["wqkv"], tb=True, add=dh, name=tag + "_dh2")
    dh = _mm(dbr, W["wb"], tb=True, add=dh, name=tag + "_dh3")
    dh = _mm(dar, W["wa"], tb=True, add=dh, name=tag + "_dh4")
    gr["wz"] = _mm(h, dz, ta=True, name=tag + "_dwz")
    gr["wqkv"] = _mm(h, dqkvp, ta=True, name=tag + "_dwqkv")
    gr["wb"] = _mm(h, dbr, ta=True, name=tag + "_dwb")
    gr["wa"] = _mm(h, dar, ta=True, name=tag + "_dwa")
    return dh, gr


def _rope_tables(positions):
    half = MLA_ROPE // 2
    inv_freq = ROPE_THETA ** (-jnp.arange(0, MLA_ROPE, 2, dtype=f32) / MLA_ROPE)
    ang = positions.astype(f32)[:, None] * inv_freq
    cos, sin = jnp.cos(ang), jnp.sin(ang)
    S = positions.shape[0]
    pad = jnp.zeros((S, LANES - 2 * half), f32)
    return jnp.concatenate([cos, cos, pad + 1.0], axis=1), jnp.concatenate([sin, sin, pad], axis=1)


class _LocalPlan:
    def __init__(self, LW):
        self.LW, self.grads = LW, [None] * DEPTH

    def weights(self, i):
        return self.LW[i]

    def fwd_early_comm(self, i):
        return None

    def fwd_early_done(self, i, res):
        pass

    def fwd_comm(self, i):
        return None

    def fwd_done(self, i, res):
        pass

    def early_grad(self, i, name, g):
        pass

    def tail_early_comm(self, i, gr):
        return None

    def tail_early_done(self, i, res):
        pass

    def tail_comm(self, i):
        return None

    def tail_done(self, i, res):
        pass

    def bwd_early_comm(self, i):
        return None

    def bwd_early_done(self, i, res):
        pass

    def bwd_comm(self, i):
        return None

    def bwd_done(self, i, res):
        pass

    def layer_grads(self, i, gr):
        self.grads[i] = gr


def _local_step(x, positions, target, ln_g, ln_b, plan):
    cos, sin = _rope_tables(positions)
    h, hb = x, x.astype(MXU_DT)
    saved, LW = [], []
    for i in range(DEPTH):
        kind, tag = i % 3, "l%d" % i
        LW.append(plan.weights(i))
        if kind == 0:
            y, sv, cres = _ssd_layer_fwd(hb, LW[i], tag, plan, i)
        elif kind == 1:
            y, sv, cres = _mla_layer_fwd(hb, LW[i], cos, sin, tag, plan.fwd_comm(i))
        else:
            y, sv, cres = _gdn_layer_fwd(hb, LW[i], tag, plan.fwd_comm(i))
        plan.fwd_done(i, cres)
        D = h.shape[1]
        r, h, hb = _rowwise(_res_ln_fn, [(ln_g[i], "a", D), (ln_b[i], "a", D)], [(h, "a", D), (y, "a", D)],
                            [(D, f32, "a", D), (D, f32, "a", D), (D, MXU_DT, "a", D)], name=tag + "_ln")
        sv["r"] = r
        saved.append(sv)
    loss, dh = _loss_head(h, target, name="loss_head")
    dg, db = [None] * DEPTH, [None] * DEPTH
    for i in reversed(range(DEPTH)):
        kind, tag = i % 3, "l%d" % i
        sv = saved[i]
        D = dh.shape[1]
        (dg[i], db[i]), (dr, drb) = _rowwise_bwd(_ln_fn, [(ln_g[i], "a", D), (ln_b[i], "a", D)], [(sv["r"], "a", D)], [(dh, "a", D)],
                                                 name=tag + "_dln", din_dtypes=[(f32, MXU_DT)])
        if kind == 0:
            dh, gr = _ssd_layer_bwd(sv, LW[i], dr, drb, tag, plan, i)
        elif kind == 1:
            dh, gr = _mla_layer_bwd(sv, LW[i], cos, sin, dr, drb, tag, plan, i)
        else:
            dh, gr = _gdn_layer_bwd(sv, LW[i], dr, drb, tag, plan, i)
        plan.layer_grads(i, gr)
    return loss, dh, dg, db


_WEIGHTS = ["ssd_in_w", "ssd_conv_w", "ssd_conv_b", "ssd_dt_bias", "ssd_a_log", "ssd_d", "ssd_norm_w", "ssd_out_w",
            "mla_in_w", "mla_q_norm_w", "mla_q_up_w", "mla_kv_norm_w", "mla_kv_up_w", "mla_out_w",
            "gdn_in_w", "gdn_conv_w", "gdn_a_log", "gdn_dt_bias", "gdn_norm_w", "gdn_out_w", "ln_g", "ln_b"]
_BIG = {"ssd_in_w": "col", "ssd_out_w": "row", "mla_in_w": "col", "mla_q_up_w": "col", "mla_kv_up_w": "col",
        "mla_out_w": "row", "gdn_in_w": "col", "gdn_out_w": "row"}
_SMALL_SHARDED = ["ssd_conv_w", "ssd_conv_b", "ssd_norm_w", "gdn_conv_w"]
_PACK_ROWS = 16


def _gathered_to_full(g, kind, nl):
    if kind == "col":
        _, RK, Ns = g.shape
        return g.reshape(4, nl, RK // nl, Ns).transpose(1, 2, 0, 3).reshape(nl, RK // nl, 4 * Ns)
    _, RK, N = g.shape
    return g.reshape(4, nl, RK // nl, N).transpose(1, 0, 2, 3).reshape(nl, 4 * (RK // nl), N)


def _full_to_slots(f, kind):
    nl, K, N = f.shape
    if kind == "col":
        return f.reshape(nl, K, 4, N // 4).transpose(2, 0, 1, 3).reshape(4, nl * K, N // 4)
    return f.reshape(nl, 4, K // 4, N).transpose(1, 0, 2, 3).reshape(4, nl * (K // 4), N)


def _pack(arrs):
    flat = jnp.concatenate([a.reshape(-1).astype(f32) for a in arrs])
    unit = _PACK_ROWS * LANES
    n = -(-flat.shape[0] // unit) * unit
    return jnp.pad(flat, (0, n - flat.shape[0])).reshape(_PACK_ROWS, n // _PACK_ROWS)


def _unpack(packed, shapes):
    flat = packed.reshape(-1)
    out, off = [], 0
    for sh in shapes:
        n = math.prod(sh)
        out.append(flat[off:off + n].reshape(sh))
        off += n
    return out


def _pad_lanes(a):
    return jnp.pad(a, [(0, 0)] * (a.ndim - 1) + [(0, LANES - a.shape[-1])])


_IN_PROJ = ("ssd_in_w", "mla_in_w", "gdn_in_w")


class _ColSlots:
    def __init__(self, slots):
        self.slots = slots
        self.shape = (slots.shape[1], 4 * slots.shape[2])

    def __getitem__(self, idx):
        _, cols = idx
        ns = self.slots.shape[2]
        a = cols.start or 0
        b = self.shape[1] if cols.stop is None else cols.stop
        parts = [self.slots[k][:, max(a, k * ns) - k * ns:min(b, (k + 1) * ns) - k * ns]
                 for k in range(4) if max(a, k * ns) < min(b, (k + 1) * ns)]
        return parts[0] if len(parts) == 1 else jnp.concatenate(parts, axis=1)


def _col_slots(pieces):
    widths = [p.shape[1] for p in pieces]
    ns = sum(widths) // 4
    slots = []
    for k in range(4):
        lo, hi, off, parts = k * ns, (k + 1) * ns, 0, []
        for p, wd in zip(pieces, widths):
            if max(lo, off) < min(hi, off + wd):
                parts.append(p[:, max(lo, off) - off:min(hi, off + wd) - off])
            off += wd
        slots.append(parts[0] if len(parts) == 1 else jnp.concatenate(parts, axis=1))
    return jnp.stack(slots)


def _layer_dict(i, full):
    G, N, P = SSD_N_GROUPS, SSD_D_STATE, SSD_HEAD_DIM
    kind, j = i % 3, i // 3
    if kind == 0:
        H = full["ssd_dt_bias"][j].shape[0]
        DI = H * P
        CD = DI + 2 * G * N
        win = full["ssd_in_w"][j]
        return dict(wz=win[:, :DI], wxbc=win[:, DI:DI + CD], wdt=win[:, DI + CD:], conv_w=full["ssd_conv_w"][j],
                    conv_b=full["ssd_conv_b"][j][None], dt_bias=full["ssd_dt_bias"][j][None], a_log=full["ssd_a_log"][j][None],
                    d=full["ssd_d"][j][None], norm_w=full["ssd_norm_w"][j][None], wout=full["ssd_out_w"][j])
    if kind == 1:
        QR, KR = MLA_Q_RANK, MLA_KV_RANK
        win = full["mla_in_w"][j]
        Hh = full["mla_q_up_w"][j].shape[1] // (MLA_NOPE + MLA_ROPE)
        qup = full["mla_q_up_w"][j].reshape(QR, Hh, MLA_NOPE + MLA_ROPE)
        kvup = full["mla_kv_up_w"][j].reshape(KR, Hh, MLA_NOPE + MLA_V)
        return dict(wqc=win[:, :QR], wkvc=win[:, QR:QR + KR], wkr=_pad_lanes(win[:, QR + KR:QR + KR + MLA_ROPE]),
                    wz=win[:, QR + KR + MLA_ROPE:], q_norm=full["mla_q_norm_w"][j][None], kv_norm=full["mla_kv_norm_w"][j][None],
                    wqn=qup[:, :, :MLA_NOPE].reshape(QR, Hh * MLA_NOPE), wqr=_pad_lanes(qup[:, :, MLA_NOPE:]).reshape(QR, Hh * LANES),
                    wkn=kvup[:, :, :MLA_NOPE].reshape(KR, Hh * MLA_NOPE), wv=kvup[:, :, MLA_NOPE:].reshape(KR, Hh * MLA_V),
                    wout=full["mla_out_w"][j])
    KD, VD, HV = GDN_N_QK_HEADS * GDN_DK, GDN_N_V_HEADS * GDN_DV, GDN_N_V_HEADS
    win = full["gdn_in_w"][j]
    c0, c1 = 2 * KD + VD, 2 * KD + 2 * VD
    return dict(wqkv=win[:, :c0], wz=win[:, c0:c1], wb=win[:, c1:c1 + HV], wa=win[:, c1 + HV:], conv_w=full["gdn_conv_w"][j],
                a_log=full["gdn_a_log"][j][None], dt_bias=full["gdn_dt_bias"][j][None], norm_w=full["gdn_norm_w"][j][None],
                wout=full["gdn_out_w"][j])


def _layer_weights(full, D):
    return [_layer_dict(i, full) for i in range(DEPTH)]


def _layer_full_grads(i, g, slots=False):
    kind = i % 3
    join = _col_slots if slots else (lambda pieces: jnp.concatenate(pieces, axis=1))
    if kind == 0:
        out = {"ssd_in_w": join([g["wz"], g["wxbc"], g["wdt"]]), "ssd_conv_w": g["conv_w"], "ssd_out_w": g["wout"]}
        for n in ("conv_b", "dt_bias", "a_log", "d", "norm_w"):
            out["ssd_" + n] = g[n][0]
        return out
    if kind == 1:
        QR, KR = g["wqn"].shape[0], g["wkn"].shape[0]
        Hh = g["wqn"].shape[1] // MLA_NOPE
        return {"mla_in_w": join([g["wqc"], g["wkvc"], g["wkr"][:, :MLA_ROPE], g["wz"]]),
                "mla_q_up_w": jnp.concatenate([g["wqn"].reshape(QR, Hh, MLA_NOPE), g["wqr"].reshape(QR, Hh, LANES)[:, :, :MLA_ROPE]],
                                              axis=2).reshape(QR, -1),
                "mla_kv_up_w": jnp.concatenate([g["wkn"].reshape(KR, Hh, MLA_NOPE), g["wv"].reshape(KR, Hh, MLA_V)], axis=2).reshape(KR, -1),
                "mla_q_norm_w": g["q_norm"][0], "mla_kv_norm_w": g["kv_norm"][0], "mla_out_w": g["wout"]}
    out = {"gdn_in_w": join([g["wqkv"], g["wz"], g["wb"], g["wa"]]), "gdn_conv_w": g["conv_w"], "gdn_out_w": g["wout"]}
    for n in ("a_log", "dt_bias", "norm_w"):
        out["gdn_" + n] = g[n][0]
    return out


def _full_grads(grads, dg, db):
    per = {n: [] for n in _WEIGHTS}
    for i in range(DEPTH):
        for n, a in _layer_full_grads(i, grads[i]).items():
            per[n].append(a)
        per["ln_g"].append(dg[i][0])
        per["ln_b"].append(db[i][0])
    return {n: jnp.stack(v) for n, v in per.items()}


class _DistPlan:
    def __init__(self, w, chip, core):
        self.w, self.chip = w, chip
        self.chip_arr = jnp.reshape(chip, (1,)).astype(jnp.int32)
        self.core_arr = jnp.reshape(core, (1,)).astype(jnp.int32)
        self.full = {n: {} for n in _BIG}
        self.gkeys, self.g4, self.p4, self.fin, self.small_grads = {}, {}, {}, {}, [None] * DEPTH
        keys = [("ssd_in_w", 0)]
        shards = self._shards(keys)
        got = _run_comm(_GatherChips(shards + [_pack([w[n] for n in _SMALL_SHARDED])]), name="gather_l0")
        self._fill(keys, shards, got[:1])
        small = lax.dynamic_update_slice(got[1], _pack([w[n] for n in _SMALL_SHARDED])[None], (chip, 0, 0))
        parts = [_unpack(small[k], [w[n].shape for n in _SMALL_SHARDED]) for k in range(4)]
        for t, n in enumerate(_SMALL_SHARDED):
            self.full[n] = jnp.concatenate([parts[k][t] for k in range(4)], axis=-1)
        for n in _WEIGHTS:
            if n not in self.full:
                self.full[n] = w[n]

    @staticmethod
    def keys(i):
        names = [["ssd_in_w", "ssd_out_w"], ["mla_in_w", "mla_q_up_w", "mla_kv_up_w", "mla_out_w"], ["gdn_in_w", "gdn_out_w"]][i % 3]
        return [(n, i // 3) for n in names]

    def _shards(self, keys):
        return [self.w[n][j].astype(MXU_DT) for n, j in keys]

    def _fill(self, keys, shards, got):
        for (n, j), s, g in zip(keys, shards, got):
            g = lax.dynamic_update_slice(g, s[None], (self.chip, 0, 0))
            self.full[n][j] = _ColSlots(g) if n in _IN_PROJ else _gathered_to_full(g, _BIG[n], 1)[0]

    def weights(self, i):
        if i == 0:
            self.full["ssd_out_w"][0] = None
        self._w = _layer_dict(i, self.full)
        return self._w

    def _start_gather(self, keys):
        self._pending = (keys, self._shards(keys))
        return _GatherChips(self._pending[1])

    def _end_gather(self, res):
        self._fill(self._pending[0], self._pending[1], res)

    def fwd_early_comm(self, i):
        return self._start_gather([("ssd_out_w", 0)]) if i == 0 else None

    def fwd_early_done(self, i, res):
        self._end_gather(res)
        self._w["wout"] = self.full["ssd_out_w"][0]

    def fwd_comm(self, i):
        return self._start_gather(self.keys(i + 1)) if i + 1 < DEPTH else None

    def fwd_done(self, i, res):
        if res is not None:
            self._end_gather(res)

    def _slots(self, n, g):
        return g if n in _IN_PROJ else _full_to_slots(g[None], _BIG[n])

    def early_grad(self, i, name, g):
        if i == 0:
            self.gkeys[1].append((name, 0))
            self.g4[1].append(self._slots(name, g))

    def _make_group(self, i, gr):
        fg = _layer_full_grads(i, gr, slots=True)
        self.small_grads[i] = {n: a for n, a in fg.items() if n not in _BIG}
        self.gkeys[i] = [k for k in self.keys(i) if not (i == 0 and k[0] == "ssd_out_w")]
        self.g4[i] = [self._slots(n, fg[n]) for n, _ in self.gkeys[i]]

    def layer_grads(self, i, gr):
        if i > 0:
            self._make_group(i, gr)

    def tail_early_comm(self, i, gr):
        if i > 0:
            return None
        self._make_group(0, gr)
        return _PairSend(self.g4[0])

    def tail_early_done(self, i, res):
        self._pair_add(0, res)

    def tail_comm(self, i):
        return self._scatter(0) if i == 0 else None

    def tail_done(self, i, res):
        self._sum(0, res)

    def _pair_add(self, g, r1):
        self.p4[g] = [_add_half(a, b, self.core_arr, name="grad_pair_add_g%d_%s%d" % (g, n, j))
                      for a, b, (n, j) in zip(self.g4[g], r1, self.gkeys[g])]

    def bwd_early_comm(self, i):
        return _PairSend(self.g4[i + 1]) if i + 1 < DEPTH else None

    def bwd_early_done(self, i, res):
        if res is not None:
            self._pair_add(i + 1, res)

    def _scatter(self, g):
        return _ScatterChips([p[1] for p in self.p4[g]])

    def _sum(self, g, r2):
        self.fin[g] = [_sum_chips(p[0], b, self.chip_arr, name="grad_chip_sum_g%d_%s%d" % (g, n, j))
                       for p, b, (n, j) in zip(self.p4[g], r2, self.gkeys[g])]

    def bwd_comm(self, i):
        return self._scatter(i + 1) if i + 1 < DEPTH else None

    def bwd_done(self, i, res):
        if res is not None:
            self._sum(i + 1, res)

    def grad_shards(self, core):
        order = [(g, t) for g in range(DEPTH) for t in range(len(self.gkeys[g]))]
        fins = [self.fin[g][t] for g, t in order]
        got = _pair_exchange_halves(fins, name="grad_pair_share")
        got = [lax.dynamic_update_slice(a, f, (core * f.shape[0], 0)) for a, f in zip(got, fins)]
        per = {n: {} for n in _BIG}
        for (g, t), a in zip(order, got):
            n, j = self.gkeys[g][t]
            per[n][j] = a
        return {n: (v[0] if len(v) == 1 else jnp.concatenate([v[j] for j in sorted(v)], axis=0)) for n, v in per.items()}


def kernel(x, positions, ssd_in_w, ssd_conv_w, ssd_conv_b, ssd_dt_bias, ssd_a_log, ssd_d, ssd_norm_w, ssd_out_w, mla_in_w, mla_q_norm_w, mla_q_up_w, mla_kv_norm_w, mla_kv_up_w, mla_out_w, gdn_in_w, gdn_conv_w, gdn_a_log, gdn_dt_bias, gdn_norm_w, gdn_out_w, ln_g, ln_b, loss_target, m_ssd_in_w, m_ssd_conv_w, m_ssd_conv_b, m_ssd_dt_bias, m_ssd_a_log, m_ssd_d, m_ssd_norm_w, m_ssd_out_w, m_mla_in_w, m_mla_q_norm_w, m_mla_q_up_w, m_mla_kv_norm_w, m_mla_kv_up_w, m_mla_out_w, m_gdn_in_w, m_gdn_conv_w, m_gdn_a_log, m_gdn_dt_bias, m_gdn_norm_w, m_gdn_out_w, m_ln_g, m_ln_b, v_ssd_in_w, v_ssd_conv_w, v_ssd_conv_b, v_ssd_dt_bias, v_ssd_a_log, v_ssd_d, v_ssd_norm_w, v_ssd_out_w, v_mla_in_w, v_mla_q_norm_w, v_mla_q_up_w, v_mla_kv_norm_w, v_mla_kv_up_w, v_mla_out_w, v_gdn_in_w, v_gdn_conv_w, v_gdn_a_log, v_gdn_dt_bias, v_gdn_norm_w, v_gdn_out_w, v_ln_g, v_ln_b):
    args = dict(locals())
    w = {n: args[n] for n in _WEIGHTS}
    mom = {n: args["m_" + n] for n in _WEIGHTS}
    vel = {n: args["v_" + n] for n in _WEIGHTS}
    mx, my, mc = _me()
    chip = 2 * mx + my
    small = [n for n in _WEIGHTS if n not in _BIG]
    big = list(_BIG)

    plan = _DistPlan(w, chip, mc)
    loss, gx, dg, db = _local_step(x[0], positions[0], loss_target[0], [plan.full["ln_g"][i][None] for i in range(DEPTH)],
                                   [plan.full["ln_b"][i][None] for i in range(DEPTH)], plan)
    per = {n: [] for n in small}
    for i in range(DEPTH):
        for n, a in plan.small_grads[i].items():
            per[n].append(a)
        per["ln_g"].append(dg[i][0])
        per["ln_b"].append(db[i][0])
    fg = {n: jnp.stack(v) for n, v in per.items()}
    gsh = plan.grad_shards(mc)

    out_g, out_d, out_m, out_v = {}, {}, {}, {}
    for n in big:
        g = gsh[n]
        sh = w[n].shape
        to2 = lambda a: a.reshape(-1, sh[-1])
        d_, m_, v_ = _adamw(to2(w[n]), g, to2(mom[n]), to2(vel[n]), name="adamw_" + n)
        out_g[n], out_d[n], out_m[n], out_v[n] = g.reshape(sh), d_.reshape(sh), m_.reshape(sh), v_.reshape(sh)

    summed = _sum8(_allgather_all(_pack([fg[n] for n in small] + [loss[0, :1]]), name="gather_small"), name="sum_small")
    sg = _unpack(summed, [fg[n].shape for n in small] + [(1,)])
    loss_total = sg[-1][0]
    gs = {}
    for n, g in zip(small, sg[:-1]):
        if n in _SMALL_SHARDED:
            ws = w[n].shape[-1]
            g = lax.dynamic_slice_in_dim(g, chip * ws, ws, axis=g.ndim - 1)
        gs[n] = g
    shapes = [w[n].shape for n in small]
    d_, m_, v_ = _adamw(_pack([w[n] for n in small]), _pack([gs[n] for n in small]), _pack([mom[n] for n in small]),
                        _pack([vel[n] for n in small]), name="adamw_small")
    for n, a, b, c in zip(small, _unpack(d_, shapes), _unpack(m_, shapes), _unpack(v_, shapes)):
        out_g[n], out_d[n], out_m[n], out_v[n] = gs[n], a, b, c

    return (loss_total, gx[None], *[out_g[n] for n in _WEIGHTS], *[out_d[n] for n in _WEIGHTS],
            *[out_m[n] for n in _WEIGHTS], *[out_v[n] for n in _WEIGHTS])
```

```python
import functools
import math

import jax
import jax.numpy as jnp
from jax import lax
from jax.experimental import pallas as pl
from jax.experimental.pallas import tpu as pltpu

f32 = jnp.float32
HI = lax.Precision.HIGHEST
MXU_DT = jnp.bfloat16
COMM_DT = jnp.bfloat16
MESH = pl.DeviceIdType.MESH

DEPTH = 4
LN_EPS = 1e-5
RMS_EPS = 1e-6
SSD_HEAD_DIM = 64
SSD_N_GROUPS = 8
SSD_D_STATE = 128
SSD_CONV = 4
SSD_CHUNK = 128
MLA_Q_RANK = 768
MLA_KV_RANK = 512
MLA_NOPE = 128
MLA_ROPE = 64
MLA_V = 128
ROPE_THETA = 10000.0
GDN_N_QK_HEADS = 16
GDN_N_V_HEADS = 32
GDN_DK = 128
GDN_DV = 128
GDN_CONV = 4
GDN_CHUNK = 64
ADAM_LR = 0.001
ADAM_B1 = 0.9
ADAM_B2 = 0.999
ADAM_EPS = 1e-08
ADAM_WD = 0.01
ADAM_STEP = 10

LANES = 128
VMEM_LIMIT = 48 * 1024 * 1024
ATT_TILE = 512
ATT_Q_TILE = 512
ATT_KEY_TILE = 512
ROW_TILE = 256
MM_TILE_M = 1024
MM_TILE_N = 1024
MM_TILE_K = 2048
MM_VMEM_BUDGET = 40 * 1024 * 1024
SSD_GROUPS_PER_STEP = 4
GDN_HEADS_PER_STEP = 16
GDN_FWD_HEADS_PER_STEP = 32


def _alpha():
    return (2.0 * DEPTH) ** 0.25


def _tile(n, pref, align=LANES):
    t = min(pref, n) // align * align
    while t >= align:
        if n % t == 0:
            return t
        t -= align
    return n


def _cp(sem=None):
    return pltpu.CompilerParams(dimension_semantics=sem, vmem_limit_bytes=VMEM_LIMIT)


def _iota(shape, dim):
    return lax.broadcasted_iota(jnp.int32, shape, dim)


def _div_pow2(x, p):
    assert p & (p - 1) == 0
    return lax.shift_right_logical(x, jnp.int32(p.bit_length() - 1))


def _dot(a, b, dims=((1,), (0,)), hi=False):
    if hi:
        return lax.dot_general(a.astype(f32), b.astype(f32), (dims, ((), ())), precision=HI, preferred_element_type=f32)
    return lax.dot_general(a.astype(MXU_DT), b.astype(MXU_DT), (dims, ((), ())), preferred_element_type=f32)


_NT = ((1,), (1,))
_TN = ((0,), (0,))


def _split3(x):
    hi = x.astype(jnp.bfloat16)
    r = x - hi.astype(f32)
    mid = r.astype(jnp.bfloat16)
    return hi, mid, (r - mid.astype(f32)).astype(jnp.bfloat16)


def _seldot_impl(a, b, dims, exact):
    def d(x, y):
        return lax.dot_general(x, y, (dims, ((), ())), preferred_element_type=f32)

    if exact == 0:
        a01 = a.astype(jnp.bfloat16)
        t = _split3(b.astype(f32))
        return (d(a01, t[0]) + d(a01, t[1])) + d(a01, t[2])
    b01 = b.astype(jnp.bfloat16)
    t = _split3(a.astype(f32))
    return (d(t[0], b01) + d(t[1], b01)) + d(t[2], b01)


@functools.partial(jax.custom_vjp, nondiff_argnums=(2, 3))
def _seldot(a, b, dims, exact):
    return _seldot_impl(a, b, dims, exact)


def _seldot_fwd(a, b, dims, exact):
    return _seldot_impl(a, b, dims, exact), (a, b)


def _seldot_bwd(dims, exact, res, dy):
    a, b = res
    (ca,), (cb,) = dims
    if exact == 0:
        assert ca == 1
        db = _seldot_impl(a, dy, _TN, 0) if cb == 0 else _seldot_impl(dy, a, _TN, 1)
        return jnp.zeros_like(a), db
    assert ca == 1 and cb == 0
    return _seldot_impl(dy, b, _NT, 1), jnp.zeros_like(b)


_seldot.defvjp(_seldot_fwd, _seldot_bwd)


def _softplus(x):
    return jnp.maximum(x, 0.0) + jnp.log1p(jnp.exp(-jnp.abs(x)))


def _silu(x):
    return x * jax.nn.sigmoid(x)


def _mm(a, b, *, name, ta=False, tb=False, add=None, add_scale=1.0, out_dtype=f32, comm=None):
    M, K = (a.shape[1], a.shape[0]) if ta else a.shape
    N = b.shape[0] if tb else b.shape[1]
    assert (b.shape[1] if tb else b.shape[0]) == K, (a.shape, b.shape, ta, tb)
    tm, tn, tk = _tile(M, MM_TILE_M), _tile(N, MM_TILE_N), _tile(K, MM_TILE_K)
    ab, bb = jnp.dtype(a.dtype).itemsize, jnp.dtype(b.dtype).itemsize
    while 2 * tk * (tm * ab + tn * bb) + 12 * tm * tn > MM_VMEM_BUDGET and tk % (2 * LANES) == 0:
        tk //= 2
    nk = K // tk
    a_spec = pl.BlockSpec((tk, tm), lambda i, j, k: (k, i)) if ta else pl.BlockSpec((tm, tk), lambda i, j, k: (i, k))
    b_spec = pl.BlockSpec((tn, tk), lambda i, j, k: (j, k)) if tb else pl.BlockSpec((tk, tn), lambda i, j, k: (k, j))
    o_spec = pl.BlockSpec((tm, tn), lambda i, j, k: (i, j))
    dims = ((0 if ta else 1,), (1 if tb else 0,))
    has_add = add is not None

    def body(*refs):
        a_ref, b_ref = refs[:2]
        add_ref = refs[2] if has_add else None
        o_ref = refs[3 if has_add else 2]

        def finish(r):
            if has_add:
                r = r + add_scale * add_ref[...].astype(f32)
            o_ref[...] = r.astype(out_dtype)

        if nk == 1:
            finish(_dot(a_ref[...], b_ref[...], dims))
            return
        acc = refs[-1]
        k = pl.program_id(2)

        @pl.when(k == 0)
        def _():
            acc[...] = jnp.zeros_like(acc)

        acc[...] += _dot(a_ref[...], b_ref[...], dims)

        @pl.when(k == nk - 1)
        def _():
            finish(acc[...])

    ins = [a, b] + ([add] if has_add else [])
    specs = [a_spec, b_spec] + ([o_spec] if has_add else [])
    (out,), cres = _call(
        body, comm, name=name, grid=(M // tm, N // tn, nk), in_specs=specs, out_specs=[o_spec],
        out_shape=[jax.ShapeDtypeStruct((M, N), out_dtype)], scratch_shapes=[pltpu.VMEM((tm, tn), f32)] if nk > 1 else [],
        sem=("parallel", "parallel", "arbitrary"), args=ins)
    return out if comm is None else (out, cres)


def _rw_specs(params, ins, ncol, tm):
    specs = []
    for arr, mode, bw, coff in params:
        if mode == "c":
            specs.append(pl.BlockSpec((1, bw), lambda c, r, coff=coff: (0, c + coff)))
        else:
            specs.append(pl.BlockSpec((1, bw), lambda c, r, coff=coff: (0, coff)))
    for arr, mode, bw, coff in ins:
        if mode == "c":
            specs.append(pl.BlockSpec((tm, bw), lambda c, r, coff=coff: (r, c + coff)))
        else:
            specs.append(pl.BlockSpec((tm, bw), lambda c, r, coff=coff: (r, coff)))
    return specs


def _norm_spec(lst):
    out = []
    for t in lst:
        arr, mode, bw = t[0], t[1], t[2]
        coff = t[3] if len(t) > 3 else 0
        out.append((arr, mode, bw, coff))
    return out


def _rowwise(fn, params, ins, outs, *, name, ncol=1, tm=None):
    params, ins = _norm_spec(params), _norm_spec(ins)
    S = ins[0][0].shape[0]
    tm = _tile(S, tm or ROW_TILE, 8)
    npar, nin = len(params), len(ins)

    def body(*refs):
        pv = [r[...].astype(f32) for r in refs[:npar]]
        iv = [r[...].astype(f32) for r in refs[npar:npar + nin]]
        res = fn(*pv, *iv)
        for o_ref, val in zip(refs[npar + nin:], res):
            o_ref[...] = val.astype(o_ref.dtype)

    out_specs, out_shapes = [], []
    for W, dt, mode, bw in outs:
        out_shapes.append(jax.ShapeDtypeStruct((S, W), dt))
        if mode == "c":
            out_specs.append(pl.BlockSpec((tm, bw), lambda c, r: (r, c)))
        else:
            out_specs.append(pl.BlockSpec((tm, bw), lambda c, r: (r, 0)))
    return pl.pallas_call(
        body, name=name, grid=(ncol, S // tm), in_specs=_rw_specs(params, ins, ncol, tm), out_specs=out_specs,
        out_shape=out_shapes, compiler_params=_cp(("parallel", "parallel")))(*[p[0] for p in params], *[i[0] for i in ins])


def _rowwise_bwd(fn, params, ins, couts, *, name, ncol=1, tm=None, diff_p=None, diff_i=None, din_dtypes=None, comm=None):
    params, ins, couts = _norm_spec(params), _norm_spec(ins), _norm_spec(couts)
    S = ins[0][0].shape[0]
    tm = _tile(S, tm or ROW_TILE, 8)
    npar, nin, nco = len(params), len(ins), len(couts)
    diff_p = list(range(npar)) if diff_p is None else diff_p
    diff_i = list(range(nin)) if diff_i is None else diff_i
    din_dtypes = [(f32,)] * len(diff_i) if din_dtypes is None else din_dtypes

    def body(*refs):
        c, r = pl.program_id(0), pl.program_id(1)
        pv = [x[...].astype(f32) for x in refs[:npar]]
        iv = [x[...].astype(f32) for x in refs[npar:npar + nin]]
        cv = [x[...].astype(f32) for x in refs[npar + nin:npar + nin + nco]]
        orefs = refs[npar + nin + nco:]

        def g(*dargs):
            p2, i2 = list(pv), list(iv)
            for n, k in enumerate(diff_p):
                p2[k] = dargs[n]
            for n, k in enumerate(diff_i):
                i2[k] = dargs[len(diff_p) + n]
            return tuple(fn(*p2, *i2))

        _, vjp = jax.vjp(g, *[pv[k] for k in diff_p], *[iv[k] for k in diff_i])
        grads = vjp(tuple(cv))
        for n, k in enumerate(diff_p):
            o_ref = orefs[n]
            first = (r == 0) if params[k][1] == "c" else jnp.logical_and(r == 0, c == 0)

            @pl.when(first)
            def _(o_ref=o_ref):
                o_ref[...] = jnp.zeros_like(o_ref)

            o_ref[...] += grads[n]
        pos = len(diff_p)
        for n, k in enumerate(diff_i):
            for _ in din_dtypes[n]:
                orefs[pos][...] = grads[len(diff_p) + n].astype(orefs[pos].dtype)
                pos += 1

    out_specs, out_shapes = [], []
    for k in diff_p:
        arr, mode, bw, coff = params[k]
        W = bw * ncol if mode == "c" else bw
        out_shapes.append(jax.ShapeDtypeStruct((1, W), f32))
        out_specs.append(pl.BlockSpec((1, bw), (lambda c, r: (0, c)) if mode == "c" else (lambda c, r: (0, 0))))
    for n, k in enumerate(diff_i):
        arr, mode, bw, coff = ins[k]
        W = bw * ncol if mode == "c" else bw
        for dt in din_dtypes[n]:
            out_shapes.append(jax.ShapeDtypeStruct((S, W), dt))
            out_specs.append(pl.BlockSpec((tm, bw), (lambda c, r: (r, c)) if mode == "c" else (lambda c, r: (r, 0))))
    res, cres = _call(
        body, comm, name=name, grid=(ncol, S // tm), in_specs=_rw_specs(params, ins + couts, ncol, tm), out_specs=out_specs,
        out_shape=out_shapes, scratch_shapes=[], sem=("arbitrary", "arbitrary"),
        args=(*[p[0] for p in params], *[i[0] for i in ins], *[c[0] for c in couts]))
    if comm is None:
        return list(res[:len(diff_p)]), list(res[len(diff_p):])
    return list(res[:len(diff_p)]), list(res[len(diff_p):]), cres


def _ln_fn(g, b, r):
    mu = jnp.mean(r, -1, keepdims=True)
    xc = r - mu
    var = jnp.mean(xc * xc, -1, keepdims=True)
    return (xc * lax.rsqrt(var + LN_EPS) * g + b,)


def _res_ln_fn(g, b, h, y):
    r = _alpha() * h + y
    hn = _ln_fn(g, b, r)
    return (r,) + hn + hn


def _rms_fn(w, x):
    return (x * lax.rsqrt(jnp.mean(x * x, -1, keepdims=True) + RMS_EPS) * w,)


def _ssd_gate_fn(w, y, z):
    yg = y * _silu(z)
    return (yg * lax.rsqrt(jnp.mean(yg * yg, -1, keepdims=True) + RMS_EPS) * w,)


def _mul_silu_fn(o, z):
    return (o * _silu(z),)


def _gdn_gate_fn(w, o, z):
    return (o * lax.rsqrt(jnp.mean(o * o, -1, keepdims=True) + RMS_EPS) * w * _silu(z),)


def _l2_fn(scale, x):
    return (x * lax.rsqrt(jnp.sum(x * x, -1, keepdims=True) + RMS_EPS) * scale,)


def _rope_fn(cos, sin, x):
    half = MLA_ROPE // 2
    i = _iota((LANES, LANES), 0)
    j = _iota((LANES, LANES), 1)
    pm = jnp.where((i == j + half) & (j < half), -1.0, 0.0) + jnp.where((i + half == j) & (j < 2 * half), 1.0, 0.0)
    return (x * cos + _seldot(x, pm.astype(f32), ((1,), (0,)), 1) * sin,)


def _conv_taps(x, K):
    S = x.shape[0]
    rows = _iota(x.shape, 0)
    return [x] + [jnp.where(rows < j, 0.0, pltpu.roll(x, j, 0)) for j in range(1, K)]


def _conv_fwd(x, w, b, *, name):
    S, C = x.shape
    K = w.shape[0]
    cw = _tile(C, LANES)

    def body(x_ref, w_ref, b_ref, o_ref):
        taps = _conv_taps(x_ref[...], K)
        wv = w_ref[...]
        pre = b_ref[...] + taps[0] * wv[K - 1:K, :]
        for j in range(1, K):
            pre = pre + taps[j] * wv[K - 1 - j:K - j, :]
        o_ref[...] = _silu(pre)

    return pl.pallas_call(
        body, name=name, grid=(C // cw,),
        in_specs=[pl.BlockSpec((S, cw), lambda c: (0, c)), pl.BlockSpec((K, cw), lambda c: (0, c)), pl.BlockSpec((1, cw), lambda c: (0, c))],
        out_specs=pl.BlockSpec((S, cw), lambda c: (0, c)), out_shape=jax.ShapeDtypeStruct((S, C), f32),
        compiler_params=_cp(("parallel",)))(x, w, b)


def _ssd_grouped_block(c, G, GW, N, cw):
    assert N == cw and GW % cw == 0
    nq = GW // cw
    per, nx = nq + 2, G * nq
    in_x = lax.div(c, nq) * per + lax.rem(c, nq)
    return jnp.where(c < nx, in_x, jnp.where(c < nx + G, (c - nx) * per + nq, (c - nx - G) * per + nq + 1))


def _conv_bwd(x, w, b, dys, *, name, dx_dtype=f32, dy_block=None):
    S, C = x.shape
    K = w.shape[0]
    cw = _tile(C, LANES)
    nblk = [d.shape[1] // cw for d in dys]
    offs = [sum(nblk[:p]) for p in range(len(dys))]
    assert sum(nblk) == C // cw and all(d.shape[1] % cw == 0 for d in dys)
    npc = len(dys)
    assert dy_block is None or npc == 1

    def body(x_ref, w_ref, b_ref, *refs):
        dy_refs, (dx_ref, dw_ref, db_ref) = refs[:npc], refs[npc:]
        if npc == 1:
            _conv_bwd_block(x_ref, w_ref, b_ref, dy_refs[0], dx_ref, dw_ref, db_ref, K, S, dx_dtype)
            return
        c = pl.program_id(0)
        for p in range(npc):
            @pl.when(jnp.logical_and(c >= offs[p], c < offs[p] + nblk[p]))
            def _(p=p):
                _conv_bwd_block(x_ref, w_ref, b_ref, dy_refs[p], dx_ref, dw_ref, db_ref, K, S, dx_dtype)

    col = lambda c: (0, c)
    if dy_block is not None:
        dy_specs = [pl.BlockSpec((S, cw), lambda c: (0, dy_block(c, cw)))]
    else:
        dy_specs = [pl.BlockSpec((S, cw), lambda c, o=offs[p], n=nblk[p]: (0, jnp.clip(c - o, 0, n - 1))) for p in range(npc)]
    return pl.pallas_call(
        body, name=name, grid=(C // cw,),
        in_specs=[pl.BlockSpec((S, cw), col), pl.BlockSpec((K, cw), col), pl.BlockSpec((1, cw), col)] + dy_specs,
        out_specs=[pl.BlockSpec((S, cw), col), pl.BlockSpec((K, cw), col), pl.BlockSpec((1, cw), col)],
        out_shape=[jax.ShapeDtypeStruct((S, C), dx_dtype), jax.ShapeDtypeStruct((K, C), f32), jax.ShapeDtypeStruct((1, C), f32)],
        compiler_params=_cp(("parallel",)))(x, w, b, *dys)


def _conv_bwd_block(x_ref, w_ref, b_ref, dy_ref, dx_ref, dw_ref, db_ref, K, S, dx_dtype):
    taps = _conv_taps(x_ref[...], K)
    wv = w_ref[...]
    pre = b_ref[...] + taps[0] * wv[K - 1:K, :]
    for j in range(1, K):
        pre = pre + taps[j] * wv[K - 1 - j:K - j, :]
    sg = jax.nn.sigmoid(pre)
    dpre = dy_ref[...] * sg * (1.0 + pre * (1.0 - sg))
    db_ref[...] = jnp.sum(dpre, axis=0, keepdims=True)
    rows = _iota(dpre.shape, 0)
    dx = dpre * wv[K - 1:K, :]
    dw_ref[K - 1:K, :] = jnp.sum(dpre * taps[0], axis=0, keepdims=True)
    for j in range(1, K):
        dw_ref[K - 1 - j:K - j, :] = jnp.sum(dpre * taps[j], axis=0, keepdims=True)
        up = jnp.where(rows >= S - j, 0.0, pltpu.roll(dpre, S - j, 0))
        dx = dx + up * wv[K - 1 - j:K - j, :]
    dx_ref[...] = dx.astype(dx_dtype)


def _ssd_chunk(prev, xs, Bm, Cm, dtr, dtb, alog, dsk, g, *, R, P):
    L, GW = xs.shape
    H = dtr.shape[1]
    tril = _iota((L, L), 0) >= _iota((L, L), 1)
    dt = _softplus(dtr + dtb)
    acs = _seldot(tril.astype(f32), dt * (-jnp.exp(alog)), ((1,), (0,)), 0)
    expand = (_iota((H, GW), 0) == g * R + _div_pow2(_iota((H, GW), 1), P)).astype(f32)
    dt_e = _seldot(dt, expand, ((1,), (0,)), 1)
    acs_e = _seldot(acs, expand, ((1,), (0,)), 1)
    d_e = jnp.sum(_seldot(jnp.broadcast_to(dsk, (8, H)), expand, ((1,), (0,)), 1), axis=0, keepdims=True) * 0.125
    last = jnp.sum(jnp.where(_iota((L, GW), 0) == L - 1, acs_e, 0.0), axis=0, keepdims=True)
    xdt = xs * dt_e
    cb = _dot(Cm, Bm, _NT)
    nsel = max(R, 8)
    sel = (_iota((nsel, H), 1) == g * R + _iota((nsel, H), 0)).astype(f32)
    acs_t = _seldot(sel, acs, _NT, 0)
    hp = LANES // P
    pieces = []
    for p in range(GW // LANES):
        xp = xdt[:, p * LANES:(p + 1) * LANES]
        acc = None
        for q in range(hp):
            r = p * hp + q
            col = jnp.sum(jnp.where(_iota((L, H), 1) == g * R + r, acs, 0.0), axis=1, keepdims=True)
            row = jnp.sum(jnp.where(_iota((nsel, L), 0) == r, acs_t, 0.0), axis=0, keepdims=True)
            dec = jnp.where(tril, jnp.exp(jnp.where(tril, col - row, 0.0)), 0.0)
            xm = jnp.where(_div_pow2(_iota((L, LANES), 1), P) == q, xp, 0.0)
            t = _dot(cb * dec, xm)
            acc = t if acc is None else acc + t
        pieces.append(acc)
    y_diag = pieces[0] if len(pieces) == 1 else jnp.concatenate(pieces, axis=1)
    st = _dot(Bm, xdt * jnp.exp(last - acs_e), _TN)
    y_off = _dot(Cm, prev) * jnp.exp(acs_e)
    new = prev * jnp.exp(last) + st
    return y_diag + y_off + xs * d_e, new


def _ssd_dims(xbc, dtr):
    S, CD = xbc.shape
    H = dtr.shape[1]
    G, N, P = SSD_N_GROUPS, SSD_D_STATE, SSD_HEAD_DIM
    DI = H * P
    R = H // G
    assert CD == DI + 2 * G * N and DI % N == 0
    return S, H, G, N, P, DI, R, R * P, SSD_CHUNK


def _ssd_groups_per_step(G, DI, N):
    GB = min(SSD_GROUPS_PER_STEP, G)
    assert G % GB == 0 and (DI // N) % GB == 0
    return GB


def _ssd_scan_fwd(xbc, dtr, dtb, alog, dsk, *, name, comm=None):
    S, H, G, N, P, DI, R, GW, L = _ssd_dims(xbc, dtr)
    nc = S // L
    GB = _ssd_groups_per_step(G, DI, N)
    boff, coff = DI // N // GB, (DI // N + G) // GB

    def body(xs_ref, b_ref, c_ref, dtr_ref, dtb_ref, alog_ref, dsk_ref, y_ref, st_ref, state):
        c, gb = pl.program_id(0), pl.program_id(1)
        for gg in range(GB):
            g = gb * GB + gg

            @pl.when(c == 0)
            def _(g=g):
                state[g] = jnp.zeros((N, GW), f32)

            prev = state[g]
            st_ref[0, gg] = prev
            y, new = _ssd_chunk(prev, xs_ref[:, gg * GW:(gg + 1) * GW], b_ref[:, gg * N:(gg + 1) * N], c_ref[:, gg * N:(gg + 1) * N],
                                dtr_ref[...], dtb_ref[...], alog_ref[...], dsk_ref[...], g, R=R, P=P)
            y_ref[:, gg * GW:(gg + 1) * GW] = y
            state[g] = new

    par = pl.BlockSpec((1, H), lambda c, g: (0, 0))
    return _call(
        body, comm, name=name, grid=(nc, G // GB),
        in_specs=[pl.BlockSpec((L, GB * GW), lambda c, g: (c, g)), pl.BlockSpec((L, GB * N), lambda c, g: (c, boff + g)),
                  pl.BlockSpec((L, GB * N), lambda c, g: (c, coff + g)), pl.BlockSpec((L, H), lambda c, g: (c, 0)), par, par, par],
        out_specs=[pl.BlockSpec((L, GB * GW), lambda c, g: (c, g)), pl.BlockSpec((1, GB, N, GW), lambda c, g: (c, g, 0, 0))],
        out_shape=[jax.ShapeDtypeStruct((S, DI), f32), jax.ShapeDtypeStruct((nc, G, N, GW), f32)],
        scratch_shapes=[pltpu.VMEM((G, N, GW), f32)],
        sem=("arbitrary", "arbitrary"), args=(xbc, xbc, xbc, dtr, dtb, alog, dsk))


def _ssd_scan_bwd(xbc, dtr, dtb, alog, dsk, states, dy, *, name, comm=None):
    S, H, G, N, P, DI, R, GW, L = _ssd_dims(xbc, dtr)
    nc = S // L
    GB = _ssd_groups_per_step(G, DI, N)
    boff, coff = DI // N // GB, (DI // N + G) // GB
    PW = GW + 2 * N

    def body(xs_ref, b_ref, c_ref, dtr_ref, dtb_ref, alog_ref, dsk_ref, st_ref, dy_ref,
             dg_ref, ddtr_ref, ddtb_ref, dalog_ref, ddsk_ref, dstate):
        c, gb = pl.program_id(0), pl.program_id(1)

        @pl.when(jnp.logical_and(c == 0, gb == 0))
        def _():
            ddtb_ref[...] = jnp.zeros_like(ddtb_ref)
            dalog_ref[...] = jnp.zeros_like(dalog_ref)
            ddsk_ref[...] = jnp.zeros_like(ddsk_ref)

        @pl.when(gb == 0)
        def _():
            ddtr_ref[...] = jnp.zeros_like(ddtr_ref)

        for gg in range(GB):
            g = gb * GB + gg

            @pl.when(c == 0)
            def _(g=g):
                dstate[g] = jnp.zeros((N, GW), f32)

            fn = functools.partial(_ssd_chunk, g=g, R=R, P=P)
            _, vjp = jax.vjp(fn, st_ref[0, gg], xs_ref[:, gg * GW:(gg + 1) * GW], b_ref[:, gg * N:(gg + 1) * N],
                             c_ref[:, gg * N:(gg + 1) * N], dtr_ref[...], dtb_ref[...], alog_ref[...], dsk_ref[...])
            dprev, dxs, dB, dC, ddtr, ddtb, dalog, ddsk = vjp((dy_ref[:, gg * GW:(gg + 1) * GW], dstate[g]))
            dstate[g] = dprev
            dg_ref[:, gg * PW:gg * PW + GW] = dxs
            dg_ref[:, gg * PW + GW:gg * PW + GW + N] = dB
            dg_ref[:, gg * PW + GW + N:(gg + 1) * PW] = dC
            ddtr_ref[...] += ddtr
            ddtb_ref[...] += ddtb
            dalog_ref[...] += dalog
            ddsk_ref[...] += ddsk

    rc = lambda c: nc - 1 - c
    par = pl.BlockSpec((1, H), lambda c, g: (0, 0))
    return _call(
        body, comm, name=name, grid=(nc, G // GB),
        in_specs=[pl.BlockSpec((L, GB * GW), lambda c, g: (rc(c), g)), pl.BlockSpec((L, GB * N), lambda c, g: (rc(c), boff + g)),
                  pl.BlockSpec((L, GB * N), lambda c, g: (rc(c), coff + g)), pl.BlockSpec((L, H), lambda c, g: (rc(c), 0)),
                  par, par, par, pl.BlockSpec((1, GB, N, GW), lambda c, g: (rc(c), g, 0, 0)),
                  pl.BlockSpec((L, GB * GW), lambda c, g: (rc(c), g))],
        out_specs=[pl.BlockSpec((L, GB * PW), lambda c, g: (rc(c), g)), pl.BlockSpec((L, H), lambda c, g: (rc(c), 0)), par, par, par],
        out_shape=[jax.ShapeDtypeStruct((S, G * (GW + 2 * N)), f32), jax.ShapeDtypeStruct((S, H), f32)] + [jax.ShapeDtypeStruct((1, H), f32)] * 3,
        scratch_shapes=[pltpu.VMEM((G, N, GW), f32)],
        sem=("arbitrary", "arbitrary"), args=(xbc, xbc, xbc, dtr, dtb, alog, dsk, states, dy))


def _dot3(a, b, dims=((1,), (0,))):
    def split(x):
        hi = x.astype(jnp.bfloat16)
        return hi, (x - hi.astype(f32)).astype(jnp.bfloat16)

    def d(x, y):
        return lax.dot_general(x, y, (dims, ((), ())), preferred_element_type=f32)

    ah, al = split(a)
    bh, bl = split(b)
    return d(ah, bh) + (d(ah, bl) + d(al, bh))


def _neumann_inverses(As):
    L = As[0].shape[0]
    eye = (_iota((L, L), 0) == _iota((L, L), 1)).astype(f32)
    X = [-A for A in As]
    P = [eye + x for x in X]
    n = 1
    while 2 * n < L:
        X = [_dot3(x, x) for x in X]
        P = [p + _dot3(p, x) for p, x in zip(P, X)]
        n *= 2
    return P


@jax.custom_vjp
def _unit_lower_solves(Ts, As, Rs):
    return tuple(_dot3(T, R) for T, R in zip(Ts, Rs))


def _uls_fwd(Ts, As, Rs):
    Xs = tuple(_dot3(T, R) for T, R in zip(Ts, Rs))
    return Xs, (Ts, Xs)


def _uls_bwd(res, dXs):
    Ts, Xs = res
    dRs = tuple(_dot3(T, dX, _TN) for T, dX in zip(Ts, dXs))
    dAs = tuple(-_dot3(dR, X, _NT) for dR, X in zip(dRs, Xs))
    return tuple(jnp.zeros_like(T) for T in Ts), dAs, dRs


_unit_lower_solves.defvjp(_uls_fwd, _uls_bwd)


def _gdn_step(states, qb, kb_, vb, br, ar, alog, dtb, h0, *, rep, inverses=None):
    HB = len(states)
    L = qb.shape[0]
    DK, DV = states[0].shape
    HV = br.shape[1]
    incl = _iota((L, L), 0) >= _iota((L, L), 1)
    strict = _iota((L, L), 0) > _iota((L, L), 1)
    lane = _iota((L, HV), 1)
    g_all = -jnp.exp(alog) * _softplus(ar + dtb)
    gcs = _seldot(incl.astype(f32), g_all, ((1,), (0,)), 0)
    beta_all = jax.nn.sigmoid(br)
    nsel = max(HV, 8)
    gcs_t = _seldot((_iota((nsel, HV), 0) == _iota((nsel, HV), 1)).astype(f32), gcs, _NT, 0)
    hs = range(HB)
    q = [qb[:, (hh // rep) * DK:(hh // rep + 1) * DK] for hh in hs]
    k = [kb_[:, (hh // rep) * DK:(hh // rep + 1) * DK] for hh in hs]
    v = [vb[:, hh * DV:(hh + 1) * DV] for hh in hs]
    gc = [jnp.sum(jnp.where(lane == h0 + hh, gcs, 0.0), axis=1, keepdims=True) for hh in hs]
    beta = [jnp.sum(jnp.where(lane == h0 + hh, beta_all, 0.0), axis=1, keepdims=True) for hh in hs]
    gc_row = [jnp.sum(jnp.where(_iota((nsel, L), 0) == h0 + hh, gcs_t, 0.0), axis=0, keepdims=True) for hh in hs]
    decay = [jnp.where(incl, jnp.exp(jnp.where(incl, gc[hh] - gc_row[hh], 0.0)), 0.0) for hh in hs]
    kbeta = [k[hh] * beta[hh] for hh in hs]
    a_mat = [jnp.where(strict, _dot(kbeta[hh], k[hh], _NT) * decay[hh], 0.0) for hh in hs]
    eg = [jnp.exp(gc[hh]) for hh in hs]
    rhs = tuple(jnp.concatenate([v[hh] * beta[hh], kbeta[hh] * eg[hh]], axis=1) for hh in hs)
    if inverses is None:
        made = tuple(_neumann_inverses(a_mat))
        sol = tuple(_dot3(T, R) for T, R in zip(made, rhs))
    else:
        sol = _unit_lower_solves(tuple(inverses), tuple(a_mat), rhs)
    qk = [jnp.where(incl, _dot(q[hh], k[hh], _NT) * decay[hh], 0.0) for hh in hs]
    g_last = [jnp.sum(jnp.where(_iota((L, 1), 0) == L - 1, gc[hh], 0.0), axis=0, keepdims=True) for hh in hs]
    v_new = [sol[hh][:, :DV] - _dot(sol[hh][:, DV:], states[hh]) for hh in hs]
    outs = [_dot(q[hh] * eg[hh], states[hh]) + _dot(qk[hh], v_new[hh]) for hh in hs]
    news = [states[hh] * jnp.exp(g_last[hh]) + _dot(k[hh] * jnp.exp(g_last[hh] - gc[hh]), v_new[hh], _TN) for hh in hs]
    o = outs[0] if HB == 1 else jnp.concatenate(outs, axis=1)
    return (o, tuple(news), made) if inverses is None else (o, tuple(news))


def _gdn_dims(heads_per_step):
    HK, HV = GDN_N_QK_HEADS, GDN_N_V_HEADS
    rep = HV // HK
    HB = min(heads_per_step, HV)
    assert HV % HB == 0 and HB % rep == 0
    return HK, HV, GDN_DK, GDN_DV, GDN_CHUNK, rep, HB


def _gdn_scan_fwd(qkn, qkv, br, ar, alog, dtb, *, name, comm=None):
    S = qkn.shape[0]
    HK, HV, DK, DV, L, rep, HB = _gdn_dims(GDN_FWD_HEADS_PER_STEP)
    nc = S // L
    QW = HB // rep * DK
    koff = HK * DK // QW
    voff = 2 * HK * DK // (HB * DV)

    def body(q_ref, k_ref, v_ref, br_ref, ar_ref, alog_ref, dtb_ref, o_ref, st_ref, inv_ref, state):
        c, hb = pl.program_id(0), pl.program_id(1)
        h0 = hb * HB

        @pl.when(c == 0)
        def _():
            for hh in range(HB):
                state[h0 + hh] = jnp.zeros((DK, DV), f32)

        prev = tuple(state[h0 + hh] for hh in range(HB))
        for hh in range(HB):
            st_ref[0, hh] = prev[hh]
        o, new, inv = _gdn_step(prev, q_ref[...], k_ref[...], v_ref[...], br_ref[...], ar_ref[...], alog_ref[...], dtb_ref[...],
                                h0, rep=rep)
        o_ref[...] = o
        for hh in range(HB):
            state[h0 + hh] = new[hh]
            inv_ref[0, hh] = inv[hh]

    par = pl.BlockSpec((1, HV), lambda c, h: (0, 0))
    return _call(
        body, comm, name=name, grid=(nc, HV // HB),
        in_specs=[pl.BlockSpec((L, QW), lambda c, h: (c, h)), pl.BlockSpec((L, QW), lambda c, h: (c, koff + h)),
                  pl.BlockSpec((L, HB * DV), lambda c, h: (c, voff + h)), pl.BlockSpec((L, HV), lambda c, h: (c, 0)),
                  pl.BlockSpec((L, HV), lambda c, h: (c, 0)), par, par],
        out_specs=[pl.BlockSpec((L, HB * DV), lambda c, h: (c, h)), pl.BlockSpec((1, HB, DK, DV), lambda c, h: (c, h, 0, 0)),
                   pl.BlockSpec((1, HB, L, L), lambda c, h: (c, h, 0, 0))],
        out_shape=[jax.ShapeDtypeStruct((S, HV * DV), f32), jax.ShapeDtypeStruct((nc, HV, DK, DV), f32),
                   jax.ShapeDtypeStruct((nc, HV, L, L), f32)],
        scratch_shapes=[pltpu.VMEM((HV, DK, DV), f32)],
        sem=("arbitrary", "arbitrary"), args=(qkn, qkn, qkv, br, ar, alog, dtb))


def _gdn_scan_bwd(qkn, qkv, br, ar, alog, dtb, states, inverses, do, *, name, comm=None):
    S = qkn.shape[0]
    HK, HV, DK, DV, L, rep, HB = _gdn_dims(GDN_HEADS_PER_STEP)
    nc = S // L
    QW = HB // rep * DK
    koff = HK * DK // QW
    voff = 2 * HK * DK // (HB * DV)

    def body(q_ref, k_ref, v_ref, br_ref, ar_ref, alog_ref, dtb_ref, st_ref, inv_ref, do_ref,
             dq_ref, dk_ref, dv_ref, dbr_ref, dar_ref, dalog_ref, ddtb_ref, dstate):
        c, hb = pl.program_id(0), pl.program_id(1)
        h0 = hb * HB

        @pl.when(c == 0)
        def _():
            for hh in range(HB):
                dstate[h0 + hh] = jnp.zeros((DK, DV), f32)

        @pl.when(jnp.logical_and(c == 0, hb == 0))
        def _():
            dalog_ref[...] = jnp.zeros_like(dalog_ref)
            ddtb_ref[...] = jnp.zeros_like(ddtb_ref)

        @pl.when(hb == 0)
        def _():
            dbr_ref[...] = jnp.zeros_like(dbr_ref)
            dar_ref[...] = jnp.zeros_like(dar_ref)

        fn = functools.partial(_gdn_step, h0=h0, rep=rep, inverses=tuple(inv_ref[0, hh] for hh in range(HB)))
        prev = tuple(st_ref[0, hh] for hh in range(HB))
        _, vjp = jax.vjp(fn, prev, q_ref[...], k_ref[...], v_ref[...], br_ref[...], ar_ref[...], alog_ref[...], dtb_ref[...])
        dprev, dq, dk, dv, dbr, dar, dalog, ddtb = vjp((do_ref[...], tuple(dstate[h0 + hh] for hh in range(HB))))
        for hh in range(HB):
            dstate[h0 + hh] = dprev[hh]
        dq_ref[...] = dq
        dk_ref[...] = dk
        dv_ref[...] = dv
        dbr_ref[...] += dbr
        dar_ref[...] += dar
        dalog_ref[...] += dalog
        ddtb_ref[...] += ddtb

    rc = lambda c: nc - 1 - c
    par = pl.BlockSpec((1, HV), lambda c, h: (0, 0))
    blk = lambda W: pl.BlockSpec((L, W), lambda c, h: (rc(c), h))
    return _call(
        body, comm, name=name, grid=(nc, HV // HB),
        in_specs=[pl.BlockSpec((L, QW), lambda c, h: (rc(c), h)), pl.BlockSpec((L, QW), lambda c, h: (rc(c), koff + h)),
                  pl.BlockSpec((L, HB * DV), lambda c, h: (rc(c), voff + h)), pl.BlockSpec((L, HV), lambda c, h: (rc(c), 0)),
                  pl.BlockSpec((L, HV), lambda c, h: (rc(c), 0)), par, par,
                  pl.BlockSpec((1, HB, DK, DV), lambda c, h: (rc(c), h, 0, 0)),
                  pl.BlockSpec((1, HB, L, L), lambda c, h: (rc(c), h, 0, 0)), blk(HB * DV)],
        out_specs=[blk(QW), blk(QW), blk(HB * DV), pl.BlockSpec((L, HV), lambda c, h: (rc(c), 0)),
                   pl.BlockSpec((L, HV), lambda c, h: (rc(c), 0)), par, par],
        out_shape=[jax.ShapeDtypeStruct((S, HK * DK), f32), jax.ShapeDtypeStruct((S, HK * DK), f32), jax.ShapeDtypeStruct((S, HV * DV), f32),
                   jax.ShapeDtypeStruct((S, HV), f32), jax.ShapeDtypeStruct((S, HV), f32),
                   jax.ShapeDtypeStruct((1, HV), f32), jax.ShapeDtypeStruct((1, HV), f32)],
        scratch_shapes=[pltpu.VMEM((HV, DK, DV), f32)],
        sem=("arbitrary", "arbitrary"), args=(qkn, qkn, qkv, br, ar, alog, dtb, states, inverses, do))


def _att_scale():
    return (MLA_NOPE + MLA_ROPE) ** -0.5


def _causal(s, i, j, t, tk=None):
    qpos = i * t + _iota(s.shape, 0)
    kpos = j * (t if tk is None else tk) + _iota(s.shape, 1)
    return kpos <= qpos


def _attn_fwd(qn, qr, kn, kr, v, *, name, comm=None):
    S, W = qn.shape
    H = W // LANES
    t = _tile(S, ATT_Q_TILE)
    tk = _tile(S, ATT_KEY_TILE)
    assert tk % t == 0
    scale = _att_scale()

    def body(qn_ref, qr_ref, kn_ref, kr_ref, v_ref, o_ref, lse_ref):
        i = pl.program_id(1)
        qc = jnp.concatenate([qn_ref[...], qr_ref[...]], axis=1)

        def step(j, carry, masked):
            m, l, acc = carry
            rows = pl.ds(pl.multiple_of(j * tk, tk), tk)
            s = _dot(qc, jnp.concatenate([kn_ref[rows, :], kr_ref[rows, :]], axis=1), _NT) * scale
            if masked:
                s = jnp.where(_causal(s, i, j, t, tk), s, -1e30)
            m_new = jnp.maximum(m, jnp.max(s, axis=1, keepdims=True))
            p = jnp.exp(s - m_new)
            a = jnp.exp(m - m_new)
            return m_new, a * l + jnp.sum(p, axis=1, keepdims=True), a * acc + _dot(p, v_ref[rows, :])

        nfull = lax.div(i * t, tk)
        carry = lax.fori_loop(0, nfull, functools.partial(step, masked=False),
                              (jnp.full((t, 1), -1e30, f32), jnp.zeros((t, 1), f32), jnp.zeros((t, LANES), f32)))
        m, l, acc = step(nfull, carry, True)
        o_ref[...] = acc / l
        lse_ref[...] = jnp.broadcast_to(m + jnp.log(l), (t, LANES))

    qb = pl.BlockSpec((t, LANES), lambda h, i: (i, h))
    kb = pl.BlockSpec((S, LANES), lambda h, i: (0, h))
    return _call(
        body, comm, name=name, grid=(H, S // t),
        in_specs=[qb, qb, kb, pl.BlockSpec((S, LANES), lambda h, i: (0, 0)), kb],
        out_specs=[qb, qb], out_shape=[jax.ShapeDtypeStruct((S, W), f32), jax.ShapeDtypeStruct((S, W), f32)],
        scratch_shapes=[], sem=("parallel", "arbitrary"), args=(qn, qr, kn, kr, v))


def _attn_bwd_dq(qn, qr, kn, kr, v, o, lse, do, *, name, comm=None):
    S, W = qn.shape
    H = W // LANES
    t = _tile(S, ATT_TILE)
    scale = _att_scale()

    def body(qn_ref, qr_ref, kn_ref, kr_ref, v_ref, o_ref, lse_ref, do_ref, dqn_ref, dqr_ref):
        i = pl.program_id(1)
        qc = jnp.concatenate([qn_ref[...], qr_ref[...]], axis=1)
        dov = do_ref[...]
        delta = jnp.sum(dov * o_ref[...], axis=1, keepdims=True)
        lsev = lse_ref[...][:, :1]

        def step(j, dq, masked):
            rows = pl.ds(pl.multiple_of(j * t, t), t)
            kc = jnp.concatenate([kn_ref[rows, :], kr_ref[rows, :]], axis=1)
            s = _dot(qc, kc, _NT) * scale
            p = jnp.exp(s - lsev)
            if masked:
                p = jnp.where(_causal(s, i, j, t), p, 0.0)
            ds = p * (_dot(dov, v_ref[rows, :], _NT) - delta)
            return dq + _dot(ds, kc)

        dq = lax.fori_loop(0, i, functools.partial(step, masked=False), jnp.zeros((t, 2 * LANES), f32))
        dq = step(i, dq, True)
        dq = dq * scale
        dqn_ref[...] = dq[:, :LANES].astype(MXU_DT)
        dqr_ref[...] = dq[:, LANES:]

    qb = pl.BlockSpec((t, LANES), lambda h, i: (i, h))
    kb = pl.BlockSpec((S, LANES), lambda h, i: (0, h))
    return _call(
        body, comm, name=name, grid=(H, S // t),
        in_specs=[qb, qb, kb, pl.BlockSpec((S, LANES), lambda h, i: (0, 0)), kb, qb, qb, qb],
        out_specs=[qb, qb], out_shape=[jax.ShapeDtypeStruct((S, W), MXU_DT), jax.ShapeDtypeStruct((S, W), f32)],
        scratch_shapes=[], sem=("parallel", "arbitrary"), args=(qn, qr, kn, kr, v, o, lse, do))


def _attn_bwd_dkv(qn, qr, kn, kr, v, o, lse, do, *, name, comm=None):
    S, W = qn.shape
    H = W // LANES
    t = _tile(S, ATT_TILE)
    nb = S // t
    scale = _att_scale()

    def body(qn_ref, qr_ref, kn_ref, kr_ref, v_ref, o_ref, lse_ref, do_ref, dkn_ref, dkr_ref, dv_ref):
        h, j = pl.program_id(0), pl.program_id(1)
        kc = jnp.concatenate([kn_ref[...], kr_ref[...]], axis=1)
        vv = v_ref[...]

        def step(i, carry, masked):
            dk, dv = carry
            rows = pl.ds(pl.multiple_of(i * t, t), t)
            qc = jnp.concatenate([qn_ref[rows, :], qr_ref[rows, :]], axis=1)
            dov = do_ref[rows, :]
            delta = jnp.sum(dov * o_ref[rows, :], axis=1, keepdims=True)
            s = _dot(qc, kc, _NT) * scale
            p = jnp.exp(s - lse_ref[rows, :][:, :1])
            if masked:
                p = jnp.where(_causal(s, i, j, t), p, 0.0)
            ds = p * (_dot(dov, vv, _NT) - delta)
            return dk + _dot(ds, qc, _TN), dv + _dot(p, dov, _TN)

        carry = step(j, (jnp.zeros((t, 2 * LANES), f32), jnp.zeros((t, LANES), f32)), True)
        dk, dv = lax.fori_loop(j + 1, nb, functools.partial(step, masked=False), carry)
        dk = dk * scale
        dkn_ref[...] = dk[:, :LANES].astype(MXU_DT)
        dv_ref[...] = dv.astype(MXU_DT)

        dkr_ref[...] = dk[:, LANES:]

    full = pl.BlockSpec((S, LANES), lambda h, j: (0, h))
    kb = pl.BlockSpec((t, LANES), lambda h, j: (j, h))
    k0 = pl.BlockSpec((t, LANES), lambda h, j: (j, 0))
    return _call(
        body, comm, name=name, grid=(H, nb),
        in_specs=[full, full, kb, k0, kb, full, full, full],
        out_specs=[kb, kb, kb],
        out_shape=[jax.ShapeDtypeStruct((S, W), MXU_DT), jax.ShapeDtypeStruct((S, W), f32), jax.ShapeDtypeStruct((S, W), MXU_DT)],
        scratch_shapes=[], sem=("arbitrary", "arbitrary"), args=(qn, qr, kn, kr, v, o, lse, do))


def _loss_head(y, target, *, name):
    S, D = y.shape
    tm = _tile(S, ROW_TILE, 8)

    def body(y_ref, t_ref, loss_ref, dy_ref):
        @pl.when(pl.program_id(0) == 0)
        def _():
            loss_ref[...] = jnp.zeros_like(loss_ref)

        e = y_ref[...] - t_ref[...]
        dy_ref[...] = e / D
        part = 0.5 * jnp.sum(jnp.mean(e * e, axis=1, keepdims=True), axis=0, keepdims=True)
        loss_ref[...] += jnp.broadcast_to(part, loss_ref.shape)

    rb = pl.BlockSpec((tm, D), lambda r: (r, 0))
    return pl.pallas_call(
        body, name=name, grid=(S // tm,), in_specs=[rb, rb],
        out_specs=[pl.BlockSpec((1, LANES), lambda r: (0, 0)), rb],
        out_shape=[jax.ShapeDtypeStruct((1, LANES), f32), jax.ShapeDtypeStruct((S, D), f32)],
        compiler_params=_cp(("arbitrary",)))(y, target)


def _adamw(w, g, m, v, *, name):
    R, C = w.shape
    tm = _tile(R, max(8, (1 << 19) // max(C, 1) // 8 * 8), 8)

    def body(w_ref, g_ref, m_ref, v_ref, d_ref, nm_ref, nv_ref):
        gv = g_ref[...]
        nm = ADAM_B1 * m_ref[...] + (1.0 - ADAM_B1) * gv
        nv = ADAM_B2 * v_ref[...] + (1.0 - ADAM_B2) * (gv * gv)
        m_hat = nm / (1.0 - ADAM_B1 ** ADAM_STEP)
        v_hat = nv / (1.0 - ADAM_B2 ** ADAM_STEP)
        d_ref[...] = -ADAM_LR * (m_hat / (jnp.sqrt(v_hat) + ADAM_EPS) + ADAM_WD * w_ref[...])
        nm_ref[...] = nm
        nv_ref[...] = nv

    rb = pl.BlockSpec((tm, C), lambda r: (r, 0))
    sh = jax.ShapeDtypeStruct((R, C), f32)
    return pl.pallas_call(body, name=name, grid=(R // tm,), in_specs=[rb] * 4, out_specs=[rb] * 3, out_shape=[sh] * 3,
                          compiler_params=_cp(("parallel",)))(w, g, m, v)


def _me():
    return lax.axis_index("x"), lax.axis_index("y"), lax.axis_index("c")


def _other_chips(mx, my):
    return [(1 - mx, my), (mx, 1 - my), (1 - mx, 1 - my)]


_ANY = pl.BlockSpec(memory_space=pl.ANY)


class _GatherChips:
    def __init__(self, xs):
        self.arrays = list(xs)
        n = len(xs)
        for x in xs:
            assert x.shape[0] % 2 == 0
        self.halves = [x.shape[0] // 2 for x in xs]
        self.out_shapes = [jax.ShapeDtypeStruct((4,) + x.shape, x.dtype) for x in xs]
        self.scratch = [pltpu.SemaphoreType.DMA((n, 6)), pltpu.SemaphoreType.DMA((n, 6))]

    def _sends(self, x_refs, o_refs, send, recv):
        mx, my, mc = _me()
        me = 2 * mx + my
        out = []
        for t, hf in enumerate(self.halves):
            mine = pl.ds(mc * hf, hf)
            for j, (cx, cy) in enumerate(_other_chips(mx, my)):
                out.append(pltpu.make_async_remote_copy(x_refs[t].at[mine], o_refs[t].at[me, mine], send.at[t, j], recv.at[t, j],
                                                        device_id=(cx, cy, mc), device_id_type=MESH))
        return out

    def start(self, x_refs, o_refs, scr):
        for cp in self._sends(x_refs, o_refs, *scr):
            cp.start()

    def finish(self, x_refs, o_refs, scr):
        send, recv = scr
        mx, my, mc = _me()
        chips = _other_chips(mx, my)
        fwd = []
        for t, hf in enumerate(self.halves):
            mine = pl.ds(mc * hf, hf)
            for j, (cx, cy) in enumerate(chips):
                k = 2 * cx + cy
                pltpu.make_async_remote_copy(x_refs[t].at[mine], o_refs[t].at[k, mine], send.at[t, j], recv.at[t, j],
                                             device_id=(cx, cy, mc), device_id_type=MESH).wait_recv()
                cp = pltpu.make_async_remote_copy(o_refs[t].at[k, mine], o_refs[t].at[k, mine], send.at[t, 3 + j], recv.at[t, 3 + j],
                                                  device_id=(mx, my, 1 - mc), device_id_type=MESH)
                cp.start()
                fwd.append(cp)
        for t, hf in enumerate(self.halves):
            theirs = pl.ds((1 - mc) * hf, hf)
            for j, (cx, cy) in enumerate(chips):
                k = 2 * cx + cy
                pltpu.make_async_remote_copy(o_refs[t].at[k, theirs], o_refs[t].at[k, theirs], send.at[t, 3 + j], recv.at[t, 3 + j],
                                             device_id=(mx, my, 1 - mc), device_id_type=MESH).wait_recv()
        for cp in self._sends(x_refs, o_refs, send, recv) + fwd:
            cp.wait_send()


class _ScatterChips:
    def __init__(self, ps):
        self.arrays = list(ps)
        n = len(ps)
        self.out_shapes = [jax.ShapeDtypeStruct((3,) + p.shape[1:], p.dtype) for p in ps]
        self.scratch = [pltpu.SemaphoreType.DMA((n, 3)), pltpu.SemaphoreType.DMA((n, 3))]

    def _copies(self, p_refs, o_refs, send, recv):
        mx, my, mc = _me()
        return [pltpu.make_async_remote_copy(p_refs[t].at[2 * cx + cy], o_refs[t].at[j], send.at[t, j], recv.at[t, j],
                                             device_id=(cx, cy, mc), device_id_type=MESH)
                for t in range(len(self.arrays)) for j, (cx, cy) in enumerate(_other_chips(mx, my))]

    def start(self, p_refs, o_refs, scr):
        for cp in self._copies(p_refs, o_refs, *scr):
            cp.start()

    def finish(self, p_refs, o_refs, scr):
        for cp in self._copies(p_refs, o_refs, *scr):
            cp.wait()


def _run_comm(comm, *, name):
    n = len(comm.arrays)

    def body(*refs):
        ins, outs, scr = refs[:n], refs[n:2 * n], refs[2 * n:]
        comm.start(ins, outs, scr)
        comm.finish(ins, outs, scr)

    return pl.pallas_call(body, name=name, in_specs=[_ANY] * n, out_specs=[_ANY] * n, out_shape=comm.out_shapes,
                          scratch_shapes=comm.scratch, compiler_params=pltpu.CompilerParams(has_side_effects=True))(*comm.arrays)


def _call(body, comm, *, name, grid, in_specs, out_specs, out_shape, scratch_shapes, sem, args):
    if comm is None:
        res = pl.pallas_call(body, name=name, grid=grid, in_specs=in_specs, out_specs=out_specs, out_shape=out_shape,
                             scratch_shapes=scratch_shapes, compiler_params=_cp(sem))(*args)
        return list(res), None
    n_in, n_out, n_scr, nc = len(in_specs), len(out_specs), len(scratch_shapes), len(comm.arrays)

    def wrapped(*refs):
        ins, cins = refs[:n_in], refs[n_in:n_in + nc]
        outs, couts = refs[n_in + nc:n_in + nc + n_out], refs[n_in + nc + n_out:n_in + 2 * nc + n_out]
        scr, cscr = refs[n_in + 2 * nc + n_out:n_in + 2 * nc + n_out + n_scr], refs[n_in + 2 * nc + n_out + n_scr:]
        ids = [pl.program_id(d) for d in range(len(grid))]
        first = functools.reduce(jnp.logical_and, [i == 0 for i in ids])
        last = functools.reduce(jnp.logical_and, [i == g - 1 for i, g in zip(ids, grid)])

        @pl.when(first)
        def _():
            comm.start(cins, couts, cscr)

        body(*ins, *outs, *scr)

        @pl.when(last)
        def _():
            comm.finish(cins, couts, cscr)

    res = pl.pallas_call(
        wrapped, name=name, grid=grid, in_specs=list(in_specs) + [_ANY] * nc, out_specs=list(out_specs) + [_ANY] * nc,
        out_shape=list(out_shape) + comm.out_shapes, scratch_shapes=list(scratch_shapes) + comm.scratch,
        compiler_params=_cp(("arbitrary",) * len(grid)))(*args, *comm.arrays)
    return list(res[:n_out]), list(res[n_out:])


class _PairSend:
    def __init__(self, gs):
        self.arrays = list(gs)
        n = len(gs)
        self.halves = [g.shape[1] // 2 for g in gs]
        self.out_shapes = [jax.ShapeDtypeStruct((4, g.shape[1] // 2, g.shape[2]), g.dtype) for g in gs]
        self.scratch = [pltpu.SemaphoreType.DMA((n, 4)), pltpu.SemaphoreType.DMA((n, 4))]

    def _copies(self, g_refs, o_refs, send, recv):
        mx, my, mc = _me()
        return [pltpu.make_async_remote_copy(g_refs[t].at[k, pl.ds((1 - mc) * hf, hf)], o_refs[t].at[k], send.at[t, k], recv.at[t, k],
                                             device_id=(mx, my, 1 - mc), device_id_type=MESH)
                for t, hf in enumerate(self.halves) for k in range(4)]

    def start(self, g_refs, o_refs, scr):
        for cp in self._copies(g_refs, o_refs, *scr):
            cp.start()

    def finish(self, g_refs, o_refs, scr):
        for cp in self._copies(g_refs, o_refs, *scr):
            cp.wait()


def _pair_exchange_halves(fs, *, name):
    n = len(fs)

    def body(*refs):
        f_refs, o_refs = refs[:n], refs[n:2 * n]
        send, recv = refs[2 * n:]
        mx, my, mc = _me()
        cps = []
        for t in range(n):
            hf = f_refs[t].shape[0]
            mine = pl.ds(mc * hf, hf)
            cp = pltpu.make_async_remote_copy(f_refs[t], o_refs[t].at[mine], send.at[t], recv.at[t],
                                              device_id=(mx, my, 1 - mc), device_id_type=MESH)
            cp.start()
            cps.append(cp)
        for t in range(n):
            hf = f_refs[t].shape[0]
            theirs = pl.ds((1 - mc) * hf, hf)
            cps[t].wait_send()
            pltpu.make_async_remote_copy(f_refs[t], o_refs[t].at[theirs], send.at[t], recv.at[t],
                                         device_id=(mx, my, 1 - mc), device_id_type=MESH).wait_recv()

    return pl.pallas_call(
        body, name=name, in_specs=[_ANY] * n, out_specs=[_ANY] * n,
        out_shape=[jax.ShapeDtypeStruct((2 * f.shape[0], f.shape[1]), f.dtype) for f in fs],
        scratch_shapes=[pltpu.SemaphoreType.DMA((n,)), pltpu.SemaphoreType.DMA((n,))],
        compiler_params=pltpu.CompilerParams(has_side_effects=True))(*fs)


def _allgather_all(x, *, name):
    def body(x_ref, o_ref, send, recv, lsem):
        mx, my, mc = _me()
        me = 4 * mx + 2 * my + mc
        local = pltpu.make_async_copy(x_ref, o_ref.at[me], lsem)
        local.start()
        cps = []
        for j in range(1, 8):
            px, py, pc = mx ^ (j >> 2), my ^ ((j >> 1) & 1), mc ^ (j & 1)
            cp = pltpu.make_async_remote_copy(x_ref, o_ref.at[me], send.at[j - 1], recv.at[j - 1],
                                              device_id=(px, py, pc), device_id_type=MESH)
            cp.start()
            cps.append(cp)
        for j in range(1, 8):
            px, py, pc = mx ^ (j >> 2), my ^ ((j >> 1) & 1), mc ^ (j & 1)
            pltpu.make_async_remote_copy(x_ref, o_ref.at[4 * px + 2 * py + pc], send.at[j - 1], recv.at[j - 1],
                                         device_id=(px, py, pc), device_id_type=MESH).wait_recv()
        for cp in cps:
            cp.wait_send()
        local.wait()

    return pl.pallas_call(
        body, name=name, in_specs=[_ANY], out_specs=_ANY, out_shape=jax.ShapeDtypeStruct((8,) + x.shape, x.dtype),
        scratch_shapes=[pltpu.SemaphoreType.DMA((7,)), pltpu.SemaphoreType.DMA((7,)), pltpu.SemaphoreType.DMA],
        compiler_params=pltpu.CompilerParams(has_side_effects=True))(x)


def _add_half(g4, recv, mc, *, name):
    _, R, C = g4.shape
    hf = R // 2
    tm = _tile(hf, max(16, (1 << 19) // C // 16 * 16), 16)
    nb = hf // tm

    def body(mc_ref, g_ref, r_ref, o_ref, ob_ref):
        s = g_ref[...] + r_ref[...]
        o_ref[...] = s
        ob_ref[...] = s.astype(COMM_DT)

    ospec = pl.BlockSpec((1, tm, C), lambda k, i, mc_ref: (k, i, 0))
    return pl.pallas_call(
        body, name=name,
        grid_spec=pltpu.PrefetchScalarGridSpec(
            num_scalar_prefetch=1, grid=(4, nb),
            in_specs=[pl.BlockSpec((1, tm, C), lambda k, i, mc_ref: (k, mc_ref[0] * nb + i, 0)),
                      pl.BlockSpec((1, tm, C), lambda k, i, mc_ref: (k, i, 0))],
            out_specs=[ospec, ospec]),
        out_shape=[jax.ShapeDtypeStruct((4, hf, C), f32), jax.ShapeDtypeStruct((4, hf, C), COMM_DT)],
        compiler_params=_cp(("parallel", "parallel")))(mc, g4, recv)


def _sum_chips(p4, recv3, me, *, name):
    _, Rh, C = p4.shape
    tm = _tile(Rh, max(16, (1 << 19) // C // 16 * 16), 16)

    def body(me_ref, p_ref, r_ref, o_ref):
        o_ref[...] = ((p_ref[0] + r_ref[0].astype(f32)) + r_ref[1].astype(f32)) + r_ref[2].astype(f32)

    return pl.pallas_call(
        body, name=name,
        grid_spec=pltpu.PrefetchScalarGridSpec(
            num_scalar_prefetch=1, grid=(Rh // tm,),
            in_specs=[pl.BlockSpec((1, tm, C), lambda i, me_ref: (me_ref[0], i, 0)),
                      pl.BlockSpec((3, tm, C), lambda i, me_ref: (0, i, 0))],
            out_specs=pl.BlockSpec((tm, C), lambda i, me_ref: (i, 0))),
        out_shape=jax.ShapeDtypeStruct((Rh, C), f32),
        compiler_params=_cp(("parallel",)))(me, p4, recv3)


def _sum8(x8, *, name):
    _, R, C = x8.shape
    tm = _tile(R, 64, 8)

    def body(x_ref, o_ref):
        acc = x_ref[0]
        for k in range(1, 8):
            acc = acc + x_ref[k]
        o_ref[...] = acc

    return pl.pallas_call(body, name=name, grid=(R // tm,), in_specs=[pl.BlockSpec((8, tm, C), lambda i: (0, i, 0))],
                          out_specs=pl.BlockSpec((tm, C), lambda i: (i, 0)), out_shape=jax.ShapeDtypeStruct((R, C), f32),
                          compiler_params=_cp(("parallel",)))(x8)


def _ssd_layer_fwd(h, W, tag, plan, i):
    z = _mm(h, W["wz"], name=tag + "_z")
    early = plan.fwd_early_comm(i)
    if early is None:
        xp = _mm(h, W["wxbc"], name=tag + "_xbc")
    else:
        xp, eres = _mm(h, W["wxbc"], name=tag + "_xbc", comm=early)
        plan.fwd_early_done(i, eres)
    dtr = _mm(h, W["wdt"], name=tag + "_dt")
    xbc = _conv_fwd(xp, W["conv_w"], W["conv_b"], name=tag + "_conv")
    (y, states), cres = _ssd_scan_fwd(xbc, dtr, W["dt_bias"], W["a_log"], W["d"], name=tag + "_scan", comm=plan.fwd_comm(i))
    DI = y.shape[1]
    G = SSD_N_GROUPS
    gs = DI // G
    (yn,) = _rowwise(_ssd_gate_fn, [(W["norm_w"], "c", gs)], [(y, "c", gs), (z, "c", gs)], [(DI, MXU_DT, "c", gs)],
                     name=tag + "_gate", ncol=G, tm=512)
    out = _mm(yn, W["wout"], name=tag + "_out")
    return out, dict(h=h, z=z, xp=xp, dtr=dtr, xbc=xbc, states=states, y=y, yn=yn), cres


def _carried_rowwise_bwd(plan, i, *a, **kw):
    early = plan.bwd_early_comm(i)
    if early is None:
        return _rowwise_bwd(*a, **kw)
    dp, di, cres = _rowwise_bwd(*a, comm=early, **kw)
    plan.bwd_early_done(i, cres)
    return dp, di


def _carried_mm(comm, done, i, *a, **kw):
    if comm is None:
        return _mm(*a, **kw)
    out, cres = _mm(*a, comm=comm, **kw)
    done(i, cres)
    return out


def _ssd_layer_bwd(sv, W, dr, drb, tag, plan, i):
    h = sv["h"]
    DI = sv["y"].shape[1]
    G = SSD_N_GROUPS
    gs = DI // G
    gr = {}
    dyn = _mm(drb, W["wout"], tb=True, name=tag + "_dyn")
    gr["wout"] = _mm(sv["yn"], drb, ta=True, name=tag + "_dwout")
    plan.early_grad(i, "ssd_out_w", gr["wout"])
    (dnw,), (dy, dz) = _carried_rowwise_bwd(plan, i, _ssd_gate_fn, [(W["norm_w"], "c", gs)], [(sv["y"], "c", gs), (sv["z"], "c", gs)],
                                             [(dyn, "c", gs)], name=tag + "_dgate", ncol=G, tm=512, din_dtypes=[(f32,), (MXU_DT,)])
    gr["norm_w"] = dnw
    (dxbc, ddtr, gr["dt_bias"], gr["a_log"], gr["d"]), cres = _ssd_scan_bwd(
        sv["xbc"], sv["dtr"], W["dt_bias"], W["a_log"], W["d"], sv["states"], dy, name=tag + "_dscan", comm=plan.bwd_comm(i))
    plan.bwd_done(i, cres)
    dxp, gr["conv_w"], gr["conv_b"] = _conv_bwd(
        sv["xp"], W["conv_w"], W["conv_b"], [dxbc], name=tag + "_dconv", dx_dtype=MXU_DT,
        dy_block=lambda c, cw: _ssd_grouped_block(c, G, gs, SSD_D_STATE, cw))
    gr["wz"] = _mm(h, dz, ta=True, name=tag + "_dwz")
    gr["wxbc"] = _mm(h, dxp, ta=True, name=tag + "_dwxbc")
    gr["wdt"] = _mm(h, ddtr, ta=True, name=tag + "_dwdt")
    dh = _carried_mm(plan.tail_early_comm(i, gr), plan.tail_early_done, i,
                     dz, W["wz"], tb=True, add=dr, add_scale=_alpha(), name=tag + "_dh1")
    dh = _carried_mm(plan.tail_comm(i), plan.tail_done, i, dxp, W["wxbc"], tb=True, add=dh, name=tag + "_dh2")
    dh = _mm(ddtr, W["wdt"], tb=True, add=dh, name=tag + "_dh3")
    return dh, gr


def _mla_layer_fwd(h, W, cos, sin, tag, comm=None):
    QR, KR = W["wqc"].shape[1], W["wkvc"].shape[1]
    HW = W["wqn"].shape[1]
    H = HW // LANES
    qc = _mm(h, W["wqc"], name=tag + "_qc")
    kvc = _mm(h, W["wkvc"], name=tag + "_kvc")
    krp = _mm(h, W["wkr"], name=tag + "_krp")
    z = _mm(h, W["wz"], name=tag + "_z")
    (qcn,) = _rowwise(_rms_fn, [(W["q_norm"], "a", QR)], [(qc, "a", QR)], [(QR, MXU_DT, "a", QR)], name=tag + "_qnorm")
    (kvn,) = _rowwise(_rms_fn, [(W["kv_norm"], "a", KR)], [(kvc, "a", KR)], [(KR, MXU_DT, "a", KR)], name=tag + "_kvnorm")
    qn = _mm(qcn, W["wqn"], name=tag + "_qn", out_dtype=MXU_DT)
    qrp = _mm(qcn, W["wqr"], name=tag + "_qrp")
    kn = _mm(kvn, W["wkn"], name=tag + "_kn", out_dtype=MXU_DT)
    v = _mm(kvn, W["wv"], name=tag + "_v", out_dtype=MXU_DT)
    (qr,) = _rowwise(_rope_fn, [], [(cos, "a", LANES), (sin, "a", LANES), (qrp, "c", LANES)], [(HW, MXU_DT, "c", LANES)],
                     name=tag + "_qrope", ncol=H, tm=1024)
    (kr,) = _rowwise(_rope_fn, [], [(cos, "a", LANES), (sin, "a", LANES), (krp, "a", LANES)], [(LANES, MXU_DT, "a", LANES)],
                     name=tag + "_krope")
    (o, lse), cres = _attn_fwd(qn, qr, kn, kr, v, name=tag + "_attn", comm=comm)
    (og,) = _rowwise(_mul_silu_fn, [], [(o, "a", HW), (z, "a", HW)], [(HW, MXU_DT, "a", HW)], name=tag + "_ogate")
    out = _mm(og, W["wout"], name=tag + "_out")
    return out, dict(h=h, qc=qc, kvc=kvc, z=z, qcn=qcn, kvn=kvn, qn=qn, qr=qr, kn=kn, kr=kr, v=v, o=o, lse=lse, og=og), cres


def _mla_layer_bwd(sv, W, cos, sin, dr, drb, tag, plan, i):
    h = sv["h"]
    QR, KR = W["wqc"].shape[1], W["wkvc"].shape[1]
    HW = W["wqn"].shape[1]
    H = HW // LANES
    gr = {}
    dog = _mm(drb, W["wout"], tb=True, name=tag + "_dog")
    gr["wout"] = _mm(sv["og"], drb, ta=True, name=tag + "_dwout")
    _, (do, dz) = _rowwise_bwd(_mul_silu_fn, [], [(sv["o"], "a", HW), (sv["z"], "a", HW)], [(dog, "a", HW)], name=tag + "_dogate",
                               din_dtypes=[(f32,), (MXU_DT,)])
    att = (sv["qn"], sv["qr"], sv["kn"], sv["kr"], sv["v"], sv["o"], sv["lse"], do)
    (dqn, dqr), cres = _attn_bwd_dq(*att, name=tag + "_dq", comm=plan.bwd_early_comm(i))
    plan.bwd_early_done(i, cres)
    (dkn, dkr_heads, dv), cres = _attn_bwd_dkv(*att, name=tag + "_dkv", comm=plan.bwd_comm(i))
    plan.bwd_done(i, cres)

    def head_sum(xh):
        tot = xh[:, :LANES]
        for hh in range(1, H):
            tot = tot + xh[:, hh * LANES:(hh + 1) * LANES]
        return (tot,)

    (dkr,) = _rowwise(head_sum, [], [(dkr_heads, "a", HW)], [(LANES, f32, "a", LANES)], name=tag + "_dkrsum")
    _, (dqrp,) = _rowwise_bwd(_rope_fn, [], [(cos, "a", LANES), (sin, "a", LANES), (dqr, "c", LANES)], [(dqr, "c", LANES)],
                              name=tag + "_dqrope", ncol=H, tm=1024, diff_i=[2], din_dtypes=[(MXU_DT,)])
    _, (dkrp,) = _rowwise_bwd(_rope_fn, [], [(cos, "a", LANES), (sin, "a", LANES), (dkr, "a", LANES)], [(dkr, "a", LANES)],
                              name=tag + "_dkrope", diff_i=[2], din_dtypes=[(MXU_DT,)])
    dqcn = _mm(dqn, W["wqn"], tb=True, name=tag + "_dqcn1")
    dqcn = _mm(dqrp, W["wqr"], tb=True, add=dqcn, name=tag + "_dqcn2")
    dkvn = _mm(dkn, W["wkn"], tb=True, name=tag + "_dkvn1")
    dkvn = _mm(dv, W["wv"], tb=True, add=dkvn, name=tag + "_dkvn2")
    gr["wqn"] = _mm(sv["qcn"], dqn, ta=True, name=tag + "_dwqn")
    gr["wqr"] = _mm(sv["qcn"], dqrp, ta=True, name=tag + "_dwqr")
    gr["wkn"] = _mm(sv["kvn"], dkn, ta=True, name=tag + "_dwkn")
    gr["wv"] = _mm(sv["kvn"], dv, ta=True, name=tag + "_dwv")
    (gr["q_norm"],), (dqc,) = _rowwise_bwd(_rms_fn, [(W["q_norm"], "a", QR)], [(sv["qc"], "a", QR)], [(dqcn, "a", QR)], name=tag + "_dqnorm",
                                           din_dtypes=[(MXU_DT,)])
    (gr["kv_norm"],), (dkvc,) = _rowwise_bwd(_rms_fn, [(W["kv_norm"], "a", KR)], [(sv["kvc"], "a", KR)], [(dkvn, "a", KR)], name=tag + "_dkvnorm",
                                             din_dtypes=[(MXU_DT,)])
    dh = _mm(dz, W["wz"], tb=True, add=dr, add_scale=_alpha(), name=tag + "_dh1")
    dh = _mm(dqc, W["wqc"], tb=True, add=dh, name=tag + "_dh2")
    dh = _mm(dkvc, W["wkvc"], tb=True, add=dh, name=tag + "_dh3")
    dh = _mm(dkrp, W["wkr"], tb=True, add=dh, name=tag + "_dh4")
    gr["wz"] = _mm(h, dz, ta=True, name=tag + "_dwz")
    gr["wqc"] = _mm(h, dqc, ta=True, name=tag + "_dwqc")
    gr["wkvc"] = _mm(h, dkvc, ta=True, name=tag + "_dwkvc")
    gr["wkr"] = _mm(h, dkrp, ta=True, name=tag + "_dwkr")
    return dh, gr


def _gdn_layer_fwd(h, W, tag, comm=None):
    HK, HV, DK, DV = GDN_N_QK_HEADS, GDN_N_V_HEADS, GDN_DK, GDN_DV
    KD, VD = HK * DK, HV * DV
    qkvp = _mm(h, W["wqkv"], name=tag + "_qkv")
    z = _mm(h, W["wz"], name=tag + "_z")
    br = _mm(h, W["wb"], name=tag + "_b")
    ar = _mm(h, W["wa"], name=tag + "_a")
    qkv = _conv_fwd(qkvp, W["conv_w"], jnp.zeros((1, qkvp.shape[1]), f32), name=tag + "_conv")
    scale = jnp.concatenate([jnp.full((1, KD), DK ** -0.5, f32), jnp.ones((1, KD), f32)], axis=1)
    (qkn,) = _rowwise(_l2_fn, [(scale, "c", DK)], [(qkv, "c", DK)], [(2 * KD, f32, "c", DK)], name=tag + "_l2", ncol=2 * HK, tm=2048)
    (o, states, inverses), cres = _gdn_scan_fwd(qkn, qkv, br, ar, W["a_log"], W["dt_bias"], name=tag + "_scan", comm=comm)
    (on,) = _rowwise(_gdn_gate_fn, [(W["norm_w"], "a", DV)], [(o, "c", DV), (z, "c", DV)], [(VD, MXU_DT, "c", DV)],
                     name=tag + "_gate", ncol=HV, tm=1024)
    out = _mm(on, W["wout"], name=tag + "_out")
    return out, dict(h=h, qkvp=qkvp, z=z, br=br, ar=ar, qkv=qkv, qkn=qkn, o=o, states=states, inverses=inverses, on=on, scale=scale), cres


def _gdn_layer_bwd(sv, W, dr, drb, tag, plan, i):
    h = sv["h"]
    HK, HV, DK, DV = GDN_N_QK_HEADS, GDN_N_V_HEADS, GDN_DK, GDN_DV
    KD, VD = HK * DK, HV * DV
    gr = {}
    don = _mm(drb, W["wout"], tb=True, name=tag + "_don")
    gr["wout"] = _mm(sv["on"], drb, ta=True, name=tag + "_dwout")
    (gr["norm_w"],), (do, dz) = _carried_rowwise_bwd(plan, i, _gdn_gate_fn, [(W["norm_w"], "a", DV)], [(sv["o"], "c", DV), (sv["z"], "c", DV)],
                                                      [(don, "c", DV)], name=tag + "_dgate", ncol=HV, tm=1024, din_dtypes=[(f32,), (MXU_DT,)])
    (dq, dk, dv, dbr, dar, gr["a_log"], gr["dt_bias"]), cres = _gdn_scan_bwd(
        sv["qkn"], sv["qkv"], sv["br"], sv["ar"], W["a_log"], W["dt_bias"], sv["states"], sv["inverses"], do, name=tag + "_dscan",
        comm=plan.bwd_comm(i))
    plan.bwd_done(i, cres)
    _, (dqq,) = _rowwise_bwd(_l2_fn, [(sv["scale"], "c", DK)], [(sv["qkv"], "c", DK)], [(dq, "c", DK)],
                             name=tag + "_dl2q", ncol=HK, tm=2048, diff_p=[])
    _, (dqk,) = _rowwise_bwd(_l2_fn, [(sv["scale"], "c", DK, HK)], [(sv["qkv"], "c", DK, HK)], [(dk, "c", DK)],
                             name=tag + "_dl2k", ncol=HK, tm=2048, diff_p=[])
    dqkvp, gr["conv_w"], _ = _conv_bwd(sv["qkvp"], W["conv_w"], jnp.zeros((1, sv["qkvp"].shape[1]), f32), [dqq, dqk, dv],
                                       name=tag + "_dconv", dx_dtype=MXU_DT)
    dh = _mm(dz, W["wz"], tb=True, add=dr, add_scale=_alpha(), name=tag + "_dh1")
    dh = _mm(dqkvp, W["wqkv"], tb=True, add=dh, name=tag + "_dh2")
    dh = _mm(dbr, W["wb"], tb=True, add=dh, name=tag + "_dh3")
    dh = _mm(dar, W["wa"], tb=True, add=dh, name=tag + "_dh4")
    gr["wz"] = _mm(h, dz, ta=True, name=tag + "_dwz")
    gr["wqkv"] = _mm(h, dqkvp, ta=True, name=tag + "_dwqkv")
    gr["wb"] = _mm(h, dbr, ta=True, name=tag + "_dwb")
    gr["wa"] = _mm(h, dar, ta=True, name=tag + "_dwa")
    return dh, gr


def _rope_tables(positions):
    half = MLA_ROPE // 2
    inv_freq = ROPE_THETA ** (-jnp.arange(0, MLA_ROPE, 2, dtype=f32) / MLA_ROPE)
    ang = positions.astype(f32)[:, None] * inv_freq
    cos, sin = jnp.cos(ang), jnp.sin(ang)
    S = positions.shape[0]
    pad = jnp.zeros((S, LANES - 2 * half), f32)
    return jnp.concatenate([cos, cos, pad + 1.0], axis=1), jnp.concatenate([sin, sin, pad], axis=1)


class _LocalPlan:
    def __init__(self, LW):
        self.LW, self.grads = LW, [None] * DEPTH

    def weights(self, i):
        return self.LW[i]

    def fwd_early_comm(self, i):
        return None

    def fwd_early_done(self, i, res):
        pass

    def fwd_comm(self, i):
        return None

    def fwd_done(self, i, res):
        pass

    def early_grad(self, i, name, g):
        pass

    def tail_early_comm(self, i, gr):
        return None

    def tail_early_done(self, i, res):
        pass

    def tail_comm(self, i):
        return None

    def tail_done(self, i, res):
        pass

    def bwd_early_comm(self, i):
        return None

    def bwd_early_done(self, i, res):
        pass

    def bwd_comm(self, i):
        return None

    def bwd_done(self, i, res):
        pass

    def layer_grads(self, i, gr):
        self.grads[i] = gr


def _local_step(x, positions, target, ln_g, ln_b, plan):
    cos, sin = _rope_tables(positions)
    h, hb = x, x.astype(MXU_DT)
    saved, LW = [], []
    for i in range(DEPTH):
        kind, tag = i % 3, "l%d" % i
        LW.append(plan.weights(i))
        if kind == 0:
            y, sv, cres = _ssd_layer_fwd(hb, LW[i], tag, plan, i)
        elif kind == 1:
            y, sv, cres = _mla_layer_fwd(hb, LW[i], cos, sin, tag, plan.fwd_comm(i))
        else:
            y, sv, cres = _gdn_layer_fwd(hb, LW[i], tag, plan.fwd_comm(i))
        plan.fwd_done(i, cres)
        D = h.shape[1]
        r, h, hb = _rowwise(_res_ln_fn, [(ln_g[i], "a", D), (ln_b[i], "a", D)], [(h, "a", D), (y, "a", D)],
                            [(D, f32, "a", D), (D, f32, "a", D), (D, MXU_DT, "a", D)], name=tag + "_ln")
        sv["r"] = r
        saved.append(sv)
    loss, dh = _loss_head(h, target, name="loss_head")
    dg, db = [None] * DEPTH, [None] * DEPTH
    for i in reversed(range(DEPTH)):
        kind, tag = i % 3, "l%d" % i
        sv = saved[i]
        D = dh.shape[1]
        (dg[i], db[i]), (dr, drb) = _rowwise_bwd(_ln_fn, [(ln_g[i], "a", D), (ln_b[i], "a", D)], [(sv["r"], "a", D)], [(dh, "a", D)],
                                                 name=tag + "_dln", din_dtypes=[(f32, MXU_DT)])
        if kind == 0:
            dh, gr = _ssd_layer_bwd(sv, LW[i], dr, drb, tag, plan, i)
        elif kind == 1:
            dh, gr = _mla_layer_bwd(sv, LW[i], cos, sin, dr, drb, tag, plan, i)
        else:
            dh, gr = _gdn_layer_bwd(sv, LW[i], dr, drb, tag, plan, i)
        plan.layer_grads(i, gr)
    return loss, dh, dg, db


_WEIGHTS = ["ssd_in_w", "ssd_conv_w", "ssd_conv_b", "ssd_dt_bias", "ssd_a_log", "ssd_d", "ssd_norm_w", "ssd_out_w",
            "mla_in_w", "mla_q_norm_w", "mla_q_up_w", "mla_kv_norm_w", "mla_kv_up_w", "mla_out_w",
            "gdn_in_w", "gdn_conv_w", "gdn_a_log", "gdn_dt_bias", "gdn_norm_w", "gdn_out_w", "ln_g", "ln_b"]
_BIG = {"ssd_in_w": "col", "ssd_out_w": "row", "mla_in_w": "col", "mla_q_up_w": "col", "mla_kv_up_w": "col",
        "mla_out_w": "row", "gdn_in_w": "col", "gdn_out_w": "row"}
_SMALL_SHARDED = ["ssd_conv_w", "ssd_conv_b", "ssd_norm_w", "gdn_conv_w"]
_PACK_ROWS = 16


def _gathered_to_full(g, kind, nl):
    if kind == "col":
        _, RK, Ns = g.shape
        return g.reshape(4, nl, RK // nl, Ns).transpose(1, 2, 0, 3).reshape(nl, RK // nl, 4 * Ns)
    _, RK, N = g.shape
    return g.reshape(4, nl, RK // nl, N).transpose(1, 0, 2, 3).reshape(nl, 4 * (RK // nl), N)


def _full_to_slots(f, kind):
    nl, K, N = f.shape
    if kind == "col":
        return f.reshape(nl, K, 4, N // 4).transpose(2, 0, 1, 3).reshape(4, nl * K, N // 4)
    return f.reshape(nl, 4, K // 4, N).transpose(1, 0, 2, 3).reshape(4, nl * (K // 4), N)


def _pack(arrs):
    flat = jnp.concatenate([a.reshape(-1).astype(f32) for a in arrs])
    unit = _PACK_ROWS * LANES
    n = -(-flat.shape[0] // unit) * unit
    return jnp.pad(flat, (0, n - flat.shape[0])).reshape(_PACK_ROWS, n // _PACK_ROWS)


def _unpack(packed, shapes):
    flat = packed.reshape(-1)
    out, off = [], 0
    for sh in shapes:
        n = math.prod(sh)
        out.append(flat[off:off + n].reshape(sh))
        off += n
    return out


def _pad_lanes(a):
    return jnp.pad(a, [(0, 0)] * (a.ndim - 1) + [(0, LANES - a.shape[-1])])


_IN_PROJ = ("ssd_in_w", "mla_in_w", "gdn_in_w")


class _ColSlots:
    def __init__(self, slots):
        self.slots = slots
        self.shape = (slots.shape[1], 4 * slots.shape[2])

    def __getitem__(self, idx):
        _, cols = idx
        ns = self.slots.shape[2]
        a = cols.start or 0
        b = self.shape[1] if cols.stop is None else cols.stop
        parts = [self.slots[k][:, max(a, k * ns) - k * ns:min(b, (k + 1) * ns) - k * ns]
                 for k in range(4) if max(a, k * ns) < min(b, (k + 1) * ns)]
        return parts[0] if len(parts) == 1 else jnp.concatenate(parts, axis=1)


def _col_slots(pieces):
    widths = [p.shape[1] for p in pieces]
    ns = sum(widths) // 4
    slots = []
    for k in range(4):
        lo, hi, off, parts = k * ns, (k + 1) * ns, 0, []
        for p, wd in zip(pieces, widths):
            if max(lo, off) < min(hi, off + wd):
                parts.append(p[:, max(lo, off) - off:min(hi, off + wd) - off])
            off += wd
        slots.append(parts[0] if len(parts) == 1 else jnp.concatenate(parts, axis=1))
    return jnp.stack(slots)


def _layer_dict(i, full):
    G, N, P = SSD_N_GROUPS, SSD_D_STATE, SSD_HEAD_DIM
    kind, j = i % 3, i // 3
    if kind == 0:
        H = full["ssd_dt_bias"][j].shape[0]
        DI = H * P
        CD = DI + 2 * G * N
        win = full["ssd_in_w"][j]
        return dict(wz=win[:, :DI], wxbc=win[:, DI:DI + CD], wdt=win[:, DI + CD:], conv_w=full["ssd_conv_w"][j],
                    conv_b=full["ssd_conv_b"][j][None], dt_bias=full["ssd_dt_bias"][j][None], a_log=full["ssd_a_log"][j][None],
                    d=full["ssd_d"][j][None], norm_w=full["ssd_norm_w"][j][None], wout=full["ssd_out_w"][j])
    if kind == 1:
        QR, KR = MLA_Q_RANK, MLA_KV_RANK
        win = full["mla_in_w"][j]
        Hh = full["mla_q_up_w"][j].shape[1] // (MLA_NOPE + MLA_ROPE)
        qup = full["mla_q_up_w"][j].reshape(QR, Hh, MLA_NOPE + MLA_ROPE)
        kvup = full["mla_kv_up_w"][j].reshape(KR, Hh, MLA_NOPE + MLA_V)
        return dict(wqc=win[:, :QR], wkvc=win[:, QR:QR + KR], wkr=_pad_lanes(win[:, QR + KR:QR + KR + MLA_ROPE]),
                    wz=win[:, QR + KR + MLA_ROPE:], q_norm=full["mla_q_norm_w"][j][None], kv_norm=full["mla_kv_norm_w"][j][None],
                    wqn=qup[:, :, :MLA_NOPE].reshape(QR, Hh * MLA_NOPE), wqr=_pad_lanes(qup[:, :, MLA_NOPE:]).reshape(QR, Hh * LANES),
                    wkn=kvup[:, :, :MLA_NOPE].reshape(KR, Hh * MLA_NOPE), wv=kvup[:, :, MLA_NOPE:].reshape(KR, Hh * MLA_V),
                    wout=full["mla_out_w"][j])
    KD, VD, HV = GDN_N_QK_HEADS * GDN_DK, GDN_N_V_HEADS * GDN_DV, GDN_N_V_HEADS
    win = full["gdn_in_w"][j]
    c0, c1 = 2 * KD + VD, 2 * KD + 2 * VD
    return dict(wqkv=win[:, :c0], wz=win[:, c0:c1], wb=win[:, c1:c1 + HV], wa=win[:, c1 + HV:], conv_w=full["gdn_conv_w"][j],
                a_log=full["gdn_a_log"][j][None], dt_bias=full["gdn_dt_bias"][j][None], norm_w=full["gdn_norm_w"][j][None],
                wout=full["gdn_out_w"][j])


def _layer_weights(full, D):
    return [_layer_dict(i, full) for i in range(DEPTH)]


def _layer_full_grads(i, g, slots=False):
    kind = i % 3
    join = _col_slots if slots else (lambda pieces: jnp.concatenate(pieces, axis=1))
    if kind == 0:
        out = {"ssd_in_w": join([g["wz"], g["wxbc"], g["wdt"]]), "ssd_conv_w": g["conv_w"], "ssd_out_w": g["wout"]}
        for n in ("conv_b", "dt_bias", "a_log", "d", "norm_w"):
            out["ssd_" + n] = g[n][0]
        return out
    if kind == 1:
        QR, KR = g["wqn"].shape[0], g["wkn"].shape[0]
        Hh = g["wqn"].shape[1] // MLA_NOPE
        return {"mla_in_w": join([g["wqc"], g["wkvc"], g["wkr"][:, :MLA_ROPE], g["wz"]]),
                "mla_q_up_w": jnp.concatenate([g["wqn"].reshape(QR, Hh, MLA_NOPE), g["wqr"].reshape(QR, Hh, LANES)[:, :, :MLA_ROPE]],
                                              axis=2).reshape(QR, -1),
                "mla_kv_up_w": jnp.concatenate([g["wkn"].reshape(KR, Hh, MLA_NOPE), g["wv"].reshape(KR, Hh, MLA_V)], axis=2).reshape(KR, -1),
                "mla_q_norm_w": g["q_norm"][0], "mla_kv_norm_w": g["kv_norm"][0], "mla_out_w": g["wout"]}
    out = {"gdn_in_w": join([g["wqkv"], g["wz"], g["wb"], g["wa"]]), "gdn_conv_w": g["conv_w"], "gdn_out_w": g["wout"]}
    for n in ("a_log", "dt_bias", "norm_w"):
        out["gdn_" + n] = g[n][0]
    return out


def _full_grads(grads, dg, db):
    per = {n: [] for n in _WEIGHTS}
    for i in range(DEPTH):
        for n, a in _layer_full_grads(i, grads[i]).items():
            per[n].append(a)
        per["ln_g"].append(dg[i][0])
        per["ln_b"].append(db[i][0])
    return {n: jnp.stack(v) for n, v in per.items()}


class _DistPlan:
    def __init__(self, w, chip, core):
        self.w, self.chip = w, chip
        self.chip_arr = jnp.reshape(chip, (1,)).astype(jnp.int32)
        self.core_arr = jnp.reshape(core, (1,)).astype(jnp.int32)
        self.full = {n: {} for n in _BIG}
        self.gkeys, self.g4, self.p4, self.fin, self.small_grads = {}, {}, {}, {}, [None] * DEPTH
        keys = [("ssd_in_w", 0)]
        shards = self._shards(keys)
        got = _run_comm(_GatherChips(shards + [_pack([w[n] for n in _SMALL_SHARDED])]), name="gather_l0")
        self._fill(keys, shards, got[:1])
        small = lax.dynamic_update_slice(got[1], _pack([w[n] for n in _SMALL_SHARDED])[None], (chip, 0, 0))
        parts = [_unpack(small[k], [w[n].shape for n in _SMALL_SHARDED]) for k in range(4)]
        for t, n in enumerate(_SMALL_SHARDED):
            self.full[n] = jnp.concatenate([parts[k][t] for k in range(4)], axis=-1)
        for n in _WEIGHTS:
            if n not in self.full:
                self.full[n] = w[n]

    @staticmethod
    def keys(i):
        names = [["ssd_in_w", "ssd_out_w"], ["mla_in_w", "mla_q_up_w", "mla_kv_up_w", "mla_out_w"], ["gdn_in_w", "gdn_out_w"]][i % 3]
        return [(n, i // 3) for n in names]

    def _shards(self, keys):
        return [self.w[n][j].astype(MXU_DT) for n, j in keys]

    def _fill(self, keys, shards, got):
        for (n, j), s, g in zip(keys, shards, got):
            g = lax.dynamic_update_slice(g, s[None], (self.chip, 0, 0))
            self.full[n][j] = _ColSlots(g) if n in _IN_PROJ else _gathered_to_full(g, _BIG[n], 1)[0]

    def weights(self, i):
        if i == 0:
            self.full["ssd_out_w"][0] = None
        self._w = _layer_dict(i, self.full)
        return self._w

    def _start_gather(self, keys):
        self._pending = (keys, self._shards(keys))
        return _GatherChips(self._pending[1])

    def _end_gather(self, res):
        self._fill(self._pending[0], self._pending[1], res)

    def fwd_early_comm(self, i):
        return self._start_gather([("ssd_out_w", 0)]) if i == 0 else None

    def fwd_early_done(self, i, res):
        self._end_gather(res)
        self._w["wout"] = self.full["ssd_out_w"][0]

    def fwd_comm(self, i):
        return self._start_gather(self.keys(i + 1)) if i + 1 < DEPTH else None

    def fwd_done(self, i, res):
        if res is not None:
            self._end_gather(res)

    def _slots(self, n, g):
        return g if n in _IN_PROJ else _full_to_slots(g[None], _BIG[n])

    def early_grad(self, i, name, g):
        if i == 0:
            self.gkeys[1].append((name, 0))
            self.g4[1].append(self._slots(name, g))

    def _make_group(self, i, gr):
        fg = _layer_full_grads(i, gr, slots=True)
        self.small_grads[i] = {n: a for n, a in fg.items() if n not in _BIG}
        self.gkeys[i] = [k for k in self.keys(i) if not (i == 0 and k[0] == "ssd_out_w")]
        self.g4[i] = [self._slots(n, fg[n]) for n, _ in self.gkeys[i]]

    def layer_grads(self, i, gr):
        if i > 0:
            self._make_group(i, gr)

    def tail_early_comm(self, i, gr):
        if i > 0:
            return None
        self._make_group(0, gr)
        return _PairSend(self.g4[0])

    def tail_early_done(self, i, res):
        self._pair_add(0, res)

    def tail_comm(self, i):
        return self._scatter(0) if i == 0 else None

    def tail_done(self, i, res):
        self._sum(0, res)

    def _pair_add(self, g, r1):
        self.p4[g] = [_add_half(a, b, self.core_arr, name="grad_pair_add_g%d_%s%d" % (g, n, j))
                      for a, b, (n, j) in zip(self.g4[g], r1, self.gkeys[g])]

    def bwd_early_comm(self, i):
        return _PairSend(self.g4[i + 1]) if i + 1 < DEPTH else None

    def bwd_early_done(self, i, res):
        if res is not None:
            self._pair_add(i + 1, res)

    def _scatter(self, g):
        return _ScatterChips([p[1] for p in self.p4[g]])

    def _sum(self, g, r2):
        self.fin[g] = [_sum_chips(p[0], b, self.chip_arr, name="grad_chip_sum_g%d_%s%d" % (g, n, j))
                       for p, b, (n, j) in zip(self.p4[g], r2, self.gkeys[g])]

    def bwd_comm(self, i):
        return self._scatter(i + 1) if i + 1 < DEPTH else None

    def bwd_done(self, i, res):
        if res is not None:
            self._sum(i + 1, res)

    def grad_shards(self, core):
        order = [(g, t) for g in range(DEPTH) for t in range(len(self.gkeys[g]))]
        fins = [self.fin[g][t] for g, t in order]
        got = _pair_exchange_halves(fins, name="grad_pair_share")
        got = [lax.dynamic_update_slice(a, f, (core * f.shape[0], 0)) for a, f in zip(got, fins)]
        per = {n: {} for n in _BIG}
        for (g, t), a in zip(order, got):
            n, j = self.gkeys[g][t]
            per[n][j] = a
        return {n: (v[0] if len(v) == 1 else jnp.concatenate([v[j] for j in sorted(v)], axis=0)) for n, v in per.items()}


def kernel(x, positions, ssd_in_w, ssd_conv_w, ssd_conv_b, ssd_dt_bias, ssd_a_log, ssd_d, ssd_norm_w, ssd_out_w, mla_in_w, mla_q_norm_w, mla_q_up_w, mla_kv_norm_w, mla_kv_up_w, mla_out_w, gdn_in_w, gdn_conv_w, gdn_a_log, gdn_dt_bias, gdn_norm_w, gdn_out_w, ln_g, ln_b, loss_target, m_ssd_in_w, m_ssd_conv_w, m_ssd_conv_b, m_ssd_dt_bias, m_ssd_a_log, m_ssd_d, m_ssd_norm_w, m_ssd_out_w, m_mla_in_w, m_mla_q_norm_w, m_mla_q_up_w, m_mla_kv_norm_w, m_mla_kv_up_w, m_mla_out_w, m_gdn_in_w, m_gdn_conv_w, m_gdn_a_log, m_gdn_dt_bias, m_gdn_norm_w, m_gdn_out_w, m_ln_g, m_ln_b, v_ssd_in_w, v_ssd_conv_w, v_ssd_conv_b, v_ssd_dt_bias, v_ssd_a_log, v_ssd_d, v_ssd_norm_w, v_ssd_out_w, v_mla_in_w, v_mla_q_norm_w, v_mla_q_up_w, v_mla_kv_norm_w, v_mla_kv_up_w, v_mla_out_w, v_gdn_in_w, v_gdn_conv_w, v_gdn_a_log, v_gdn_dt_bias, v_gdn_norm_w, v_gdn_out_w, v_ln_g, v_ln_b):
    args = dict(locals())
    w = {n: args[n] for n in _WEIGHTS}
    mom = {n: args["m_" + n] for n in _WEIGHTS}
    vel = {n: args["v_" + n] for n in _WEIGHTS}
    mx, my, mc = _me()
    chip = 2 * mx + my
    small = [n for n in _WEIGHTS if n not in _BIG]
    big = list(_BIG)

    plan = _DistPlan(w, chip, mc)
    loss, gx, dg, db = _local_step(x[0], positions[0], loss_target[0], [plan.full["ln_g"][i][None] for i in range(DEPTH)],
                                   [plan.full["ln_b"][i][None] for i in range(DEPTH)], plan)
    per = {n: [] for n in small}
    for i in range(DEPTH):
        for n, a in plan.small_grads[i].items():
            per[n].append(a)
        per["ln_g"].append(dg[i][0])
        per["ln_b"].append(db[i][0])
    fg = {n: jnp.stack(v) for n, v in per.items()}
    gsh = plan.grad_shards(mc)

    out_g, out_d, out_m, out_v = {}, {}, {}, {}
    for n in big:
        g = gsh[n]
        sh = w[n].shape
        to2 = lambda a: a.reshape(-1, sh[-1])
        d_, m_, v_ = _adamw(to2(w[n]), g, to2(mom[n]), to2(vel[n]), name="adamw_" + n)
        out_g[n], out_d[n], out_m[n], out_v[n] = g.reshape(sh), d_.reshape(sh), m_.reshape(sh), v_.reshape(sh)

    summed = _sum8(_allgather_all(_pack([fg[n] for n in small] + [loss[0, :1]]), name="gather_small"), name="sum_small")
    sg = _unpack(summed, [fg[n].shape for n in small] + [(1,)])
    loss_total = sg[-1][0]
    gs = {}
    for n, g in zip(small, sg[:-1]):
        if n in _SMALL_SHARDED:
            ws = w[n].shape[-1]
            g = lax.dynamic_slice_in_dim(g, chip * ws, ws, axis=g.ndim - 1)
        gs[n] = g
    shapes = [w[n].shape for n in small]
    d_, m_, v_ = _adamw(_pack([w[n] for n in small]), _pack([gs[n] for n in small]), _pack([mom[n] for n in small]),
                        _pack([vel[n] for n in small]), name="adamw_small")
    for n, a, b, c in zip(small, _unpack(d_, shapes), _unpack(m_, shapes), _unpack(v_, shapes)):
        out_g[n], out_d[n], out_m[n], out_v[n] = gs[n], a, b, c

    return (loss_total, gx[None], *[out_g[n] for n in _WEIGHTS], *[out_d[n] for n in _WEIGHTS],
            *[out_m[n] for n in _WEIGHTS], *[out_v[n] for n in _WEIGHTS])
```
